```python
import jax, jax.numpy as jnp
from jax import lax
import numpy as np

D_MODEL = 1024
BATCH = 16
SEQ = 256
DEPTH = 1
DEC_BATCH = 2
DEC_SEQ = 2048
PAST_LEN = 256

GRID_W = 64
N_DIR = 2
RWKV_HEADS = 8
RWKV_HEAD_DIM = 64
RWKV_WIDTH = RWKV_HEADS * RWKV_HEAD_DIM
DECAY_LORA = 64
ICLR_LORA = 64
GATE_LORA = 128
MLSTM_HEADS = 4
MLSTM_HEAD_DIM = 128
MLSTM_WIDTH = MLSTM_HEADS * MLSTM_HEAD_DIM
MLSTM_CHUNK = 64
D_FF = 2816
RMS_EPS = 1e-6
RWKV_GN_EPS = 64e-5
MLSTM_GN_EPS = 1e-5
DECAY_SCALE = 0.606531

RWKV_COLS = 3 * RWKV_WIDTH + N_DIR * DECAY_LORA + N_DIR * ICLR_LORA + GATE_LORA
MLSTM_COLS = 4 * MLSTM_WIDTH + 2 * N_DIR * MLSTM_HEADS
GATE_COLS = 2 * D_MODEL
IN_COLS = RWKV_COLS + MLSTM_COLS + GATE_COLS

kernel_name = 'bidir_rwkv7_mlstm_prefix_dit_step'


def rmsnorm(x, g):
    x32 = x.astype(jnp.float32)
    y = x32 * lax.rsqrt(jnp.mean(x32 * x32, axis=-1, keepdims=True) + RMS_EPS)
    return (y * g.astype(jnp.float32)).astype(x.dtype)


def centred_shift(z, mu):
    zp = jnp.pad(z, ((0, 0), (1, 1), (0, 0)))
    return z + mu * (0.5 * (zp[:, :-2] + zp[:, 2:]) - z)


def dwconv_grid(x, w, rows):
    b, t, ch = x.shape
    img = x.reshape(b, rows, t // rows, ch)
    out = lax.conv_general_dilated(img, w[:, :, None, :].astype(x.dtype), (1, 1), 'SAME',
                                   dimension_numbers=('NHWC', 'HWIO', 'NHWC'),
                                   feature_group_count=ch)
    return out.reshape(b, t, ch)


def flip_backward(x, dir_axis, time_axis):
    fwd = jnp.take(x, 0, axis=dir_axis)
    bwd = jnp.take(x, 1, axis=dir_axis)
    t_ax = time_axis - 1 if time_axis > dir_axis else time_axis
    return jnp.stack([fwd, jnp.flip(bwd, axis=t_ax)], axis=dir_axis)


def rwkv7_bidir(z, S0, w0, w_up, a0, a_up, g_up, kk_scale, k_a, r_k, lnx_g, lnx_b):
    f32 = jnp.float32
    z = z.astype(f32)
    B, T, _ = z.shape
    H, N = RWKV_HEADS, RWKV_HEAD_DIM
    o1 = RWKV_WIDTH
    o2 = 2 * RWKV_WIDTH
    o3 = 3 * RWKV_WIDTH
    o4 = o3 + N_DIR * DECAY_LORA
    o5 = o4 + N_DIR * ICLR_LORA
    r, k, v, wd, ad, gd = jnp.split(z, [o1, o2, o3, o4, o5], axis=-1)
    wd = wd.reshape(B, T, N_DIR, DECAY_LORA)
    ad = ad.reshape(B, T, N_DIR, ICLR_LORA)
    decay = jnp.exp(-DECAY_SCALE * jax.nn.sigmoid(w0 + jnp.einsum('btdr,drc->btdc', jnp.tanh(wd), w_up)))
    a = jax.nn.sigmoid(a0 + jnp.einsum('btdr,drc->btdc', ad, a_up))
    g = jax.nn.sigmoid(gd) @ g_up
    kk = (k * kk_scale).reshape(B, T, H, N)
    kk = kk / jnp.maximum(jnp.linalg.norm(kk, axis=-1, keepdims=True), 1e-12)
    k_dir = k[:, :, None, :] * (1.0 + (a - 1.0) * k_a)

    def shared(u):
        return jnp.broadcast_to(u[:, :, None, :], (B, T, N_DIR, u.shape[-1]))

    def per_dir(u):
        return jnp.moveaxis(flip_backward(u.reshape(B, T, N_DIR, H, N), 2, 1), 1, 0)

    xs = (per_dir(shared(r)), per_dir(decay), per_dir(k_dir), per_dir(shared(v)),
          per_dir(shared(kk.reshape(B, T, RWKV_WIDTH))), per_dir(a))

    def step(S, inp):
        r_t, w_t, k_t, v_t, kk_t, a_t = inp
        removed = jnp.einsum('bdhvk,bdhk->bdhv', S, kk_t)
        S = (S * w_t[..., None, :] - removed[..., :, None] * (kk_t * a_t)[..., None, :]
             + v_t[..., :, None] * k_t[..., None, :])
        return S, jnp.einsum('bdhvk,bdhk->bdhv', S, r_t)

    S_fin, ys = lax.scan(step, S0.astype(f32), xs)
    ys = flip_backward(jnp.moveaxis(ys, 0, 1), 2, 1).sum(axis=2)
    mean = jnp.mean(ys, axis=-1, keepdims=True)
    var = jnp.var(ys, axis=-1, keepdims=True)
    y = ((ys - mean) * lax.rsqrt(var + RWKV_GN_EPS)).reshape(B, T, RWKV_WIDTH) * lnx_g + lnx_b
    bonus = (jnp.einsum('bthn,btdhn,hn->bth', r.reshape(B, T, H, N),
                        k_dir.reshape(B, T, N_DIR, H, N), r_k)[..., None]
             * v.reshape(B, T, H, N))
    return (y + bonus.reshape(B, T, RWKV_WIDTH)) * g, S_fin


def mlstm_chunkwise(q, k, v, log_i, log_f, C0, n0, m0):
    T = q.shape[-2]
    L = MLSTM_CHUNK
    nc = T // L
    lead = q.shape[:-2]
    mask = jnp.tril(jnp.ones((L, L), dtype=bool))

    def chunk(u):
        return jnp.moveaxis(u.reshape(u.shape[:-2] + (nc, L, u.shape[-1])), -3, 0)

    def chunk_g(u):
        return jnp.moveaxis(u.reshape(u.shape[:-1] + (nc, L)), -2, 0)

    def step(carry, inp):
        C, n, m = carry
        qc, kc, vc, ic, fc = inp
        b = jnp.cumsum(fc, axis=-1)
        log_d = jnp.where(mask, b[..., :, None] - b[..., None, :] + ic[..., None, :], -jnp.inf)
        log_inter = b + m[..., None]
        m_s = jnp.maximum(log_inter, jnp.max(log_d, axis=-1))
        dmat = jnp.exp(log_d - m_s[..., None])
        inter = jnp.exp(log_inter - m_s)
        s = jnp.einsum('...sk,...jk->...sj', qc, kc) * dmat
        num = (inter[..., None] * jnp.einsum('...sk,...vk->...sv', qc, C)
               + jnp.einsum('...sj,...jv->...sv', s, vc))
        den = inter * jnp.einsum('...sk,...k->...s', qc, n) + jnp.sum(s, axis=-1)
        h = num / jnp.maximum(jnp.abs(den), jnp.exp(-m_s))[..., None]
        bL = b[..., -1]
        log_w = bL[..., None] - b + ic
        m_new = jnp.maximum(bL + m, jnp.max(log_w, axis=-1))
        wj = jnp.exp(log_w - m_new[..., None])
        carry_decay = jnp.exp(bL + m - m_new)
        C_new = carry_decay[..., None, None] * C + jnp.einsum('...j,...jv,...jk->...vk', wj, vc, kc)
        n_new = carry_decay[..., None] * n + jnp.einsum('...j,...jk->...k', wj, kc)
        return (C_new, n_new, m_new), h

    (C, n, m), hs = lax.scan(step, (C0, n0, m0),
                             (chunk(q), chunk(k), chunk(v), chunk_g(log_i), chunk_g(log_f)))
    h = jnp.moveaxis(hs, 0, -3).reshape(lead + (T, v.shape[-1]))
    return h, C, n, m


def mlstm_bidir(z, C0, n0, m0, rows, conv_w, gate_b, gn_g):
    f32 = jnp.float32
    B, T, _ = z.shape
    H, dh, W = MLSTM_HEADS, MLSTM_HEAD_DIM, MLSTM_WIDTH
    qk, v, o, gates = jnp.split(z, [2 * W, 3 * W, 4 * W], axis=-1)
    qk = jax.nn.silu(dwconv_grid(qk, conv_w, rows)).astype(f32)
    q, k = jnp.split(qk, 2, axis=-1)

    def heads(u):
        u = u.astype(f32).reshape(B, T, H, dh).transpose(0, 2, 1, 3)
        return flip_backward(jnp.broadcast_to(u[:, None], (B, N_DIR, H, T, dh)), 1, 3)

    gates = gates.astype(f32).reshape(B, T, 2, N_DIR, H) + gate_b
    gates = jnp.transpose(gates, (0, 2, 3, 4, 1))
    log_i = flip_backward(gates[:, 0], 1, 3)
    log_f = flip_backward(jax.nn.log_sigmoid(gates[:, 1]), 1, 3)
    h, C, n, m = mlstm_chunkwise(heads(q * dh ** -0.5), heads(k), heads(v), log_i, log_f,
                                 C0.astype(f32), n0.astype(f32), m0.astype(f32))
    h = flip_backward(h, 1, 3).sum(axis=1)
    mean = jnp.mean(h, axis=-1, keepdims=True)
    var = jnp.var(h, axis=-1, keepdims=True)
    h = ((h - mean) * lax.rsqrt(var + MLSTM_GN_EPS)).transpose(0, 2, 1, 3).reshape(B, T, W) * gn_g
    return h * jax.nn.sigmoid(o.astype(f32)), (C, n, m)


def trunk_layer(x, cond, rows, states, p):
    mod = (jax.nn.silu(cond) @ p['ada_w'] + p['ada_b'])[:, None, :]
    sh1, sc1, g1, sh2, sc2, g2 = jnp.split(mod, 6, axis=-1)
    h = rmsnorm(x, p['norm_g'][0]) * (1.0 + sc1) + sh1
    z = h @ p['w_in']
    z_r, z_m, z_g = jnp.split(z, [RWKV_COLS, RWKV_COLS + MLSTM_COLS], axis=-1)
    S0, C0, n0, m0 = states
    y_r, S = rwkv7_bidir(centred_shift(z_r, p['rwkv_mu']), S0, p['rwkv_w0'], p['rwkv_w_up'],
                         p['rwkv_a0'], p['rwkv_a_up'], p['rwkv_g_up'], p['rwkv_kk_scale'],
                         p['rwkv_k_a'], p['rwkv_r_k'], p['rwkv_lnx_g'], p['rwkv_lnx_b'])
    y_m, (C, n, m) = mlstm_bidir(z_m, C0, n0, m0, rows, p['mlstm_conv'], p['mlstm_gate_b'],
                                 p['mlstm_gn_g'])
    gate_r, gate_m = jnp.split(jax.nn.sigmoid(z_g), 2, axis=-1)
    merged = (gate_r * (y_r.astype(x.dtype) @ p['w_branch_rwkv'])
              + gate_m * (y_m.astype(x.dtype) @ p['w_branch_mlstm']))
    x = x + g1 * rmsnorm(merged @ p['w_out'], p['norm_g'][1])
    h = rmsnorm(x, p['norm_g'][2]) * (1.0 + sc2) + sh2
    u_act, u_val = jnp.split(h @ p['ffn_up'], 2, axis=-1)
    u_act = dwconv_grid(u_act, p['ffn_conv'], rows) + p['ffn_conv_b']
    f = (jax.nn.silu(u_act) * u_val) @ p['ffn_down']
    x = x + g2 * rmsnorm(f, p['norm_g'][3])
    return x, (S, C, n, m)


def setup_inputs(seed: int = 0) -> dict:
    key = jax.random.key(seed)
    ks = iter(jax.random.split(key, 40))
    f32 = jnp.float32

    def nrm(shape, scale):
        return jax.random.normal(next(ks), shape, f32) * scale

    L, D, H, N = DEPTH, D_MODEL, RWKV_HEADS, RWKV_HEAD_DIM
    MH, dh = MLSTM_HEADS, MLSTM_HEAD_DIM
    return {
        'x_prompt': nrm((BATCH, SEQ, D), 1.0),
        'x_sample': nrm((DEC_BATCH, DEC_SEQ, D), 1.0),
        'c': nrm((DEC_BATCH, D), 1.0),
        'state_rwkv': nrm((DEC_BATCH, L, N_DIR, H, N, N), 0.3),
        'state_mlstm_C': nrm((DEC_BATCH, L, N_DIR, MH, dh, dh), 0.1),
        'state_mlstm_n': nrm((DEC_BATCH, L, N_DIR, MH, dh), 0.5),
        'state_mlstm_m': nrm((DEC_BATCH, L, N_DIR, MH), 1.0),
        'c_ctx': nrm((D,), 1.0),
        'ada_w': nrm((L, D, 6 * D), 0.5 * D ** -0.5),
        'ada_b': nrm((L, 6 * D), 0.02),
        'norm_g': 1.0 + nrm((L, 4, D), 0.02),
        'w_in': nrm((L, D, IN_COLS), D ** -0.5),
        'rwkv_mu': jax.random.uniform(next(ks), (L, RWKV_COLS), f32),
        'rwkv_w0': nrm((L, N_DIR, RWKV_WIDTH), 1.0),
        'rwkv_w_up': nrm((L, N_DIR, DECAY_LORA, RWKV_WIDTH), 0.1),
        'rwkv_a0': nrm((L, N_DIR, RWKV_WIDTH), 0.5),
        'rwkv_a_up': nrm((L, N_DIR, ICLR_LORA, RWKV_WIDTH), 0.1),
        'rwkv_g_up': nrm((L, GATE_LORA, RWKV_WIDTH), GATE_LORA ** -0.5),
        'rwkv_kk_scale': 0.85 + nrm((L, RWKV_WIDTH), 0.02),
        'rwkv_k_a': 1.0 + nrm((L, RWKV_WIDTH), 0.02),
        'rwkv_r_k': nrm((L, H, N), 0.1),
        'rwkv_lnx_g': 1.0 + nrm((L, RWKV_WIDTH), 0.02),
        'rwkv_lnx_b': nrm((L, RWKV_WIDTH), 0.02),
        'mlstm_conv': nrm((L, 3, 3, 2 * MLSTM_WIDTH), 1.0 / 3.0),
        'mlstm_gate_b': jnp.stack([nrm((L, N_DIR, MH), 0.1), 3.0 + nrm((L, N_DIR, MH), 0.5)], axis=1),
        'mlstm_gn_g': 1.0 + nrm((L, MLSTM_WIDTH), 0.02),
        'w_branch_rwkv': nrm((L, RWKV_WIDTH, D), RWKV_WIDTH ** -0.5),
        'w_branch_mlstm': nrm((L, MLSTM_WIDTH, D), MLSTM_WIDTH ** -0.5),
        'w_out': nrm((L, D, D), D ** -0.5),
        'ffn_up': nrm((L, D, 2 * D_FF), D ** -0.5),
        'ffn_conv': nrm((L, 3, 3, D_FF), 1.0 / 3.0),
        'ffn_conv_b': nrm((L, D_FF), 0.02),
        'ffn_down': nrm((L, D_FF, D), D_FF ** -0.5),
    }


def reference(x_prompt, x_sample, c, state_rwkv, state_mlstm_C, state_mlstm_n, state_mlstm_m,
              c_ctx, ada_w, ada_b, norm_g, w_in, rwkv_mu, rwkv_w0, rwkv_w_up, rwkv_a0, rwkv_a_up,
              rwkv_g_up, rwkv_kk_scale, rwkv_k_a, rwkv_r_k, rwkv_lnx_g, rwkv_lnx_b, mlstm_conv,
              mlstm_gate_b, mlstm_gn_g, w_branch_rwkv, w_branch_mlstm, w_out, ffn_up, ffn_conv,
              ffn_conv_b, ffn_down):
    f32 = jnp.float32
    B = x_prompt.shape[0]
    ctx_init = (jnp.zeros((B, N_DIR, RWKV_HEADS, RWKV_HEAD_DIM, RWKV_HEAD_DIM), f32),
                jnp.zeros((B, N_DIR, MLSTM_HEADS, MLSTM_HEAD_DIM, MLSTM_HEAD_DIM), f32),
                jnp.zeros((B, N_DIR, MLSTM_HEADS, MLSTM_HEAD_DIM), f32),
                jnp.zeros((B, N_DIR, MLSTM_HEADS), f32))
    latent_rows = x_sample.shape[1] // GRID_W
    xp, xs = x_prompt, x_sample
    new_S, new_C, new_n, new_m = [], [], [], []
    for l in range(DEPTH):
        p = dict(ada_w=ada_w[l], ada_b=ada_b[l], norm_g=norm_g[l], w_in=w_in[l],
                 rwkv_mu=rwkv_mu[l], rwkv_w0=rwkv_w0[l], rwkv_w_up=rwkv_w_up[l],
                 rwkv_a0=rwkv_a0[l], rwkv_a_up=rwkv_a_up[l], rwkv_g_up=rwkv_g_up[l],
                 rwkv_kk_scale=rwkv_kk_scale[l], rwkv_k_a=rwkv_k_a[l], rwkv_r_k=rwkv_r_k[l],
                 rwkv_lnx_g=rwkv_lnx_g[l], rwkv_lnx_b=rwkv_lnx_b[l], mlstm_conv=mlstm_conv[l],
                 mlstm_gate_b=mlstm_gate_b[l], mlstm_gn_g=mlstm_gn_g[l],
                 w_branch_rwkv=w_branch_rwkv[l], w_branch_mlstm=w_branch_mlstm[l], w_out=w_out[l],
                 ffn_up=ffn_up[l], ffn_conv=ffn_conv[l], ffn_conv_b=ffn_conv_b[l],
                 ffn_down=ffn_down[l])
        xp, (S, C, n, m) = trunk_layer(xp, c_ctx[None, :], 1, ctx_init, p)
        new_S.append(S)
        new_C.append(C)
        new_n.append(n)
        new_m.append(m)
        xs, _ = trunk_layer(xs, c, latent_rows,
                            (state_rwkv[:, l], state_mlstm_C[:, l], state_mlstm_n[:, l],
                             state_mlstm_m[:, l]), p)
    out_dtype = x_prompt.dtype
    new_state_rwkv = jnp.stack(new_S, axis=1).astype(out_dtype)
    new_state_mlstm_C = jnp.stack(new_C, axis=1).astype(out_dtype)
    new_state_mlstm_n = jnp.stack(new_n, axis=1).astype(out_dtype)
    new_state_mlstm_m = jnp.stack(new_m, axis=1).astype(out_dtype)
    return (xp, xs, new_state_rwkv, new_state_mlstm_C, new_state_mlstm_n, new_state_mlstm_m)
```

```python
import functools

import jax
import jax.numpy as jnp
from jax import lax
from jax.experimental import pallas as pl
from jax.experimental.pallas import tpu as pltpu

F32 = jnp.float32
BF16 = jnp.bfloat16

D_MODEL = 1024
N_DIR = 2
RWKV_HEADS = 8
RWKV_HEAD_DIM = 64
RWKV_WIDTH = RWKV_HEADS * RWKV_HEAD_DIM
DECAY_LORA = 64
ICLR_LORA = 64
GATE_LORA = 128
MLSTM_HEADS = 4
MLSTM_HEAD_DIM = 128
MLSTM_WIDTH = MLSTM_HEADS * MLSTM_HEAD_DIM
MLSTM_CHUNK = 64
D_FF = 2816
GRID_W = 64
RMS_EPS = 1e-6
RWKV_GN_EPS = 64e-5
MLSTM_GN_EPS = 1e-5
DECAY_SCALE = 0.606531

RWKV_COLS = 3 * RWKV_WIDTH + N_DIR * DECAY_LORA + N_DIR * ICLR_LORA + GATE_LORA
MLSTM_GATES = 2 * N_DIR * MLSTM_HEADS
MLSTM_COLS = 4 * MLSTM_WIDTH + MLSTM_GATES
GATE_COLS = 2 * D_MODEL

LANE = 128
ZR_BLOCK = 2048
ZG_OFF = RWKV_COLS
ZM_OFF = ZR_BLOCK
ZS_OFF = ZM_OFF + 4 * MLSTM_WIDTH
Z_COLS = ZS_OFF + GATE_COLS

TOK_TILE = 256
RCHUNK = 64
CONV_CH_TILE = 256
VMEM_LIMIT = 56 * 1024 * 1024


def _params(sem):
    return pltpu.CompilerParams(dimension_semantics=sem, vmem_limit_bytes=VMEM_LIMIT)


def _resident(shape):
    nd = len(shape)
    return pl.BlockSpec(shape, lambda *_: (0,) * nd, pipeline_mode=pl.Buffered(1))


def _split2(a):
    hi = a.astype(BF16)
    lo = (a - hi.astype(F32)).astype(BF16)
    return hi, lo


def _split3(a):
    hi = a.astype(BF16)
    r1 = a - hi.astype(F32)
    mid = r1.astype(BF16)
    lo = (r1 - mid.astype(F32)).astype(BF16)
    return hi, mid, lo


def _dg(a, b, dims):
    return lax.dot_general(a, b, dims, preferred_element_type=F32)


def _mm(a, b, dims, passes):
    if passes == 1:
        return _dg(a.astype(BF16), b.astype(BF16), dims)
    ah, al = _split2(a)
    bh, bl = _split2(b)
    return _dg(ah, bh, dims) + (_dg(ah, bl, dims) + _dg(al, bh, dims))


def _mm_exact_lhs(a_bf16, b, dims):
    b1, b2, b3 = _split3(b)
    return _dg(a_bf16, b1, dims) + (_dg(a_bf16, b2, dims) + _dg(a_bf16, b3, dims))


def _mm_exact_rhs(a, b_bf16, dims):
    a1, a2, a3 = _split3(a)
    return _dg(a1, b_bf16, dims) + (_dg(a2, b_bf16, dims) + _dg(a3, b_bf16, dims))


_NN = (((1,), (0,)), ((), ()))
_NT = (((1,), (1,)), ((), ()))
_TN = (((0,), (0,)), ((), ()))
_BNN = (((2,), (1,)), ((0,), (0,)))
_BNT = (((2,), (2,)), ((0,), (0,)))
_BTN = (((1,), (1,)), ((0,), (0,)))


def _sigmoid(x):
    return jax.nn.sigmoid(x)


def _silu(x):
    return x * jax.nn.sigmoid(x)


def _rms(x, g):
    return x * lax.rsqrt(jnp.mean(x * x, axis=-1, keepdims=True) + RMS_EPS) * g


def _ada_kernel(cond_ref, w_ref, b_ref, o_ref):
    s = _silu(cond_ref[...])
    o_ref[...] = _dg(s.astype(BF16), w_ref[...].astype(BF16), _NN) + b_ref[...]


def _ada(cond8, ada_w, ada_b):
    n = ada_w.shape[1]
    tn = 1536
    return pl.pallas_call(
        _ada_kernel,
        grid=(n // tn,),
        in_specs=[_resident((8, D_MODEL)),
                  pl.BlockSpec((D_MODEL, tn), lambda j: (0, j)),
                  pl.BlockSpec((1, tn), lambda j: (0, j))],
        out_specs=pl.BlockSpec((8, tn), lambda j: (0, j)),
        out_shape=jax.ShapeDtypeStruct((8, n), F32),
        compiler_params=_params(("arbitrary",)),
        name="ada_mod",
    )(cond8, ada_w, ada_b.reshape(1, n))


def _in_kernel(x_ref, mod_ref, g_ref, w_ref, z_ref):
    mod = mod_ref[0]
    sh = mod[:, 0:D_MODEL]
    sc = mod[:, D_MODEL:2 * D_MODEL]
    h = _rms(x_ref[...], g_ref[...]) * (1.0 + sc) + sh
    z = _dg(h.astype(BF16), w_ref[...], _NN)
    z_ref[:, 0:ZS_OFF] = z[:, 0:ZS_OFF]
    z_ref[:, ZS_OFF:Z_COLS] = _sigmoid(z[:, ZS_OFF:Z_COLS])


def _in_proj(x2, mod, mod_row, norm_g0, w_in_packed):
    n = x2.shape[0]
    return pl.pallas_call(
        _in_kernel,
        grid=(n // TOK_TILE,),
        in_specs=[pl.BlockSpec((TOK_TILE, D_MODEL), lambda i: (i, 0)),
                  pl.BlockSpec((1, 1, 6 * D_MODEL), lambda i: (mod_row(i), 0, 0)),
                  _resident((1, D_MODEL)),
                  _resident((D_MODEL, Z_COLS))],
        out_specs=pl.BlockSpec((TOK_TILE, Z_COLS), lambda i: (i, 0)),
        out_shape=jax.ShapeDtypeStruct((n, Z_COLS), F32),
        compiler_params=_params(("arbitrary",)),
        name="in_proj",
    )(x2, mod, norm_g0, w_in_packed)


def _heads(x):
    return jnp.stack([x[:, h * RWKV_HEAD_DIM:(h + 1) * RWKV_HEAD_DIM] for h in range(RWKV_HEADS)], axis=0)


def _rwkv_local_kernel(chunks_per_seq, passes,
                       z_ref, zp_ref, zn_ref, mu_ref, w0_ref, wup_ref, a0_ref, aup_ref, gup_ref,
                       kks_ref, ka_ref, rk_ref, pones_ref,
                       rp_ref, y0_ref, gm_ref, hm_ref, gate_ref, bonus_ref):
    ci = pl.program_id(0)
    d = pl.program_id(1)
    C = RCHUNK
    W = RWKV_WIDTH
    pos = ci % chunks_per_seq
    has_prev = pos != 0
    has_next = pos != chunks_per_seq - 1

    z = z_ref[:, 0:RWKV_COLS]
    zp = jnp.where(has_prev, zp_ref[7:8, 0:RWKV_COLS], 0.0)
    zn = jnp.where(has_next, zn_ref[0:1, 0:RWKV_COLS], 0.0)
    trow = lax.broadcasted_iota(jnp.int32, (C, 1), 0)
    prev = jnp.where(trow == 0, zp, pltpu.roll(z, 1, 0))
    nxt = jnp.where(trow == C - 1, zn, pltpu.roll(z, C - 1, 0))
    zs = z + mu_ref[...] * (0.5 * (prev + nxt) - z)

    r = zs[:, 0:W]
    k = zs[:, W:2 * W]
    v = zs[:, 2 * W:3 * W]
    o = 3 * W
    wd = jnp.where(d == 0, zs[:, o:o + DECAY_LORA], zs[:, o + DECAY_LORA:o + 2 * DECAY_LORA])
    o += 2 * DECAY_LORA
    ad = jnp.where(d == 0, zs[:, o:o + ICLR_LORA], zs[:, o + ICLR_LORA:o + 2 * ICLR_LORA])
    o += 2 * ICLR_LORA
    gd = zs[:, o:o + GATE_LORA]

    logw = -DECAY_SCALE * _sigmoid(w0_ref[0] + _dg(jnp.tanh(wd).astype(BF16), wup_ref[0].astype(BF16), _NN))
    a = _sigmoid(a0_ref[0] + _dg(ad.astype(BF16), aup_ref[0].astype(BF16), _NN))

    pones = pones_ref[...]
    kks = k * kks_ref[...]
    norm = jnp.sqrt(_mm_exact_rhs(kks * kks, pones, _NN))
    kk = kks / jnp.maximum(norm, 1e-12)
    kd = k * (1.0 + (a - 1.0) * ka_ref[...])
    b = kk * a

    bonus = _mm_exact_rhs(r * kd * rk_ref[...], pones, _NN) * v

    @pl.when(d == 0)
    def _():
        gate_ref[...] = _dg(_sigmoid(gd).astype(BF16), gup_ref[...].astype(BF16), _NN)
        bonus_ref[...] = bonus

    @pl.when(d != 0)
    def _():
        bonus_ref[...] += bonus

    row = lax.broadcasted_iota(jnp.int32, (C, C), 0)
    col = lax.broadcasted_iota(jnp.int32, (C, C), 1)
    lag = (row - col) * (1 - 2 * d)
    incl = lag >= 0
    strict = lag > 0
    cum_i = _mm_exact_lhs(jnp.where(incl, 1.0, 0.0).astype(BF16), logw, _NN)
    cum_e = cum_i - logw
    ctot = jnp.sum(logw, axis=0, keepdims=True)
    e_ni = jnp.exp(-cum_i)
    e_ti = jnp.exp(ctot - cum_i)

    abar = _heads(kk * jnp.exp(cum_e))
    rbar = _heads(r * jnp.exp(cum_i))
    kt = _heads(kd * e_ni)
    bt = _heads(b * e_ni)
    kw = _heads(kd * e_ti)
    bw = _heads(b * e_ti)
    vh = _heads(v)
    wc = _heads(jnp.exp(ctot))

    mm = functools.partial(_mm, passes=passes)
    zero = jnp.zeros((), F32)
    a_kk = jnp.where(strict[None], mm(abar, bt, _BNT), zero)
    a_kv = jnp.where(strict[None], mm(abar, kt, _BNT), zero)
    a_rk = jnp.where(incl[None], mm(rbar, kt, _BNT), zero)
    a_rb = jnp.where(incl[None], mm(rbar, bt, _BNT), zero)

    eye = jnp.where(row == col, 1.0, 0.0)[None]
    x = -a_kk
    tinv = eye + x
    for _ in range(5):
        x = mm(x, x, _BNN)
        tinv = tinv + mm(tinv, x, _BNN)

    av = mm(a_kv, vh, _BNN)
    ap = mm(tinv, abar, _BNN)
    u0 = mm(tinv, av, _BNN)
    rp_ref[0, 0] = rbar - mm(a_rb, ap, _BNN)
    y0_ref[0, 0] = mm(a_rk, vh, _BNN) - mm(a_rb, u0, _BNN)
    gm_ref[0, 0] = eye * wc - mm(ap, bw, _BTN)
    hm_ref[0, 0] = mm(vh, kw, _BTN) - mm(u0, bw, _BTN)


def _rwkv_local(z, seq_len, p, passes):
    n = z.shape[0]
    nchunk = n // RCHUNK
    cps = seq_len // RCHUNK
    W = RWKV_WIDTH
    hb = RCHUNK // 8
    last8 = n // 8 - 1
    mat = jax.ShapeDtypeStruct((N_DIR, nchunk, RWKV_HEADS, RCHUNK, RWKV_HEAD_DIM), F32)
    mat_spec = pl.BlockSpec((1, 1, RWKV_HEADS, RCHUNK, RWKV_HEAD_DIM), lambda c, d: (d, c, 0, 0, 0))
    row_spec = pl.BlockSpec((RCHUNK, W), lambda c, d: (c, 0))
    dir_vec = pl.BlockSpec((1, 1, W), lambda c, d: (d, 0, 0))
    return pl.pallas_call(
        functools.partial(_rwkv_local_kernel, cps, passes),
        grid=(nchunk, N_DIR),
        in_specs=[pl.BlockSpec((RCHUNK, ZR_BLOCK), lambda c, d: (c, 0)),
                  pl.BlockSpec((8, ZR_BLOCK), lambda c, d: (jnp.maximum(c * hb - 1, 0), 0)),
                  pl.BlockSpec((8, ZR_BLOCK), lambda c, d: (jnp.minimum((c + 1) * hb, last8), 0)),
                  _resident((1, RWKV_COLS)),
                  dir_vec,
                  pl.BlockSpec((1, DECAY_LORA, W), lambda c, d: (d, 0, 0)),
                  dir_vec,
                  pl.BlockSpec((1, ICLR_LORA, W), lambda c, d: (d, 0, 0)),
                  _resident((GATE_LORA, W)),
                  _resident((1, W)), _resident((1, W)), _resident((1, W)),
                  _resident((W, W))],
        out_specs=[mat_spec, mat_spec, mat_spec, mat_spec, row_spec, row_spec],
        out_shape=[mat, mat, mat, mat,
                   jax.ShapeDtypeStruct((n, W), F32), jax.ShapeDtypeStruct((n, W), F32)],
        compiler_params=_params(("arbitrary", "arbitrary")),
        name="rwkv_local",
    )(z, z, z, p["mu"], p["w0"], p["w_up"], p["a0"], p["a_up"], p["g_up"],
      p["kk_scale"], p["k_a"], p["r_k"], p["pones"])


def _rwkv_scan_kernel(passes, s0_ref, rp_ref, y0_ref, gm_ref, hm_ref, ys_ref, sout_ref, s_scr):
    c = pl.program_id(2)

    @pl.when(c == 0)
    def _():
        s_scr[...] = s0_ref[0, 0]

    s = s_scr[...]
    y = _mm(rp_ref[0, 0], s, _BNT, passes) + y0_ref[0, 0]
    for h in range(RWKV_HEADS):
        ys_ref[0, :, h * RWKV_HEAD_DIM:(h + 1) * RWKV_HEAD_DIM] = y[h]
    s_new = _mm(s, gm_ref[0, 0], _BNN, passes) + hm_ref[0, 0]
    s_scr[...] = s_new
    sout_ref[0, 0] = s_new


def _rwkv_scan(s0, rp, y0, gm, hm, batch, seq_len, passes):
    cps = seq_len // RCHUNK
    n = batch * seq_len

    def chunk(b, d, c):
        return b * cps + c + d * (cps - 1 - 2 * c)

    mat_spec = pl.BlockSpec((1, 1, RWKV_HEADS, RCHUNK, RWKV_HEAD_DIM),
                            lambda b, d, c: (d, chunk(b, d, c), 0, 0, 0))
    st_spec = pl.BlockSpec((1, 1, RWKV_HEADS, RWKV_HEAD_DIM, RWKV_HEAD_DIM), lambda b, d, c: (b, d, 0, 0, 0))
    return pl.pallas_call(
        functools.partial(_rwkv_scan_kernel, passes),
        grid=(batch, N_DIR, cps),
        in_specs=[st_spec, mat_spec, mat_spec, mat_spec, mat_spec],
        out_specs=[pl.BlockSpec((1, RCHUNK, RWKV_WIDTH), lambda b, d, c: (d, chunk(b, d, c), 0)),
                   st_spec],
        out_shape=[jax.ShapeDtypeStruct((N_DIR, n, RWKV_WIDTH), F32),
                   jax.ShapeDtypeStruct((batch, N_DIR, RWKV_HEADS, RWKV_HEAD_DIM, RWKV_HEAD_DIM), F32)],
        scratch_shapes=[pltpu.VMEM((RWKV_HEADS, RWKV_HEAD_DIM, RWKV_HEAD_DIM), F32)],
        compiler_params=_params(("arbitrary", "arbitrary", "arbitrary")),
        name="rwkv_scan",
    )(s0, rp, y0, gm, hm)


def _dwconv(x, w_ref, rows):
    T = x.shape[0]
    t = lax.broadcasted_iota(jnp.int32, (T, 1), 0)
    width = T // rows
    assert width & (width - 1) == 0
    colp = jnp.bitwise_and(t, width - 1)
    xl = jnp.where(colp == 0, 0.0, pltpu.roll(x, 1, 0))
    xr = jnp.where(colp == width - 1, 0.0, pltpu.roll(x, T - 1, 0))

    def tap_row(i):
        return w_ref[3 * i:3 * i + 1, :] * xl + w_ref[3 * i + 1:3 * i + 2, :] * x + w_ref[3 * i + 2:3 * i + 3, :] * xr

    out = tap_row(1)
    if rows > 1:
        out = out + jnp.where(t < width, 0.0, pltpu.roll(tap_row(0), width, 0))
        out = out + jnp.where(t >= T - width, 0.0, pltpu.roll(tap_row(2), T - width, 0))
    return out


def _qk_conv_kernel(rows, x_ref, w_ref, o_ref):
    o_ref[...] = _silu(_dwconv(x_ref[...], w_ref, rows))


def _qk_conv(z, batch, seq_len, rows, conv_w9):
    n = batch * seq_len
    ch = 2 * MLSTM_WIDTH
    tc = CONV_CH_TILE
    off = ZM_OFF // tc
    return pl.pallas_call(
        functools.partial(_qk_conv_kernel, rows),
        grid=(batch, ch // tc),
        in_specs=[pl.BlockSpec((seq_len, tc), lambda b, j: (b, off + j)),
                  pl.BlockSpec((9, tc), lambda b, j: (0, j))],
        out_specs=pl.BlockSpec((seq_len, tc), lambda b, j: (b, j)),
        out_shape=jax.ShapeDtypeStruct((n, ch), F32),
        compiler_params=_params(("arbitrary", "arbitrary")),
        name="mlstm_qk_conv",
    )(z, conv_w9)


def _ffn_conv_kernel(rows, ua_ref, uv_ref, w_ref, b_ref, o_ref):
    act = _dwconv(ua_ref[...], w_ref, rows) + b_ref[...]
    o_ref[...] = (_silu(act) * uv_ref[...]).astype(BF16)


def _ffn_conv(u, batch, seq_len, rows, conv_w9, conv_b):
    n = batch * seq_len
    tc = CONV_CH_TILE
    nct = D_FF // tc
    return pl.pallas_call(
        functools.partial(_ffn_conv_kernel, rows),
        grid=(batch, nct),
        in_specs=[pl.BlockSpec((seq_len, tc), lambda b, j: (b, j)),
                  pl.BlockSpec((seq_len, tc), lambda b, j: (b, nct + j)),
                  pl.BlockSpec((9, tc), lambda b, j: (0, j)),
                  pl.BlockSpec((1, tc), lambda b, j: (0, j))],
        out_specs=pl.BlockSpec((seq_len, tc), lambda b, j: (b, j)),
        out_shape=jax.ShapeDtypeStruct((n, D_FF), BF16),
        compiler_params=_params(("arbitrary", "arbitrary")),
        name="ffn_conv",
    )(u, u, conv_w9, conv_b)


def _mlstm_scan_kernel(qkf_ref, qkb_ref, vf_ref, vb_ref, gcf_ref, gcb_ref, grf_ref, grb_ref,
                       gbc_ref, gbr_ref, c0_ref, n0_ref, m0_ref,
                       hf_ref, hb_ref, cout_ref, nout_ref, mout_ref,
                       c_scr, n_scr, m_scr):
    step = pl.program_id(1)
    L = MLSTM_CHUNK
    dh = MLSTM_HEAD_DIM
    H = MLSTM_HEADS

    @pl.when(step == 0)
    def _():
        c_scr[...] = c0_ref[0]
        n_scr[...] = n0_ref[0]
        m_scr[...] = m0_ref[0]

    row = lax.broadcasted_iota(jnp.int32, (L, L), 0)
    col = lax.broadcasted_iota(jnp.int32, (L, L), 1)
    lower = (row >= col)
    upper = (row <= col)
    lower_b = jnp.where(lower, 1.0, 0.0).astype(BF16)
    upper_b = jnp.where(upper, 1.0, 0.0).astype(BF16)
    neg_inf = jnp.full((), -jnp.inf, F32)

    for d in range(N_DIR):
        qk_ref, v_ref, gc_ref, gr_ref, h_ref = ((qkf_ref, vf_ref, gcf_ref, grf_ref, hf_ref) if d == 0 else
                                                (qkb_ref, vb_ref, gcb_ref, grb_ref, hb_ref))
        mask = lower if d == 0 else upper
        gcol = gc_ref[...] + gbc_ref[...]
        grow = gr_ref[0] + gbr_ref[...]
        fcol = jax.nn.log_sigmoid(gcol)
        frow = jax.nn.log_sigmoid(grow)
        bcol = _mm_exact_lhs(lower_b if d == 0 else upper_b, fcol, _NN)
        brow = _mm_exact_rhs(frow, upper_b if d == 0 else lower_b, _NN)
        last = L - 1 if d == 0 else 0
        for h in range(H):
            j = d * H + h
            gi = j
            gf = 2 * H + j
            q = qk_ref[:, h * dh:(h + 1) * dh] * (dh ** -0.5)
            k = qk_ref[:, MLSTM_WIDTH + h * dh:MLSTM_WIDTH + (h + 1) * dh]
            v = v_ref[:, h * dh:(h + 1) * dh]
            ic_col = gcol[:, gi:gi + 1]
            ic_row = grow[gi:gi + 1, :]
            b_col = bcol[:, gf:gf + 1]
            b_row = brow[gf:gf + 1, :]
            m_prev = m_scr[j:j + 1, 0:1]
            c_prev = c_scr[j]
            n_prev = n_scr[j:j + 1, :]

            log_d = jnp.where(mask, b_col - b_row + ic_row, neg_inf)
            log_inter = b_col + m_prev
            m_s = jnp.maximum(log_inter, jnp.max(log_d, axis=-1, keepdims=True))
            dmat = jnp.exp(log_d - m_s)
            inter = jnp.exp(log_inter - m_s)
            qb = q.astype(BF16)
            kb = k.astype(BF16)
            s = _dg(qb, kb, _NT) * dmat
            num = inter * _dg(qb, c_prev.astype(BF16), _NT) + _dg(s.astype(BF16), v.astype(BF16), _NN)
            den = inter * jnp.sum(q * n_prev, axis=-1, keepdims=True) + jnp.sum(s, axis=-1, keepdims=True)
            h_ref[:, h * dh:(h + 1) * dh] = num / jnp.maximum(jnp.abs(den), jnp.exp(-m_s))

            b_last = b_col[last:last + 1, :]
            m_new = jnp.maximum(b_last + m_prev, jnp.max(b_last - b_row + ic_row, axis=-1, keepdims=True))
            wj = jnp.exp(b_last - b_col + ic_col - m_new)
            carry = jnp.exp(b_last + m_prev - m_new)
            c_new = carry * c_prev + _dg((wj * v).astype(BF16), kb, _TN)
            n_new = carry * n_prev + jnp.sum(wj * k, axis=0, keepdims=True)
            c_scr[j] = c_new
            n_scr[j:j + 1, :] = n_new
            m_scr[j:j + 1, :] = jnp.broadcast_to(m_new, (1, LANE))

    cout_ref[0] = c_scr[...]
    nout_ref[0] = n_scr[...]
    mout_ref[0] = m_scr[...]


def _mlstm_scan(z, qk, gt, gate_bc, gate_br, c0, n0, m0, batch, seq_len):
    L = MLSTM_CHUNK
    cps = seq_len // L
    n = batch * seq_len
    W = MLSTM_WIDTH
    nst = N_DIR * MLSTM_HEADS
    dh = MLSTM_HEAD_DIM

    def fw(b, c):
        return b * cps + c

    def bw(b, c):
        return b * cps + cps - 1 - c

    vblk = (ZM_OFF + 2 * W) // W
    gblk = ZG_OFF // LANE
    return pl.pallas_call(
        _mlstm_scan_kernel,
        grid=(batch, cps),
        in_specs=[pl.BlockSpec((L, 2 * W), lambda b, c: (fw(b, c), 0)),
                  pl.BlockSpec((L, 2 * W), lambda b, c: (bw(b, c), 0)),
                  pl.BlockSpec((L, W), lambda b, c: (fw(b, c), vblk)),
                  pl.BlockSpec((L, W), lambda b, c: (bw(b, c), vblk)),
                  pl.BlockSpec((L, LANE), lambda b, c: (fw(b, c), gblk)),
                  pl.BlockSpec((L, LANE), lambda b, c: (bw(b, c), gblk)),
                  pl.BlockSpec((1, MLSTM_GATES, L), lambda b, c: (fw(b, c), 0, 0)),
                  pl.BlockSpec((1, MLSTM_GATES, L), lambda b, c: (bw(b, c), 0, 0)),
                  _resident((1, LANE)),
                  _resident((MLSTM_GATES, 1)),
                  pl.BlockSpec((1, nst, dh, dh), lambda b, c: (b, 0, 0, 0)),
                  pl.BlockSpec((1, nst, dh), lambda b, c: (b, 0, 0)),
                  pl.BlockSpec((1, nst, LANE), lambda b, c: (b, 0, 0))],
        out_specs=[pl.BlockSpec((L, W), lambda b, c: (fw(b, c), 0)),
                   pl.BlockSpec((L, W), lambda b, c: (bw(b, c), 0)),
                   pl.BlockSpec((1, nst, dh, dh), lambda b, c: (b, 0, 0, 0)),
                   pl.BlockSpec((1, nst, dh), lambda b, c: (b, 0, 0)),
                   pl.BlockSpec((1, nst, LANE), lambda b, c: (b, 0, 0))],
        out_shape=[jax.ShapeDtypeStruct((n, W), F32), jax.ShapeDtypeStruct((n, W), F32),
                   jax.ShapeDtypeStruct((batch, nst, dh, dh), F32),
                   jax.ShapeDtypeStruct((batch, nst, dh), F32),
                   jax.ShapeDtypeStruct((batch, nst, LANE), F32)],
        scratch_shapes=[pltpu.VMEM((nst, dh, dh), F32), pltpu.VMEM((nst, dh), F32),
                        pltpu.VMEM((nst, LANE), F32)],
        compiler_params=_params(("arbitrary", "arbitrary")),
        name="mlstm_scan",
    )(qk, qk, z, z, z, z, gt, gt, gate_bc, gate_br, c0, n0, m0)


def _merge_kernel(x_ref, mod_ref, ys_ref, bonus_ref, gate_ref, hf_ref, hb_ref, zo_ref, zs_ref,
                  lnxg_ref, lnxb_ref, gng_ref, pmean_ref, wbr_ref, wbm_ref, wout_ref, ng_ref, wup_ref,
                  x1_ref, u_ref):
    mod = mod_ref[0]
    g1 = mod[:, 2 * D_MODEL:3 * D_MODEL]
    sh2 = mod[:, 3 * D_MODEL:4 * D_MODEL]
    sc2 = mod[:, 4 * D_MODEL:5 * D_MODEL]

    ys = ys_ref[0] + ys_ref[1]
    pmean = pmean_ref[...]
    mean = _mm_exact_rhs(ys, pmean, _NN)
    cen = ys - mean
    var = _mm_exact_rhs(cen * cen, pmean, _NN)
    y_r = (cen * lax.rsqrt(var + RWKV_GN_EPS) * lnxg_ref[...] + lnxb_ref[...] + bonus_ref[...]) * gate_ref[...]

    hs = hf_ref[...] + hb_ref[...]
    parts = []
    for h in range(MLSTM_HEADS):
        hh = hs[:, h * MLSTM_HEAD_DIM:(h + 1) * MLSTM_HEAD_DIM]
        mu = jnp.mean(hh, axis=-1, keepdims=True)
        ce = hh - mu
        va = jnp.mean(ce * ce, axis=-1, keepdims=True)
        parts.append(ce * lax.rsqrt(va + MLSTM_GN_EPS))
    y_m = jnp.concatenate(parts, axis=1) * gng_ref[...] * _sigmoid(zo_ref[...])

    gates = zs_ref[...]
    merged = (gates[:, 0:D_MODEL] * _dg(y_r.astype(BF16), wbr_ref[...], _NN)
              + gates[:, D_MODEL:2 * D_MODEL] * _dg(y_m.astype(BF16), wbm_ref[...], _NN))
    t = _dg(merged.astype(BF16), wout_ref[...], _NN)
    x1 = x_ref[...] + g1 * _rms(t, ng_ref[1:2, :])
    x1_ref[...] = x1
    h2 = _rms(x1, ng_ref[2:3, :]) * (1.0 + sc2) + sh2
    u_ref[...] = _dg(h2.astype(BF16), wup_ref[...], _NN)


def _merge(x2, mod, mod_row, z, ys, bonus, gate, hf, hb, p):
    n = x2.shape[0]
    W = RWKV_WIDTH
    tile = lambda w: pl.BlockSpec((TOK_TILE, w), lambda i: (i, 0))
    return pl.pallas_call(
        _merge_kernel,
        grid=(n // TOK_TILE,),
        in_specs=[tile(D_MODEL),
                  pl.BlockSpec((1, 1, 6 * D_MODEL), lambda i: (mod_row(i), 0, 0)),
                  pl.BlockSpec((N_DIR, TOK_TILE, W), lambda i: (0, i, 0)),
                  tile(W), tile(W), tile(MLSTM_WIDTH), tile(MLSTM_WIDTH),
                  pl.BlockSpec((TOK_TILE, MLSTM_WIDTH), lambda i: (i, (ZM_OFF + 3 * MLSTM_WIDTH) // MLSTM_WIDTH)),
                  pl.BlockSpec((TOK_TILE, GATE_COLS), lambda i: (i, ZS_OFF // GATE_COLS)),
                  _resident((1, W)), _resident((1, W)), _resident((1, MLSTM_WIDTH)),
                  _resident((W, W)),
                  _resident((W, D_MODEL)), _resident((MLSTM_WIDTH, D_MODEL)),
                  _resident((D_MODEL, D_MODEL)), _resident((4, D_MODEL)),
                  _resident((D_MODEL, 2 * D_FF))],
        out_specs=[tile(D_MODEL), tile(2 * D_FF)],
        out_shape=[jax.ShapeDtypeStruct((n, D_MODEL), F32), jax.ShapeDtypeStruct((n, 2 * D_FF), F32)],
        compiler_params=_params(("arbitrary",)),
        name="merge_ffn_up",
    )(x2, mod, ys, bonus, gate, hf, hb, z, z, p["lnx_g"], p["lnx_b"], p["gn_g"], p["pmean"],
      p["w_br"], p["w_bm"], p["w_out"], p["norm_g"], p["ffn_up"])


def _down_kernel(x1_ref, mod_ref, a_ref, w_ref, ng_ref, o_ref):
    g2 = mod_ref[0][:, 5 * D_MODEL:6 * D_MODEL]
    f = _dg(a_ref[...], w_ref[...], _NN)
    o_ref[...] = x1_ref[...] + g2 * _rms(f, ng_ref[3:4, :])


def _down(x1, mod, mod_row, act, p):
    n = x1.shape[0]
    return pl.pallas_call(
        _down_kernel,
        grid=(n // TOK_TILE,),
        in_specs=[pl.BlockSpec((TOK_TILE, D_MODEL), lambda i: (i, 0)),
                  pl.BlockSpec((1, 1, 6 * D_MODEL), lambda i: (mod_row(i), 0, 0)),
                  pl.BlockSpec((TOK_TILE, D_FF), lambda i: (i, 0)),
                  _resident((D_FF, D_MODEL)), _resident((4, D_MODEL))],
        out_specs=pl.BlockSpec((TOK_TILE, D_MODEL), lambda i: (i, 0)),
        out_shape=jax.ShapeDtypeStruct((n, D_MODEL), F32),
        compiler_params=_params(("arbitrary",)),
        name="ffn_down",
    )(x1, mod, act, p["ffn_down"], p["norm_g"])


RWKV_PASSES = 3


def _trunk(x, mod, mod_row, rows, states, p):
    batch, seq_len, _ = x.shape
    n = batch * seq_len
    x2 = x.reshape(n, D_MODEL)
    s0, c0, n0, m0 = states

    z = _in_proj(x2, mod, mod_row, p["norm_g"][0:1], p["w_in"])

    rp, y0, gm, hm, gate, bonus = _rwkv_local(z, seq_len, p, RWKV_PASSES)
    ys, s_fin = _rwkv_scan(s0, rp, y0, gm, hm, batch, seq_len, RWKV_PASSES)

    qk = _qk_conv(z, batch, seq_len, rows, p["mlstm_conv"])
    gt = z[:, ZG_OFF:ZG_OFF + MLSTM_GATES].reshape(n // MLSTM_CHUNK, MLSTM_CHUNK, MLSTM_GATES).transpose(0, 2, 1)
    nst = N_DIR * MLSTM_HEADS
    hf, hb, c_fin, n_fin, m_fin = _mlstm_scan(
        z, qk, gt, p["gate_bc"], p["gate_br"],
        c0.reshape(batch, nst, MLSTM_HEAD_DIM, MLSTM_HEAD_DIM), n0.reshape(batch, nst, MLSTM_HEAD_DIM),
        jnp.broadcast_to(m0.reshape(batch, nst, 1), (batch, nst, LANE)), batch, seq_len)

    x1, u = _merge(x2, mod, mod_row, z, ys, bonus, gate, hf, hb, p)
    act = _ffn_conv(u, batch, seq_len, rows, p["ffn_conv"], p["ffn_conv_b"])
    out = _down(x1, mod, mod_row, act, p)

    new_states = (s_fin,
                  c_fin.reshape(batch, N_DIR, MLSTM_HEADS, MLSTM_HEAD_DIM, MLSTM_HEAD_DIM),
                  n_fin.reshape(batch, N_DIR, MLSTM_HEADS, MLSTM_HEAD_DIM),
                  m_fin[:, :, 0].reshape(batch, N_DIR, MLSTM_HEADS))
    return out.reshape(batch, seq_len, D_MODEL), new_states


def _pack_layer(l, ada_w, ada_b, norm_g, w_in, rwkv_mu, rwkv_w0, rwkv_w_up, rwkv_a0, rwkv_a_up,
                rwkv_g_up, rwkv_kk_scale, rwkv_k_a, rwkv_r_k, rwkv_lnx_g, rwkv_lnx_b, mlstm_conv,
                mlstm_gate_b, mlstm_gn_g, w_branch_rwkv, w_branch_mlstm, w_out, ffn_up, ffn_conv,
                ffn_conv_b, ffn_down):
    W = RWKV_WIDTH
    wi = w_in[l]
    w_r = wi[:, 0:RWKV_COLS]
    w_m = wi[:, RWKV_COLS:RWKV_COLS + 4 * MLSTM_WIDTH]
    w_mg = wi[:, RWKV_COLS + 4 * MLSTM_WIDTH:RWKV_COLS + MLSTM_COLS]
    w_s = wi[:, RWKV_COLS + MLSTM_COLS:]
    pad = jnp.zeros((D_MODEL, LANE - MLSTM_GATES), wi.dtype)
    w_packed = jnp.concatenate([w_r, w_mg, pad, w_m, w_s], axis=1).astype(BF16)

    head = jnp.arange(W, dtype=jnp.int32) // RWKV_HEAD_DIM
    same = (head[:, None] == head[None, :])
    gb = mlstm_gate_b[l].reshape(1, MLSTM_GATES)
    return dict(
        ada_w=ada_w[l], ada_b=ada_b[l], norm_g=norm_g[l], w_in=w_packed,
        mu=rwkv_mu[l].reshape(1, RWKV_COLS),
        w0=rwkv_w0[l].reshape(N_DIR, 1, W), w_up=rwkv_w_up[l],
        a0=rwkv_a0[l].reshape(N_DIR, 1, W), a_up=rwkv_a_up[l], g_up=rwkv_g_up[l],
        kk_scale=rwkv_kk_scale[l].reshape(1, W), k_a=rwkv_k_a[l].reshape(1, W),
        r_k=rwkv_r_k[l].reshape(1, W),
        lnx_g=rwkv_lnx_g[l].reshape(1, W), lnx_b=rwkv_lnx_b[l].reshape(1, W),
        pones=same.astype(BF16), pmean=(same.astype(F32) / RWKV_HEAD_DIM).astype(BF16),
        mlstm_conv=mlstm_conv[l].reshape(9, 2 * MLSTM_WIDTH),
        gate_bc=jnp.pad(gb, ((0, 0), (0, LANE - MLSTM_GATES))), gate_br=gb.reshape(MLSTM_GATES, 1),
        gn_g=mlstm_gn_g[l].reshape(1, MLSTM_WIDTH),
        w_br=w_branch_rwkv[l].astype(BF16), w_bm=w_branch_mlstm[l].astype(BF16),
        w_out=w_out[l].astype(BF16), ffn_up=ffn_up[l].astype(BF16),
        ffn_conv=ffn_conv[l].reshape(9, D_FF), ffn_conv_b=ffn_conv_b[l].reshape(1, D_FF),
        ffn_down=ffn_down[l].astype(BF16),
    )


def kernel(x_prompt, x_sample, c, state_rwkv, state_mlstm_C, state_mlstm_n, state_mlstm_m, c_ctx,
           ada_w, ada_b, norm_g, w_in, rwkv_mu, rwkv_w0, rwkv_w_up, rwkv_a0, rwkv_a_up, rwkv_g_up,
           rwkv_kk_scale, rwkv_k_a, rwkv_r_k, rwkv_lnx_g, rwkv_lnx_b, mlstm_conv, mlstm_gate_b,
           mlstm_gn_g, w_branch_rwkv, w_branch_mlstm, w_out, ffn_up, ffn_conv, ffn_conv_b, ffn_down):
    depth = ada_w.shape[0]
    batch = x_prompt.shape[0]
    dec_batch, dec_seq, _ = x_sample.shape
    latent_rows = dec_seq // GRID_W
    tiles_per_latent = dec_seq // TOK_TILE
    ctx_init = (jnp.zeros((batch, N_DIR, RWKV_HEADS, RWKV_HEAD_DIM, RWKV_HEAD_DIM), F32),
                jnp.zeros((batch, N_DIR, MLSTM_HEADS, MLSTM_HEAD_DIM, MLSTM_HEAD_DIM), F32),
                jnp.zeros((batch, N_DIR, MLSTM_HEADS, MLSTM_HEAD_DIM), F32),
                jnp.zeros((batch, N_DIR, MLSTM_HEADS), F32))
    cond = jnp.concatenate([c_ctx[None, :], c, jnp.zeros((8 - 1 - dec_batch, D_MODEL), F32)], axis=0)

    xp, xs = x_prompt, x_sample
    new_s, new_c, new_n, new_m = [], [], [], []
    for l in range(depth):
        p = _pack_layer(l, ada_w, ada_b, norm_g, w_in, rwkv_mu, rwkv_w0, rwkv_w_up, rwkv_a0, rwkv_a_up,
                        rwkv_g_up, rwkv_kk_scale, rwkv_k_a, rwkv_r_k, rwkv_lnx_g, rwkv_lnx_b, mlstm_conv,
                        mlstm_gate_b, mlstm_gn_g, w_branch_rwkv, w_branch_mlstm, w_out, ffn_up, ffn_conv,
                        ffn_conv_b, ffn_down)
        mod = _ada(cond, p["ada_w"], p["ada_b"]).reshape(8, 1, 6 * D_MODEL)
        xp, (s, cc, nn, mm) = _trunk(xp, mod, lambda i: 0, 1, ctx_init, p)
        new_s.append(s)
        new_c.append(cc)
        new_n.append(nn)
        new_m.append(mm)
        xs, _ = _trunk(xs, mod, lambda i: 1 + i // tiles_per_latent, latent_rows,
                       (state_rwkv[:, l], state_mlstm_C[:, l], state_mlstm_n[:, l], state_mlstm_m[:, l]), p)
    return (xp, xs, jnp.stack(new_s, axis=1), jnp.stack(new_c, axis=1),
            jnp.stack(new_n, axis=1), jnp.stack(new_m, axis=1))
```

```python
import functools

import jax
import jax.numpy as jnp
from jax import lax
from jax.experimental import pallas as pl
from jax.experimental.pallas import tpu as pltpu

F32 = jnp.float32
BF16 = jnp.bfloat16

D_MODEL = 1024
N_DIR = 2
RWKV_HEADS = 8
RWKV_HEAD_DIM = 64
RWKV_WIDTH = RWKV_HEADS * RWKV_HEAD_DIM
DECAY_LORA = 64
ICLR_LORA = 64
GATE_LORA = 128
MLSTM_HEADS = 4
MLSTM_HEAD_DIM = 128
MLSTM_WIDTH = MLSTM_HEADS * MLSTM_HEAD_DIM
MLSTM_CHUNK = 64
D_FF = 2816
GRID_W = 64
RMS_EPS = 1e-6
RWKV_GN_EPS = 64e-5
MLSTM_GN_EPS = 1e-5
DECAY_SCALE = 0.606531

RWKV_COLS = 3 * RWKV_WIDTH + N_DIR * DECAY_LORA + N_DIR * ICLR_LORA + GATE_LORA
MLSTM_GATES = 2 * N_DIR * MLSTM_HEADS
MLSTM_COLS = 4 * MLSTM_WIDTH + MLSTM_GATES
GATE_COLS = 2 * D_MODEL

LANE = 128
ZR_BLOCK = 2048
ZG_OFF = RWKV_COLS
ZM_OFF = ZR_BLOCK
ZS_OFF = ZM_OFF + 4 * MLSTM_WIDTH
Z_COLS = ZS_OFF + GATE_COLS

TOK_TILE = 256
RCHUNK = 64
CONV_CH_TILE = 256
VMEM_LIMIT = 56 * 1024 * 1024


def _params(sem):
    return pltpu.CompilerParams(dimension_semantics=sem, vmem_limit_bytes=VMEM_LIMIT)


def _resident(shape):
    nd = len(shape)
    return pl.BlockSpec(shape, lambda *_: (0,) * nd, pipeline_mode=pl.Buffered(1))


def _split2(a):
    hi = a.astype(BF16)
    lo = (a - hi.astype(F32)).astype(BF16)
    return hi, lo


def _split3(a):
    hi = a.astype(BF16)
    r1 = a - hi.astype(F32)
    mid = r1.astype(BF16)
    lo = (r1 - mid.astype(F32)).astype(BF16)
    return hi, mid, lo


def _dg(a, b, dims):
    return lax.dot_general(a, b, dims, preferred_element_type=F32)


def _mm(a, b, dims, passes):
    if passes == 1:
        return _dg(a.astype(BF16), b.astype(BF16), dims)
    ah, al = _split2(a)
    bh, bl = _split2(b)
    return _dg(ah, bh, dims) + (_dg(ah, bl, dims) + _dg(al, bh, dims))


def _mm_exact_lhs(a_bf16, b, dims):
    b1, b2, b3 = _split3(b)
    return _dg(a_bf16, b1, dims) + (_dg(a_bf16, b2, dims) + _dg(a_bf16, b3, dims))


def _mm_exact_rhs(a, b_bf16, dims):
    a1, a2, a3 = _split3(a)
    return _dg(a1, b_bf16, dims) + (_dg(a2, b_bf16, dims) + _dg(a3, b_bf16, dims))


_NN = (((1,), (0,)), ((), ()))
_NT = (((1,), (1,)), ((), ()))
_TN = (((0,), (0,)), ((), ()))
_BNN = (((2,), (1,)), ((0,), (0,)))
_BNT = (((2,), (2,)), ((0,), (0,)))
_BTN = (((1,), (1,)), ((0,), (0,)))


def _sigmoid(x):
    return jax.nn.sigmoid(x)


def _silu(x):
    return x * jax.nn.sigmoid(x)


def _rms(x, g):
    return x * lax.rsqrt(jnp.mean(x * x, axis=-1, keepdims=True) + RMS_EPS) * g


def _ada_kernel(cond_ref, w_ref, b_ref, o_ref):
    s = _silu(cond_ref[...])
    o_ref[...] = _dg(s.astype(BF16), w_ref[...].astype(BF16), _NN) + b_ref[...]


def _ada(cond8, ada_w, ada_b):
    n = ada_w.shape[1]
    tn = 1536
    return pl.pallas_call(
        _ada_kernel,
        grid=(n // tn,),
        in_specs=[_resident((8, D_MODEL)),
                  pl.BlockSpec((D_MODEL, tn), lambda j: (0, j)),
                  pl.BlockSpec((1, tn), lambda j: (0, j))],
        out_specs=pl.BlockSpec((8, tn), lambda j: (0, j)),
        out_shape=jax.ShapeDtypeStruct((8, n), F32),
        compiler_params=_params(("arbitrary",)),
        name="ada_mod",
    )(cond8, ada_w, ada_b.reshape(1, n))


def _in_kernel(x_ref, mod_ref, g_ref, w_ref, z_ref):
    mod = mod_ref[0]
    sh = mod[:, 0:D_MODEL]
    sc = mod[:, D_MODEL:2 * D_MODEL]
    h = _rms(x_ref[...], g_ref[...]) * (1.0 + sc) + sh
    z = _dg(h.astype(BF16), w_ref[...], _NN)
    z_ref[:, 0:ZS_OFF] = z[:, 0:ZS_OFF]
    z_ref[:, ZS_OFF:Z_COLS] = _sigmoid(z[:, ZS_OFF:Z_COLS])


def _in_proj(x2, mod, mod_row, norm_g0, w_in_packed):
    n = x2.shape[0]
    return pl.pallas_call(
        _in_kernel,
        grid=(n // TOK_TILE,),
        in_specs=[pl.BlockSpec((TOK_TILE, D_MODEL), lambda i: (i, 0)),
                  pl.BlockSpec((1, 1, 6 * D_MODEL), lambda i: (mod_row(i), 0, 0)),
                  _resident((1, D_MODEL)),
                  _resident((D_MODEL, Z_COLS))],
        out_specs=pl.BlockSpec((TOK_TILE, Z_COLS), lambda i: (i, 0)),
        out_shape=jax.ShapeDtypeStruct((n, Z_COLS), F32),
        compiler_params=_params(("arbitrary",)),
        name="in_proj",
    )(x2, mod, norm_g0, w_in_packed)


PAIR_LANES = 2 * RWKV_HEAD_DIM
RWKV_PAIRS = RWKV_HEADS // 2


def _bd(x):
    lane = lax.broadcasted_iota(jnp.int32, x.shape, 1)
    left = lane < RWKV_HEAD_DIM
    return jnp.concatenate([jnp.where(left, x, 0.0), jnp.where(left, 0.0, x)], axis=0)


def _rwkv_local_kernel(chunks_per_seq, passes,
                       z_ref, zp_ref, zn_ref, mu_ref, w0_ref, wup_ref, a0_ref, aup_ref, gup_ref,
                       kks_ref, ka_ref, rk_ref, pones_ref,
                       rp_ref, y0_ref, gm_ref, hm_ref, gate_ref, bonus_ref):
    ci = pl.program_id(0)
    C = RCHUNK
    W = RWKV_WIDTH
    pos = ci % chunks_per_seq
    has_prev = pos != 0
    has_next = pos != chunks_per_seq - 1

    z = z_ref[:, 0:RWKV_COLS]
    zp = jnp.where(has_prev, zp_ref[7:8, 0:RWKV_COLS], 0.0)
    zn = jnp.where(has_next, zn_ref[0:1, 0:RWKV_COLS], 0.0)
    trow = lax.broadcasted_iota(jnp.int32, (C, 1), 0)
    prev = jnp.where(trow == 0, zp, pltpu.roll(z, 1, 0))
    nxt = jnp.where(trow == C - 1, zn, pltpu.roll(z, C - 1, 0))
    zs = z + mu_ref[...] * (0.5 * (prev + nxt) - z)

    r = zs[:, 0:W]
    k = zs[:, W:2 * W]
    v = zs[:, 2 * W:3 * W]
    gd = zs[:, 3 * W + 2 * DECAY_LORA + 2 * ICLR_LORA:RWKV_COLS]
    gate_ref[...] = _dg(_sigmoid(gd).astype(BF16), gup_ref[...].astype(BF16), _NN)

    pones = pones_ref[...]
    kks = k * kks_ref[...]
    norm = jnp.sqrt(_mm_exact_rhs(kks * kks, pones, _NN))
    kk = kks / jnp.maximum(norm, 1e-12)

    P = PAIR_LANES
    row = lax.broadcasted_iota(jnp.int32, (C, C), 0)
    col = lax.broadcasted_iota(jnp.int32, (C, C), 1)
    prow = lax.broadcasted_iota(jnp.int32, (C, P), 0)
    pcol = jnp.bitwise_and(lax.broadcasted_iota(jnp.int32, (C, P), 1), RWKV_HEAD_DIM - 1)
    eye_p = jnp.where(prow == pcol, 1.0, 0.0)
    brow = lax.broadcasted_iota(jnp.int32, (P, P), 0)
    bcol = lax.broadcasted_iota(jnp.int32, (P, P), 1)
    same_head = (brow < RWKV_HEAD_DIM) == (bcol < RWKV_HEAD_DIM)
    eye_b = jnp.where(brow == bcol, 1.0, 0.0)

    abar, rbar, kt, bt, kw, bw, wc, strict, incl = [], [], [], [], [], [], [], [], []
    bonus = None
    for d in range(N_DIR):
        o = 3 * W + d * DECAY_LORA
        wd = zs[:, o:o + DECAY_LORA]
        o = 3 * W + 2 * DECAY_LORA + d * ICLR_LORA
        ad = zs[:, o:o + ICLR_LORA]
        logw = -DECAY_SCALE * _sigmoid(w0_ref[d] + _dg(jnp.tanh(wd).astype(BF16), wup_ref[d].astype(BF16), _NN))
        a = _sigmoid(a0_ref[d] + _dg(ad.astype(BF16), aup_ref[d].astype(BF16), _NN))
        kd = k * (1.0 + (a - 1.0) * ka_ref[...])
        b = kk * a
        bonus_d = _mm_exact_rhs(r * kd * rk_ref[...], pones, _NN) * v
        bonus = bonus_d if bonus is None else bonus + bonus_d

        earlier_or_same = (row >= col) if d == 0 else (row <= col)
        cum_i = _mm_exact_lhs(jnp.where(earlier_or_same, 1.0, 0.0).astype(BF16), logw, _NN)
        cum_e = cum_i - logw
        ctot = jnp.sum(logw, axis=0, keepdims=True)
        e_ni = jnp.exp(-cum_i)
        e_ti = jnp.exp(ctot - cum_i)
        abar.append(kk * jnp.exp(cum_e))
        rbar.append(r * jnp.exp(cum_i))
        kt.append(kd * e_ni)
        bt.append(b * e_ni)
        kw.append(kd * e_ti)
        bw.append(b * e_ti)
        wc.append(jnp.exp(ctot))
        strict.append((prow > pcol) if d == 0 else (prow < pcol))
        incl.append((prow >= pcol) if d == 0 else (prow <= pcol))
    bonus_ref[...] = bonus

    mm = functools.partial(_mm, passes=passes)
    chains = [(d, p) for d in range(N_DIR) for p in range(RWKV_PAIRS)]
    nch = range(len(chains))
    sel = lambda arr, i: arr[chains[i][0]][:, chains[i][1] * P:(chains[i][1] + 1) * P]
    cat0 = lambda a_, b_: jnp.concatenate([a_, b_], axis=0)
    cat1 = lambda a_, b_: jnp.concatenate([a_, b_], axis=1)
    vsl = [v[:, p * P:(p + 1) * P] for _, p in chains]
    lhs = [cat0(sel(abar, i), sel(rbar, i)) for i in nch]
    by_b = [mm(lhs[i], _bd(sel(bt, i)), _NT) for i in nch]
    by_k = [mm(lhs[i], _bd(sel(kt, i)), _NT) for i in nch]
    a_kk = [jnp.where(strict[chains[i][0]], by_b[i][0:C], 0.0) for i in nch]
    a_rb = [jnp.where(incl[chains[i][0]], by_b[i][C:2 * C], 0.0) for i in nch]
    a_kv = [jnp.where(strict[chains[i][0]], by_k[i][0:C], 0.0) for i in nch]
    a_rk = [jnp.where(incl[chains[i][0]], by_k[i][C:2 * C], 0.0) for i in nch]
    on_v = [mm(cat0(a_kv[i], a_rk[i]), _bd(vsl[i]), _NN) for i in nch]

    x = [-m for m in a_kk]
    tinv = [eye_p + m for m in x]
    x = [mm(m, _bd(m), _NN) for m in x]
    for _ in range(4):
        both = [mm(cat0(tinv[i], x[i]), _bd(x[i]), _NN) for i in nch]
        tinv = [tinv[i] + both[i][0:C] for i in nch]
        x = [m[C:2 * C] for m in both]
    tinv = [tinv[i] + mm(tinv[i], _bd(x[i]), _NN) for i in nch]

    solved = [mm(tinv[i], cat1(_bd(sel(abar, i)), _bd(on_v[i][0:C])), _NN) for i in nch]
    ap = [m[:, 0:P] for m in solved]
    u0 = [m[:, P:2 * P] for m in solved]
    corr = [mm(a_rb[i], cat1(_bd(ap[i]), _bd(u0[i])), _NN) for i in nch]
    on_b = [mm(cat1(ap[i], u0[i]), sel(bw, i), _TN) for i in nch]
    vk = [mm(vsl[i], sel(kw, i), _TN) for i in nch]
    for i in nch:
        d, p = chains[i]
        lanes = slice(p * P, (p + 1) * P)
        rp_ref[d, :, lanes] = sel(rbar, i) - corr[i][:, 0:P]
        y0_ref[d, :, lanes] = on_v[i][C:2 * C] - corr[i][:, P:2 * P]
        gm_ref[d, 0, p] = eye_b * sel(wc, i) - jnp.where(same_head, on_b[i][0:P], 0.0)
        hm_ref[d, 0, p] = jnp.where(same_head, vk[i] - on_b[i][P:2 * P], 0.0)


def _rwkv_local(z, seq_len, p, passes):
    n = z.shape[0]
    nchunk = n // RCHUNK
    cps = seq_len // RCHUNK
    W = RWKV_WIDTH
    hb = RCHUNK // 8
    last8 = n // 8 - 1
    mat = jax.ShapeDtypeStruct((N_DIR, nchunk, RWKV_PAIRS, PAIR_LANES, PAIR_LANES), F32)
    mat_spec = pl.BlockSpec((N_DIR, 1, RWKV_PAIRS, PAIR_LANES, PAIR_LANES), lambda c: (0, c, 0, 0, 0))
    tok = jax.ShapeDtypeStruct((N_DIR, n, W), F32)
    tok_spec = pl.BlockSpec((N_DIR, RCHUNK, W), lambda c: (0, c, 0))
    row_spec = pl.BlockSpec((RCHUNK, W), lambda c: (c, 0))
    return pl.pallas_call(
        functools.partial(_rwkv_local_kernel, cps, passes),
        grid=(nchunk,),
        in_specs=[pl.BlockSpec((RCHUNK, ZR_BLOCK), lambda c: (c, 0)),
                  pl.BlockSpec((8, ZR_BLOCK), lambda c: (jnp.maximum(c * hb - 1, 0), 0)),
                  pl.BlockSpec((8, ZR_BLOCK), lambda c: (jnp.minimum((c + 1) * hb, last8), 0)),
                  _resident((1, RWKV_COLS)),
                  _resident((N_DIR, 1, W)), _resident((N_DIR, DECAY_LORA, W)),
                  _resident((N_DIR, 1, W)), _resident((N_DIR, ICLR_LORA, W)),
                  _resident((GATE_LORA, W)),
                  _resident((1, W)), _resident((1, W)), _resident((1, W)),
                  _resident((W, W))],
        out_specs=[tok_spec, tok_spec, mat_spec, mat_spec, row_spec, row_spec],
        out_shape=[tok, tok, mat, mat,
                   jax.ShapeDtypeStruct((n, W), F32), jax.ShapeDtypeStruct((n, W), F32)],
        compiler_params=_params(("arbitrary",)),
        name="rwkv_local",
    )(z, z, z, p["mu"], p["w0"], p["w_up"], p["a0"], p["a_up"], p["g_up"],
      p["kk_scale"], p["k_a"], p["r_k"], p["pones"])


def _rwkv_scan_kernel(passes, s0_ref, rp_ref, y0_ref, gm_ref, hm_ref, ys_ref, sout_ref, s_scr):
    c = pl.program_id(2)

    @pl.when(c == 0)
    def _():
        s_scr[...] = s0_ref[0, 0]

    pairs = range(RWKV_PAIRS)
    sl = [slice(p * PAIR_LANES, (p + 1) * PAIR_LANES) for p in pairs]
    s = [s_scr[p] for p in pairs]
    y = [_mm(rp_ref[0, :, sl[p]], s[p], _NT, passes) for p in pairs]
    s_new = [_mm(s[p], gm_ref[0, 0, p], _NN, passes) for p in pairs]
    for p in pairs:
        ys_ref[0, :, sl[p]] = y[p] + y0_ref[0, :, sl[p]]
        s_scr[p] = s_new[p] + hm_ref[0, 0, p]
        sout_ref[0, 0, p] = s_new[p] + hm_ref[0, 0, p]


def _rwkv_scan(s0, rp, y0, gm, hm, batch, seq_len, passes):
    cps = seq_len // RCHUNK
    n = batch * seq_len

    def chunk(b, d, c):
        return b * cps + c + d * (cps - 1 - 2 * c)

    mat_spec = pl.BlockSpec((1, 1, RWKV_PAIRS, PAIR_LANES, PAIR_LANES),
                            lambda b, d, c: (d, chunk(b, d, c), 0, 0, 0))
    tok_spec = pl.BlockSpec((1, RCHUNK, RWKV_WIDTH), lambda b, d, c: (d, chunk(b, d, c), 0))
    st_spec = pl.BlockSpec((1, 1, RWKV_PAIRS, PAIR_LANES, PAIR_LANES), lambda b, d, c: (b, d, 0, 0, 0))
    return pl.pallas_call(
        functools.partial(_rwkv_scan_kernel, passes),
        grid=(batch, N_DIR, cps),
        in_specs=[st_spec, tok_spec, tok_spec, mat_spec, mat_spec],
        out_specs=[tok_spec, st_spec],
        out_shape=[jax.ShapeDtypeStruct((N_DIR, n, RWKV_WIDTH), F32),
                   jax.ShapeDtypeStruct((batch, N_DIR, RWKV_PAIRS, PAIR_LANES, PAIR_LANES), F32)],
        scratch_shapes=[pltpu.VMEM((RWKV_PAIRS, PAIR_LANES, PAIR_LANES), F32)],
        compiler_params=_params(("arbitrary", "arbitrary", "arbitrary")),
        name="rwkv_scan",
    )(s0, rp, y0, gm, hm)


def _dwconv(x, w_ref, rows):
    T = x.shape[0]
    t = lax.broadcasted_iota(jnp.int32, (T, 1), 0)
    width = T // rows
    assert width & (width - 1) == 0
    colp = jnp.bitwise_and(t, width - 1)
    xl = jnp.where(colp == 0, 0.0, pltpu.roll(x, 1, 0))
    xr = jnp.where(colp == width - 1, 0.0, pltpu.roll(x, T - 1, 0))

    def tap_row(i):
        return w_ref[3 * i:3 * i + 1, :] * xl + w_ref[3 * i + 1:3 * i + 2, :] * x + w_ref[3 * i + 2:3 * i + 3, :] * xr

    out = tap_row(1)
    if rows > 1:
        out = out + jnp.where(t < width, 0.0, pltpu.roll(tap_row(0), width, 0))
        out = out + jnp.where(t >= T - width, 0.0, pltpu.roll(tap_row(2), T - width, 0))
    return out


def _qk_conv_kernel(rows, x_ref, w_ref, o_ref):
    o_ref[...] = _silu(_dwconv(x_ref[...], w_ref, rows))


def _qk_conv(z, batch, seq_len, rows, conv_w9):
    n = batch * seq_len
    ch = 2 * MLSTM_WIDTH
    tc = CONV_CH_TILE
    off = ZM_OFF // tc
    return pl.pallas_call(
        functools.partial(_qk_conv_kernel, rows),
        grid=(batch, ch // tc),
        in_specs=[pl.BlockSpec((seq_len, tc), lambda b, j: (b, off + j)),
                  pl.BlockSpec((9, tc), lambda b, j: (0, j))],
        out_specs=pl.BlockSpec((seq_len, tc), lambda b, j: (b, j)),
        out_shape=jax.ShapeDtypeStruct((n, ch), F32),
        compiler_params=_params(("arbitrary", "arbitrary")),
        name="mlstm_qk_conv",
    )(z, conv_w9)


def _ffn_conv_kernel(rows, ua_ref, uv_ref, w_ref, b_ref, o_ref):
    act = _dwconv(ua_ref[...], w_ref, rows) + b_ref[...]
    o_ref[...] = (_silu(act) * uv_ref[...]).astype(BF16)


def _ffn_conv(u, batch, seq_len, rows, conv_w9, conv_b):
    n = batch * seq_len
    tc = CONV_CH_TILE
    nct = D_FF // tc
    return pl.pallas_call(
        functools.partial(_ffn_conv_kernel, rows),
        grid=(batch, nct),
        in_specs=[pl.BlockSpec((seq_len, tc), lambda b, j: (b, j)),
                  pl.BlockSpec((seq_len, tc), lambda b, j: (b, nct + j)),
                  pl.BlockSpec((9, tc), lambda b, j: (0, j)),
                  pl.BlockSpec((1, tc), lambda b, j: (0, j))],
        out_specs=pl.BlockSpec((seq_len, tc), lambda b, j: (b, j)),
        out_shape=jax.ShapeDtypeStruct((n, D_FF), BF16),
        compiler_params=_params(("arbitrary", "arbitrary")),
        name="ffn_conv",
    )(u, u, conv_w9, conv_b)


def _mlstm_scan_kernel(qkf_ref, qkb_ref, vf_ref, vb_ref, gcf_ref, gcb_ref, grf_ref, grb_ref,
                       gbc_ref, gbr_ref, c0_ref, n0_ref, m0_ref,
                       hf_ref, hb_ref, cout_ref, nout_ref, mout_ref,
                       c_scr, n_scr, m_scr):
    step = pl.program_id(1)
    L = MLSTM_CHUNK
    dh = MLSTM_HEAD_DIM
    H = MLSTM_HEADS

    @pl.when(step == 0)
    def _():
        c_scr[...] = c0_ref[0]
        n_scr[...] = n0_ref[0]
        m_scr[...] = m0_ref[0]

    row = lax.broadcasted_iota(jnp.int32, (L, L), 0)
    col = lax.broadcasted_iota(jnp.int32, (L, L), 1)
    lower = (row >= col)
    upper = (row <= col)
    lower_b = jnp.where(lower, 1.0, 0.0).astype(BF16)
    upper_b = jnp.where(upper, 1.0, 0.0).astype(BF16)
    neg_inf = jnp.full((), -jnp.inf, F32)

    gcol, grow, bcol, brow = [], [], [], []
    for d in range(N_DIR):
        gc_ref, gr_ref = (gcf_ref, grf_ref) if d == 0 else (gcb_ref, grb_ref)
        gcol.append(gc_ref[...] + gbc_ref[...])
        grow.append(gr_ref[0] + gbr_ref[...])
        bcol.append(_mm_exact_lhs(lower_b if d == 0 else upper_b, jax.nn.log_sigmoid(gcol[d]), _NN))
        brow.append(_mm_exact_rhs(jax.nn.log_sigmoid(grow[d]), upper_b if d == 0 else lower_b, _NN))

    chains = [(d, h) for d in range(N_DIR) for h in range(H)]
    nch = range(len(chains))
    q, k, v, kb, qb = [], [], [], [], []
    ic_col, ic_row, b_col, b_row, m_prev, b_last = [], [], [], [], [], []
    for d, h in chains:
        j = d * H + h
        gi, gf = j, 2 * H + j
        qk_ref, v_ref = (qkf_ref, vf_ref) if d == 0 else (qkb_ref, vb_ref)
        q.append(qk_ref[:, h * dh:(h + 1) * dh] * (dh ** -0.5))
        k.append(qk_ref[:, MLSTM_WIDTH + h * dh:MLSTM_WIDTH + (h + 1) * dh])
        v.append(v_ref[:, h * dh:(h + 1) * dh])
        qb.append(q[-1].astype(BF16))
        kb.append(k[-1].astype(BF16))
        ic_col.append(gcol[d][:, gi:gi + 1])
        ic_row.append(grow[d][gi:gi + 1, :])
        b_col.append(bcol[d][:, gf:gf + 1])
        b_row.append(brow[d][gf:gf + 1, :])
        last = L - 1 if d == 0 else 0
        b_last.append(b_col[-1][last:last + 1, :])
        m_prev.append(m_scr[j:j + 1, 0:1])

    qk_t = [_dg(qb[i], kb[i], _NT) for i in nch]
    q_c = [_dg(qb[i], c_scr[i].astype(BF16), _NT) for i in nch]
    log_d = [jnp.where(lower if chains[i][0] == 0 else upper, b_col[i] - b_row[i] + ic_row[i], neg_inf)
             for i in nch]
    log_inter = [b_col[i] + m_prev[i] for i in nch]
    m_s = [jnp.maximum(log_inter[i], jnp.max(log_d[i], axis=-1, keepdims=True)) for i in nch]
    s = [qk_t[i] * jnp.exp(log_d[i] - m_s[i]) for i in nch]
    s_v = [_dg(s[i].astype(BF16), v[i].astype(BF16), _NN) for i in nch]
    m_new = [jnp.maximum(b_last[i] + m_prev[i],
                         jnp.max(b_last[i] - b_row[i] + ic_row[i], axis=-1, keepdims=True)) for i in nch]
    wj = [jnp.exp(b_last[i] - b_col[i] + ic_col[i] - m_new[i]) for i in nch]
    wv_k = [_dg((wj[i] * v[i]).astype(BF16), kb[i], _TN) for i in nch]
    for i in nch:
        d, h = chains[i]
        h_ref = hf_ref if d == 0 else hb_ref
        inter = jnp.exp(log_inter[i] - m_s[i])
        n_prev = n_scr[i:i + 1, :]
        num = inter * q_c[i] + s_v[i]
        den = inter * jnp.sum(q[i] * n_prev, axis=-1, keepdims=True) + jnp.sum(s[i], axis=-1, keepdims=True)
        h_ref[:, h * dh:(h + 1) * dh] = num / jnp.maximum(jnp.abs(den), jnp.exp(-m_s[i]))
        carry = jnp.exp(b_last[i] + m_prev[i] - m_new[i])
        c_scr[i] = carry * c_scr[i] + wv_k[i]
        n_scr[i:i + 1, :] = carry * n_prev + jnp.sum(wj[i] * k[i], axis=0, keepdims=True)
        m_scr[i:i + 1, :] = jnp.broadcast_to(m_new[i], (1, LANE))

    cout_ref[0] = c_scr[...]
    nout_ref[0] = n_scr[...]
    mout_ref[0] = m_scr[...]


def _mlstm_scan(z, qk, gt, gate_bc, gate_br, c0, n0, m0, batch, seq_len):
    L = MLSTM_CHUNK
    cps = seq_len // L
    n = batch * seq_len
    W = MLSTM_WIDTH
    nst = N_DIR * MLSTM_HEADS
    dh = MLSTM_HEAD_DIM

    def fw(b, c):
        return b * cps + c

    def bw(b, c):
        return b * cps + cps - 1 - c

    vblk = (ZM_OFF + 2 * W) // W
    gblk = ZG_OFF // LANE
    return pl.pallas_call(
        _mlstm_scan_kernel,
        grid=(batch, cps),
        in_specs=[pl.BlockSpec((L, 2 * W), lambda b, c: (fw(b, c), 0)),
                  pl.BlockSpec((L, 2 * W), lambda b, c: (bw(b, c), 0)),
                  pl.BlockSpec((L, W), lambda b, c: (fw(b, c), vblk)),
                  pl.BlockSpec((L, W), lambda b, c: (bw(b, c), vblk)),
                  pl.BlockSpec((L, LANE), lambda b, c: (fw(b, c), gblk)),
                  pl.BlockSpec((L, LANE), lambda b, c: (bw(b, c), gblk)),
                  pl.BlockSpec((1, MLSTM_GATES, L), lambda b, c: (fw(b, c), 0, 0)),
                  pl.BlockSpec((1, MLSTM_GATES, L), lambda b, c: (bw(b, c), 0, 0)),
                  _resident((1, LANE)),
                  _resident((MLSTM_GATES, 1)),
                  pl.BlockSpec((1, nst, dh, dh), lambda b, c: (b, 0, 0, 0)),
                  pl.BlockSpec((1, nst, dh), lambda b, c: (b, 0, 0)),
                  pl.BlockSpec((1, nst, LANE), lambda b, c: (b, 0, 0))],
        out_specs=[pl.BlockSpec((L, W), lambda b, c: (fw(b, c), 0)),
                   pl.BlockSpec((L, W), lambda b, c: (bw(b, c), 0)),
                   pl.BlockSpec((1, nst, dh, dh), lambda b, c: (b, 0, 0, 0)),
                   pl.BlockSpec((1, nst, dh), lambda b, c: (b, 0, 0)),
                   pl.BlockSpec((1, nst, LANE), lambda b, c: (b, 0, 0))],
        out_shape=[jax.ShapeDtypeStruct((n, W), F32), jax.ShapeDtypeStruct((n, W), F32),
                   jax.ShapeDtypeStruct((batch, nst, dh, dh), F32),
                   jax.ShapeDtypeStruct((batch, nst, dh), F32),
                   jax.ShapeDtypeStruct((batch, nst, LANE), F32)],
        scratch_shapes=[pltpu.VMEM((nst, dh, dh), F32), pltpu.VMEM((nst, dh), F32),
                        pltpu.VMEM((nst, LANE), F32)],
        compiler_params=_params(("arbitrary", "arbitrary")),
        name="mlstm_scan",
    )(qk, qk, z, z, z, z, gt, gt, gate_bc, gate_br, c0, n0, m0)


def _merge_kernel(x_ref, mod_ref, ys_ref, bonus_ref, gate_ref, hf_ref, hb_ref, zo_ref, zs_ref,
                  lnxg_ref, lnxb_ref, gng_ref, pmean_ref, wbr_ref, wbm_ref, wout_ref, ng_ref, wup_ref,
                  x1_ref, u_ref):
    mod = mod_ref[0]
    g1 = mod[:, 2 * D_MODEL:3 * D_MODEL]
    sh2 = mod[:, 3 * D_MODEL:4 * D_MODEL]
    sc2 = mod[:, 4 * D_MODEL:5 * D_MODEL]

    ys = ys_ref[0] + ys_ref[1]
    pmean = pmean_ref[...]
    mean = _mm_exact_rhs(ys, pmean, _NN)
    cen = ys - mean
    var = _mm_exact_rhs(cen * cen, pmean, _NN)
    y_r = (cen * lax.rsqrt(var + RWKV_GN_EPS) * lnxg_ref[...] + lnxb_ref[...] + bonus_ref[...]) * gate_ref[...]

    hs = hf_ref[...] + hb_ref[...]
    parts = []
    for h in range(MLSTM_HEADS):
        hh = hs[:, h * MLSTM_HEAD_DIM:(h + 1) * MLSTM_HEAD_DIM]
        mu = jnp.mean(hh, axis=-1, keepdims=True)
        ce = hh - mu
        va = jnp.mean(ce * ce, axis=-1, keepdims=True)
        parts.append(ce * lax.rsqrt(va + MLSTM_GN_EPS))
    y_m = jnp.concatenate(parts, axis=1) * gng_ref[...] * _sigmoid(zo_ref[...])

    gates = zs_ref[...]
    merged = (gates[:, 0:D_MODEL] * _dg(y_r.astype(BF16), wbr_ref[...], _NN)
              + gates[:, D_MODEL:2 * D_MODEL] * _dg(y_m.astype(BF16), wbm_ref[...], _NN))
    t = _dg(merged.astype(BF16), wout_ref[...], _NN)
    x1 = x_ref[...] + g1 * _rms(t, ng_ref[1:2, :])
    x1_ref[...] = x1
    h2 = _rms(x1, ng_ref[2:3, :]) * (1.0 + sc2) + sh2
    u_ref[...] = _dg(h2.astype(BF16), wup_ref[...], _NN)


def _merge(x2, mod, mod_row, z, ys, bonus, gate, hf, hb, p):
    n = x2.shape[0]
    W = RWKV_WIDTH
    tile = lambda w: pl.BlockSpec((TOK_TILE, w), lambda i: (i, 0))
    return pl.pallas_call(
        _merge_kernel,
        grid=(n // TOK_TILE,),
        in_specs=[tile(D_MODEL),
                  pl.BlockSpec((1, 1, 6 * D_MODEL), lambda i: (mod_row(i), 0, 0)),
                  pl.BlockSpec((N_DIR, TOK_TILE, W), lambda i: (0, i, 0)),
                  tile(W), tile(W), tile(MLSTM_WIDTH), tile(MLSTM_WIDTH),
                  pl.BlockSpec((TOK_TILE, MLSTM_WIDTH), lambda i: (i, (ZM_OFF + 3 * MLSTM_WIDTH) // MLSTM_WIDTH)),
                  pl.BlockSpec((TOK_TILE, GATE_COLS), lambda i: (i, ZS_OFF // GATE_COLS)),
                  _resident((1, W)), _resident((1, W)), _resident((1, MLSTM_WIDTH)),
                  _resident((W, W)),
                  _resident((W, D_MODEL)), _resident((MLSTM_WIDTH, D_MODEL)),
                  _resident((D_MODEL, D_MODEL)), _resident((4, D_MODEL)),
                  _resident((D_MODEL, 2 * D_FF))],
        out_specs=[tile(D_MODEL), tile(2 * D_FF)],
        out_shape=[jax.ShapeDtypeStruct((n, D_MODEL), F32), jax.ShapeDtypeStruct((n, 2 * D_FF), F32)],
        compiler_params=_params(("arbitrary",)),
        name="merge_ffn_up",
    )(x2, mod, ys, bonus, gate, hf, hb, z, z, p["lnx_g"], p["lnx_b"], p["gn_g"], p["pmean"],
      p["w_br"], p["w_bm"], p["w_out"], p["norm_g"], p["ffn_up"])


def _down_kernel(x1_ref, mod_ref, a_ref, w_ref, ng_ref, o_ref):
    g2 = mod_ref[0][:, 5 * D_MODEL:6 * D_MODEL]
    f = _dg(a_ref[...], w_ref[...], _NN)
    o_ref[...] = x1_ref[...] + g2 * _rms(f, ng_ref[3:4, :])


def _down(x1, mod, mod_row, act, p):
    n = x1.shape[0]
    return pl.pallas_call(
        _down_kernel,
        grid=(n // TOK_TILE,),
        in_specs=[pl.BlockSpec((TOK_TILE, D_MODEL), lambda i: (i, 0)),
                  pl.BlockSpec((1, 1, 6 * D_MODEL), lambda i: (mod_row(i), 0, 0)),
                  pl.BlockSpec((TOK_TILE, D_FF), lambda i: (i, 0)),
                  _resident((D_FF, D_MODEL)), _resident((4, D_MODEL))],
        out_specs=pl.BlockSpec((TOK_TILE, D_MODEL), lambda i: (i, 0)),
        out_shape=jax.ShapeDtypeStruct((n, D_MODEL), F32),
        compiler_params=_params(("arbitrary",)),
        name="ffn_down",
    )(x1, mod, act, p["ffn_down"], p["norm_g"])


RWKV_LOCAL_PASSES = 1
RWKV_SCAN_PASSES = 3


def _state_to_pairs(s):
    b = s.shape[0]
    s = s.reshape(b, N_DIR, RWKV_PAIRS, 2, RWKV_HEAD_DIM, RWKV_HEAD_DIM)
    zero = jnp.zeros_like(s[:, :, :, 0])
    top = jnp.concatenate([s[:, :, :, 0], zero], axis=-1)
    bot = jnp.concatenate([zero, s[:, :, :, 1]], axis=-1)
    return jnp.concatenate([top, bot], axis=-2)


def _state_from_pairs(sb):
    b = sb.shape[0]
    n = RWKV_HEAD_DIM
    parts = jnp.stack([sb[..., 0:n, 0:n], sb[..., n:2 * n, n:2 * n]], axis=3)
    return parts.reshape(b, N_DIR, RWKV_HEADS, n, n)


def _trunk(x, mod, mod_row, rows, states, p):
    batch, seq_len, _ = x.shape
    n = batch * seq_len
    x2 = x.reshape(n, D_MODEL)
    s0, c0, n0, m0 = states

    z = _in_proj(x2, mod, mod_row, p["norm_g"][0:1], p["w_in"])

    rp, y0, gm, hm, gate, bonus = _rwkv_local(z, seq_len, p, RWKV_LOCAL_PASSES)
    ys, s_fin = _rwkv_scan(_state_to_pairs(s0), rp, y0, gm, hm, batch, seq_len, RWKV_SCAN_PASSES)
    s_fin = _state_from_pairs(s_fin)

    qk = _qk_conv(z, batch, seq_len, rows, p["mlstm_conv"])
    gt = z[:, ZG_OFF:ZG_OFF + MLSTM_GATES].reshape(n // MLSTM_CHUNK, MLSTM_CHUNK, MLSTM_GATES).transpose(0, 2, 1)
    nst = N_DIR * MLSTM_HEADS
    hf, hb, c_fin, n_fin, m_fin = _mlstm_scan(
        z, qk, gt, p["gate_bc"], p["gate_br"],
        c0.reshape(batch, nst, MLSTM_HEAD_DIM, MLSTM_HEAD_DIM), n0.reshape(batch, nst, MLSTM_HEAD_DIM),
        jnp.broadcast_to(m0.reshape(batch, nst, 1), (batch, nst, LANE)), batch, seq_len)

    x1, u = _merge(x2, mod, mod_row, z, ys, bonus, gate, hf, hb, p)
    act = _ffn_conv(u, batch, seq_len, rows, p["ffn_conv"], p["ffn_conv_b"])
    out = _down(x1, mod, mod_row, act, p)

    new_states = (s_fin,
                  c_fin.reshape(batch, N_DIR, MLSTM_HEADS, MLSTM_HEAD_DIM, MLSTM_HEAD_DIM),
                  n_fin.reshape(batch, N_DIR, MLSTM_HEADS, MLSTM_HEAD_DIM),
                  m_fin[:, :, 0].reshape(batch, N_DIR, MLSTM_HEADS))
    return out.reshape(batch, seq_len, D_MODEL), new_states


def _pack_layer(l, ada_w, ada_b, norm_g, w_in, rwkv_mu, rwkv_w0, rwkv_w_up, rwkv_a0, rwkv_a_up,
                rwkv_g_up, rwkv_kk_scale, rwkv_k_a, rwkv_r_k, rwkv_lnx_g, rwkv_lnx_b, mlstm_conv,
                mlstm_gate_b, mlstm_gn_g, w_branch_rwkv, w_branch_mlstm, w_out, ffn_up, ffn_conv,
                ffn_conv_b, ffn_down):
    W = RWKV_WIDTH
    wi = w_in[l]
    w_r = wi[:, 0:RWKV_COLS]
    w_m = wi[:, RWKV_COLS:RWKV_COLS + 4 * MLSTM_WIDTH]
    w_mg = wi[:, RWKV_COLS + 4 * MLSTM_WIDTH:RWKV_COLS + MLSTM_COLS]
    w_s = wi[:, RWKV_COLS + MLSTM_COLS:]
    pad = jnp.zeros((D_MODEL, LANE - MLSTM_GATES), wi.dtype)
    w_packed = jnp.concatenate([w_r, w_mg, pad, w_m, w_s], axis=1).astype(BF16)

    head = jnp.arange(W, dtype=jnp.int32) // RWKV_HEAD_DIM
    same = (head[:, None] == head[None, :])
    gb = mlstm_gate_b[l].reshape(1, MLSTM_GATES)
    return dict(
        ada_w=ada_w[l], ada_b=ada_b[l], norm_g=norm_g[l], w_in=w_packed,
        mu=rwkv_mu[l].reshape(1, RWKV_COLS),
        w0=rwkv_w0[l].reshape(N_DIR, 1, W), w_up=rwkv_w_up[l],
        a0=rwkv_a0[l].reshape(N_DIR, 1, W), a_up=rwkv_a_up[l], g_up=rwkv_g_up[l],
        kk_scale=rwkv_kk_scale[l].reshape(1, W), k_a=rwkv_k_a[l].reshape(1, W),
        r_k=rwkv_r_k[l].reshape(1, W),
        lnx_g=rwkv_lnx_g[l].reshape(1, W), lnx_b=rwkv_lnx_b[l].reshape(1, W),
        pones=same.astype(BF16), pmean=(same.astype(F32) / RWKV_HEAD_DIM).astype(BF16),
        mlstm_conv=mlstm_conv[l].reshape(9, 2 * MLSTM_WIDTH),
        gate_bc=jnp.pad(gb, ((0, 0), (0, LANE - MLSTM_GATES))), gate_br=gb.reshape(MLSTM_GATES, 1),
        gn_g=mlstm_gn_g[l].reshape(1, MLSTM_WIDTH),
        w_br=w_branch_rwkv[l].astype(BF16), w_bm=w_branch_mlstm[l].astype(BF16),
        w_out=w_out[l].astype(BF16), ffn_up=ffn_up[l].astype(BF16),
        ffn_conv=ffn_conv[l].reshape(9, D_FF), ffn_conv_b=ffn_conv_b[l].reshape(1, D_FF),
        ffn_down=ffn_down[l].astype(BF16),
    )


def kernel(x_prompt, x_sample, c, state_rwkv, state_mlstm_C, state_mlstm_n, state_mlstm_m, c_ctx,
           ada_w, ada_b, norm_g, w_in, rwkv_mu, rwkv_w0, rwkv_w_up, rwkv_a0, rwkv_a_up, rwkv_g_up,
           rwkv_kk_scale, rwkv_k_a, rwkv_r_k, rwkv_lnx_g, rwkv_lnx_b, mlstm_conv, mlstm_gate_b,
           mlstm_gn_g, w_branch_rwkv, w_branch_mlstm, w_out, ffn_up, ffn_conv, ffn_conv_b, ffn_down):
    depth = ada_w.shape[0]
    batch = x_prompt.shape[0]
    dec_batch, dec_seq, _ = x_sample.shape
    latent_rows = dec_seq // GRID_W
    tiles_per_latent = dec_seq // TOK_TILE
    ctx_init = (jnp.zeros((batch, N_DIR, RWKV_HEADS, RWKV_HEAD_DIM, RWKV_HEAD_DIM), F32),
                jnp.zeros((batch, N_DIR, MLSTM_HEADS, MLSTM_HEAD_DIM, MLSTM_HEAD_DIM), F32),
                jnp.zeros((batch, N_DIR, MLSTM_HEADS, MLSTM_HEAD_DIM), F32),
                jnp.zeros((batch, N_DIR, MLSTM_HEADS), F32))
    cond = jnp.concatenate([c_ctx[None, :], c, jnp.zeros((8 - 1 - dec_batch, D_MODEL), F32)], axis=0)

    xp, xs = x_prompt, x_sample
    new_s, new_c, new_n, new_m = [], [], [], []
    for l in range(depth):
        p = _pack_layer(l, ada_w, ada_b, norm_g, w_in, rwkv_mu, rwkv_w0, rwkv_w_up, rwkv_a0, rwkv_a_up,
                        rwkv_g_up, rwkv_kk_scale, rwkv_k_a, rwkv_r_k, rwkv_lnx_g, rwkv_lnx_b, mlstm_conv,
                        mlstm_gate_b, mlstm_gn_g, w_branch_rwkv, w_branch_mlstm, w_out, ffn_up, ffn_conv,
                        ffn_conv_b, ffn_down)
        mod = _ada(cond, p["ada_w"], p["ada_b"]).reshape(8, 1, 6 * D_MODEL)
        xp, (s, cc, nn, mm) = _trunk(xp, mod, lambda i: 0, 1, ctx_init, p)
        new_s.append(s)
        new_c.append(cc)
        new_n.append(nn)
        new_m.append(mm)
        xs, _ = _trunk(xs, mod, lambda i: 1 + i // tiles_per_latent, latent_rows,
                       (state_rwkv[:, l], state_mlstm_C[:, l], state_mlstm_n[:, l], state_mlstm_m[:, l]), p)
    return (xp, xs, jnp.stack(new_s, axis=1), jnp.stack(new_c, axis=1),
            jnp.stack(new_n, axis=1), jnp.stack(new_m, axis=1))
```

```python
import functools

import jax
import jax.numpy as jnp
from jax import lax
from jax.experimental import pallas as pl
from jax.experimental.pallas import tpu as pltpu

F32 = jnp.float32
BF16 = jnp.bfloat16

D_MODEL = 1024
N_DIR = 2
RWKV_HEADS = 8
RWKV_HEAD_DIM = 64
RWKV_WIDTH = RWKV_HEADS * RWKV_HEAD_DIM
DECAY_LORA = 64
ICLR_LORA = 64
GATE_LORA = 128
MLSTM_HEADS = 4
MLSTM_HEAD_DIM = 128
MLSTM_WIDTH = MLSTM_HEADS * MLSTM_HEAD_DIM
MLSTM_CHUNK = 64
D_FF = 2816
GRID_W = 64
RMS_EPS = 1e-6
RWKV_GN_EPS = 64e-5
MLSTM_GN_EPS = 1e-5
DECAY_SCALE = 0.606531

RWKV_COLS = 3 * RWKV_WIDTH + N_DIR * DECAY_LORA + N_DIR * ICLR_LORA + GATE_LORA
MLSTM_GATES = 2 * N_DIR * MLSTM_HEADS
MLSTM_COLS = 4 * MLSTM_WIDTH + MLSTM_GATES
GATE_COLS = 2 * D_MODEL

LANE = 128
ZR_BLOCK = 2048
ZG_OFF = RWKV_COLS
ZM_OFF = ZR_BLOCK
ZS_OFF = ZM_OFF + 4 * MLSTM_WIDTH
Z_COLS = ZS_OFF + GATE_COLS

TOK_TILE = 256
RCHUNK = 64
CONV_CH_TILE = 256
VMEM_LIMIT = 56 * 1024 * 1024


def _params(sem):
    return pltpu.CompilerParams(dimension_semantics=sem, vmem_limit_bytes=VMEM_LIMIT)


def _resident(shape):
    nd = len(shape)
    return pl.BlockSpec(shape, lambda *_: (0,) * nd, pipeline_mode=pl.Buffered(1))


def _split2(a):
    hi = a.astype(BF16)
    lo = (a - hi.astype(F32)).astype(BF16)
    return hi, lo


def _split3(a):
    hi = a.astype(BF16)
    r1 = a - hi.astype(F32)
    mid = r1.astype(BF16)
    lo = (r1 - mid.astype(F32)).astype(BF16)
    return hi, mid, lo


def _dg(a, b, dims):
    return lax.dot_general(a, b, dims, preferred_element_type=F32)


def _mm(a, b, dims, passes):
    if passes == 1:
        return _dg(a.astype(BF16), b.astype(BF16), dims)
    ah, al = _split2(a)
    bh, bl = _split2(b)
    return _dg(ah, bh, dims) + (_dg(ah, bl, dims) + _dg(al, bh, dims))


def _mm_exact_lhs(a_bf16, b, dims):
    b1, b2, b3 = _split3(b)
    return _dg(a_bf16, b1, dims) + (_dg(a_bf16, b2, dims) + _dg(a_bf16, b3, dims))


def _mm_exact_rhs(a, b_bf16, dims):
    a1, a2, a3 = _split3(a)
    return _dg(a1, b_bf16, dims) + (_dg(a2, b_bf16, dims) + _dg(a3, b_bf16, dims))


_NN = (((1,), (0,)), ((), ()))
_NT = (((1,), (1,)), ((), ()))
_TN = (((0,), (0,)), ((), ()))
_BNN = (((2,), (1,)), ((0,), (0,)))
_BNT = (((2,), (2,)), ((0,), (0,)))
_BTN = (((1,), (1,)), ((0,), (0,)))


def _sigmoid(x):
    return jax.nn.sigmoid(x)


def _silu(x):
    return x * jax.nn.sigmoid(x)


def _rms(x, g):
    return x * lax.rsqrt(jnp.mean(x * x, axis=-1, keepdims=True) + RMS_EPS) * g


def _ada_kernel(cond_ref, w_ref, b_ref, o_ref):
    s = _silu(cond_ref[...])
    o_ref[...] = _dg(s.astype(BF16), w_ref[...].astype(BF16), _NN) + b_ref[...]


def _ada(cond8, ada_w, ada_b):
    n = ada_w.shape[1]
    tn = 1536
    return pl.pallas_call(
        _ada_kernel,
        grid=(n // tn,),
        in_specs=[_resident((8, D_MODEL)),
                  pl.BlockSpec((D_MODEL, tn), lambda j: (0, j)),
                  pl.BlockSpec((1, tn), lambda j: (0, j))],
        out_specs=pl.BlockSpec((8, tn), lambda j: (0, j)),
        out_shape=jax.ShapeDtypeStruct((8, n), F32),
        compiler_params=_params(("arbitrary",)),
        name="ada_mod",
    )(cond8, ada_w, ada_b.reshape(1, n))


def _in_kernel(x_ref, mod_ref, g_ref, wr_ref, wg_ref, wm_ref, ws_ref, z_ref):
    mod = mod_ref[0]
    sh = mod[:, 0:D_MODEL]
    sc = mod[:, D_MODEL:2 * D_MODEL]
    h = (_rms(x_ref[...], g_ref[...]) * (1.0 + sc) + sh).astype(BF16)
    z_ref[:, 0:ZG_OFF] = _dg(h, wr_ref[...], _NN)
    z_ref[:, ZG_OFF:ZM_OFF] = _dg(h, wg_ref[...], _NN)
    z_ref[:, ZM_OFF:ZS_OFF] = _dg(h, wm_ref[...], _NN)
    z_ref[:, ZS_OFF:Z_COLS] = _sigmoid(_dg(h, ws_ref[...], _NN))


def _in_proj(x2, mod, mod_row, norm_g0, w_r, w_g, w_m, w_s):
    n = x2.shape[0]
    return pl.pallas_call(
        _in_kernel,
        grid=(n // TOK_TILE,),
        in_specs=[pl.BlockSpec((TOK_TILE, D_MODEL), lambda i: (i, 0)),
                  pl.BlockSpec((1, 1, 6 * D_MODEL), lambda i: (mod_row(i), 0, 0)),
                  _resident((1, D_MODEL)),
                  _resident(w_r.shape), _resident(w_g.shape), _resident(w_m.shape), _resident(w_s.shape)],
        out_specs=pl.BlockSpec((TOK_TILE, Z_COLS), lambda i: (i, 0)),
        out_shape=jax.ShapeDtypeStruct((n, Z_COLS), F32),
        compiler_params=_params(("arbitrary",)),
        name="in_proj",
    )(x2, mod, norm_g0, w_r, w_g, w_m, w_s)


LOCAL_CHUNKS = 4
PAIR_LANES = 2 * RWKV_HEAD_DIM
RWKV_PAIRS = RWKV_HEADS // 2


def _bd(x):
    lane = lax.broadcasted_iota(jnp.int32, x.shape, 1)
    left = lane < RWKV_HEAD_DIM
    return jnp.concatenate([jnp.where(left, x, 0.0), jnp.where(left, 0.0, x)], axis=0)


def _rwkv_local_kernel(chunks_per_seq, passes,
                       z_ref, zp_ref, zn_ref, mu_ref, w0_ref, wup_ref, a0_ref, aup_ref, gup_ref,
                       kks_ref, ka_ref, rk_ref, pones_ref,
                       rp_ref, y0_ref, gm_ref, hm_ref, gate_ref, bonus_ref):
    C = RCHUNK
    W = RWKV_WIDTH
    NS = LOCAL_CHUNKS
    R = NS * C
    first = (pl.program_id(0) * NS) % chunks_per_seq
    has_prev = first != 0
    has_next = first + NS != chunks_per_seq

    z = z_ref[:, 0:RWKV_COLS]
    zp = jnp.where(has_prev, zp_ref[7:8, 0:RWKV_COLS], 0.0)
    zn = jnp.where(has_next, zn_ref[0:1, 0:RWKV_COLS], 0.0)
    trow = lax.broadcasted_iota(jnp.int32, (R, 1), 0)
    prev = jnp.where(trow == 0, zp, pltpu.roll(z, 1, 0))
    nxt = jnp.where(trow == R - 1, zn, pltpu.roll(z, R - 1, 0))
    zs = z + mu_ref[...] * (0.5 * (prev + nxt) - z)

    r = zs[:, 0:W]
    k = zs[:, W:2 * W]
    v = zs[:, 2 * W:3 * W]
    gd = zs[:, 3 * W + 2 * DECAY_LORA + 2 * ICLR_LORA:RWKV_COLS]
    gate_ref[...] = _dg(_sigmoid(gd).astype(BF16), gup_ref[...].astype(BF16), _NN)

    pones = pones_ref[...]
    kks = k * kks_ref[...]
    norm = jnp.sqrt(_mm_exact_rhs(kks * kks, pones, _NN))
    kk = kks / jnp.maximum(norm, 1e-12)

    P = PAIR_LANES
    row = lax.broadcasted_iota(jnp.int32, (R, R), 0)
    col = lax.broadcasted_iota(jnp.int32, (R, R), 1)
    same_chunk = jnp.bitwise_and(row, -C) == jnp.bitwise_and(col, -C)
    prow = lax.broadcasted_iota(jnp.int32, (C, P), 0)
    pcol = jnp.bitwise_and(lax.broadcasted_iota(jnp.int32, (C, P), 1), RWKV_HEAD_DIM - 1)
    eye_p = jnp.where(prow == pcol, 1.0, 0.0)
    brow = lax.broadcasted_iota(jnp.int32, (P, P), 0)
    bcol = lax.broadcasted_iota(jnp.int32, (P, P), 1)
    same_head = (brow < RWKV_HEAD_DIM) == (bcol < RWKV_HEAD_DIM)
    eye_b = jnp.where(brow == bcol, 1.0, 0.0)

    abar, rbar, kt, bt, kw, bw, wc, strict, incl = [], [], [], [], [], [], [], [], []
    bonus = None
    for d in range(N_DIR):
        o = 3 * W + d * DECAY_LORA
        wd = zs[:, o:o + DECAY_LORA]
        o = 3 * W + 2 * DECAY_LORA + d * ICLR_LORA
        ad = zs[:, o:o + ICLR_LORA]
        logw = -DECAY_SCALE * _sigmoid(w0_ref[d] + _dg(jnp.tanh(wd).astype(BF16), wup_ref[d].astype(BF16), _NN))
        a = _sigmoid(a0_ref[d] + _dg(ad.astype(BF16), aup_ref[d].astype(BF16), _NN))
        kd = k * (1.0 + (a - 1.0) * ka_ref[...])
        b = kk * a
        bonus_d = _mm_exact_rhs(r * kd * rk_ref[...], pones, _NN) * v
        bonus = bonus_d if bonus is None else bonus + bonus_d

        earlier_or_same = same_chunk & ((row >= col) if d == 0 else (row <= col))
        cum_i = _mm_exact_lhs(jnp.where(earlier_or_same, 1.0, 0.0).astype(BF16), logw, _NN)
        cum_e = cum_i - logw
        ab_d, rb_d, kt_d, bt_d, kw_d, bw_d, wc_d = [], [], [], [], [], [], []
        for s in range(NS):
            rs = slice(s * C, (s + 1) * C)
            ci_s = cum_i[rs]
            ctot = jnp.sum(logw[rs], axis=0, keepdims=True)
            e_ni = jnp.exp(-ci_s)
            e_ti = jnp.exp(ctot - ci_s)
            ab_d.append(kk[rs] * jnp.exp(cum_e[rs]))
            rb_d.append(r[rs] * jnp.exp(ci_s))
            kt_d.append(kd[rs] * e_ni)
            bt_d.append(b[rs] * e_ni)
            kw_d.append(kd[rs] * e_ti)
            bw_d.append(b[rs] * e_ti)
            wc_d.append(jnp.exp(ctot))
        abar.append(ab_d)
        rbar.append(rb_d)
        kt.append(kt_d)
        bt.append(bt_d)
        kw.append(kw_d)
        bw.append(bw_d)
        wc.append(wc_d)
        strict.append((prow > pcol) if d == 0 else (prow < pcol))
        incl.append((prow >= pcol) if d == 0 else (prow <= pcol))
    bonus_ref[...] = bonus

    mm = functools.partial(_mm, passes=passes)
    chains = [(s, d, p) for s in range(NS) for d in range(N_DIR) for p in range(RWKV_PAIRS)]
    nch = range(len(chains))

    def sel(arr, i):
        s, d, p = chains[i]
        return arr[d][s][:, p * P:(p + 1) * P]

    cat0 = lambda a_, b_: jnp.concatenate([a_, b_], axis=0)
    cat1 = lambda a_, b_: jnp.concatenate([a_, b_], axis=1)
    vsl = [v[s * C:(s + 1) * C, p * P:(p + 1) * P] for s, _, p in chains]
    lhs = [cat0(sel(abar, i), sel(rbar, i)) for i in nch]
    by_b = [mm(lhs[i], _bd(sel(bt, i)), _NT) for i in nch]
    by_k = [mm(lhs[i], _bd(sel(kt, i)), _NT) for i in nch]
    a_kk = [jnp.where(strict[chains[i][1]], by_b[i][0:C], 0.0) for i in nch]
    a_rb = [jnp.where(incl[chains[i][1]], by_b[i][C:2 * C], 0.0) for i in nch]
    a_kv = [jnp.where(strict[chains[i][1]], by_k[i][0:C], 0.0) for i in nch]
    a_rk = [jnp.where(incl[chains[i][1]], by_k[i][C:2 * C], 0.0) for i in nch]
    on_v = [mm(cat0(a_kv[i], a_rk[i]), _bd(vsl[i]), _NN) for i in nch]

    x = [-m for m in a_kk]
    tinv = [eye_p + m for m in x]
    x = [mm(m, _bd(m), _NN) for m in x]
    for _ in range(4):
        both = [mm(cat0(tinv[i], x[i]), _bd(x[i]), _NN) for i in nch]
        tinv = [tinv[i] + both[i][0:C] for i in nch]
        x = [m[C:2 * C] for m in both]
    tinv = [tinv[i] + mm(tinv[i], _bd(x[i]), _NN) for i in nch]

    solved = [mm(tinv[i], cat1(_bd(sel(abar, i)), _bd(on_v[i][0:C])), _NN) for i in nch]
    ap = [m[:, 0:P] for m in solved]
    u0 = [m[:, P:2 * P] for m in solved]
    corr = [mm(a_rb[i], cat1(_bd(ap[i]), _bd(u0[i])), _NN) for i in nch]
    on_b = [mm(cat1(ap[i], u0[i]), sel(bw, i), _TN) for i in nch]
    vk = [mm(vsl[i], sel(kw, i), _TN) for i in nch]
    for i in nch:
        s, d, p = chains[i]
        rows = slice(s * C, (s + 1) * C)
        lanes = slice(p * P, (p + 1) * P)
        rp_ref[d, rows, lanes] = sel(rbar, i) - corr[i][:, 0:P]
        y0_ref[d, rows, lanes] = on_v[i][C:2 * C] - corr[i][:, P:2 * P]
        gm_ref[d, s, p] = eye_b * sel(wc, i) - jnp.where(same_head, on_b[i][0:P], 0.0)
        hm_ref[d, s, p] = jnp.where(same_head, vk[i] - on_b[i][P:2 * P], 0.0)


def _rwkv_local(z, seq_len, p, passes):
    n = z.shape[0]
    nchunk = n // RCHUNK
    cps = seq_len // RCHUNK
    assert cps % LOCAL_CHUNKS == 0
    W = RWKV_WIDTH
    rows = LOCAL_CHUNKS * RCHUNK
    hb = rows // 8
    last8 = n // 8 - 1
    mat = jax.ShapeDtypeStruct((N_DIR, nchunk, RWKV_PAIRS, PAIR_LANES, PAIR_LANES), F32)
    mat_spec = pl.BlockSpec((N_DIR, LOCAL_CHUNKS, RWKV_PAIRS, PAIR_LANES, PAIR_LANES),
                            lambda c: (0, c, 0, 0, 0))
    tok = jax.ShapeDtypeStruct((N_DIR, n, W), F32)
    tok_spec = pl.BlockSpec((N_DIR, rows, W), lambda c: (0, c, 0))
    row_spec = pl.BlockSpec((rows, W), lambda c: (c, 0))
    return pl.pallas_call(
        functools.partial(_rwkv_local_kernel, cps, passes),
        grid=(nchunk // LOCAL_CHUNKS,),
        in_specs=[pl.BlockSpec((rows, ZR_BLOCK), lambda c: (c, 0)),
                  pl.BlockSpec((8, ZR_BLOCK), lambda c: (jnp.maximum(c * hb - 1, 0), 0)),
                  pl.BlockSpec((8, ZR_BLOCK), lambda c: (jnp.minimum((c + 1) * hb, last8), 0)),
                  _resident((1, RWKV_COLS)),
                  _resident((N_DIR, 1, W)), _resident((N_DIR, DECAY_LORA, W)),
                  _resident((N_DIR, 1, W)), _resident((N_DIR, ICLR_LORA, W)),
                  _resident((GATE_LORA, W)),
                  _resident((1, W)), _resident((1, W)), _resident((1, W)),
                  _resident((W, W))],
        out_specs=[tok_spec, tok_spec, mat_spec, mat_spec, row_spec, row_spec],
        out_shape=[tok, tok, mat, mat,
                   jax.ShapeDtypeStruct((n, W), F32), jax.ShapeDtypeStruct((n, W), F32)],
        compiler_params=_params(("arbitrary",)),
        name="rwkv_local",
    )(z, z, z, p["mu"], p["w0"], p["w_up"], p["a0"], p["a_up"], p["g_up"],
      p["kk_scale"], p["k_a"], p["r_k"], p["pones"])


SCAN_CHUNKS = 4


def _rwkv_scan_kernel(passes, s0_ref, rpf_ref, rpb_ref, y0f_ref, y0b_ref, gmf_ref, gmb_ref, hmf_ref, hmb_ref,
                      ysf_ref, ysb_ref, sout_ref, s_scr):
    @pl.when(pl.program_id(1) == 0)
    def _():
        s_scr[...] = s0_ref[0]

    K = SCAN_CHUNKS
    C = RCHUNK
    rp_ref, y0_ref, gm_ref, hm_ref, ys_ref = ((rpf_ref, rpb_ref), (y0f_ref, y0b_ref), (gmf_ref, gmb_ref),
                                              (hmf_ref, hmb_ref), (ysf_ref, ysb_ref))
    chains = [(d, p) for d in range(N_DIR) for p in range(RWKV_PAIRS)]
    lanes = [slice(p * PAIR_LANES, (p + 1) * PAIR_LANES) for _, p in chains]
    nch = range(len(chains))
    s = [s_scr[d, p] for d, p in chains]
    for j in range(K):
        at = (j, K - 1 - j)
        rows = [slice(at[d] * C, (at[d] + 1) * C) for d, _ in chains]
        y = [_mm(rp_ref[chains[i][0]][0, rows[i], lanes[i]], s[i], _NT, passes) for i in nch]
        sg = [_mm(s[i], gm_ref[chains[i][0]][0, at[chains[i][0]], chains[i][1]], _NN, passes) for i in nch]
        for i in nch:
            d, p = chains[i]
            ys_ref[d][rows[i], lanes[i]] = y[i] + y0_ref[d][0, rows[i], lanes[i]]
        s = [sg[i] + hm_ref[chains[i][0]][0, at[chains[i][0]], chains[i][1]] for i in nch]
    for i in nch:
        d, p = chains[i]
        s_scr[d, p] = s[i]
        sout_ref[0, d, p] = s[i]


def _rwkv_scan(s0, rp, y0, gm, hm, batch, seq_len, passes):
    K = SCAN_CHUNKS
    spb = seq_len // (RCHUNK * K)
    n = batch * seq_len

    def fwd(b, s):
        return b * spb + s

    def bwd(b, s):
        return b * spb + spb - 1 - s

    def mat_spec(d, at):
        return pl.BlockSpec((1, K, RWKV_PAIRS, PAIR_LANES, PAIR_LANES), lambda b, s: (d, at(b, s), 0, 0, 0))

    def tok_spec(d, at):
        return pl.BlockSpec((1, K * RCHUNK, RWKV_WIDTH), lambda b, s: (d, at(b, s), 0))

    st_spec = pl.BlockSpec((1, N_DIR, RWKV_PAIRS, PAIR_LANES, PAIR_LANES), lambda b, s: (b, 0, 0, 0, 0))
    ys = jax.ShapeDtypeStruct((n, RWKV_WIDTH), F32)
    return pl.pallas_call(
        functools.partial(_rwkv_scan_kernel, passes),
        grid=(batch, spb),
        in_specs=[st_spec, tok_spec(0, fwd), tok_spec(1, bwd), tok_spec(0, fwd), tok_spec(1, bwd),
                  mat_spec(0, fwd), mat_spec(1, bwd), mat_spec(0, fwd), mat_spec(1, bwd)],
        out_specs=[pl.BlockSpec((K * RCHUNK, RWKV_WIDTH), lambda b, s: (fwd(b, s), 0)),
                   pl.BlockSpec((K * RCHUNK, RWKV_WIDTH), lambda b, s: (bwd(b, s), 0)),
                   st_spec],
        out_shape=[ys, ys,
                   jax.ShapeDtypeStruct((batch, N_DIR, RWKV_PAIRS, PAIR_LANES, PAIR_LANES), F32)],
        scratch_shapes=[pltpu.VMEM((N_DIR, RWKV_PAIRS, PAIR_LANES, PAIR_LANES), F32)],
        compiler_params=_params(("arbitrary", "arbitrary")),
        name="rwkv_scan",
    )(s0, rp, rp, y0, y0, gm, gm, hm, hm)


CONV_BLOCK_ROWS = 2048


def _dwconv(x, w_ref, width, vertical):
    T = x.shape[0]
    t = lax.broadcasted_iota(jnp.int32, (T, 1), 0)
    assert width & (width - 1) == 0
    colp = jnp.bitwise_and(t, width - 1)
    xl = jnp.where(colp == 0, 0.0, pltpu.roll(x, 1, 0))
    xr = jnp.where(colp == width - 1, 0.0, pltpu.roll(x, T - 1, 0))

    def tap_row(i):
        return w_ref[3 * i:3 * i + 1, :] * xl + w_ref[3 * i + 1:3 * i + 2, :] * x + w_ref[3 * i + 2:3 * i + 3, :] * xr

    out = tap_row(1)
    if vertical:
        out = out + jnp.where(t < width, 0.0, pltpu.roll(tap_row(0), width, 0))
        out = out + jnp.where(t >= T - width, 0.0, pltpu.roll(tap_row(2), T - width, 0))
    return out


def _conv_geometry(n, seq_len, rows):
    if rows > 1:
        return seq_len, seq_len // rows, True
    block = CONV_BLOCK_ROWS if (n % CONV_BLOCK_ROWS == 0 and CONV_BLOCK_ROWS % seq_len == 0) else seq_len
    return block, seq_len, False


def _qk_conv_kernel(width, vertical, x_ref, w_ref, o_ref):
    o_ref[...] = _silu(_dwconv(x_ref[...], w_ref, width, vertical))


def _qk_conv(z, batch, seq_len, rows, conv_w9):
    n = batch * seq_len
    ch = 2 * MLSTM_WIDTH
    tc = CONV_CH_TILE
    off = ZM_OFF // tc
    block, width, vertical = _conv_geometry(n, seq_len, rows)
    return pl.pallas_call(
        functools.partial(_qk_conv_kernel, width, vertical),
        grid=(n // block, ch // tc),
        in_specs=[pl.BlockSpec((block, tc), lambda b, j: (b, off + j)),
                  pl.BlockSpec((9, tc), lambda b, j: (0, j))],
        out_specs=pl.BlockSpec((block, tc), lambda b, j: (b, j)),
        out_shape=jax.ShapeDtypeStruct((n, ch), F32),
        compiler_params=_params(("arbitrary", "arbitrary")),
        name="mlstm_qk_conv",
    )(z, conv_w9)


def _ffn_conv_kernel(width, vertical, ua_ref, uv_ref, w_ref, b_ref, o_ref):
    act = _dwconv(ua_ref[...], w_ref, width, vertical) + b_ref[...]
    o_ref[...] = (_silu(act) * uv_ref[...]).astype(BF16)


def _ffn_conv(u, batch, seq_len, rows, conv_w9, conv_b):
    n = batch * seq_len
    tc = CONV_CH_TILE
    nct = D_FF // tc
    block, width, vertical = _conv_geometry(n, seq_len, rows)
    return pl.pallas_call(
        functools.partial(_ffn_conv_kernel, width, vertical),
        grid=(n // block, nct),
        in_specs=[pl.BlockSpec((block, tc), lambda b, j: (b, j)),
                  pl.BlockSpec((block, tc), lambda b, j: (b, nct + j)),
                  pl.BlockSpec((9, tc), lambda b, j: (0, j)),
                  pl.BlockSpec((1, tc), lambda b, j: (0, j))],
        out_specs=pl.BlockSpec((block, tc), lambda b, j: (b, j)),
        out_shape=jax.ShapeDtypeStruct((n, D_FF), BF16),
        compiler_params=_params(("arbitrary", "arbitrary")),
        name="ffn_conv",
    )(u, u, conv_w9, conv_b)


def _mlstm_scan_kernel(qkf_ref, qkb_ref, vf_ref, vb_ref, gcf_ref, gcb_ref, grf_ref, grb_ref,
                       gbc_ref, gbr_ref, c0_ref, n0_ref, m0_ref,
                       hf_ref, hb_ref, cout_ref, nout_ref, mout_ref,
                       c_scr, n_scr, m_scr):
    step = pl.program_id(1)
    L = MLSTM_CHUNK
    dh = MLSTM_HEAD_DIM
    H = MLSTM_HEADS

    @pl.when(step == 0)
    def _():
        c_scr[...] = c0_ref[0]
        n_scr[...] = n0_ref[0]
        m_scr[...] = m0_ref[0]

    row = lax.broadcasted_iota(jnp.int32, (L, L), 0)
    col = lax.broadcasted_iota(jnp.int32, (L, L), 1)
    lower = (row >= col)
    upper = (row <= col)
    lower_b = jnp.where(lower, 1.0, 0.0).astype(BF16)
    upper_b = jnp.where(upper, 1.0, 0.0).astype(BF16)
    neg_inf = jnp.full((), -jnp.inf, F32)

    gcol, grow, bcol, brow = [], [], [], []
    for d in range(N_DIR):
        gc_ref, gr_ref = (gcf_ref, grf_ref) if d == 0 else (gcb_ref, grb_ref)
        gcol.append(gc_ref[...] + gbc_ref[...])
        grow.append(gr_ref[0] + gbr_ref[...])
        bcol.append(_mm_exact_lhs(lower_b if d == 0 else upper_b, jax.nn.log_sigmoid(gcol[d]), _NN))
        brow.append(_mm_exact_rhs(jax.nn.log_sigmoid(grow[d]), upper_b if d == 0 else lower_b, _NN))

    chains = [(d, h) for d in range(N_DIR) for h in range(H)]
    nch = range(len(chains))
    q, k, v, kb, qb = [], [], [], [], []
    ic_col, ic_row, b_col, b_row, m_prev, b_last = [], [], [], [], [], []
    for d, h in chains:
        j = d * H + h
        gi, gf = j, 2 * H + j
        qk_ref, v_ref = (qkf_ref, vf_ref) if d == 0 else (qkb_ref, vb_ref)
        q.append(qk_ref[:, h * dh:(h + 1) * dh] * (dh ** -0.5))
        k.append(qk_ref[:, MLSTM_WIDTH + h * dh:MLSTM_WIDTH + (h + 1) * dh])
        v.append(v_ref[:, h * dh:(h + 1) * dh])
        qb.append(q[-1].astype(BF16))
        kb.append(k[-1].astype(BF16))
        ic_col.append(gcol[d][:, gi:gi + 1])
        ic_row.append(grow[d][gi:gi + 1, :])
        b_col.append(bcol[d][:, gf:gf + 1])
        b_row.append(brow[d][gf:gf + 1, :])
        last = L - 1 if d == 0 else 0
        b_last.append(b_col[-1][last:last + 1, :])
        m_prev.append(m_scr[j:j + 1, 0:1])

    qk_t = [_dg(qb[i], kb[i], _NT) for i in nch]
    q_c = [_dg(qb[i], c_scr[i].astype(BF16), _NT) for i in nch]
    log_d = [jnp.where(lower if chains[i][0] == 0 else upper, b_col[i] - b_row[i] + ic_row[i], neg_inf)
             for i in nch]
    log_inter = [b_col[i] + m_prev[i] for i in nch]
    m_s = [jnp.maximum(log_inter[i], jnp.max(log_d[i], axis=-1, keepdims=True)) for i in nch]
    s = [qk_t[i] * jnp.exp(log_d[i] - m_s[i]) for i in nch]
    s_v = [_dg(s[i].astype(BF16), v[i].astype(BF16), _NN) for i in nch]
    m_new = [jnp.maximum(b_last[i] + m_prev[i],
                         jnp.max(b_last[i] - b_row[i] + ic_row[i], axis=-1, keepdims=True)) for i in nch]
    wj = [jnp.exp(b_last[i] - b_col[i] + ic_col[i] - m_new[i]) for i in nch]
    wv_k = [_dg((wj[i] * v[i]).astype(BF16), kb[i], _TN) for i in nch]
    for i in nch:
        d, h = chains[i]
        h_ref = hf_ref if d == 0 else hb_ref
        inter = jnp.exp(log_inter[i] - m_s[i])
        n_prev = n_scr[i:i + 1, :]
        num = inter * q_c[i] + s_v[i]
        den = inter * jnp.sum(q[i] * n_prev, axis=-1, keepdims=True) + jnp.sum(s[i], axis=-1, keepdims=True)
        h_ref[:, h * dh:(h + 1) * dh] = num / jnp.maximum(jnp.abs(den), jnp.exp(-m_s[i]))
        carry = jnp.exp(b_last[i] + m_prev[i] - m_new[i])
        c_scr[i] = carry * c_scr[i] + wv_k[i]
        n_scr[i:i + 1, :] = carry * n_prev + jnp.sum(wj[i] * k[i], axis=0, keepdims=True)
        m_scr[i:i + 1, :] = jnp.broadcast_to(m_new[i], (1, LANE))

    cout_ref[0] = c_scr[...]
    nout_ref[0] = n_scr[...]
    mout_ref[0] = m_scr[...]


def _mlstm_scan(z, qk, gt, gate_bc, gate_br, c0, n0, m0, batch, seq_len):
    L = MLSTM_CHUNK
    cps = seq_len // L
    n = batch * seq_len
    W = MLSTM_WIDTH
    nst = N_DIR * MLSTM_HEADS
    dh = MLSTM_HEAD_DIM

    def fw(b, c):
        return b * cps + c

    def bw(b, c):
        return b * cps + cps - 1 - c

    vblk = (ZM_OFF + 2 * W) // W
    gblk = ZG_OFF // LANE
    return pl.pallas_call(
        _mlstm_scan_kernel,
        grid=(batch, cps),
        in_specs=[pl.BlockSpec((L, 2 * W), lambda b, c: (fw(b, c), 0)),
                  pl.BlockSpec((L, 2 * W), lambda b, c: (bw(b, c), 0)),
                  pl.BlockSpec((L, W), lambda b, c: (fw(b, c), vblk)),
                  pl.BlockSpec((L, W), lambda b, c: (bw(b, c), vblk)),
                  pl.BlockSpec((L, LANE), lambda b, c: (fw(b, c), gblk)),
                  pl.BlockSpec((L, LANE), lambda b, c: (bw(b, c), gblk)),
                  pl.BlockSpec((1, MLSTM_GATES, L), lambda b, c: (fw(b, c), 0, 0)),
                  pl.BlockSpec((1, MLSTM_GATES, L), lambda b, c: (bw(b, c), 0, 0)),
                  _resident((1, LANE)),
                  _resident((MLSTM_GATES, 1)),
                  pl.BlockSpec((1, nst, dh, dh), lambda b, c: (b, 0, 0, 0)),
                  pl.BlockSpec((1, nst, dh), lambda b, c: (b, 0, 0)),
                  pl.BlockSpec((1, nst, LANE), lambda b, c: (b, 0, 0))],
        out_specs=[pl.BlockSpec((L, W), lambda b, c: (fw(b, c), 0)),
                   pl.BlockSpec((L, W), lambda b, c: (bw(b, c), 0)),
                   pl.BlockSpec((1, nst, dh, dh), lambda b, c: (b, 0, 0, 0)),
                   pl.BlockSpec((1, nst, dh), lambda b, c: (b, 0, 0)),
                   pl.BlockSpec((1, nst, LANE), lambda b, c: (b, 0, 0))],
        out_shape=[jax.ShapeDtypeStruct((n, W), F32), jax.ShapeDtypeStruct((n, W), F32),
                   jax.ShapeDtypeStruct((batch, nst, dh, dh), F32),
                   jax.ShapeDtypeStruct((batch, nst, dh), F32),
                   jax.ShapeDtypeStruct((batch, nst, LANE), F32)],
        scratch_shapes=[pltpu.VMEM((nst, dh, dh), F32), pltpu.VMEM((nst, dh), F32),
                        pltpu.VMEM((nst, LANE), F32)],
        compiler_params=_params(("arbitrary", "arbitrary")),
        name="mlstm_scan",
    )(qk, qk, z, z, z, z, gt, gt, gate_bc, gate_br, c0, n0, m0)


def _merge_kernel(x_ref, mod_ref, ysf_ref, ysb_ref, bonus_ref, gate_ref, hf_ref, hb_ref, zo_ref, zs_ref,
                  lnxg_ref, lnxb_ref, gng_ref, pmean_ref, wbr_ref, wbm_ref, wout_ref, ng_ref, wup_ref,
                  x1_ref, u_ref):
    mod = mod_ref[0]
    g1 = mod[:, 2 * D_MODEL:3 * D_MODEL]
    sh2 = mod[:, 3 * D_MODEL:4 * D_MODEL]
    sc2 = mod[:, 4 * D_MODEL:5 * D_MODEL]

    ys = ysf_ref[...] + ysb_ref[...]
    pmean = pmean_ref[...]
    mean = _mm_exact_rhs(ys, pmean, _NN)
    cen = ys - mean
    var = _mm_exact_rhs(cen * cen, pmean, _NN)
    y_r = (cen * lax.rsqrt(var + RWKV_GN_EPS) * lnxg_ref[...] + lnxb_ref[...] + bonus_ref[...]) * gate_ref[...]

    hs = hf_ref[...] + hb_ref[...]
    parts = []
    for h in range(MLSTM_HEADS):
        hh = hs[:, h * MLSTM_HEAD_DIM:(h + 1) * MLSTM_HEAD_DIM]
        mu = jnp.mean(hh, axis=-1, keepdims=True)
        ce = hh - mu
        va = jnp.mean(ce * ce, axis=-1, keepdims=True)
        parts.append(ce * lax.rsqrt(va + MLSTM_GN_EPS))
    y_m = jnp.concatenate(parts, axis=1) * gng_ref[...] * _sigmoid(zo_ref[...])

    gates = zs_ref[...]
    merged = (gates[:, 0:D_MODEL] * _dg(y_r.astype(BF16), wbr_ref[...], _NN)
              + gates[:, D_MODEL:2 * D_MODEL] * _dg(y_m.astype(BF16), wbm_ref[...], _NN))
    t = _dg(merged.astype(BF16), wout_ref[...], _NN)
    x1 = x_ref[...] + g1 * _rms(t, ng_ref[1:2, :])
    x1_ref[...] = x1
    h2 = _rms(x1, ng_ref[2:3, :]) * (1.0 + sc2) + sh2
    u_ref[...] = _dg(h2.astype(BF16), wup_ref[...], _NN)


def _merge(x2, mod, mod_row, z, ysf, ysb, bonus, gate, hf, hb, p):
    n = x2.shape[0]
    W = RWKV_WIDTH
    tile = lambda w: pl.BlockSpec((TOK_TILE, w), lambda i: (i, 0))
    return pl.pallas_call(
        _merge_kernel,
        grid=(n // TOK_TILE,),
        in_specs=[tile(D_MODEL),
                  pl.BlockSpec((1, 1, 6 * D_MODEL), lambda i: (mod_row(i), 0, 0)),
                  tile(W), tile(W), tile(W), tile(W), tile(MLSTM_WIDTH), tile(MLSTM_WIDTH),
                  pl.BlockSpec((TOK_TILE, MLSTM_WIDTH), lambda i: (i, (ZM_OFF + 3 * MLSTM_WIDTH) // MLSTM_WIDTH)),
                  pl.BlockSpec((TOK_TILE, GATE_COLS), lambda i: (i, ZS_OFF // GATE_COLS)),
                  _resident((1, W)), _resident((1, W)), _resident((1, MLSTM_WIDTH)),
                  _resident((W, W)),
                  _resident((W, D_MODEL)), _resident((MLSTM_WIDTH, D_MODEL)),
                  _resident((D_MODEL, D_MODEL)), _resident((4, D_MODEL)),
                  _resident((D_MODEL, 2 * D_FF))],
        out_specs=[tile(D_MODEL), tile(2 * D_FF)],
        out_shape=[jax.ShapeDtypeStruct((n, D_MODEL), F32), jax.ShapeDtypeStruct((n, 2 * D_FF), F32)],
        compiler_params=_params(("arbitrary",)),
        name="merge_ffn_up",
    )(x2, mod, ysf, ysb, bonus, gate, hf, hb, z, z, p["lnx_g"], p["lnx_b"], p["gn_g"], p["pmean"],
      p["w_br"], p["w_bm"], p["w_out"], p["norm_g"], p["ffn_up"])


def _down_kernel(x1_ref, mod_ref, a_ref, w_ref, ng_ref, o_ref):
    g2 = mod_ref[0][:, 5 * D_MODEL:6 * D_MODEL]
    f = _dg(a_ref[...], w_ref[...], _NN)
    o_ref[...] = x1_ref[...] + g2 * _rms(f, ng_ref[3:4, :])


def _down(x1, mod, mod_row, act, p):
    n = x1.shape[0]
    return pl.pallas_call(
        _down_kernel,
        grid=(n // TOK_TILE,),
        in_specs=[pl.BlockSpec((TOK_TILE, D_MODEL), lambda i: (i, 0)),
                  pl.BlockSpec((1, 1, 6 * D_MODEL), lambda i: (mod_row(i), 0, 0)),
                  pl.BlockSpec((TOK_TILE, D_FF), lambda i: (i, 0)),
                  _resident((D_FF, D_MODEL)), _resident((4, D_MODEL))],
        out_specs=pl.BlockSpec((TOK_TILE, D_MODEL), lambda i: (i, 0)),
        out_shape=jax.ShapeDtypeStruct((n, D_MODEL), F32),
        compiler_params=_params(("arbitrary",)),
        name="ffn_down",
    )(x1, mod, act, p["ffn_down"], p["norm_g"])


RWKV_LOCAL_PASSES = 1
RWKV_SCAN_PASSES = 3


def _state_to_pairs(s):
    b = s.shape[0]
    s = s.reshape(b, N_DIR, RWKV_PAIRS, 2, RWKV_HEAD_DIM, RWKV_HEAD_DIM)
    zero = jnp.zeros_like(s[:, :, :, 0])
    top = jnp.concatenate([s[:, :, :, 0], zero], axis=-1)
    bot = jnp.concatenate([zero, s[:, :, :, 1]], axis=-1)
    return jnp.concatenate([top, bot], axis=-2)


def _state_from_pairs(sb):
    b = sb.shape[0]
    n = RWKV_HEAD_DIM
    parts = jnp.stack([sb[..., 0:n, 0:n], sb[..., n:2 * n, n:2 * n]], axis=3)
    return parts.reshape(b, N_DIR, RWKV_HEADS, n, n)


def _trunk(x, mod, mod_row, rows, states, p):
    batch, seq_len, _ = x.shape
    n = batch * seq_len
    x2 = x.reshape(n, D_MODEL)
    s0, c0, n0, m0 = states

    z = _in_proj(x2, mod, mod_row, p["norm_g"][0:1], p["w_r"], p["w_g"], p["w_m"], p["w_s"])

    rp, y0, gm, hm, gate, bonus = _rwkv_local(z, seq_len, p, RWKV_LOCAL_PASSES)
    ysf, ysb, s_fin = _rwkv_scan(_state_to_pairs(s0), rp, y0, gm, hm, batch, seq_len, RWKV_SCAN_PASSES)
    s_fin = _state_from_pairs(s_fin)

    qk = _qk_conv(z, batch, seq_len, rows, p["mlstm_conv"])
    gt = z[:, ZG_OFF:ZG_OFF + MLSTM_GATES].reshape(n // MLSTM_CHUNK, MLSTM_CHUNK, MLSTM_GATES).transpose(0, 2, 1)
    nst = N_DIR * MLSTM_HEADS
    hf, hb, c_fin, n_fin, m_fin = _mlstm_scan(
        z, qk, gt, p["gate_bc"], p["gate_br"],
        c0.reshape(batch, nst, MLSTM_HEAD_DIM, MLSTM_HEAD_DIM), n0.reshape(batch, nst, MLSTM_HEAD_DIM),
        jnp.broadcast_to(m0.reshape(batch, nst, 1), (batch, nst, LANE)), batch, seq_len)

    x1, u = _merge(x2, mod, mod_row, z, ysf, ysb, bonus, gate, hf, hb, p)
    act = _ffn_conv(u, batch, seq_len, rows, p["ffn_conv"], p["ffn_conv_b"])
    out = _down(x1, mod, mod_row, act, p)

    new_states = (s_fin,
                  c_fin.reshape(batch, N_DIR, MLSTM_HEADS, MLSTM_HEAD_DIM, MLSTM_HEAD_DIM),
                  n_fin.reshape(batch, N_DIR, MLSTM_HEADS, MLSTM_HEAD_DIM),
                  m_fin[:, :, 0].reshape(batch, N_DIR, MLSTM_HEADS))
    return out.reshape(batch, seq_len, D_MODEL), new_states


def _pack_layer(l, ada_w, ada_b, norm_g, w_in, rwkv_mu, rwkv_w0, rwkv_w_up, rwkv_a0, rwkv_a_up,
                rwkv_g_up, rwkv_kk_scale, rwkv_k_a, rwkv_r_k, rwkv_lnx_g, rwkv_lnx_b, mlstm_conv,
                mlstm_gate_b, mlstm_gn_g, w_branch_rwkv, w_branch_mlstm, w_out, ffn_up, ffn_conv,
                ffn_conv_b, ffn_down):
    W = RWKV_WIDTH
    wi = w_in[l]
    w_r = wi[:, 0:RWKV_COLS].astype(BF16)
    w_m = wi[:, RWKV_COLS:RWKV_COLS + 4 * MLSTM_WIDTH].astype(BF16)
    w_g = jnp.pad(wi[:, RWKV_COLS + 4 * MLSTM_WIDTH:RWKV_COLS + MLSTM_COLS].astype(BF16),
                  ((0, 0), (0, LANE - MLSTM_GATES)))
    w_s = wi[:, RWKV_COLS + MLSTM_COLS:].astype(BF16)

    head = jnp.arange(W, dtype=jnp.int32) // RWKV_HEAD_DIM
    same = (head[:, None] == head[None, :])
    gb = mlstm_gate_b[l].reshape(1, MLSTM_GATES)
    return dict(
        ada_w=ada_w[l], ada_b=ada_b[l], norm_g=norm_g[l], w_r=w_r, w_g=w_g, w_m=w_m, w_s=w_s,
        mu=rwkv_mu[l].reshape(1, RWKV_COLS),
        w0=rwkv_w0[l].reshape(N_DIR, 1, W), w_up=rwkv_w_up[l],
        a0=rwkv_a0[l].reshape(N_DIR, 1, W), a_up=rwkv_a_up[l], g_up=rwkv_g_up[l],
        kk_scale=rwkv_kk_scale[l].reshape(1, W), k_a=rwkv_k_a[l].reshape(1, W),
        r_k=rwkv_r_k[l].reshape(1, W),
        lnx_g=rwkv_lnx_g[l].reshape(1, W), lnx_b=rwkv_lnx_b[l].reshape(1, W),
        pones=same.astype(BF16), pmean=(same.astype(F32) / RWKV_HEAD_DIM).astype(BF16),
        mlstm_conv=mlstm_conv[l].reshape(9, 2 * MLSTM_WIDTH),
        gate_bc=jnp.pad(gb, ((0, 0), (0, LANE - MLSTM_GATES))), gate_br=gb.reshape(MLSTM_GATES, 1),
        gn_g=mlstm_gn_g[l].reshape(1, MLSTM_WIDTH),
        w_br=w_branch_rwkv[l].astype(BF16), w_bm=w_branch_mlstm[l].astype(BF16),
        w_out=w_out[l].astype(BF16), ffn_up=ffn_up[l].astype(BF16),
        ffn_conv=ffn_conv[l].reshape(9, D_FF), ffn_conv_b=ffn_conv_b[l].reshape(1, D_FF),
        ffn_down=ffn_down[l].astype(BF16),
    )


def kernel(x_prompt, x_sample, c, state_rwkv, state_mlstm_C, state_mlstm_n, state_mlstm_m, c_ctx,
           ada_w, ada_b, norm_g, w_in, rwkv_mu, rwkv_w0, rwkv_w_up, rwkv_a0, rwkv_a_up, rwkv_g_up,
           rwkv_kk_scale, rwkv_k_a, rwkv_r_k, rwkv_lnx_g, rwkv_lnx_b, mlstm_conv, mlstm_gate_b,
           mlstm_gn_g, w_branch_rwkv, w_branch_mlstm, w_out, ffn_up, ffn_conv, ffn_conv_b, ffn_down):
    depth = ada_w.shape[0]
    batch = x_prompt.shape[0]
    dec_batch, dec_seq, _ = x_sample.shape
    latent_rows = dec_seq // GRID_W
    tiles_per_latent = dec_seq // TOK_TILE
    ctx_init = (jnp.zeros((batch, N_DIR, RWKV_HEADS, RWKV_HEAD_DIM, RWKV_HEAD_DIM), F32),
                jnp.zeros((batch, N_DIR, MLSTM_HEADS, MLSTM_HEAD_DIM, MLSTM_HEAD_DIM), F32),
                jnp.zeros((batch, N_DIR, MLSTM_HEADS, MLSTM_HEAD_DIM), F32),
                jnp.zeros((batch, N_DIR, MLSTM_HEADS), F32))
    cond = jnp.concatenate([c_ctx[None, :], c, jnp.zeros((8 - 1 - dec_batch, D_MODEL), F32)], axis=0)

    xp, xs = x_prompt, x_sample
    new_s, new_c, new_n, new_m = [], [], [], []
    for l in range(depth):
        p = _pack_layer(l, ada_w, ada_b, norm_g, w_in, rwkv_mu, rwkv_w0, rwkv_w_up, rwkv_a0, rwkv_a_up,
                        rwkv_g_up, rwkv_kk_scale, rwkv_k_a, rwkv_r_k, rwkv_lnx_g, rwkv_lnx_b, mlstm_conv,
                        mlstm_gate_b, mlstm_gn_g, w_branch_rwkv, w_branch_mlstm, w_out, ffn_up, ffn_conv,
                        ffn_conv_b, ffn_down)
        mod = _ada(cond, p["ada_w"], p["ada_b"]).reshape(8, 1, 6 * D_MODEL)
        xp, (s, cc, nn, mm) = _trunk(xp, mod, lambda i: 0, 1, ctx_init, p)
        new_s.append(s)
        new_c.append(cc)
        new_n.append(nn)
        new_m.append(mm)
        xs, _ = _trunk(xs, mod, lambda i: 1 + i // tiles_per_latent, latent_rows,
                       (state_rwkv[:, l], state_mlstm_C[:, l], state_mlstm_n[:, l], state_mlstm_m[:, l]), p)
    return (xp, xs, jnp.stack(new_s, axis=1), jnp.stack(new_c, axis=1),
            jnp.stack(new_n, axis=1), jnp.stack(new_m, axis=1))
```

```python
import functools

import jax
import jax.numpy as jnp
from jax import lax
from jax.experimental import pallas as pl
from jax.experimental.pallas import tpu as pltpu

F32 = jnp.float32
BF16 = jnp.bfloat16

D_MODEL = 1024
N_DIR = 2
RWKV_HEADS = 8
RWKV_HEAD_DIM = 64
RWKV_WIDTH = RWKV_HEADS * RWKV_HEAD_DIM
DECAY_LORA = 64
ICLR_LORA = 64
GATE_LORA = 128
MLSTM_HEADS = 4
MLSTM_HEAD_DIM = 128
MLSTM_WIDTH = MLSTM_HEADS * MLSTM_HEAD_DIM
MLSTM_CHUNK = 64
D_FF = 2816
GRID_W = 64
RMS_EPS = 1e-6
RWKV_GN_EPS = 64e-5
MLSTM_GN_EPS = 1e-5
DECAY_SCALE = 0.606531

RWKV_COLS = 3 * RWKV_WIDTH + N_DIR * DECAY_LORA + N_DIR * ICLR_LORA + GATE_LORA
MLSTM_GATES = 2 * N_DIR * MLSTM_HEADS
MLSTM_COLS = 4 * MLSTM_WIDTH + MLSTM_GATES
GATE_COLS = 2 * D_MODEL

LANE = 128
ZR_BLOCK = 2048
ZG_OFF = RWKV_COLS
ZM_OFF = ZR_BLOCK
ZS_OFF = ZM_OFF + 4 * MLSTM_WIDTH
Z_COLS = ZS_OFF + GATE_COLS

TOK_TILE = 256
RCHUNK = 64
CONV_CH_TILE = 256
VMEM_LIMIT = 56 * 1024 * 1024


def _params(sem):
    return pltpu.CompilerParams(dimension_semantics=sem, vmem_limit_bytes=VMEM_LIMIT)


def _resident(shape):
    nd = len(shape)
    return pl.BlockSpec(shape, lambda *_: (0,) * nd, pipeline_mode=pl.Buffered(1))


def _split2(a):
    hi = a.astype(BF16)
    lo = (a - hi.astype(F32)).astype(BF16)
    return hi, lo


def _split3(a):
    hi = a.astype(BF16)
    r1 = a - hi.astype(F32)
    mid = r1.astype(BF16)
    lo = (r1 - mid.astype(F32)).astype(BF16)
    return hi, mid, lo


def _dg(a, b, dims):
    return lax.dot_general(a, b, dims, preferred_element_type=F32)


def _mm(a, b, dims, passes):
    if passes == 1:
        return _dg(a.astype(BF16), b.astype(BF16), dims)
    ah, al = _split2(a)
    bh, bl = _split2(b)
    return _dg(ah, bh, dims) + (_dg(ah, bl, dims) + _dg(al, bh, dims))


def _mm_exact_lhs(a_bf16, b, dims):
    b1, b2, b3 = _split3(b)
    return _dg(a_bf16, b1, dims) + (_dg(a_bf16, b2, dims) + _dg(a_bf16, b3, dims))


def _mm_exact_rhs(a, b_bf16, dims):
    a1, a2, a3 = _split3(a)
    return _dg(a1, b_bf16, dims) + (_dg(a2, b_bf16, dims) + _dg(a3, b_bf16, dims))


_NN = (((1,), (0,)), ((), ()))
_NT = (((1,), (1,)), ((), ()))
_TN = (((0,), (0,)), ((), ()))
_BNN = (((2,), (1,)), ((0,), (0,)))
_BNT = (((2,), (2,)), ((0,), (0,)))
_BTN = (((1,), (1,)), ((0,), (0,)))


def _sigmoid(x):
    return jax.nn.sigmoid(x)


def _silu(x):
    return x * jax.nn.sigmoid(x)


def _rms(x, g):
    return x * lax.rsqrt(jnp.mean(x * x, axis=-1, keepdims=True) + RMS_EPS) * g


def _ada_kernel(cond_ref, w_ref, b_ref, o_ref):
    s = _silu(cond_ref[...])
    o_ref[...] = _dg(s.astype(BF16), w_ref[...].astype(BF16), _NN) + b_ref[...]


def _ada(cond8, ada_w, ada_b):
    n = ada_w.shape[1]
    tn = 1536
    return pl.pallas_call(
        _ada_kernel,
        grid=(n // tn,),
        in_specs=[_resident((8, D_MODEL)),
                  pl.BlockSpec((D_MODEL, tn), lambda j: (0, j)),
                  pl.BlockSpec((1, tn), lambda j: (0, j))],
        out_specs=pl.BlockSpec((8, tn), lambda j: (0, j)),
        out_shape=jax.ShapeDtypeStruct((8, n), F32),
        compiler_params=_params(("arbitrary",)),
        name="ada_mod",
    )(cond8, ada_w, ada_b.reshape(1, n))


def _in_kernel(x_ref, mod_ref, g_ref, wr_ref, wg_ref, wm_ref, ws_ref, z_ref):
    mod = mod_ref[0]
    sh = mod[:, 0:D_MODEL]
    sc = mod[:, D_MODEL:2 * D_MODEL]
    h = (_rms(x_ref[...], g_ref[...]) * (1.0 + sc) + sh).astype(BF16)
    z_ref[:, 0:ZG_OFF] = _dg(h, wr_ref[...], _NN)
    z_ref[:, ZG_OFF:ZM_OFF] = _dg(h, wg_ref[...], _NN)
    z_ref[:, ZM_OFF:ZS_OFF] = _dg(h, wm_ref[...], _NN)
    z_ref[:, ZS_OFF:Z_COLS] = _sigmoid(_dg(h, ws_ref[...], _NN))


def _in_proj(x2, mod, mod_row, norm_g0, w_r, w_g, w_m, w_s):
    n = x2.shape[0]
    return pl.pallas_call(
        _in_kernel,
        grid=(n // TOK_TILE,),
        in_specs=[pl.BlockSpec((TOK_TILE, D_MODEL), lambda i: (i, 0)),
                  pl.BlockSpec((1, 1, 6 * D_MODEL), lambda i: (mod_row(i), 0, 0)),
                  _resident((1, D_MODEL)),
                  _resident(w_r.shape), _resident(w_g.shape), _resident(w_m.shape), _resident(w_s.shape)],
        out_specs=pl.BlockSpec((TOK_TILE, Z_COLS), lambda i: (i, 0)),
        out_shape=jax.ShapeDtypeStruct((n, Z_COLS), F32),
        compiler_params=_params(("arbitrary",)),
        name="in_proj",
    )(x2, mod, norm_g0, w_r, w_g, w_m, w_s)


LOCAL_CHUNKS = 4
PAIR_LANES = 2 * RWKV_HEAD_DIM
RWKV_PAIRS = RWKV_HEADS // 2


def _bd(x):
    lane = lax.broadcasted_iota(jnp.int32, x.shape, 1)
    left = lane < RWKV_HEAD_DIM
    return jnp.concatenate([jnp.where(left, x, 0.0), jnp.where(left, 0.0, x)], axis=0)


def _rwkv_local_kernel(chunks_per_seq, passes,
                       z_ref, zp_ref, zn_ref, mu_ref, w0_ref, wup_ref, a0_ref, aup_ref, gup_ref,
                       kks_ref, ka_ref, rk_ref, pones_ref,
                       rp_ref, y0_ref, gm_ref, hm_ref, gate_ref, bonus_ref):
    C = RCHUNK
    W = RWKV_WIDTH
    NS = LOCAL_CHUNKS
    R = NS * C
    first = (pl.program_id(0) * NS) % chunks_per_seq
    has_prev = first != 0
    has_next = first + NS != chunks_per_seq

    z = z_ref[:, 0:RWKV_COLS]
    zp = jnp.where(has_prev, zp_ref[7:8, 0:RWKV_COLS], 0.0)
    zn = jnp.where(has_next, zn_ref[0:1, 0:RWKV_COLS], 0.0)
    trow = lax.broadcasted_iota(jnp.int32, (R, 1), 0)
    prev = jnp.where(trow == 0, zp, pltpu.roll(z, 1, 0))
    nxt = jnp.where(trow == R - 1, zn, pltpu.roll(z, R - 1, 0))
    zs = z + mu_ref[...] * (0.5 * (prev + nxt) - z)

    r = zs[:, 0:W]
    k = zs[:, W:2 * W]
    v = zs[:, 2 * W:3 * W]
    gd = zs[:, 3 * W + 2 * DECAY_LORA + 2 * ICLR_LORA:RWKV_COLS]
    gate_ref[...] = _dg(_sigmoid(gd).astype(BF16), gup_ref[...].astype(BF16), _NN)

    pones = pones_ref[...]
    kks = k * kks_ref[...]
    norm = jnp.sqrt(_mm_exact_rhs(kks * kks, pones, _NN))
    kk = kks / jnp.maximum(norm, 1e-12)

    P = PAIR_LANES
    row = lax.broadcasted_iota(jnp.int32, (R, R), 0)
    col = lax.broadcasted_iota(jnp.int32, (R, R), 1)
    same_chunk = jnp.bitwise_and(row, -C) == jnp.bitwise_and(col, -C)
    prow = lax.broadcasted_iota(jnp.int32, (C, P), 0)
    pcol = jnp.bitwise_and(lax.broadcasted_iota(jnp.int32, (C, P), 1), RWKV_HEAD_DIM - 1)
    eye_p = jnp.where(prow == pcol, 1.0, 0.0)
    brow = lax.broadcasted_iota(jnp.int32, (P, P), 0)
    bcol = lax.broadcasted_iota(jnp.int32, (P, P), 1)
    same_head = (brow < RWKV_HEAD_DIM) == (bcol < RWKV_HEAD_DIM)
    eye_b = jnp.where(brow == bcol, 1.0, 0.0)

    abar, rbar, kt, bt, kw, bw, wc, strict, incl = [], [], [], [], [], [], [], [], []
    bonus = None
    for d in range(N_DIR):
        o = 3 * W + d * DECAY_LORA
        wd = zs[:, o:o + DECAY_LORA]
        o = 3 * W + 2 * DECAY_LORA + d * ICLR_LORA
        ad = zs[:, o:o + ICLR_LORA]
        logw = -DECAY_SCALE * _sigmoid(w0_ref[d] + _dg(jnp.tanh(wd).astype(BF16), wup_ref[d].astype(BF16), _NN))
        a = _sigmoid(a0_ref[d] + _dg(ad.astype(BF16), aup_ref[d].astype(BF16), _NN))
        kd = k * (1.0 + (a - 1.0) * ka_ref[...])
        b = kk * a
        bonus_d = _mm_exact_rhs(r * kd * rk_ref[...], pones, _NN) * v
        bonus = bonus_d if bonus is None else bonus + bonus_d

        earlier_or_same = same_chunk & ((row >= col) if d == 0 else (row <= col))
        cum_i = _mm_exact_lhs(jnp.where(earlier_or_same, 1.0, 0.0).astype(BF16), logw, _NN)
        cum_e = cum_i - logw
        ab_d, rb_d, kt_d, bt_d, kw_d, bw_d, wc_d = [], [], [], [], [], [], []
        for s in range(NS):
            rs = slice(s * C, (s + 1) * C)
            ci_s = cum_i[rs]
            ctot = jnp.sum(logw[rs], axis=0, keepdims=True)
            e_ni = jnp.exp(-ci_s)
            e_ti = jnp.exp(ctot - ci_s)
            ab_d.append(kk[rs] * jnp.exp(cum_e[rs]))
            rb_d.append(r[rs] * jnp.exp(ci_s))
            kt_d.append(kd[rs] * e_ni)
            bt_d.append(b[rs] * e_ni)
            kw_d.append(kd[rs] * e_ti)
            bw_d.append(b[rs] * e_ti)
            wc_d.append(jnp.exp(ctot))
        abar.append(ab_d)
        rbar.append(rb_d)
        kt.append(kt_d)
        bt.append(bt_d)
        kw.append(kw_d)
        bw.append(bw_d)
        wc.append(wc_d)
        strict.append((prow > pcol) if d == 0 else (prow < pcol))
        incl.append((prow >= pcol) if d == 0 else (prow <= pcol))
    bonus_ref[...] = bonus

    mm = functools.partial(_mm, passes=passes)
    chains = [(s, d, p) for s in range(NS) for d in range(N_DIR) for p in range(RWKV_PAIRS)]
    nch = range(len(chains))

    def sel(arr, i):
        s, d, p = chains[i]
        return arr[d][s][:, p * P:(p + 1) * P]

    cat0 = lambda a_, b_: jnp.concatenate([a_, b_], axis=0)
    cat1 = lambda a_, b_: jnp.concatenate([a_, b_], axis=1)
    vsl = [v[s * C:(s + 1) * C, p * P:(p + 1) * P] for s, _, p in chains]
    lhs = [cat0(sel(abar, i), sel(rbar, i)) for i in nch]
    by_b = [mm(lhs[i], _bd(sel(bt, i)), _NT) for i in nch]
    by_k = [mm(lhs[i], _bd(sel(kt, i)), _NT) for i in nch]
    a_kk = [jnp.where(strict[chains[i][1]], by_b[i][0:C], 0.0) for i in nch]
    a_rb = [jnp.where(incl[chains[i][1]], by_b[i][C:2 * C], 0.0) for i in nch]
    a_kv = [jnp.where(strict[chains[i][1]], by_k[i][0:C], 0.0) for i in nch]
    a_rk = [jnp.where(incl[chains[i][1]], by_k[i][C:2 * C], 0.0) for i in nch]
    on_v = [mm(cat0(a_kv[i], a_rk[i]), _bd(vsl[i]), _NN) for i in nch]

    x = [-m for m in a_kk]
    tinv = [eye_p + m for m in x]
    x = [mm(m, _bd(m), _NN) for m in x]
    for _ in range(4):
        both = [mm(cat0(tinv[i], x[i]), _bd(x[i]), _NN) for i in nch]
        tinv = [tinv[i] + both[i][0:C] for i in nch]
        x = [m[C:2 * C] for m in both]
    tinv = [tinv[i] + mm(tinv[i], _bd(x[i]), _NN) for i in nch]

    solved = [mm(tinv[i], cat1(_bd(sel(abar, i)), _bd(on_v[i][0:C])), _NN) for i in nch]
    ap = [m[:, 0:P] for m in solved]
    u0 = [m[:, P:2 * P] for m in solved]
    corr = [mm(a_rb[i], cat1(_bd(ap[i]), _bd(u0[i])), _NN) for i in nch]
    on_b = [mm(cat1(ap[i], u0[i]), sel(bw, i), _TN) for i in nch]
    vk = [mm(vsl[i], sel(kw, i), _TN) for i in nch]
    for i in nch:
        s, d, p = chains[i]
        rows = slice(s * C, (s + 1) * C)
        lanes = slice(p * P, (p + 1) * P)
        rp_ref[d, rows, lanes] = sel(rbar, i) - corr[i][:, 0:P]
        y0_ref[d, rows, lanes] = on_v[i][C:2 * C] - corr[i][:, P:2 * P]
        gm_ref[d, s, p] = eye_b * sel(wc, i) - jnp.where(same_head, on_b[i][0:P], 0.0)
        hm_ref[d, s, p] = jnp.where(same_head, vk[i] - on_b[i][P:2 * P], 0.0)


def _rwkv_local(z, seq_len, p, passes):
    n = z.shape[0]
    nchunk = n // RCHUNK
    cps = seq_len // RCHUNK
    assert cps % LOCAL_CHUNKS == 0
    W = RWKV_WIDTH
    rows = LOCAL_CHUNKS * RCHUNK
    hb = rows // 8
    last8 = n // 8 - 1
    mat = jax.ShapeDtypeStruct((N_DIR, nchunk, RWKV_PAIRS, PAIR_LANES, PAIR_LANES), F32)
    mat_spec = pl.BlockSpec((N_DIR, LOCAL_CHUNKS, RWKV_PAIRS, PAIR_LANES, PAIR_LANES),
                            lambda c: (0, c, 0, 0, 0))
    tok = jax.ShapeDtypeStruct((N_DIR, n, W), F32)
    tok_spec = pl.BlockSpec((N_DIR, rows, W), lambda c: (0, c, 0))
    row_spec = pl.BlockSpec((rows, W), lambda c: (c, 0))
    return pl.pallas_call(
        functools.partial(_rwkv_local_kernel, cps, passes),
        grid=(nchunk // LOCAL_CHUNKS,),
        in_specs=[pl.BlockSpec((rows, ZR_BLOCK), lambda c: (c, 0)),
                  pl.BlockSpec((8, ZR_BLOCK), lambda c: (jnp.maximum(c * hb - 1, 0), 0)),
                  pl.BlockSpec((8, ZR_BLOCK), lambda c: (jnp.minimum((c + 1) * hb, last8), 0)),
                  _resident((1, RWKV_COLS)),
                  _resident((N_DIR, 1, W)), _resident((N_DIR, DECAY_LORA, W)),
                  _resident((N_DIR, 1, W)), _resident((N_DIR, ICLR_LORA, W)),
                  _resident((GATE_LORA, W)),
                  _resident((1, W)), _resident((1, W)), _resident((1, W)),
                  _resident((W, W))],
        out_specs=[tok_spec, tok_spec, mat_spec, mat_spec, row_spec, row_spec],
        out_shape=[tok, tok, mat, mat,
                   jax.ShapeDtypeStruct((n, W), F32), jax.ShapeDtypeStruct((n, W), F32)],
        compiler_params=_params(("arbitrary",)),
        name="rwkv_local",
    )(z, z, z, p["mu"], p["w0"], p["w_up"], p["a0"], p["a_up"], p["g_up"],
      p["kk_scale"], p["k_a"], p["r_k"], p["pones"])


SCAN_CHUNKS = 4


def _rwkv_scan_kernel(passes, s0_ref, rpf_ref, rpb_ref, y0f_ref, y0b_ref, gmf_ref, gmb_ref, hmf_ref, hmb_ref,
                      ysf_ref, ysb_ref, sout_ref, s_scr):
    @pl.when(pl.program_id(1) == 0)
    def _():
        s_scr[...] = s0_ref[0]

    K = SCAN_CHUNKS
    C = RCHUNK
    rp_ref, y0_ref, gm_ref, hm_ref, ys_ref = ((rpf_ref, rpb_ref), (y0f_ref, y0b_ref), (gmf_ref, gmb_ref),
                                              (hmf_ref, hmb_ref), (ysf_ref, ysb_ref))
    chains = [(d, p) for d in range(N_DIR) for p in range(RWKV_PAIRS)]
    lanes = [slice(p * PAIR_LANES, (p + 1) * PAIR_LANES) for _, p in chains]
    nch = range(len(chains))
    s = [s_scr[d, p] for d, p in chains]
    for j in range(K):
        at = (j, K - 1 - j)
        rows = [slice(at[d] * C, (at[d] + 1) * C) for d, _ in chains]
        y = [_mm(rp_ref[chains[i][0]][0, rows[i], lanes[i]], s[i], _NT, passes) for i in nch]
        sg = [_mm(s[i], gm_ref[chains[i][0]][0, at[chains[i][0]], chains[i][1]], _NN, passes) for i in nch]
        for i in nch:
            d, p = chains[i]
            ys_ref[d][rows[i], lanes[i]] = y[i] + y0_ref[d][0, rows[i], lanes[i]]
        s = [sg[i] + hm_ref[chains[i][0]][0, at[chains[i][0]], chains[i][1]] for i in nch]
    for i in nch:
        d, p = chains[i]
        s_scr[d, p] = s[i]
        sout_ref[0, d, p] = s[i]


def _rwkv_scan(s0, rp, y0, gm, hm, batch, seq_len, passes):
    K = SCAN_CHUNKS
    spb = seq_len // (RCHUNK * K)
    n = batch * seq_len

    def fwd(b, s):
        return b * spb + s

    def bwd(b, s):
        return b * spb + spb - 1 - s

    def mat_spec(d, at):
        return pl.BlockSpec((1, K, RWKV_PAIRS, PAIR_LANES, PAIR_LANES), lambda b, s: (d, at(b, s), 0, 0, 0))

    def tok_spec(d, at):
        return pl.BlockSpec((1, K * RCHUNK, RWKV_WIDTH), lambda b, s: (d, at(b, s), 0))

    st_spec = pl.BlockSpec((1, N_DIR, RWKV_PAIRS, PAIR_LANES, PAIR_LANES), lambda b, s: (b, 0, 0, 0, 0))
    ys = jax.ShapeDtypeStruct((n, RWKV_WIDTH), F32)
    return pl.pallas_call(
        functools.partial(_rwkv_scan_kernel, passes),
        grid=(batch, spb),
        in_specs=[st_spec, tok_spec(0, fwd), tok_spec(1, bwd), tok_spec(0, fwd), tok_spec(1, bwd),
                  mat_spec(0, fwd), mat_spec(1, bwd), mat_spec(0, fwd), mat_spec(1, bwd)],
        out_specs=[pl.BlockSpec((K * RCHUNK, RWKV_WIDTH), lambda b, s: (fwd(b, s), 0)),
                   pl.BlockSpec((K * RCHUNK, RWKV_WIDTH), lambda b, s: (bwd(b, s), 0)),
                   st_spec],
        out_shape=[ys, ys,
                   jax.ShapeDtypeStruct((batch, N_DIR, RWKV_PAIRS, PAIR_LANES, PAIR_LANES), F32)],
        scratch_shapes=[pltpu.VMEM((N_DIR, RWKV_PAIRS, PAIR_LANES, PAIR_LANES), F32)],
        compiler_params=_params(("arbitrary", "arbitrary")),
        name="rwkv_scan",
    )(s0, rp, rp, y0, y0, gm, gm, hm, hm)


CONV_BLOCK_ROWS = 2048


def _dwconv(x, w_ref, width, vertical):
    T = x.shape[0]
    t = lax.broadcasted_iota(jnp.int32, (T, 1), 0)
    assert width & (width - 1) == 0
    colp = jnp.bitwise_and(t, width - 1)
    xl = jnp.where(colp == 0, 0.0, pltpu.roll(x, 1, 0))
    xr = jnp.where(colp == width - 1, 0.0, pltpu.roll(x, T - 1, 0))

    def tap_row(i):
        return w_ref[3 * i:3 * i + 1, :] * xl + w_ref[3 * i + 1:3 * i + 2, :] * x + w_ref[3 * i + 2:3 * i + 3, :] * xr

    out = tap_row(1)
    if vertical:
        out = out + jnp.where(t < width, 0.0, pltpu.roll(tap_row(0), width, 0))
        out = out + jnp.where(t >= T - width, 0.0, pltpu.roll(tap_row(2), T - width, 0))
    return out


def _conv_geometry(n, seq_len, rows):
    if rows > 1:
        return seq_len, seq_len // rows, True
    block = CONV_BLOCK_ROWS if (n % CONV_BLOCK_ROWS == 0 and CONV_BLOCK_ROWS % seq_len == 0) else seq_len
    return block, seq_len, False


def _qk_conv_kernel(width, vertical, x_ref, w_ref, o_ref):
    o_ref[...] = _silu(_dwconv(x_ref[...], w_ref, width, vertical))


def _qk_conv(z, batch, seq_len, rows, conv_w9):
    n = batch * seq_len
    ch = 2 * MLSTM_WIDTH
    tc = CONV_CH_TILE
    off = ZM_OFF // tc
    block, width, vertical = _conv_geometry(n, seq_len, rows)
    return pl.pallas_call(
        functools.partial(_qk_conv_kernel, width, vertical),
        grid=(n // block, ch // tc),
        in_specs=[pl.BlockSpec((block, tc), lambda b, j: (b, off + j)),
                  pl.BlockSpec((9, tc), lambda b, j: (0, j))],
        out_specs=pl.BlockSpec((block, tc), lambda b, j: (b, j)),
        out_shape=jax.ShapeDtypeStruct((n, ch), F32),
        compiler_params=_params(("arbitrary", "arbitrary")),
        name="mlstm_qk_conv",
    )(z, conv_w9)


def _ffn_conv_kernel(width, vertical, ua_ref, uv_ref, w_ref, b_ref, o_ref):
    act = _dwconv(ua_ref[...], w_ref, width, vertical) + b_ref[...]
    o_ref[...] = (_silu(act) * uv_ref[...]).astype(BF16)


def _ffn_conv(u, batch, seq_len, rows, conv_w9, conv_b):
    n = batch * seq_len
    tc = CONV_CH_TILE
    nct = D_FF // tc
    block, width, vertical = _conv_geometry(n, seq_len, rows)
    return pl.pallas_call(
        functools.partial(_ffn_conv_kernel, width, vertical),
        grid=(n // block, nct),
        in_specs=[pl.BlockSpec((block, tc), lambda b, j: (b, j)),
                  pl.BlockSpec((block, tc), lambda b, j: (b, nct + j)),
                  pl.BlockSpec((9, tc), lambda b, j: (0, j)),
                  pl.BlockSpec((1, tc), lambda b, j: (0, j))],
        out_specs=pl.BlockSpec((block, tc), lambda b, j: (b, j)),
        out_shape=jax.ShapeDtypeStruct((n, D_FF), BF16),
        compiler_params=_params(("arbitrary", "arbitrary")),
        name="ffn_conv",
    )(u, u, conv_w9, conv_b)


MLSTM_STEP_CHUNKS = 4

def _mlstm_scan_kernel(qkf_ref, qkb_ref, ktf_ref, ktb_ref, vf_ref, vb_ref, gcf_ref, gcb_ref, grf_ref, grb_ref,
                       gbc_ref, gbr_ref, c0_ref, n0_ref, m0_ref,
                       hf_ref, hb_ref, cout_ref, nout_ref, mout_ref,
                       c_scr, n_scr, m_scr):
    step = pl.program_id(1)
    L = MLSTM_CHUNK
    dh = MLSTM_HEAD_DIM
    H = MLSTM_HEADS

    @pl.when(step == 0)
    def _():
        c_scr[...] = c0_ref[0]
        n_scr[...] = n0_ref[0]
        m_scr[...] = m0_ref[0]

    K = MLSTM_STEP_CHUNKS
    R = K * L
    row = lax.broadcasted_iota(jnp.int32, (L, L), 0)
    col = lax.broadcasted_iota(jnp.int32, (L, L), 1)
    lower = (row >= col)
    upper = (row <= col)
    lower_b = jnp.where(lower, 1.0, 0.0).astype(BF16)
    upper_b = jnp.where(upper, 1.0, 0.0).astype(BF16)
    rrow = lax.broadcasted_iota(jnp.int32, (R, R), 0)
    rcol = lax.broadcasted_iota(jnp.int32, (R, R), 1)
    same_chunk = jnp.bitwise_and(rrow, -L) == jnp.bitwise_and(rcol, -L)
    neg_inf = jnp.full((), -jnp.inf, F32)

    gcol, grow, bcol, brow, btot = [], [], [], [], []
    ones_b = jnp.ones((L, LANE), BF16)
    for d in range(N_DIR):
        gc_ref, gr_ref = (gcf_ref, grf_ref) if d == 0 else (gcb_ref, grb_ref)
        gcol.append(gc_ref[...] + gbc_ref[...])
        grow.append((gr_ref[...] + gbr_ref[...][None]).reshape(K * MLSTM_GATES, L))
        before = same_chunk & ((rrow >= rcol) if d == 0 else (rrow <= rcol))
        bcol.append(_mm_exact_lhs(jnp.where(before, 1.0, 0.0).astype(BF16), jax.nn.log_sigmoid(gcol[d]), _NN))
        frow = jax.nn.log_sigmoid(grow[d])
        brow.append(_mm_exact_rhs(frow, upper_b if d == 0 else lower_b, _NN))
        btot.append(_mm_exact_rhs(frow, ones_b, _NN))

    units = [(j, d, h) for j in range(K) for d in range(N_DIR) for h in range(H)]
    nun = range(len(units))
    q, k, kt, v, vb, qb = [], [], [], [], [], []
    c_row, b_col, b_last = [], [], []
    for j, d, h in units:
        at = j if d == 0 else K - 1 - j
        rows = slice(at * L, (at + 1) * L)
        st = d * H + h
        gi, gf = st, 2 * H + st
        qk_ref, kt_ref, v_ref = (qkf_ref, ktf_ref, vf_ref) if d == 0 else (qkb_ref, ktb_ref, vb_ref)
        q.append(qk_ref[rows, h * dh:(h + 1) * dh] * (dh ** -0.5))
        k.append(qk_ref[rows, MLSTM_WIDTH + h * dh:MLSTM_WIDTH + (h + 1) * dh])
        kt.append(kt_ref[at, h * dh:(h + 1) * dh, :])
        v.append(v_ref[rows, h * dh:(h + 1) * dh])
        qb.append(q[-1].astype(BF16))
        vb.append(v[-1].astype(BF16))
        b_col.append(jnp.broadcast_to(bcol[d][rows, gf:gf + 1], (L, LANE)))
        c_row.append(grow[d][at * MLSTM_GATES + gi:at * MLSTM_GATES + gi + 1, :]
                     - brow[d][at * MLSTM_GATES + gf:at * MLSTM_GATES + gf + 1, :])
        b_last.append(btot[d][at * MLSTM_GATES + gf:at * MLSTM_GATES + gf + 1, :])

    last = [L - 1 if d == 0 else 0 for _, d, _ in units]
    qk_t = [_dg(qb[i], k[i].astype(BF16), _NT) for i in nun]
    rel = [jnp.where(lower if units[i][1] == 0 else upper, c_row[i], neg_inf) for i in nun]
    mx = [jnp.broadcast_to(jnp.max(rel[i], axis=-1, keepdims=True), (L, LANE)) for i in nun]
    m_loc = [b_col[i] + mx[i] for i in nun]
    s_loc = [qk_t[i] * jnp.exp(rel[i] - mx[i][:, 0:L]) for i in nun]
    s_v = [_dg(s_loc[i].astype(BF16), vb[i], _NN) for i in nun]
    s_sum = [jnp.broadcast_to(jnp.sum(s_loc[i], axis=-1, keepdims=True), (L, LANE)) for i in nun]
    cmax = [mx[i][last[i]:last[i] + 1, :] for i in nun]
    m_w = [b_last[i] + cmax[i] for i in nun]
    wj = [jnp.exp(c_row[i] - cmax[i][:, 0:L]) for i in nun]
    kv = [_dg((kt[i] * wj[i]).astype(BF16), vb[i], _NN) for i in nun]
    w_k = [_mm(jnp.broadcast_to(wj[i], (8, L)), k[i], _NN, 3)[0:1] for i in nun]

    nst = N_DIR * H
    c_st = [c_scr[st] for st in range(nst)]
    n_st = [n_scr[st:st + 1, :] for st in range(nst)]
    m_st = [m_scr[st:st + 1, :] for st in range(nst)]
    for j in range(K):
        idx = [j * nst + st for st in range(nst)]
        q_c = [_dg(qb[i], c_st[st].astype(BF16), _NN) for st, i in enumerate(idx)]
        for st, i in enumerate(idx):
            _, d, h = units[i]
            at = j if d == 0 else K - 1 - j
            h_ref = hf_ref if d == 0 else hb_ref
            log_inter = b_col[i] + m_st[st]
            m_s = jnp.maximum(log_inter, m_loc[i])
            inter = jnp.exp(log_inter - m_s)
            local = jnp.exp(m_loc[i] - m_s)
            q_n = jnp.broadcast_to(jnp.sum(q[i] * n_st[st], axis=-1, keepdims=True), (L, LANE))
            den = inter * q_n + local * s_sum[i]
            scale = 1.0 / jnp.maximum(jnp.abs(den), jnp.exp(-m_s))
            h_ref[at * L:(at + 1) * L, h * dh:(h + 1) * dh] = (inter * scale) * q_c[st] + (local * scale) * s_v[i]
            m_new = jnp.maximum(b_last[i] + m_st[st], m_w[i])
            carry = jnp.exp(b_last[i] + m_st[st] - m_new)
            fresh = jnp.exp(m_w[i] - m_new)
            c_st[st] = carry * c_st[st] + fresh * kv[i]
            n_st[st] = carry * n_st[st] + fresh * w_k[i]
            m_st[st] = m_new

    for st in range(nst):
        c_scr[st] = c_st[st]
        n_scr[st:st + 1, :] = n_st[st]
        m_scr[st:st + 1, :] = m_st[st]
    cout_ref[0] = c_scr[...]
    nout_ref[0] = n_scr[...]
    mout_ref[0] = m_scr[...]


def _mlstm_scan(z, qk, kt, gt, gate_bc, gate_br, c0, n0, m0, batch, seq_len):
    K = MLSTM_STEP_CHUNKS
    L = K * MLSTM_CHUNK
    assert seq_len % L == 0
    cps = seq_len // L
    n = batch * seq_len
    W = MLSTM_WIDTH
    nst = N_DIR * MLSTM_HEADS
    dh = MLSTM_HEAD_DIM

    def fw(b, c):
        return b * cps + c

    def bw(b, c):
        return b * cps + cps - 1 - c

    vblk = (ZM_OFF + 2 * W) // W
    gblk = ZG_OFF // LANE
    return pl.pallas_call(
        _mlstm_scan_kernel,
        grid=(batch, cps),
        in_specs=[pl.BlockSpec((L, 2 * W), lambda b, c: (fw(b, c), 0)),
                  pl.BlockSpec((L, 2 * W), lambda b, c: (bw(b, c), 0)),
                  pl.BlockSpec((K, W, MLSTM_CHUNK), lambda b, c: (fw(b, c), 0, 0)),
                  pl.BlockSpec((K, W, MLSTM_CHUNK), lambda b, c: (bw(b, c), 0, 0)),
                  pl.BlockSpec((L, W), lambda b, c: (fw(b, c), vblk)),
                  pl.BlockSpec((L, W), lambda b, c: (bw(b, c), vblk)),
                  pl.BlockSpec((L, LANE), lambda b, c: (fw(b, c), gblk)),
                  pl.BlockSpec((L, LANE), lambda b, c: (bw(b, c), gblk)),
                  pl.BlockSpec((K, MLSTM_GATES, MLSTM_CHUNK), lambda b, c: (fw(b, c), 0, 0)),
                  pl.BlockSpec((K, MLSTM_GATES, MLSTM_CHUNK), lambda b, c: (bw(b, c), 0, 0)),
                  _resident((1, LANE)),
                  _resident((MLSTM_GATES, 1)),
                  pl.BlockSpec((1, nst, dh, dh), lambda b, c: (b, 0, 0, 0)),
                  pl.BlockSpec((1, nst, dh), lambda b, c: (b, 0, 0)),
                  pl.BlockSpec((1, nst, LANE), lambda b, c: (b, 0, 0))],
        out_specs=[pl.BlockSpec((L, W), lambda b, c: (fw(b, c), 0)),
                   pl.BlockSpec((L, W), lambda b, c: (bw(b, c), 0)),
                   pl.BlockSpec((1, nst, dh, dh), lambda b, c: (b, 0, 0, 0)),
                   pl.BlockSpec((1, nst, dh), lambda b, c: (b, 0, 0)),
                   pl.BlockSpec((1, nst, LANE), lambda b, c: (b, 0, 0))],
        out_shape=[jax.ShapeDtypeStruct((n, W), F32), jax.ShapeDtypeStruct((n, W), F32),
                   jax.ShapeDtypeStruct((batch, nst, dh, dh), F32),
                   jax.ShapeDtypeStruct((batch, nst, dh), F32),
                   jax.ShapeDtypeStruct((batch, nst, LANE), F32)],
        scratch_shapes=[pltpu.VMEM((nst, dh, dh), F32), pltpu.VMEM((nst, dh), F32),
                        pltpu.VMEM((nst, LANE), F32)],
        compiler_params=_params(("arbitrary", "arbitrary")),
        name="mlstm_scan",
    )(qk, qk, kt, kt, z, z, z, z, gt, gt, gate_bc, gate_br, c0, n0, m0)


def _merge_kernel(x_ref, mod_ref, ysf_ref, ysb_ref, bonus_ref, gate_ref, hf_ref, hb_ref, zo_ref, zs_ref,
                  lnxg_ref, lnxb_ref, gng_ref, pmean_ref, wbr_ref, wbm_ref, wout_ref, ng_ref, wup_ref,
                  x1_ref, u_ref):
    mod = mod_ref[0]
    g1 = mod[:, 2 * D_MODEL:3 * D_MODEL]
    sh2 = mod[:, 3 * D_MODEL:4 * D_MODEL]
    sc2 = mod[:, 4 * D_MODEL:5 * D_MODEL]

    ys = ysf_ref[...] + ysb_ref[...]
    pmean = pmean_ref[...]
    mean = _mm_exact_rhs(ys, pmean, _NN)
    cen = ys - mean
    var = _mm_exact_rhs(cen * cen, pmean, _NN)
    y_r = (cen * lax.rsqrt(var + RWKV_GN_EPS) * lnxg_ref[...] + lnxb_ref[...] + bonus_ref[...]) * gate_ref[...]

    hs = hf_ref[...] + hb_ref[...]
    parts = []
    for h in range(MLSTM_HEADS):
        hh = hs[:, h * MLSTM_HEAD_DIM:(h + 1) * MLSTM_HEAD_DIM]
        mu = jnp.mean(hh, axis=-1, keepdims=True)
        ce = hh - mu
        va = jnp.mean(ce * ce, axis=-1, keepdims=True)
        parts.append(ce * lax.rsqrt(va + MLSTM_GN_EPS))
    y_m = jnp.concatenate(parts, axis=1) * gng_ref[...] * _sigmoid(zo_ref[...])

    gates = zs_ref[...]
    merged = (gates[:, 0:D_MODEL] * _dg(y_r.astype(BF16), wbr_ref[...], _NN)
              + gates[:, D_MODEL:2 * D_MODEL] * _dg(y_m.astype(BF16), wbm_ref[...], _NN))
    t = _dg(merged.astype(BF16), wout_ref[...], _NN)
    x1 = x_ref[...] + g1 * _rms(t, ng_ref[1:2, :])
    x1_ref[...] = x1
    h2 = _rms(x1, ng_ref[2:3, :]) * (1.0 + sc2) + sh2
    u_ref[...] = _dg(h2.astype(BF16), wup_ref[...], _NN)


def _merge(x2, mod, mod_row, z, ysf, ysb, bonus, gate, hf, hb, p):
    n = x2.shape[0]
    W = RWKV_WIDTH
    tile = lambda w: pl.BlockSpec((TOK_TILE, w), lambda i: (i, 0))
    return pl.pallas_call(
        _merge_kernel,
        grid=(n // TOK_TILE,),
        in_specs=[tile(D_MODEL),
                  pl.BlockSpec((1, 1, 6 * D_MODEL), lambda i: (mod_row(i), 0, 0)),
                  tile(W), tile(W), tile(W), tile(W), tile(MLSTM_WIDTH), tile(MLSTM_WIDTH),
                  pl.BlockSpec((TOK_TILE, MLSTM_WIDTH), lambda i: (i, (ZM_OFF + 3 * MLSTM_WIDTH) // MLSTM_WIDTH)),
                  pl.BlockSpec((TOK_TILE, GATE_COLS), lambda i: (i, ZS_OFF // GATE_COLS)),
                  _resident((1, W)), _resident((1, W)), _resident((1, MLSTM_WIDTH)),
                  _resident((W, W)),
                  _resident((W, D_MODEL)), _resident((MLSTM_WIDTH, D_MODEL)),
                  _resident((D_MODEL, D_MODEL)), _resident((4, D_MODEL)),
                  _resident((D_MODEL, 2 * D_FF))],
        out_specs=[tile(D_MODEL), tile(2 * D_FF)],
        out_shape=[jax.ShapeDtypeStruct((n, D_MODEL), F32), jax.ShapeDtypeStruct((n, 2 * D_FF), F32)],
        compiler_params=_params(("arbitrary",)),
        name="merge_ffn_up",
    )(x2, mod, ysf, ysb, bonus, gate, hf, hb, z, z, p["lnx_g"], p["lnx_b"], p["gn_g"], p["pmean"],
      p["w_br"], p["w_bm"], p["w_out"], p["norm_g"], p["ffn_up"])


def _down_kernel(x1_ref, mod_ref, a_ref, w_ref, ng_ref, o_ref):
    g2 = mod_ref[0][:, 5 * D_MODEL:6 * D_MODEL]
    f = _dg(a_ref[...], w_ref[...], _NN)
    o_ref[...] = x1_ref[...] + g2 * _rms(f, ng_ref[3:4, :])


def _down(x1, mod, mod_row, act, p):
    n = x1.shape[0]
    return pl.pallas_call(
        _down_kernel,
        grid=(n // TOK_TILE,),
        in_specs=[pl.BlockSpec((TOK_TILE, D_MODEL), lambda i: (i, 0)),
                  pl.BlockSpec((1, 1, 6 * D_MODEL), lambda i: (mod_row(i), 0, 0)),
                  pl.BlockSpec((TOK_TILE, D_FF), lambda i: (i, 0)),
                  _resident((D_FF, D_MODEL)), _resident((4, D_MODEL))],
        out_specs=pl.BlockSpec((TOK_TILE, D_MODEL), lambda i: (i, 0)),
        out_shape=jax.ShapeDtypeStruct((n, D_MODEL), F32),
        compiler_params=_params(("arbitrary",)),
        name="ffn_down",
    )(x1, mod, act, p["ffn_down"], p["norm_g"])


RWKV_LOCAL_PASSES = 1
RWKV_SCAN_PASSES = 3


def _state_to_pairs(s):
    b = s.shape[0]
    s = s.reshape(b, N_DIR, RWKV_PAIRS, 2, RWKV_HEAD_DIM, RWKV_HEAD_DIM)
    zero = jnp.zeros_like(s[:, :, :, 0])
    top = jnp.concatenate([s[:, :, :, 0], zero], axis=-1)
    bot = jnp.concatenate([zero, s[:, :, :, 1]], axis=-1)
    return jnp.concatenate([top, bot], axis=-2)


def _state_from_pairs(sb):
    b = sb.shape[0]
    n = RWKV_HEAD_DIM
    parts = jnp.stack([sb[..., 0:n, 0:n], sb[..., n:2 * n, n:2 * n]], axis=3)
    return parts.reshape(b, N_DIR, RWKV_HEADS, n, n)


def _trunk(x, mod, mod_row, rows, states, p):
    batch, seq_len, _ = x.shape
    n = batch * seq_len
    x2 = x.reshape(n, D_MODEL)
    s0, c0, n0, m0 = states

    z = _in_proj(x2, mod, mod_row, p["norm_g"][0:1], p["w_r"], p["w_g"], p["w_m"], p["w_s"])

    rp, y0, gm, hm, gate, bonus = _rwkv_local(z, seq_len, p, RWKV_LOCAL_PASSES)
    ysf, ysb, s_fin = _rwkv_scan(_state_to_pairs(s0), rp, y0, gm, hm, batch, seq_len, RWKV_SCAN_PASSES)
    s_fin = _state_from_pairs(s_fin)

    qk = _qk_conv(z, batch, seq_len, rows, p["mlstm_conv"])
    gt = z[:, ZG_OFF:ZG_OFF + MLSTM_GATES].reshape(n // MLSTM_CHUNK, MLSTM_CHUNK, MLSTM_GATES).transpose(0, 2, 1)
    nst = N_DIR * MLSTM_HEADS
    kt = qk[:, MLSTM_WIDTH:].reshape(n // MLSTM_CHUNK, MLSTM_CHUNK, MLSTM_WIDTH).transpose(0, 2, 1)
    hf, hb, ct_fin, n_fin, m_fin = _mlstm_scan(
        z, qk, kt, gt, p["gate_bc"], p["gate_br"],
        jnp.swapaxes(c0, -1, -2).reshape(batch, nst, MLSTM_HEAD_DIM, MLSTM_HEAD_DIM),
        n0.reshape(batch, nst, MLSTM_HEAD_DIM),
        jnp.broadcast_to(m0.reshape(batch, nst, 1), (batch, nst, LANE)), batch, seq_len)
    c_fin = jnp.swapaxes(ct_fin, -1, -2)

    x1, u = _merge(x2, mod, mod_row, z, ysf, ysb, bonus, gate, hf, hb, p)
    act = _ffn_conv(u, batch, seq_len, rows, p["ffn_conv"], p["ffn_conv_b"])
    out = _down(x1, mod, mod_row, act, p)

    new_states = (s_fin,
                  c_fin.reshape(batch, N_DIR, MLSTM_HEADS, MLSTM_HEAD_DIM, MLSTM_HEAD_DIM),
                  n_fin.reshape(batch, N_DIR, MLSTM_HEADS, MLSTM_HEAD_DIM),
                  m_fin[:, :, 0].reshape(batch, N_DIR, MLSTM_HEADS))
    return out.reshape(batch, seq_len, D_MODEL), new_states


def _pack_layer(l, ada_w, ada_b, norm_g, w_in, rwkv_mu, rwkv_w0, rwkv_w_up, rwkv_a0, rwkv_a_up,
                rwkv_g_up, rwkv_kk_scale, rwkv_k_a, rwkv_r_k, rwkv_lnx_g, rwkv_lnx_b, mlstm_conv,
                mlstm_gate_b, mlstm_gn_g, w_branch_rwkv, w_branch_mlstm, w_out, ffn_up, ffn_conv,
                ffn_conv_b, ffn_down):
    W = RWKV_WIDTH
    wi = w_in[l]
    w_r = wi[:, 0:RWKV_COLS].astype(BF16)
    w_m = wi[:, RWKV_COLS:RWKV_COLS + 4 * MLSTM_WIDTH].astype(BF16)
    w_g = jnp.pad(wi[:, RWKV_COLS + 4 * MLSTM_WIDTH:RWKV_COLS + MLSTM_COLS].astype(BF16),
                  ((0, 0), (0, LANE - MLSTM_GATES)))
    w_s = wi[:, RWKV_COLS + MLSTM_COLS:].astype(BF16)

    head = jnp.arange(W, dtype=jnp.int32) // RWKV_HEAD_DIM
    same = (head[:, None] == head[None, :])
    gb = mlstm_gate_b[l].reshape(1, MLSTM_GATES)
    return dict(
        ada_w=ada_w[l], ada_b=ada_b[l], norm_g=norm_g[l], w_r=w_r, w_g=w_g, w_m=w_m, w_s=w_s,
        mu=rwkv_mu[l].reshape(1, RWKV_COLS),
        w0=rwkv_w0[l].reshape(N_DIR, 1, W), w_up=rwkv_w_up[l],
        a0=rwkv_a0[l].reshape(N_DIR, 1, W), a_up=rwkv_a_up[l], g_up=rwkv_g_up[l],
        kk_scale=rwkv_kk_scale[l].reshape(1, W), k_a=rwkv_k_a[l].reshape(1, W),
        r_k=rwkv_r_k[l].reshape(1, W),
        lnx_g=rwkv_lnx_g[l].reshape(1, W), lnx_b=rwkv_lnx_b[l].reshape(1, W),
        pones=same.astype(BF16), pmean=(same.astype(F32) / RWKV_HEAD_DIM).astype(BF16),
        mlstm_conv=mlstm_conv[l].reshape(9, 2 * MLSTM_WIDTH),
        gate_bc=jnp.pad(gb, ((0, 0), (0, LANE - MLSTM_GATES))), gate_br=gb.reshape(MLSTM_GATES, 1),
        gn_g=mlstm_gn_g[l].reshape(1, MLSTM_WIDTH),
        w_br=w_branch_rwkv[l].astype(BF16), w_bm=w_branch_mlstm[l].astype(BF16),
        w_out=w_out[l].astype(BF16), ffn_up=ffn_up[l].astype(BF16),
        ffn_conv=ffn_conv[l].reshape(9, D_FF), ffn_conv_b=ffn_conv_b[l].reshape(1, D_FF),
        ffn_down=ffn_down[l].astype(BF16),
    )


def kernel(x_prompt, x_sample, c, state_rwkv, state_mlstm_C, state_mlstm_n, state_mlstm_m, c_ctx,
           ada_w, ada_b, norm_g, w_in, rwkv_mu, rwkv_w0, rwkv_w_up, rwkv_a0, rwkv_a_up, rwkv_g_up,
           rwkv_kk_scale, rwkv_k_a, rwkv_r_k, rwkv_lnx_g, rwkv_lnx_b, mlstm_conv, mlstm_gate_b,
           mlstm_gn_g, w_branch_rwkv, w_branch_mlstm, w_out, ffn_up, ffn_conv, ffn_conv_b, ffn_down):
    depth = ada_w.shape[0]
    batch = x_prompt.shape[0]
    dec_batch, dec_seq, _ = x_sample.shape
    latent_rows = dec_seq // GRID_W
    tiles_per_latent = dec_seq // TOK_TILE
    ctx_init = (jnp.zeros((batch, N_DIR, RWKV_HEADS, RWKV_HEAD_DIM, RWKV_HEAD_DIM), F32),
                jnp.zeros((batch, N_DIR, MLSTM_HEADS, MLSTM_HEAD_DIM, MLSTM_HEAD_DIM), F32),
                jnp.zeros((batch, N_DIR, MLSTM_HEADS, MLSTM_HEAD_DIM), F32),
                jnp.zeros((batch, N_DIR, MLSTM_HEADS), F32))
    cond = jnp.concatenate([c_ctx[None, :], c, jnp.zeros((8 - 1 - dec_batch, D_MODEL), F32)], axis=0)

    xp, xs = x_prompt, x_sample
    new_s, new_c, new_n, new_m = [], [], [], []
    for l in range(depth):
        p = _pack_layer(l, ada_w, ada_b, norm_g, w_in, rwkv_mu, rwkv_w0, rwkv_w_up, rwkv_a0, rwkv_a_up,
                        rwkv_g_up, rwkv_kk_scale, rwkv_k_a, rwkv_r_k, rwkv_lnx_g, rwkv_lnx_b, mlstm_conv,
                        mlstm_gate_b, mlstm_gn_g, w_branch_rwkv, w_branch_mlstm, w_out, ffn_up, ffn_conv,
                        ffn_conv_b, ffn_down)
        mod = _ada(cond, p["ada_w"], p["ada_b"]).reshape(8, 1, 6 * D_MODEL)
        xp, (s, cc, nn, mm) = _trunk(xp, mod, lambda i: 0, 1, ctx_init, p)
        new_s.append(s)
        new_c.append(cc)
        new_n.append(nn)
        new_m.append(mm)
        xs, _ = _trunk(xs, mod, lambda i: 1 + i // tiles_per_latent, latent_rows,
                       (state_rwkv[:, l], state_mlstm_C[:, l], state_mlstm_n[:, l], state_mlstm_m[:, l]), p)
    return (xp, xs, jnp.stack(new_s, axis=1), jnp.stack(new_c, axis=1),
            jnp.stack(new_n, axis=1), jnp.stack(new_m, axis=1))
```

```python
import functools

import jax
import jax.numpy as jnp
from jax import lax
from jax.experimental import pallas as pl
from jax.experimental.pallas import tpu as pltpu

F32 = jnp.float32
BF16 = jnp.bfloat16

D_MODEL = 1024
N_DIR = 2
RWKV_HEADS = 8
RWKV_HEAD_DIM = 64
RWKV_WIDTH = RWKV_HEADS * RWKV_HEAD_DIM
DECAY_LORA = 64
ICLR_LORA = 64
GATE_LORA = 128
MLSTM_HEADS = 4
MLSTM_HEAD_DIM = 128
MLSTM_WIDTH = MLSTM_HEADS * MLSTM_HEAD_DIM
MLSTM_CHUNK = 64
D_FF = 2816
GRID_W = 64
RMS_EPS = 1e-6
RWKV_GN_EPS = 64e-5
MLSTM_GN_EPS = 1e-5
DECAY_SCALE = 0.606531

RWKV_COLS = 3 * RWKV_WIDTH + N_DIR * DECAY_LORA + N_DIR * ICLR_LORA + GATE_LORA
MLSTM_GATES = 2 * N_DIR * MLSTM_HEADS
MLSTM_COLS = 4 * MLSTM_WIDTH + MLSTM_GATES
GATE_COLS = 2 * D_MODEL

LANE = 128
ZR_BLOCK = 2048
ZG_OFF = RWKV_COLS
ZM_OFF = ZR_BLOCK
ZS_OFF = ZM_OFF + 4 * MLSTM_WIDTH
Z_COLS = ZS_OFF + GATE_COLS

TOK_TILE = 256
RCHUNK = 64
CONV_CH_TILE = 256
VMEM_LIMIT = 56 * 1024 * 1024


def _params(sem):
    return pltpu.CompilerParams(dimension_semantics=sem, vmem_limit_bytes=VMEM_LIMIT)


def _resident(shape):
    nd = len(shape)
    return pl.BlockSpec(shape, lambda *_: (0,) * nd, pipeline_mode=pl.Buffered(1))


def _split2(a):
    hi = a.astype(BF16)
    lo = (a - hi.astype(F32)).astype(BF16)
    return hi, lo


def _split3(a):
    hi = a.astype(BF16)
    r1 = a - hi.astype(F32)
    mid = r1.astype(BF16)
    lo = (r1 - mid.astype(F32)).astype(BF16)
    return hi, mid, lo


def _dg(a, b, dims):
    return lax.dot_general(a, b, dims, preferred_element_type=F32)


def _mm(a, b, dims, passes):
    if passes == 1:
        return _dg(a.astype(BF16), b.astype(BF16), dims)
    ah, al = _split2(a)
    bh, bl = _split2(b)
    return _dg(ah, bh, dims) + (_dg(ah, bl, dims) + _dg(al, bh, dims))


def _mm_exact_lhs(a_bf16, b, dims):
    b1, b2, b3 = _split3(b)
    return _dg(a_bf16, b1, dims) + (_dg(a_bf16, b2, dims) + _dg(a_bf16, b3, dims))


def _mm_exact_rhs(a, b_bf16, dims):
    a1, a2, a3 = _split3(a)
    return _dg(a1, b_bf16, dims) + (_dg(a2, b_bf16, dims) + _dg(a3, b_bf16, dims))


_NN = (((1,), (0,)), ((), ()))
_NT = (((1,), (1,)), ((), ()))
_TN = (((0,), (0,)), ((), ()))
_BNN = (((2,), (1,)), ((0,), (0,)))
_BNT = (((2,), (2,)), ((0,), (0,)))
_BTN = (((1,), (1,)), ((0,), (0,)))


def _sigmoid(x):
    return jax.nn.sigmoid(x)


def _silu(x):
    return x * jax.nn.sigmoid(x)


def _rms(x, g):
    return x * lax.rsqrt(jnp.mean(x * x, axis=-1, keepdims=True) + RMS_EPS) * g


def _ada_kernel(cond_ref, w_ref, b_ref, o_ref):
    s = _silu(cond_ref[...])
    o_ref[...] = _dg(s.astype(BF16), w_ref[...].astype(BF16), _NN) + b_ref[...]


def _ada(cond8, ada_w, ada_b):
    n = ada_w.shape[1]
    tn = 1536
    return pl.pallas_call(
        _ada_kernel,
        grid=(n // tn,),
        in_specs=[_resident((8, D_MODEL)),
                  pl.BlockSpec((D_MODEL, tn), lambda j: (0, j)),
                  pl.BlockSpec((1, tn), lambda j: (0, j))],
        out_specs=pl.BlockSpec((8, tn), lambda j: (0, j)),
        out_shape=jax.ShapeDtypeStruct((8, n), F32),
        compiler_params=_params(("arbitrary",)),
        name="ada_mod",
    )(cond8, ada_w, ada_b.reshape(1, n))


def _in_kernel(conv_width, x_ref, mod_ref, g_ref, wr_ref, wg_ref, wm_ref, ws_ref, cw_ref, z_ref):
    mod = mod_ref[0]
    sh = mod[:, 0:D_MODEL]
    sc = mod[:, D_MODEL:2 * D_MODEL]
    h = (_rms(x_ref[...], g_ref[...]) * (1.0 + sc) + sh).astype(BF16)
    z_ref[:, 0:ZG_OFF] = _dg(h, wr_ref[...], _NN)
    z_ref[:, ZG_OFF:ZM_OFF] = _dg(h, wg_ref[...], _NN)
    zm = _dg(h, wm_ref[...], _NN)
    if conv_width is None:
        z_ref[:, ZM_OFF:ZS_OFF] = zm
    else:
        qk_cols = 2 * MLSTM_WIDTH
        z_ref[:, ZM_OFF:ZM_OFF + qk_cols] = _silu(_dwconv(zm[:, 0:qk_cols], cw_ref, conv_width, False))
        z_ref[:, ZM_OFF + qk_cols:ZS_OFF] = zm[:, qk_cols:]
    z_ref[:, ZS_OFF:Z_COLS] = _sigmoid(_dg(h, ws_ref[...], _NN))


def _in_proj(x2, mod, mod_row, norm_g0, w_r, w_g, w_m, w_s, conv_w9, conv_width):
    n = x2.shape[0]
    return pl.pallas_call(
        functools.partial(_in_kernel, conv_width),
        grid=(n // TOK_TILE,),
        in_specs=[pl.BlockSpec((TOK_TILE, D_MODEL), lambda i: (i, 0)),
                  pl.BlockSpec((1, 1, 6 * D_MODEL), lambda i: (mod_row(i), 0, 0)),
                  _resident((1, D_MODEL)),
                  _resident(w_r.shape), _resident(w_g.shape), _resident(w_m.shape), _resident(w_s.shape),
                  _resident(conv_w9.shape)],
        out_specs=pl.BlockSpec((TOK_TILE, Z_COLS), lambda i: (i, 0)),
        out_shape=jax.ShapeDtypeStruct((n, Z_COLS), F32),
        compiler_params=_params(("arbitrary",)),
        name="in_proj",
    )(x2, mod, norm_g0, w_r, w_g, w_m, w_s, conv_w9)


LOCAL_CHUNKS = 4
PAIR_LANES = 2 * RWKV_HEAD_DIM
RWKV_PAIRS = RWKV_HEADS // 2


def _bd(x):
    lane = lax.broadcasted_iota(jnp.int32, x.shape, 1)
    left = lane < RWKV_HEAD_DIM
    return jnp.concatenate([jnp.where(left, x, 0.0), jnp.where(left, 0.0, x)], axis=0)


def _rwkv_local_kernel(chunks_per_seq, passes,
                       z_ref, zp_ref, zn_ref, mu_ref, w0_ref, wup_ref, a0_ref, aup_ref, gup_ref,
                       kks_ref, ka_ref, rk_ref, pones_ref,
                       rp_ref, y0_ref, gm_ref, hm_ref, gate_ref, bonus_ref):
    C = RCHUNK
    W = RWKV_WIDTH
    NS = LOCAL_CHUNKS
    R = NS * C
    first = (pl.program_id(0) * NS) % chunks_per_seq
    has_prev = first != 0
    has_next = first + NS != chunks_per_seq

    z = z_ref[:, 0:RWKV_COLS]
    zp = jnp.where(has_prev, zp_ref[7:8, 0:RWKV_COLS], 0.0)
    zn = jnp.where(has_next, zn_ref[0:1, 0:RWKV_COLS], 0.0)
    trow = lax.broadcasted_iota(jnp.int32, (R, 1), 0)
    prev = jnp.where(trow == 0, zp, pltpu.roll(z, 1, 0))
    nxt = jnp.where(trow == R - 1, zn, pltpu.roll(z, R - 1, 0))
    zs = z + mu_ref[...] * (0.5 * (prev + nxt) - z)

    r = zs[:, 0:W]
    k = zs[:, W:2 * W]
    v = zs[:, 2 * W:3 * W]
    gd = zs[:, 3 * W + 2 * DECAY_LORA + 2 * ICLR_LORA:RWKV_COLS]
    gate_ref[...] = _dg(_sigmoid(gd).astype(BF16), gup_ref[...].astype(BF16), _NN)

    pones = pones_ref[...]
    kks = k * kks_ref[...]
    norm = jnp.sqrt(_mm_exact_rhs(kks * kks, pones, _NN))
    kk = kks / jnp.maximum(norm, 1e-12)

    P = PAIR_LANES
    row = lax.broadcasted_iota(jnp.int32, (R, R), 0)
    col = lax.broadcasted_iota(jnp.int32, (R, R), 1)
    same_chunk = jnp.bitwise_and(row, -C) == jnp.bitwise_and(col, -C)
    prow = lax.broadcasted_iota(jnp.int32, (C, P), 0)
    pcol = jnp.bitwise_and(lax.broadcasted_iota(jnp.int32, (C, P), 1), RWKV_HEAD_DIM - 1)
    eye_p = jnp.where(prow == pcol, 1.0, 0.0)
    brow = lax.broadcasted_iota(jnp.int32, (P, P), 0)
    bcol = lax.broadcasted_iota(jnp.int32, (P, P), 1)
    same_head = (brow < RWKV_HEAD_DIM) == (bcol < RWKV_HEAD_DIM)
    eye_b = jnp.where(brow == bcol, 1.0, 0.0)

    abar, rbar, kt, bt, kw, bw, wc, strict, incl = [], [], [], [], [], [], [], [], []
    bonus = None
    for d in range(N_DIR):
        o = 3 * W + d * DECAY_LORA
        wd = zs[:, o:o + DECAY_LORA]
        o = 3 * W + 2 * DECAY_LORA + d * ICLR_LORA
        ad = zs[:, o:o + ICLR_LORA]
        logw = -DECAY_SCALE * _sigmoid(w0_ref[d] + _dg(jnp.tanh(wd).astype(BF16), wup_ref[d].astype(BF16), _NN))
        a = _sigmoid(a0_ref[d] + _dg(ad.astype(BF16), aup_ref[d].astype(BF16), _NN))
        kd = k * (1.0 + (a - 1.0) * ka_ref[...])
        b = kk * a
        bonus_d = _mm_exact_rhs(r * kd * rk_ref[...], pones, _NN) * v
        bonus = bonus_d if bonus is None else bonus + bonus_d

        earlier_or_same = same_chunk & ((row >= col) if d == 0 else (row <= col))
        cum_i = _mm_exact_lhs(jnp.where(earlier_or_same, 1.0, 0.0).astype(BF16), logw, _NN)
        cum_e = cum_i - logw
        ab_d, rb_d, kt_d, bt_d, kw_d, bw_d, wc_d = [], [], [], [], [], [], []
        for s in range(NS):
            rs = slice(s * C, (s + 1) * C)
            ci_s = cum_i[rs]
            ctot = jnp.sum(logw[rs], axis=0, keepdims=True)
            e_ni = jnp.exp(-ci_s)
            e_ti = jnp.exp(ctot - ci_s)
            ab_d.append(kk[rs] * jnp.exp(cum_e[rs]))
            rb_d.append(r[rs] * jnp.exp(ci_s))
            kt_d.append(kd[rs] * e_ni)
            bt_d.append(b[rs] * e_ni)
            kw_d.append(kd[rs] * e_ti)
            bw_d.append(b[rs] * e_ti)
            wc_d.append(jnp.exp(ctot))
        abar.append(ab_d)
        rbar.append(rb_d)
        kt.append(kt_d)
        bt.append(bt_d)
        kw.append(kw_d)
        bw.append(bw_d)
        wc.append(wc_d)
        strict.append((prow > pcol) if d == 0 else (prow < pcol))
        incl.append((prow >= pcol) if d == 0 else (prow <= pcol))
    bonus_ref[...] = bonus

    mm = functools.partial(_mm, passes=passes)
    chains = [(s, d, p) for s in range(NS) for d in range(N_DIR) for p in range(RWKV_PAIRS)]
    nch = range(len(chains))

    def sel(arr, i):
        s, d, p = chains[i]
        return arr[d][s][:, p * P:(p + 1) * P]

    cat0 = lambda a_, b_: jnp.concatenate([a_, b_], axis=0)
    cat1 = lambda a_, b_: jnp.concatenate([a_, b_], axis=1)
    vsl = [v[s * C:(s + 1) * C, p * P:(p + 1) * P] for s, _, p in chains]
    lhs = [cat0(sel(abar, i), sel(rbar, i)) for i in nch]
    by_b = [mm(lhs[i], _bd(sel(bt, i)), _NT) for i in nch]
    by_k = [mm(lhs[i], _bd(sel(kt, i)), _NT) for i in nch]
    a_kk = [jnp.where(strict[chains[i][1]], by_b[i][0:C], 0.0) for i in nch]
    a_rb = [jnp.where(incl[chains[i][1]], by_b[i][C:2 * C], 0.0) for i in nch]
    a_kv = [jnp.where(strict[chains[i][1]], by_k[i][0:C], 0.0) for i in nch]
    a_rk = [jnp.where(incl[chains[i][1]], by_k[i][C:2 * C], 0.0) for i in nch]
    on_v = [mm(cat0(a_kv[i], a_rk[i]), _bd(vsl[i]), _NN) for i in nch]

    x = [-m for m in a_kk]
    tinv = [eye_p + m for m in x]
    x = [mm(m, _bd(m), _NN) for m in x]
    for _ in range(4):
        both = [mm(cat0(tinv[i], x[i]), _bd(x[i]), _NN) for i in nch]
        tinv = [tinv[i] + both[i][0:C] for i in nch]
        x = [m[C:2 * C] for m in both]
    tinv = [tinv[i] + mm(tinv[i], _bd(x[i]), _NN) for i in nch]

    solved = [mm(tinv[i], cat1(_bd(sel(abar, i)), _bd(on_v[i][0:C])), _NN) for i in nch]
    ap = [m[:, 0:P] for m in solved]
    u0 = [m[:, P:2 * P] for m in solved]
    corr = [mm(a_rb[i], cat1(_bd(ap[i]), _bd(u0[i])), _NN) for i in nch]
    on_b = [mm(cat1(ap[i], u0[i]), sel(bw, i), _TN) for i in nch]
    vk = [mm(vsl[i], sel(kw, i), _TN) for i in nch]
    for i in nch:
        s, d, p = chains[i]
        rows = slice(s * C, (s + 1) * C)
        lanes = slice(p * P, (p + 1) * P)
        rp_ref[d, rows, lanes] = (sel(rbar, i) - corr[i][:, 0:P]).astype(BF16)
        y0_ref[d, rows, lanes] = on_v[i][C:2 * C] - corr[i][:, P:2 * P]
        gm_ref[d, s, p] = (eye_b * sel(wc, i) - jnp.where(same_head, on_b[i][0:P], 0.0)).astype(BF16)
        hm_ref[d, s, p] = jnp.where(same_head, vk[i] - on_b[i][P:2 * P], 0.0)


def _rwkv_local(z, seq_len, p, passes):
    n = z.shape[0]
    nchunk = n // RCHUNK
    cps = seq_len // RCHUNK
    assert cps % LOCAL_CHUNKS == 0
    W = RWKV_WIDTH
    rows = LOCAL_CHUNKS * RCHUNK
    hb = rows // 8
    last8 = n // 8 - 1
    mat = lambda dt: jax.ShapeDtypeStruct((N_DIR, nchunk, RWKV_PAIRS, PAIR_LANES, PAIR_LANES), dt)
    mat_spec = pl.BlockSpec((N_DIR, LOCAL_CHUNKS, RWKV_PAIRS, PAIR_LANES, PAIR_LANES),
                            lambda c: (0, c, 0, 0, 0))
    tok = lambda dt: jax.ShapeDtypeStruct((N_DIR, n, W), dt)
    tok_spec = pl.BlockSpec((N_DIR, rows, W), lambda c: (0, c, 0))
    row_spec = pl.BlockSpec((rows, W), lambda c: (c, 0))
    return pl.pallas_call(
        functools.partial(_rwkv_local_kernel, cps, passes),
        grid=(nchunk // LOCAL_CHUNKS,),
        in_specs=[pl.BlockSpec((rows, ZR_BLOCK), lambda c: (c, 0)),
                  pl.BlockSpec((8, ZR_BLOCK), lambda c: (jnp.maximum(c * hb - 1, 0), 0)),
                  pl.BlockSpec((8, ZR_BLOCK), lambda c: (jnp.minimum((c + 1) * hb, last8), 0)),
                  _resident((1, RWKV_COLS)),
                  _resident((N_DIR, 1, W)), _resident((N_DIR, DECAY_LORA, W)),
                  _resident((N_DIR, 1, W)), _resident((N_DIR, ICLR_LORA, W)),
                  _resident((GATE_LORA, W)),
                  _resident((1, W)), _resident((1, W)), _resident((1, W)),
                  _resident((W, W))],
        out_specs=[tok_spec, tok_spec, mat_spec, mat_spec, row_spec, row_spec],
        out_shape=[tok(BF16), tok(F32), mat(BF16), mat(F32),
                   jax.ShapeDtypeStruct((n, W), F32), jax.ShapeDtypeStruct((n, W), F32)],
        compiler_params=_params(("arbitrary",)),
        name="rwkv_local",
    )(z, z, z, p["mu"], p["w0"], p["w_up"], p["a0"], p["a_up"], p["g_up"],
      p["kk_scale"], p["k_a"], p["r_k"], p["pones"])


SCAN_CHUNKS = 4


def _rwkv_scan_kernel(s0_ref, rpf_ref, rpb_ref, y0f_ref, y0b_ref, gmf_ref, gmb_ref, hmf_ref, hmb_ref,
                      ysf_ref, ysb_ref, sout_ref, s_scr):
    @pl.when(pl.program_id(1) == 0)
    def _():
        s_scr[...] = s0_ref[0]

    K = SCAN_CHUNKS
    C = RCHUNK
    rp_ref, y0_ref, gm_ref, hm_ref, ys_ref = ((rpf_ref, rpb_ref), (y0f_ref, y0b_ref), (gmf_ref, gmb_ref),
                                              (hmf_ref, hmb_ref), (ysf_ref, ysb_ref))
    chains = [(d, p) for d in range(N_DIR) for p in range(RWKV_PAIRS)]
    lanes = [slice(p * PAIR_LANES, (p + 1) * PAIR_LANES) for _, p in chains]
    nch = range(len(chains))
    s = [s_scr[d, p] for d, p in chains]
    for j in range(K):
        at = (j, K - 1 - j)
        rows = [slice(at[d] * C, (at[d] + 1) * C) for d, _ in chains]
        sb = [m.astype(BF16) for m in s]
        y = [_dg(rp_ref[chains[i][0]][0, rows[i], lanes[i]], sb[i], _NT) for i in nch]
        sg = [_dg(sb[i], gm_ref[chains[i][0]][0, at[chains[i][0]], chains[i][1]], _NN) for i in nch]
        for i in nch:
            d, p = chains[i]
            ys_ref[d][rows[i], lanes[i]] = y[i] + y0_ref[d][0, rows[i], lanes[i]]
        s = [sg[i] + hm_ref[chains[i][0]][0, at[chains[i][0]], chains[i][1]] for i in nch]
    for i in nch:
        d, p = chains[i]
        s_scr[d, p] = s[i]
        sout_ref[0, d, p] = s[i]


def _rwkv_scan(s0, rp, y0, gm, hm, batch, seq_len):
    K = SCAN_CHUNKS
    spb = seq_len // (RCHUNK * K)
    n = batch * seq_len

    def fwd(b, s):
        return b * spb + s

    def bwd(b, s):
        return b * spb + spb - 1 - s

    def mat_spec(d, at):
        return pl.BlockSpec((1, K, RWKV_PAIRS, PAIR_LANES, PAIR_LANES), lambda b, s: (d, at(b, s), 0, 0, 0))

    def tok_spec(d, at):
        return pl.BlockSpec((1, K * RCHUNK, RWKV_WIDTH), lambda b, s: (d, at(b, s), 0))

    st_spec = pl.BlockSpec((1, N_DIR, RWKV_PAIRS, PAIR_LANES, PAIR_LANES), lambda b, s: (b, 0, 0, 0, 0))
    ys = jax.ShapeDtypeStruct((n, RWKV_WIDTH), F32)
    return pl.pallas_call(
        _rwkv_scan_kernel,
        grid=(batch, spb),
        in_specs=[st_spec, tok_spec(0, fwd), tok_spec(1, bwd), tok_spec(0, fwd), tok_spec(1, bwd),
                  mat_spec(0, fwd), mat_spec(1, bwd), mat_spec(0, fwd), mat_spec(1, bwd)],
        out_specs=[pl.BlockSpec((K * RCHUNK, RWKV_WIDTH), lambda b, s: (fwd(b, s), 0)),
                   pl.BlockSpec((K * RCHUNK, RWKV_WIDTH), lambda b, s: (bwd(b, s), 0)),
                   st_spec],
        out_shape=[ys, ys,
                   jax.ShapeDtypeStruct((batch, N_DIR, RWKV_PAIRS, PAIR_LANES, PAIR_LANES), F32)],
        scratch_shapes=[pltpu.VMEM((N_DIR, RWKV_PAIRS, PAIR_LANES, PAIR_LANES), F32)],
        compiler_params=_params(("arbitrary", "arbitrary")),
        name="rwkv_scan",
    )(s0, rp, rp, y0, y0, gm, gm, hm, hm)


CONV_BLOCK_ROWS = 2048


def _dwconv(x, w_ref, width, vertical):
    T = x.shape[0]
    t = lax.broadcasted_iota(jnp.int32, (T, 1), 0)
    assert width & (width - 1) == 0
    colp = jnp.bitwise_and(t, width - 1)
    xl = jnp.where(colp == 0, 0.0, pltpu.roll(x, 1, 0))
    xr = jnp.where(colp == width - 1, 0.0, pltpu.roll(x, T - 1, 0))

    def tap_row(i):
        return w_ref[3 * i:3 * i + 1, :] * xl + w_ref[3 * i + 1:3 * i + 2, :] * x + w_ref[3 * i + 2:3 * i + 3, :] * xr

    out = tap_row(1)
    if vertical:
        out = out + jnp.where(t < width, 0.0, pltpu.roll(tap_row(0), width, 0))
        out = out + jnp.where(t >= T - width, 0.0, pltpu.roll(tap_row(2), T - width, 0))
    return out


def _conv_geometry(n, seq_len, rows):
    if rows > 1:
        return seq_len, seq_len // rows, True
    block = CONV_BLOCK_ROWS if (n % CONV_BLOCK_ROWS == 0 and CONV_BLOCK_ROWS % seq_len == 0) else seq_len
    return block, seq_len, False


def _qk_conv_kernel(width, vertical, x_ref, w_ref, o_ref):
    o_ref[...] = _silu(_dwconv(x_ref[...], w_ref, width, vertical))


def _qk_conv(z, batch, seq_len, rows, conv_w9):
    n = batch * seq_len
    ch = 2 * MLSTM_WIDTH
    tc = CONV_CH_TILE
    off = ZM_OFF // tc
    block, width, vertical = _conv_geometry(n, seq_len, rows)
    return pl.pallas_call(
        functools.partial(_qk_conv_kernel, width, vertical),
        grid=(n // block, ch // tc),
        in_specs=[pl.BlockSpec((block, tc), lambda b, j: (b, off + j)),
                  pl.BlockSpec((9, tc), lambda b, j: (0, j))],
        out_specs=pl.BlockSpec((block, tc), lambda b, j: (b, j)),
        out_shape=jax.ShapeDtypeStruct((n, ch), F32),
        compiler_params=_params(("arbitrary", "arbitrary")),
        name="mlstm_qk_conv",
    )(z, conv_w9)


def _ffn_conv_kernel(width, vertical, ua_ref, uv_ref, w_ref, b_ref, o_ref):
    act = _dwconv(ua_ref[...], w_ref, width, vertical) + b_ref[...]
    o_ref[...] = (_silu(act) * uv_ref[...]).astype(BF16)


def _ffn_conv(u, batch, seq_len, rows, conv_w9, conv_b):
    n = batch * seq_len
    tc = CONV_CH_TILE
    nct = D_FF // tc
    block, width, vertical = _conv_geometry(n, seq_len, rows)
    return pl.pallas_call(
        functools.partial(_ffn_conv_kernel, width, vertical),
        grid=(n // block, nct),
        in_specs=[pl.BlockSpec((block, tc), lambda b, j: (b, j)),
                  pl.BlockSpec((block, tc), lambda b, j: (b, nct + j)),
                  pl.BlockSpec((9, tc), lambda b, j: (0, j)),
                  pl.BlockSpec((1, tc), lambda b, j: (0, j))],
        out_specs=pl.BlockSpec((block, tc), lambda b, j: (b, j)),
        out_shape=jax.ShapeDtypeStruct((n, D_FF), BF16),
        compiler_params=_params(("arbitrary", "arbitrary")),
        name="ffn_conv",
    )(u, u, conv_w9, conv_b)


MLSTM_STEP_CHUNKS = 4

def _mlstm_scan_kernel(qkf_ref, qkb_ref, ktf_ref, ktb_ref, vf_ref, vb_ref, gcf_ref, gcb_ref, grf_ref, grb_ref,
                       gbc_ref, gbr_ref, c0_ref, n0_ref, m0_ref,
                       hf_ref, hb_ref, cout_ref, nout_ref, mout_ref,
                       c_scr, n_scr, m_scr):
    step = pl.program_id(1)
    L = MLSTM_CHUNK
    dh = MLSTM_HEAD_DIM
    H = MLSTM_HEADS

    @pl.when(step == 0)
    def _():
        c_scr[...] = c0_ref[0]
        n_scr[...] = n0_ref[0]
        m_scr[...] = m0_ref[0]

    K = MLSTM_STEP_CHUNKS
    R = K * L
    row = lax.broadcasted_iota(jnp.int32, (L, L), 0)
    col = lax.broadcasted_iota(jnp.int32, (L, L), 1)
    lower = (row >= col)
    upper = (row <= col)
    lower_b = jnp.where(lower, 1.0, 0.0).astype(BF16)
    upper_b = jnp.where(upper, 1.0, 0.0).astype(BF16)
    rrow = lax.broadcasted_iota(jnp.int32, (R, R), 0)
    rcol = lax.broadcasted_iota(jnp.int32, (R, R), 1)
    same_chunk = jnp.bitwise_and(rrow, -L) == jnp.bitwise_and(rcol, -L)
    neg_inf = jnp.full((), -jnp.inf, F32)

    gcol, grow, bcol, brow, btot = [], [], [], [], []
    ones_b = jnp.ones((L, LANE), BF16)
    for d in range(N_DIR):
        gc_ref, gr_ref = (gcf_ref, grf_ref) if d == 0 else (gcb_ref, grb_ref)
        gcol.append(gc_ref[...] + gbc_ref[...])
        grow.append((gr_ref[...] + gbr_ref[...][None]).reshape(K * MLSTM_GATES, L))
        before = same_chunk & ((rrow >= rcol) if d == 0 else (rrow <= rcol))
        bcol.append(_mm_exact_lhs(jnp.where(before, 1.0, 0.0).astype(BF16), jax.nn.log_sigmoid(gcol[d]), _NN))
        frow = jax.nn.log_sigmoid(grow[d])
        brow.append(_mm_exact_rhs(frow, upper_b if d == 0 else lower_b, _NN))
        btot.append(_mm_exact_rhs(frow, ones_b, _NN))

    units = [(j, d, h) for j in range(K) for d in range(N_DIR) for h in range(H)]
    nun = range(len(units))
    q, k, kt, v, vb, qb = [], [], [], [], [], []
    c_row, b_col, b_last = [], [], []
    for j, d, h in units:
        at = j if d == 0 else K - 1 - j
        rows = slice(at * L, (at + 1) * L)
        st = d * H + h
        gi, gf = st, 2 * H + st
        qk_ref, kt_ref, v_ref = (qkf_ref, ktf_ref, vf_ref) if d == 0 else (qkb_ref, ktb_ref, vb_ref)
        q.append(qk_ref[rows, h * dh:(h + 1) * dh] * (dh ** -0.5))
        k.append(qk_ref[rows, MLSTM_WIDTH + h * dh:MLSTM_WIDTH + (h + 1) * dh])
        kt.append(kt_ref[at, h * dh:(h + 1) * dh, :])
        v.append(v_ref[rows, h * dh:(h + 1) * dh])
        qb.append(q[-1].astype(BF16))
        vb.append(v[-1].astype(BF16))
        b_col.append(jnp.broadcast_to(bcol[d][rows, gf:gf + 1], (L, LANE)))
        c_row.append(grow[d][at * MLSTM_GATES + gi:at * MLSTM_GATES + gi + 1, :]
                     - brow[d][at * MLSTM_GATES + gf:at * MLSTM_GATES + gf + 1, :])
        b_last.append(btot[d][at * MLSTM_GATES + gf:at * MLSTM_GATES + gf + 1, :])

    last = [L - 1 if d == 0 else 0 for _, d, _ in units]
    qk_t = [_dg(qb[i], k[i].astype(BF16), _NT) for i in nun]
    rel = [jnp.where(lower if units[i][1] == 0 else upper, c_row[i], neg_inf) for i in nun]
    mx = [jnp.broadcast_to(jnp.max(rel[i], axis=-1, keepdims=True), (L, LANE)) for i in nun]
    m_loc = [b_col[i] + mx[i] for i in nun]
    s_loc = [qk_t[i] * jnp.exp(rel[i] - mx[i][:, 0:L]) for i in nun]
    s_v = [_dg(s_loc[i].astype(BF16), vb[i], _NN) for i in nun]
    s_sum = [jnp.broadcast_to(jnp.sum(s_loc[i], axis=-1, keepdims=True), (L, LANE)) for i in nun]
    cmax = [mx[i][last[i]:last[i] + 1, :] for i in nun]
    m_w = [b_last[i] + cmax[i] for i in nun]
    wj = [jnp.exp(c_row[i] - cmax[i][:, 0:L]) for i in nun]
    kv = [_dg((kt[i] * wj[i]).astype(BF16), vb[i], _NN) for i in nun]
    w_k = [_mm(jnp.broadcast_to(wj[i], (8, L)), k[i], _NN, 3)[0:1] for i in nun]

    nst = N_DIR * H
    c_st = [c_scr[st] for st in range(nst)]
    n_st = [n_scr[st:st + 1, :] for st in range(nst)]
    m_st = [m_scr[st:st + 1, :] for st in range(nst)]
    for j in range(K):
        idx = [j * nst + st for st in range(nst)]
        q_c = [_dg(qb[i], c_st[st].astype(BF16), _NN) for st, i in enumerate(idx)]
        for st, i in enumerate(idx):
            _, d, h = units[i]
            at = j if d == 0 else K - 1 - j
            h_ref = hf_ref if d == 0 else hb_ref
            log_inter = b_col[i] + m_st[st]
            m_s = jnp.maximum(log_inter, m_loc[i])
            inter = jnp.exp(log_inter - m_s)
            local = jnp.exp(m_loc[i] - m_s)
            q_n = jnp.broadcast_to(jnp.sum(q[i] * n_st[st], axis=-1, keepdims=True), (L, LANE))
            den = inter * q_n + local * s_sum[i]
            scale = 1.0 / jnp.maximum(jnp.abs(den), jnp.exp(-m_s))
            h_ref[at * L:(at + 1) * L, h * dh:(h + 1) * dh] = (inter * scale) * q_c[st] + (local * scale) * s_v[i]
            m_new = jnp.maximum(b_last[i] + m_st[st], m_w[i])
            carry = jnp.exp(b_last[i] + m_st[st] - m_new)
            fresh = jnp.exp(m_w[i] - m_new)
            c_st[st] = carry * c_st[st] + fresh * kv[i]
            n_st[st] = carry * n_st[st] + fresh * w_k[i]
            m_st[st] = m_new

    for st in range(nst):
        c_scr[st] = c_st[st]
        n_scr[st:st + 1, :] = n_st[st]
        m_scr[st:st + 1, :] = m_st[st]
    cout_ref[0] = c_scr[...]
    nout_ref[0] = n_scr[...]
    mout_ref[0] = m_scr[...]


def _mlstm_scan(z, qk, qk_blk, kt, gt, gate_bc, gate_br, c0, n0, m0, batch, seq_len):
    K = MLSTM_STEP_CHUNKS
    L = K * MLSTM_CHUNK
    assert seq_len % L == 0
    cps = seq_len // L
    n = batch * seq_len
    W = MLSTM_WIDTH
    nst = N_DIR * MLSTM_HEADS
    dh = MLSTM_HEAD_DIM

    def fw(b, c):
        return b * cps + c

    def bw(b, c):
        return b * cps + cps - 1 - c

    vblk = (ZM_OFF + 2 * W) // W
    gblk = ZG_OFF // LANE
    return pl.pallas_call(
        _mlstm_scan_kernel,
        grid=(batch, cps),
        in_specs=[pl.BlockSpec((L, 2 * W), lambda b, c: (fw(b, c), qk_blk)),
                  pl.BlockSpec((L, 2 * W), lambda b, c: (bw(b, c), qk_blk)),
                  pl.BlockSpec((K, W, MLSTM_CHUNK), lambda b, c: (fw(b, c), 0, 0)),
                  pl.BlockSpec((K, W, MLSTM_CHUNK), lambda b, c: (bw(b, c), 0, 0)),
                  pl.BlockSpec((L, W), lambda b, c: (fw(b, c), vblk)),
                  pl.BlockSpec((L, W), lambda b, c: (bw(b, c), vblk)),
                  pl.BlockSpec((L, LANE), lambda b, c: (fw(b, c), gblk)),
                  pl.BlockSpec((L, LANE), lambda b, c: (bw(b, c), gblk)),
                  pl.BlockSpec((K, MLSTM_GATES, MLSTM_CHUNK), lambda b, c: (fw(b, c), 0, 0)),
                  pl.BlockSpec((K, MLSTM_GATES, MLSTM_CHUNK), lambda b, c: (bw(b, c), 0, 0)),
                  _resident((1, LANE)),
                  _resident((MLSTM_GATES, 1)),
                  pl.BlockSpec((1, nst, dh, dh), lambda b, c: (b, 0, 0, 0)),
                  pl.BlockSpec((1, nst, dh), lambda b, c: (b, 0, 0)),
                  pl.BlockSpec((1, nst, LANE), lambda b, c: (b, 0, 0))],
        out_specs=[pl.BlockSpec((L, W), lambda b, c: (fw(b, c), 0)),
                   pl.BlockSpec((L, W), lambda b, c: (bw(b, c), 0)),
                   pl.BlockSpec((1, nst, dh, dh), lambda b, c: (b, 0, 0, 0)),
                   pl.BlockSpec((1, nst, dh), lambda b, c: (b, 0, 0)),
                   pl.BlockSpec((1, nst, LANE), lambda b, c: (b, 0, 0))],
        out_shape=[jax.ShapeDtypeStruct((n, W), F32), jax.ShapeDtypeStruct((n, W), F32),
                   jax.ShapeDtypeStruct((batch, nst, dh, dh), F32),
                   jax.ShapeDtypeStruct((batch, nst, dh), F32),
                   jax.ShapeDtypeStruct((batch, nst, LANE), F32)],
        scratch_shapes=[pltpu.VMEM((nst, dh, dh), F32), pltpu.VMEM((nst, dh), F32),
                        pltpu.VMEM((nst, LANE), F32)],
        compiler_params=_params(("arbitrary", "arbitrary")),
        name="mlstm_scan",
    )(qk, qk, kt, kt, z, z, z, z, gt, gt, gate_bc, gate_br, c0, n0, m0)


def _merge_kernel(conv_width, x_ref, mod_ref, ysf_ref, ysb_ref, bonus_ref, gate_ref, hf_ref, hb_ref, zo_ref,
                  zs_ref, lnxg_ref, lnxb_ref, gng_ref, pmean_ref, wbr_ref, wbm_ref, wout_ref, ng_ref, wup_ref,
                  cw_ref, cb_ref, x1_ref, u_ref):
    mod = mod_ref[0]
    g1 = mod[:, 2 * D_MODEL:3 * D_MODEL]
    sh2 = mod[:, 3 * D_MODEL:4 * D_MODEL]
    sc2 = mod[:, 4 * D_MODEL:5 * D_MODEL]

    ys = ysf_ref[...] + ysb_ref[...]
    pmean = pmean_ref[...]
    mean = _mm_exact_rhs(ys, pmean, _NN)
    cen = ys - mean
    var = _mm_exact_rhs(cen * cen, pmean, _NN)
    y_r = (cen * lax.rsqrt(var + RWKV_GN_EPS) * lnxg_ref[...] + lnxb_ref[...] + bonus_ref[...]) * gate_ref[...]

    hs = hf_ref[...] + hb_ref[...]
    parts = []
    for h in range(MLSTM_HEADS):
        hh = hs[:, h * MLSTM_HEAD_DIM:(h + 1) * MLSTM_HEAD_DIM]
        mu = jnp.mean(hh, axis=-1, keepdims=True)
        ce = hh - mu
        va = jnp.mean(ce * ce, axis=-1, keepdims=True)
        parts.append(ce * lax.rsqrt(va + MLSTM_GN_EPS))
    y_m = jnp.concatenate(parts, axis=1) * gng_ref[...] * _sigmoid(zo_ref[...])

    gates = zs_ref[...]
    merged = (gates[:, 0:D_MODEL] * _dg(y_r.astype(BF16), wbr_ref[...], _NN)
              + gates[:, D_MODEL:2 * D_MODEL] * _dg(y_m.astype(BF16), wbm_ref[...], _NN))
    t = _dg(merged.astype(BF16), wout_ref[...], _NN)
    x1 = x_ref[...] + g1 * _rms(t, ng_ref[1:2, :])
    x1_ref[...] = x1
    h2 = _rms(x1, ng_ref[2:3, :]) * (1.0 + sc2) + sh2
    u = _dg(h2.astype(BF16), wup_ref[...], _NN)
    if conv_width is None:
        u_ref[...] = u
    else:
        act = _dwconv(u[:, 0:D_FF], cw_ref, conv_width, False) + cb_ref[...]
        u_ref[...] = (_silu(act) * u[:, D_FF:2 * D_FF]).astype(BF16)


def _merge(x2, mod, mod_row, z, ysf, ysb, bonus, gate, hf, hb, p, conv_width):
    n = x2.shape[0]
    W = RWKV_WIDTH
    tile = lambda w: pl.BlockSpec((TOK_TILE, w), lambda i: (i, 0))
    u_out = ((tile(2 * D_FF), jax.ShapeDtypeStruct((n, 2 * D_FF), F32)) if conv_width is None else
             (tile(D_FF), jax.ShapeDtypeStruct((n, D_FF), BF16)))
    return pl.pallas_call(
        functools.partial(_merge_kernel, conv_width),
        grid=(n // TOK_TILE,),
        in_specs=[tile(D_MODEL),
                  pl.BlockSpec((1, 1, 6 * D_MODEL), lambda i: (mod_row(i), 0, 0)),
                  tile(W), tile(W), tile(W), tile(W), tile(MLSTM_WIDTH), tile(MLSTM_WIDTH),
                  pl.BlockSpec((TOK_TILE, MLSTM_WIDTH), lambda i: (i, (ZM_OFF + 3 * MLSTM_WIDTH) // MLSTM_WIDTH)),
                  pl.BlockSpec((TOK_TILE, GATE_COLS), lambda i: (i, ZS_OFF // GATE_COLS)),
                  _resident((1, W)), _resident((1, W)), _resident((1, MLSTM_WIDTH)),
                  _resident((W, W)),
                  _resident((W, D_MODEL)), _resident((MLSTM_WIDTH, D_MODEL)),
                  _resident((D_MODEL, D_MODEL)), _resident((4, D_MODEL)),
                  _resident((D_MODEL, 2 * D_FF)), _resident((9, D_FF)), _resident((1, D_FF))],
        out_specs=[tile(D_MODEL), u_out[0]],
        out_shape=[jax.ShapeDtypeStruct((n, D_MODEL), F32), u_out[1]],
        compiler_params=_params(("arbitrary",)),
        name="merge_ffn_up",
    )(x2, mod, ysf, ysb, bonus, gate, hf, hb, z, z, p["lnx_g"], p["lnx_b"], p["gn_g"], p["pmean"],
      p["w_br"], p["w_bm"], p["w_out"], p["norm_g"], p["ffn_up"], p["ffn_conv"], p["ffn_conv_b"])


def _down_kernel(x1_ref, mod_ref, a_ref, w_ref, ng_ref, o_ref):
    g2 = mod_ref[0][:, 5 * D_MODEL:6 * D_MODEL]
    f = _dg(a_ref[...], w_ref[...], _NN)
    o_ref[...] = x1_ref[...] + g2 * _rms(f, ng_ref[3:4, :])


def _down(x1, mod, mod_row, act, p):
    n = x1.shape[0]
    return pl.pallas_call(
        _down_kernel,
        grid=(n // TOK_TILE,),
        in_specs=[pl.BlockSpec((TOK_TILE, D_MODEL), lambda i: (i, 0)),
                  pl.BlockSpec((1, 1, 6 * D_MODEL), lambda i: (mod_row(i), 0, 0)),
                  pl.BlockSpec((TOK_TILE, D_FF), lambda i: (i, 0)),
                  _resident((D_FF, D_MODEL)), _resident((4, D_MODEL))],
        out_specs=pl.BlockSpec((TOK_TILE, D_MODEL), lambda i: (i, 0)),
        out_shape=jax.ShapeDtypeStruct((n, D_MODEL), F32),
        compiler_params=_params(("arbitrary",)),
        name="ffn_down",
    )(x1, mod, act, p["ffn_down"], p["norm_g"])


RWKV_LOCAL_PASSES = 1


def _state_to_pairs(s):
    b = s.shape[0]
    s = s.reshape(b, N_DIR, RWKV_PAIRS, 2, RWKV_HEAD_DIM, RWKV_HEAD_DIM)
    zero = jnp.zeros_like(s[:, :, :, 0])
    top = jnp.concatenate([s[:, :, :, 0], zero], axis=-1)
    bot = jnp.concatenate([zero, s[:, :, :, 1]], axis=-1)
    return jnp.concatenate([top, bot], axis=-2)


def _state_from_pairs(sb):
    b = sb.shape[0]
    n = RWKV_HEAD_DIM
    parts = jnp.stack([sb[..., 0:n, 0:n], sb[..., n:2 * n, n:2 * n]], axis=3)
    return parts.reshape(b, N_DIR, RWKV_HEADS, n, n)


def _trunk(x, mod, mod_row, rows, states, p):
    batch, seq_len, _ = x.shape
    n = batch * seq_len
    x2 = x.reshape(n, D_MODEL)
    s0, c0, n0, m0 = states

    fuse_width = seq_len if (rows == 1 and TOK_TILE % seq_len == 0) else None
    z = _in_proj(x2, mod, mod_row, p["norm_g"][0:1], p["w_r"], p["w_g"], p["w_m"], p["w_s"],
                 p["mlstm_conv"], fuse_width)

    rp, y0, gm, hm, gate, bonus = _rwkv_local(z, seq_len, p, RWKV_LOCAL_PASSES)
    ysf, ysb, s_fin = _rwkv_scan(_state_to_pairs(s0), rp, y0, gm, hm, batch, seq_len)
    s_fin = _state_from_pairs(s_fin)

    if fuse_width is None:
        qk, qk_blk, k_off = _qk_conv(z, batch, seq_len, rows, p["mlstm_conv"]), 0, MLSTM_WIDTH
    else:
        qk, qk_blk, k_off = z, ZM_OFF // (2 * MLSTM_WIDTH), ZM_OFF + MLSTM_WIDTH
    gt = z[:, ZG_OFF:ZG_OFF + MLSTM_GATES].reshape(n // MLSTM_CHUNK, MLSTM_CHUNK, MLSTM_GATES).transpose(0, 2, 1)
    nst = N_DIR * MLSTM_HEADS
    kt = qk[:, k_off:k_off + MLSTM_WIDTH].reshape(n // MLSTM_CHUNK, MLSTM_CHUNK, MLSTM_WIDTH).transpose(0, 2, 1)
    hf, hb, ct_fin, n_fin, m_fin = _mlstm_scan(
        z, qk, qk_blk, kt, gt, p["gate_bc"], p["gate_br"],
        jnp.swapaxes(c0, -1, -2).reshape(batch, nst, MLSTM_HEAD_DIM, MLSTM_HEAD_DIM),
        n0.reshape(batch, nst, MLSTM_HEAD_DIM),
        jnp.broadcast_to(m0.reshape(batch, nst, 1), (batch, nst, LANE)), batch, seq_len)
    c_fin = jnp.swapaxes(ct_fin, -1, -2)

    if fuse_width is not None:
        x1, act = _merge(x2, mod, mod_row, z, ysf, ysb, bonus, gate, hf, hb, p, fuse_width)
    else:
        x1, u = _merge(x2, mod, mod_row, z, ysf, ysb, bonus, gate, hf, hb, p, None)
        act = _ffn_conv(u, batch, seq_len, rows, p["ffn_conv"], p["ffn_conv_b"])
    out = _down(x1, mod, mod_row, act, p)

    new_states = (s_fin,
                  c_fin.reshape(batch, N_DIR, MLSTM_HEADS, MLSTM_HEAD_DIM, MLSTM_HEAD_DIM),
                  n_fin.reshape(batch, N_DIR, MLSTM_HEADS, MLSTM_HEAD_DIM),
                  m_fin[:, :, 0].reshape(batch, N_DIR, MLSTM_HEADS))
    return out.reshape(batch, seq_len, D_MODEL), new_states


def _pack_layer(l, ada_w, ada_b, norm_g, w_in, rwkv_mu, rwkv_w0, rwkv_w_up, rwkv_a0, rwkv_a_up,
                rwkv_g_up, rwkv_kk_scale, rwkv_k_a, rwkv_r_k, rwkv_lnx_g, rwkv_lnx_b, mlstm_conv,
                mlstm_gate_b, mlstm_gn_g, w_branch_rwkv, w_branch_mlstm, w_out, ffn_up, ffn_conv,
                ffn_conv_b, ffn_down):
    W = RWKV_WIDTH
    wi = w_in[l]
    w_r = wi[:, 0:RWKV_COLS].astype(BF16)
    w_m = wi[:, RWKV_COLS:RWKV_COLS + 4 * MLSTM_WIDTH].astype(BF16)
    w_g = jnp.pad(wi[:, RWKV_COLS + 4 * MLSTM_WIDTH:RWKV_COLS + MLSTM_COLS].astype(BF16),
                  ((0, 0), (0, LANE - MLSTM_GATES)))
    w_s = wi[:, RWKV_COLS + MLSTM_COLS:].astype(BF16)

    head = jnp.arange(W, dtype=jnp.int32) // RWKV_HEAD_DIM
    same = (head[:, None] == head[None, :])
    gb = mlstm_gate_b[l].reshape(1, MLSTM_GATES)
    return dict(
        ada_w=ada_w[l], ada_b=ada_b[l], norm_g=norm_g[l], w_r=w_r, w_g=w_g, w_m=w_m, w_s=w_s,
        mu=rwkv_mu[l].reshape(1, RWKV_COLS),
        w0=rwkv_w0[l].reshape(N_DIR, 1, W), w_up=rwkv_w_up[l],
        a0=rwkv_a0[l].reshape(N_DIR, 1, W), a_up=rwkv_a_up[l], g_up=rwkv_g_up[l],
        kk_scale=rwkv_kk_scale[l].reshape(1, W), k_a=rwkv_k_a[l].reshape(1, W),
        r_k=rwkv_r_k[l].reshape(1, W),
        lnx_g=rwkv_lnx_g[l].reshape(1, W), lnx_b=rwkv_lnx_b[l].reshape(1, W),
        pones=same.astype(BF16), pmean=(same.astype(F32) / RWKV_HEAD_DIM).astype(BF16),
        mlstm_conv=mlstm_conv[l].reshape(9, 2 * MLSTM_WIDTH),
        gate_bc=jnp.pad(gb, ((0, 0), (0, LANE - MLSTM_GATES))), gate_br=gb.reshape(MLSTM_GATES, 1),
        gn_g=mlstm_gn_g[l].reshape(1, MLSTM_WIDTH),
        w_br=w_branch_rwkv[l].astype(BF16), w_bm=w_branch_mlstm[l].astype(BF16),
        w_out=w_out[l].astype(BF16), ffn_up=ffn_up[l].astype(BF16),
        ffn_conv=ffn_conv[l].reshape(9, D_FF), ffn_conv_b=ffn_conv_b[l].reshape(1, D_FF),
        ffn_down=ffn_down[l].astype(BF16),
    )


def kernel(x_prompt, x_sample, c, state_rwkv, state_mlstm_C, state_mlstm_n, state_mlstm_m, c_ctx,
           ada_w, ada_b, norm_g, w_in, rwkv_mu, rwkv_w0, rwkv_w_up, rwkv_a0, rwkv_a_up, rwkv_g_up,
           rwkv_kk_scale, rwkv_k_a, rwkv_r_k, rwkv_lnx_g, rwkv_lnx_b, mlstm_conv, mlstm_gate_b,
           mlstm_gn_g, w_branch_rwkv, w_branch_mlstm, w_out, ffn_up, ffn_conv, ffn_conv_b, ffn_down):
    depth = ada_w.shape[0]
    batch = x_prompt.shape[0]
    dec_batch, dec_seq, _ = x_sample.shape
    latent_rows = dec_seq // GRID_W
    tiles_per_latent = dec_seq // TOK_TILE
    ctx_init = (jnp.zeros((batch, N_DIR, RWKV_HEADS, RWKV_HEAD_DIM, RWKV_HEAD_DIM), F32),
                jnp.zeros((batch, N_DIR, MLSTM_HEADS, MLSTM_HEAD_DIM, MLSTM_HEAD_DIM), F32),
                jnp.zeros((batch, N_DIR, MLSTM_HEADS, MLSTM_HEAD_DIM), F32),
                jnp.zeros((batch, N_DIR, MLSTM_HEADS), F32))
    cond = jnp.concatenate([c_ctx[None, :], c, jnp.zeros((8 - 1 - dec_batch, D_MODEL), F32)], axis=0)

    xp, xs = x_prompt, x_sample
    new_s, new_c, new_n, new_m = [], [], [], []
    for l in range(depth):
        p = _pack_layer(l, ada_w, ada_b, norm_g, w_in, rwkv_mu, rwkv_w0, rwkv_w_up, rwkv_a0, rwkv_a_up,
                        rwkv_g_up, rwkv_kk_scale, rwkv_k_a, rwkv_r_k, rwkv_lnx_g, rwkv_lnx_b, mlstm_conv,
                        mlstm_gate_b, mlstm_gn_g, w_branch_rwkv, w_branch_mlstm, w_out, ffn_up, ffn_conv,
                        ffn_conv_b, ffn_down)
        mod = _ada(cond, p["ada_w"], p["ada_b"]).reshape(8, 1, 6 * D_MODEL)
        xp, (s, cc, nn, mm) = _trunk(xp, mod, lambda i: 0, 1, ctx_init, p)
        new_s.append(s)
        new_c.append(cc)
        new_n.append(nn)
        new_m.append(mm)
        xs, _ = _trunk(xs, mod, lambda i: 1 + i // tiles_per_latent, latent_rows,
                       (state_rwkv[:, l], state_mlstm_C[:, l], state_mlstm_n[:, l], state_mlstm_m[:, l]), p)
    return (xp, xs, jnp.stack(new_s, axis=1), jnp.stack(new_c, axis=1),
            jnp.stack(new_n, axis=1), jnp.stack(new_m, axis=1))
```

```python
import functools

import jax
import jax.numpy as jnp
from jax import lax
from jax.experimental import pallas as pl
from jax.experimental.pallas import tpu as pltpu

F32 = jnp.float32
BF16 = jnp.bfloat16

D_MODEL = 1024
N_DIR = 2
RWKV_HEADS = 8
RWKV_HEAD_DIM = 64
RWKV_WIDTH = RWKV_HEADS * RWKV_HEAD_DIM
DECAY_LORA = 64
ICLR_LORA = 64
GATE_LORA = 128
MLSTM_HEADS = 4
MLSTM_HEAD_DIM = 128
MLSTM_WIDTH = MLSTM_HEADS * MLSTM_HEAD_DIM
MLSTM_CHUNK = 64
D_FF = 2816
GRID_W = 64
RMS_EPS = 1e-6
RWKV_GN_EPS = 64e-5
MLSTM_GN_EPS = 1e-5
DECAY_SCALE = 0.606531

RWKV_COLS = 3 * RWKV_WIDTH + N_DIR * DECAY_LORA + N_DIR * ICLR_LORA + GATE_LORA
MLSTM_GATES = 2 * N_DIR * MLSTM_HEADS
MLSTM_COLS = 4 * MLSTM_WIDTH + MLSTM_GATES
GATE_COLS = 2 * D_MODEL

LANE = 128
ZR_BLOCK = 2048
ZG_OFF = RWKV_COLS
ZM_OFF = ZR_BLOCK
ZS_OFF = ZM_OFF + 4 * MLSTM_WIDTH
Z_COLS = ZS_OFF + GATE_COLS

IN_TILE = 512
MERGE_TILE = 256
DOWN_TILE = 512
CONV_DOWN_TILE = 256
RCHUNK = 64
CONV_CH_TILE = 256
VMEM_LIMIT = 56 * 1024 * 1024


def _params(sem):
    return pltpu.CompilerParams(dimension_semantics=sem, vmem_limit_bytes=VMEM_LIMIT)


def _resident(shape):
    nd = len(shape)
    return pl.BlockSpec(shape, lambda *_: (0,) * nd, pipeline_mode=pl.Buffered(1))


def _split2(a):
    hi = a.astype(BF16)
    lo = (a - hi.astype(F32)).astype(BF16)
    return hi, lo


def _split3(a):
    hi = a.astype(BF16)
    r1 = a - hi.astype(F32)
    mid = r1.astype(BF16)
    lo = (r1 - mid.astype(F32)).astype(BF16)
    return hi, mid, lo


def _dg(a, b, dims):
    return lax.dot_general(a, b, dims, preferred_element_type=F32)


def _mm(a, b, dims, passes):
    if passes == 1:
        return _dg(a.astype(BF16), b.astype(BF16), dims)
    ah, al = _split2(a)
    bh, bl = _split2(b)
    return _dg(ah, bh, dims) + (_dg(ah, bl, dims) + _dg(al, bh, dims))


def _mm_exact_lhs(a_bf16, b, dims):
    b1, b2, b3 = _split3(b)
    return _dg(a_bf16, b1, dims) + (_dg(a_bf16, b2, dims) + _dg(a_bf16, b3, dims))


def _mm_exact_rhs(a, b_bf16, dims):
    a1, a2, a3 = _split3(a)
    return _dg(a1, b_bf16, dims) + (_dg(a2, b_bf16, dims) + _dg(a3, b_bf16, dims))


_NN = (((1,), (0,)), ((), ()))
_NT = (((1,), (1,)), ((), ()))
_TN = (((0,), (0,)), ((), ()))
_BNN = (((2,), (1,)), ((0,), (0,)))
_BNT = (((2,), (2,)), ((0,), (0,)))
_BTN = (((1,), (1,)), ((0,), (0,)))


def _sigmoid(x):
    return jax.nn.sigmoid(x)


def _silu(x):
    return x * jax.nn.sigmoid(x)


def _rms(x, g):
    return x * lax.rsqrt(jnp.mean(x * x, axis=-1, keepdims=True) + RMS_EPS) * g


def _ada_kernel(cond_ref, w_ref, b_ref, o_ref):
    s = _silu(cond_ref[...])
    o_ref[...] = _dg(s.astype(BF16), w_ref[...].astype(BF16), _NN) + b_ref[...]


def _ada(cond8, ada_w, ada_b):
    n = ada_w.shape[1]
    tn = 1536
    return pl.pallas_call(
        _ada_kernel,
        grid=(n // tn,),
        in_specs=[_resident((8, D_MODEL)),
                  pl.BlockSpec((D_MODEL, tn), lambda j: (0, j)),
                  pl.BlockSpec((1, tn), lambda j: (0, j))],
        out_specs=pl.BlockSpec((8, tn), lambda j: (0, j)),
        out_shape=jax.ShapeDtypeStruct((8, n), F32),
        compiler_params=_params(("arbitrary",)),
        name="ada_mod",
    )(cond8, ada_w, ada_b.reshape(1, n))


def _in_kernel(conv_width, x_ref, mod_ref, g_ref, wr_ref, wg_ref, wm_ref, ws_ref, cw_ref, z_ref):
    mod = mod_ref[0]
    sh = mod[:, 0:D_MODEL]
    sc = mod[:, D_MODEL:2 * D_MODEL]
    h = (_rms(x_ref[...], g_ref[...]) * (1.0 + sc) + sh).astype(BF16)
    z_ref[:, 0:ZG_OFF] = _dg(h, wr_ref[...], _NN)
    z_ref[:, ZG_OFF:ZM_OFF] = _dg(h, wg_ref[...], _NN)
    zm = _dg(h, wm_ref[...], _NN)
    if conv_width is None:
        z_ref[:, ZM_OFF:ZS_OFF] = zm
    else:
        qk_cols = 2 * MLSTM_WIDTH
        z_ref[:, ZM_OFF:ZM_OFF + qk_cols] = _silu(_dwconv(zm[:, 0:qk_cols], cw_ref, conv_width, False))
        z_ref[:, ZM_OFF + qk_cols:ZS_OFF] = zm[:, qk_cols:]
    z_ref[:, ZS_OFF:Z_COLS] = _sigmoid(_dg(h, ws_ref[...], _NN))


def _in_proj(x2, mod, mod_row, norm_g0, w_r, w_g, w_m, w_s, conv_w9, conv_width):
    n = x2.shape[0]
    tile = IN_TILE
    return pl.pallas_call(
        functools.partial(_in_kernel, conv_width),
        grid=(n // tile,),
        in_specs=[pl.BlockSpec((tile, D_MODEL), lambda i: (i, 0)),
                  pl.BlockSpec((1, 1, 6 * D_MODEL), lambda i: (mod_row(i * tile), 0, 0)),
                  _resident((1, D_MODEL)),
                  _resident(w_r.shape), _resident(w_g.shape), _resident(w_m.shape), _resident(w_s.shape),
                  _resident(conv_w9.shape)],
        out_specs=pl.BlockSpec((tile, Z_COLS), lambda i: (i, 0)),
        out_shape=jax.ShapeDtypeStruct((n, Z_COLS), F32),
        compiler_params=_params(("arbitrary",)),
        name="in_proj",
    )(x2, mod, norm_g0, w_r, w_g, w_m, w_s, conv_w9)


LOCAL_CHUNKS = 4
PAIR_LANES = 2 * RWKV_HEAD_DIM
RWKV_PAIRS = RWKV_HEADS // 2


def _bd(x):
    lane = lax.broadcasted_iota(jnp.int32, x.shape, 1)
    left = lane < RWKV_HEAD_DIM
    return jnp.concatenate([jnp.where(left, x, 0.0), jnp.where(left, 0.0, x)], axis=0)


def _rwkv_local_kernel(chunks_per_seq, passes,
                       z_ref, zp_ref, zn_ref, mu_ref, w0_ref, wup_ref, a0_ref, aup_ref, gup_ref,
                       kks_ref, ka_ref, rk_ref, pones_ref,
                       rp_ref, y0_ref, gm_ref, hm_ref, gate_ref, bonus_ref):
    C = RCHUNK
    W = RWKV_WIDTH
    NS = LOCAL_CHUNKS
    R = NS * C
    first = (pl.program_id(0) * NS) % chunks_per_seq
    has_prev = first != 0
    has_next = first + NS != chunks_per_seq

    z = z_ref[:, 0:RWKV_COLS]
    zp = jnp.where(has_prev, zp_ref[7:8, 0:RWKV_COLS], 0.0)
    zn = jnp.where(has_next, zn_ref[0:1, 0:RWKV_COLS], 0.0)
    trow = lax.broadcasted_iota(jnp.int32, (R, 1), 0)
    prev = jnp.where(trow == 0, zp, pltpu.roll(z, 1, 0))
    nxt = jnp.where(trow == R - 1, zn, pltpu.roll(z, R - 1, 0))
    zs = z + mu_ref[...] * (0.5 * (prev + nxt) - z)

    r = zs[:, 0:W]
    k = zs[:, W:2 * W]
    v = zs[:, 2 * W:3 * W]
    gd = zs[:, 3 * W + 2 * DECAY_LORA + 2 * ICLR_LORA:RWKV_COLS]
    gate_ref[...] = _dg(_sigmoid(gd).astype(BF16), gup_ref[...].astype(BF16), _NN)

    pones = pones_ref[...]
    kks = k * kks_ref[...]
    norm = jnp.sqrt(_mm_exact_rhs(kks * kks, pones, _NN))
    kk = kks / jnp.maximum(norm, 1e-12)

    P = PAIR_LANES
    row = lax.broadcasted_iota(jnp.int32, (R, R), 0)
    col = lax.broadcasted_iota(jnp.int32, (R, R), 1)
    same_chunk = jnp.bitwise_and(row, -C) == jnp.bitwise_and(col, -C)
    prow = lax.broadcasted_iota(jnp.int32, (C, P), 0)
    pcol = jnp.bitwise_and(lax.broadcasted_iota(jnp.int32, (C, P), 1), RWKV_HEAD_DIM - 1)
    eye_p = jnp.where(prow == pcol, 1.0, 0.0)
    brow = lax.broadcasted_iota(jnp.int32, (P, P), 0)
    bcol = lax.broadcasted_iota(jnp.int32, (P, P), 1)
    same_head = (brow < RWKV_HEAD_DIM) == (bcol < RWKV_HEAD_DIM)
    eye_b = jnp.where(brow == bcol, 1.0, 0.0)

    abar, rbar, kt, bt, kw, bw, wc, strict, incl = [], [], [], [], [], [], [], [], []
    bonus = None
    for d in range(N_DIR):
        o = 3 * W + d * DECAY_LORA
        wd = zs[:, o:o + DECAY_LORA]
        o = 3 * W + 2 * DECAY_LORA + d * ICLR_LORA
        ad = zs[:, o:o + ICLR_LORA]
        logw = -DECAY_SCALE * _sigmoid(w0_ref[d] + _dg(jnp.tanh(wd).astype(BF16), wup_ref[d].astype(BF16), _NN))
        a = _sigmoid(a0_ref[d] + _dg(ad.astype(BF16), aup_ref[d].astype(BF16), _NN))
        kd = k * (1.0 + (a - 1.0) * ka_ref[...])
        b = kk * a
        bonus_d = _mm_exact_rhs(r * kd * rk_ref[...], pones, _NN) * v
        bonus = bonus_d if bonus is None else bonus + bonus_d

        earlier_or_same = same_chunk & ((row >= col) if d == 0 else (row <= col))
        cum_i = _mm_exact_lhs(jnp.where(earlier_or_same, 1.0, 0.0).astype(BF16), logw, _NN)
        cum_e = cum_i - logw
        ab_d, rb_d, kt_d, bt_d, kw_d, bw_d, wc_d = [], [], [], [], [], [], []
        for s in range(NS):
            rs = slice(s * C, (s + 1) * C)
            ci_s = cum_i[rs]
            ctot = jnp.sum(logw[rs], axis=0, keepdims=True)
            e_ni = jnp.exp(-ci_s)
            e_ti = jnp.exp(ctot - ci_s)
            ab_d.append(kk[rs] * jnp.exp(cum_e[rs]))
            rb_d.append(r[rs] * jnp.exp(ci_s))
            kt_d.append(kd[rs] * e_ni)
            bt_d.append(b[rs] * e_ni)
            kw_d.append(kd[rs] * e_ti)
            bw_d.append(b[rs] * e_ti)
            wc_d.append(jnp.exp(ctot))
        abar.append(ab_d)
        rbar.append(rb_d)
        kt.append(kt_d)
        bt.append(bt_d)
        kw.append(kw_d)
        bw.append(bw_d)
        wc.append(wc_d)
        strict.append((prow > pcol) if d == 0 else (prow < pcol))
        incl.append((prow >= pcol) if d == 0 else (prow <= pcol))
    bonus_ref[...] = bonus

    mm = functools.partial(_mm, passes=passes)
    chains = [(s, d, p) for s in range(NS) for d in range(N_DIR) for p in range(RWKV_PAIRS)]
    nch = range(len(chains))

    def sel(arr, i):
        s, d, p = chains[i]
        return arr[d][s][:, p * P:(p + 1) * P]

    cat0 = lambda a_, b_: jnp.concatenate([a_, b_], axis=0)
    cat1 = lambda a_, b_: jnp.concatenate([a_, b_], axis=1)
    vsl = [v[s * C:(s + 1) * C, p * P:(p + 1) * P] for s, _, p in chains]
    lhs = [cat0(sel(abar, i), sel(rbar, i)) for i in nch]
    by_b = [mm(lhs[i], _bd(sel(bt, i)), _NT) for i in nch]
    by_k = [mm(lhs[i], _bd(sel(kt, i)), _NT) for i in nch]
    a_kk = [jnp.where(strict[chains[i][1]], by_b[i][0:C], 0.0) for i in nch]
    a_rb = [jnp.where(incl[chains[i][1]], by_b[i][C:2 * C], 0.0) for i in nch]
    a_kv = [jnp.where(strict[chains[i][1]], by_k[i][0:C], 0.0) for i in nch]
    a_rk = [jnp.where(incl[chains[i][1]], by_k[i][C:2 * C], 0.0) for i in nch]
    on_v = [mm(cat0(a_kv[i], a_rk[i]), _bd(vsl[i]), _NN) for i in nch]

    x = [-m for m in a_kk]
    tinv = [eye_p + m for m in x]
    x = [mm(m, _bd(m), _NN) for m in x]
    for _ in range(4):
        both = [mm(cat0(tinv[i], x[i]), _bd(x[i]), _NN) for i in nch]
        tinv = [tinv[i] + both[i][0:C] for i in nch]
        x = [m[C:2 * C] for m in both]
    tinv = [tinv[i] + mm(tinv[i], _bd(x[i]), _NN) for i in nch]

    solved = [mm(tinv[i], cat1(_bd(sel(abar, i)), _bd(on_v[i][0:C])), _NN) for i in nch]
    ap = [m[:, 0:P] for m in solved]
    u0 = [m[:, P:2 * P] for m in solved]
    corr = [mm(a_rb[i], cat1(_bd(ap[i]), _bd(u0[i])), _NN) for i in nch]
    on_b = [mm(cat1(ap[i], u0[i]), sel(bw, i), _TN) for i in nch]
    vk = [mm(vsl[i], sel(kw, i), _TN) for i in nch]
    for i in nch:
        s, d, p = chains[i]
        rows = slice(s * C, (s + 1) * C)
        lanes = slice(p * P, (p + 1) * P)
        rp_ref[d, rows, lanes] = (sel(rbar, i) - corr[i][:, 0:P]).astype(BF16)
        y0_ref[d, rows, lanes] = on_v[i][C:2 * C] - corr[i][:, P:2 * P]
        gm_ref[d, s, p] = (eye_b * sel(wc, i) - jnp.where(same_head, on_b[i][0:P], 0.0)).astype(BF16)
        hm_ref[d, s, p] = jnp.where(same_head, vk[i] - on_b[i][P:2 * P], 0.0)


def _rwkv_local(z, seq_len, p, passes):
    n = z.shape[0]
    nchunk = n // RCHUNK
    cps = seq_len // RCHUNK
    assert cps % LOCAL_CHUNKS == 0
    W = RWKV_WIDTH
    rows = LOCAL_CHUNKS * RCHUNK
    hb = rows // 8
    last8 = n // 8 - 1
    mat = lambda dt: jax.ShapeDtypeStruct((N_DIR, nchunk, RWKV_PAIRS, PAIR_LANES, PAIR_LANES), dt)
    mat_spec = pl.BlockSpec((N_DIR, LOCAL_CHUNKS, RWKV_PAIRS, PAIR_LANES, PAIR_LANES),
                            lambda c: (0, c, 0, 0, 0))
    tok = lambda dt: jax.ShapeDtypeStruct((N_DIR, n, W), dt)
    tok_spec = pl.BlockSpec((N_DIR, rows, W), lambda c: (0, c, 0))
    row_spec = pl.BlockSpec((rows, W), lambda c: (c, 0))
    return pl.pallas_call(
        functools.partial(_rwkv_local_kernel, cps, passes),
        grid=(nchunk // LOCAL_CHUNKS,),
        in_specs=[pl.BlockSpec((rows, ZR_BLOCK), lambda c: (c, 0)),
                  pl.BlockSpec((8, ZR_BLOCK), lambda c: (jnp.maximum(c * hb - 1, 0), 0)),
                  pl.BlockSpec((8, ZR_BLOCK), lambda c: (jnp.minimum((c + 1) * hb, last8), 0)),
                  _resident((1, RWKV_COLS)),
                  _resident((N_DIR, 1, W)), _resident((N_DIR, DECAY_LORA, W)),
                  _resident((N_DIR, 1, W)), _resident((N_DIR, ICLR_LORA, W)),
                  _resident((GATE_LORA, W)),
                  _resident((1, W)), _resident((1, W)), _resident((1, W)),
                  _resident((W, W))],
        out_specs=[tok_spec, tok_spec, mat_spec, mat_spec, row_spec, row_spec],
        out_shape=[tok(BF16), tok(F32), mat(BF16), mat(F32),
                   jax.ShapeDtypeStruct((n, W), F32), jax.ShapeDtypeStruct((n, W), F32)],
        compiler_params=_params(("arbitrary",)),
        name="rwkv_local",
    )(z, z, z, p["mu"], p["w0"], p["w_up"], p["a0"], p["a_up"], p["g_up"],
      p["kk_scale"], p["k_a"], p["r_k"], p["pones"])


SCAN_CHUNKS = 4


def _rwkv_scan_kernel(s0_ref, rpf_ref, rpb_ref, y0f_ref, y0b_ref, gmf_ref, gmb_ref, hmf_ref, hmb_ref,
                      ysf_ref, ysb_ref, sout_ref, s_scr):
    @pl.when(pl.program_id(1) == 0)
    def _():
        s_scr[...] = s0_ref[0]

    K = SCAN_CHUNKS
    C = RCHUNK
    rp_ref, y0_ref, gm_ref, hm_ref, ys_ref = ((rpf_ref, rpb_ref), (y0f_ref, y0b_ref), (gmf_ref, gmb_ref),
                                              (hmf_ref, hmb_ref), (ysf_ref, ysb_ref))
    chains = [(d, p) for d in range(N_DIR) for p in range(RWKV_PAIRS)]
    lanes = [slice(p * PAIR_LANES, (p + 1) * PAIR_LANES) for _, p in chains]
    nch = range(len(chains))
    s = [s_scr[d, p] for d, p in chains]
    for j in range(K):
        at = (j, K - 1 - j)
        rows = [slice(at[d] * C, (at[d] + 1) * C) for d, _ in chains]
        sb = [m.astype(BF16) for m in s]
        y = [_dg(rp_ref[chains[i][0]][0, rows[i], lanes[i]], sb[i], _NT) for i in nch]
        sg = [_dg(sb[i], gm_ref[chains[i][0]][0, at[chains[i][0]], chains[i][1]], _NN) for i in nch]
        for i in nch:
            d, p = chains[i]
            ys_ref[d][rows[i], lanes[i]] = y[i] + y0_ref[d][0, rows[i], lanes[i]]
        s = [sg[i] + hm_ref[chains[i][0]][0, at[chains[i][0]], chains[i][1]] for i in nch]
    for i in nch:
        d, p = chains[i]
        s_scr[d, p] = s[i]
        sout_ref[0, d, p] = s[i]


def _rwkv_scan(s0, rp, y0, gm, hm, batch, seq_len):
    K = SCAN_CHUNKS
    spb = seq_len // (RCHUNK * K)
    n = batch * seq_len

    def fwd(b, s):
        return b * spb + s

    def bwd(b, s):
        return b * spb + spb - 1 - s

    def mat_spec(d, at):
        return pl.BlockSpec((1, K, RWKV_PAIRS, PAIR_LANES, PAIR_LANES), lambda b, s: (d, at(b, s), 0, 0, 0))

    def tok_spec(d, at):
        return pl.BlockSpec((1, K * RCHUNK, RWKV_WIDTH), lambda b, s: (d, at(b, s), 0))

    st_spec = pl.BlockSpec((1, N_DIR, RWKV_PAIRS, PAIR_LANES, PAIR_LANES), lambda b, s: (b, 0, 0, 0, 0))
    ys = jax.ShapeDtypeStruct((n, RWKV_WIDTH), F32)
    return pl.pallas_call(
        _rwkv_scan_kernel,
        grid=(batch, spb),
        in_specs=[st_spec, tok_spec(0, fwd), tok_spec(1, bwd), tok_spec(0, fwd), tok_spec(1, bwd),
                  mat_spec(0, fwd), mat_spec(1, bwd), mat_spec(0, fwd), mat_spec(1, bwd)],
        out_specs=[pl.BlockSpec((K * RCHUNK, RWKV_WIDTH), lambda b, s: (fwd(b, s), 0)),
                   pl.BlockSpec((K * RCHUNK, RWKV_WIDTH), lambda b, s: (bwd(b, s), 0)),
                   st_spec],
        out_shape=[ys, ys,
                   jax.ShapeDtypeStruct((batch, N_DIR, RWKV_PAIRS, PAIR_LANES, PAIR_LANES), F32)],
        scratch_shapes=[pltpu.VMEM((N_DIR, RWKV_PAIRS, PAIR_LANES, PAIR_LANES), F32)],
        compiler_params=_params(("arbitrary", "arbitrary")),
        name="rwkv_scan",
    )(s0, rp, rp, y0, y0, gm, gm, hm, hm)


CONV_BLOCK_ROWS = 2048


def _dwconv(x, w_ref, width, vertical):
    T = x.shape[0]
    t = lax.broadcasted_iota(jnp.int32, (T, 1), 0)
    assert width & (width - 1) == 0
    colp = jnp.bitwise_and(t, width - 1)
    xl = jnp.where(colp == 0, 0.0, pltpu.roll(x, 1, 0))
    xr = jnp.where(colp == width - 1, 0.0, pltpu.roll(x, T - 1, 0))

    def tap_row(i):
        return w_ref[3 * i:3 * i + 1, :] * xl + w_ref[3 * i + 1:3 * i + 2, :] * x + w_ref[3 * i + 2:3 * i + 3, :] * xr

    out = tap_row(1)
    if vertical:
        out = out + jnp.where(t < width, 0.0, pltpu.roll(tap_row(0), width, 0))
        out = out + jnp.where(t >= T - width, 0.0, pltpu.roll(tap_row(2), T - width, 0))
    return out


def _conv_geometry(n, seq_len, rows):
    if rows > 1:
        return seq_len, seq_len // rows, True
    block = CONV_BLOCK_ROWS if (n % CONV_BLOCK_ROWS == 0 and CONV_BLOCK_ROWS % seq_len == 0) else seq_len
    return block, seq_len, False


def _qk_conv_kernel(width, vertical, x_ref, w_ref, o_ref):
    o_ref[...] = _silu(_dwconv(x_ref[...], w_ref, width, vertical))


def _qk_conv(z, batch, seq_len, rows, conv_w9):
    n = batch * seq_len
    ch = 2 * MLSTM_WIDTH
    tc = CONV_CH_TILE
    off = ZM_OFF // tc
    block, width, vertical = _conv_geometry(n, seq_len, rows)
    return pl.pallas_call(
        functools.partial(_qk_conv_kernel, width, vertical),
        grid=(n // block, ch // tc),
        in_specs=[pl.BlockSpec((block, tc), lambda b, j: (b, off + j)),
                  pl.BlockSpec((9, tc), lambda b, j: (0, j))],
        out_specs=pl.BlockSpec((block, tc), lambda b, j: (b, j)),
        out_shape=jax.ShapeDtypeStruct((n, ch), F32),
        compiler_params=_params(("arbitrary", "arbitrary")),
        name="mlstm_qk_conv",
    )(z, conv_w9)


def _ffn_conv_kernel(width, vertical, ua_ref, uv_ref, w_ref, b_ref, o_ref):
    act = _dwconv(ua_ref[...], w_ref, width, vertical) + b_ref[...]
    o_ref[...] = (_silu(act) * uv_ref[...]).astype(BF16)


def _ffn_conv(u, batch, seq_len, rows, conv_w9, conv_b):
    n = batch * seq_len
    tc = CONV_CH_TILE
    nct = D_FF // tc
    block, width, vertical = _conv_geometry(n, seq_len, rows)
    return pl.pallas_call(
        functools.partial(_ffn_conv_kernel, width, vertical),
        grid=(n // block, nct),
        in_specs=[pl.BlockSpec((block, tc), lambda b, j: (b, j)),
                  pl.BlockSpec((block, tc), lambda b, j: (b, nct + j)),
                  pl.BlockSpec((9, tc), lambda b, j: (0, j)),
                  pl.BlockSpec((1, tc), lambda b, j: (0, j))],
        out_specs=pl.BlockSpec((block, tc), lambda b, j: (b, j)),
        out_shape=jax.ShapeDtypeStruct((n, D_FF), BF16),
        compiler_params=_params(("arbitrary", "arbitrary")),
        name="ffn_conv",
    )(u, u, conv_w9, conv_b)


MLSTM_STEP_CHUNKS = 4

def _mlstm_scan_kernel(qkf_ref, qkb_ref, ktf_ref, ktb_ref, vf_ref, vb_ref, gcf_ref, gcb_ref, grf_ref, grb_ref,
                       gbc_ref, gbr_ref, c0_ref, n0_ref, m0_ref,
                       hf_ref, hb_ref, cout_ref, nout_ref, mout_ref,
                       c_scr, n_scr, m_scr):
    step = pl.program_id(1)
    L = MLSTM_CHUNK
    dh = MLSTM_HEAD_DIM
    H = MLSTM_HEADS

    @pl.when(step == 0)
    def _():
        c_scr[...] = c0_ref[0]
        n_scr[...] = n0_ref[0]
        m_scr[...] = m0_ref[0]

    K = MLSTM_STEP_CHUNKS
    R = K * L
    row = lax.broadcasted_iota(jnp.int32, (L, L), 0)
    col = lax.broadcasted_iota(jnp.int32, (L, L), 1)
    lower = (row >= col)
    upper = (row <= col)
    lower_b = jnp.where(lower, 1.0, 0.0).astype(BF16)
    upper_b = jnp.where(upper, 1.0, 0.0).astype(BF16)
    rrow = lax.broadcasted_iota(jnp.int32, (R, R), 0)
    rcol = lax.broadcasted_iota(jnp.int32, (R, R), 1)
    same_chunk = jnp.bitwise_and(rrow, -L) == jnp.bitwise_and(rcol, -L)
    neg_inf = jnp.full((), -jnp.inf, F32)

    gcol, grow, bcol, brow, btot = [], [], [], [], []
    ones_b = jnp.ones((L, LANE), BF16)
    for d in range(N_DIR):
        gc_ref, gr_ref = (gcf_ref, grf_ref) if d == 0 else (gcb_ref, grb_ref)
        gcol.append(gc_ref[...] + gbc_ref[...])
        grow.append((gr_ref[...] + gbr_ref[...][None]).reshape(K * MLSTM_GATES, L))
        before = same_chunk & ((rrow >= rcol) if d == 0 else (rrow <= rcol))
        bcol.append(_mm_exact_lhs(jnp.where(before, 1.0, 0.0).astype(BF16), jax.nn.log_sigmoid(gcol[d]), _NN))
        frow = jax.nn.log_sigmoid(grow[d])
        brow.append(_mm_exact_rhs(frow, upper_b if d == 0 else lower_b, _NN))
        btot.append(_mm_exact_rhs(frow, ones_b, _NN))

    units = [(j, d, h) for j in range(K) for d in range(N_DIR) for h in range(H)]
    nun = range(len(units))
    q, k, kt, v, vb, qb = [], [], [], [], [], []
    c_row, b_col, b_last = [], [], []
    for j, d, h in units:
        at = j if d == 0 else K - 1 - j
        rows = slice(at * L, (at + 1) * L)
        st = d * H + h
        gi, gf = st, 2 * H + st
        qk_ref, kt_ref, v_ref = (qkf_ref, ktf_ref, vf_ref) if d == 0 else (qkb_ref, ktb_ref, vb_ref)
        q.append(qk_ref[rows, h * dh:(h + 1) * dh] * (dh ** -0.5))
        k.append(qk_ref[rows, MLSTM_WIDTH + h * dh:MLSTM_WIDTH + (h + 1) * dh])
        kt.append(kt_ref[at, h * dh:(h + 1) * dh, :])
        v.append(v_ref[rows, h * dh:(h + 1) * dh])
        qb.append(q[-1].astype(BF16))
        vb.append(v[-1].astype(BF16))
        b_col.append(jnp.broadcast_to(bcol[d][rows, gf:gf + 1], (L, LANE)))
        c_row.append(grow[d][at * MLSTM_GATES + gi:at * MLSTM_GATES + gi + 1, :]
                     - brow[d][at * MLSTM_GATES + gf:at * MLSTM_GATES + gf + 1, :])
        b_last.append(btot[d][at * MLSTM_GATES + gf:at * MLSTM_GATES + gf + 1, :])

    last = [L - 1 if d == 0 else 0 for _, d, _ in units]
    qk_t = [_dg(qb[i], k[i].astype(BF16), _NT) for i in nun]
    rel = [jnp.where(lower if units[i][1] == 0 else upper, c_row[i], neg_inf) for i in nun]
    mx = [jnp.broadcast_to(jnp.max(rel[i], axis=-1, keepdims=True), (L, LANE)) for i in nun]
    m_loc = [b_col[i] + mx[i] for i in nun]
    s_loc = [qk_t[i] * jnp.exp(rel[i] - mx[i][:, 0:L]) for i in nun]
    s_v = [_dg(s_loc[i].astype(BF16), vb[i], _NN) for i in nun]
    s_sum = [jnp.broadcast_to(jnp.sum(s_loc[i], axis=-1, keepdims=True), (L, LANE)) for i in nun]
    cmax = [mx[i][last[i]:last[i] + 1, :] for i in nun]
    m_w = [b_last[i] + cmax[i] for i in nun]
    wj = [jnp.exp(c_row[i] - cmax[i][:, 0:L]) for i in nun]
    kv = [_dg((kt[i] * wj[i]).astype(BF16), vb[i], _NN) for i in nun]
    w_k = [_mm(jnp.broadcast_to(wj[i], (8, L)), k[i], _NN, 3)[0:1] for i in nun]

    nst = N_DIR * H
    c_st = [c_scr[st] for st in range(nst)]
    n_st = [n_scr[st:st + 1, :] for st in range(nst)]
    m_st = [m_scr[st:st + 1, :] for st in range(nst)]
    for j in range(K):
        idx = [j * nst + st for st in range(nst)]
        q_c = [_dg(qb[i], c_st[st].astype(BF16), _NN) for st, i in enumerate(idx)]
        for st, i in enumerate(idx):
            _, d, h = units[i]
            at = j if d == 0 else K - 1 - j
            h_ref = hf_ref if d == 0 else hb_ref
            log_inter = b_col[i] + m_st[st]
            m_s = jnp.maximum(log_inter, m_loc[i])
            inter = jnp.exp(log_inter - m_s)
            local = jnp.exp(m_loc[i] - m_s)
            q_n = jnp.broadcast_to(jnp.sum(q[i] * n_st[st], axis=-1, keepdims=True), (L, LANE))
            den = inter * q_n + local * s_sum[i]
            scale = 1.0 / jnp.maximum(jnp.abs(den), jnp.exp(-m_s))
            h_ref[at * L:(at + 1) * L, h * dh:(h + 1) * dh] = (inter * scale) * q_c[st] + (local * scale) * s_v[i]
            m_new = jnp.maximum(b_last[i] + m_st[st], m_w[i])
            carry = jnp.exp(b_last[i] + m_st[st] - m_new)
            fresh = jnp.exp(m_w[i] - m_new)
            c_st[st] = carry * c_st[st] + fresh * kv[i]
            n_st[st] = carry * n_st[st] + fresh * w_k[i]
            m_st[st] = m_new

    for st in range(nst):
        c_scr[st] = c_st[st]
        n_scr[st:st + 1, :] = n_st[st]
        m_scr[st:st + 1, :] = m_st[st]
    cout_ref[0] = c_scr[...]
    nout_ref[0] = n_scr[...]
    mout_ref[0] = m_scr[...]


def _mlstm_scan(z, qk, qk_blk, kt, gt, gate_bc, gate_br, c0, n0, m0, batch, seq_len):
    K = MLSTM_STEP_CHUNKS
    L = K * MLSTM_CHUNK
    assert seq_len % L == 0
    cps = seq_len // L
    n = batch * seq_len
    W = MLSTM_WIDTH
    nst = N_DIR * MLSTM_HEADS
    dh = MLSTM_HEAD_DIM

    def fw(b, c):
        return b * cps + c

    def bw(b, c):
        return b * cps + cps - 1 - c

    vblk = (ZM_OFF + 2 * W) // W
    gblk = ZG_OFF // LANE
    return pl.pallas_call(
        _mlstm_scan_kernel,
        grid=(batch, cps),
        in_specs=[pl.BlockSpec((L, 2 * W), lambda b, c: (fw(b, c), qk_blk)),
                  pl.BlockSpec((L, 2 * W), lambda b, c: (bw(b, c), qk_blk)),
                  pl.BlockSpec((K, W, MLSTM_CHUNK), lambda b, c: (fw(b, c), 0, 0)),
                  pl.BlockSpec((K, W, MLSTM_CHUNK), lambda b, c: (bw(b, c), 0, 0)),
                  pl.BlockSpec((L, W), lambda b, c: (fw(b, c), vblk)),
                  pl.BlockSpec((L, W), lambda b, c: (bw(b, c), vblk)),
                  pl.BlockSpec((L, LANE), lambda b, c: (fw(b, c), gblk)),
                  pl.BlockSpec((L, LANE), lambda b, c: (bw(b, c), gblk)),
                  pl.BlockSpec((K, MLSTM_GATES, MLSTM_CHUNK), lambda b, c: (fw(b, c), 0, 0)),
                  pl.BlockSpec((K, MLSTM_GATES, MLSTM_CHUNK), lambda b, c: (bw(b, c), 0, 0)),
                  _resident((1, LANE)),
                  _resident((MLSTM_GATES, 1)),
                  pl.BlockSpec((1, nst, dh, dh), lambda b, c: (b, 0, 0, 0)),
                  pl.BlockSpec((1, nst, dh), lambda b, c: (b, 0, 0)),
                  pl.BlockSpec((1, nst, LANE), lambda b, c: (b, 0, 0))],
        out_specs=[pl.BlockSpec((L, W), lambda b, c: (fw(b, c), 0)),
                   pl.BlockSpec((L, W), lambda b, c: (bw(b, c), 0)),
                   pl.BlockSpec((1, nst, dh, dh), lambda b, c: (b, 0, 0, 0)),
                   pl.BlockSpec((1, nst, dh), lambda b, c: (b, 0, 0)),
                   pl.BlockSpec((1, nst, LANE), lambda b, c: (b, 0, 0))],
        out_shape=[jax.ShapeDtypeStruct((n, W), F32), jax.ShapeDtypeStruct((n, W), F32),
                   jax.ShapeDtypeStruct((batch, nst, dh, dh), F32),
                   jax.ShapeDtypeStruct((batch, nst, dh), F32),
                   jax.ShapeDtypeStruct((batch, nst, LANE), F32)],
        scratch_shapes=[pltpu.VMEM((nst, dh, dh), F32), pltpu.VMEM((nst, dh), F32),
                        pltpu.VMEM((nst, LANE), F32)],
        compiler_params=_params(("arbitrary", "arbitrary")),
        name="mlstm_scan",
    )(qk, qk, kt, kt, z, z, z, z, gt, gt, gate_bc, gate_br, c0, n0, m0)


def _merge_kernel(conv_width, x_ref, mod_ref, ysf_ref, ysb_ref, bonus_ref, gate_ref, hf_ref, hb_ref, zo_ref,
                  zs_ref, lnxg_ref, lnxb_ref, gng_ref, pmean_ref, wbr_ref, wbm_ref, wout_ref, ng_ref, wup_ref,
                  cw_ref, cb_ref, x1_ref, u_ref):
    mod = mod_ref[0]
    g1 = mod[:, 2 * D_MODEL:3 * D_MODEL]
    sh2 = mod[:, 3 * D_MODEL:4 * D_MODEL]
    sc2 = mod[:, 4 * D_MODEL:5 * D_MODEL]

    ys = ysf_ref[...] + ysb_ref[...]
    pmean = pmean_ref[...]
    mean = _mm_exact_rhs(ys, pmean, _NN)
    cen = ys - mean
    var = _mm_exact_rhs(cen * cen, pmean, _NN)
    y_r = (cen * lax.rsqrt(var + RWKV_GN_EPS) * lnxg_ref[...] + lnxb_ref[...] + bonus_ref[...]) * gate_ref[...]

    hs = hf_ref[...] + hb_ref[...]
    parts = []
    for h in range(MLSTM_HEADS):
        hh = hs[:, h * MLSTM_HEAD_DIM:(h + 1) * MLSTM_HEAD_DIM]
        mu = jnp.mean(hh, axis=-1, keepdims=True)
        ce = hh - mu
        va = jnp.mean(ce * ce, axis=-1, keepdims=True)
        parts.append(ce * lax.rsqrt(va + MLSTM_GN_EPS))
    y_m = jnp.concatenate(parts, axis=1) * gng_ref[...] * _sigmoid(zo_ref[...])

    gates = zs_ref[...]
    merged = (gates[:, 0:D_MODEL] * _dg(y_r.astype(BF16), wbr_ref[...], _NN)
              + gates[:, D_MODEL:2 * D_MODEL] * _dg(y_m.astype(BF16), wbm_ref[...], _NN))
    t = _dg(merged.astype(BF16), wout_ref[...], _NN)
    x1 = x_ref[...] + g1 * _rms(t, ng_ref[1:2, :])
    x1_ref[...] = x1
    h2 = _rms(x1, ng_ref[2:3, :]) * (1.0 + sc2) + sh2
    u = _dg(h2.astype(BF16), wup_ref[...], _NN)
    if conv_width is None:
        u_ref[...] = u
    else:
        act = _dwconv(u[:, 0:D_FF], cw_ref, conv_width, False) + cb_ref[...]
        u_ref[...] = (_silu(act) * u[:, D_FF:2 * D_FF]).astype(BF16)


def _merge(x2, mod, mod_row, z, ysf, ysb, bonus, gate, hf, hb, p, conv_width):
    n = x2.shape[0]
    W = RWKV_WIDTH
    rows = MERGE_TILE
    tile = lambda w: pl.BlockSpec((rows, w), lambda i: (i, 0))
    u_out = ((tile(2 * D_FF), jax.ShapeDtypeStruct((n, 2 * D_FF), F32)) if conv_width is None else
             (tile(D_FF), jax.ShapeDtypeStruct((n, D_FF), BF16)))
    return pl.pallas_call(
        functools.partial(_merge_kernel, conv_width),
        grid=(n // rows,),
        in_specs=[tile(D_MODEL),
                  pl.BlockSpec((1, 1, 6 * D_MODEL), lambda i: (mod_row(i * rows), 0, 0)),
                  tile(W), tile(W), tile(W), tile(W), tile(MLSTM_WIDTH), tile(MLSTM_WIDTH),
                  pl.BlockSpec((rows, MLSTM_WIDTH), lambda i: (i, (ZM_OFF + 3 * MLSTM_WIDTH) // MLSTM_WIDTH)),
                  pl.BlockSpec((rows, GATE_COLS), lambda i: (i, ZS_OFF // GATE_COLS)),
                  _resident((1, W)), _resident((1, W)), _resident((1, MLSTM_WIDTH)),
                  _resident((W, W)),
                  _resident((W, D_MODEL)), _resident((MLSTM_WIDTH, D_MODEL)),
                  _resident((D_MODEL, D_MODEL)), _resident((4, D_MODEL)),
                  _resident((D_MODEL, 2 * D_FF)), _resident((9, D_FF)), _resident((1, D_FF))],
        out_specs=[tile(D_MODEL), u_out[0]],
        out_shape=[jax.ShapeDtypeStruct((n, D_MODEL), F32), u_out[1]],
        compiler_params=_params(("arbitrary",)),
        name="merge_ffn_up",
    )(x2, mod, ysf, ysb, bonus, gate, hf, hb, z, z, p["lnx_g"], p["lnx_b"], p["gn_g"], p["pmean"],
      p["w_br"], p["w_bm"], p["w_out"], p["norm_g"], p["ffn_up"], p["ffn_conv"], p["ffn_conv_b"])


def _down_kernel(x1_ref, mod_ref, a_ref, w_ref, ng_ref, o_ref):
    g2 = mod_ref[0][:, 5 * D_MODEL:6 * D_MODEL]
    f = _dg(a_ref[...], w_ref[...], _NN)
    o_ref[...] = x1_ref[...] + g2 * _rms(f, ng_ref[3:4, :])


def _down(x1, mod, mod_row, act, p):
    n = x1.shape[0]
    tile = DOWN_TILE
    return pl.pallas_call(
        _down_kernel,
        grid=(n // tile,),
        in_specs=[pl.BlockSpec((tile, D_MODEL), lambda i: (i, 0)),
                  pl.BlockSpec((1, 1, 6 * D_MODEL), lambda i: (mod_row(i * tile), 0, 0)),
                  pl.BlockSpec((tile, D_FF), lambda i: (i, 0)),
                  _resident((D_FF, D_MODEL)), _resident((4, D_MODEL))],
        out_specs=pl.BlockSpec((tile, D_MODEL), lambda i: (i, 0)),
        out_shape=jax.ShapeDtypeStruct((n, D_MODEL), F32),
        compiler_params=_params(("arbitrary",)),
        name="ffn_down",
    )(x1, mod, act, p["ffn_down"], p["norm_g"])


def _conv_down_kernel(tiles_per_image, width, x1_ref, mod_ref, u_ref, up_ref, un_ref, cw_ref, cb_ref,
                      w_ref, ng_ref, o_ref):
    pos = pl.program_id(0) % tiles_per_image
    T = u_ref.shape[0]
    ua = u_ref[:, 0:D_FF]
    above_row = jnp.where(pos != 0, up_ref[...], 0.0)
    below_row = jnp.where(pos != tiles_per_image - 1, un_ref[...], 0.0)
    ext = jnp.concatenate([above_row, ua, below_row], axis=0)
    E = T + 2 * width
    colp = jnp.bitwise_and(lax.broadcasted_iota(jnp.int32, (E, 1), 0), width - 1)
    left = jnp.where(colp == 0, 0.0, pltpu.roll(ext, 1, 0))
    right = jnp.where(colp == width - 1, 0.0, pltpu.roll(ext, E - 1, 0))

    def tap_row(i):
        rows = slice(i * width, i * width + T)
        return (cw_ref[3 * i:3 * i + 1, :] * left[rows] + cw_ref[3 * i + 1:3 * i + 2, :] * ext[rows]
                + cw_ref[3 * i + 2:3 * i + 3, :] * right[rows])

    conv = tap_row(0) + tap_row(1) + tap_row(2) + cb_ref[...]
    act = (_silu(conv) * u_ref[:, D_FF:2 * D_FF]).astype(BF16)
    g2 = mod_ref[0][:, 5 * D_MODEL:6 * D_MODEL]
    f = _dg(act, w_ref[...], _NN)
    o_ref[...] = x1_ref[...] + g2 * _rms(f, ng_ref[3:4, :])


def _conv_down(x1, mod, mod_row, u, seq_len, rows, p):
    n = x1.shape[0]
    tile = CONV_DOWN_TILE
    width = seq_len // rows
    assert width & (width - 1) == 0 and tile % width == 0 and seq_len % tile == 0
    per_tile = tile // width
    last = n // width - 1
    return pl.pallas_call(
        functools.partial(_conv_down_kernel, seq_len // tile, width),
        grid=(n // tile,),
        in_specs=[pl.BlockSpec((tile, D_MODEL), lambda i: (i, 0)),
                  pl.BlockSpec((1, 1, 6 * D_MODEL), lambda i: (mod_row(i * tile), 0, 0)),
                  pl.BlockSpec((tile, 2 * D_FF), lambda i: (i, 0)),
                  pl.BlockSpec((width, D_FF), lambda i: (jnp.maximum(i * per_tile - 1, 0), 0)),
                  pl.BlockSpec((width, D_FF), lambda i: (jnp.minimum((i + 1) * per_tile, last), 0)),
                  _resident((9, D_FF)), _resident((1, D_FF)),
                  _resident((D_FF, D_MODEL)), _resident((4, D_MODEL))],
        out_specs=pl.BlockSpec((tile, D_MODEL), lambda i: (i, 0)),
        out_shape=jax.ShapeDtypeStruct((n, D_MODEL), F32),
        compiler_params=_params(("arbitrary",)),
        name="ffn_conv_down",
    )(x1, mod, u, u, u, p["ffn_conv"], p["ffn_conv_b"], p["ffn_down"], p["norm_g"])


RWKV_LOCAL_PASSES = 1


def _state_to_pairs(s):
    b = s.shape[0]
    s = s.reshape(b, N_DIR, RWKV_PAIRS, 2, RWKV_HEAD_DIM, RWKV_HEAD_DIM)
    zero = jnp.zeros_like(s[:, :, :, 0])
    top = jnp.concatenate([s[:, :, :, 0], zero], axis=-1)
    bot = jnp.concatenate([zero, s[:, :, :, 1]], axis=-1)
    return jnp.concatenate([top, bot], axis=-2)


def _state_from_pairs(sb):
    b = sb.shape[0]
    n = RWKV_HEAD_DIM
    parts = jnp.stack([sb[..., 0:n, 0:n], sb[..., n:2 * n, n:2 * n]], axis=3)
    return parts.reshape(b, N_DIR, RWKV_HEADS, n, n)


def _trunk(x, mod, mod_row, rows, states, p):
    batch, seq_len, _ = x.shape
    n = batch * seq_len
    x2 = x.reshape(n, D_MODEL)
    s0, c0, n0, m0 = states

    fuse_width = seq_len if (rows == 1 and IN_TILE % seq_len == 0 and MERGE_TILE % seq_len == 0) else None
    z = _in_proj(x2, mod, mod_row, p["norm_g"][0:1], p["w_r"], p["w_g"], p["w_m"], p["w_s"],
                 p["mlstm_conv"], fuse_width)

    rp, y0, gm, hm, gate, bonus = _rwkv_local(z, seq_len, p, RWKV_LOCAL_PASSES)
    ysf, ysb, s_fin = _rwkv_scan(_state_to_pairs(s0), rp, y0, gm, hm, batch, seq_len)
    s_fin = _state_from_pairs(s_fin)

    if fuse_width is None:
        qk, qk_blk, k_off = _qk_conv(z, batch, seq_len, rows, p["mlstm_conv"]), 0, MLSTM_WIDTH
    else:
        qk, qk_blk, k_off = z, ZM_OFF // (2 * MLSTM_WIDTH), ZM_OFF + MLSTM_WIDTH
    gt = z[:, ZG_OFF:ZG_OFF + MLSTM_GATES].reshape(n // MLSTM_CHUNK, MLSTM_CHUNK, MLSTM_GATES).transpose(0, 2, 1)
    nst = N_DIR * MLSTM_HEADS
    kt = qk[:, k_off:k_off + MLSTM_WIDTH].reshape(n // MLSTM_CHUNK, MLSTM_CHUNK, MLSTM_WIDTH).transpose(0, 2, 1)
    hf, hb, ct_fin, n_fin, m_fin = _mlstm_scan(
        z, qk, qk_blk, kt, gt, p["gate_bc"], p["gate_br"],
        jnp.swapaxes(c0, -1, -2).reshape(batch, nst, MLSTM_HEAD_DIM, MLSTM_HEAD_DIM),
        n0.reshape(batch, nst, MLSTM_HEAD_DIM),
        jnp.broadcast_to(m0.reshape(batch, nst, 1), (batch, nst, LANE)), batch, seq_len)
    c_fin = jnp.swapaxes(ct_fin, -1, -2)

    if fuse_width is not None:
        x1, act = _merge(x2, mod, mod_row, z, ysf, ysb, bonus, gate, hf, hb, p, fuse_width)
        out = _down(x1, mod, mod_row, act, p)
    else:
        x1, u = _merge(x2, mod, mod_row, z, ysf, ysb, bonus, gate, hf, hb, p, None)
        if rows > 1:
            out = _conv_down(x1, mod, mod_row, u, seq_len, rows, p)
        else:
            out = _down(x1, mod, mod_row, _ffn_conv(u, batch, seq_len, rows, p["ffn_conv"], p["ffn_conv_b"]), p)

    new_states = (s_fin,
                  c_fin.reshape(batch, N_DIR, MLSTM_HEADS, MLSTM_HEAD_DIM, MLSTM_HEAD_DIM),
                  n_fin.reshape(batch, N_DIR, MLSTM_HEADS, MLSTM_HEAD_DIM),
                  m_fin[:, :, 0].reshape(batch, N_DIR, MLSTM_HEADS))
    return out.reshape(batch, seq_len, D_MODEL), new_states


def _pack_layer(l, ada_w, ada_b, norm_g, w_in, rwkv_mu, rwkv_w0, rwkv_w_up, rwkv_a0, rwkv_a_up,
                rwkv_g_up, rwkv_kk_scale, rwkv_k_a, rwkv_r_k, rwkv_lnx_g, rwkv_lnx_b, mlstm_conv,
                mlstm_gate_b, mlstm_gn_g, w_branch_rwkv, w_branch_mlstm, w_out, ffn_up, ffn_conv,
                ffn_conv_b, ffn_down):
    W = RWKV_WIDTH
    wi = w_in[l]
    w_r = wi[:, 0:RWKV_COLS].astype(BF16)
    w_m = wi[:, RWKV_COLS:RWKV_COLS + 4 * MLSTM_WIDTH].astype(BF16)
    w_g = jnp.pad(wi[:, RWKV_COLS + 4 * MLSTM_WIDTH:RWKV_COLS + MLSTM_COLS].astype(BF16),
                  ((0, 0), (0, LANE - MLSTM_GATES)))
    w_s = wi[:, RWKV_COLS + MLSTM_COLS:].astype(BF16)

    head = jnp.arange(W, dtype=jnp.int32) // RWKV_HEAD_DIM
    same = (head[:, None] == head[None, :])
    gb = mlstm_gate_b[l].reshape(1, MLSTM_GATES)
    return dict(
        ada_w=ada_w[l], ada_b=ada_b[l], norm_g=norm_g[l], w_r=w_r, w_g=w_g, w_m=w_m, w_s=w_s,
        mu=rwkv_mu[l].reshape(1, RWKV_COLS),
        w0=rwkv_w0[l].reshape(N_DIR, 1, W), w_up=rwkv_w_up[l],
        a0=rwkv_a0[l].reshape(N_DIR, 1, W), a_up=rwkv_a_up[l], g_up=rwkv_g_up[l],
        kk_scale=rwkv_kk_scale[l].reshape(1, W), k_a=rwkv_k_a[l].reshape(1, W),
        r_k=rwkv_r_k[l].reshape(1, W),
        lnx_g=rwkv_lnx_g[l].reshape(1, W), lnx_b=rwkv_lnx_b[l].reshape(1, W),
        pones=same.astype(BF16), pmean=(same.astype(F32) / RWKV_HEAD_DIM).astype(BF16),
        mlstm_conv=mlstm_conv[l].reshape(9, 2 * MLSTM_WIDTH),
        gate_bc=jnp.pad(gb, ((0, 0), (0, LANE - MLSTM_GATES))), gate_br=gb.reshape(MLSTM_GATES, 1),
        gn_g=mlstm_gn_g[l].reshape(1, MLSTM_WIDTH),
        w_br=w_branch_rwkv[l].astype(BF16), w_bm=w_branch_mlstm[l].astype(BF16),
        w_out=w_out[l].astype(BF16), ffn_up=ffn_up[l].astype(BF16),
        ffn_conv=ffn_conv[l].reshape(9, D_FF), ffn_conv_b=ffn_conv_b[l].reshape(1, D_FF),
        ffn_down=ffn_down[l].astype(BF16),
    )


def kernel(x_prompt, x_sample, c, state_rwkv, state_mlstm_C, state_mlstm_n, state_mlstm_m, c_ctx,
           ada_w, ada_b, norm_g, w_in, rwkv_mu, rwkv_w0, rwkv_w_up, rwkv_a0, rwkv_a_up, rwkv_g_up,
           rwkv_kk_scale, rwkv_k_a, rwkv_r_k, rwkv_lnx_g, rwkv_lnx_b, mlstm_conv, mlstm_gate_b,
           mlstm_gn_g, w_branch_rwkv, w_branch_mlstm, w_out, ffn_up, ffn_conv, ffn_conv_b, ffn_down):
    depth = ada_w.shape[0]
    batch = x_prompt.shape[0]
    dec_batch, dec_seq, _ = x_sample.shape
    latent_rows = dec_seq // GRID_W
    ctx_init = (jnp.zeros((batch, N_DIR, RWKV_HEADS, RWKV_HEAD_DIM, RWKV_HEAD_DIM), F32),
                jnp.zeros((batch, N_DIR, MLSTM_HEADS, MLSTM_HEAD_DIM, MLSTM_HEAD_DIM), F32),
                jnp.zeros((batch, N_DIR, MLSTM_HEADS, MLSTM_HEAD_DIM), F32),
                jnp.zeros((batch, N_DIR, MLSTM_HEADS), F32))
    cond = jnp.concatenate([c_ctx[None, :], c, jnp.zeros((8 - 1 - dec_batch, D_MODEL), F32)], axis=0)

    xp, xs = x_prompt, x_sample
    new_s, new_c, new_n, new_m = [], [], [], []
    for l in range(depth):
        p = _pack_layer(l, ada_w, ada_b, norm_g, w_in, rwkv_mu, rwkv_w0, rwkv_w_up, rwkv_a0, rwkv_a_up,
                        rwkv_g_up, rwkv_kk_scale, rwkv_k_a, rwkv_r_k, rwkv_lnx_g, rwkv_lnx_b, mlstm_conv,
                        mlstm_gate_b, mlstm_gn_g, w_branch_rwkv, w_branch_mlstm, w_out, ffn_up, ffn_conv,
                        ffn_conv_b, ffn_down)
        mod = _ada(cond, p["ada_w"], p["ada_b"]).reshape(8, 1, 6 * D_MODEL)
        xp, (s, cc, nn, mm) = _trunk(xp, mod, lambda r: 0, 1, ctx_init, p)
        new_s.append(s)
        new_c.append(cc)
        new_n.append(nn)
        new_m.append(mm)
        xs, _ = _trunk(xs, mod, lambda r: 1 + r // dec_seq, latent_rows,
                       (state_rwkv[:, l], state_mlstm_C[:, l], state_mlstm_n[:, l], state_mlstm_m[:, l]), p)
    return (xp, xs, jnp.stack(new_s, axis=1), jnp.stack(new_c, axis=1),
            jnp.stack(new_n, axis=1), jnp.stack(new_m, axis=1))
```

```python
import functools

import jax
import jax.numpy as jnp
from jax import lax
from jax.experimental import pallas as pl
from jax.experimental.pallas import tpu as pltpu

F32 = jnp.float32
BF16 = jnp.bfloat16

D_MODEL = 1024
N_DIR = 2
RWKV_HEADS = 8
RWKV_HEAD_DIM = 64
RWKV_WIDTH = RWKV_HEADS * RWKV_HEAD_DIM
DECAY_LORA = 64
ICLR_LORA = 64
GATE_LORA = 128
MLSTM_HEADS = 4
MLSTM_HEAD_DIM = 128
MLSTM_WIDTH = MLSTM_HEADS * MLSTM_HEAD_DIM
MLSTM_CHUNK = 64
D_FF = 2816
GRID_W = 64
RMS_EPS = 1e-6
RWKV_GN_EPS = 64e-5
MLSTM_GN_EPS = 1e-5
DECAY_SCALE = 0.606531

RWKV_COLS = 3 * RWKV_WIDTH + N_DIR * DECAY_LORA + N_DIR * ICLR_LORA + GATE_LORA
MLSTM_GATES = 2 * N_DIR * MLSTM_HEADS
MLSTM_COLS = 4 * MLSTM_WIDTH + MLSTM_GATES
GATE_COLS = 2 * D_MODEL

LANE = 128
ZR_BLOCK = 2048
ZG_OFF = RWKV_COLS
ZM_OFF = ZR_BLOCK
ZS_OFF = ZM_OFF + 4 * MLSTM_WIDTH
Z_COLS = ZS_OFF + GATE_COLS

IN_TILE = 512
MERGE_TILE = 256
DOWN_TILE = 512
CONV_DOWN_TILE = 256
RCHUNK = 64
CONV_CH_TILE = 256
VMEM_LIMIT = 56 * 1024 * 1024


def _params(sem):
    return pltpu.CompilerParams(dimension_semantics=sem, vmem_limit_bytes=VMEM_LIMIT)


def _resident(shape):
    nd = len(shape)
    return pl.BlockSpec(shape, lambda *_: (0,) * nd, pipeline_mode=pl.Buffered(1))


def _split2(a):
    hi = a.astype(BF16)
    lo = (a - hi.astype(F32)).astype(BF16)
    return hi, lo


def _split3(a):
    hi = a.astype(BF16)
    r1 = a - hi.astype(F32)
    mid = r1.astype(BF16)
    lo = (r1 - mid.astype(F32)).astype(BF16)
    return hi, mid, lo


def _dg(a, b, dims):
    return lax.dot_general(a, b, dims, preferred_element_type=F32)


def _mm(a, b, dims, passes):
    if passes == 1:
        return _dg(a.astype(BF16), b.astype(BF16), dims)
    ah, al = _split2(a)
    bh, bl = _split2(b)
    return _dg(ah, bh, dims) + (_dg(ah, bl, dims) + _dg(al, bh, dims))


def _mm_exact_lhs(a_bf16, b, dims):
    b1, b2, b3 = _split3(b)
    return _dg(a_bf16, b1, dims) + (_dg(a_bf16, b2, dims) + _dg(a_bf16, b3, dims))


def _mm_exact_rhs(a, b_bf16, dims, pieces=3):
    if pieces == 2:
        a1, a2 = _split2(a)
        return _dg(a1, b_bf16, dims) + _dg(a2, b_bf16, dims)
    a1, a2, a3 = _split3(a)
    return _dg(a1, b_bf16, dims) + (_dg(a2, b_bf16, dims) + _dg(a3, b_bf16, dims))


_NN = (((1,), (0,)), ((), ()))
_NT = (((1,), (1,)), ((), ()))
_TN = (((0,), (0,)), ((), ()))
_BNN = (((2,), (1,)), ((0,), (0,)))
_BNT = (((2,), (2,)), ((0,), (0,)))
_BTN = (((1,), (1,)), ((0,), (0,)))


def _sigmoid(x):
    return jax.nn.sigmoid(x)


def _silu(x):
    return x * jax.nn.sigmoid(x)


def _rms(x, g):
    return x * lax.rsqrt(jnp.mean(x * x, axis=-1, keepdims=True) + RMS_EPS) * g


def _ada_kernel(cond_ref, w_ref, b_ref, o_ref):
    s = _silu(cond_ref[...])
    o_ref[...] = _dg(s.astype(BF16), w_ref[...].astype(BF16), _NN) + b_ref[...]


def _ada(cond8, ada_w, ada_b):
    n = ada_w.shape[1]
    tn = 1536
    return pl.pallas_call(
        _ada_kernel,
        grid=(n // tn,),
        in_specs=[_resident((8, D_MODEL)),
                  pl.BlockSpec((D_MODEL, tn), lambda j: (0, j)),
                  pl.BlockSpec((1, tn), lambda j: (0, j))],
        out_specs=pl.BlockSpec((8, tn), lambda j: (0, j)),
        out_shape=jax.ShapeDtypeStruct((8, n), F32),
        compiler_params=_params(("arbitrary",)),
        name="ada_mod",
    )(cond8, ada_w, ada_b.reshape(1, n))


def _in_kernel(conv_width, x_ref, mod_ref, g_ref, wr_ref, wg_ref, wm_ref, ws_ref, cw_ref, z_ref):
    mod = mod_ref[0]
    sh = mod[:, 0:D_MODEL]
    sc = mod[:, D_MODEL:2 * D_MODEL]
    h = (_rms(x_ref[...], g_ref[...]) * (1.0 + sc) + sh).astype(BF16)
    z_ref[:, 0:ZG_OFF] = _dg(h, wr_ref[...], _NN)
    z_ref[:, ZG_OFF:ZM_OFF] = _dg(h, wg_ref[...], _NN)
    zm = _dg(h, wm_ref[...], _NN)
    if conv_width is None:
        z_ref[:, ZM_OFF:ZS_OFF] = zm
    else:
        qk_cols = 2 * MLSTM_WIDTH
        z_ref[:, ZM_OFF:ZM_OFF + qk_cols] = _silu(_dwconv(zm[:, 0:qk_cols], cw_ref, conv_width, False))
        z_ref[:, ZM_OFF + qk_cols:ZS_OFF] = zm[:, qk_cols:]
    z_ref[:, ZS_OFF:Z_COLS] = _sigmoid(_dg(h, ws_ref[...], _NN))


def _in_proj(x2, mod, mod_row, norm_g0, w_r, w_g, w_m, w_s, conv_w9, conv_width):
    n = x2.shape[0]
    tile = IN_TILE
    return pl.pallas_call(
        functools.partial(_in_kernel, conv_width),
        grid=(n // tile,),
        in_specs=[pl.BlockSpec((tile, D_MODEL), lambda i: (i, 0)),
                  pl.BlockSpec((1, 1, 6 * D_MODEL), lambda i: (mod_row(i * tile), 0, 0)),
                  _resident((1, D_MODEL)),
                  _resident(w_r.shape), _resident(w_g.shape), _resident(w_m.shape), _resident(w_s.shape),
                  _resident(conv_w9.shape)],
        out_specs=pl.BlockSpec((tile, Z_COLS), lambda i: (i, 0)),
        out_shape=jax.ShapeDtypeStruct((n, Z_COLS), F32),
        compiler_params=_params(("arbitrary",)),
        name="in_proj",
    )(x2, mod, norm_g0, w_r, w_g, w_m, w_s, conv_w9)


LOCAL_CHUNKS = 4
PAIR_LANES = 2 * RWKV_HEAD_DIM
RWKV_PAIRS = RWKV_HEADS // 2


def _bd(x):
    lane = lax.broadcasted_iota(jnp.int32, x.shape, 1)
    left = lane < RWKV_HEAD_DIM
    return jnp.concatenate([jnp.where(left, x, 0.0), jnp.where(left, 0.0, x)], axis=0)


def _rwkv_local_kernel(chunks_per_seq, passes,
                       z_ref, zp_ref, zn_ref, mu_ref, w0_ref, wup_ref, a0_ref, aup_ref, gup_ref,
                       kks_ref, ka_ref, rk_ref, pones_ref,
                       rp_ref, y0_ref, gm_ref, hm_ref, gate_ref, bonus_ref):
    C = RCHUNK
    W = RWKV_WIDTH
    NS = LOCAL_CHUNKS
    R = NS * C
    first = (pl.program_id(0) * NS) % chunks_per_seq
    has_prev = first != 0
    has_next = first + NS != chunks_per_seq

    z = z_ref[:, 0:RWKV_COLS]
    zp = jnp.where(has_prev, zp_ref[7:8, 0:RWKV_COLS], 0.0)
    zn = jnp.where(has_next, zn_ref[0:1, 0:RWKV_COLS], 0.0)
    trow = lax.broadcasted_iota(jnp.int32, (R, 1), 0)
    prev = jnp.where(trow == 0, zp, pltpu.roll(z, 1, 0))
    nxt = jnp.where(trow == R - 1, zn, pltpu.roll(z, R - 1, 0))
    zs = z + mu_ref[...] * (0.5 * (prev + nxt) - z)

    r = zs[:, 0:W]
    k = zs[:, W:2 * W]
    v = zs[:, 2 * W:3 * W]
    gd = zs[:, 3 * W + 2 * DECAY_LORA + 2 * ICLR_LORA:RWKV_COLS]
    gate_ref[...] = _dg(_sigmoid(gd).astype(BF16), gup_ref[...].astype(BF16), _NN)

    pones = pones_ref[...]
    kks = k * kks_ref[...]
    norm = jnp.sqrt(_mm_exact_rhs(kks * kks, pones, _NN, pieces=2))
    kk = kks / jnp.maximum(norm, 1e-12)

    P = PAIR_LANES
    row = lax.broadcasted_iota(jnp.int32, (R, R), 0)
    col = lax.broadcasted_iota(jnp.int32, (R, R), 1)
    same_chunk = jnp.bitwise_and(row, -C) == jnp.bitwise_and(col, -C)
    prow = lax.broadcasted_iota(jnp.int32, (C, P), 0)
    pcol = jnp.bitwise_and(lax.broadcasted_iota(jnp.int32, (C, P), 1), RWKV_HEAD_DIM - 1)
    eye_p = jnp.where(prow == pcol, 1.0, 0.0)
    left_head = lax.broadcasted_iota(jnp.int32, (C, P), 1) < RWKV_HEAD_DIM

    def diag_blocks(m):
        return jnp.where(left_head, m[0:RWKV_HEAD_DIM], m[RWKV_HEAD_DIM:P])

    abar, rbar, kt, bt, kw, bw, wc, strict, incl = [], [], [], [], [], [], [], [], []
    kd_sum = None
    for d in range(N_DIR):
        o = 3 * W + d * DECAY_LORA
        wd = zs[:, o:o + DECAY_LORA]
        o = 3 * W + 2 * DECAY_LORA + d * ICLR_LORA
        ad = zs[:, o:o + ICLR_LORA]
        logw = -DECAY_SCALE * _sigmoid(w0_ref[d] + _dg(jnp.tanh(wd).astype(BF16), wup_ref[d].astype(BF16), _NN))
        a = _sigmoid(a0_ref[d] + _dg(ad.astype(BF16), aup_ref[d].astype(BF16), _NN))
        kd = k * (1.0 + (a - 1.0) * ka_ref[...])
        b = kk * a
        kd_sum = kd if kd_sum is None else kd_sum + kd

        earlier_or_same = same_chunk & ((row >= col) if d == 0 else (row <= col))
        cum_i = _mm_exact_lhs(jnp.where(earlier_or_same, 1.0, 0.0).astype(BF16), logw, _NN)
        cum_e = cum_i - logw
        ab_d, rb_d, kt_d, bt_d, kw_d, bw_d, wc_d = [], [], [], [], [], [], []
        for s in range(NS):
            rs = slice(s * C, (s + 1) * C)
            ci_s = cum_i[rs]
            ctot = jnp.sum(logw[rs], axis=0, keepdims=True)
            e_ni = jnp.exp(-ci_s)
            e_ti = jnp.exp(ctot - ci_s)
            ab_d.append(kk[rs] * jnp.exp(cum_e[rs]))
            rb_d.append(r[rs] * jnp.exp(ci_s))
            kt_d.append(kd[rs] * e_ni)
            bt_d.append(b[rs] * e_ni)
            kw_d.append(kd[rs] * e_ti)
            bw_d.append(b[rs] * e_ti)
            wc_d.append(jnp.exp(ctot))
        abar.append(ab_d)
        rbar.append(rb_d)
        kt.append(kt_d)
        bt.append(bt_d)
        kw.append(kw_d)
        bw.append(bw_d)
        wc.append(wc_d)
        strict.append((prow > pcol) if d == 0 else (prow < pcol))
        incl.append((prow >= pcol) if d == 0 else (prow <= pcol))
    bonus_ref[...] = _mm_exact_rhs(r * kd_sum * rk_ref[...], pones, _NN, pieces=2) * v

    mm = functools.partial(_mm, passes=passes)
    chains = [(s, d, p) for s in range(NS) for d in range(N_DIR) for p in range(RWKV_PAIRS)]
    nch = range(len(chains))

    def sel(arr, i):
        s, d, p = chains[i]
        return arr[d][s][:, p * P:(p + 1) * P]

    cat0 = lambda a_, b_: jnp.concatenate([a_, b_], axis=0)
    cat1 = lambda a_, b_: jnp.concatenate([a_, b_], axis=1)
    vsl = [v[s * C:(s + 1) * C, p * P:(p + 1) * P] for s, _, p in chains]
    lhs = [cat0(sel(abar, i), sel(rbar, i)) for i in nch]
    by_b = [mm(lhs[i], _bd(sel(bt, i)), _NT) for i in nch]
    by_k = [mm(lhs[i], _bd(sel(kt, i)), _NT) for i in nch]
    a_kk = [jnp.where(strict[chains[i][1]], by_b[i][0:C], 0.0) for i in nch]
    a_rb = [jnp.where(incl[chains[i][1]], by_b[i][C:2 * C], 0.0) for i in nch]
    a_kv = [jnp.where(strict[chains[i][1]], by_k[i][0:C], 0.0) for i in nch]
    a_rk = [jnp.where(incl[chains[i][1]], by_k[i][C:2 * C], 0.0) for i in nch]
    on_v = [mm(cat0(a_kv[i], a_rk[i]), _bd(vsl[i]), _NN) for i in nch]

    x = [-m for m in a_kk]
    tinv = [eye_p + m for m in x]
    x = [mm(m, _bd(m), _NN) for m in x]
    for _ in range(4):
        both = [mm(cat0(tinv[i], x[i]), _bd(x[i]), _NN) for i in nch]
        tinv = [tinv[i] + both[i][0:C] for i in nch]
        x = [m[C:2 * C] for m in both]
    tinv = [tinv[i] + mm(tinv[i], _bd(x[i]), _NN) for i in nch]

    solved = [mm(tinv[i], cat1(_bd(sel(abar, i)), _bd(on_v[i][0:C])), _NN) for i in nch]
    ap = [m[:, 0:P] for m in solved]
    u0 = [m[:, P:2 * P] for m in solved]
    corr = [mm(a_rb[i], cat1(_bd(ap[i]), _bd(u0[i])), _NN) for i in nch]
    on_b = [mm(cat1(ap[i], u0[i]), sel(bw, i), _TN) for i in nch]
    vk = [mm(vsl[i], sel(kw, i), _TN) for i in nch]
    for i in nch:
        s, d, p = chains[i]
        rows = slice(s * C, (s + 1) * C)
        lanes = slice(p * P, (p + 1) * P)
        rp_ref[d, rows, lanes] = (sel(rbar, i) - corr[i][:, 0:P]).astype(BF16)
        y0_ref[d, rows, lanes] = on_v[i][C:2 * C] - corr[i][:, P:2 * P]
        gm_ref[d, s, p] = (eye_p * sel(wc, i) - diag_blocks(on_b[i][0:P])).astype(BF16)
        hm_ref[d, s, p] = diag_blocks(vk[i] - on_b[i][P:2 * P])


def _rwkv_local(z, seq_len, p, passes):
    n = z.shape[0]
    nchunk = n // RCHUNK
    cps = seq_len // RCHUNK
    assert cps % LOCAL_CHUNKS == 0
    W = RWKV_WIDTH
    rows = LOCAL_CHUNKS * RCHUNK
    hb = rows // 8
    last8 = n // 8 - 1
    mat = lambda dt: jax.ShapeDtypeStruct((N_DIR, nchunk, RWKV_PAIRS, RWKV_HEAD_DIM, PAIR_LANES), dt)
    mat_spec = pl.BlockSpec((N_DIR, LOCAL_CHUNKS, RWKV_PAIRS, RWKV_HEAD_DIM, PAIR_LANES),
                            lambda c: (0, c, 0, 0, 0))
    tok = lambda dt: jax.ShapeDtypeStruct((N_DIR, n, W), dt)
    tok_spec = pl.BlockSpec((N_DIR, rows, W), lambda c: (0, c, 0))
    row_spec = pl.BlockSpec((rows, W), lambda c: (c, 0))
    return pl.pallas_call(
        functools.partial(_rwkv_local_kernel, cps, passes),
        grid=(nchunk // LOCAL_CHUNKS,),
        in_specs=[pl.BlockSpec((rows, ZR_BLOCK), lambda c: (c, 0)),
                  pl.BlockSpec((8, ZR_BLOCK), lambda c: (jnp.maximum(c * hb - 1, 0), 0)),
                  pl.BlockSpec((8, ZR_BLOCK), lambda c: (jnp.minimum((c + 1) * hb, last8), 0)),
                  _resident((1, RWKV_COLS)),
                  _resident((N_DIR, 1, W)), _resident((N_DIR, DECAY_LORA, W)),
                  _resident((N_DIR, 1, W)), _resident((N_DIR, ICLR_LORA, W)),
                  _resident((GATE_LORA, W)),
                  _resident((1, W)), _resident((1, W)), _resident((1, W)),
                  _resident((W, W))],
        out_specs=[tok_spec, tok_spec, mat_spec, mat_spec, row_spec, row_spec],
        out_shape=[tok(BF16), tok(F32), mat(BF16), mat(F32),
                   jax.ShapeDtypeStruct((n, W), F32), jax.ShapeDtypeStruct((n, W), F32)],
        compiler_params=_params(("arbitrary",)),
        name="rwkv_local",
    )(z, z, z, p["mu"], p["w0"], p["w_up"], p["a0"], p["a_up"], p["g_up"],
      p["kk_scale"], p["k_a"], p["r_k"], p["pones"])


SCAN_CHUNKS = 4


def _rwkv_scan_kernel(s0_ref, rpf_ref, rpb_ref, y0f_ref, y0b_ref, gmf_ref, gmb_ref, hmf_ref, hmb_ref,
                      ysf_ref, ysb_ref, sout_ref, s_scr):
    @pl.when(pl.program_id(1) == 0)
    def _():
        s_scr[...] = s0_ref[0]

    K = SCAN_CHUNKS
    C = RCHUNK
    rp_ref, y0_ref, gm_ref, hm_ref, ys_ref = ((rpf_ref, rpb_ref), (y0f_ref, y0b_ref), (gmf_ref, gmb_ref),
                                              (hmf_ref, hmb_ref), (ysf_ref, ysb_ref))
    chains = [(d, p) for d in range(N_DIR) for p in range(RWKV_PAIRS)]
    lanes = [slice(p * PAIR_LANES, (p + 1) * PAIR_LANES) for _, p in chains]
    nch = range(len(chains))
    s = [s_scr[d, p] for d, p in chains]
    for j in range(K):
        at = (j, K - 1 - j)
        rows = [slice(at[d] * C, (at[d] + 1) * C) for d, _ in chains]
        sb = [m.astype(BF16) for m in s]
        y = [_dg(rp_ref[chains[i][0]][0, rows[i], lanes[i]], sb[i], _NT) for i in nch]
        sg = [_dg(sb[i], _bd(gm_ref[chains[i][0]][0, at[chains[i][0]], chains[i][1]]), _NN) for i in nch]
        for i in nch:
            d, p = chains[i]
            ys_ref[d][rows[i], lanes[i]] = y[i] + y0_ref[d][0, rows[i], lanes[i]]
        s = [sg[i] + _bd(hm_ref[chains[i][0]][0, at[chains[i][0]], chains[i][1]]) for i in nch]
    for i in nch:
        d, p = chains[i]
        s_scr[d, p] = s[i]
        sout_ref[0, d, p] = s[i]


def _rwkv_scan(s0, rp, y0, gm, hm, batch, seq_len):
    K = SCAN_CHUNKS
    spb = seq_len // (RCHUNK * K)
    n = batch * seq_len

    def fwd(b, s):
        return b * spb + s

    def bwd(b, s):
        return b * spb + spb - 1 - s

    def mat_spec(d, at):
        return pl.BlockSpec((1, K, RWKV_PAIRS, RWKV_HEAD_DIM, PAIR_LANES), lambda b, s: (d, at(b, s), 0, 0, 0))

    def tok_spec(d, at):
        return pl.BlockSpec((1, K * RCHUNK, RWKV_WIDTH), lambda b, s: (d, at(b, s), 0))

    st_spec = pl.BlockSpec((1, N_DIR, RWKV_PAIRS, PAIR_LANES, PAIR_LANES), lambda b, s: (b, 0, 0, 0, 0))
    ys = jax.ShapeDtypeStruct((n, RWKV_WIDTH), F32)
    return pl.pallas_call(
        _rwkv_scan_kernel,
        grid=(batch, spb),
        in_specs=[st_spec, tok_spec(0, fwd), tok_spec(1, bwd), tok_spec(0, fwd), tok_spec(1, bwd),
                  mat_spec(0, fwd), mat_spec(1, bwd), mat_spec(0, fwd), mat_spec(1, bwd)],
        out_specs=[pl.BlockSpec((K * RCHUNK, RWKV_WIDTH), lambda b, s: (fwd(b, s), 0)),
                   pl.BlockSpec((K * RCHUNK, RWKV_WIDTH), lambda b, s: (bwd(b, s), 0)),
                   st_spec],
        out_shape=[ys, ys,
                   jax.ShapeDtypeStruct((batch, N_DIR, RWKV_PAIRS, PAIR_LANES, PAIR_LANES), F32)],
        scratch_shapes=[pltpu.VMEM((N_DIR, RWKV_PAIRS, PAIR_LANES, PAIR_LANES), F32)],
        compiler_params=_params(("arbitrary", "arbitrary")),
        name="rwkv_scan",
    )(s0, rp, rp, y0, y0, gm, gm, hm, hm)


CONV_BLOCK_ROWS = 2048


def _dwconv(x, w_ref, width, vertical):
    T = x.shape[0]
    t = lax.broadcasted_iota(jnp.int32, (T, 1), 0)
    assert width & (width - 1) == 0
    colp = jnp.bitwise_and(t, width - 1)
    xl = jnp.where(colp == 0, 0.0, pltpu.roll(x, 1, 0))
    xr = jnp.where(colp == width - 1, 0.0, pltpu.roll(x, T - 1, 0))

    def tap_row(i):
        return w_ref[3 * i:3 * i + 1, :] * xl + w_ref[3 * i + 1:3 * i + 2, :] * x + w_ref[3 * i + 2:3 * i + 3, :] * xr

    out = tap_row(1)
    if vertical:
        out = out + jnp.where(t < width, 0.0, pltpu.roll(tap_row(0), width, 0))
        out = out + jnp.where(t >= T - width, 0.0, pltpu.roll(tap_row(2), T - width, 0))
    return out


def _conv_geometry(n, seq_len, rows):
    if rows > 1:
        return seq_len, seq_len // rows, True
    block = CONV_BLOCK_ROWS if (n % CONV_BLOCK_ROWS == 0 and CONV_BLOCK_ROWS % seq_len == 0) else seq_len
    return block, seq_len, False


def _qk_conv_kernel(width, vertical, x_ref, w_ref, o_ref):
    o_ref[...] = _silu(_dwconv(x_ref[...], w_ref, width, vertical))


def _qk_conv(z, batch, seq_len, rows, conv_w9):
    n = batch * seq_len
    ch = 2 * MLSTM_WIDTH
    tc = CONV_CH_TILE
    off = ZM_OFF // tc
    block, width, vertical = _conv_geometry(n, seq_len, rows)
    return pl.pallas_call(
        functools.partial(_qk_conv_kernel, width, vertical),
        grid=(n // block, ch // tc),
        in_specs=[pl.BlockSpec((block, tc), lambda b, j: (b, off + j)),
                  pl.BlockSpec((9, tc), lambda b, j: (0, j))],
        out_specs=pl.BlockSpec((block, tc), lambda b, j: (b, j)),
        out_shape=jax.ShapeDtypeStruct((n, ch), F32),
        compiler_params=_params(("arbitrary", "arbitrary")),
        name="mlstm_qk_conv",
    )(z, conv_w9)


def _ffn_conv_kernel(width, vertical, ua_ref, uv_ref, w_ref, b_ref, o_ref):
    act = _dwconv(ua_ref[...], w_ref, width, vertical) + b_ref[...]
    o_ref[...] = (_silu(act) * uv_ref[...]).astype(BF16)


def _ffn_conv(u, batch, seq_len, rows, conv_w9, conv_b):
    n = batch * seq_len
    tc = CONV_CH_TILE
    nct = D_FF // tc
    block, width, vertical = _conv_geometry(n, seq_len, rows)
    return pl.pallas_call(
        functools.partial(_ffn_conv_kernel, width, vertical),
        grid=(n // block, nct),
        in_specs=[pl.BlockSpec((block, tc), lambda b, j: (b, j)),
                  pl.BlockSpec((block, tc), lambda b, j: (b, nct + j)),
                  pl.BlockSpec((9, tc), lambda b, j: (0, j)),
                  pl.BlockSpec((1, tc), lambda b, j: (0, j))],
        out_specs=pl.BlockSpec((block, tc), lambda b, j: (b, j)),
        out_shape=jax.ShapeDtypeStruct((n, D_FF), BF16),
        compiler_params=_params(("arbitrary", "arbitrary")),
        name="ffn_conv",
    )(u, u, conv_w9, conv_b)


MLSTM_STEP_CHUNKS = 4

def _mlstm_scan_kernel(qkf_ref, qkb_ref, ktf_ref, ktb_ref, vf_ref, vb_ref, gcf_ref, gcb_ref, grf_ref, grb_ref,
                       gbc_ref, gbr_ref, c0_ref, n0_ref, m0_ref,
                       hf_ref, hb_ref, cout_ref, nout_ref, mout_ref,
                       c_scr, n_scr, m_scr):
    step = pl.program_id(1)
    L = MLSTM_CHUNK
    dh = MLSTM_HEAD_DIM
    H = MLSTM_HEADS

    @pl.when(step == 0)
    def _():
        c_scr[...] = c0_ref[0]
        n_scr[...] = n0_ref[0]
        m_scr[...] = m0_ref[0]

    K = MLSTM_STEP_CHUNKS
    R = K * L
    row = lax.broadcasted_iota(jnp.int32, (L, L), 0)
    col = lax.broadcasted_iota(jnp.int32, (L, L), 1)
    lower = (row >= col)
    upper = (row <= col)
    lower_b = jnp.where(lower, 1.0, 0.0).astype(BF16)
    upper_b = jnp.where(upper, 1.0, 0.0).astype(BF16)
    rrow = lax.broadcasted_iota(jnp.int32, (R, R), 0)
    rcol = lax.broadcasted_iota(jnp.int32, (R, R), 1)
    same_chunk = jnp.bitwise_and(rrow, -L) == jnp.bitwise_and(rcol, -L)
    neg_inf = jnp.full((), -jnp.inf, F32)

    gcol, grow, bcol, brow, btot = [], [], [], [], []
    ones_b = jnp.ones((L, LANE), BF16)
    for d in range(N_DIR):
        gc_ref, gr_ref = (gcf_ref, grf_ref) if d == 0 else (gcb_ref, grb_ref)
        gcol.append(gc_ref[...] + gbc_ref[...])
        grow.append((gr_ref[...] + gbr_ref[...][None]).reshape(K * MLSTM_GATES, L))
        before = same_chunk & ((rrow >= rcol) if d == 0 else (rrow <= rcol))
        bcol.append(_mm_exact_lhs(jnp.where(before, 1.0, 0.0).astype(BF16), jax.nn.log_sigmoid(gcol[d]), _NN))
        frow = jax.nn.log_sigmoid(grow[d])
        brow.append(_mm_exact_rhs(frow, upper_b if d == 0 else lower_b, _NN))
        btot.append(_mm_exact_rhs(frow, ones_b, _NN))

    units = [(j, d, h) for j in range(K) for d in range(N_DIR) for h in range(H)]
    nun = range(len(units))
    q, k, kt, v, vb, qb = [], [], [], [], [], []
    c_row, b_col, b_last = [], [], []
    for j, d, h in units:
        at = j if d == 0 else K - 1 - j
        rows = slice(at * L, (at + 1) * L)
        st = d * H + h
        gi, gf = st, 2 * H + st
        qk_ref, kt_ref, v_ref = (qkf_ref, ktf_ref, vf_ref) if d == 0 else (qkb_ref, ktb_ref, vb_ref)
        q.append(qk_ref[rows, h * dh:(h + 1) * dh] * (dh ** -0.5))
        k.append(qk_ref[rows, MLSTM_WIDTH + h * dh:MLSTM_WIDTH + (h + 1) * dh])
        kt.append(kt_ref[at, h * dh:(h + 1) * dh, :])
        v.append(v_ref[rows, h * dh:(h + 1) * dh])
        qb.append(q[-1].astype(BF16))
        vb.append(v[-1].astype(BF16))
        b_col.append(jnp.broadcast_to(bcol[d][rows, gf:gf + 1], (L, LANE)))
        c_row.append(grow[d][at * MLSTM_GATES + gi:at * MLSTM_GATES + gi + 1, :]
                     - brow[d][at * MLSTM_GATES + gf:at * MLSTM_GATES + gf + 1, :])
        b_last.append(btot[d][at * MLSTM_GATES + gf:at * MLSTM_GATES + gf + 1, :])

    last = [L - 1 if d == 0 else 0 for _, d, _ in units]
    qk_t = [_dg(qb[i], k[i].astype(BF16), _NT) for i in nun]
    rel = [jnp.where(lower if units[i][1] == 0 else upper, c_row[i], neg_inf) for i in nun]
    mx = [jnp.broadcast_to(jnp.max(rel[i], axis=-1, keepdims=True), (L, LANE)) for i in nun]
    m_loc = [b_col[i] + mx[i] for i in nun]
    s_loc = [qk_t[i] * jnp.exp(rel[i] - mx[i][:, 0:L]) for i in nun]
    s_v = [_dg(s_loc[i].astype(BF16), vb[i], _NN) for i in nun]
    s_sum = [jnp.broadcast_to(jnp.sum(s_loc[i], axis=-1, keepdims=True), (L, LANE)) for i in nun]
    cmax = [mx[i][last[i]:last[i] + 1, :] for i in nun]
    m_w = [b_last[i] + cmax[i] for i in nun]
    wj = [jnp.exp(c_row[i] - cmax[i][:, 0:L]) for i in nun]
    kv = [_dg((kt[i] * wj[i]).astype(BF16), vb[i], _NN) for i in nun]
    w_k = [_mm(jnp.broadcast_to(wj[i], (8, L)), k[i], _NN, 3)[0:1] for i in nun]

    nst = N_DIR * H
    c_st = [c_scr[st] for st in range(nst)]
    n_st = [n_scr[st:st + 1, :] for st in range(nst)]
    m_st = [m_scr[st:st + 1, :] for st in range(nst)]
    for j in range(K):
        idx = [j * nst + st for st in range(nst)]
        q_c = [_dg(qb[i], c_st[st].astype(BF16), _NN) for st, i in enumerate(idx)]
        for st, i in enumerate(idx):
            _, d, h = units[i]
            at = j if d == 0 else K - 1 - j
            h_ref = hf_ref if d == 0 else hb_ref
            log_inter = b_col[i] + m_st[st]
            m_s = jnp.maximum(log_inter, m_loc[i])
            inter = jnp.exp(log_inter - m_s)
            local = jnp.exp(m_loc[i] - m_s)
            q_n = jnp.broadcast_to(jnp.sum(q[i] * n_st[st], axis=-1, keepdims=True), (L, LANE))
            den = inter * q_n + local * s_sum[i]
            scale = 1.0 / jnp.maximum(jnp.abs(den), jnp.exp(-m_s))
            h_ref[at * L:(at + 1) * L, h * dh:(h + 1) * dh] = (inter * scale) * q_c[st] + (local * scale) * s_v[i]
            m_new = jnp.maximum(b_last[i] + m_st[st], m_w[i])
            carry = jnp.exp(b_last[i] + m_st[st] - m_new)
            fresh = jnp.exp(m_w[i] - m_new)
            c_st[st] = carry * c_st[st] + fresh * kv[i]
            n_st[st] = carry * n_st[st] + fresh * w_k[i]
            m_st[st] = m_new

    for st in range(nst):
        c_scr[st] = c_st[st]
        n_scr[st:st + 1, :] = n_st[st]
        m_scr[st:st + 1, :] = m_st[st]
    cout_ref[0] = c_scr[...]
    nout_ref[0] = n_scr[...]
    mout_ref[0] = m_scr[...]


def _mlstm_scan(z, qk, qk_blk, kt, gt, gate_bc, gate_br, c0, n0, m0, batch, seq_len):
    K = MLSTM_STEP_CHUNKS
    L = K * MLSTM_CHUNK
    assert seq_len % L == 0
    cps = seq_len // L
    n = batch * seq_len
    W = MLSTM_WIDTH
    nst = N_DIR * MLSTM_HEADS
    dh = MLSTM_HEAD_DIM

    def fw(b, c):
        return b * cps + c

    def bw(b, c):
        return b * cps + cps - 1 - c

    vblk = (ZM_OFF + 2 * W) // W
    gblk = ZG_OFF // LANE
    return pl.pallas_call(
        _mlstm_scan_kernel,
        grid=(batch, cps),
        in_specs=[pl.BlockSpec((L, 2 * W), lambda b, c: (fw(b, c), qk_blk)),
                  pl.BlockSpec((L, 2 * W), lambda b, c: (bw(b, c), qk_blk)),
                  pl.BlockSpec((K, W, MLSTM_CHUNK), lambda b, c: (fw(b, c), 0, 0)),
                  pl.BlockSpec((K, W, MLSTM_CHUNK), lambda b, c: (bw(b, c), 0, 0)),
                  pl.BlockSpec((L, W), lambda b, c: (fw(b, c), vblk)),
                  pl.BlockSpec((L, W), lambda b, c: (bw(b, c), vblk)),
                  pl.BlockSpec((L, LANE), lambda b, c: (fw(b, c), gblk)),
                  pl.BlockSpec((L, LANE), lambda b, c: (bw(b, c), gblk)),
                  pl.BlockSpec((K, MLSTM_GATES, MLSTM_CHUNK), lambda b, c: (fw(b, c), 0, 0)),
                  pl.BlockSpec((K, MLSTM_GATES, MLSTM_CHUNK), lambda b, c: (bw(b, c), 0, 0)),
                  _resident((1, LANE)),
                  _resident((MLSTM_GATES, 1)),
                  pl.BlockSpec((1, nst, dh, dh), lambda b, c: (b, 0, 0, 0)),
                  pl.BlockSpec((1, nst, dh), lambda b, c: (b, 0, 0)),
                  pl.BlockSpec((1, nst, LANE), lambda b, c: (b, 0, 0))],
        out_specs=[pl.BlockSpec((L, W), lambda b, c: (fw(b, c), 0)),
                   pl.BlockSpec((L, W), lambda b, c: (bw(b, c), 0)),
                   pl.BlockSpec((1, nst, dh, dh), lambda b, c: (b, 0, 0, 0)),
                   pl.BlockSpec((1, nst, dh), lambda b, c: (b, 0, 0)),
                   pl.BlockSpec((1, nst, LANE), lambda b, c: (b, 0, 0))],
        out_shape=[jax.ShapeDtypeStruct((n, W), F32), jax.ShapeDtypeStruct((n, W), F32),
                   jax.ShapeDtypeStruct((batch, nst, dh, dh), F32),
                   jax.ShapeDtypeStruct((batch, nst, dh), F32),
                   jax.ShapeDtypeStruct((batch, nst, LANE), F32)],
        scratch_shapes=[pltpu.VMEM((nst, dh, dh), F32), pltpu.VMEM((nst, dh), F32),
                        pltpu.VMEM((nst, LANE), F32)],
        compiler_params=_params(("arbitrary", "arbitrary")),
        name="mlstm_scan",
    )(qk, qk, kt, kt, z, z, z, z, gt, gt, gate_bc, gate_br, c0, n0, m0)


def _merge_kernel(conv_width, x_ref, mod_ref, ysf_ref, ysb_ref, bonus_ref, gate_ref, hf_ref, hb_ref, zo_ref,
                  zs_ref, lnxg_ref, lnxb_ref, gng_ref, pmean_ref, wbr_ref, wbm_ref, wout_ref, ng_ref, wup_ref,
                  cw_ref, cb_ref, x1_ref, u_ref):
    mod = mod_ref[0]
    g1 = mod[:, 2 * D_MODEL:3 * D_MODEL]
    sh2 = mod[:, 3 * D_MODEL:4 * D_MODEL]
    sc2 = mod[:, 4 * D_MODEL:5 * D_MODEL]

    ys = ysf_ref[...] + ysb_ref[...]
    pmean = pmean_ref[...]
    mean = _mm_exact_rhs(ys, pmean, _NN)
    cen = ys - mean
    var = _mm_exact_rhs(cen * cen, pmean, _NN)
    y_r = (cen * lax.rsqrt(var + RWKV_GN_EPS) * lnxg_ref[...] + lnxb_ref[...] + bonus_ref[...]) * gate_ref[...]

    hs = hf_ref[...] + hb_ref[...]
    parts = []
    for h in range(MLSTM_HEADS):
        hh = hs[:, h * MLSTM_HEAD_DIM:(h + 1) * MLSTM_HEAD_DIM]
        mu = jnp.mean(hh, axis=-1, keepdims=True)
        ce = hh - mu
        va = jnp.mean(ce * ce, axis=-1, keepdims=True)
        parts.append(ce * lax.rsqrt(va + MLSTM_GN_EPS))
    y_m = jnp.concatenate(parts, axis=1) * gng_ref[...] * _sigmoid(zo_ref[...])

    gates = zs_ref[...]
    merged = (gates[:, 0:D_MODEL] * _dg(y_r.astype(BF16), wbr_ref[...], _NN)
              + gates[:, D_MODEL:2 * D_MODEL] * _dg(y_m.astype(BF16), wbm_ref[...], _NN))
    t = _dg(merged.astype(BF16), wout_ref[...], _NN)
    x1 = x_ref[...] + g1 * _rms(t, ng_ref[1:2, :])
    x1_ref[...] = x1
    h2 = _rms(x1, ng_ref[2:3, :]) * (1.0 + sc2) + sh2
    u = _dg(h2.astype(BF16), wup_ref[...], _NN)
    if conv_width is None:
        u_ref[...] = u
    else:
        act = _dwconv(u[:, 0:D_FF], cw_ref, conv_width, False) + cb_ref[...]
        u_ref[...] = (_silu(act) * u[:, D_FF:2 * D_FF]).astype(BF16)


def _merge(x2, mod, mod_row, z, ysf, ysb, bonus, gate, hf, hb, p, conv_width):
    n = x2.shape[0]
    W = RWKV_WIDTH
    rows = MERGE_TILE
    tile = lambda w: pl.BlockSpec((rows, w), lambda i: (i, 0))
    u_out = ((tile(2 * D_FF), jax.ShapeDtypeStruct((n, 2 * D_FF), F32)) if conv_width is None else
             (tile(D_FF), jax.ShapeDtypeStruct((n, D_FF), BF16)))
    return pl.pallas_call(
        functools.partial(_merge_kernel, conv_width),
        grid=(n // rows,),
        in_specs=[tile(D_MODEL),
                  pl.BlockSpec((1, 1, 6 * D_MODEL), lambda i: (mod_row(i * rows), 0, 0)),
                  tile(W), tile(W), tile(W), tile(W), tile(MLSTM_WIDTH), tile(MLSTM_WIDTH),
                  pl.BlockSpec((rows, MLSTM_WIDTH), lambda i: (i, (ZM_OFF + 3 * MLSTM_WIDTH) // MLSTM_WIDTH)),
                  pl.BlockSpec((rows, GATE_COLS), lambda i: (i, ZS_OFF // GATE_COLS)),
                  _resident((1, W)), _resident((1, W)), _resident((1, MLSTM_WIDTH)),
                  _resident((W, W)),
                  _resident((W, D_MODEL)), _resident((MLSTM_WIDTH, D_MODEL)),
                  _resident((D_MODEL, D_MODEL)), _resident((4, D_MODEL)),
                  _resident((D_MODEL, 2 * D_FF)), _resident((9, D_FF)), _resident((1, D_FF))],
        out_specs=[tile(D_MODEL), u_out[0]],
        out_shape=[jax.ShapeDtypeStruct((n, D_MODEL), F32), u_out[1]],
        compiler_params=_params(("arbitrary",)),
        name="merge_ffn_up",
    )(x2, mod, ysf, ysb, bonus, gate, hf, hb, z, z, p["lnx_g"], p["lnx_b"], p["gn_g"], p["pmean"],
      p["w_br"], p["w_bm"], p["w_out"], p["norm_g"], p["ffn_up"], p["ffn_conv"], p["ffn_conv_b"])


def _down_kernel(x1_ref, mod_ref, a_ref, w_ref, ng_ref, o_ref):
    g2 = mod_ref[0][:, 5 * D_MODEL:6 * D_MODEL]
    f = _dg(a_ref[...], w_ref[...], _NN)
    o_ref[...] = x1_ref[...] + g2 * _rms(f, ng_ref[3:4, :])


def _down(x1, mod, mod_row, act, p):
    n = x1.shape[0]
    tile = DOWN_TILE
    return pl.pallas_call(
        _down_kernel,
        grid=(n // tile,),
        in_specs=[pl.BlockSpec((tile, D_MODEL), lambda i: (i, 0)),
                  pl.BlockSpec((1, 1, 6 * D_MODEL), lambda i: (mod_row(i * tile), 0, 0)),
                  pl.BlockSpec((tile, D_FF), lambda i: (i, 0)),
                  _resident((D_FF, D_MODEL)), _resident((4, D_MODEL))],
        out_specs=pl.BlockSpec((tile, D_MODEL), lambda i: (i, 0)),
        out_shape=jax.ShapeDtypeStruct((n, D_MODEL), F32),
        compiler_params=_params(("arbitrary",)),
        name="ffn_down",
    )(x1, mod, act, p["ffn_down"], p["norm_g"])


def _conv_down_kernel(tiles_per_image, width, x1_ref, mod_ref, u_ref, up_ref, un_ref, cw_ref, cb_ref,
                      w_ref, ng_ref, o_ref):
    pos = pl.program_id(0) % tiles_per_image
    T = u_ref.shape[0]
    ua = u_ref[:, 0:D_FF]
    above_row = jnp.where(pos != 0, up_ref[...], 0.0)
    below_row = jnp.where(pos != tiles_per_image - 1, un_ref[...], 0.0)
    ext = jnp.concatenate([above_row, ua, below_row], axis=0)
    E = T + 2 * width
    colp = jnp.bitwise_and(lax.broadcasted_iota(jnp.int32, (E, 1), 0), width - 1)
    left = jnp.where(colp == 0, 0.0, pltpu.roll(ext, 1, 0))
    right = jnp.where(colp == width - 1, 0.0, pltpu.roll(ext, E - 1, 0))

    def tap_row(i):
        rows = slice(i * width, i * width + T)
        return (cw_ref[3 * i:3 * i + 1, :] * left[rows] + cw_ref[3 * i + 1:3 * i + 2, :] * ext[rows]
                + cw_ref[3 * i + 2:3 * i + 3, :] * right[rows])

    conv = tap_row(0) + tap_row(1) + tap_row(2) + cb_ref[...]
    act = (_silu(conv) * u_ref[:, D_FF:2 * D_FF]).astype(BF16)
    g2 = mod_ref[0][:, 5 * D_MODEL:6 * D_MODEL]
    f = _dg(act, w_ref[...], _NN)
    o_ref[...] = x1_ref[...] + g2 * _rms(f, ng_ref[3:4, :])


def _conv_down(x1, mod, mod_row, u, seq_len, rows, p):
    n = x1.shape[0]
    tile = CONV_DOWN_TILE
    width = seq_len // rows
    assert width & (width - 1) == 0 and tile % width == 0 and seq_len % tile == 0
    per_tile = tile // width
    last = n // width - 1
    return pl.pallas_call(
        functools.partial(_conv_down_kernel, seq_len // tile, width),
        grid=(n // tile,),
        in_specs=[pl.BlockSpec((tile, D_MODEL), lambda i: (i, 0)),
                  pl.BlockSpec((1, 1, 6 * D_MODEL), lambda i: (mod_row(i * tile), 0, 0)),
                  pl.BlockSpec((tile, 2 * D_FF), lambda i: (i, 0)),
                  pl.BlockSpec((width, D_FF), lambda i: (jnp.maximum(i * per_tile - 1, 0), 0)),
                  pl.BlockSpec((width, D_FF), lambda i: (jnp.minimum((i + 1) * per_tile, last), 0)),
                  _resident((9, D_FF)), _resident((1, D_FF)),
                  _resident((D_FF, D_MODEL)), _resident((4, D_MODEL))],
        out_specs=pl.BlockSpec((tile, D_MODEL), lambda i: (i, 0)),
        out_shape=jax.ShapeDtypeStruct((n, D_MODEL), F32),
        compiler_params=_params(("arbitrary",)),
        name="ffn_conv_down",
    )(x1, mod, u, u, u, p["ffn_conv"], p["ffn_conv_b"], p["ffn_down"], p["norm_g"])


RWKV_LOCAL_PASSES = 1


def _state_to_pairs(s):
    b = s.shape[0]
    s = s.reshape(b, N_DIR, RWKV_PAIRS, 2, RWKV_HEAD_DIM, RWKV_HEAD_DIM)
    zero = jnp.zeros_like(s[:, :, :, 0])
    top = jnp.concatenate([s[:, :, :, 0], zero], axis=-1)
    bot = jnp.concatenate([zero, s[:, :, :, 1]], axis=-1)
    return jnp.concatenate([top, bot], axis=-2)


def _state_from_pairs(sb):
    b = sb.shape[0]
    n = RWKV_HEAD_DIM
    parts = jnp.stack([sb[..., 0:n, 0:n], sb[..., n:2 * n, n:2 * n]], axis=3)
    return parts.reshape(b, N_DIR, RWKV_HEADS, n, n)


def _trunk(x, mod, mod_row, rows, states, p):
    batch, seq_len, _ = x.shape
    n = batch * seq_len
    x2 = x.reshape(n, D_MODEL)
    s0, c0, n0, m0 = states

    fuse_width = seq_len if (rows == 1 and IN_TILE % seq_len == 0 and MERGE_TILE % seq_len == 0) else None
    z = _in_proj(x2, mod, mod_row, p["norm_g"][0:1], p["w_r"], p["w_g"], p["w_m"], p["w_s"],
                 p["mlstm_conv"], fuse_width)

    rp, y0, gm, hm, gate, bonus = _rwkv_local(z, seq_len, p, RWKV_LOCAL_PASSES)
    ysf, ysb, s_fin = _rwkv_scan(_state_to_pairs(s0), rp, y0, gm, hm, batch, seq_len)
    s_fin = _state_from_pairs(s_fin)

    if fuse_width is None:
        qk, qk_blk, k_off = _qk_conv(z, batch, seq_len, rows, p["mlstm_conv"]), 0, MLSTM_WIDTH
    else:
        qk, qk_blk, k_off = z, ZM_OFF // (2 * MLSTM_WIDTH), ZM_OFF + MLSTM_WIDTH
    gt = z[:, ZG_OFF:ZG_OFF + MLSTM_GATES].reshape(n // MLSTM_CHUNK, MLSTM_CHUNK, MLSTM_GATES).transpose(0, 2, 1)
    nst = N_DIR * MLSTM_HEADS
    kt = qk[:, k_off:k_off + MLSTM_WIDTH].reshape(n // MLSTM_CHUNK, MLSTM_CHUNK, MLSTM_WIDTH).transpose(0, 2, 1)
    hf, hb, ct_fin, n_fin, m_fin = _mlstm_scan(
        z, qk, qk_blk, kt, gt, p["gate_bc"], p["gate_br"],
        jnp.swapaxes(c0, -1, -2).reshape(batch, nst, MLSTM_HEAD_DIM, MLSTM_HEAD_DIM),
        n0.reshape(batch, nst, MLSTM_HEAD_DIM),
        jnp.broadcast_to(m0.reshape(batch, nst, 1), (batch, nst, LANE)), batch, seq_len)
    c_fin = jnp.swapaxes(ct_fin, -1, -2)

    if fuse_width is not None:
        x1, act = _merge(x2, mod, mod_row, z, ysf, ysb, bonus, gate, hf, hb, p, fuse_width)
        out = _down(x1, mod, mod_row, act, p)
    else:
        x1, u = _merge(x2, mod, mod_row, z, ysf, ysb, bonus, gate, hf, hb, p, None)
        if rows > 1:
            out = _conv_down(x1, mod, mod_row, u, seq_len, rows, p)
        else:
            out = _down(x1, mod, mod_row, _ffn_conv(u, batch, seq_len, rows, p["ffn_conv"], p["ffn_conv_b"]), p)

    new_states = (s_fin,
                  c_fin.reshape(batch, N_DIR, MLSTM_HEADS, MLSTM_HEAD_DIM, MLSTM_HEAD_DIM),
                  n_fin.reshape(batch, N_DIR, MLSTM_HEADS, MLSTM_HEAD_DIM),
                  m_fin[:, :, 0].reshape(batch, N_DIR, MLSTM_HEADS))
    return out.reshape(batch, seq_len, D_MODEL), new_states


def _pack_layer(l, ada_w, ada_b, norm_g, w_in, rwkv_mu, rwkv_w0, rwkv_w_up, rwkv_a0, rwkv_a_up,
                rwkv_g_up, rwkv_kk_scale, rwkv_k_a, rwkv_r_k, rwkv_lnx_g, rwkv_lnx_b, mlstm_conv,
                mlstm_gate_b, mlstm_gn_g, w_branch_rwkv, w_branch_mlstm, w_out, ffn_up, ffn_conv,
                ffn_conv_b, ffn_down):
    W = RWKV_WIDTH
    wi = w_in[l]
    w_r = wi[:, 0:RWKV_COLS].astype(BF16)
    w_m = wi[:, RWKV_COLS:RWKV_COLS + 4 * MLSTM_WIDTH].astype(BF16)
    w_g = jnp.pad(wi[:, RWKV_COLS + 4 * MLSTM_WIDTH:RWKV_COLS + MLSTM_COLS].astype(BF16),
                  ((0, 0), (0, LANE - MLSTM_GATES)))
    w_s = wi[:, RWKV_COLS + MLSTM_COLS:].astype(BF16)

    head = jnp.arange(W, dtype=jnp.int32) // RWKV_HEAD_DIM
    same = (head[:, None] == head[None, :])
    gb = mlstm_gate_b[l].reshape(1, MLSTM_GATES)
    return dict(
        ada_w=ada_w[l], ada_b=ada_b[l], norm_g=norm_g[l], w_r=w_r, w_g=w_g, w_m=w_m, w_s=w_s,
        mu=rwkv_mu[l].reshape(1, RWKV_COLS),
        w0=rwkv_w0[l].reshape(N_DIR, 1, W), w_up=rwkv_w_up[l],
        a0=rwkv_a0[l].reshape(N_DIR, 1, W), a_up=rwkv_a_up[l], g_up=rwkv_g_up[l],
        kk_scale=rwkv_kk_scale[l].reshape(1, W), k_a=rwkv_k_a[l].reshape(1, W),
        r_k=rwkv_r_k[l].reshape(1, W),
        lnx_g=rwkv_lnx_g[l].reshape(1, W), lnx_b=rwkv_lnx_b[l].reshape(1, W),
        pones=same.astype(BF16), pmean=(same.astype(F32) / RWKV_HEAD_DIM).astype(BF16),
        mlstm_conv=mlstm_conv[l].reshape(9, 2 * MLSTM_WIDTH),
        gate_bc=jnp.pad(gb, ((0, 0), (0, LANE - MLSTM_GATES))), gate_br=gb.reshape(MLSTM_GATES, 1),
        gn_g=mlstm_gn_g[l].reshape(1, MLSTM_WIDTH),
        w_br=w_branch_rwkv[l].astype(BF16), w_bm=w_branch_mlstm[l].astype(BF16),
        w_out=w_out[l].astype(BF16), ffn_up=ffn_up[l].astype(BF16),
        ffn_conv=ffn_conv[l].reshape(9, D_FF), ffn_conv_b=ffn_conv_b[l].reshape(1, D_FF),
        ffn_down=ffn_down[l].astype(BF16),
    )


def kernel(x_prompt, x_sample, c, state_rwkv, state_mlstm_C, state_mlstm_n, state_mlstm_m, c_ctx,
           ada_w, ada_b, norm_g, w_in, rwkv_mu, rwkv_w0, rwkv_w_up, rwkv_a0, rwkv_a_up, rwkv_g_up,
           rwkv_kk_scale, rwkv_k_a, rwkv_r_k, rwkv_lnx_g, rwkv_lnx_b, mlstm_conv, mlstm_gate_b,
           mlstm_gn_g, w_branch_rwkv, w_branch_mlstm, w_out, ffn_up, ffn_conv, ffn_conv_b, ffn_down):
    depth = ada_w.shape[0]
    batch = x_prompt.shape[0]
    dec_batch, dec_seq, _ = x_sample.shape
    latent_rows = dec_seq // GRID_W
    ctx_init = (jnp.zeros((batch, N_DIR, RWKV_HEADS, RWKV_HEAD_DIM, RWKV_HEAD_DIM), F32),
                jnp.zeros((batch, N_DIR, MLSTM_HEADS, MLSTM_HEAD_DIM, MLSTM_HEAD_DIM), F32),
                jnp.zeros((batch, N_DIR, MLSTM_HEADS, MLSTM_HEAD_DIM), F32),
                jnp.zeros((batch, N_DIR, MLSTM_HEADS), F32))
    cond = jnp.concatenate([c_ctx[None, :], c, jnp.zeros((8 - 1 - dec_batch, D_MODEL), F32)], axis=0)

    xp, xs = x_prompt, x_sample
    new_s, new_c, new_n, new_m = [], [], [], []
    for l in range(depth):
        p = _pack_layer(l, ada_w, ada_b, norm_g, w_in, rwkv_mu, rwkv_w0, rwkv_w_up, rwkv_a0, rwkv_a_up,
                        rwkv_g_up, rwkv_kk_scale, rwkv_k_a, rwkv_r_k, rwkv_lnx_g, rwkv_lnx_b, mlstm_conv,
                        mlstm_gate_b, mlstm_gn_g, w_branch_rwkv, w_branch_mlstm, w_out, ffn_up, ffn_conv,
                        ffn_conv_b, ffn_down)
        mod = _ada(cond, p["ada_w"], p["ada_b"]).reshape(8, 1, 6 * D_MODEL)
        xp, (s, cc, nn, mm) = _trunk(xp, mod, lambda r: 0, 1, ctx_init, p)
        new_s.append(s)
        new_c.append(cc)
        new_n.append(nn)
        new_m.append(mm)
        xs, _ = _trunk(xs, mod, lambda r: 1 + r // dec_seq, latent_rows,
                       (state_rwkv[:, l], state_mlstm_C[:, l], state_mlstm_n[:, l], state_mlstm_m[:, l]), p)
    return (xp, xs, jnp.stack(new_s, axis=1), jnp.stack(new_c, axis=1),
            jnp.stack(new_n, axis=1), jnp.stack(new_m, axis=1))
```

```python
import functools

import jax
import jax.numpy as jnp
from jax import lax
from jax.experimental import pallas as pl
from jax.experimental.pallas import tpu as pltpu

F32 = jnp.float32
BF16 = jnp.bfloat16

D_MODEL = 1024
N_DIR = 2
RWKV_HEADS = 8
RWKV_HEAD_DIM = 64
RWKV_WIDTH = RWKV_HEADS * RWKV_HEAD_DIM
DECAY_LORA = 64
ICLR_LORA = 64
GATE_LORA = 128
MLSTM_HEADS = 4
MLSTM_HEAD_DIM = 128
MLSTM_WIDTH = MLSTM_HEADS * MLSTM_HEAD_DIM
MLSTM_CHUNK = 64
D_FF = 2816
GRID_W = 64
RMS_EPS = 1e-6
RWKV_GN_EPS = 64e-5
MLSTM_GN_EPS = 1e-5
DECAY_SCALE = 0.606531

RWKV_COLS = 3 * RWKV_WIDTH + N_DIR * DECAY_LORA + N_DIR * ICLR_LORA + GATE_LORA
MLSTM_GATES = 2 * N_DIR * MLSTM_HEADS
MLSTM_COLS = 4 * MLSTM_WIDTH + MLSTM_GATES
GATE_COLS = 2 * D_MODEL

LANE = 128
ZR_BLOCK = 2048
ZG_OFF = RWKV_COLS
ZM_OFF = ZR_BLOCK
ZS_OFF = ZM_OFF + 4 * MLSTM_WIDTH
Z_COLS = ZS_OFF + GATE_COLS

IN_TILE = 512
MERGE_TILE = 256
DOWN_TILE = 512
CONV_DOWN_TILE = 256
RCHUNK = 64
CONV_CH_TILE = 256
VMEM_LIMIT = 56 * 1024 * 1024


def _params(sem):
    return pltpu.CompilerParams(dimension_semantics=sem, vmem_limit_bytes=VMEM_LIMIT)


def _resident(shape):
    nd = len(shape)
    return pl.BlockSpec(shape, lambda *_: (0,) * nd, pipeline_mode=pl.Buffered(1))


def _split2(a):
    hi = a.astype(BF16)
    lo = (a - hi.astype(F32)).astype(BF16)
    return hi, lo


def _split3(a):
    hi = a.astype(BF16)
    r1 = a - hi.astype(F32)
    mid = r1.astype(BF16)
    lo = (r1 - mid.astype(F32)).astype(BF16)
    return hi, mid, lo


def _dg(a, b, dims):
    return lax.dot_general(a, b, dims, preferred_element_type=F32)


def _mm(a, b, dims, passes):
    if passes == 1:
        return _dg(a.astype(BF16), b.astype(BF16), dims)
    ah, al = _split2(a)
    bh, bl = _split2(b)
    return _dg(ah, bh, dims) + (_dg(ah, bl, dims) + _dg(al, bh, dims))


def _mm_exact_lhs(a_bf16, b, dims):
    b1, b2, b3 = _split3(b)
    return _dg(a_bf16, b1, dims) + (_dg(a_bf16, b2, dims) + _dg(a_bf16, b3, dims))


def _mm_exact_rhs(a, b_bf16, dims, pieces=3):
    if pieces == 2:
        a1, a2 = _split2(a)
        return _dg(a1, b_bf16, dims) + _dg(a2, b_bf16, dims)
    a1, a2, a3 = _split3(a)
    return _dg(a1, b_bf16, dims) + (_dg(a2, b_bf16, dims) + _dg(a3, b_bf16, dims))


_NN = (((1,), (0,)), ((), ()))
_NT = (((1,), (1,)), ((), ()))
_TN = (((0,), (0,)), ((), ()))
_BNN = (((2,), (1,)), ((0,), (0,)))
_BNT = (((2,), (2,)), ((0,), (0,)))
_BTN = (((1,), (1,)), ((0,), (0,)))


def _sigmoid(x):
    return jax.nn.sigmoid(x)


def _silu(x):
    return x * jax.nn.sigmoid(x)


def _rms(x, g):
    return x * lax.rsqrt(jnp.mean(x * x, axis=-1, keepdims=True) + RMS_EPS) * g


def _ada_kernel(cond_ref, w_ref, b_ref, o_ref):
    s = _silu(cond_ref[...])
    o_ref[...] = _dg(s.astype(BF16), w_ref[...].astype(BF16), _NN) + b_ref[...]


def _ada(cond8, ada_w, ada_b):
    n = ada_w.shape[1]
    tn = 1536
    return pl.pallas_call(
        _ada_kernel,
        grid=(n // tn,),
        in_specs=[_resident((8, D_MODEL)),
                  pl.BlockSpec((D_MODEL, tn), lambda j: (0, j)),
                  pl.BlockSpec((1, tn), lambda j: (0, j))],
        out_specs=pl.BlockSpec((8, tn), lambda j: (0, j)),
        out_shape=jax.ShapeDtypeStruct((8, n), F32),
        compiler_params=_params(("arbitrary",)),
        name="ada_mod",
    )(cond8, ada_w, ada_b.reshape(1, n))


def _in_kernel(conv_width, x_ref, mod_ref, g_ref, wr_ref, wg_ref, wm_ref, ws_ref, cw_ref, z_ref):
    mod = mod_ref[0]
    sh = mod[:, 0:D_MODEL]
    sc = mod[:, D_MODEL:2 * D_MODEL]
    h = (_rms(x_ref[...], g_ref[...]) * (1.0 + sc) + sh).astype(BF16)
    z_ref[:, 0:ZG_OFF] = _dg(h, wr_ref[...], _NN)
    z_ref[:, ZG_OFF:ZM_OFF] = _dg(h, wg_ref[...], _NN)
    zm = _dg(h, wm_ref[...], _NN)
    if conv_width is None:
        z_ref[:, ZM_OFF:ZS_OFF] = zm
    else:
        qk_cols = 2 * MLSTM_WIDTH
        z_ref[:, ZM_OFF:ZM_OFF + qk_cols] = _silu(_dwconv(zm[:, 0:qk_cols], cw_ref, conv_width, False))
        z_ref[:, ZM_OFF + qk_cols:ZS_OFF] = zm[:, qk_cols:]
    z_ref[:, ZS_OFF:Z_COLS] = _sigmoid(_dg(h, ws_ref[...], _NN))


def _in_proj(x2, mod, mod_row, norm_g0, w_r, w_g, w_m, w_s, conv_w9, conv_width):
    n = x2.shape[0]
    tile = IN_TILE
    return pl.pallas_call(
        functools.partial(_in_kernel, conv_width),
        grid=(n // tile,),
        in_specs=[pl.BlockSpec((tile, D_MODEL), lambda i: (i, 0)),
                  pl.BlockSpec((1, 1, 6 * D_MODEL), lambda i: (mod_row(i * tile), 0, 0)),
                  _resident((1, D_MODEL)),
                  _resident(w_r.shape), _resident(w_g.shape), _resident(w_m.shape), _resident(w_s.shape),
                  _resident(conv_w9.shape)],
        out_specs=pl.BlockSpec((tile, Z_COLS), lambda i: (i, 0)),
        out_shape=jax.ShapeDtypeStruct((n, Z_COLS), F32),
        compiler_params=_params(("arbitrary",)),
        name="in_proj",
    )(x2, mod, norm_g0, w_r, w_g, w_m, w_s, conv_w9)


LOCAL_CHUNKS = 4
PAIR_LANES = 2 * RWKV_HEAD_DIM
RWKV_PAIRS = RWKV_HEADS // 2


def _bd(x):
    lane = lax.broadcasted_iota(jnp.int32, x.shape, 1)
    left = lane < RWKV_HEAD_DIM
    return jnp.concatenate([jnp.where(left, x, 0.0), jnp.where(left, 0.0, x)], axis=0)


def _rwkv_local_kernel(chunks_per_seq, passes,
                       z_ref, zp_ref, zn_ref, mu_ref, w0_ref, wup_ref, a0_ref, aup_ref, gup_ref,
                       kks_ref, ka_ref, rk_ref, pones_ref,
                       rp_ref, y0_ref, gm_ref, hm_ref, gate_ref, bonus_ref):
    C = RCHUNK
    W = RWKV_WIDTH
    NS = LOCAL_CHUNKS
    R = NS * C
    first = (pl.program_id(0) * NS) % chunks_per_seq
    has_prev = first != 0
    has_next = first + NS != chunks_per_seq

    z = z_ref[:, 0:RWKV_COLS]
    zp = jnp.where(has_prev, zp_ref[7:8, 0:RWKV_COLS], 0.0)
    zn = jnp.where(has_next, zn_ref[0:1, 0:RWKV_COLS], 0.0)
    trow = lax.broadcasted_iota(jnp.int32, (R, 1), 0)
    prev = jnp.where(trow == 0, zp, pltpu.roll(z, 1, 0))
    nxt = jnp.where(trow == R - 1, zn, pltpu.roll(z, R - 1, 0))
    zs = z + mu_ref[...] * (0.5 * (prev + nxt) - z)

    r = zs[:, 0:W]
    k = zs[:, W:2 * W]
    v = zs[:, 2 * W:3 * W]
    gd = zs[:, 3 * W + 2 * DECAY_LORA + 2 * ICLR_LORA:RWKV_COLS]
    gate_ref[...] = _dg(_sigmoid(gd).astype(BF16), gup_ref[...].astype(BF16), _NN)

    pones = pones_ref[...]
    kks = k * kks_ref[...]
    norm = jnp.sqrt(_mm_exact_rhs(kks * kks, pones, _NN, pieces=2))
    kk = kks / jnp.maximum(norm, 1e-12)

    P = PAIR_LANES
    row = lax.broadcasted_iota(jnp.int32, (R, R), 0)
    col = lax.broadcasted_iota(jnp.int32, (R, R), 1)
    same_chunk = jnp.bitwise_and(row, -C) == jnp.bitwise_and(col, -C)
    prow = lax.broadcasted_iota(jnp.int32, (C, P), 0)
    pcol = jnp.bitwise_and(lax.broadcasted_iota(jnp.int32, (C, P), 1), RWKV_HEAD_DIM - 1)
    eye_p = jnp.where(prow == pcol, 1.0, 0.0)
    left_head = lax.broadcasted_iota(jnp.int32, (C, P), 1) < RWKV_HEAD_DIM

    def diag_blocks(m):
        return jnp.where(left_head, m[0:RWKV_HEAD_DIM], m[RWKV_HEAD_DIM:P])

    abar, rbar, kt, bt, kw, bw, wc, strict, incl = [], [], [], [], [], [], [], [], []
    kd_sum = None
    for d in range(N_DIR):
        o = 3 * W + d * DECAY_LORA
        wd = zs[:, o:o + DECAY_LORA]
        o = 3 * W + 2 * DECAY_LORA + d * ICLR_LORA
        ad = zs[:, o:o + ICLR_LORA]
        logw = -DECAY_SCALE * _sigmoid(w0_ref[d] + _dg(jnp.tanh(wd).astype(BF16), wup_ref[d].astype(BF16), _NN))
        a = _sigmoid(a0_ref[d] + _dg(ad.astype(BF16), aup_ref[d].astype(BF16), _NN))
        kd = k * (1.0 + (a - 1.0) * ka_ref[...])
        b = kk * a
        kd_sum = kd if kd_sum is None else kd_sum + kd

        earlier_or_same = same_chunk & ((row >= col) if d == 0 else (row <= col))
        cum_i = _mm_exact_lhs(jnp.where(earlier_or_same, 1.0, 0.0).astype(BF16), logw, _NN)
        cum_e = cum_i - logw
        ab_d, rb_d, kt_d, bt_d, kw_d, bw_d, wc_d = [], [], [], [], [], [], []
        for s in range(NS):
            rs = slice(s * C, (s + 1) * C)
            ci_s = cum_i[rs]
            ctot = jnp.sum(logw[rs], axis=0, keepdims=True)
            e_ni = jnp.exp(-ci_s)
            e_ti = jnp.exp(ctot - ci_s)
            ab_d.append(kk[rs] * jnp.exp(cum_e[rs]))
            rb_d.append(r[rs] * jnp.exp(ci_s))
            kt_d.append(kd[rs] * e_ni)
            bt_d.append(b[rs] * e_ni)
            kw_d.append(kd[rs] * e_ti)
            bw_d.append(b[rs] * e_ti)
            wc_d.append(jnp.exp(ctot))
        abar.append(ab_d)
        rbar.append(rb_d)
        kt.append(kt_d)
        bt.append(bt_d)
        kw.append(kw_d)
        bw.append(bw_d)
        wc.append(wc_d)
        strict.append((prow > pcol) if d == 0 else (prow < pcol))
        incl.append((prow >= pcol) if d == 0 else (prow <= pcol))
    bonus_ref[...] = _mm_exact_rhs(r * kd_sum * rk_ref[...], pones, _NN, pieces=2) * v

    mm = functools.partial(_mm, passes=passes)
    chains = [(s, d, p) for s in range(NS) for d in range(N_DIR) for p in range(RWKV_PAIRS)]
    nch = range(len(chains))

    def sel(arr, i):
        s, d, p = chains[i]
        return arr[d][s][:, p * P:(p + 1) * P]

    cat0 = lambda a_, b_: jnp.concatenate([a_, b_], axis=0)
    cat1 = lambda a_, b_: jnp.concatenate([a_, b_], axis=1)
    vsl = [v[s * C:(s + 1) * C, p * P:(p + 1) * P] for s, _, p in chains]
    lhs = [cat0(sel(abar, i), sel(rbar, i)) for i in nch]
    by_b = [mm(lhs[i], _bd(sel(bt, i)), _NT) for i in nch]
    by_k = [mm(lhs[i], _bd(sel(kt, i)), _NT) for i in nch]
    a_kk = [jnp.where(strict[chains[i][1]], by_b[i][0:C], 0.0) for i in nch]
    a_rb = [jnp.where(incl[chains[i][1]], by_b[i][C:2 * C], 0.0) for i in nch]
    a_kv = [jnp.where(strict[chains[i][1]], by_k[i][0:C], 0.0) for i in nch]
    a_rk = [jnp.where(incl[chains[i][1]], by_k[i][C:2 * C], 0.0) for i in nch]
    on_v = [mm(cat0(a_kv[i], a_rk[i]), _bd(vsl[i]), _NN) for i in nch]

    x = [-m for m in a_kk]
    tinv = [eye_p + m for m in x]
    x = [mm(m, _bd(m), _NN) for m in x]
    for _ in range(4):
        both = [mm(cat0(tinv[i], x[i]), _bd(x[i]), _NN) for i in nch]
        tinv = [tinv[i] + both[i][0:C] for i in nch]
        x = [m[C:2 * C] for m in both]
    tinv = [tinv[i] + mm(tinv[i], _bd(x[i]), _NN) for i in nch]

    solved = [mm(tinv[i], cat1(_bd(sel(abar, i)), _bd(on_v[i][0:C])), _NN) for i in nch]
    ap = [m[:, 0:P] for m in solved]
    u0 = [m[:, P:2 * P] for m in solved]
    corr = [mm(a_rb[i], cat1(_bd(ap[i]), _bd(u0[i])), _NN) for i in nch]
    on_b = [mm(cat1(ap[i], u0[i]), sel(bw, i), _TN) for i in nch]
    vk = [mm(vsl[i], sel(kw, i), _TN) for i in nch]
    for i in nch:
        s, d, p = chains[i]
        rows = slice(s * C, (s + 1) * C)
        lanes = slice(p * P, (p + 1) * P)
        rp_ref[d, rows, lanes] = (sel(rbar, i) - corr[i][:, 0:P]).astype(BF16)
        y0_ref[d, rows, lanes] = on_v[i][C:2 * C] - corr[i][:, P:2 * P]
        gm_ref[d, s, p] = (eye_p * sel(wc, i) - diag_blocks(on_b[i][0:P])).astype(BF16)
        hm_ref[d, s, p] = diag_blocks(vk[i] - on_b[i][P:2 * P])


def _rwkv_local(z, seq_len, p, passes):
    n = z.shape[0]
    nchunk = n // RCHUNK
    cps = seq_len // RCHUNK
    assert cps % LOCAL_CHUNKS == 0
    W = RWKV_WIDTH
    rows = LOCAL_CHUNKS * RCHUNK
    hb = rows // 8
    last8 = n // 8 - 1
    mat = lambda dt: jax.ShapeDtypeStruct((N_DIR, nchunk, RWKV_PAIRS, RWKV_HEAD_DIM, PAIR_LANES), dt)
    mat_spec = pl.BlockSpec((N_DIR, LOCAL_CHUNKS, RWKV_PAIRS, RWKV_HEAD_DIM, PAIR_LANES),
                            lambda c: (0, c, 0, 0, 0))
    tok = lambda dt: jax.ShapeDtypeStruct((N_DIR, n, W), dt)
    tok_spec = pl.BlockSpec((N_DIR, rows, W), lambda c: (0, c, 0))
    row_spec = pl.BlockSpec((rows, W), lambda c: (c, 0))
    return pl.pallas_call(
        functools.partial(_rwkv_local_kernel, cps, passes),
        grid=(nchunk // LOCAL_CHUNKS,),
        in_specs=[pl.BlockSpec((rows, ZR_BLOCK), lambda c: (c, 0)),
                  pl.BlockSpec((8, ZR_BLOCK), lambda c: (jnp.maximum(c * hb - 1, 0), 0)),
                  pl.BlockSpec((8, ZR_BLOCK), lambda c: (jnp.minimum((c + 1) * hb, last8), 0)),
                  _resident((1, RWKV_COLS)),
                  _resident((N_DIR, 1, W)), _resident((N_DIR, DECAY_LORA, W)),
                  _resident((N_DIR, 1, W)), _resident((N_DIR, ICLR_LORA, W)),
                  _resident((GATE_LORA, W)),
                  _resident((1, W)), _resident((1, W)), _resident((1, W)),
                  _resident((W, W))],
        out_specs=[tok_spec, tok_spec, mat_spec, mat_spec, row_spec, row_spec],
        out_shape=[tok(BF16), tok(F32), mat(BF16), mat(F32),
                   jax.ShapeDtypeStruct((n, W), F32), jax.ShapeDtypeStruct((n, W), F32)],
        compiler_params=_params(("arbitrary",)),
        name="rwkv_local",
    )(z, z, z, p["mu"], p["w0"], p["w_up"], p["a0"], p["a_up"], p["g_up"],
      p["kk_scale"], p["k_a"], p["r_k"], p["pones"])


SCAN_CHUNKS = 4


def _rwkv_scan_kernel(s0_ref, rpf_ref, rpb_ref, y0f_ref, y0b_ref, gmf_ref, gmb_ref, hmf_ref, hmb_ref,
                      ysf_ref, ysb_ref, sout_ref, s_scr):
    @pl.when(pl.program_id(1) == 0)
    def _():
        s_scr[...] = s0_ref[0]

    K = SCAN_CHUNKS
    C = RCHUNK
    rp_ref, y0_ref, gm_ref, hm_ref, ys_ref = ((rpf_ref, rpb_ref), (y0f_ref, y0b_ref), (gmf_ref, gmb_ref),
                                              (hmf_ref, hmb_ref), (ysf_ref, ysb_ref))
    chains = [(d, p) for d in range(N_DIR) for p in range(RWKV_PAIRS)]
    lanes = [slice(p * PAIR_LANES, (p + 1) * PAIR_LANES) for _, p in chains]
    nch = range(len(chains))
    s = [s_scr[d, p] for d, p in chains]
    for j in range(K):
        at = (j, K - 1 - j)
        rows = [slice(at[d] * C, (at[d] + 1) * C) for d, _ in chains]
        sb = [m.astype(BF16) for m in s]
        y = [_dg(rp_ref[chains[i][0]][0, rows[i], lanes[i]], sb[i], _NT) for i in nch]
        sg = [_dg(sb[i], _bd(gm_ref[chains[i][0]][0, at[chains[i][0]], chains[i][1]]), _NN) for i in nch]
        for i in nch:
            d, p = chains[i]
            ys_ref[d][rows[i], lanes[i]] = y[i] + y0_ref[d][0, rows[i], lanes[i]]
        s = [sg[i] + _bd(hm_ref[chains[i][0]][0, at[chains[i][0]], chains[i][1]]) for i in nch]
    for i in nch:
        d, p = chains[i]
        s_scr[d, p] = s[i]
        sout_ref[0, d, p] = s[i]


def _rwkv_scan(s0, rp, y0, gm, hm, batch, seq_len):
    K = SCAN_CHUNKS
    spb = seq_len // (RCHUNK * K)
    n = batch * seq_len

    def fwd(b, s):
        return b * spb + s

    def bwd(b, s):
        return b * spb + spb - 1 - s

    def mat_spec(d, at):
        return pl.BlockSpec((1, K, RWKV_PAIRS, RWKV_HEAD_DIM, PAIR_LANES), lambda b, s: (d, at(b, s), 0, 0, 0))

    def tok_spec(d, at):
        return pl.BlockSpec((1, K * RCHUNK, RWKV_WIDTH), lambda b, s: (d, at(b, s), 0))

    st_spec = pl.BlockSpec((1, N_DIR, RWKV_PAIRS, PAIR_LANES, PAIR_LANES), lambda b, s: (b, 0, 0, 0, 0))
    ys = jax.ShapeDtypeStruct((n, RWKV_WIDTH), F32)
    return pl.pallas_call(
        _rwkv_scan_kernel,
        grid=(batch, spb),
        in_specs=[st_spec, tok_spec(0, fwd), tok_spec(1, bwd), tok_spec(0, fwd), tok_spec(1, bwd),
                  mat_spec(0, fwd), mat_spec(1, bwd), mat_spec(0, fwd), mat_spec(1, bwd)],
        out_specs=[pl.BlockSpec((K * RCHUNK, RWKV_WIDTH), lambda b, s: (fwd(b, s), 0)),
                   pl.BlockSpec((K * RCHUNK, RWKV_WIDTH), lambda b, s: (bwd(b, s), 0)),
                   st_spec],
        out_shape=[ys, ys,
                   jax.ShapeDtypeStruct((batch, N_DIR, RWKV_PAIRS, PAIR_LANES, PAIR_LANES), F32)],
        scratch_shapes=[pltpu.VMEM((N_DIR, RWKV_PAIRS, PAIR_LANES, PAIR_LANES), F32)],
        compiler_params=_params(("arbitrary", "arbitrary")),
        name="rwkv_scan",
    )(s0, rp, rp, y0, y0, gm, gm, hm, hm)


CONV_BLOCK_ROWS = 2048


def _dwconv(x, w_ref, width, vertical):
    T = x.shape[0]
    t = lax.broadcasted_iota(jnp.int32, (T, 1), 0)
    assert width & (width - 1) == 0
    colp = jnp.bitwise_and(t, width - 1)
    xl = jnp.where(colp == 0, 0.0, pltpu.roll(x, 1, 0))
    xr = jnp.where(colp == width - 1, 0.0, pltpu.roll(x, T - 1, 0))

    def tap_row(i):
        return w_ref[3 * i:3 * i + 1, :] * xl + w_ref[3 * i + 1:3 * i + 2, :] * x + w_ref[3 * i + 2:3 * i + 3, :] * xr

    out = tap_row(1)
    if vertical:
        out = out + jnp.where(t < width, 0.0, pltpu.roll(tap_row(0), width, 0))
        out = out + jnp.where(t >= T - width, 0.0, pltpu.roll(tap_row(2), T - width, 0))
    return out


def _conv_geometry(n, seq_len, rows):
    if rows > 1:
        return seq_len, seq_len // rows, True
    block = CONV_BLOCK_ROWS if (n % CONV_BLOCK_ROWS == 0 and CONV_BLOCK_ROWS % seq_len == 0) else seq_len
    return block, seq_len, False


def _qk_conv_kernel(width, vertical, x_ref, w_ref, o_ref):
    o_ref[...] = _silu(_dwconv(x_ref[...], w_ref, width, vertical))


def _qk_conv(z, batch, seq_len, rows, conv_w9):
    n = batch * seq_len
    ch = 2 * MLSTM_WIDTH
    tc = CONV_CH_TILE
    off = ZM_OFF // tc
    block, width, vertical = _conv_geometry(n, seq_len, rows)
    return pl.pallas_call(
        functools.partial(_qk_conv_kernel, width, vertical),
        grid=(n // block, ch // tc),
        in_specs=[pl.BlockSpec((block, tc), lambda b, j: (b, off + j)),
                  pl.BlockSpec((9, tc), lambda b, j: (0, j))],
        out_specs=pl.BlockSpec((block, tc), lambda b, j: (b, j)),
        out_shape=jax.ShapeDtypeStruct((n, ch), F32),
        compiler_params=_params(("arbitrary", "arbitrary")),
        name="mlstm_qk_conv",
    )(z, conv_w9)


MLSTM_STEP_CHUNKS = 4

def _mlstm_scan_kernel(qkf_ref, qkb_ref, ktf_ref, ktb_ref, vf_ref, vb_ref, gcf_ref, gcb_ref, grf_ref, grb_ref,
                       gbc_ref, gbr_ref, c0_ref, n0_ref, m0_ref,
                       hf_ref, hb_ref, cout_ref, nout_ref, mout_ref,
                       c_scr, n_scr, m_scr):
    step = pl.program_id(1)
    L = MLSTM_CHUNK
    dh = MLSTM_HEAD_DIM
    H = MLSTM_HEADS

    @pl.when(step == 0)
    def _():
        c_scr[...] = c0_ref[0]
        n_scr[...] = n0_ref[0]
        m_scr[...] = m0_ref[0]

    K = MLSTM_STEP_CHUNKS
    R = K * L
    row = lax.broadcasted_iota(jnp.int32, (L, L), 0)
    col = lax.broadcasted_iota(jnp.int32, (L, L), 1)
    lower = (row >= col)
    upper = (row <= col)
    lower_b = jnp.where(lower, 1.0, 0.0).astype(BF16)
    upper_b = jnp.where(upper, 1.0, 0.0).astype(BF16)
    rrow = lax.broadcasted_iota(jnp.int32, (R, R), 0)
    rcol = lax.broadcasted_iota(jnp.int32, (R, R), 1)
    same_chunk = jnp.bitwise_and(rrow, -L) == jnp.bitwise_and(rcol, -L)
    neg_inf = jnp.full((), -jnp.inf, F32)

    gcol, grow, bcol, brow, btot = [], [], [], [], []
    ones_b = jnp.ones((L, LANE), BF16)
    for d in range(N_DIR):
        gc_ref, gr_ref = (gcf_ref, grf_ref) if d == 0 else (gcb_ref, grb_ref)
        gcol.append(gc_ref[...] + gbc_ref[...])
        grow.append((gr_ref[...] + gbr_ref[...][None]).reshape(K * MLSTM_GATES, L))
        before = same_chunk & ((rrow >= rcol) if d == 0 else (rrow <= rcol))
        bcol.append(_mm_exact_lhs(jnp.where(before, 1.0, 0.0).astype(BF16), jax.nn.log_sigmoid(gcol[d]), _NN))
        frow = jax.nn.log_sigmoid(grow[d])
        brow.append(_mm_exact_rhs(frow, upper_b if d == 0 else lower_b, _NN))
        btot.append(_mm_exact_rhs(frow, ones_b, _NN))

    nst = N_DIR * H
    states = [(d, h) for d in range(N_DIR) for h in range(H)]
    sts = range(nst)

    def local_part(j):
        q, k, kt, vb, qb, c_row, b_col, b_last, last = [], [], [], [], [], [], [], [], []
        for d, h in states:
            at = j if d == 0 else K - 1 - j
            rows = slice(at * L, (at + 1) * L)
            st = d * H + h
            gi, gf = st, 2 * H + st
            qk_ref, kt_ref, v_ref = (qkf_ref, ktf_ref, vf_ref) if d == 0 else (qkb_ref, ktb_ref, vb_ref)
            q.append(qk_ref[rows, h * dh:(h + 1) * dh] * (dh ** -0.5))
            k.append(qk_ref[rows, MLSTM_WIDTH + h * dh:MLSTM_WIDTH + (h + 1) * dh])
            kt.append(kt_ref[at, h * dh:(h + 1) * dh, :])
            qb.append(q[-1].astype(BF16))
            vb.append(v_ref[rows, h * dh:(h + 1) * dh].astype(BF16))
            b_col.append(jnp.broadcast_to(bcol[d][rows, gf:gf + 1], (L, LANE)))
            c_row.append(grow[d][at * MLSTM_GATES + gi:at * MLSTM_GATES + gi + 1, :]
                         - brow[d][at * MLSTM_GATES + gf:at * MLSTM_GATES + gf + 1, :])
            b_last.append(btot[d][at * MLSTM_GATES + gf:at * MLSTM_GATES + gf + 1, :])
            last.append(L - 1 if d == 0 else 0)

        qk_t = [_dg(qb[i], k[i].astype(BF16), _NT) for i in sts]
        rel = [jnp.where(lower if states[i][0] == 0 else upper, c_row[i], neg_inf) for i in sts]
        mx = [jnp.broadcast_to(jnp.max(rel[i], axis=-1, keepdims=True), (L, LANE)) for i in sts]
        s_loc = [qk_t[i] * jnp.exp(rel[i] - mx[i][:, 0:L]) for i in sts]
        cmax = [mx[i][last[i]:last[i] + 1, :] for i in sts]
        wj = [jnp.exp(c_row[i] - cmax[i][:, 0:L]) for i in sts]
        return dict(
            q=q, qb=qb, b_col=b_col, b_last=b_last,
            m_loc=[b_col[i] + mx[i] for i in sts],
            s_v=[_dg(s_loc[i].astype(BF16), vb[i], _NN) for i in sts],
            s_sum=[jnp.broadcast_to(jnp.sum(s_loc[i], axis=-1, keepdims=True), (L, LANE)) for i in sts],
            m_w=[b_last[i] + cmax[i] for i in sts],
            kv=[_dg((kt[i] * wj[i]).astype(BF16), vb[i], _NN) for i in sts],
            w_k=[_mm(jnp.broadcast_to(wj[i], (8, L)), k[i], _NN, 3)[0:1] for i in sts])

    c_st = [c_scr[st] for st in sts]
    n_st = [n_scr[st:st + 1, :] for st in sts]
    m_st = [m_scr[st:st + 1, :] for st in sts]

    def state_part(j, u):
        q_c = [_dg(u["qb"][st], c_st[st].astype(BF16), _NN) for st in sts]
        for st in sts:
            d, h = states[st]
            at = j if d == 0 else K - 1 - j
            h_ref = hf_ref if d == 0 else hb_ref
            log_inter = u["b_col"][st] + m_st[st]
            m_s = jnp.maximum(log_inter, u["m_loc"][st])
            inter = jnp.exp(log_inter - m_s)
            local = jnp.exp(u["m_loc"][st] - m_s)
            q_n = jnp.broadcast_to(jnp.sum(u["q"][st] * n_st[st], axis=-1, keepdims=True), (L, LANE))
            den = inter * q_n + local * u["s_sum"][st]
            scale = 1.0 / jnp.maximum(jnp.abs(den), jnp.exp(-m_s))
            h_ref[at * L:(at + 1) * L, h * dh:(h + 1) * dh] = ((inter * scale) * q_c[st]
                                                               + (local * scale) * u["s_v"][st])
            m_new = jnp.maximum(u["b_last"][st] + m_st[st], u["m_w"][st])
            carry = jnp.exp(u["b_last"][st] + m_st[st] - m_new)
            fresh = jnp.exp(u["m_w"][st] - m_new)
            c_st[st] = carry * c_st[st] + fresh * u["kv"][st]
            n_st[st] = carry * n_st[st] + fresh * u["w_k"][st]
            m_st[st] = m_new

    ahead = local_part(0)
    for j in range(K):
        cur = ahead
        if j + 1 < K:
            ahead = local_part(j + 1)
        state_part(j, cur)

    for st in range(nst):
        c_scr[st] = c_st[st]
        n_scr[st:st + 1, :] = n_st[st]
        m_scr[st:st + 1, :] = m_st[st]
    cout_ref[0] = c_scr[...]
    nout_ref[0] = n_scr[...]
    mout_ref[0] = m_scr[...]


def _mlstm_scan(z, qk, qk_blk, kt, gt, gate_bc, gate_br, c0, n0, m0, batch, seq_len):
    K = MLSTM_STEP_CHUNKS
    L = K * MLSTM_CHUNK
    assert seq_len % L == 0
    cps = seq_len // L
    n = batch * seq_len
    W = MLSTM_WIDTH
    nst = N_DIR * MLSTM_HEADS
    dh = MLSTM_HEAD_DIM

    def fw(b, c):
        return b * cps + c

    def bw(b, c):
        return b * cps + cps - 1 - c

    vblk = (ZM_OFF + 2 * W) // W
    gblk = ZG_OFF // LANE
    return pl.pallas_call(
        _mlstm_scan_kernel,
        grid=(batch, cps),
        in_specs=[pl.BlockSpec((L, 2 * W), lambda b, c: (fw(b, c), qk_blk)),
                  pl.BlockSpec((L, 2 * W), lambda b, c: (bw(b, c), qk_blk)),
                  pl.BlockSpec((K, W, MLSTM_CHUNK), lambda b, c: (fw(b, c), 0, 0)),
                  pl.BlockSpec((K, W, MLSTM_CHUNK), lambda b, c: (bw(b, c), 0, 0)),
                  pl.BlockSpec((L, W), lambda b, c: (fw(b, c), vblk)),
                  pl.BlockSpec((L, W), lambda b, c: (bw(b, c), vblk)),
                  pl.BlockSpec((L, LANE), lambda b, c: (fw(b, c), gblk)),
                  pl.BlockSpec((L, LANE), lambda b, c: (bw(b, c), gblk)),
                  pl.BlockSpec((K, MLSTM_GATES, MLSTM_CHUNK), lambda b, c: (fw(b, c), 0, 0)),
                  pl.BlockSpec((K, MLSTM_GATES, MLSTM_CHUNK), lambda b, c: (bw(b, c), 0, 0)),
                  _resident((1, LANE)),
                  _resident((MLSTM_GATES, 1)),
                  pl.BlockSpec((1, nst, dh, dh), lambda b, c: (b, 0, 0, 0)),
                  pl.BlockSpec((1, nst, dh), lambda b, c: (b, 0, 0)),
                  pl.BlockSpec((1, nst, LANE), lambda b, c: (b, 0, 0))],
        out_specs=[pl.BlockSpec((L, W), lambda b, c: (fw(b, c), 0)),
                   pl.BlockSpec((L, W), lambda b, c: (bw(b, c), 0)),
                   pl.BlockSpec((1, nst, dh, dh), lambda b, c: (b, 0, 0, 0)),
                   pl.BlockSpec((1, nst, dh), lambda b, c: (b, 0, 0)),
                   pl.BlockSpec((1, nst, LANE), lambda b, c: (b, 0, 0))],
        out_shape=[jax.ShapeDtypeStruct((n, W), F32), jax.ShapeDtypeStruct((n, W), F32),
                   jax.ShapeDtypeStruct((batch, nst, dh, dh), F32),
                   jax.ShapeDtypeStruct((batch, nst, dh), F32),
                   jax.ShapeDtypeStruct((batch, nst, LANE), F32)],
        scratch_shapes=[pltpu.VMEM((nst, dh, dh), F32), pltpu.VMEM((nst, dh), F32),
                        pltpu.VMEM((nst, LANE), F32)],
        compiler_params=_params(("arbitrary", "arbitrary")),
        name="mlstm_scan",
    )(qk, qk, kt, kt, z, z, z, z, gt, gt, gate_bc, gate_br, c0, n0, m0)


def _grid_conv(above, cur, below, cw_ref, width):
    T = cur.shape[0]
    E = T + 2 * width
    ext = jnp.concatenate([above, cur, below], axis=0)
    colp = jnp.bitwise_and(lax.broadcasted_iota(jnp.int32, (E, 1), 0), width - 1)
    left = jnp.where(colp == 0, 0.0, pltpu.roll(ext, 1, 0))
    right = jnp.where(colp == width - 1, 0.0, pltpu.roll(ext, E - 1, 0))

    def tap_row(i):
        rows = slice(i * width, i * width + T)
        return (cw_ref[3 * i:3 * i + 1, :] * left[rows] + cw_ref[3 * i + 1:3 * i + 2, :] * ext[rows]
                + cw_ref[3 * i + 2:3 * i + 3, :] * right[rows])

    return tap_row(0) + tap_row(1) + tap_row(2)


def _merge_kernel(conv, x_ref, mod_ref, ysf_ref, ysb_ref, bonus_ref, gate_ref, hf_ref, hb_ref, zo_ref,
                  zs_ref, lnxg_ref, lnxb_ref, gng_ref, pmean_ref, wbr_ref, wbm_ref, wout_ref, ng_ref, wup_ref,
                  cw_ref, cb_ref, x1_ref, act_ref, *scratch):
    if conv[0] == "grid":
        @pl.when(pl.program_id(0) == 0)
        def _():
            for ref in scratch:
                ref[...] = jnp.zeros(ref.shape, F32)

    mod = mod_ref[0]
    g1 = mod[:, 2 * D_MODEL:3 * D_MODEL]
    sh2 = mod[:, 3 * D_MODEL:4 * D_MODEL]
    sc2 = mod[:, 4 * D_MODEL:5 * D_MODEL]

    ys = ysf_ref[...] + ysb_ref[...]
    pmean = pmean_ref[...]
    mean = _mm_exact_rhs(ys, pmean, _NN)
    cen = ys - mean
    var = _mm_exact_rhs(cen * cen, pmean, _NN)
    y_r = (cen * lax.rsqrt(var + RWKV_GN_EPS) * lnxg_ref[...] + lnxb_ref[...] + bonus_ref[...]) * gate_ref[...]

    hs = hf_ref[...] + hb_ref[...]
    parts = []
    for h in range(MLSTM_HEADS):
        hh = hs[:, h * MLSTM_HEAD_DIM:(h + 1) * MLSTM_HEAD_DIM]
        mu = jnp.mean(hh, axis=-1, keepdims=True)
        ce = hh - mu
        va = jnp.mean(ce * ce, axis=-1, keepdims=True)
        parts.append(ce * lax.rsqrt(va + MLSTM_GN_EPS))
    y_m = jnp.concatenate(parts, axis=1) * gng_ref[...] * _sigmoid(zo_ref[...])

    gates = zs_ref[...]
    merged = (gates[:, 0:D_MODEL] * _dg(y_r.astype(BF16), wbr_ref[...], _NN)
              + gates[:, D_MODEL:2 * D_MODEL] * _dg(y_m.astype(BF16), wbm_ref[...], _NN))
    t = _dg(merged.astype(BF16), wout_ref[...], _NN)
    x1 = x_ref[...] + g1 * _rms(t, ng_ref[1:2, :])
    x1_ref[...] = x1
    h2 = _rms(x1, ng_ref[2:3, :]) * (1.0 + sc2) + sh2
    u = _dg(h2.astype(BF16), wup_ref[...], _NN)
    if conv[0] == "seq":
        pre = _dwconv(u[:, 0:D_FF], cw_ref, conv[1], False) + cb_ref[...]
        act_ref[...] = (_silu(pre) * u[:, D_FF:2 * D_FF]).astype(BF16)
    else:
        _, width, tiles_per_image = conv
        act_scr, val_scr, tail_scr = scratch
        T = act_scr.shape[0]
        step = pl.program_id(0)
        pos = (step + tiles_per_image - 1) % tiles_per_image
        for c0 in range(0, D_FF, CONV_CH_TILE):
            ch = slice(c0, c0 + CONV_CH_TILE)
            above = jnp.where(pos != 0, tail_scr[:, ch], 0.0)
            below = jnp.where(pos != tiles_per_image - 1, u[0:width, ch], 0.0)
            pre = _grid_conv(above, act_scr[:, ch], below, cw_ref.at[:, ch], width) + cb_ref[:, ch]
            act_ref[:, ch] = (_silu(pre) * val_scr[:, ch]).astype(BF16)
        tail_scr[...] = act_scr[T - width:T, :]
        act_scr[...] = u[:, 0:D_FF]
        val_scr[...] = u[:, D_FF:2 * D_FF]


def _merge(x2, mod, mod_row, z, ysf, ysb, bonus, gate, hf, hb, p, conv):
    n = x2.shape[0]
    W = RWKV_WIDTH
    rows = MERGE_TILE
    ntiles = n // rows
    delayed = conv[0] == "grid"
    cur = (lambda i: jnp.minimum(i, ntiles - 1)) if delayed else (lambda i: i)
    tile = lambda w: pl.BlockSpec((rows, w), lambda i: (cur(i), 0))
    act_spec = pl.BlockSpec((rows, D_FF), (lambda i: (jnp.maximum(i - 1, 0), 0)) if delayed else (lambda i: (i, 0)))
    scratch = ([pltpu.VMEM((rows, D_FF), F32), pltpu.VMEM((rows, D_FF), F32), pltpu.VMEM((conv[1], D_FF), F32)]
               if delayed else [])
    return pl.pallas_call(
        functools.partial(_merge_kernel, conv),
        grid=(ntiles + 1 if delayed else ntiles,),
        in_specs=[tile(D_MODEL),
                  pl.BlockSpec((1, 1, 6 * D_MODEL), lambda i: (mod_row(cur(i) * rows), 0, 0)),
                  tile(W), tile(W), tile(W), tile(W), tile(MLSTM_WIDTH), tile(MLSTM_WIDTH),
                  pl.BlockSpec((rows, MLSTM_WIDTH),
                               lambda i: (cur(i), (ZM_OFF + 3 * MLSTM_WIDTH) // MLSTM_WIDTH)),
                  pl.BlockSpec((rows, GATE_COLS), lambda i: (cur(i), ZS_OFF // GATE_COLS)),
                  _resident((1, W)), _resident((1, W)), _resident((1, MLSTM_WIDTH)),
                  _resident((W, W)),
                  _resident((W, D_MODEL)), _resident((MLSTM_WIDTH, D_MODEL)),
                  _resident((D_MODEL, D_MODEL)), _resident((4, D_MODEL)),
                  _resident((D_MODEL, 2 * D_FF)), _resident((9, D_FF)), _resident((1, D_FF))],
        out_specs=[tile(D_MODEL), act_spec],
        out_shape=[jax.ShapeDtypeStruct((n, D_MODEL), F32), jax.ShapeDtypeStruct((n, D_FF), BF16)],
        scratch_shapes=scratch,
        compiler_params=_params(("arbitrary",)),
        name="merge_ffn_up",
    )(x2, mod, ysf, ysb, bonus, gate, hf, hb, z, z, p["lnx_g"], p["lnx_b"], p["gn_g"], p["pmean"],
      p["w_br"], p["w_bm"], p["w_out"], p["norm_g"], p["ffn_up"], p["ffn_conv"], p["ffn_conv_b"])


def _down_kernel(x1_ref, mod_ref, a_ref, w_ref, ng_ref, o_ref):
    g2 = mod_ref[0][:, 5 * D_MODEL:6 * D_MODEL]
    f = _dg(a_ref[...], w_ref[...], _NN)
    o_ref[...] = x1_ref[...] + g2 * _rms(f, ng_ref[3:4, :])


def _down(x1, mod, mod_row, act, p):
    n = x1.shape[0]
    tile = DOWN_TILE
    return pl.pallas_call(
        _down_kernel,
        grid=(n // tile,),
        in_specs=[pl.BlockSpec((tile, D_MODEL), lambda i: (i, 0)),
                  pl.BlockSpec((1, 1, 6 * D_MODEL), lambda i: (mod_row(i * tile), 0, 0)),
                  pl.BlockSpec((tile, D_FF), lambda i: (i, 0)),
                  _resident((D_FF, D_MODEL)), _resident((4, D_MODEL))],
        out_specs=pl.BlockSpec((tile, D_MODEL), lambda i: (i, 0)),
        out_shape=jax.ShapeDtypeStruct((n, D_MODEL), F32),
        compiler_params=_params(("arbitrary",)),
        name="ffn_down",
    )(x1, mod, act, p["ffn_down"], p["norm_g"])


def _conv_down_kernel(tiles_per_image, width, x1_ref, mod_ref, u_ref, up_ref, un_ref, cw_ref, cb_ref,
                      w_ref, ng_ref, o_ref):
    pos = pl.program_id(0) % tiles_per_image
    T = u_ref.shape[0]
    ua = u_ref[:, 0:D_FF]
    above_row = jnp.where(pos != 0, up_ref[...], 0.0)
    below_row = jnp.where(pos != tiles_per_image - 1, un_ref[...], 0.0)
    ext = jnp.concatenate([above_row, ua, below_row], axis=0)
    E = T + 2 * width
    colp = jnp.bitwise_and(lax.broadcasted_iota(jnp.int32, (E, 1), 0), width - 1)
    left = jnp.where(colp == 0, 0.0, pltpu.roll(ext, 1, 0))
    right = jnp.where(colp == width - 1, 0.0, pltpu.roll(ext, E - 1, 0))

    def tap_row(i):
        rows = slice(i * width, i * width + T)
        return (cw_ref[3 * i:3 * i + 1, :] * left[rows] + cw_ref[3 * i + 1:3 * i + 2, :] * ext[rows]
                + cw_ref[3 * i + 2:3 * i + 3, :] * right[rows])

    conv = tap_row(0) + tap_row(1) + tap_row(2) + cb_ref[...]
    act = (_silu(conv) * u_ref[:, D_FF:2 * D_FF]).astype(BF16)
    g2 = mod_ref[0][:, 5 * D_MODEL:6 * D_MODEL]
    f = _dg(act, w_ref[...], _NN)
    o_ref[...] = x1_ref[...] + g2 * _rms(f, ng_ref[3:4, :])


def _conv_down(x1, mod, mod_row, u, seq_len, rows, p):
    n = x1.shape[0]
    tile = CONV_DOWN_TILE
    width = seq_len // rows
    assert width & (width - 1) == 0 and tile % width == 0 and seq_len % tile == 0
    per_tile = tile // width
    last = n // width - 1
    return pl.pallas_call(
        functools.partial(_conv_down_kernel, seq_len // tile, width),
        grid=(n // tile,),
        in_specs=[pl.BlockSpec((tile, D_MODEL), lambda i: (i, 0)),
                  pl.BlockSpec((1, 1, 6 * D_MODEL), lambda i: (mod_row(i * tile), 0, 0)),
                  pl.BlockSpec((tile, 2 * D_FF), lambda i: (i, 0)),
                  pl.BlockSpec((width, D_FF), lambda i: (jnp.maximum(i * per_tile - 1, 0), 0)),
                  pl.BlockSpec((width, D_FF), lambda i: (jnp.minimum((i + 1) * per_tile, last), 0)),
                  _resident((9, D_FF)), _resident((1, D_FF)),
                  _resident((D_FF, D_MODEL)), _resident((4, D_MODEL))],
        out_specs=pl.BlockSpec((tile, D_MODEL), lambda i: (i, 0)),
        out_shape=jax.ShapeDtypeStruct((n, D_MODEL), F32),
        compiler_params=_params(("arbitrary",)),
        name="ffn_conv_down",
    )(x1, mod, u, u, u, p["ffn_conv"], p["ffn_conv_b"], p["ffn_down"], p["norm_g"])


RWKV_LOCAL_PASSES = 1


def _state_to_pairs(s):
    b = s.shape[0]
    s = s.reshape(b, N_DIR, RWKV_PAIRS, 2, RWKV_HEAD_DIM, RWKV_HEAD_DIM)
    zero = jnp.zeros_like(s[:, :, :, 0])
    top = jnp.concatenate([s[:, :, :, 0], zero], axis=-1)
    bot = jnp.concatenate([zero, s[:, :, :, 1]], axis=-1)
    return jnp.concatenate([top, bot], axis=-2)


def _state_from_pairs(sb):
    b = sb.shape[0]
    n = RWKV_HEAD_DIM
    parts = jnp.stack([sb[..., 0:n, 0:n], sb[..., n:2 * n, n:2 * n]], axis=3)
    return parts.reshape(b, N_DIR, RWKV_HEADS, n, n)


def _trunk(x, mod, mod_row, rows, states, p):
    batch, seq_len, _ = x.shape
    n = batch * seq_len
    x2 = x.reshape(n, D_MODEL)
    s0, c0, n0, m0 = states

    fuse_width = seq_len if (rows == 1 and IN_TILE % seq_len == 0 and MERGE_TILE % seq_len == 0) else None
    z = _in_proj(x2, mod, mod_row, p["norm_g"][0:1], p["w_r"], p["w_g"], p["w_m"], p["w_s"],
                 p["mlstm_conv"], fuse_width)

    rp, y0, gm, hm, gate, bonus = _rwkv_local(z, seq_len, p, RWKV_LOCAL_PASSES)
    ysf, ysb, s_fin = _rwkv_scan(_state_to_pairs(s0), rp, y0, gm, hm, batch, seq_len)
    s_fin = _state_from_pairs(s_fin)

    if fuse_width is None:
        qk, qk_blk, k_off = _qk_conv(z, batch, seq_len, rows, p["mlstm_conv"]), 0, MLSTM_WIDTH
    else:
        qk, qk_blk, k_off = z, ZM_OFF // (2 * MLSTM_WIDTH), ZM_OFF + MLSTM_WIDTH
    gt = z[:, ZG_OFF:ZG_OFF + MLSTM_GATES].reshape(n // MLSTM_CHUNK, MLSTM_CHUNK, MLSTM_GATES).transpose(0, 2, 1)
    nst = N_DIR * MLSTM_HEADS
    kt = qk[:, k_off:k_off + MLSTM_WIDTH].reshape(n // MLSTM_CHUNK, MLSTM_CHUNK, MLSTM_WIDTH).transpose(0, 2, 1)
    hf, hb, ct_fin, n_fin, m_fin = _mlstm_scan(
        z, qk, qk_blk, kt, gt, p["gate_bc"], p["gate_br"],
        jnp.swapaxes(c0, -1, -2).reshape(batch, nst, MLSTM_HEAD_DIM, MLSTM_HEAD_DIM),
        n0.reshape(batch, nst, MLSTM_HEAD_DIM),
        jnp.broadcast_to(m0.reshape(batch, nst, 1), (batch, nst, LANE)), batch, seq_len)
    c_fin = jnp.swapaxes(ct_fin, -1, -2)

    if fuse_width is not None:
        ffn_conv = ("seq", fuse_width)
    else:
        width = seq_len // rows
        assert rows > 1 and width & (width - 1) == 0 and MERGE_TILE % width == 0 and seq_len % MERGE_TILE == 0
        ffn_conv = ("grid", width, seq_len // MERGE_TILE)
    x1, act = _merge(x2, mod, mod_row, z, ysf, ysb, bonus, gate, hf, hb, p, ffn_conv)
    out = _down(x1, mod, mod_row, act, p)

    new_states = (s_fin,
                  c_fin.reshape(batch, N_DIR, MLSTM_HEADS, MLSTM_HEAD_DIM, MLSTM_HEAD_DIM),
                  n_fin.reshape(batch, N_DIR, MLSTM_HEADS, MLSTM_HEAD_DIM),
                  m_fin[:, :, 0].reshape(batch, N_DIR, MLSTM_HEADS))
    return out.reshape(batch, seq_len, D_MODEL), new_states


def _pack_layer(l, ada_w, ada_b, norm_g, w_in, rwkv_mu, rwkv_w0, rwkv_w_up, rwkv_a0, rwkv_a_up,
                rwkv_g_up, rwkv_kk_scale, rwkv_k_a, rwkv_r_k, rwkv_lnx_g, rwkv_lnx_b, mlstm_conv,
                mlstm_gate_b, mlstm_gn_g, w_branch_rwkv, w_branch_mlstm, w_out, ffn_up, ffn_conv,
                ffn_conv_b, ffn_down):
    W = RWKV_WIDTH
    wi = w_in[l]
    w_r = wi[:, 0:RWKV_COLS].astype(BF16)
    w_m = wi[:, RWKV_COLS:RWKV_COLS + 4 * MLSTM_WIDTH].astype(BF16)
    w_g = jnp.pad(wi[:, RWKV_COLS + 4 * MLSTM_WIDTH:RWKV_COLS + MLSTM_COLS].astype(BF16),
                  ((0, 0), (0, LANE - MLSTM_GATES)))
    w_s = wi[:, RWKV_COLS + MLSTM_COLS:].astype(BF16)

    head = jnp.arange(W, dtype=jnp.int32) // RWKV_HEAD_DIM
    same = (head[:, None] == head[None, :])
    gb = mlstm_gate_b[l].reshape(1, MLSTM_GATES)
    return dict(
        ada_w=ada_w[l], ada_b=ada_b[l], norm_g=norm_g[l], w_r=w_r, w_g=w_g, w_m=w_m, w_s=w_s,
        mu=rwkv_mu[l].reshape(1, RWKV_COLS),
        w0=rwkv_w0[l].reshape(N_DIR, 1, W), w_up=rwkv_w_up[l],
        a0=rwkv_a0[l].reshape(N_DIR, 1, W), a_up=rwkv_a_up[l], g_up=rwkv_g_up[l],
        kk_scale=rwkv_kk_scale[l].reshape(1, W), k_a=rwkv_k_a[l].reshape(1, W),
        r_k=rwkv_r_k[l].reshape(1, W),
        lnx_g=rwkv_lnx_g[l].reshape(1, W), lnx_b=rwkv_lnx_b[l].reshape(1, W),
        pones=same.astype(BF16), pmean=(same.astype(F32) / RWKV_HEAD_DIM).astype(BF16),
        mlstm_conv=mlstm_conv[l].reshape(9, 2 * MLSTM_WIDTH),
        gate_bc=jnp.pad(gb, ((0, 0), (0, LANE - MLSTM_GATES))), gate_br=gb.reshape(MLSTM_GATES, 1),
        gn_g=mlstm_gn_g[l].reshape(1, MLSTM_WIDTH),
        w_br=w_branch_rwkv[l].astype(BF16), w_bm=w_branch_mlstm[l].astype(BF16),
        w_out=w_out[l].astype(BF16), ffn_up=ffn_up[l].astype(BF16),
        ffn_conv=ffn_conv[l].reshape(9, D_FF), ffn_conv_b=ffn_conv_b[l].reshape(1, D_FF),
        ffn_down=ffn_down[l].astype(BF16),
    )


def kernel(x_prompt, x_sample, c, state_rwkv, state_mlstm_C, state_mlstm_n, state_mlstm_m, c_ctx,
           ada_w, ada_b, norm_g, w_in, rwkv_mu, rwkv_w0, rwkv_w_up, rwkv_a0, rwkv_a_up, rwkv_g_up,
           rwkv_kk_scale, rwkv_k_a, rwkv_r_k, rwkv_lnx_g, rwkv_lnx_b, mlstm_conv, mlstm_gate_b,
           mlstm_gn_g, w_branch_rwkv, w_branch_mlstm, w_out, ffn_up, ffn_conv, ffn_conv_b, ffn_down):
    depth = ada_w.shape[0]
    batch = x_prompt.shape[0]
    dec_batch, dec_seq, _ = x_sample.shape
    latent_rows = dec_seq // GRID_W
    ctx_init = (jnp.zeros((batch, N_DIR, RWKV_HEADS, RWKV_HEAD_DIM, RWKV_HEAD_DIM), F32),
                jnp.zeros((batch, N_DIR, MLSTM_HEADS, MLSTM_HEAD_DIM, MLSTM_HEAD_DIM), F32),
                jnp.zeros((batch, N_DIR, MLSTM_HEADS, MLSTM_HEAD_DIM), F32),
                jnp.zeros((batch, N_DIR, MLSTM_HEADS), F32))
    cond = jnp.concatenate([c_ctx[None, :], c, jnp.zeros((8 - 1 - dec_batch, D_MODEL), F32)], axis=0)

    xp, xs = x_prompt, x_sample
    new_s, new_c, new_n, new_m = [], [], [], []
    for l in range(depth):
        p = _pack_layer(l, ada_w, ada_b, norm_g, w_in, rwkv_mu, rwkv_w0, rwkv_w_up, rwkv_a0, rwkv_a_up,
                        rwkv_g_up, rwkv_kk_scale, rwkv_k_a, rwkv_r_k, rwkv_lnx_g, rwkv_lnx_b, mlstm_conv,
                        mlstm_gate_b, mlstm_gn_g, w_branch_rwkv, w_branch_mlstm, w_out, ffn_up, ffn_conv,
                        ffn_conv_b, ffn_down)
        mod = _ada(cond, p["ada_w"], p["ada_b"]).reshape(8, 1, 6 * D_MODEL)
        xp, (s, cc, nn, mm) = _trunk(xp, mod, lambda r: 0, 1, ctx_init, p)
        new_s.append(s)
        new_c.append(cc)
        new_n.append(nn)
        new_m.append(mm)
        xs, _ = _trunk(xs, mod, lambda r: 1 + r // dec_seq, latent_rows,
                       (state_rwkv[:, l], state_mlstm_C[:, l], state_mlstm_n[:, l], state_mlstm_m[:, l]), p)
    return (xp, xs, jnp.stack(new_s, axis=1), jnp.stack(new_c, axis=1),
            jnp.stack(new_n, axis=1), jnp.stack(new_m, axis=1))
```

```python
import functools

import jax
import jax.numpy as jnp
from jax import lax
from jax.experimental import pallas as pl
from jax.experimental.pallas import tpu as pltpu

F32 = jnp.float32
BF16 = jnp.bfloat16

D_MODEL = 1024
N_DIR = 2
RWKV_HEADS = 8
RWKV_HEAD_DIM = 64
RWKV_WIDTH = RWKV_HEADS * RWKV_HEAD_DIM
DECAY_LORA = 64
ICLR_LORA = 64
GATE_LORA = 128
MLSTM_HEADS = 4
MLSTM_HEAD_DIM = 128
MLSTM_WIDTH = MLSTM_HEADS * MLSTM_HEAD_DIM
MLSTM_CHUNK = 64
D_FF = 2816
GRID_W = 64
RMS_EPS = 1e-6
RWKV_GN_EPS = 64e-5
MLSTM_GN_EPS = 1e-5
DECAY_SCALE = 0.606531

RWKV_COLS = 3 * RWKV_WIDTH + N_DIR * DECAY_LORA + N_DIR * ICLR_LORA + GATE_LORA
MLSTM_GATES = 2 * N_DIR * MLSTM_HEADS
MLSTM_COLS = 4 * MLSTM_WIDTH + MLSTM_GATES
GATE_COLS = 2 * D_MODEL

LANE = 128
ZR_BLOCK = 2048
ZG_OFF = RWKV_COLS
ZM_OFF = ZR_BLOCK
ZS_OFF = ZM_OFF + 4 * MLSTM_WIDTH
Z_COLS = ZS_OFF + GATE_COLS

IN_TILE = 512
MERGE_TILE = 256
DOWN_TILE = 512
RCHUNK = 64
CONV_CH_TILE = 256
VMEM_LIMIT = 56 * 1024 * 1024


def _params(sem):
    return pltpu.CompilerParams(dimension_semantics=sem, vmem_limit_bytes=VMEM_LIMIT)


def _resident(shape):
    nd = len(shape)
    return pl.BlockSpec(shape, lambda *_: (0,) * nd, pipeline_mode=pl.Buffered(1))


def _split2(a):
    hi = a.astype(BF16)
    lo = (a - hi.astype(F32)).astype(BF16)
    return hi, lo


def _split3(a):
    hi = a.astype(BF16)
    r1 = a - hi.astype(F32)
    mid = r1.astype(BF16)
    lo = (r1 - mid.astype(F32)).astype(BF16)
    return hi, mid, lo


def _dg(a, b, dims):
    return lax.dot_general(a, b, dims, preferred_element_type=F32)


def _mm(a, b, dims, passes):
    if passes == 1:
        return _dg(a.astype(BF16), b.astype(BF16), dims)
    ah, al = _split2(a)
    bh, bl = _split2(b)
    return _dg(ah, bh, dims) + (_dg(ah, bl, dims) + _dg(al, bh, dims))


def _mm_exact_lhs(a_bf16, b, dims):
    b1, b2, b3 = _split3(b)
    return _dg(a_bf16, b1, dims) + (_dg(a_bf16, b2, dims) + _dg(a_bf16, b3, dims))


def _mm_exact_rhs(a, b_bf16, dims, pieces=3):
    if pieces == 2:
        a1, a2 = _split2(a)
        return _dg(a1, b_bf16, dims) + _dg(a2, b_bf16, dims)
    a1, a2, a3 = _split3(a)
    return _dg(a1, b_bf16, dims) + (_dg(a2, b_bf16, dims) + _dg(a3, b_bf16, dims))


_NN = (((1,), (0,)), ((), ()))
_NT = (((1,), (1,)), ((), ()))
_TN = (((0,), (0,)), ((), ()))
_BNN = (((2,), (1,)), ((0,), (0,)))
_BNT = (((2,), (2,)), ((0,), (0,)))
_BTN = (((1,), (1,)), ((0,), (0,)))


def _sigmoid(x):
    return jax.nn.sigmoid(x)


def _silu(x):
    return x * jax.nn.sigmoid(x)


def _rms(x, g):
    return x * lax.rsqrt(jnp.mean(x * x, axis=-1, keepdims=True) + RMS_EPS) * g


def _ada_kernel(cond_ref, w_ref, b_ref, o_ref):
    s = _silu(cond_ref[...])
    o_ref[...] = _dg(s.astype(BF16), w_ref[...].astype(BF16), _NN) + b_ref[...]


def _ada(cond8, ada_w, ada_b):
    n = ada_w.shape[1]
    tn = 1536
    return pl.pallas_call(
        _ada_kernel,
        grid=(n // tn,),
        in_specs=[_resident((8, D_MODEL)),
                  pl.BlockSpec((D_MODEL, tn), lambda j: (0, j)),
                  pl.BlockSpec((1, tn), lambda j: (0, j))],
        out_specs=pl.BlockSpec((8, tn), lambda j: (0, j)),
        out_shape=jax.ShapeDtypeStruct((8, n), F32),
        compiler_params=_params(("arbitrary",)),
        name="ada_mod",
    )(cond8, ada_w, ada_b.reshape(1, n))


def _in_kernel(conv_width, x_ref, mod_ref, g_ref, wr_ref, wg_ref, wm_ref, ws_ref, cw_ref, z_ref, gt_ref,
               *kt_ref):
    mod = mod_ref[0]
    sh = mod[:, 0:D_MODEL]
    sc = mod[:, D_MODEL:2 * D_MODEL]
    h = (_rms(x_ref[...], g_ref[...]) * (1.0 + sc) + sh).astype(BF16)
    z_ref[:, 0:ZG_OFF] = _dg(h, wr_ref[...], _NN)
    zg = _dg(h, wg_ref[...], _NN)
    z_ref[:, ZG_OFF:ZM_OFF] = zg
    zgt = zg.T
    for c in range(zg.shape[0] // MLSTM_CHUNK):
        gt_ref[c] = zgt[0:MLSTM_GATES, c * MLSTM_CHUNK:(c + 1) * MLSTM_CHUNK]
    zm = _dg(h, wm_ref[...], _NN)
    if conv_width is None:
        z_ref[:, ZM_OFF:ZS_OFF] = zm
    else:
        qk_cols = 2 * MLSTM_WIDTH
        qk = _silu(_dwconv(zm[:, 0:qk_cols], cw_ref, conv_width, False))
        z_ref[:, ZM_OFF:ZM_OFF + qk_cols] = qk
        z_ref[:, ZM_OFF + qk_cols:ZS_OFF] = zm[:, qk_cols:]
        _store_time_on_lanes(qk[:, MLSTM_WIDTH:qk_cols], kt_ref[0], MLSTM_CHUNK)
    z_ref[:, ZS_OFF:Z_COLS] = _sigmoid(_dg(h, ws_ref[...], _NN))


def _in_proj(x2, mod, mod_row, norm_g0, w_r, w_g, w_m, w_s, conv_w9, conv_width):
    n = x2.shape[0]
    tile = IN_TILE
    cpt = tile // MLSTM_CHUNK
    nchunk = n // MLSTM_CHUNK
    out_specs = [pl.BlockSpec((tile, Z_COLS), lambda i: (i, 0)),
                 pl.BlockSpec((cpt, MLSTM_GATES, MLSTM_CHUNK), lambda i: (i, 0, 0))]
    out_shape = [jax.ShapeDtypeStruct((n, Z_COLS), F32),
                 jax.ShapeDtypeStruct((nchunk, MLSTM_GATES, MLSTM_CHUNK), F32)]
    if conv_width is not None:
        out_specs.append(pl.BlockSpec((cpt, MLSTM_WIDTH, MLSTM_CHUNK), lambda i: (i, 0, 0)))
        out_shape.append(jax.ShapeDtypeStruct((nchunk, MLSTM_WIDTH, MLSTM_CHUNK), F32))
    return pl.pallas_call(
        functools.partial(_in_kernel, conv_width),
        grid=(n // tile,),
        in_specs=[pl.BlockSpec((tile, D_MODEL), lambda i: (i, 0)),
                  pl.BlockSpec((1, 1, 6 * D_MODEL), lambda i: (mod_row(i * tile), 0, 0)),
                  _resident((1, D_MODEL)),
                  _resident(w_r.shape), _resident(w_g.shape), _resident(w_m.shape), _resident(w_s.shape),
                  _resident(conv_w9.shape)],
        out_specs=out_specs,
        out_shape=out_shape,
        compiler_params=_params(("arbitrary",)),
        name="in_proj",
    )(x2, mod, norm_g0, w_r, w_g, w_m, w_s, conv_w9)


LOCAL_CHUNKS = 4
PAIR_LANES = 2 * RWKV_HEAD_DIM
RWKV_PAIRS = RWKV_HEADS // 2


def _bd(x):
    lane = lax.broadcasted_iota(jnp.int32, x.shape, 1)
    left = lane < RWKV_HEAD_DIM
    return jnp.concatenate([jnp.where(left, x, 0.0), jnp.where(left, 0.0, x)], axis=0)


def _rwkv_local_kernel(chunks_per_seq, passes,
                       z_ref, zp_ref, zn_ref, mu_ref, w0_ref, wup_ref, a0_ref, aup_ref, gup_ref,
                       kks_ref, ka_ref, rk_ref, pones_ref,
                       rp_ref, y0_ref, gm_ref, hm_ref, gate_ref, bonus_ref):
    C = RCHUNK
    W = RWKV_WIDTH
    NS = LOCAL_CHUNKS
    R = NS * C
    first = (pl.program_id(0) * NS) % chunks_per_seq
    has_prev = first != 0
    has_next = first + NS != chunks_per_seq

    z = z_ref[:, 0:RWKV_COLS]
    zp = jnp.where(has_prev, zp_ref[7:8, 0:RWKV_COLS], 0.0)
    zn = jnp.where(has_next, zn_ref[0:1, 0:RWKV_COLS], 0.0)
    trow = lax.broadcasted_iota(jnp.int32, (R, 1), 0)
    prev = jnp.where(trow == 0, zp, pltpu.roll(z, 1, 0))
    nxt = jnp.where(trow == R - 1, zn, pltpu.roll(z, R - 1, 0))
    zs = z + mu_ref[...] * (0.5 * (prev + nxt) - z)

    r = zs[:, 0:W]
    k = zs[:, W:2 * W]
    v = zs[:, 2 * W:3 * W]
    gd = zs[:, 3 * W + 2 * DECAY_LORA + 2 * ICLR_LORA:RWKV_COLS]
    gate_ref[...] = _dg(_sigmoid(gd).astype(BF16), gup_ref[...].astype(BF16), _NN)

    pones = pones_ref[...]
    kks = k * kks_ref[...]
    norm = jnp.sqrt(_mm_exact_rhs(kks * kks, pones, _NN, pieces=2))
    kk = kks / jnp.maximum(norm, 1e-12)

    P = PAIR_LANES
    row = lax.broadcasted_iota(jnp.int32, (R, R), 0)
    col = lax.broadcasted_iota(jnp.int32, (R, R), 1)
    same_chunk = jnp.bitwise_and(row, -C) == jnp.bitwise_and(col, -C)
    prow = lax.broadcasted_iota(jnp.int32, (C, P), 0)
    pcol = jnp.bitwise_and(lax.broadcasted_iota(jnp.int32, (C, P), 1), RWKV_HEAD_DIM - 1)
    eye_p = jnp.where(prow == pcol, 1.0, 0.0)
    left_head = lax.broadcasted_iota(jnp.int32, (C, P), 1) < RWKV_HEAD_DIM

    def diag_blocks(m):
        return jnp.where(left_head, m[0:RWKV_HEAD_DIM], m[RWKV_HEAD_DIM:P])

    abar, rbar, kt, bt, kw, bw, wc, strict, incl = [], [], [], [], [], [], [], [], []
    kd_sum = None
    for d in range(N_DIR):
        o = 3 * W + d * DECAY_LORA
        wd = zs[:, o:o + DECAY_LORA]
        o = 3 * W + 2 * DECAY_LORA + d * ICLR_LORA
        ad = zs[:, o:o + ICLR_LORA]
        logw = -DECAY_SCALE * _sigmoid(w0_ref[d] + _dg(jnp.tanh(wd).astype(BF16), wup_ref[d].astype(BF16), _NN))
        a = _sigmoid(a0_ref[d] + _dg(ad.astype(BF16), aup_ref[d].astype(BF16), _NN))
        kd = k * (1.0 + (a - 1.0) * ka_ref[...])
        b = kk * a
        kd_sum = kd if kd_sum is None else kd_sum + kd

        earlier_or_same = same_chunk & ((row >= col) if d == 0 else (row <= col))
        cum_i = _mm_exact_lhs(jnp.where(earlier_or_same, 1.0, 0.0).astype(BF16), logw, _NN)
        cum_e = cum_i - logw
        ab_d, rb_d, kt_d, bt_d, kw_d, bw_d, wc_d = [], [], [], [], [], [], []
        for s in range(NS):
            rs = slice(s * C, (s + 1) * C)
            ci_s = cum_i[rs]
            ctot = jnp.sum(logw[rs], axis=0, keepdims=True)
            e_ni = jnp.exp(-ci_s)
            e_ti = jnp.exp(ctot - ci_s)
            ab_d.append(kk[rs] * jnp.exp(cum_e[rs]))
            rb_d.append(r[rs] * jnp.exp(ci_s))
            kt_d.append(kd[rs] * e_ni)
            bt_d.append(b[rs] * e_ni)
            kw_d.append(kd[rs] * e_ti)
            bw_d.append(b[rs] * e_ti)
            wc_d.append(jnp.exp(ctot))
        abar.append(ab_d)
        rbar.append(rb_d)
        kt.append(kt_d)
        bt.append(bt_d)
        kw.append(kw_d)
        bw.append(bw_d)
        wc.append(wc_d)
        strict.append((prow > pcol) if d == 0 else (prow < pcol))
        incl.append((prow >= pcol) if d == 0 else (prow <= pcol))
    bonus_ref[...] = _mm_exact_rhs(r * kd_sum * rk_ref[...], pones, _NN, pieces=2) * v

    mm = functools.partial(_mm, passes=passes)
    chains = [(s, d, p) for s in range(NS) for d in range(N_DIR) for p in range(RWKV_PAIRS)]
    nch = range(len(chains))

    def sel(arr, i):
        s, d, p = chains[i]
        return arr[d][s][:, p * P:(p + 1) * P]

    cat0 = lambda a_, b_: jnp.concatenate([a_, b_], axis=0)
    cat1 = lambda a_, b_: jnp.concatenate([a_, b_], axis=1)
    vsl = [v[s * C:(s + 1) * C, p * P:(p + 1) * P] for s, _, p in chains]
    lhs = [cat0(sel(abar, i), sel(rbar, i)) for i in nch]
    by_b = [mm(lhs[i], _bd(sel(bt, i)), _NT) for i in nch]
    by_k = [mm(lhs[i], _bd(sel(kt, i)), _NT) for i in nch]
    a_kk = [jnp.where(strict[chains[i][1]], by_b[i][0:C], 0.0) for i in nch]
    a_rb = [jnp.where(incl[chains[i][1]], by_b[i][C:2 * C], 0.0) for i in nch]
    a_kv = [jnp.where(strict[chains[i][1]], by_k[i][0:C], 0.0) for i in nch]
    a_rk = [jnp.where(incl[chains[i][1]], by_k[i][C:2 * C], 0.0) for i in nch]
    on_v = [mm(cat0(a_kv[i], a_rk[i]), _bd(vsl[i]), _NN) for i in nch]

    x = [-m for m in a_kk]
    tinv = [eye_p + m for m in x]
    x = [mm(m, _bd(m), _NN) for m in x]
    for _ in range(4):
        both = [mm(cat0(tinv[i], x[i]), _bd(x[i]), _NN) for i in nch]
        tinv = [tinv[i] + both[i][0:C] for i in nch]
        x = [m[C:2 * C] for m in both]
    tinv = [tinv[i] + mm(tinv[i], _bd(x[i]), _NN) for i in nch]

    solved = [mm(tinv[i], cat1(_bd(sel(abar, i)), _bd(on_v[i][0:C])), _NN) for i in nch]
    ap = [m[:, 0:P] for m in solved]
    u0 = [m[:, P:2 * P] for m in solved]
    corr = [mm(a_rb[i], cat1(_bd(ap[i]), _bd(u0[i])), _NN) for i in nch]
    on_b = [mm(cat1(ap[i], u0[i]), sel(bw, i), _TN) for i in nch]
    vk = [mm(vsl[i], sel(kw, i), _TN) for i in nch]
    for i in nch:
        s, d, p = chains[i]
        rows = slice(s * C, (s + 1) * C)
        lanes = slice(p * P, (p + 1) * P)
        rp_ref[d, rows, lanes] = (sel(rbar, i) - corr[i][:, 0:P]).astype(BF16)
        y0_ref[d, rows, lanes] = on_v[i][C:2 * C] - corr[i][:, P:2 * P]
        gm_ref[d, s, p] = (eye_p * sel(wc, i) - diag_blocks(on_b[i][0:P])).astype(BF16)
        hm_ref[d, s, p] = diag_blocks(vk[i] - on_b[i][P:2 * P])


def _rwkv_local(z, seq_len, p, passes):
    n = z.shape[0]
    nchunk = n // RCHUNK
    cps = seq_len // RCHUNK
    assert cps % LOCAL_CHUNKS == 0
    W = RWKV_WIDTH
    rows = LOCAL_CHUNKS * RCHUNK
    hb = rows // 8
    last8 = n // 8 - 1
    mat = lambda dt: jax.ShapeDtypeStruct((N_DIR, nchunk, RWKV_PAIRS, RWKV_HEAD_DIM, PAIR_LANES), dt)
    mat_spec = pl.BlockSpec((N_DIR, LOCAL_CHUNKS, RWKV_PAIRS, RWKV_HEAD_DIM, PAIR_LANES),
                            lambda c: (0, c, 0, 0, 0))
    tok = lambda dt: jax.ShapeDtypeStruct((N_DIR, n, W), dt)
    tok_spec = pl.BlockSpec((N_DIR, rows, W), lambda c: (0, c, 0))
    row_spec = pl.BlockSpec((rows, W), lambda c: (c, 0))
    return pl.pallas_call(
        functools.partial(_rwkv_local_kernel, cps, passes),
        grid=(nchunk // LOCAL_CHUNKS,),
        in_specs=[pl.BlockSpec((rows, ZR_BLOCK), lambda c: (c, 0)),
                  pl.BlockSpec((8, ZR_BLOCK), lambda c: (jnp.maximum(c * hb - 1, 0), 0)),
                  pl.BlockSpec((8, ZR_BLOCK), lambda c: (jnp.minimum((c + 1) * hb, last8), 0)),
                  _resident((1, RWKV_COLS)),
                  _resident((N_DIR, 1, W)), _resident((N_DIR, DECAY_LORA, W)),
                  _resident((N_DIR, 1, W)), _resident((N_DIR, ICLR_LORA, W)),
                  _resident((GATE_LORA, W)),
                  _resident((1, W)), _resident((1, W)), _resident((1, W)),
                  _resident((W, W))],
        out_specs=[tok_spec, tok_spec, mat_spec, mat_spec, row_spec, row_spec],
        out_shape=[tok(BF16), tok(F32), mat(BF16), mat(F32),
                   jax.ShapeDtypeStruct((n, W), F32), jax.ShapeDtypeStruct((n, W), F32)],
        compiler_params=_params(("arbitrary",)),
        name="rwkv_local",
    )(z, z, z, p["mu"], p["w0"], p["w_up"], p["a0"], p["a_up"], p["g_up"],
      p["kk_scale"], p["k_a"], p["r_k"], p["pones"])


SCAN_CHUNKS = 4


def _rwkv_scan_kernel(s0_ref, rpf_ref, rpb_ref, y0f_ref, y0b_ref, gmf_ref, gmb_ref, hmf_ref, hmb_ref,
                      ysf_ref, ysb_ref, sout_ref, s_scr):
    @pl.when(pl.program_id(1) == 0)
    def _():
        s_scr[...] = s0_ref[0]

    K = SCAN_CHUNKS
    C = RCHUNK
    rp_ref, y0_ref, gm_ref, hm_ref, ys_ref = ((rpf_ref, rpb_ref), (y0f_ref, y0b_ref), (gmf_ref, gmb_ref),
                                              (hmf_ref, hmb_ref), (ysf_ref, ysb_ref))
    chains = [(d, p) for d in range(N_DIR) for p in range(RWKV_PAIRS)]
    lanes = [slice(p * PAIR_LANES, (p + 1) * PAIR_LANES) for _, p in chains]
    nch = range(len(chains))
    s = [s_scr[d, p] for d, p in chains]
    for j in range(K):
        at = (j, K - 1 - j)
        rows = [slice(at[d] * C, (at[d] + 1) * C) for d, _ in chains]
        sb = [m.astype(BF16) for m in s]
        y = [_dg(rp_ref[chains[i][0]][0, rows[i], lanes[i]], sb[i], _NT) for i in nch]
        sg = [_dg(sb[i], _bd(gm_ref[chains[i][0]][0, at[chains[i][0]], chains[i][1]]), _NN) for i in nch]
        for i in nch:
            d, p = chains[i]
            ys_ref[d][rows[i], lanes[i]] = y[i] + y0_ref[d][0, rows[i], lanes[i]]
        s = [sg[i] + _bd(hm_ref[chains[i][0]][0, at[chains[i][0]], chains[i][1]]) for i in nch]
    for i in nch:
        d, p = chains[i]
        s_scr[d, p] = s[i]
        sout_ref[0, d, p] = s[i]


def _rwkv_scan(s0, rp, y0, gm, hm, batch, seq_len):
    K = SCAN_CHUNKS
    spb = seq_len // (RCHUNK * K)
    n = batch * seq_len

    def fwd(b, s):
        return b * spb + s

    def bwd(b, s):
        return b * spb + spb - 1 - s

    def mat_spec(d, at):
        return pl.BlockSpec((1, K, RWKV_PAIRS, RWKV_HEAD_DIM, PAIR_LANES), lambda b, s: (d, at(b, s), 0, 0, 0))

    def tok_spec(d, at):
        return pl.BlockSpec((1, K * RCHUNK, RWKV_WIDTH), lambda b, s: (d, at(b, s), 0))

    st_spec = pl.BlockSpec((1, N_DIR, RWKV_PAIRS, PAIR_LANES, PAIR_LANES), lambda b, s: (b, 0, 0, 0, 0))
    ys = jax.ShapeDtypeStruct((n, RWKV_WIDTH), F32)
    return pl.pallas_call(
        _rwkv_scan_kernel,
        grid=(batch, spb),
        in_specs=[st_spec, tok_spec(0, fwd), tok_spec(1, bwd), tok_spec(0, fwd), tok_spec(1, bwd),
                  mat_spec(0, fwd), mat_spec(1, bwd), mat_spec(0, fwd), mat_spec(1, bwd)],
        out_specs=[pl.BlockSpec((K * RCHUNK, RWKV_WIDTH), lambda b, s: (fwd(b, s), 0)),
                   pl.BlockSpec((K * RCHUNK, RWKV_WIDTH), lambda b, s: (bwd(b, s), 0)),
                   st_spec],
        out_shape=[ys, ys,
                   jax.ShapeDtypeStruct((batch, N_DIR, RWKV_PAIRS, PAIR_LANES, PAIR_LANES), F32)],
        scratch_shapes=[pltpu.VMEM((N_DIR, RWKV_PAIRS, PAIR_LANES, PAIR_LANES), F32)],
        compiler_params=_params(("arbitrary", "arbitrary")),
        name="rwkv_scan",
    )(s0, rp, rp, y0, y0, gm, gm, hm, hm)


CONV_BLOCK_ROWS = 2048


def _dwconv(x, w_ref, width, vertical):
    T = x.shape[0]
    t = lax.broadcasted_iota(jnp.int32, (T, 1), 0)
    assert width & (width - 1) == 0
    colp = jnp.bitwise_and(t, width - 1)
    xl = jnp.where(colp == 0, 0.0, pltpu.roll(x, 1, 0))
    xr = jnp.where(colp == width - 1, 0.0, pltpu.roll(x, T - 1, 0))

    def tap_row(i):
        return w_ref[3 * i:3 * i + 1, :] * xl + w_ref[3 * i + 1:3 * i + 2, :] * x + w_ref[3 * i + 2:3 * i + 3, :] * xr

    out = tap_row(1)
    if vertical:
        out = out + jnp.where(t < width, 0.0, pltpu.roll(tap_row(0), width, 0))
        out = out + jnp.where(t >= T - width, 0.0, pltpu.roll(tap_row(2), T - width, 0))
    return out


def _conv_geometry(n, seq_len, rows):
    if rows > 1:
        return seq_len, seq_len // rows, True
    block = CONV_BLOCK_ROWS if (n % CONV_BLOCK_ROWS == 0 and CONV_BLOCK_ROWS % seq_len == 0) else seq_len
    return block, seq_len, False


def _store_time_on_lanes(x, out_ref, chunk):
    xt = x.T
    for c in range(x.shape[0] // chunk):
        out_ref[c] = xt[:, c * chunk:(c + 1) * chunk]


def _qk_conv_kernel(width, vertical, first_k_tile, x_ref, w_ref, o_ref, kt_ref):
    out = _silu(_dwconv(x_ref[...], w_ref, width, vertical))
    o_ref[...] = out

    @pl.when(pl.program_id(1) >= first_k_tile)
    def _():
        _store_time_on_lanes(out, kt_ref, MLSTM_CHUNK)


def _qk_conv(z, batch, seq_len, rows, conv_w9):
    n = batch * seq_len
    ch = 2 * MLSTM_WIDTH
    tc = CONV_CH_TILE
    off = ZM_OFF // tc
    first_k = MLSTM_WIDTH // tc
    block, width, vertical = _conv_geometry(n, seq_len, rows)
    cpb = block // MLSTM_CHUNK
    return pl.pallas_call(
        functools.partial(_qk_conv_kernel, width, vertical, first_k),
        grid=(n // block, ch // tc),
        in_specs=[pl.BlockSpec((block, tc), lambda b, j: (b, off + j)),
                  pl.BlockSpec((9, tc), lambda b, j: (0, j))],
        out_specs=[pl.BlockSpec((block, tc), lambda b, j: (b, j)),
                   pl.BlockSpec((cpb, tc, MLSTM_CHUNK), lambda b, j: (b, jnp.maximum(j - first_k, 0), 0))],
        out_shape=[jax.ShapeDtypeStruct((n, ch), F32),
                   jax.ShapeDtypeStruct((n // MLSTM_CHUNK, MLSTM_WIDTH, MLSTM_CHUNK), F32)],
        compiler_params=_params(("arbitrary", "arbitrary")),
        name="mlstm_qk_conv",
    )(z, conv_w9)


MLSTM_STEP_CHUNKS = 4

def _mlstm_scan_kernel(qkf_ref, qkb_ref, ktf_ref, ktb_ref, vf_ref, vb_ref, gcf_ref, gcb_ref, grf_ref, grb_ref,
                       gbc_ref, gbr_ref, c0_ref, n0_ref, m0_ref,
                       hf_ref, hb_ref, cout_ref, nout_ref, mout_ref,
                       c_scr, n_scr, m_scr):
    step = pl.program_id(1)
    L = MLSTM_CHUNK
    dh = MLSTM_HEAD_DIM
    H = MLSTM_HEADS

    @pl.when(step == 0)
    def _():
        c_scr[...] = c0_ref[0]
        n_scr[...] = n0_ref[0]
        m_scr[...] = m0_ref[0]

    K = MLSTM_STEP_CHUNKS
    R = K * L
    row = lax.broadcasted_iota(jnp.int32, (L, L), 0)
    col = lax.broadcasted_iota(jnp.int32, (L, L), 1)
    lower = (row >= col)
    upper = (row <= col)
    lower_b = jnp.where(lower, 1.0, 0.0).astype(BF16)
    upper_b = jnp.where(upper, 1.0, 0.0).astype(BF16)
    rrow = lax.broadcasted_iota(jnp.int32, (R, R), 0)
    rcol = lax.broadcasted_iota(jnp.int32, (R, R), 1)
    same_chunk = jnp.bitwise_and(rrow, -L) == jnp.bitwise_and(rcol, -L)
    neg_inf = jnp.full((), -jnp.inf, F32)

    gcol, grow, bcol, brow, btot = [], [], [], [], []
    ones_b = jnp.ones((L, LANE), BF16)
    for d in range(N_DIR):
        gc_ref, gr_ref = (gcf_ref, grf_ref) if d == 0 else (gcb_ref, grb_ref)
        gcol.append(gc_ref[...] + gbc_ref[...])
        grow.append((gr_ref[...] + gbr_ref[...][None]).reshape(K * MLSTM_GATES, L))
        before = same_chunk & ((rrow >= rcol) if d == 0 else (rrow <= rcol))
        bcol.append(_mm_exact_lhs(jnp.where(before, 1.0, 0.0).astype(BF16), jax.nn.log_sigmoid(gcol[d]), _NN))
        frow = jax.nn.log_sigmoid(grow[d])
        brow.append(_mm_exact_rhs(frow, upper_b if d == 0 else lower_b, _NN))
        btot.append(_mm_exact_rhs(frow, ones_b, _NN))

    units = [(j, d, h) for j in range(K) for d in range(N_DIR) for h in range(H)]
    nun = range(len(units))
    q, k, kt, v, vb, qb = [], [], [], [], [], []
    c_row, b_col, b_last = [], [], []
    for j, d, h in units:
        at = j if d == 0 else K - 1 - j
        rows = slice(at * L, (at + 1) * L)
        st = d * H + h
        gi, gf = st, 2 * H + st
        qk_ref, kt_ref, v_ref = (qkf_ref, ktf_ref, vf_ref) if d == 0 else (qkb_ref, ktb_ref, vb_ref)
        q.append(qk_ref[rows, h * dh:(h + 1) * dh] * (dh ** -0.5))
        k.append(qk_ref[rows, MLSTM_WIDTH + h * dh:MLSTM_WIDTH + (h + 1) * dh])
        kt.append(kt_ref[at, h * dh:(h + 1) * dh, :])
        v.append(v_ref[rows, h * dh:(h + 1) * dh])
        qb.append(q[-1].astype(BF16))
        vb.append(v[-1].astype(BF16))
        b_col.append(jnp.broadcast_to(bcol[d][rows, gf:gf + 1], (L, LANE)))
        c_row.append(grow[d][at * MLSTM_GATES + gi:at * MLSTM_GATES + gi + 1, :]
                     - brow[d][at * MLSTM_GATES + gf:at * MLSTM_GATES + gf + 1, :])
        b_last.append(btot[d][at * MLSTM_GATES + gf:at * MLSTM_GATES + gf + 1, :])

    last = [L - 1 if d == 0 else 0 for _, d, _ in units]
    qk_t = [_dg(qb[i], k[i].astype(BF16), _NT) for i in nun]
    rel = [jnp.where(lower if units[i][1] == 0 else upper, c_row[i], neg_inf) for i in nun]
    mx = [jnp.broadcast_to(jnp.max(rel[i], axis=-1, keepdims=True), (L, LANE)) for i in nun]
    m_loc = [b_col[i] + mx[i] for i in nun]
    s_loc = [qk_t[i] * jnp.exp(rel[i] - mx[i][:, 0:L]) for i in nun]
    s_v = [_dg(s_loc[i].astype(BF16), vb[i], _NN) for i in nun]
    s_sum = [jnp.broadcast_to(jnp.sum(s_loc[i], axis=-1, keepdims=True), (L, LANE)) for i in nun]
    cmax = [mx[i][last[i]:last[i] + 1, :] for i in nun]
    m_w = [b_last[i] + cmax[i] for i in nun]
    wj = [jnp.exp(c_row[i] - cmax[i][:, 0:L]) for i in nun]
    kv = [_dg((kt[i] * wj[i]).astype(BF16), vb[i], _NN) for i in nun]
    w_k = [_mm(jnp.broadcast_to(wj[i], (8, L)), k[i], _NN, 3)[0:1] for i in nun]

    nst = N_DIR * H
    c_st = [c_scr[st] for st in range(nst)]
    n_st = [n_scr[st:st + 1, :] for st in range(nst)]
    m_st = [m_scr[st:st + 1, :] for st in range(nst)]
    for j in range(K):
        idx = [j * nst + st for st in range(nst)]
        q_c = [_dg(qb[i], c_st[st].astype(BF16), _NN) for st, i in enumerate(idx)]
        for st, i in enumerate(idx):
            _, d, h = units[i]
            at = j if d == 0 else K - 1 - j
            h_ref = hf_ref if d == 0 else hb_ref
            log_inter = b_col[i] + m_st[st]
            m_s = jnp.maximum(log_inter, m_loc[i])
            inter = jnp.exp(log_inter - m_s)
            local = jnp.exp(m_loc[i] - m_s)
            q_n = jnp.broadcast_to(jnp.sum(q[i] * n_st[st], axis=-1, keepdims=True), (L, LANE))
            den = inter * q_n + local * s_sum[i]
            scale = 1.0 / jnp.maximum(jnp.abs(den), jnp.exp(-m_s))
            h_ref[at * L:(at + 1) * L, h * dh:(h + 1) * dh] = (inter * scale) * q_c[st] + (local * scale) * s_v[i]
            m_new = jnp.maximum(b_last[i] + m_st[st], m_w[i])
            carry = jnp.exp(b_last[i] + m_st[st] - m_new)
            fresh = jnp.exp(m_w[i] - m_new)
            c_st[st] = carry * c_st[st] + fresh * kv[i]
            n_st[st] = carry * n_st[st] + fresh * w_k[i]
            m_st[st] = m_new


    for st in range(nst):
        c_scr[st] = c_st[st]
        n_scr[st:st + 1, :] = n_st[st]
        m_scr[st:st + 1, :] = m_st[st]
    cout_ref[0] = c_scr[...]
    nout_ref[0] = n_scr[...]
    mout_ref[0] = m_scr[...]


def _mlstm_scan(z, qk, qk_blk, kt, gt, gate_bc, gate_br, c0, n0, m0, batch, seq_len):
    K = MLSTM_STEP_CHUNKS
    L = K * MLSTM_CHUNK
    assert seq_len % L == 0
    cps = seq_len // L
    n = batch * seq_len
    W = MLSTM_WIDTH
    nst = N_DIR * MLSTM_HEADS
    dh = MLSTM_HEAD_DIM

    def fw(b, c):
        return b * cps + c

    def bw(b, c):
        return b * cps + cps - 1 - c

    vblk = (ZM_OFF + 2 * W) // W
    gblk = ZG_OFF // LANE
    return pl.pallas_call(
        _mlstm_scan_kernel,
        grid=(batch, cps),
        in_specs=[pl.BlockSpec((L, 2 * W), lambda b, c: (fw(b, c), qk_blk)),
                  pl.BlockSpec((L, 2 * W), lambda b, c: (bw(b, c), qk_blk)),
                  pl.BlockSpec((K, W, MLSTM_CHUNK), lambda b, c: (fw(b, c), 0, 0)),
                  pl.BlockSpec((K, W, MLSTM_CHUNK), lambda b, c: (bw(b, c), 0, 0)),
                  pl.BlockSpec((L, W), lambda b, c: (fw(b, c), vblk)),
                  pl.BlockSpec((L, W), lambda b, c: (bw(b, c), vblk)),
                  pl.BlockSpec((L, LANE), lambda b, c: (fw(b, c), gblk)),
                  pl.BlockSpec((L, LANE), lambda b, c: (bw(b, c), gblk)),
                  pl.BlockSpec((K, MLSTM_GATES, MLSTM_CHUNK), lambda b, c: (fw(b, c), 0, 0)),
                  pl.BlockSpec((K, MLSTM_GATES, MLSTM_CHUNK), lambda b, c: (bw(b, c), 0, 0)),
                  _resident((1, LANE)),
                  _resident((MLSTM_GATES, 1)),
                  pl.BlockSpec((1, nst, dh, dh), lambda b, c: (b, 0, 0, 0)),
                  pl.BlockSpec((1, nst, dh), lambda b, c: (b, 0, 0)),
                  pl.BlockSpec((1, nst, LANE), lambda b, c: (b, 0, 0))],
        out_specs=[pl.BlockSpec((L, W), lambda b, c: (fw(b, c), 0)),
                   pl.BlockSpec((L, W), lambda b, c: (bw(b, c), 0)),
                   pl.BlockSpec((1, nst, dh, dh), lambda b, c: (b, 0, 0, 0)),
                   pl.BlockSpec((1, nst, dh), lambda b, c: (b, 0, 0)),
                   pl.BlockSpec((1, nst, LANE), lambda b, c: (b, 0, 0))],
        out_shape=[jax.ShapeDtypeStruct((n, W), F32), jax.ShapeDtypeStruct((n, W), F32),
                   jax.ShapeDtypeStruct((batch, nst, dh, dh), F32),
                   jax.ShapeDtypeStruct((batch, nst, dh), F32),
                   jax.ShapeDtypeStruct((batch, nst, LANE), F32)],
        scratch_shapes=[pltpu.VMEM((nst, dh, dh), F32), pltpu.VMEM((nst, dh), F32),
                        pltpu.VMEM((nst, LANE), F32)],
        compiler_params=_params(("arbitrary", "arbitrary")),
        name="mlstm_scan",
    )(qk, qk, kt, kt, z, z, z, z, gt, gt, gate_bc, gate_br, c0, n0, m0)


def _grid_conv(above, cur, below, cw_ref, width):
    T = cur.shape[0]
    E = T + 2 * width
    ext = jnp.concatenate([above, cur, below], axis=0)
    colp = jnp.bitwise_and(lax.broadcasted_iota(jnp.int32, (E, 1), 0), width - 1)
    left = jnp.where(colp == 0, 0.0, pltpu.roll(ext, 1, 0))
    right = jnp.where(colp == width - 1, 0.0, pltpu.roll(ext, E - 1, 0))

    def tap_row(i):
        rows = slice(i * width, i * width + T)
        return (cw_ref[3 * i:3 * i + 1, :] * left[rows] + cw_ref[3 * i + 1:3 * i + 2, :] * ext[rows]
                + cw_ref[3 * i + 2:3 * i + 3, :] * right[rows])

    return tap_row(0) + tap_row(1) + tap_row(2)


def _merge_kernel(conv, x_ref, mod_ref, ysf_ref, ysb_ref, bonus_ref, gate_ref, hf_ref, hb_ref, zo_ref,
                  zs_ref, lnxg_ref, lnxb_ref, gng_ref, pmean_ref, wbr_ref, wbm_ref, wout_ref, ng_ref, wup_ref,
                  cw_ref, cb_ref, x1_ref, act_ref, *scratch):
    if conv[0] == "grid":
        @pl.when(pl.program_id(0) == 0)
        def _():
            for ref in scratch:
                ref[...] = jnp.zeros(ref.shape, F32)

    mod = mod_ref[0]
    g1 = mod[:, 2 * D_MODEL:3 * D_MODEL]
    sh2 = mod[:, 3 * D_MODEL:4 * D_MODEL]
    sc2 = mod[:, 4 * D_MODEL:5 * D_MODEL]

    ys = ysf_ref[...] + ysb_ref[...]
    pmean = pmean_ref[...]
    mean = _mm_exact_rhs(ys, pmean, _NN)
    cen = ys - mean
    var = _mm_exact_rhs(cen * cen, pmean, _NN)
    y_r = (cen * lax.rsqrt(var + RWKV_GN_EPS) * lnxg_ref[...] + lnxb_ref[...] + bonus_ref[...]) * gate_ref[...]

    hs = hf_ref[...] + hb_ref[...]
    parts = []
    for h in range(MLSTM_HEADS):
        hh = hs[:, h * MLSTM_HEAD_DIM:(h + 1) * MLSTM_HEAD_DIM]
        mu = jnp.mean(hh, axis=-1, keepdims=True)
        ce = hh - mu
        va = jnp.mean(ce * ce, axis=-1, keepdims=True)
        parts.append(ce * lax.rsqrt(va + MLSTM_GN_EPS))
    y_m = jnp.concatenate(parts, axis=1) * gng_ref[...] * _sigmoid(zo_ref[...])

    gates = zs_ref[...]
    merged = (gates[:, 0:D_MODEL] * _dg(y_r.astype(BF16), wbr_ref[...], _NN)
              + gates[:, D_MODEL:2 * D_MODEL] * _dg(y_m.astype(BF16), wbm_ref[...], _NN))
    t = _dg(merged.astype(BF16), wout_ref[...], _NN)
    x1 = x_ref[...] + g1 * _rms(t, ng_ref[1:2, :])
    x1_ref[...] = x1
    h2 = _rms(x1, ng_ref[2:3, :]) * (1.0 + sc2) + sh2
    u = _dg(h2.astype(BF16), wup_ref[...], _NN)
    if conv[0] == "seq":
        pre = _dwconv(u[:, 0:D_FF], cw_ref, conv[1], False) + cb_ref[...]
        act_ref[...] = (_silu(pre) * u[:, D_FF:2 * D_FF]).astype(BF16)
    else:
        _, width, tiles_per_image = conv
        act_scr, val_scr, tail_scr = scratch
        T = act_scr.shape[0]
        step = pl.program_id(0)
        pos = (step + tiles_per_image - 1) % tiles_per_image
        for c0 in range(0, D_FF, CONV_CH_TILE):
            ch = slice(c0, c0 + CONV_CH_TILE)
            above = jnp.where(pos != 0, tail_scr[:, ch], 0.0)
            below = jnp.where(pos != tiles_per_image - 1, u[0:width, ch], 0.0)
            pre = _grid_conv(above, act_scr[:, ch], below, cw_ref.at[:, ch], width) + cb_ref[:, ch]
            act_ref[:, ch] = (_silu(pre) * val_scr[:, ch]).astype(BF16)
        tail_scr[...] = act_scr[T - width:T, :]
        act_scr[...] = u[:, 0:D_FF]
        val_scr[...] = u[:, D_FF:2 * D_FF]


def _merge(x2, mod, mod_row, z, ysf, ysb, bonus, gate, hf, hb, p, conv):
    n = x2.shape[0]
    W = RWKV_WIDTH
    rows = MERGE_TILE
    ntiles = n // rows
    delayed = conv[0] == "grid"
    cur = (lambda i: jnp.minimum(i, ntiles - 1)) if delayed else (lambda i: i)
    tile = lambda w: pl.BlockSpec((rows, w), lambda i: (cur(i), 0))
    act_spec = pl.BlockSpec((rows, D_FF), (lambda i: (jnp.maximum(i - 1, 0), 0)) if delayed else (lambda i: (i, 0)))
    scratch = ([pltpu.VMEM((rows, D_FF), F32), pltpu.VMEM((rows, D_FF), F32), pltpu.VMEM((conv[1], D_FF), F32)]
               if delayed else [])
    return pl.pallas_call(
        functools.partial(_merge_kernel, conv),
        grid=(ntiles + 1 if delayed else ntiles,),
        in_specs=[tile(D_MODEL),
                  pl.BlockSpec((1, 1, 6 * D_MODEL), lambda i: (mod_row(cur(i) * rows), 0, 0)),
                  tile(W), tile(W), tile(W), tile(W), tile(MLSTM_WIDTH), tile(MLSTM_WIDTH),
                  pl.BlockSpec((rows, MLSTM_WIDTH),
                               lambda i: (cur(i), (ZM_OFF + 3 * MLSTM_WIDTH) // MLSTM_WIDTH)),
                  pl.BlockSpec((rows, GATE_COLS), lambda i: (cur(i), ZS_OFF // GATE_COLS)),
                  _resident((1, W)), _resident((1, W)), _resident((1, MLSTM_WIDTH)),
                  _resident((W, W)),
                  _resident((W, D_MODEL)), _resident((MLSTM_WIDTH, D_MODEL)),
                  _resident((D_MODEL, D_MODEL)), _resident((4, D_MODEL)),
                  _resident((D_MODEL, 2 * D_FF)), _resident((9, D_FF)), _resident((1, D_FF))],
        out_specs=[tile(D_MODEL), act_spec],
        out_shape=[jax.ShapeDtypeStruct((n, D_MODEL), F32), jax.ShapeDtypeStruct((n, D_FF), BF16)],
        scratch_shapes=scratch,
        compiler_params=_params(("arbitrary",)),
        name="merge_ffn_up",
    )(x2, mod, ysf, ysb, bonus, gate, hf, hb, z, z, p["lnx_g"], p["lnx_b"], p["gn_g"], p["pmean"],
      p["w_br"], p["w_bm"], p["w_out"], p["norm_g"], p["ffn_up"], p["ffn_conv"], p["ffn_conv_b"])


def _down_kernel(x1_ref, mod_ref, a_ref, w_ref, ng_ref, o_ref):
    g2 = mod_ref[0][:, 5 * D_MODEL:6 * D_MODEL]
    f = _dg(a_ref[...], w_ref[...], _NN)
    o_ref[...] = x1_ref[...] + g2 * _rms(f, ng_ref[3:4, :])


def _down(x1, mod, mod_row, act, p):
    n = x1.shape[0]
    tile = DOWN_TILE
    return pl.pallas_call(
        _down_kernel,
        grid=(n // tile,),
        in_specs=[pl.BlockSpec((tile, D_MODEL), lambda i: (i, 0)),
                  pl.BlockSpec((1, 1, 6 * D_MODEL), lambda i: (mod_row(i * tile), 0, 0)),
                  pl.BlockSpec((tile, D_FF), lambda i: (i, 0)),
                  _resident((D_FF, D_MODEL)), _resident((4, D_MODEL))],
        out_specs=pl.BlockSpec((tile, D_MODEL), lambda i: (i, 0)),
        out_shape=jax.ShapeDtypeStruct((n, D_MODEL), F32),
        compiler_params=_params(("arbitrary",)),
        name="ffn_down",
    )(x1, mod, act, p["ffn_down"], p["norm_g"])


RWKV_LOCAL_PASSES = 1


def _state_to_pairs(s):
    b = s.shape[0]
    s = s.reshape(b, N_DIR, RWKV_PAIRS, 2, RWKV_HEAD_DIM, RWKV_HEAD_DIM)
    zero = jnp.zeros_like(s[:, :, :, 0])
    top = jnp.concatenate([s[:, :, :, 0], zero], axis=-1)
    bot = jnp.concatenate([zero, s[:, :, :, 1]], axis=-1)
    return jnp.concatenate([top, bot], axis=-2)


def _state_from_pairs(sb):
    b = sb.shape[0]
    n = RWKV_HEAD_DIM
    parts = jnp.stack([sb[..., 0:n, 0:n], sb[..., n:2 * n, n:2 * n]], axis=3)
    return parts.reshape(b, N_DIR, RWKV_HEADS, n, n)


def _trunk(x, mod, mod_row, rows, states, p):
    batch, seq_len, _ = x.shape
    n = batch * seq_len
    x2 = x.reshape(n, D_MODEL)
    s0, c0, n0, m0 = states

    fuse_width = seq_len if (rows == 1 and IN_TILE % seq_len == 0 and MERGE_TILE % seq_len == 0) else None
    proj = _in_proj(x2, mod, mod_row, p["norm_g"][0:1], p["w_r"], p["w_g"], p["w_m"], p["w_s"],
                    p["mlstm_conv"], fuse_width)
    z = proj[0]

    rp, y0, gm, hm, gate, bonus = _rwkv_local(z, seq_len, p, RWKV_LOCAL_PASSES)
    ysf, ysb, s_fin = _rwkv_scan(_state_to_pairs(s0), rp, y0, gm, hm, batch, seq_len)
    s_fin = _state_from_pairs(s_fin)

    if fuse_width is None:
        z, gt = proj
        qk, kt = _qk_conv(z, batch, seq_len, rows, p["mlstm_conv"])
        qk_blk = 0
    else:
        z, gt, kt = proj
        qk, qk_blk = z, ZM_OFF // (2 * MLSTM_WIDTH)
    nst = N_DIR * MLSTM_HEADS
    hf, hb, ct_fin, n_fin, m_fin = _mlstm_scan(
        z, qk, qk_blk, kt, gt, p["gate_bc"], p["gate_br"],
        jnp.swapaxes(c0, -1, -2).reshape(batch, nst, MLSTM_HEAD_DIM, MLSTM_HEAD_DIM),
        n0.reshape(batch, nst, MLSTM_HEAD_DIM),
        jnp.broadcast_to(m0.reshape(batch, nst, 1), (batch, nst, LANE)), batch, seq_len)
    c_fin = jnp.swapaxes(ct_fin, -1, -2)

    if fuse_width is not None:
        ffn_conv = ("seq", fuse_width)
    else:
        width = seq_len // rows
        assert rows > 1 and width & (width - 1) == 0 and MERGE_TILE % width == 0 and seq_len % MERGE_TILE == 0
        ffn_conv = ("grid", width, seq_len // MERGE_TILE)
    x1, act = _merge(x2, mod, mod_row, z, ysf, ysb, bonus, gate, hf, hb, p, ffn_conv)
    out = _down(x1, mod, mod_row, act, p)

    new_states = (s_fin,
                  c_fin.reshape(batch, N_DIR, MLSTM_HEADS, MLSTM_HEAD_DIM, MLSTM_HEAD_DIM),
                  n_fin.reshape(batch, N_DIR, MLSTM_HEADS, MLSTM_HEAD_DIM),
                  m_fin[:, :, 0].reshape(batch, N_DIR, MLSTM_HEADS))
    return out.reshape(batch, seq_len, D_MODEL), new_states


def _pack_layer(l, ada_w, ada_b, norm_g, w_in, rwkv_mu, rwkv_w0, rwkv_w_up, rwkv_a0, rwkv_a_up,
                rwkv_g_up, rwkv_kk_scale, rwkv_k_a, rwkv_r_k, rwkv_lnx_g, rwkv_lnx_b, mlstm_conv,
                mlstm_gate_b, mlstm_gn_g, w_branch_rwkv, w_branch_mlstm, w_out, ffn_up, ffn_conv,
                ffn_conv_b, ffn_down):
    W = RWKV_WIDTH
    wi = w_in[l]
    w_r = wi[:, 0:RWKV_COLS].astype(BF16)
    w_m = wi[:, RWKV_COLS:RWKV_COLS + 4 * MLSTM_WIDTH].astype(BF16)
    w_g = jnp.pad(wi[:, RWKV_COLS + 4 * MLSTM_WIDTH:RWKV_COLS + MLSTM_COLS].astype(BF16),
                  ((0, 0), (0, LANE - MLSTM_GATES)))
    w_s = wi[:, RWKV_COLS + MLSTM_COLS:].astype(BF16)

    head = jnp.arange(W, dtype=jnp.int32) // RWKV_HEAD_DIM
    same = (head[:, None] == head[None, :])
    gb = mlstm_gate_b[l].reshape(1, MLSTM_GATES)
    return dict(
        ada_w=ada_w[l], ada_b=ada_b[l], norm_g=norm_g[l], w_r=w_r, w_g=w_g, w_m=w_m, w_s=w_s,
        mu=rwkv_mu[l].reshape(1, RWKV_COLS),
        w0=rwkv_w0[l].reshape(N_DIR, 1, W), w_up=rwkv_w_up[l],
        a0=rwkv_a0[l].reshape(N_DIR, 1, W), a_up=rwkv_a_up[l], g_up=rwkv_g_up[l],
        kk_scale=rwkv_kk_scale[l].reshape(1, W), k_a=rwkv_k_a[l].reshape(1, W),
        r_k=rwkv_r_k[l].reshape(1, W),
        lnx_g=rwkv_lnx_g[l].reshape(1, W), lnx_b=rwkv_lnx_b[l].reshape(1, W),
        pones=same.astype(BF16), pmean=(same.astype(F32) / RWKV_HEAD_DIM).astype(BF16),
        mlstm_conv=mlstm_conv[l].reshape(9, 2 * MLSTM_WIDTH),
        gate_bc=jnp.pad(gb, ((0, 0), (0, LANE - MLSTM_GATES))), gate_br=gb.reshape(MLSTM_GATES, 1),
        gn_g=mlstm_gn_g[l].reshape(1, MLSTM_WIDTH),
        w_br=w_branch_rwkv[l].astype(BF16), w_bm=w_branch_mlstm[l].astype(BF16),
        w_out=w_out[l].astype(BF16), ffn_up=ffn_up[l].astype(BF16),
        ffn_conv=ffn_conv[l].reshape(9, D_FF), ffn_conv_b=ffn_conv_b[l].reshape(1, D_FF),
        ffn_down=ffn_down[l].astype(BF16),
    )


def kernel(x_prompt, x_sample, c, state_rwkv, state_mlstm_C, state_mlstm_n, state_mlstm_m, c_ctx,
           ada_w, ada_b, norm_g, w_in, rwkv_mu, rwkv_w0, rwkv_w_up, rwkv_a0, rwkv_a_up, rwkv_g_up,
           rwkv_kk_scale, rwkv_k_a, rwkv_r_k, rwkv_lnx_g, rwkv_lnx_b, mlstm_conv, mlstm_gate_b,
           mlstm_gn_g, w_branch_rwkv, w_branch_mlstm, w_out, ffn_up, ffn_conv, ffn_conv_b, ffn_down):
    depth = ada_w.shape[0]
    batch = x_prompt.shape[0]
    dec_batch, dec_seq, _ = x_sample.shape
    latent_rows = dec_seq // GRID_W
    ctx_init = (jnp.zeros((batch, N_DIR, RWKV_HEADS, RWKV_HEAD_DIM, RWKV_HEAD_DIM), F32),
                jnp.zeros((batch, N_DIR, MLSTM_HEADS, MLSTM_HEAD_DIM, MLSTM_HEAD_DIM), F32),
                jnp.zeros((batch, N_DIR, MLSTM_HEADS, MLSTM_HEAD_DIM), F32),
                jnp.zeros((batch, N_DIR, MLSTM_HEADS), F32))
    cond = jnp.concatenate([c_ctx[None, :], c, jnp.zeros((8 - 1 - dec_batch, D_MODEL), F32)], axis=0)

    xp, xs = x_prompt, x_sample
    new_s, new_c, new_n, new_m = [], [], [], []
    for l in range(depth):
        p = _pack_layer(l, ada_w, ada_b, norm_g, w_in, rwkv_mu, rwkv_w0, rwkv_w_up, rwkv_a0, rwkv_a_up,
                        rwkv_g_up, rwkv_kk_scale, rwkv_k_a, rwkv_r_k, rwkv_lnx_g, rwkv_lnx_b, mlstm_conv,
                        mlstm_gate_b, mlstm_gn_g, w_branch_rwkv, w_branch_mlstm, w_out, ffn_up, ffn_conv,
                        ffn_conv_b, ffn_down)
        mod = _ada(cond, p["ada_w"], p["ada_b"]).reshape(8, 1, 6 * D_MODEL)
        xp, (s, cc, nn, mm) = _trunk(xp, mod, lambda r: 0, 1, ctx_init, p)
        new_s.append(s)
        new_c.append(cc)
        new_n.append(nn)
        new_m.append(mm)
        xs, _ = _trunk(xs, mod, lambda r: 1 + r // dec_seq, latent_rows,
                       (state_rwkv[:, l], state_mlstm_C[:, l], state_mlstm_n[:, l], state_mlstm_m[:, l]), p)
    return (xp, xs, jnp.stack(new_s, axis=1), jnp.stack(new_c, axis=1),
            jnp.stack(new_n, axis=1), jnp.stack(new_m, axis=1))
```

```python
import functools

import jax
import jax.numpy as jnp
from jax import lax
from jax.experimental import pallas as pl
from jax.experimental.pallas import tpu as pltpu

F32 = jnp.float32
BF16 = jnp.bfloat16

D_MODEL = 1024
N_DIR = 2
RWKV_HEADS = 8
RWKV_HEAD_DIM = 64
RWKV_WIDTH = RWKV_HEADS * RWKV_HEAD_DIM
DECAY_LORA = 64
ICLR_LORA = 64
GATE_LORA = 128
MLSTM_HEADS = 4
MLSTM_HEAD_DIM = 128
MLSTM_WIDTH = MLSTM_HEADS * MLSTM_HEAD_DIM
MLSTM_CHUNK = 64
D_FF = 2816
GRID_W = 64
RMS_EPS = 1e-6
RWKV_GN_EPS = 64e-5
MLSTM_GN_EPS = 1e-5
DECAY_SCALE = 0.606531

RWKV_COLS = 3 * RWKV_WIDTH + N_DIR * DECAY_LORA + N_DIR * ICLR_LORA + GATE_LORA
MLSTM_GATES = 2 * N_DIR * MLSTM_HEADS
MLSTM_COLS = 4 * MLSTM_WIDTH + MLSTM_GATES
GATE_COLS = 2 * D_MODEL

LANE = 128
ZR_BLOCK = 2048
ZG_OFF = RWKV_COLS
ZM_OFF = ZR_BLOCK
ZS_OFF = ZM_OFF + 4 * MLSTM_WIDTH
Z_COLS = ZS_OFF + GATE_COLS
W_IN_PAD = Z_COLS

IN_TILE = 512
MERGE_TILE = 256
DOWN_TILE = 512
RCHUNK = 64
CONV_CH_TILE = 256
VMEM_LIMIT = 56 * 1024 * 1024


def _params(sem):
    return pltpu.CompilerParams(dimension_semantics=sem, vmem_limit_bytes=VMEM_LIMIT)


def _resident(shape):
    nd = len(shape)
    return pl.BlockSpec(shape, lambda *_: (0,) * nd, pipeline_mode=pl.Buffered(1))


def _split2(a):
    hi = a.astype(BF16)
    lo = (a - hi.astype(F32)).astype(BF16)
    return hi, lo


def _split3(a):
    hi = a.astype(BF16)
    r1 = a - hi.astype(F32)
    mid = r1.astype(BF16)
    lo = (r1 - mid.astype(F32)).astype(BF16)
    return hi, mid, lo


def _dg(a, b, dims):
    return lax.dot_general(a, b, dims, preferred_element_type=F32)


def _mm(a, b, dims, passes):
    if passes == 1:
        return _dg(a.astype(BF16), b.astype(BF16), dims)
    ah, al = _split2(a)
    bh, bl = _split2(b)
    return _dg(ah, bh, dims) + (_dg(ah, bl, dims) + _dg(al, bh, dims))


def _mm_exact_lhs(a_bf16, b, dims):
    b1, b2, b3 = _split3(b)
    return _dg(a_bf16, b1, dims) + (_dg(a_bf16, b2, dims) + _dg(a_bf16, b3, dims))


def _mm_exact_rhs(a, b_bf16, dims, pieces=3):
    if pieces == 2:
        a1, a2 = _split2(a)
        return _dg(a1, b_bf16, dims) + _dg(a2, b_bf16, dims)
    a1, a2, a3 = _split3(a)
    return _dg(a1, b_bf16, dims) + (_dg(a2, b_bf16, dims) + _dg(a3, b_bf16, dims))


_NN = (((1,), (0,)), ((), ()))
_NT = (((1,), (1,)), ((), ()))
_TN = (((0,), (0,)), ((), ()))
_BNN = (((2,), (1,)), ((0,), (0,)))
_BNT = (((2,), (2,)), ((0,), (0,)))
_BTN = (((1,), (1,)), ((0,), (0,)))


def _sigmoid(x):
    return jax.nn.sigmoid(x)


def _silu(x):
    return x * jax.nn.sigmoid(x)


def _rms(x, g):
    return x * lax.rsqrt(jnp.mean(x * x, axis=-1, keepdims=True) + RMS_EPS) * g


def _ada_kernel(cond_ref, w_ref, b_ref, o_ref):
    s = _silu(cond_ref[...])
    o_ref[...] = _dg(s.astype(BF16), w_ref[...].astype(BF16), _NN) + b_ref[...]


def _ada(cond8, ada_w, ada_b):
    n = ada_w.shape[1]
    tn = 1536
    return pl.pallas_call(
        _ada_kernel,
        grid=(n // tn,),
        in_specs=[_resident((8, D_MODEL)),
                  pl.BlockSpec((D_MODEL, tn), lambda j: (0, j)),
                  pl.BlockSpec((1, tn), lambda j: (0, j))],
        out_specs=pl.BlockSpec((8, tn), lambda j: (0, j)),
        out_shape=jax.ShapeDtypeStruct((8, n), F32),
        compiler_params=_params(("arbitrary",)),
        name="ada_mod",
    )(cond8, ada_w, ada_b.reshape(1, n))


def _in_kernel(conv_width, x_ref, mod_ref, g_ref, w_ref, cw_ref, z_ref, gt_ref, *kt_ref):
    mod = mod_ref[0]
    sh = mod[:, 0:D_MODEL]
    sc = mod[:, D_MODEL:2 * D_MODEL]
    h = (_rms(x_ref[...], g_ref[...]) * (1.0 + sc) + sh).astype(BF16)
    m_off = RWKV_COLS
    t_off = RWKV_COLS + 4 * MLSTM_WIDTH
    z_ref[:, 0:ZG_OFF] = _dg(h, w_ref[:, 0:m_off], _NN)
    tail = _dg(h, w_ref[:, t_off:W_IN_PAD], _NN)
    zg = tail[:, 0:LANE]
    z_ref[:, ZG_OFF:ZM_OFF] = zg
    zgt = zg.T
    for c in range(zg.shape[0] // MLSTM_CHUNK):
        gt_ref[c] = zgt[0:MLSTM_GATES, c * MLSTM_CHUNK:(c + 1) * MLSTM_CHUNK]
    z_ref[:, ZS_OFF:Z_COLS] = _sigmoid(tail[:, MLSTM_GATES:MLSTM_GATES + GATE_COLS])
    zm = _dg(h, w_ref[:, m_off:t_off], _NN)
    if conv_width is None:
        z_ref[:, ZM_OFF:ZS_OFF] = zm
    else:
        qk_cols = 2 * MLSTM_WIDTH
        qk = _silu(_dwconv(zm[:, 0:qk_cols], cw_ref, conv_width, False))
        z_ref[:, ZM_OFF:ZM_OFF + qk_cols] = qk
        z_ref[:, ZM_OFF + qk_cols:ZS_OFF] = zm[:, qk_cols:]
        _store_time_on_lanes(qk[:, MLSTM_WIDTH:qk_cols], kt_ref[0], MLSTM_CHUNK)


def _in_proj(x2, mod, mod_row, norm_g0, w_in, conv_w9, conv_width):
    n = x2.shape[0]
    tile = IN_TILE
    cpt = tile // MLSTM_CHUNK
    nchunk = n // MLSTM_CHUNK
    out_specs = [pl.BlockSpec((tile, Z_COLS), lambda i: (i, 0)),
                 pl.BlockSpec((cpt, MLSTM_GATES, MLSTM_CHUNK), lambda i: (i, 0, 0))]
    out_shape = [jax.ShapeDtypeStruct((n, Z_COLS), F32),
                 jax.ShapeDtypeStruct((nchunk, MLSTM_GATES, MLSTM_CHUNK), F32)]
    if conv_width is not None:
        out_specs.append(pl.BlockSpec((cpt, MLSTM_WIDTH, MLSTM_CHUNK), lambda i: (i, 0, 0)))
        out_shape.append(jax.ShapeDtypeStruct((nchunk, MLSTM_WIDTH, MLSTM_CHUNK), F32))
    return pl.pallas_call(
        functools.partial(_in_kernel, conv_width),
        grid=(n // tile,),
        in_specs=[pl.BlockSpec((tile, D_MODEL), lambda i: (i, 0)),
                  pl.BlockSpec((1, 1, 6 * D_MODEL), lambda i: (mod_row(i * tile), 0, 0)),
                  _resident((1, D_MODEL)), _resident(w_in.shape), _resident(conv_w9.shape)],
        out_specs=out_specs,
        out_shape=out_shape,
        compiler_params=_params(("arbitrary",)),
        name="in_proj",
    )(x2, mod, norm_g0, w_in, conv_w9)


LOCAL_CHUNKS = 4
PAIR_LANES = 2 * RWKV_HEAD_DIM
RWKV_PAIRS = RWKV_HEADS // 2


def _bd(x):
    lane = lax.broadcasted_iota(jnp.int32, x.shape, 1)
    left = lane < RWKV_HEAD_DIM
    return jnp.concatenate([jnp.where(left, x, 0.0), jnp.where(left, 0.0, x)], axis=0)


def _rwkv_local_kernel(chunks_per_seq, passes,
                       z_ref, zp_ref, zn_ref, mu_ref, w0_ref, wup_ref, a0_ref, aup_ref, gup_ref,
                       kks_ref, ka_ref, rk_ref, pones_ref,
                       rp_ref, y0_ref, gm_ref, hm_ref, gate_ref, bonus_ref):
    C = RCHUNK
    W = RWKV_WIDTH
    NS = LOCAL_CHUNKS
    R = NS * C
    first = (pl.program_id(0) * NS) % chunks_per_seq
    has_prev = first != 0
    has_next = first + NS != chunks_per_seq

    z = z_ref[:, 0:RWKV_COLS]
    zp = jnp.where(has_prev, zp_ref[7:8, 0:RWKV_COLS], 0.0)
    zn = jnp.where(has_next, zn_ref[0:1, 0:RWKV_COLS], 0.0)
    trow = lax.broadcasted_iota(jnp.int32, (R, 1), 0)
    prev = jnp.where(trow == 0, zp, pltpu.roll(z, 1, 0))
    nxt = jnp.where(trow == R - 1, zn, pltpu.roll(z, R - 1, 0))
    zs = z + mu_ref[...] * (0.5 * (prev + nxt) - z)

    r = zs[:, 0:W]
    k = zs[:, W:2 * W]
    v = zs[:, 2 * W:3 * W]
    gd = zs[:, 3 * W + 2 * DECAY_LORA + 2 * ICLR_LORA:RWKV_COLS]
    gate_ref[...] = _dg(_sigmoid(gd).astype(BF16), gup_ref[...].astype(BF16), _NN)

    pones = pones_ref[...]
    kks = k * kks_ref[...]
    norm = jnp.sqrt(_mm_exact_rhs(kks * kks, pones, _NN, pieces=2))
    kk = kks / jnp.maximum(norm, 1e-12)

    P = PAIR_LANES
    row = lax.broadcasted_iota(jnp.int32, (R, R), 0)
    col = lax.broadcasted_iota(jnp.int32, (R, R), 1)
    same_chunk = jnp.bitwise_and(row, -C) == jnp.bitwise_and(col, -C)
    prow = lax.broadcasted_iota(jnp.int32, (C, P), 0)
    pcol = jnp.bitwise_and(lax.broadcasted_iota(jnp.int32, (C, P), 1), RWKV_HEAD_DIM - 1)
    eye_p = jnp.where(prow == pcol, 1.0, 0.0)
    left_head = lax.broadcasted_iota(jnp.int32, (C, P), 1) < RWKV_HEAD_DIM

    def diag_blocks(m):
        return jnp.where(left_head, m[0:RWKV_HEAD_DIM], m[RWKV_HEAD_DIM:P])

    abar, rbar, kt, bt, kw, bw, wc, strict, incl = [], [], [], [], [], [], [], [], []
    kd_sum = None
    for d in range(N_DIR):
        o = 3 * W + d * DECAY_LORA
        wd = zs[:, o:o + DECAY_LORA]
        o = 3 * W + 2 * DECAY_LORA + d * ICLR_LORA
        ad = zs[:, o:o + ICLR_LORA]
        logw = -DECAY_SCALE * _sigmoid(w0_ref[d] + _dg(jnp.tanh(wd).astype(BF16), wup_ref[d].astype(BF16), _NN))
        a = _sigmoid(a0_ref[d] + _dg(ad.astype(BF16), aup_ref[d].astype(BF16), _NN))
        kd = k * (1.0 + (a - 1.0) * ka_ref[...])
        b = kk * a
        kd_sum = kd if kd_sum is None else kd_sum + kd

        earlier_or_same = same_chunk & ((row >= col) if d == 0 else (row <= col))
        cum_i = _mm_exact_lhs(jnp.where(earlier_or_same, 1.0, 0.0).astype(BF16), logw, _NN)
        cum_e = cum_i - logw
        ab_d, rb_d, kt_d, bt_d, kw_d, bw_d, wc_d = [], [], [], [], [], [], []
        for s in range(NS):
            rs = slice(s * C, (s + 1) * C)
            ci_s = cum_i[rs]
            ctot = jnp.sum(logw[rs], axis=0, keepdims=True)
            e_ni = jnp.exp(-ci_s)
            e_ti = jnp.exp(ctot - ci_s)
            ab_d.append(kk[rs] * jnp.exp(cum_e[rs]))
            rb_d.append(r[rs] * jnp.exp(ci_s))
            kt_d.append(kd[rs] * e_ni)
            bt_d.append(b[rs] * e_ni)
            kw_d.append(kd[rs] * e_ti)
            bw_d.append(b[rs] * e_ti)
            wc_d.append(jnp.exp(ctot))
        abar.append(ab_d)
        rbar.append(rb_d)
        kt.append(kt_d)
        bt.append(bt_d)
        kw.append(kw_d)
        bw.append(bw_d)
        wc.append(wc_d)
        strict.append((prow > pcol) if d == 0 else (prow < pcol))
        incl.append((prow >= pcol) if d == 0 else (prow <= pcol))
    bonus_ref[...] = _mm_exact_rhs(r * kd_sum * rk_ref[...], pones, _NN, pieces=2) * v

    mm = functools.partial(_mm, passes=passes)
    chains = [(s, d, p) for s in range(NS) for d in range(N_DIR) for p in range(RWKV_PAIRS)]
    nch = range(len(chains))

    def sel(arr, i):
        s, d, p = chains[i]
        return arr[d][s][:, p * P:(p + 1) * P]

    cat0 = lambda a_, b_: jnp.concatenate([a_, b_], axis=0)
    cat1 = lambda a_, b_: jnp.concatenate([a_, b_], axis=1)
    vsl = [v[s * C:(s + 1) * C, p * P:(p + 1) * P] for s, _, p in chains]
    lhs = [cat0(sel(abar, i), sel(rbar, i)) for i in nch]
    by_b = [mm(lhs[i], _bd(sel(bt, i)), _NT) for i in nch]
    by_k = [mm(lhs[i], _bd(sel(kt, i)), _NT) for i in nch]
    a_kk = [jnp.where(strict[chains[i][1]], by_b[i][0:C], 0.0) for i in nch]
    a_rb = [jnp.where(incl[chains[i][1]], by_b[i][C:2 * C], 0.0) for i in nch]
    a_kv = [jnp.where(strict[chains[i][1]], by_k[i][0:C], 0.0) for i in nch]
    a_rk = [jnp.where(incl[chains[i][1]], by_k[i][C:2 * C], 0.0) for i in nch]
    on_v = [mm(cat0(a_kv[i], a_rk[i]), _bd(vsl[i]), _NN) for i in nch]

    x = [-m for m in a_kk]
    tinv = [eye_p + m for m in x]
    x = [mm(m, _bd(m), _NN) for m in x]
    for _ in range(4):
        both = [mm(cat0(tinv[i], x[i]), _bd(x[i]), _NN) for i in nch]
        tinv = [tinv[i] + both[i][0:C] for i in nch]
        x = [m[C:2 * C] for m in both]
    tinv = [tinv[i] + mm(tinv[i], _bd(x[i]), _NN) for i in nch]

    solved = [mm(tinv[i], cat1(_bd(sel(abar, i)), _bd(on_v[i][0:C])), _NN) for i in nch]
    ap = [m[:, 0:P] for m in solved]
    u0 = [m[:, P:2 * P] for m in solved]
    corr = [mm(a_rb[i], cat1(_bd(ap[i]), _bd(u0[i])), _NN) for i in nch]
    on_b = [mm(cat1(ap[i], u0[i]), sel(bw, i), _TN) for i in nch]
    vk = [mm(vsl[i], sel(kw, i), _TN) for i in nch]
    for i in nch:
        s, d, p = chains[i]
        rows = slice(s * C, (s + 1) * C)
        lanes = slice(p * P, (p + 1) * P)
        rp_ref[d, rows, lanes] = (sel(rbar, i) - corr[i][:, 0:P]).astype(BF16)
        y0_ref[d, rows, lanes] = on_v[i][C:2 * C] - corr[i][:, P:2 * P]
        gm_ref[d, s, p] = (eye_p * sel(wc, i) - diag_blocks(on_b[i][0:P])).astype(BF16)
        hm_ref[d, s, p] = diag_blocks(vk[i] - on_b[i][P:2 * P])


def _rwkv_local(z, seq_len, p, passes):
    n = z.shape[0]
    nchunk = n // RCHUNK
    cps = seq_len // RCHUNK
    assert cps % LOCAL_CHUNKS == 0
    W = RWKV_WIDTH
    rows = LOCAL_CHUNKS * RCHUNK
    hb = rows // 8
    last8 = n // 8 - 1
    mat = lambda dt: jax.ShapeDtypeStruct((N_DIR, nchunk, RWKV_PAIRS, RWKV_HEAD_DIM, PAIR_LANES), dt)
    mat_spec = pl.BlockSpec((N_DIR, LOCAL_CHUNKS, RWKV_PAIRS, RWKV_HEAD_DIM, PAIR_LANES),
                            lambda c: (0, c, 0, 0, 0))
    tok = lambda dt: jax.ShapeDtypeStruct((N_DIR, n, W), dt)
    tok_spec = pl.BlockSpec((N_DIR, rows, W), lambda c: (0, c, 0))
    row_spec = pl.BlockSpec((rows, W), lambda c: (c, 0))
    return pl.pallas_call(
        functools.partial(_rwkv_local_kernel, cps, passes),
        grid=(nchunk // LOCAL_CHUNKS,),
        in_specs=[pl.BlockSpec((rows, ZR_BLOCK), lambda c: (c, 0)),
                  pl.BlockSpec((8, ZR_BLOCK), lambda c: (jnp.maximum(c * hb - 1, 0), 0)),
                  pl.BlockSpec((8, ZR_BLOCK), lambda c: (jnp.minimum((c + 1) * hb, last8), 0)),
                  _resident((1, RWKV_COLS)),
                  _resident((N_DIR, 1, W)), _resident((N_DIR, DECAY_LORA, W)),
                  _resident((N_DIR, 1, W)), _resident((N_DIR, ICLR_LORA, W)),
                  _resident((GATE_LORA, W)),
                  _resident((1, W)), _resident((1, W)), _resident((1, W)),
                  _resident((W, W))],
        out_specs=[tok_spec, tok_spec, mat_spec, mat_spec, row_spec, row_spec],
        out_shape=[tok(BF16), tok(F32), mat(BF16), mat(F32),
                   jax.ShapeDtypeStruct((n, W), F32), jax.ShapeDtypeStruct((n, W), F32)],
        compiler_params=_params(("arbitrary",)),
        name="rwkv_local",
    )(z, z, z, p["mu"], p["w0"], p["w_up"], p["a0"], p["a_up"], p["g_up"],
      p["kk_scale"], p["k_a"], p["r_k"], p["pones"])


SCAN_CHUNKS = 4


def _rwkv_scan_kernel(has_init, *refs):
    s0_ref = refs[0] if has_init else None
    (rpf_ref, rpb_ref, y0f_ref, y0b_ref, gmf_ref, gmb_ref, hmf_ref, hmb_ref,
     ysf_ref, ysb_ref, sout_ref, s_scr) = refs[1:] if has_init else refs

    @pl.when(pl.program_id(1) == 0)
    def _():
        s_scr[...] = s0_ref[0] if has_init else jnp.zeros(s_scr.shape, F32)

    K = SCAN_CHUNKS
    C = RCHUNK
    rp_ref, y0_ref, gm_ref, hm_ref, ys_ref = ((rpf_ref, rpb_ref), (y0f_ref, y0b_ref), (gmf_ref, gmb_ref),
                                              (hmf_ref, hmb_ref), (ysf_ref, ysb_ref))
    chains = [(d, p) for d in range(N_DIR) for p in range(RWKV_PAIRS)]
    lanes = [slice(p * PAIR_LANES, (p + 1) * PAIR_LANES) for _, p in chains]
    nch = range(len(chains))
    s = [s_scr[d, p] for d, p in chains]
    for j in range(K):
        at = (j, K - 1 - j)
        rows = [slice(at[d] * C, (at[d] + 1) * C) for d, _ in chains]
        sb = [m.astype(BF16) for m in s]
        y = [_dg(rp_ref[chains[i][0]][0, rows[i], lanes[i]], sb[i], _NT) for i in nch]
        sg = [_dg(sb[i], _bd(gm_ref[chains[i][0]][0, at[chains[i][0]], chains[i][1]]), _NN) for i in nch]
        for i in nch:
            d, p = chains[i]
            ys_ref[d][rows[i], lanes[i]] = y[i] + y0_ref[d][0, rows[i], lanes[i]]
        s = [sg[i] + _bd(hm_ref[chains[i][0]][0, at[chains[i][0]], chains[i][1]]) for i in nch]
    for i in nch:
        d, p = chains[i]
        s_scr[d, p] = s[i]
        sout_ref[0, d, p] = s[i]


def _rwkv_scan(s0, rp, y0, gm, hm, batch, seq_len):
    K = SCAN_CHUNKS
    spb = seq_len // (RCHUNK * K)
    n = batch * seq_len

    def fwd(b, s):
        return b * spb + s

    def bwd(b, s):
        return b * spb + spb - 1 - s

    def mat_spec(d, at):
        return pl.BlockSpec((1, K, RWKV_PAIRS, RWKV_HEAD_DIM, PAIR_LANES), lambda b, s: (d, at(b, s), 0, 0, 0))

    def tok_spec(d, at):
        return pl.BlockSpec((1, K * RCHUNK, RWKV_WIDTH), lambda b, s: (d, at(b, s), 0))

    st_spec = pl.BlockSpec((1, N_DIR, RWKV_PAIRS, PAIR_LANES, PAIR_LANES), lambda b, s: (b, 0, 0, 0, 0))
    ys = jax.ShapeDtypeStruct((n, RWKV_WIDTH), F32)
    has_init = s0 is not None
    return pl.pallas_call(
        functools.partial(_rwkv_scan_kernel, has_init),
        grid=(batch, spb),
        in_specs=([st_spec] if has_init else [])
        + [tok_spec(0, fwd), tok_spec(1, bwd), tok_spec(0, fwd), tok_spec(1, bwd),
           mat_spec(0, fwd), mat_spec(1, bwd), mat_spec(0, fwd), mat_spec(1, bwd)],
        out_specs=[pl.BlockSpec((K * RCHUNK, RWKV_WIDTH), lambda b, s: (fwd(b, s), 0)),
                   pl.BlockSpec((K * RCHUNK, RWKV_WIDTH), lambda b, s: (bwd(b, s), 0)),
                   st_spec],
        out_shape=[ys, ys,
                   jax.ShapeDtypeStruct((batch, N_DIR, RWKV_PAIRS, PAIR_LANES, PAIR_LANES), F32)],
        scratch_shapes=[pltpu.VMEM((N_DIR, RWKV_PAIRS, PAIR_LANES, PAIR_LANES), F32)],
        compiler_params=_params(("arbitrary", "arbitrary")),
        name="rwkv_scan",
    )(*([s0] if has_init else []), rp, rp, y0, y0, gm, gm, hm, hm)


CONV_BLOCK_ROWS = 2048


def _dwconv(x, w_ref, width, vertical):
    T = x.shape[0]
    t = lax.broadcasted_iota(jnp.int32, (T, 1), 0)
    assert width & (width - 1) == 0
    colp = jnp.bitwise_and(t, width - 1)
    xl = jnp.where(colp == 0, 0.0, pltpu.roll(x, 1, 0))
    xr = jnp.where(colp == width - 1, 0.0, pltpu.roll(x, T - 1, 0))

    def tap_row(i):
        return w_ref[3 * i:3 * i + 1, :] * xl + w_ref[3 * i + 1:3 * i + 2, :] * x + w_ref[3 * i + 2:3 * i + 3, :] * xr

    out = tap_row(1)
    if vertical:
        out = out + jnp.where(t < width, 0.0, pltpu.roll(tap_row(0), width, 0))
        out = out + jnp.where(t >= T - width, 0.0, pltpu.roll(tap_row(2), T - width, 0))
    return out


def _conv_geometry(n, seq_len, rows):
    if rows > 1:
        return seq_len, seq_len // rows, True
    block = CONV_BLOCK_ROWS if (n % CONV_BLOCK_ROWS == 0 and CONV_BLOCK_ROWS % seq_len == 0) else seq_len
    return block, seq_len, False


def _store_time_on_lanes(x, out_ref, chunk):
    xt = x.T
    for c in range(x.shape[0] // chunk):
        out_ref[c] = xt[:, c * chunk:(c + 1) * chunk]


def _qk_conv_kernel(width, vertical, first_k_tile, x_ref, w_ref, o_ref, kt_ref):
    out = _silu(_dwconv(x_ref[...], w_ref, width, vertical))
    o_ref[...] = out

    @pl.when(pl.program_id(1) >= first_k_tile)
    def _():
        _store_time_on_lanes(out, kt_ref, MLSTM_CHUNK)


def _qk_conv(z, batch, seq_len, rows, conv_w9):
    n = batch * seq_len
    ch = 2 * MLSTM_WIDTH
    tc = CONV_CH_TILE
    off = ZM_OFF // tc
    first_k = MLSTM_WIDTH // tc
    block, width, vertical = _conv_geometry(n, seq_len, rows)
    cpb = block // MLSTM_CHUNK
    return pl.pallas_call(
        functools.partial(_qk_conv_kernel, width, vertical, first_k),
        grid=(n // block, ch // tc),
        in_specs=[pl.BlockSpec((block, tc), lambda b, j: (b, off + j)),
                  pl.BlockSpec((9, tc), lambda b, j: (0, j))],
        out_specs=[pl.BlockSpec((block, tc), lambda b, j: (b, j)),
                   pl.BlockSpec((cpb, tc, MLSTM_CHUNK), lambda b, j: (b, jnp.maximum(j - first_k, 0), 0))],
        out_shape=[jax.ShapeDtypeStruct((n, ch), F32),
                   jax.ShapeDtypeStruct((n // MLSTM_CHUNK, MLSTM_WIDTH, MLSTM_CHUNK), F32)],
        compiler_params=_params(("arbitrary", "arbitrary")),
        name="mlstm_qk_conv",
    )(z, conv_w9)


MLSTM_STEP_CHUNKS = 4

def _mlstm_scan_kernel(has_init, qkf_ref, qkb_ref, ktf_ref, ktb_ref, vf_ref, vb_ref, gcf_ref, gcb_ref,
                       grf_ref, grb_ref, gbc_ref, gbr_ref, *refs):
    init_refs = refs[0:3] if has_init else None
    hf_ref, hb_ref, cout_ref, nout_ref, mout_ref, c_scr, n_scr, m_scr = refs[3:] if has_init else refs
    step = pl.program_id(1)
    L = MLSTM_CHUNK
    dh = MLSTM_HEAD_DIM
    H = MLSTM_HEADS

    @pl.when(step == 0)
    def _():
        for scr, k in zip((c_scr, n_scr, m_scr), range(3)):
            scr[...] = init_refs[k][0] if has_init else jnp.zeros(scr.shape, F32)

    K = MLSTM_STEP_CHUNKS
    R = K * L
    row = lax.broadcasted_iota(jnp.int32, (L, L), 0)
    col = lax.broadcasted_iota(jnp.int32, (L, L), 1)
    lower = (row >= col)
    upper = (row <= col)
    lower_b = jnp.where(lower, 1.0, 0.0).astype(BF16)
    upper_b = jnp.where(upper, 1.0, 0.0).astype(BF16)
    rrow = lax.broadcasted_iota(jnp.int32, (R, R), 0)
    rcol = lax.broadcasted_iota(jnp.int32, (R, R), 1)
    same_chunk = jnp.bitwise_and(rrow, -L) == jnp.bitwise_and(rcol, -L)
    neg_inf = jnp.full((), -jnp.inf, F32)

    gcol, grow, bcol, brow, btot = [], [], [], [], []
    ones_b = jnp.ones((L, LANE), BF16)
    for d in range(N_DIR):
        gc_ref, gr_ref = (gcf_ref, grf_ref) if d == 0 else (gcb_ref, grb_ref)
        gcol.append(gc_ref[...] + gbc_ref[...])
        grow.append((gr_ref[...] + gbr_ref[...][None]).reshape(K * MLSTM_GATES, L))
        before = same_chunk & ((rrow >= rcol) if d == 0 else (rrow <= rcol))
        bcol.append(_mm_exact_lhs(jnp.where(before, 1.0, 0.0).astype(BF16), jax.nn.log_sigmoid(gcol[d]), _NN))
        frow = jax.nn.log_sigmoid(grow[d])
        brow.append(_mm_exact_rhs(frow, upper_b if d == 0 else lower_b, _NN))
        btot.append(_mm_exact_rhs(frow, ones_b, _NN))

    units = [(j, d, h) for j in range(K) for d in range(N_DIR) for h in range(H)]
    nun = range(len(units))
    q, k, kt, v, vb, qb = [], [], [], [], [], []
    c_row, b_col, b_last = [], [], []
    for j, d, h in units:
        at = j if d == 0 else K - 1 - j
        rows = slice(at * L, (at + 1) * L)
        st = d * H + h
        gi, gf = st, 2 * H + st
        qk_ref, kt_ref, v_ref = (qkf_ref, ktf_ref, vf_ref) if d == 0 else (qkb_ref, ktb_ref, vb_ref)
        q.append(qk_ref[rows, h * dh:(h + 1) * dh] * (dh ** -0.5))
        k.append(qk_ref[rows, MLSTM_WIDTH + h * dh:MLSTM_WIDTH + (h + 1) * dh])
        kt.append(kt_ref[at, h * dh:(h + 1) * dh, :])
        v.append(v_ref[rows, h * dh:(h + 1) * dh])
        qb.append(q[-1].astype(BF16))
        vb.append(v[-1].astype(BF16))
        b_col.append(jnp.broadcast_to(bcol[d][rows, gf:gf + 1], (L, LANE)))
        c_row.append(grow[d][at * MLSTM_GATES + gi:at * MLSTM_GATES + gi + 1, :]
                     - brow[d][at * MLSTM_GATES + gf:at * MLSTM_GATES + gf + 1, :])
        b_last.append(btot[d][at * MLSTM_GATES + gf:at * MLSTM_GATES + gf + 1, :])

    last = [L - 1 if d == 0 else 0 for _, d, _ in units]
    qk_t = [_dg(qb[i], k[i].astype(BF16), _NT) for i in nun]
    rel = [jnp.where(lower if units[i][1] == 0 else upper, c_row[i], neg_inf) for i in nun]
    mx = [jnp.broadcast_to(jnp.max(rel[i], axis=-1, keepdims=True), (L, LANE)) for i in nun]
    m_loc = [b_col[i] + mx[i] for i in nun]
    s_loc = [qk_t[i] * jnp.exp(rel[i] - mx[i][:, 0:L]) for i in nun]
    s_v = [_dg(s_loc[i].astype(BF16), vb[i], _NN) for i in nun]
    s_sum = [jnp.broadcast_to(jnp.sum(s_loc[i], axis=-1, keepdims=True), (L, LANE)) for i in nun]
    cmax = [mx[i][last[i]:last[i] + 1, :] for i in nun]
    m_w = [b_last[i] + cmax[i] for i in nun]
    wj = [jnp.exp(c_row[i] - cmax[i][:, 0:L]) for i in nun]
    kv = [_dg((kt[i] * wj[i]).astype(BF16), vb[i], _NN) for i in nun]
    w_k = [_mm(jnp.broadcast_to(wj[i], (8, L)), k[i], _NN, 3)[0:1] for i in nun]

    nst = N_DIR * H
    c_st = [c_scr[st] for st in range(nst)]
    n_st = [n_scr[st:st + 1, :] for st in range(nst)]
    m_st = [m_scr[st:st + 1, :] for st in range(nst)]
    for j in range(K):
        idx = [j * nst + st for st in range(nst)]
        q_c = [_dg(qb[i], c_st[st].astype(BF16), _NN) for st, i in enumerate(idx)]
        for st, i in enumerate(idx):
            _, d, h = units[i]
            at = j if d == 0 else K - 1 - j
            h_ref = hf_ref if d == 0 else hb_ref
            log_inter = b_col[i] + m_st[st]
            m_s = jnp.maximum(log_inter, m_loc[i])
            inter = jnp.exp(log_inter - m_s)
            local = jnp.exp(m_loc[i] - m_s)
            q_n = jnp.broadcast_to(jnp.sum(q[i] * n_st[st], axis=-1, keepdims=True), (L, LANE))
            den = inter * q_n + local * s_sum[i]
            scale = 1.0 / jnp.maximum(jnp.abs(den), jnp.exp(-m_s))
            h_ref[at * L:(at + 1) * L, h * dh:(h + 1) * dh] = (inter * scale) * q_c[st] + (local * scale) * s_v[i]
            m_new = jnp.maximum(b_last[i] + m_st[st], m_w[i])
            carry = jnp.exp(b_last[i] + m_st[st] - m_new)
            fresh = jnp.exp(m_w[i] - m_new)
            c_st[st] = carry * c_st[st] + fresh * kv[i]
            n_st[st] = carry * n_st[st] + fresh * w_k[i]
            m_st[st] = m_new


    for st in range(nst):
        c_scr[st] = c_st[st]
        n_scr[st:st + 1, :] = n_st[st]
        m_scr[st:st + 1, :] = m_st[st]
    cout_ref[0] = c_scr[...]
    nout_ref[0] = n_scr[...]
    mout_ref[0] = m_scr[...]


def _mlstm_scan(z, qk, qk_blk, kt, gt, gate_bc, gate_br, c0, n0, m0, batch, seq_len):
    K = MLSTM_STEP_CHUNKS
    L = K * MLSTM_CHUNK
    assert seq_len % L == 0
    cps = seq_len // L
    n = batch * seq_len
    W = MLSTM_WIDTH
    nst = N_DIR * MLSTM_HEADS
    dh = MLSTM_HEAD_DIM

    def fw(b, c):
        return b * cps + c

    def bw(b, c):
        return b * cps + cps - 1 - c

    vblk = (ZM_OFF + 2 * W) // W
    gblk = ZG_OFF // LANE
    has_init = c0 is not None
    state_specs = [pl.BlockSpec((1, nst, dh, dh), lambda b, c: (b, 0, 0, 0)),
                   pl.BlockSpec((1, nst, dh), lambda b, c: (b, 0, 0)),
                   pl.BlockSpec((1, nst, LANE), lambda b, c: (b, 0, 0))]
    return pl.pallas_call(
        functools.partial(_mlstm_scan_kernel, has_init),
        grid=(batch, cps),
        in_specs=[pl.BlockSpec((L, 2 * W), lambda b, c: (fw(b, c), qk_blk)),
                  pl.BlockSpec((L, 2 * W), lambda b, c: (bw(b, c), qk_blk)),
                  pl.BlockSpec((K, W, MLSTM_CHUNK), lambda b, c: (fw(b, c), 0, 0)),
                  pl.BlockSpec((K, W, MLSTM_CHUNK), lambda b, c: (bw(b, c), 0, 0)),
                  pl.BlockSpec((L, W), lambda b, c: (fw(b, c), vblk)),
                  pl.BlockSpec((L, W), lambda b, c: (bw(b, c), vblk)),
                  pl.BlockSpec((L, LANE), lambda b, c: (fw(b, c), gblk)),
                  pl.BlockSpec((L, LANE), lambda b, c: (bw(b, c), gblk)),
                  pl.BlockSpec((K, MLSTM_GATES, MLSTM_CHUNK), lambda b, c: (fw(b, c), 0, 0)),
                  pl.BlockSpec((K, MLSTM_GATES, MLSTM_CHUNK), lambda b, c: (bw(b, c), 0, 0)),
                  _resident((1, LANE)),
                  _resident((MLSTM_GATES, 1))] + (state_specs if has_init else []),
        out_specs=[pl.BlockSpec((L, W), lambda b, c: (fw(b, c), 0)),
                   pl.BlockSpec((L, W), lambda b, c: (bw(b, c), 0))] + state_specs,
        out_shape=[jax.ShapeDtypeStruct((n, W), F32), jax.ShapeDtypeStruct((n, W), F32),
                   jax.ShapeDtypeStruct((batch, nst, dh, dh), F32),
                   jax.ShapeDtypeStruct((batch, nst, dh), F32),
                   jax.ShapeDtypeStruct((batch, nst, LANE), F32)],
        scratch_shapes=[pltpu.VMEM((nst, dh, dh), F32), pltpu.VMEM((nst, dh), F32),
                        pltpu.VMEM((nst, LANE), F32)],
        compiler_params=_params(("arbitrary", "arbitrary")),
        name="mlstm_scan",
    )(qk, qk, kt, kt, z, z, z, z, gt, gt, gate_bc, gate_br, *([c0, n0, m0] if has_init else []))


def _grid_conv(above, cur, below, cw_ref, width):
    T = cur.shape[0]
    E = T + 2 * width
    ext = jnp.concatenate([above, cur, below], axis=0)
    colp = jnp.bitwise_and(lax.broadcasted_iota(jnp.int32, (E, 1), 0), width - 1)
    left = jnp.where(colp == 0, 0.0, pltpu.roll(ext, 1, 0))
    right = jnp.where(colp == width - 1, 0.0, pltpu.roll(ext, E - 1, 0))

    def tap_row(i):
        rows = slice(i * width, i * width + T)
        return (cw_ref[3 * i:3 * i + 1, :] * left[rows] + cw_ref[3 * i + 1:3 * i + 2, :] * ext[rows]
                + cw_ref[3 * i + 2:3 * i + 3, :] * right[rows])

    return tap_row(0) + tap_row(1) + tap_row(2)


def _merge_kernel(conv, x_ref, mod_ref, ysf_ref, ysb_ref, bonus_ref, gate_ref, hf_ref, hb_ref, zo_ref,
                  zs_ref, lnxg_ref, lnxb_ref, gng_ref, pmean_ref, wbr_ref, wbm_ref, wout_ref, ng_ref, wup_ref,
                  cw_ref, cb_ref, x1_ref, act_ref, *scratch):
    if conv[0] == "grid":
        @pl.when(pl.program_id(0) == 0)
        def _():
            for ref in scratch:
                ref[...] = jnp.zeros(ref.shape, F32)

    mod = mod_ref[0]
    g1 = mod[:, 2 * D_MODEL:3 * D_MODEL]
    sh2 = mod[:, 3 * D_MODEL:4 * D_MODEL]
    sc2 = mod[:, 4 * D_MODEL:5 * D_MODEL]

    ys = ysf_ref[...] + ysb_ref[...]
    pmean = pmean_ref[...]
    mean = _mm_exact_rhs(ys, pmean, _NN)
    cen = ys - mean
    var = _mm_exact_rhs(cen * cen, pmean, _NN)
    y_r = (cen * lax.rsqrt(var + RWKV_GN_EPS) * lnxg_ref[...] + lnxb_ref[...] + bonus_ref[...]) * gate_ref[...]

    hs = hf_ref[...] + hb_ref[...]
    parts = []
    for h in range(MLSTM_HEADS):
        hh = hs[:, h * MLSTM_HEAD_DIM:(h + 1) * MLSTM_HEAD_DIM]
        mu = jnp.mean(hh, axis=-1, keepdims=True)
        ce = hh - mu
        va = jnp.mean(ce * ce, axis=-1, keepdims=True)
        parts.append(ce * lax.rsqrt(va + MLSTM_GN_EPS))
    y_m = jnp.concatenate(parts, axis=1) * gng_ref[...] * _sigmoid(zo_ref[...])

    gates = zs_ref[...]
    merged = (gates[:, 0:D_MODEL] * _dg(y_r.astype(BF16), wbr_ref[...], _NN)
              + gates[:, D_MODEL:2 * D_MODEL] * _dg(y_m.astype(BF16), wbm_ref[...], _NN))
    t = _dg(merged.astype(BF16), wout_ref[...], _NN)
    x1 = x_ref[...] + g1 * _rms(t, ng_ref[1:2, :])
    x1_ref[...] = x1
    h2 = _rms(x1, ng_ref[2:3, :]) * (1.0 + sc2) + sh2
    u = _dg(h2.astype(BF16), wup_ref[...], _NN)
    if conv[0] == "seq":
        pre = _dwconv(u[:, 0:D_FF], cw_ref, conv[1], False) + cb_ref[...]
        act_ref[...] = (_silu(pre) * u[:, D_FF:2 * D_FF]).astype(BF16)
    else:
        _, width, tiles_per_image = conv
        act_scr, val_scr, tail_scr = scratch
        T = act_scr.shape[0]
        step = pl.program_id(0)
        pos = (step + tiles_per_image - 1) % tiles_per_image
        for c0 in range(0, D_FF, CONV_CH_TILE):
            ch = slice(c0, c0 + CONV_CH_TILE)
            above = jnp.where(pos != 0, tail_scr[:, ch], 0.0)
            below = jnp.where(pos != tiles_per_image - 1, u[0:width, ch], 0.0)
            pre = _grid_conv(above, act_scr[:, ch], below, cw_ref.at[:, ch], width) + cb_ref[:, ch]
            act_ref[:, ch] = (_silu(pre) * val_scr[:, ch]).astype(BF16)
        tail_scr[...] = act_scr[T - width:T, :]
        act_scr[...] = u[:, 0:D_FF]
        val_scr[...] = u[:, D_FF:2 * D_FF]


def _merge(x2, mod, mod_row, z, ysf, ysb, bonus, gate, hf, hb, p, conv):
    n = x2.shape[0]
    W = RWKV_WIDTH
    rows = MERGE_TILE
    ntiles = n // rows
    delayed = conv[0] == "grid"
    cur = (lambda i: jnp.minimum(i, ntiles - 1)) if delayed else (lambda i: i)
    tile = lambda w: pl.BlockSpec((rows, w), lambda i: (cur(i), 0))
    act_spec = pl.BlockSpec((rows, D_FF), (lambda i: (jnp.maximum(i - 1, 0), 0)) if delayed else (lambda i: (i, 0)))
    scratch = ([pltpu.VMEM((rows, D_FF), F32), pltpu.VMEM((rows, D_FF), F32), pltpu.VMEM((conv[1], D_FF), F32)]
               if delayed else [])
    return pl.pallas_call(
        functools.partial(_merge_kernel, conv),
        grid=(ntiles + 1 if delayed else ntiles,),
        in_specs=[tile(D_MODEL),
                  pl.BlockSpec((1, 1, 6 * D_MODEL), lambda i: (mod_row(cur(i) * rows), 0, 0)),
                  tile(W), tile(W), tile(W), tile(W), tile(MLSTM_WIDTH), tile(MLSTM_WIDTH),
                  pl.BlockSpec((rows, MLSTM_WIDTH),
                               lambda i: (cur(i), (ZM_OFF + 3 * MLSTM_WIDTH) // MLSTM_WIDTH)),
                  pl.BlockSpec((rows, GATE_COLS), lambda i: (cur(i), ZS_OFF // GATE_COLS)),
                  _resident((1, W)), _resident((1, W)), _resident((1, MLSTM_WIDTH)),
                  _resident((W, W)),
                  _resident((W, D_MODEL)), _resident((MLSTM_WIDTH, D_MODEL)),
                  _resident((D_MODEL, D_MODEL)), _resident((4, D_MODEL)),
                  _resident((D_MODEL, 2 * D_FF)), _resident((9, D_FF)), _resident((1, D_FF))],
        out_specs=[tile(D_MODEL), act_spec],
        out_shape=[jax.ShapeDtypeStruct((n, D_MODEL), F32), jax.ShapeDtypeStruct((n, D_FF), BF16)],
        scratch_shapes=scratch,
        compiler_params=_params(("arbitrary",)),
        name="merge_ffn_up",
    )(x2, mod, ysf, ysb, bonus, gate, hf, hb, z, z, p["lnx_g"], p["lnx_b"], p["gn_g"], p["pmean"],
      p["w_br"], p["w_bm"], p["w_out"], p["norm_g"], p["ffn_up"], p["ffn_conv"], p["ffn_conv_b"])


def _down_kernel(x1_ref, mod_ref, a_ref, w_ref, ng_ref, o_ref):
    g2 = mod_ref[0][:, 5 * D_MODEL:6 * D_MODEL]
    f = _dg(a_ref[...], w_ref[...], _NN)
    o_ref[...] = x1_ref[...] + g2 * _rms(f, ng_ref[3:4, :])


def _down(x1, mod, mod_row, act, p):
    n = x1.shape[0]
    tile = DOWN_TILE
    return pl.pallas_call(
        _down_kernel,
        grid=(n // tile,),
        in_specs=[pl.BlockSpec((tile, D_MODEL), lambda i: (i, 0)),
                  pl.BlockSpec((1, 1, 6 * D_MODEL), lambda i: (mod_row(i * tile), 0, 0)),
                  pl.BlockSpec((tile, D_FF), lambda i: (i, 0)),
                  _resident((D_FF, D_MODEL)), _resident((4, D_MODEL))],
        out_specs=pl.BlockSpec((tile, D_MODEL), lambda i: (i, 0)),
        out_shape=jax.ShapeDtypeStruct((n, D_MODEL), F32),
        compiler_params=_params(("arbitrary",)),
        name="ffn_down",
    )(x1, mod, act, p["ffn_down"], p["norm_g"])


RWKV_LOCAL_PASSES = 1


def _state_to_pairs(s):
    b = s.shape[0]
    s = s.reshape(b, N_DIR, RWKV_PAIRS, 2, RWKV_HEAD_DIM, RWKV_HEAD_DIM)
    zero = jnp.zeros_like(s[:, :, :, 0])
    top = jnp.concatenate([s[:, :, :, 0], zero], axis=-1)
    bot = jnp.concatenate([zero, s[:, :, :, 1]], axis=-1)
    return jnp.concatenate([top, bot], axis=-2)


def _state_from_pairs(sb):
    b = sb.shape[0]
    n = RWKV_HEAD_DIM
    parts = jnp.stack([sb[..., 0:n, 0:n], sb[..., n:2 * n, n:2 * n]], axis=3)
    return parts.reshape(b, N_DIR, RWKV_HEADS, n, n)


def _trunk(x, mod, mod_row, rows, states, p):
    batch, seq_len, _ = x.shape
    n = batch * seq_len
    x2 = x.reshape(n, D_MODEL)
    nst = N_DIR * MLSTM_HEADS
    if states is None:
        s0 = c0 = n0 = m0 = None
    else:
        s0, c0, n0, m0 = states
        s0 = _state_to_pairs(s0)
        c0 = jnp.swapaxes(c0, -1, -2).reshape(batch, nst, MLSTM_HEAD_DIM, MLSTM_HEAD_DIM)
        n0 = n0.reshape(batch, nst, MLSTM_HEAD_DIM)
        m0 = jnp.broadcast_to(m0.reshape(batch, nst, 1), (batch, nst, LANE))

    fuse_width = seq_len if (rows == 1 and IN_TILE % seq_len == 0 and MERGE_TILE % seq_len == 0) else None
    proj = _in_proj(x2, mod, mod_row, p["norm_g"][0:1], p["w_in"], p["mlstm_conv"], fuse_width)
    z = proj[0]

    rp, y0, gm, hm, gate, bonus = _rwkv_local(z, seq_len, p, RWKV_LOCAL_PASSES)
    ysf, ysb, s_fin = _rwkv_scan(s0, rp, y0, gm, hm, batch, seq_len)
    s_fin = _state_from_pairs(s_fin)

    if fuse_width is None:
        z, gt = proj
        qk, kt = _qk_conv(z, batch, seq_len, rows, p["mlstm_conv"])
        qk_blk = 0
    else:
        z, gt, kt = proj
        qk, qk_blk = z, ZM_OFF // (2 * MLSTM_WIDTH)
    hf, hb, ct_fin, n_fin, m_fin = _mlstm_scan(z, qk, qk_blk, kt, gt, p["gate_bc"], p["gate_br"],
                                               c0, n0, m0, batch, seq_len)
    c_fin = jnp.swapaxes(ct_fin, -1, -2)

    if fuse_width is not None:
        ffn_conv = ("seq", fuse_width)
    else:
        width = seq_len // rows
        assert rows > 1 and width & (width - 1) == 0 and MERGE_TILE % width == 0 and seq_len % MERGE_TILE == 0
        ffn_conv = ("grid", width, seq_len // MERGE_TILE)
    x1, act = _merge(x2, mod, mod_row, z, ysf, ysb, bonus, gate, hf, hb, p, ffn_conv)
    out = _down(x1, mod, mod_row, act, p)

    new_states = (s_fin,
                  c_fin.reshape(batch, N_DIR, MLSTM_HEADS, MLSTM_HEAD_DIM, MLSTM_HEAD_DIM),
                  n_fin.reshape(batch, N_DIR, MLSTM_HEADS, MLSTM_HEAD_DIM),
                  m_fin[:, :, 0].reshape(batch, N_DIR, MLSTM_HEADS))
    return out.reshape(batch, seq_len, D_MODEL), new_states


def _pack_layer(l, ada_w, ada_b, norm_g, w_in, rwkv_mu, rwkv_w0, rwkv_w_up, rwkv_a0, rwkv_a_up,
                rwkv_g_up, rwkv_kk_scale, rwkv_k_a, rwkv_r_k, rwkv_lnx_g, rwkv_lnx_b, mlstm_conv,
                mlstm_gate_b, mlstm_gn_g, w_branch_rwkv, w_branch_mlstm, w_out, ffn_up, ffn_conv,
                ffn_conv_b, ffn_down):
    W = RWKV_WIDTH
    w_in_b = jnp.pad(w_in[l].astype(BF16), ((0, 0), (0, W_IN_PAD - w_in.shape[-1])))

    head = jnp.arange(W, dtype=jnp.int32) // RWKV_HEAD_DIM
    same = (head[:, None] == head[None, :])
    gb = mlstm_gate_b[l].reshape(1, MLSTM_GATES)
    return dict(
        ada_w=ada_w[l], ada_b=ada_b[l], norm_g=norm_g[l], w_in=w_in_b,
        mu=rwkv_mu[l].reshape(1, RWKV_COLS),
        w0=rwkv_w0[l].reshape(N_DIR, 1, W), w_up=rwkv_w_up[l],
        a0=rwkv_a0[l].reshape(N_DIR, 1, W), a_up=rwkv_a_up[l], g_up=rwkv_g_up[l],
        kk_scale=rwkv_kk_scale[l].reshape(1, W), k_a=rwkv_k_a[l].reshape(1, W),
        r_k=rwkv_r_k[l].reshape(1, W),
        lnx_g=rwkv_lnx_g[l].reshape(1, W), lnx_b=rwkv_lnx_b[l].reshape(1, W),
        pones=same.astype(BF16), pmean=(same.astype(F32) / RWKV_HEAD_DIM).astype(BF16),
        mlstm_conv=mlstm_conv[l].reshape(9, 2 * MLSTM_WIDTH),
        gate_bc=jnp.pad(gb, ((0, 0), (0, LANE - MLSTM_GATES))), gate_br=gb.reshape(MLSTM_GATES, 1),
        gn_g=mlstm_gn_g[l].reshape(1, MLSTM_WIDTH),
        w_br=w_branch_rwkv[l].astype(BF16), w_bm=w_branch_mlstm[l].astype(BF16),
        w_out=w_out[l].astype(BF16), ffn_up=ffn_up[l].astype(BF16),
        ffn_conv=ffn_conv[l].reshape(9, D_FF), ffn_conv_b=ffn_conv_b[l].reshape(1, D_FF),
        ffn_down=ffn_down[l].astype(BF16),
    )


def kernel(x_prompt, x_sample, c, state_rwkv, state_mlstm_C, state_mlstm_n, state_mlstm_m, c_ctx,
           ada_w, ada_b, norm_g, w_in, rwkv_mu, rwkv_w0, rwkv_w_up, rwkv_a0, rwkv_a_up, rwkv_g_up,
           rwkv_kk_scale, rwkv_k_a, rwkv_r_k, rwkv_lnx_g, rwkv_lnx_b, mlstm_conv, mlstm_gate_b,
           mlstm_gn_g, w_branch_rwkv, w_branch_mlstm, w_out, ffn_up, ffn_conv, ffn_conv_b, ffn_down):
    depth = ada_w.shape[0]
    batch = x_prompt.shape[0]
    dec_batch, dec_seq, _ = x_sample.shape
    latent_rows = dec_seq // GRID_W
    cond = jnp.concatenate([c_ctx[None, :], c, jnp.zeros((8 - 1 - dec_batch, D_MODEL), F32)], axis=0)

    xp, xs = x_prompt, x_sample
    new_s, new_c, new_n, new_m = [], [], [], []
    for l in range(depth):
        p = _pack_layer(l, ada_w, ada_b, norm_g, w_in, rwkv_mu, rwkv_w0, rwkv_w_up, rwkv_a0, rwkv_a_up,
                        rwkv_g_up, rwkv_kk_scale, rwkv_k_a, rwkv_r_k, rwkv_lnx_g, rwkv_lnx_b, mlstm_conv,
                        mlstm_gate_b, mlstm_gn_g, w_branch_rwkv, w_branch_mlstm, w_out, ffn_up, ffn_conv,
                        ffn_conv_b, ffn_down)
        mod = _ada(cond, p["ada_w"], p["ada_b"]).reshape(8, 1, 6 * D_MODEL)
        xp, (s, cc, nn, mm) = _trunk(xp, mod, lambda r: 0, 1, None, p)
        new_s.append(s)
        new_c.append(cc)
        new_n.append(nn)
        new_m.append(mm)
        xs, _ = _trunk(xs, mod, lambda r: 1 + r // dec_seq, latent_rows,
                       (state_rwkv[:, l], state_mlstm_C[:, l], state_mlstm_n[:, l], state_mlstm_m[:, l]), p)
    return (xp, xs, jnp.stack(new_s, axis=1), jnp.stack(new_c, axis=1),
            jnp.stack(new_n, axis=1), jnp.stack(new_m, axis=1))
```

```python
import functools

import jax
import jax.numpy as jnp
from jax import lax
from jax.experimental import pallas as pl
from jax.experimental.pallas import tpu as pltpu

F32 = jnp.float32
BF16 = jnp.bfloat16

D_MODEL = 1024
N_DIR = 2
RWKV_HEADS = 8
RWKV_HEAD_DIM = 64
RWKV_WIDTH = RWKV_HEADS * RWKV_HEAD_DIM
DECAY_LORA = 64
ICLR_LORA = 64
GATE_LORA = 128
MLSTM_HEADS = 4
MLSTM_HEAD_DIM = 128
MLSTM_WIDTH = MLSTM_HEADS * MLSTM_HEAD_DIM
MLSTM_CHUNK = 64
D_FF = 2816
GRID_W = 64
RMS_EPS = 1e-6
RWKV_GN_EPS = 64e-5
MLSTM_GN_EPS = 1e-5
DECAY_SCALE = 0.606531

RWKV_COLS = 3 * RWKV_WIDTH + N_DIR * DECAY_LORA + N_DIR * ICLR_LORA + GATE_LORA
MLSTM_GATES = 2 * N_DIR * MLSTM_HEADS
MLSTM_COLS = 4 * MLSTM_WIDTH + MLSTM_GATES
GATE_COLS = 2 * D_MODEL

LANE = 128
ZR_BLOCK = 2048
ZG_OFF = RWKV_COLS
ZM_OFF = ZR_BLOCK
ZS_OFF = ZM_OFF + 4 * MLSTM_WIDTH
Z_COLS = ZS_OFF + GATE_COLS

IN_TILE = 512
MERGE_TILE = 256
DOWN_TILE = 512
RCHUNK = 64
CONV_CH_TILE = 256
VMEM_LIMIT = 56 * 1024 * 1024


def _params(sem):
    return pltpu.CompilerParams(dimension_semantics=sem, vmem_limit_bytes=VMEM_LIMIT)


def _resident(shape):
    nd = len(shape)
    return pl.BlockSpec(shape, lambda *_: (0,) * nd, pipeline_mode=pl.Buffered(1))


def _split2(a):
    hi = a.astype(BF16)
    lo = (a - hi.astype(F32)).astype(BF16)
    return hi, lo


def _split3(a):
    hi = a.astype(BF16)
    r1 = a - hi.astype(F32)
    mid = r1.astype(BF16)
    lo = (r1 - mid.astype(F32)).astype(BF16)
    return hi, mid, lo


def _dg(a, b, dims):
    return lax.dot_general(a, b, dims, preferred_element_type=F32)


def _mm(a, b, dims, passes):
    if passes == 1:
        return _dg(a.astype(BF16), b.astype(BF16), dims)
    ah, al = _split2(a)
    bh, bl = _split2(b)
    return _dg(ah, bh, dims) + (_dg(ah, bl, dims) + _dg(al, bh, dims))


def _mm_exact_lhs(a_bf16, b, dims):
    b1, b2, b3 = _split3(b)
    return _dg(a_bf16, b1, dims) + (_dg(a_bf16, b2, dims) + _dg(a_bf16, b3, dims))


def _mm_exact_rhs(a, b_bf16, dims, pieces=3):
    if pieces == 2:
        a1, a2 = _split2(a)
        return _dg(a1, b_bf16, dims) + _dg(a2, b_bf16, dims)
    a1, a2, a3 = _split3(a)
    return _dg(a1, b_bf16, dims) + (_dg(a2, b_bf16, dims) + _dg(a3, b_bf16, dims))


_NN = (((1,), (0,)), ((), ()))
_NT = (((1,), (1,)), ((), ()))
_TN = (((0,), (0,)), ((), ()))
_BNN = (((2,), (1,)), ((0,), (0,)))
_BNT = (((2,), (2,)), ((0,), (0,)))
_BTN = (((1,), (1,)), ((0,), (0,)))


def _sigmoid(x):
    return jax.nn.sigmoid(x)


def _silu(x):
    return x * jax.nn.sigmoid(x)


def _rms(x, g):
    return x * lax.rsqrt(jnp.mean(x * x, axis=-1, keepdims=True) + RMS_EPS) * g


def _ada_kernel(cond_ref, w_ref, b_ref, o_ref):
    s = _silu(cond_ref[...])
    o_ref[...] = _dg(s.astype(BF16), w_ref[...].astype(BF16), _NN) + b_ref[...]


def _ada(cond8, ada_w, ada_b):
    n = ada_w.shape[1]
    tn = 1536
    return pl.pallas_call(
        _ada_kernel,
        grid=(n // tn,),
        in_specs=[_resident((8, D_MODEL)),
                  pl.BlockSpec((D_MODEL, tn), lambda j: (0, j)),
                  pl.BlockSpec((1, tn), lambda j: (0, j))],
        out_specs=pl.BlockSpec((8, tn), lambda j: (0, j)),
        out_shape=jax.ShapeDtypeStruct((8, n), F32),
        compiler_params=_params(("arbitrary",)),
        name="ada_mod",
    )(cond8, ada_w, ada_b.reshape(1, n))


def _in_kernel(conv_width, x_ref, mod_ref, g_ref, w_ref, cw_ref, z_ref, gt_ref, *kt_ref):
    mod = mod_ref[0]
    sh = mod[:, 0:D_MODEL]
    sc = mod[:, D_MODEL:2 * D_MODEL]
    h = (_rms(x_ref[...], g_ref[...]) * (1.0 + sc) + sh).astype(BF16)
    m_off = RWKV_COLS
    t_off = RWKV_COLS + 4 * MLSTM_WIDTH
    z_ref[:, 0:ZG_OFF] = _dg(h, w_ref[:, 0:m_off], _NN)
    tail = _dg(h, w_ref[:, t_off:w_ref.shape[1]], _NN)
    zg = tail[:, 0:LANE]
    z_ref[:, ZG_OFF:ZM_OFF] = zg
    zgt = zg.T
    for c in range(zg.shape[0] // MLSTM_CHUNK):
        gt_ref[c] = zgt[0:MLSTM_GATES, c * MLSTM_CHUNK:(c + 1) * MLSTM_CHUNK]
    z_ref[:, ZS_OFF:Z_COLS] = _sigmoid(tail[:, MLSTM_GATES:MLSTM_GATES + GATE_COLS])
    zm = _dg(h, w_ref[:, m_off:t_off], _NN)
    if conv_width is None:
        z_ref[:, ZM_OFF:ZS_OFF] = zm
    else:
        qk_cols = 2 * MLSTM_WIDTH
        qk = _silu(_dwconv(zm[:, 0:qk_cols], cw_ref, conv_width, False))
        z_ref[:, ZM_OFF:ZM_OFF + qk_cols] = qk
        z_ref[:, ZM_OFF + qk_cols:ZS_OFF] = zm[:, qk_cols:]
        _store_time_on_lanes(qk[:, MLSTM_WIDTH:qk_cols], kt_ref[0], MLSTM_CHUNK)


def _in_proj(x2, mod, mod_row, norm_g0, w_in, conv_w9, conv_width):
    n = x2.shape[0]
    tile = IN_TILE
    cpt = tile // MLSTM_CHUNK
    nchunk = n // MLSTM_CHUNK
    out_specs = [pl.BlockSpec((tile, Z_COLS), lambda i: (i, 0)),
                 pl.BlockSpec((cpt, MLSTM_GATES, MLSTM_CHUNK), lambda i: (i, 0, 0))]
    out_shape = [jax.ShapeDtypeStruct((n, Z_COLS), F32),
                 jax.ShapeDtypeStruct((nchunk, MLSTM_GATES, MLSTM_CHUNK), F32)]
    if conv_width is not None:
        out_specs.append(pl.BlockSpec((cpt, MLSTM_WIDTH, MLSTM_CHUNK), lambda i: (i, 0, 0)))
        out_shape.append(jax.ShapeDtypeStruct((nchunk, MLSTM_WIDTH, MLSTM_CHUNK), F32))
    return pl.pallas_call(
        functools.partial(_in_kernel, conv_width),
        grid=(n // tile,),
        in_specs=[pl.BlockSpec((tile, D_MODEL), lambda i: (i, 0)),
                  pl.BlockSpec((1, 1, 6 * D_MODEL), lambda i: (mod_row(i * tile), 0, 0)),
                  _resident((1, D_MODEL)), _resident(w_in.shape), _resident(conv_w9.shape)],
        out_specs=out_specs,
        out_shape=out_shape,
        compiler_params=_params(("arbitrary",)),
        name="in_proj",
    )(x2, mod, norm_g0, w_in, conv_w9)


LOCAL_CHUNKS = 4
PAIR_LANES = 2 * RWKV_HEAD_DIM
RWKV_PAIRS = RWKV_HEADS // 2


def _bd(x):
    lane = lax.broadcasted_iota(jnp.int32, x.shape, 1)
    left = lane < RWKV_HEAD_DIM
    return jnp.concatenate([jnp.where(left, x, 0.0), jnp.where(left, 0.0, x)], axis=0)


def _rwkv_local_kernel(chunks_per_seq, passes,
                       z_ref, zp_ref, zn_ref, mu_ref, w0_ref, wup_ref, a0_ref, aup_ref, gup_ref,
                       kks_ref, ka_ref, rk_ref, pones_ref,
                       rp_ref, y0_ref, gm_ref, hm_ref, gate_ref, bonus_ref):
    C = RCHUNK
    W = RWKV_WIDTH
    NS = LOCAL_CHUNKS
    R = NS * C
    first = (pl.program_id(0) * NS) % chunks_per_seq
    has_prev = first != 0
    has_next = first + NS != chunks_per_seq

    z = z_ref[:, 0:RWKV_COLS]
    zp = jnp.where(has_prev, zp_ref[7:8, 0:RWKV_COLS], 0.0)
    zn = jnp.where(has_next, zn_ref[0:1, 0:RWKV_COLS], 0.0)
    trow = lax.broadcasted_iota(jnp.int32, (R, 1), 0)
    prev = jnp.where(trow == 0, zp, pltpu.roll(z, 1, 0))
    nxt = jnp.where(trow == R - 1, zn, pltpu.roll(z, R - 1, 0))
    zs = z + mu_ref[...] * (0.5 * (prev + nxt) - z)

    r = zs[:, 0:W]
    k = zs[:, W:2 * W]
    v = zs[:, 2 * W:3 * W]
    gd = zs[:, 3 * W + 2 * DECAY_LORA + 2 * ICLR_LORA:RWKV_COLS]
    gate_ref[...] = _dg(_sigmoid(gd).astype(BF16), gup_ref[...].astype(BF16), _NN)

    pones = pones_ref[...]
    kks = k * kks_ref[...]
    norm = jnp.sqrt(_mm_exact_rhs(kks * kks, pones, _NN, pieces=2))
    kk = kks / jnp.maximum(norm, 1e-12)

    P = PAIR_LANES
    row = lax.broadcasted_iota(jnp.int32, (R, R), 0)
    col = lax.broadcasted_iota(jnp.int32, (R, R), 1)
    same_chunk = jnp.bitwise_and(row, -C) == jnp.bitwise_and(col, -C)
    prow = lax.broadcasted_iota(jnp.int32, (C, P), 0)
    pcol = jnp.bitwise_and(lax.broadcasted_iota(jnp.int32, (C, P), 1), RWKV_HEAD_DIM - 1)
    eye_p = jnp.where(prow == pcol, 1.0, 0.0)
    left_head = lax.broadcasted_iota(jnp.int32, (C, P), 1) < RWKV_HEAD_DIM

    def diag_blocks(m):
        return jnp.where(left_head, m[0:RWKV_HEAD_DIM], m[RWKV_HEAD_DIM:P])

    abar, rbar, kt, bt, kw, bw, wc, strict, incl = [], [], [], [], [], [], [], [], []
    kd_sum = None
    for d in range(N_DIR):
        o = 3 * W + d * DECAY_LORA
        wd = zs[:, o:o + DECAY_LORA]
        o = 3 * W + 2 * DECAY_LORA + d * ICLR_LORA
        ad = zs[:, o:o + ICLR_LORA]
        logw = -DECAY_SCALE * _sigmoid(w0_ref[d] + _dg(jnp.tanh(wd).astype(BF16), wup_ref[d].astype(BF16), _NN))
        a = _sigmoid(a0_ref[d] + _dg(ad.astype(BF16), aup_ref[d].astype(BF16), _NN))
        kd = k * (1.0 + (a - 1.0) * ka_ref[...])
        b = kk * a
        kd_sum = kd if kd_sum is None else kd_sum + kd

        earlier_or_same = same_chunk & ((row >= col) if d == 0 else (row <= col))
        cum_i = _mm_exact_lhs(jnp.where(earlier_or_same, 1.0, 0.0).astype(BF16), logw, _NN)
        cum_e = cum_i - logw
        ab_d, rb_d, kt_d, bt_d, kw_d, bw_d, wc_d = [], [], [], [], [], [], []
        for s in range(NS):
            rs = slice(s * C, (s + 1) * C)
            ci_s = cum_i[rs]
            ctot = jnp.sum(logw[rs], axis=0, keepdims=True)
            e_ni = jnp.exp(-ci_s)
            e_ti = jnp.exp(ctot - ci_s)
            ab_d.append(kk[rs] * jnp.exp(cum_e[rs]))
            rb_d.append(r[rs] * jnp.exp(ci_s))
            kt_d.append(kd[rs] * e_ni)
            bt_d.append(b[rs] * e_ni)
            kw_d.append(kd[rs] * e_ti)
            bw_d.append(b[rs] * e_ti)
            wc_d.append(jnp.exp(ctot))
        abar.append(ab_d)
        rbar.append(rb_d)
        kt.append(kt_d)
        bt.append(bt_d)
        kw.append(kw_d)
        bw.append(bw_d)
        wc.append(wc_d)
        strict.append((prow > pcol) if d == 0 else (prow < pcol))
        incl.append((prow >= pcol) if d == 0 else (prow <= pcol))
    bonus_ref[...] = _mm_exact_rhs(r * kd_sum * rk_ref[...], pones, _NN, pieces=2) * v

    mm = functools.partial(_mm, passes=passes)
    chains = [(s, d, p) for s in range(NS) for d in range(N_DIR) for p in range(RWKV_PAIRS)]
    nch = range(len(chains))

    def sel(arr, i):
        s, d, p = chains[i]
        return arr[d][s][:, p * P:(p + 1) * P]

    cat0 = lambda a_, b_: jnp.concatenate([a_, b_], axis=0)
    cat1 = lambda a_, b_: jnp.concatenate([a_, b_], axis=1)
    vsl = [v[s * C:(s + 1) * C, p * P:(p + 1) * P] for s, _, p in chains]
    lhs = [cat0(sel(abar, i), sel(rbar, i)) for i in nch]
    by_b = [mm(lhs[i], _bd(sel(bt, i)), _NT) for i in nch]
    by_k = [mm(lhs[i], _bd(sel(kt, i)), _NT) for i in nch]
    a_kk = [jnp.where(strict[chains[i][1]], by_b[i][0:C], 0.0) for i in nch]
    a_rb = [jnp.where(incl[chains[i][1]], by_b[i][C:2 * C], 0.0) for i in nch]
    a_kv = [jnp.where(strict[chains[i][1]], by_k[i][0:C], 0.0) for i in nch]
    a_rk = [jnp.where(incl[chains[i][1]], by_k[i][C:2 * C], 0.0) for i in nch]
    on_v = [mm(cat0(a_kv[i], a_rk[i]), _bd(vsl[i]), _NN) for i in nch]

    x = [-m for m in a_kk]
    tinv = [eye_p + m for m in x]
    x = [mm(m, _bd(m), _NN) for m in x]
    for _ in range(4):
        both = [mm(cat0(tinv[i], x[i]), _bd(x[i]), _NN) for i in nch]
        tinv = [tinv[i] + both[i][0:C] for i in nch]
        x = [m[C:2 * C] for m in both]
    tinv = [tinv[i] + mm(tinv[i], _bd(x[i]), _NN) for i in nch]

    solved = [mm(tinv[i], cat1(_bd(sel(abar, i)), _bd(on_v[i][0:C])), _NN) for i in nch]
    ap = [m[:, 0:P] for m in solved]
    u0 = [m[:, P:2 * P] for m in solved]
    corr = [mm(a_rb[i], cat1(_bd(ap[i]), _bd(u0[i])), _NN) for i in nch]
    on_b = [mm(cat1(ap[i], u0[i]), sel(bw, i), _TN) for i in nch]
    vk = [mm(vsl[i], sel(kw, i), _TN) for i in nch]
    for i in nch:
        s, d, p = chains[i]
        rows = slice(s * C, (s + 1) * C)
        lanes = slice(p * P, (p + 1) * P)
        rp_ref[d, rows, lanes] = (sel(rbar, i) - corr[i][:, 0:P]).astype(BF16)
        y0_ref[d, rows, lanes] = on_v[i][C:2 * C] - corr[i][:, P:2 * P]
        gm_ref[d, s, p] = (eye_p * sel(wc, i) - diag_blocks(on_b[i][0:P])).astype(BF16)
        hm_ref[d, s, p] = diag_blocks(vk[i] - on_b[i][P:2 * P])


def _rwkv_local(z, seq_len, p, passes):
    n = z.shape[0]
    nchunk = n // RCHUNK
    cps = seq_len // RCHUNK
    assert cps % LOCAL_CHUNKS == 0
    W = RWKV_WIDTH
    rows = LOCAL_CHUNKS * RCHUNK
    hb = rows // 8
    last8 = n // 8 - 1
    mat = lambda dt: jax.ShapeDtypeStruct((N_DIR, nchunk, RWKV_PAIRS, RWKV_HEAD_DIM, PAIR_LANES), dt)
    mat_spec = pl.BlockSpec((N_DIR, LOCAL_CHUNKS, RWKV_PAIRS, RWKV_HEAD_DIM, PAIR_LANES),
                            lambda c: (0, c, 0, 0, 0))
    tok = lambda dt: jax.ShapeDtypeStruct((N_DIR, n, W), dt)
    tok_spec = pl.BlockSpec((N_DIR, rows, W), lambda c: (0, c, 0))
    row_spec = pl.BlockSpec((rows, W), lambda c: (c, 0))
    return pl.pallas_call(
        functools.partial(_rwkv_local_kernel, cps, passes),
        grid=(nchunk // LOCAL_CHUNKS,),
        in_specs=[pl.BlockSpec((rows, ZR_BLOCK), lambda c: (c, 0)),
                  pl.BlockSpec((8, ZR_BLOCK), lambda c: (jnp.maximum(c * hb - 1, 0), 0)),
                  pl.BlockSpec((8, ZR_BLOCK), lambda c: (jnp.minimum((c + 1) * hb, last8), 0)),
                  _resident((1, RWKV_COLS)),
                  _resident((N_DIR, 1, W)), _resident((N_DIR, DECAY_LORA, W)),
                  _resident((N_DIR, 1, W)), _resident((N_DIR, ICLR_LORA, W)),
                  _resident((GATE_LORA, W)),
                  _resident((1, W)), _resident((1, W)), _resident((1, W)),
                  _resident((W, W))],
        out_specs=[tok_spec, tok_spec, mat_spec, mat_spec, row_spec, row_spec],
        out_shape=[tok(BF16), tok(F32), mat(BF16), mat(F32),
                   jax.ShapeDtypeStruct((n, W), F32), jax.ShapeDtypeStruct((n, W), F32)],
        compiler_params=_params(("arbitrary",)),
        name="rwkv_local",
    )(z, z, z, p["mu"], p["w0"], p["w_up"], p["a0"], p["a_up"], p["g_up"],
      p["kk_scale"], p["k_a"], p["r_k"], p["pones"])


SCAN_CHUNKS = 4


def _rwkv_scan_kernel(has_init, *refs):
    s0_ref = refs[0] if has_init else None
    (rpf_ref, rpb_ref, y0f_ref, y0b_ref, gmf_ref, gmb_ref, hmf_ref, hmb_ref,
     ysf_ref, ysb_ref, sout_ref, s_scr) = refs[1:] if has_init else refs

    @pl.when(pl.program_id(1) == 0)
    def _():
        s_scr[...] = s0_ref[0] if has_init else jnp.zeros(s_scr.shape, F32)

    K = SCAN_CHUNKS
    C = RCHUNK
    rp_ref, y0_ref, gm_ref, hm_ref, ys_ref = ((rpf_ref, rpb_ref), (y0f_ref, y0b_ref), (gmf_ref, gmb_ref),
                                              (hmf_ref, hmb_ref), (ysf_ref, ysb_ref))
    chains = [(d, p) for d in range(N_DIR) for p in range(RWKV_PAIRS)]
    lanes = [slice(p * PAIR_LANES, (p + 1) * PAIR_LANES) for _, p in chains]
    nch = range(len(chains))
    s = [s_scr[d, p] for d, p in chains]
    for j in range(K):
        at = (j, K - 1 - j)
        rows = [slice(at[d] * C, (at[d] + 1) * C) for d, _ in chains]
        sb = [m.astype(BF16) for m in s]
        y = [_dg(rp_ref[chains[i][0]][0, rows[i], lanes[i]], sb[i], _NT) for i in nch]
        sg = [_dg(sb[i], _bd(gm_ref[chains[i][0]][0, at[chains[i][0]], chains[i][1]]), _NN) for i in nch]
        for i in nch:
            d, p = chains[i]
            ys_ref[d][rows[i], lanes[i]] = y[i] + y0_ref[d][0, rows[i], lanes[i]]
        s = [sg[i] + _bd(hm_ref[chains[i][0]][0, at[chains[i][0]], chains[i][1]]) for i in nch]
    for i in nch:
        d, p = chains[i]
        s_scr[d, p] = s[i]
        n = RWKV_HEAD_DIM
        sout_ref[0, d, 2 * p] = s[i][0:n, 0:n]
        sout_ref[0, d, 2 * p + 1] = s[i][n:2 * n, n:2 * n]


def _rwkv_scan(s0, rp, y0, gm, hm, batch, seq_len):
    K = SCAN_CHUNKS
    spb = seq_len // (RCHUNK * K)
    n = batch * seq_len

    def fwd(b, s):
        return b * spb + s

    def bwd(b, s):
        return b * spb + spb - 1 - s

    def mat_spec(d, at):
        return pl.BlockSpec((1, K, RWKV_PAIRS, RWKV_HEAD_DIM, PAIR_LANES), lambda b, s: (d, at(b, s), 0, 0, 0))

    def tok_spec(d, at):
        return pl.BlockSpec((1, K * RCHUNK, RWKV_WIDTH), lambda b, s: (d, at(b, s), 0))

    st_spec = pl.BlockSpec((1, N_DIR, RWKV_PAIRS, PAIR_LANES, PAIR_LANES), lambda b, s: (b, 0, 0, 0, 0))
    ys = jax.ShapeDtypeStruct((n, RWKV_WIDTH), F32)
    has_init = s0 is not None
    return pl.pallas_call(
        functools.partial(_rwkv_scan_kernel, has_init),
        grid=(batch, spb),
        in_specs=([st_spec] if has_init else [])
        + [tok_spec(0, fwd), tok_spec(1, bwd), tok_spec(0, fwd), tok_spec(1, bwd),
           mat_spec(0, fwd), mat_spec(1, bwd), mat_spec(0, fwd), mat_spec(1, bwd)],
        out_specs=[pl.BlockSpec((K * RCHUNK, RWKV_WIDTH), lambda b, s: (fwd(b, s), 0)),
                   pl.BlockSpec((K * RCHUNK, RWKV_WIDTH), lambda b, s: (bwd(b, s), 0)),
                   pl.BlockSpec((1, N_DIR, RWKV_HEADS, RWKV_HEAD_DIM, RWKV_HEAD_DIM),
                                lambda b, s: (b, 0, 0, 0, 0))],
        out_shape=[ys, ys,
                   jax.ShapeDtypeStruct((batch, N_DIR, RWKV_HEADS, RWKV_HEAD_DIM, RWKV_HEAD_DIM), F32)],
        scratch_shapes=[pltpu.VMEM((N_DIR, RWKV_PAIRS, PAIR_LANES, PAIR_LANES), F32)],
        compiler_params=_params(("arbitrary", "arbitrary")),
        name="rwkv_scan",
    )(*([s0] if has_init else []), rp, rp, y0, y0, gm, gm, hm, hm)


CONV_BLOCK_ROWS = 2048


def _dwconv(x, w_ref, width, vertical):
    T = x.shape[0]
    t = lax.broadcasted_iota(jnp.int32, (T, 1), 0)
    assert width & (width - 1) == 0
    colp = jnp.bitwise_and(t, width - 1)
    xl = jnp.where(colp == 0, 0.0, pltpu.roll(x, 1, 0))
    xr = jnp.where(colp == width - 1, 0.0, pltpu.roll(x, T - 1, 0))

    def tap_row(i):
        return w_ref[3 * i:3 * i + 1, :] * xl + w_ref[3 * i + 1:3 * i + 2, :] * x + w_ref[3 * i + 2:3 * i + 3, :] * xr

    out = tap_row(1)
    if vertical:
        out = out + jnp.where(t < width, 0.0, pltpu.roll(tap_row(0), width, 0))
        out = out + jnp.where(t >= T - width, 0.0, pltpu.roll(tap_row(2), T - width, 0))
    return out


def _conv_geometry(n, seq_len, rows):
    if rows > 1:
        return seq_len, seq_len // rows, True
    block = CONV_BLOCK_ROWS if (n % CONV_BLOCK_ROWS == 0 and CONV_BLOCK_ROWS % seq_len == 0) else seq_len
    return block, seq_len, False


def _store_time_on_lanes(x, out_ref, chunk):
    xt = x.T
    for c in range(x.shape[0] // chunk):
        out_ref[c] = xt[:, c * chunk:(c + 1) * chunk]


def _qk_conv_kernel(width, vertical, first_k_tile, x_ref, w_ref, o_ref, kt_ref):
    out = _silu(_dwconv(x_ref[...], w_ref, width, vertical))
    o_ref[...] = out

    @pl.when(pl.program_id(1) >= first_k_tile)
    def _():
        _store_time_on_lanes(out, kt_ref, MLSTM_CHUNK)


def _qk_conv(z, batch, seq_len, rows, conv_w9):
    n = batch * seq_len
    ch = 2 * MLSTM_WIDTH
    tc = CONV_CH_TILE
    off = ZM_OFF // tc
    first_k = MLSTM_WIDTH // tc
    block, width, vertical = _conv_geometry(n, seq_len, rows)
    cpb = block // MLSTM_CHUNK
    return pl.pallas_call(
        functools.partial(_qk_conv_kernel, width, vertical, first_k),
        grid=(n // block, ch // tc),
        in_specs=[pl.BlockSpec((block, tc), lambda b, j: (b, off + j)),
                  pl.BlockSpec((9, tc), lambda b, j: (0, j))],
        out_specs=[pl.BlockSpec((block, tc), lambda b, j: (b, j)),
                   pl.BlockSpec((cpb, tc, MLSTM_CHUNK), lambda b, j: (b, jnp.maximum(j - first_k, 0), 0))],
        out_shape=[jax.ShapeDtypeStruct((n, ch), F32),
                   jax.ShapeDtypeStruct((n // MLSTM_CHUNK, MLSTM_WIDTH, MLSTM_CHUNK), F32)],
        compiler_params=_params(("arbitrary", "arbitrary")),
        name="mlstm_qk_conv",
    )(z, conv_w9)


MLSTM_STEP_CHUNKS = 4

def _mlstm_scan_kernel(has_init, qkf_ref, qkb_ref, ktf_ref, ktb_ref, vf_ref, vb_ref, gcf_ref, gcb_ref,
                       grf_ref, grb_ref, gbc_ref, gbr_ref, *refs):
    init_refs = refs[0:3] if has_init else None
    hf_ref, hb_ref, cout_ref, nout_ref, mout_ref, c_scr, n_scr, m_scr = refs[3:] if has_init else refs
    step = pl.program_id(1)
    L = MLSTM_CHUNK
    dh = MLSTM_HEAD_DIM
    H = MLSTM_HEADS

    @pl.when(step == 0)
    def _():
        for scr, k in zip((c_scr, n_scr, m_scr), range(3)):
            scr[...] = init_refs[k][0] if has_init else jnp.zeros(scr.shape, F32)

    K = MLSTM_STEP_CHUNKS
    R = K * L
    row = lax.broadcasted_iota(jnp.int32, (L, L), 0)
    col = lax.broadcasted_iota(jnp.int32, (L, L), 1)
    lower = (row >= col)
    upper = (row <= col)
    lower_b = jnp.where(lower, 1.0, 0.0).astype(BF16)
    upper_b = jnp.where(upper, 1.0, 0.0).astype(BF16)
    rrow = lax.broadcasted_iota(jnp.int32, (R, R), 0)
    rcol = lax.broadcasted_iota(jnp.int32, (R, R), 1)
    same_chunk = jnp.bitwise_and(rrow, -L) == jnp.bitwise_and(rcol, -L)
    neg_inf = jnp.full((), -jnp.inf, F32)

    gcol, grow, bcol, brow, btot = [], [], [], [], []
    ones_b = jnp.ones((L, LANE), BF16)
    for d in range(N_DIR):
        gc_ref, gr_ref = (gcf_ref, grf_ref) if d == 0 else (gcb_ref, grb_ref)
        gcol.append(gc_ref[...] + gbc_ref[...])
        grow.append((gr_ref[...] + gbr_ref[...][None]).reshape(K * MLSTM_GATES, L))
        before = same_chunk & ((rrow >= rcol) if d == 0 else (rrow <= rcol))
        bcol.append(_mm_exact_lhs(jnp.where(before, 1.0, 0.0).astype(BF16), jax.nn.log_sigmoid(gcol[d]), _NN))
        frow = jax.nn.log_sigmoid(grow[d])
        brow.append(_mm_exact_rhs(frow, upper_b if d == 0 else lower_b, _NN))
        btot.append(_mm_exact_rhs(frow, ones_b, _NN))

    units = [(j, d, h) for j in range(K) for d in range(N_DIR) for h in range(H)]
    nun = range(len(units))
    q, k, kt, v, vb, qb = [], [], [], [], [], []
    c_row, b_col, b_last = [], [], []
    for j, d, h in units:
        at = j if d == 0 else K - 1 - j
        rows = slice(at * L, (at + 1) * L)
        st = d * H + h
        gi, gf = st, 2 * H + st
        qk_ref, kt_ref, v_ref = (qkf_ref, ktf_ref, vf_ref) if d == 0 else (qkb_ref, ktb_ref, vb_ref)
        q.append(qk_ref[rows, h * dh:(h + 1) * dh] * (dh ** -0.5))
        k.append(qk_ref[rows, MLSTM_WIDTH + h * dh:MLSTM_WIDTH + (h + 1) * dh])
        kt.append(kt_ref[at, h * dh:(h + 1) * dh, :])
        v.append(v_ref[rows, h * dh:(h + 1) * dh])
        qb.append(q[-1].astype(BF16))
        vb.append(v[-1].astype(BF16))
        b_col.append(jnp.broadcast_to(bcol[d][rows, gf:gf + 1], (L, LANE)))
        c_row.append(grow[d][at * MLSTM_GATES + gi:at * MLSTM_GATES + gi + 1, :]
                     - brow[d][at * MLSTM_GATES + gf:at * MLSTM_GATES + gf + 1, :])
        b_last.append(btot[d][at * MLSTM_GATES + gf:at * MLSTM_GATES + gf + 1, :])

    last = [L - 1 if d == 0 else 0 for _, d, _ in units]
    qk_t = [_dg(qb[i], k[i].astype(BF16), _NT) for i in nun]
    rel = [jnp.where(lower if units[i][1] == 0 else upper, c_row[i], neg_inf) for i in nun]
    mx = [jnp.broadcast_to(jnp.max(rel[i], axis=-1, keepdims=True), (L, LANE)) for i in nun]
    m_loc = [b_col[i] + mx[i] for i in nun]
    s_loc = [qk_t[i] * jnp.exp(rel[i] - mx[i][:, 0:L]) for i in nun]
    s_v = [_dg(s_loc[i].astype(BF16), vb[i], _NN) for i in nun]
    s_sum = [jnp.broadcast_to(jnp.sum(s_loc[i], axis=-1, keepdims=True), (L, LANE)) for i in nun]
    cmax = [mx[i][last[i]:last[i] + 1, :] for i in nun]
    m_w = [b_last[i] + cmax[i] for i in nun]
    wj = [jnp.exp(c_row[i] - cmax[i][:, 0:L]) for i in nun]
    kv = [_dg((kt[i] * wj[i]).astype(BF16), vb[i], _NN) for i in nun]
    w_k = [_mm(jnp.broadcast_to(wj[i], (8, L)), k[i], _NN, 3)[0:1] for i in nun]

    nst = N_DIR * H
    c_st = [c_scr[st] for st in range(nst)]
    n_st = [n_scr[st:st + 1, :] for st in range(nst)]
    m_st = [m_scr[st:st + 1, :] for st in range(nst)]
    for j in range(K):
        idx = [j * nst + st for st in range(nst)]
        q_c = [_dg(qb[i], c_st[st].astype(BF16), _NN) for st, i in enumerate(idx)]
        for st, i in enumerate(idx):
            _, d, h = units[i]
            at = j if d == 0 else K - 1 - j
            h_ref = hf_ref if d == 0 else hb_ref
            log_inter = b_col[i] + m_st[st]
            m_s = jnp.maximum(log_inter, m_loc[i])
            inter = jnp.exp(log_inter - m_s)
            local = jnp.exp(m_loc[i] - m_s)
            q_n = jnp.broadcast_to(jnp.sum(q[i] * n_st[st], axis=-1, keepdims=True), (L, LANE))
            den = inter * q_n + local * s_sum[i]
            scale = 1.0 / jnp.maximum(jnp.abs(den), jnp.exp(-m_s))
            h_ref[at * L:(at + 1) * L, h * dh:(h + 1) * dh] = (inter * scale) * q_c[st] + (local * scale) * s_v[i]
            m_new = jnp.maximum(b_last[i] + m_st[st], m_w[i])
            carry = jnp.exp(b_last[i] + m_st[st] - m_new)
            fresh = jnp.exp(m_w[i] - m_new)
            c_st[st] = carry * c_st[st] + fresh * kv[i]
            n_st[st] = carry * n_st[st] + fresh * w_k[i]
            m_st[st] = m_new


    for st in range(nst):
        c_scr[st] = c_st[st]
        n_scr[st:st + 1, :] = n_st[st]
        m_scr[st:st + 1, :] = m_st[st]
    nout_ref[0] = n_scr[...]
    mout_ref[0] = m_scr[...]

    @pl.when(step == pl.num_programs(1) - 1)
    def _():
        for st in range(nst):
            cout_ref[0, st] = c_st[st].T


def _mlstm_scan(z, qk, qk_blk, kt, gt, gate_bc, gate_br, c0, n0, m0, batch, seq_len):
    K = MLSTM_STEP_CHUNKS
    L = K * MLSTM_CHUNK
    assert seq_len % L == 0
    cps = seq_len // L
    n = batch * seq_len
    W = MLSTM_WIDTH
    nst = N_DIR * MLSTM_HEADS
    dh = MLSTM_HEAD_DIM

    def fw(b, c):
        return b * cps + c

    def bw(b, c):
        return b * cps + cps - 1 - c

    vblk = (ZM_OFF + 2 * W) // W
    gblk = ZG_OFF // LANE
    has_init = c0 is not None
    state_specs = [pl.BlockSpec((1, nst, dh, dh), lambda b, c: (b, 0, 0, 0)),
                   pl.BlockSpec((1, nst, dh), lambda b, c: (b, 0, 0)),
                   pl.BlockSpec((1, nst, LANE), lambda b, c: (b, 0, 0))]
    return pl.pallas_call(
        functools.partial(_mlstm_scan_kernel, has_init),
        grid=(batch, cps),
        in_specs=[pl.BlockSpec((L, 2 * W), lambda b, c: (fw(b, c), qk_blk)),
                  pl.BlockSpec((L, 2 * W), lambda b, c: (bw(b, c), qk_blk)),
                  pl.BlockSpec((K, W, MLSTM_CHUNK), lambda b, c: (fw(b, c), 0, 0)),
                  pl.BlockSpec((K, W, MLSTM_CHUNK), lambda b, c: (bw(b, c), 0, 0)),
                  pl.BlockSpec((L, W), lambda b, c: (fw(b, c), vblk)),
                  pl.BlockSpec((L, W), lambda b, c: (bw(b, c), vblk)),
                  pl.BlockSpec((L, LANE), lambda b, c: (fw(b, c), gblk)),
                  pl.BlockSpec((L, LANE), lambda b, c: (bw(b, c), gblk)),
                  pl.BlockSpec((K, MLSTM_GATES, MLSTM_CHUNK), lambda b, c: (fw(b, c), 0, 0)),
                  pl.BlockSpec((K, MLSTM_GATES, MLSTM_CHUNK), lambda b, c: (bw(b, c), 0, 0)),
                  _resident((1, LANE)),
                  _resident((MLSTM_GATES, 1))] + (state_specs if has_init else []),
        out_specs=[pl.BlockSpec((L, W), lambda b, c: (fw(b, c), 0)),
                   pl.BlockSpec((L, W), lambda b, c: (bw(b, c), 0))] + state_specs,
        out_shape=[jax.ShapeDtypeStruct((n, W), F32), jax.ShapeDtypeStruct((n, W), F32),
                   jax.ShapeDtypeStruct((batch, nst, dh, dh), F32),
                   jax.ShapeDtypeStruct((batch, nst, dh), F32),
                   jax.ShapeDtypeStruct((batch, nst, LANE), F32)],
        scratch_shapes=[pltpu.VMEM((nst, dh, dh), F32), pltpu.VMEM((nst, dh), F32),
                        pltpu.VMEM((nst, LANE), F32)],
        compiler_params=_params(("arbitrary", "arbitrary")),
        name="mlstm_scan",
    )(qk, qk, kt, kt, z, z, z, z, gt, gt, gate_bc, gate_br, *([c0, n0, m0] if has_init else []))


def _grid_conv(above, cur, below, cw_ref, width):
    T = cur.shape[0]
    E = T + 2 * width
    ext = jnp.concatenate([above, cur, below], axis=0)
    colp = jnp.bitwise_and(lax.broadcasted_iota(jnp.int32, (E, 1), 0), width - 1)
    left = jnp.where(colp == 0, 0.0, pltpu.roll(ext, 1, 0))
    right = jnp.where(colp == width - 1, 0.0, pltpu.roll(ext, E - 1, 0))

    def tap_row(i):
        rows = slice(i * width, i * width + T)
        return (cw_ref[3 * i:3 * i + 1, :] * left[rows] + cw_ref[3 * i + 1:3 * i + 2, :] * ext[rows]
                + cw_ref[3 * i + 2:3 * i + 3, :] * right[rows])

    return tap_row(0) + tap_row(1) + tap_row(2)


def _merge_kernel(conv, x_ref, mod_ref, ysf_ref, ysb_ref, bonus_ref, gate_ref, hf_ref, hb_ref, zo_ref,
                  zs_ref, lnxg_ref, lnxb_ref, gng_ref, pmean_ref, wbr_ref, wbm_ref, wout_ref, ng_ref, wup_ref,
                  cw_ref, cb_ref, x1_ref, act_ref, *scratch):
    if conv[0] == "grid":
        @pl.when(pl.program_id(0) == 0)
        def _():
            for ref in scratch:
                ref[...] = jnp.zeros(ref.shape, F32)

    mod = mod_ref[0]
    g1 = mod[:, 2 * D_MODEL:3 * D_MODEL]
    sh2 = mod[:, 3 * D_MODEL:4 * D_MODEL]
    sc2 = mod[:, 4 * D_MODEL:5 * D_MODEL]

    ys = ysf_ref[...] + ysb_ref[...]
    pmean = pmean_ref[...]
    mean = _mm_exact_rhs(ys, pmean, _NN)
    cen = ys - mean
    var = _mm_exact_rhs(cen * cen, pmean, _NN)
    y_r = (cen * lax.rsqrt(var + RWKV_GN_EPS) * lnxg_ref[...] + lnxb_ref[...] + bonus_ref[...]) * gate_ref[...]

    hs = hf_ref[...] + hb_ref[...]
    parts = []
    for h in range(MLSTM_HEADS):
        hh = hs[:, h * MLSTM_HEAD_DIM:(h + 1) * MLSTM_HEAD_DIM]
        mu = jnp.mean(hh, axis=-1, keepdims=True)
        ce = hh - mu
        va = jnp.mean(ce * ce, axis=-1, keepdims=True)
        parts.append(ce * lax.rsqrt(va + MLSTM_GN_EPS))
    y_m = jnp.concatenate(parts, axis=1) * gng_ref[...] * _sigmoid(zo_ref[...])

    gates = zs_ref[...]
    merged = (gates[:, 0:D_MODEL] * _dg(y_r.astype(BF16), wbr_ref[...], _NN)
              + gates[:, D_MODEL:2 * D_MODEL] * _dg(y_m.astype(BF16), wbm_ref[...], _NN))
    t = _dg(merged.astype(BF16), wout_ref[...], _NN)
    x1 = x_ref[...] + g1 * _rms(t, ng_ref[1:2, :])
    x1_ref[...] = x1
    h2 = _rms(x1, ng_ref[2:3, :]) * (1.0 + sc2) + sh2
    u = _dg(h2.astype(BF16), wup_ref[...], _NN)
    if conv[0] == "seq":
        pre = _dwconv(u[:, 0:D_FF], cw_ref, conv[1], False) + cb_ref[...]
        act_ref[...] = (_silu(pre) * u[:, D_FF:2 * D_FF]).astype(BF16)
    else:
        _, width, tiles_per_image = conv
        act_scr, val_scr, tail_scr = scratch
        T = act_scr.shape[0]
        step = pl.program_id(0)
        pos = (step + tiles_per_image - 1) % tiles_per_image
        for c0 in range(0, D_FF, CONV_CH_TILE):
            ch = slice(c0, c0 + CONV_CH_TILE)
            above = jnp.where(pos != 0, tail_scr[:, ch], 0.0)
            below = jnp.where(pos != tiles_per_image - 1, u[0:width, ch], 0.0)
            pre = _grid_conv(above, act_scr[:, ch], below, cw_ref.at[:, ch], width) + cb_ref[:, ch]
            act_ref[:, ch] = (_silu(pre) * val_scr[:, ch]).astype(BF16)
        tail_scr[...] = act_scr[T - width:T, :]
        act_scr[...] = u[:, 0:D_FF]
        val_scr[...] = u[:, D_FF:2 * D_FF]


def _merge(x2, mod, mod_row, z, ysf, ysb, bonus, gate, hf, hb, p, conv):
    n = x2.shape[0]
    W = RWKV_WIDTH
    rows = MERGE_TILE
    ntiles = n // rows
    delayed = conv[0] == "grid"
    cur = (lambda i: jnp.minimum(i, ntiles - 1)) if delayed else (lambda i: i)
    tile = lambda w: pl.BlockSpec((rows, w), lambda i: (cur(i), 0))
    act_spec = pl.BlockSpec((rows, D_FF), (lambda i: (jnp.maximum(i - 1, 0), 0)) if delayed else (lambda i: (i, 0)))
    scratch = ([pltpu.VMEM((rows, D_FF), F32), pltpu.VMEM((rows, D_FF), F32), pltpu.VMEM((conv[1], D_FF), F32)]
               if delayed else [])
    return pl.pallas_call(
        functools.partial(_merge_kernel, conv),
        grid=(ntiles + 1 if delayed else ntiles,),
        in_specs=[tile(D_MODEL),
                  pl.BlockSpec((1, 1, 6 * D_MODEL), lambda i: (mod_row(cur(i) * rows), 0, 0)),
                  tile(W), tile(W), tile(W), tile(W), tile(MLSTM_WIDTH), tile(MLSTM_WIDTH),
                  pl.BlockSpec((rows, MLSTM_WIDTH),
                               lambda i: (cur(i), (ZM_OFF + 3 * MLSTM_WIDTH) // MLSTM_WIDTH)),
                  pl.BlockSpec((rows, GATE_COLS), lambda i: (cur(i), ZS_OFF // GATE_COLS)),
                  _resident((1, W)), _resident((1, W)), _resident((1, MLSTM_WIDTH)),
                  _resident((W, W)),
                  _resident((W, D_MODEL)), _resident((MLSTM_WIDTH, D_MODEL)),
                  _resident((D_MODEL, D_MODEL)), _resident((4, D_MODEL)),
                  _resident((D_MODEL, 2 * D_FF)), _resident((9, D_FF)), _resident((1, D_FF))],
        out_specs=[tile(D_MODEL), act_spec],
        out_shape=[jax.ShapeDtypeStruct((n, D_MODEL), F32), jax.ShapeDtypeStruct((n, D_FF), BF16)],
        scratch_shapes=scratch,
        compiler_params=_params(("arbitrary",)),
        name="merge_ffn_up",
    )(x2, mod, ysf, ysb, bonus, gate, hf, hb, z, z, p["lnx_g"], p["lnx_b"], p["gn_g"], p["pmean"],
      p["w_br"], p["w_bm"], p["w_out"], p["norm_g"], p["ffn_up"], p["ffn_conv"], p["ffn_conv_b"])


def _down_kernel(x1_ref, mod_ref, a_ref, w_ref, ng_ref, o_ref):
    g2 = mod_ref[0][:, 5 * D_MODEL:6 * D_MODEL]
    f = _dg(a_ref[...], w_ref[...], _NN)
    o_ref[...] = x1_ref[...] + g2 * _rms(f, ng_ref[3:4, :])


def _down(x1, mod, mod_row, act, p):
    n = x1.shape[0]
    tile = DOWN_TILE
    return pl.pallas_call(
        _down_kernel,
        grid=(n // tile,),
        in_specs=[pl.BlockSpec((tile, D_MODEL), lambda i: (i, 0)),
                  pl.BlockSpec((1, 1, 6 * D_MODEL), lambda i: (mod_row(i * tile), 0, 0)),
                  pl.BlockSpec((tile, D_FF), lambda i: (i, 0)),
                  _resident((D_FF, D_MODEL)), _resident((4, D_MODEL))],
        out_specs=pl.BlockSpec((tile, D_MODEL), lambda i: (i, 0)),
        out_shape=jax.ShapeDtypeStruct((n, D_MODEL), F32),
        compiler_params=_params(("arbitrary",)),
        name="ffn_down",
    )(x1, mod, act, p["ffn_down"], p["norm_g"])


RWKV_LOCAL_PASSES = 1


def _state_to_pairs(s):
    b = s.shape[0]
    s = s.reshape(b, N_DIR, RWKV_PAIRS, 2, RWKV_HEAD_DIM, RWKV_HEAD_DIM)
    zero = jnp.zeros_like(s[:, :, :, 0])
    top = jnp.concatenate([s[:, :, :, 0], zero], axis=-1)
    bot = jnp.concatenate([zero, s[:, :, :, 1]], axis=-1)
    return jnp.concatenate([top, bot], axis=-2)


def _trunk(x, mod, mod_row, rows, states, p):
    batch, seq_len, _ = x.shape
    n = batch * seq_len
    x2 = x.reshape(n, D_MODEL)
    nst = N_DIR * MLSTM_HEADS
    if states is None:
        s0 = c0 = n0 = m0 = None
    else:
        s0, c0, n0, m0 = states
        s0 = _state_to_pairs(s0)
        c0 = jnp.swapaxes(c0, -1, -2).reshape(batch, nst, MLSTM_HEAD_DIM, MLSTM_HEAD_DIM)
        n0 = n0.reshape(batch, nst, MLSTM_HEAD_DIM)
        m0 = jnp.broadcast_to(m0.reshape(batch, nst, 1), (batch, nst, LANE))

    fuse_width = seq_len if (rows == 1 and IN_TILE % seq_len == 0 and MERGE_TILE % seq_len == 0) else None
    proj = _in_proj(x2, mod, mod_row, p["norm_g"][0:1], p["w_in"], p["mlstm_conv"], fuse_width)
    z = proj[0]

    rp, y0, gm, hm, gate, bonus = _rwkv_local(z, seq_len, p, RWKV_LOCAL_PASSES)
    ysf, ysb, s_fin = _rwkv_scan(s0, rp, y0, gm, hm, batch, seq_len)

    if fuse_width is None:
        z, gt = proj
        qk, kt = _qk_conv(z, batch, seq_len, rows, p["mlstm_conv"])
        qk_blk = 0
    else:
        z, gt, kt = proj
        qk, qk_blk = z, ZM_OFF // (2 * MLSTM_WIDTH)
    hf, hb, c_fin, n_fin, m_fin = _mlstm_scan(z, qk, qk_blk, kt, gt, p["gate_bc"], p["gate_br"],
                                              c0, n0, m0, batch, seq_len)

    if fuse_width is not None:
        ffn_conv = ("seq", fuse_width)
    else:
        width = seq_len // rows
        assert rows > 1 and width & (width - 1) == 0 and MERGE_TILE % width == 0 and seq_len % MERGE_TILE == 0
        ffn_conv = ("grid", width, seq_len // MERGE_TILE)
    x1, act = _merge(x2, mod, mod_row, z, ysf, ysb, bonus, gate, hf, hb, p, ffn_conv)
    out = _down(x1, mod, mod_row, act, p)

    new_states = (s_fin,
                  c_fin.reshape(batch, N_DIR, MLSTM_HEADS, MLSTM_HEAD_DIM, MLSTM_HEAD_DIM),
                  n_fin.reshape(batch, N_DIR, MLSTM_HEADS, MLSTM_HEAD_DIM),
                  m_fin[:, :, 0].reshape(batch, N_DIR, MLSTM_HEADS))
    return out.reshape(batch, seq_len, D_MODEL), new_states


def _pack_layer(l, ada_w, ada_b, norm_g, w_in, rwkv_mu, rwkv_w0, rwkv_w_up, rwkv_a0, rwkv_a_up,
                rwkv_g_up, rwkv_kk_scale, rwkv_k_a, rwkv_r_k, rwkv_lnx_g, rwkv_lnx_b, mlstm_conv,
                mlstm_gate_b, mlstm_gn_g, w_branch_rwkv, w_branch_mlstm, w_out, ffn_up, ffn_conv,
                ffn_conv_b, ffn_down):
    W = RWKV_WIDTH
    w_in_b = w_in[l].astype(BF16)

    head = jnp.arange(W, dtype=jnp.int32) // RWKV_HEAD_DIM
    same = (head[:, None] == head[None, :])
    gb = mlstm_gate_b[l].reshape(1, MLSTM_GATES)
    return dict(
        ada_w=ada_w[l], ada_b=ada_b[l], norm_g=norm_g[l], w_in=w_in_b,
        mu=rwkv_mu[l].reshape(1, RWKV_COLS),
        w0=rwkv_w0[l].reshape(N_DIR, 1, W), w_up=rwkv_w_up[l],
        a0=rwkv_a0[l].reshape(N_DIR, 1, W), a_up=rwkv_a_up[l], g_up=rwkv_g_up[l],
        kk_scale=rwkv_kk_scale[l].reshape(1, W), k_a=rwkv_k_a[l].reshape(1, W),
        r_k=rwkv_r_k[l].reshape(1, W),
        lnx_g=rwkv_lnx_g[l].reshape(1, W), lnx_b=rwkv_lnx_b[l].reshape(1, W),
        pones=same.astype(BF16), pmean=(same.astype(F32) / RWKV_HEAD_DIM).astype(BF16),
        mlstm_conv=mlstm_conv[l].reshape(9, 2 * MLSTM_WIDTH),
        gate_bc=jnp.pad(gb, ((0, 0), (0, LANE - MLSTM_GATES))), gate_br=gb.reshape(MLSTM_GATES, 1),
        gn_g=mlstm_gn_g[l].reshape(1, MLSTM_WIDTH),
        w_br=w_branch_rwkv[l].astype(BF16), w_bm=w_branch_mlstm[l].astype(BF16),
        w_out=w_out[l].astype(BF16), ffn_up=ffn_up[l].astype(BF16),
        ffn_conv=ffn_conv[l].reshape(9, D_FF), ffn_conv_b=ffn_conv_b[l].reshape(1, D_FF),
        ffn_down=ffn_down[l].astype(BF16),
    )


def kernel(x_prompt, x_sample, c, state_rwkv, state_mlstm_C, state_mlstm_n, state_mlstm_m, c_ctx,
           ada_w, ada_b, norm_g, w_in, rwkv_mu, rwkv_w0, rwkv_w_up, rwkv_a0, rwkv_a_up, rwkv_g_up,
           rwkv_kk_scale, rwkv_k_a, rwkv_r_k, rwkv_lnx_g, rwkv_lnx_b, mlstm_conv, mlstm_gate_b,
           mlstm_gn_g, w_branch_rwkv, w_branch_mlstm, w_out, ffn_up, ffn_conv, ffn_conv_b, ffn_down):
    depth = ada_w.shape[0]
    batch = x_prompt.shape[0]
    dec_batch, dec_seq, _ = x_sample.shape
    latent_rows = dec_seq // GRID_W
    cond = jnp.concatenate([c_ctx[None, :], c, jnp.zeros((8 - 1 - dec_batch, D_MODEL), F32)], axis=0)

    xp, xs = x_prompt, x_sample
    new_s, new_c, new_n, new_m = [], [], [], []
    for l in range(depth):
        p = _pack_layer(l, ada_w, ada_b, norm_g, w_in, rwkv_mu, rwkv_w0, rwkv_w_up, rwkv_a0, rwkv_a_up,
                        rwkv_g_up, rwkv_kk_scale, rwkv_k_a, rwkv_r_k, rwkv_lnx_g, rwkv_lnx_b, mlstm_conv,
                        mlstm_gate_b, mlstm_gn_g, w_branch_rwkv, w_branch_mlstm, w_out, ffn_up, ffn_conv,
                        ffn_conv_b, ffn_down)
        mod = _ada(cond, p["ada_w"], p["ada_b"]).reshape(8, 1, 6 * D_MODEL)
        xp, (s, cc, nn, mm) = _trunk(xp, mod, lambda r: 0, 1, None, p)
        new_s.append(s)
        new_c.append(cc)
        new_n.append(nn)
        new_m.append(mm)
        xs, _ = _trunk(xs, mod, lambda r: 1 + r // dec_seq, latent_rows,
                       (state_rwkv[:, l], state_mlstm_C[:, l], state_mlstm_n[:, l], state_mlstm_m[:, l]), p)
    return (xp, xs, jnp.stack(new_s, axis=1), jnp.stack(new_c, axis=1),
            jnp.stack(new_n, axis=1), jnp.stack(new_m, axis=1))
```

```python
import functools

import jax
import jax.numpy as jnp
from jax import lax
from jax.experimental import pallas as pl
from jax.experimental.pallas import tpu as pltpu

F32 = jnp.float32
BF16 = jnp.bfloat16

D_MODEL = 1024
N_DIR = 2
RWKV_HEADS = 8
RWKV_HEAD_DIM = 64
RWKV_WIDTH = RWKV_HEADS * RWKV_HEAD_DIM
DECAY_LORA = 64
ICLR_LORA = 64
GATE_LORA = 128
MLSTM_HEADS = 4
MLSTM_HEAD_DIM = 128
MLSTM_WIDTH = MLSTM_HEADS * MLSTM_HEAD_DIM
MLSTM_CHUNK = 64
D_FF = 2816
GRID_W = 64
RMS_EPS = 1e-6
RWKV_GN_EPS = 64e-5
MLSTM_GN_EPS = 1e-5
DECAY_SCALE = 0.606531

RWKV_COLS = 3 * RWKV_WIDTH + N_DIR * DECAY_LORA + N_DIR * ICLR_LORA + GATE_LORA
MLSTM_GATES = 2 * N_DIR * MLSTM_HEADS
MLSTM_COLS = 4 * MLSTM_WIDTH + MLSTM_GATES
GATE_COLS = 2 * D_MODEL

LANE = 128
ZR_BLOCK = 2048
ZG_OFF = RWKV_COLS
ZM_OFF = ZR_BLOCK
ZS_OFF = ZM_OFF + 4 * MLSTM_WIDTH
Z_COLS = ZS_OFF + GATE_COLS

IN_TILE = 512
MERGE_TILE = 256
DOWN_TILE = 512
RCHUNK = 64
CONV_CH_TILE = 256
VMEM_LIMIT = 56 * 1024 * 1024


def _params(sem):
    return pltpu.CompilerParams(dimension_semantics=sem, vmem_limit_bytes=VMEM_LIMIT)


def _resident(shape):
    nd = len(shape)
    return pl.BlockSpec(shape, lambda *_: (0,) * nd, pipeline_mode=pl.Buffered(1))


def _split2(a):
    hi = a.astype(BF16)
    lo = (a - hi.astype(F32)).astype(BF16)
    return hi, lo


def _split3(a):
    hi = a.astype(BF16)
    r1 = a - hi.astype(F32)
    mid = r1.astype(BF16)
    lo = (r1 - mid.astype(F32)).astype(BF16)
    return hi, mid, lo


def _dg(a, b, dims):
    return lax.dot_general(a, b, dims, preferred_element_type=F32)


def _mm(a, b, dims, passes):
    if passes == 1:
        return _dg(a.astype(BF16), b.astype(BF16), dims)
    ah, al = _split2(a)
    bh, bl = _split2(b)
    return _dg(ah, bh, dims) + (_dg(ah, bl, dims) + _dg(al, bh, dims))


def _mm_exact_lhs(a_bf16, b, dims):
    b1, b2, b3 = _split3(b)
    return _dg(a_bf16, b1, dims) + (_dg(a_bf16, b2, dims) + _dg(a_bf16, b3, dims))


def _mm_exact_rhs(a, b_bf16, dims, pieces=3):
    if pieces == 2:
        a1, a2 = _split2(a)
        return _dg(a1, b_bf16, dims) + _dg(a2, b_bf16, dims)
    a1, a2, a3 = _split3(a)
    return _dg(a1, b_bf16, dims) + (_dg(a2, b_bf16, dims) + _dg(a3, b_bf16, dims))


_NN = (((1,), (0,)), ((), ()))
_NT = (((1,), (1,)), ((), ()))
_TN = (((0,), (0,)), ((), ()))
_BNN = (((2,), (1,)), ((0,), (0,)))
_BNT = (((2,), (2,)), ((0,), (0,)))
_BTN = (((1,), (1,)), ((0,), (0,)))


def _sigmoid(x):
    return jax.nn.sigmoid(x)


def _silu(x):
    return x * jax.nn.sigmoid(x)


def _rms(x, g):
    return x * lax.rsqrt(jnp.mean(x * x, axis=-1, keepdims=True) + RMS_EPS) * g


def _ada_kernel(cond_ref, w_ref, b_ref, o_ref):
    s = _silu(cond_ref[...])
    o_ref[...] = _dg(s.astype(BF16), w_ref[...].astype(BF16), _NN) + b_ref[...]


def _ada(cond8, ada_w, ada_b):
    n = ada_w.shape[1]
    tn = 1536
    return pl.pallas_call(
        _ada_kernel,
        grid=(n // tn,),
        in_specs=[_resident((8, D_MODEL)),
                  pl.BlockSpec((D_MODEL, tn), lambda j: (0, j)),
                  pl.BlockSpec((1, tn), lambda j: (0, j))],
        out_specs=pl.BlockSpec((8, tn), lambda j: (0, j)),
        out_shape=jax.ShapeDtypeStruct((8, n), F32),
        compiler_params=_params(("arbitrary",)),
        name="ada_mod",
    )(cond8, ada_w, ada_b.reshape(1, n))


def _in_kernel(conv_width, x_ref, mod_ref, g_ref, w_ref, cw_ref, z_ref, gt_ref, *kt_ref):
    mod = mod_ref[0]
    sh = mod[:, 0:D_MODEL]
    sc = mod[:, D_MODEL:2 * D_MODEL]
    h = (_rms(x_ref[...], g_ref[...]) * (1.0 + sc) + sh).astype(BF16)
    m_off = RWKV_COLS
    t_off = RWKV_COLS + 4 * MLSTM_WIDTH
    z_ref[:, 0:ZG_OFF] = _dg(h, w_ref[:, 0:m_off], _NN)
    tail = _dg(h, w_ref[:, t_off:w_ref.shape[1]], _NN)
    zg = tail[:, 0:LANE]
    z_ref[:, ZG_OFF:ZM_OFF] = zg
    zgt = zg.T
    for c in range(zg.shape[0] // MLSTM_CHUNK):
        gt_ref[c] = zgt[0:MLSTM_GATES, c * MLSTM_CHUNK:(c + 1) * MLSTM_CHUNK]
    z_ref[:, ZS_OFF:Z_COLS] = _sigmoid(tail[:, MLSTM_GATES:MLSTM_GATES + GATE_COLS])
    zm = _dg(h, w_ref[:, m_off:t_off], _NN)
    if conv_width is None:
        z_ref[:, ZM_OFF:ZS_OFF] = zm
    else:
        qk_cols = 2 * MLSTM_WIDTH
        qk = _silu(_dwconv(zm[:, 0:qk_cols], cw_ref, conv_width, False))
        z_ref[:, ZM_OFF:ZM_OFF + qk_cols] = qk
        z_ref[:, ZM_OFF + qk_cols:ZS_OFF] = zm[:, qk_cols:]
        _store_time_on_lanes(qk[:, MLSTM_WIDTH:qk_cols], kt_ref[0], MLSTM_CHUNK)


def _in_proj(x2, mod, mod_row, norm_g0, w_in, conv_w9, conv_width):
    n = x2.shape[0]
    tile = IN_TILE
    cpt = tile // MLSTM_CHUNK
    nchunk = n // MLSTM_CHUNK
    out_specs = [pl.BlockSpec((tile, Z_COLS), lambda i: (i, 0)),
                 pl.BlockSpec((cpt, MLSTM_GATES, MLSTM_CHUNK), lambda i: (i, 0, 0))]
    out_shape = [jax.ShapeDtypeStruct((n, Z_COLS), F32),
                 jax.ShapeDtypeStruct((nchunk, MLSTM_GATES, MLSTM_CHUNK), F32)]
    if conv_width is not None:
        out_specs.append(pl.BlockSpec((cpt, MLSTM_WIDTH, MLSTM_CHUNK), lambda i: (i, 0, 0)))
        out_shape.append(jax.ShapeDtypeStruct((nchunk, MLSTM_WIDTH, MLSTM_CHUNK), F32))
    return pl.pallas_call(
        functools.partial(_in_kernel, conv_width),
        grid=(n // tile,),
        in_specs=[pl.BlockSpec((tile, D_MODEL), lambda i: (i, 0)),
                  pl.BlockSpec((1, 1, 6 * D_MODEL), lambda i: (mod_row(i * tile), 0, 0)),
                  _resident((1, D_MODEL)), _resident(w_in.shape), _resident(conv_w9.shape)],
        out_specs=out_specs,
        out_shape=out_shape,
        compiler_params=_params(("arbitrary",)),
        name="in_proj",
    )(x2, mod, norm_g0, w_in, conv_w9)


LOCAL_CHUNKS = 4
PAIR_LANES = 2 * RWKV_HEAD_DIM
RWKV_PAIRS = RWKV_HEADS // 2


def _bd(x):
    lane = lax.broadcasted_iota(jnp.int32, x.shape, 1)
    left = lane < RWKV_HEAD_DIM
    return jnp.concatenate([jnp.where(left, x, 0.0), jnp.where(left, 0.0, x)], axis=0)


def _rwkv_local_kernel(chunks_per_seq, passes,
                       z_ref, zp_ref, zn_ref, mu_ref, w0_ref, wup_ref, a0_ref, aup_ref, gup_ref,
                       kks_ref, ka_ref, rk_ref, pones_ref,
                       rp_ref, y0_ref, gm_ref, hm_ref, gate_ref, bonus_ref):
    C = RCHUNK
    W = RWKV_WIDTH
    NS = LOCAL_CHUNKS
    R = NS * C
    first = (pl.program_id(0) * NS) % chunks_per_seq
    has_prev = first != 0
    has_next = first + NS != chunks_per_seq

    z = z_ref[:, 0:RWKV_COLS]
    zp = jnp.where(has_prev, zp_ref[7:8, 0:RWKV_COLS], 0.0)
    zn = jnp.where(has_next, zn_ref[0:1, 0:RWKV_COLS], 0.0)
    trow = lax.broadcasted_iota(jnp.int32, (R, 1), 0)
    prev = jnp.where(trow == 0, zp, pltpu.roll(z, 1, 0))
    nxt = jnp.where(trow == R - 1, zn, pltpu.roll(z, R - 1, 0))
    zs = z + mu_ref[...] * (0.5 * (prev + nxt) - z)

    r = zs[:, 0:W]
    k = zs[:, W:2 * W]
    v = zs[:, 2 * W:3 * W]
    gd = zs[:, 3 * W + 2 * DECAY_LORA + 2 * ICLR_LORA:RWKV_COLS]
    gate_ref[...] = _dg(_sigmoid(gd).astype(BF16), gup_ref[...].astype(BF16), _NN)

    pones = pones_ref[...]
    kks = k * kks_ref[...]
    norm = jnp.sqrt(_mm_exact_rhs(kks * kks, pones, _NN, pieces=2))
    kk = kks / jnp.maximum(norm, 1e-12)

    P = PAIR_LANES
    row = lax.broadcasted_iota(jnp.int32, (R, R), 0)
    col = lax.broadcasted_iota(jnp.int32, (R, R), 1)
    same_chunk = jnp.bitwise_and(row, -C) == jnp.bitwise_and(col, -C)
    prow = lax.broadcasted_iota(jnp.int32, (C, P), 0)
    pcol = jnp.bitwise_and(lax.broadcasted_iota(jnp.int32, (C, P), 1), RWKV_HEAD_DIM - 1)
    eye_p = jnp.where(prow == pcol, 1.0, 0.0)
    left_head = lax.broadcasted_iota(jnp.int32, (C, P), 1) < RWKV_HEAD_DIM

    def diag_blocks(m):
        return jnp.where(left_head, m[0:RWKV_HEAD_DIM], m[RWKV_HEAD_DIM:P])

    abar, rbar, kt, bt, kw, bw, wc, strict, incl = [], [], [], [], [], [], [], [], []
    kd_sum = None
    for d in range(N_DIR):
        o = 3 * W + d * DECAY_LORA
        wd = zs[:, o:o + DECAY_LORA]
        o = 3 * W + 2 * DECAY_LORA + d * ICLR_LORA
        ad = zs[:, o:o + ICLR_LORA]
        logw = -DECAY_SCALE * _sigmoid(w0_ref[d] + _dg(jnp.tanh(wd).astype(BF16), wup_ref[d].astype(BF16), _NN))
        a = _sigmoid(a0_ref[d] + _dg(ad.astype(BF16), aup_ref[d].astype(BF16), _NN))
        kd = k * (1.0 + (a - 1.0) * ka_ref[...])
        b = kk * a
        kd_sum = kd if kd_sum is None else kd_sum + kd

        earlier_or_same = same_chunk & ((row >= col) if d == 0 else (row <= col))
        cum_i = _mm_exact_lhs(jnp.where(earlier_or_same, 1.0, 0.0).astype(BF16), logw, _NN)
        cum_e = cum_i - logw
        ab_d, rb_d, kt_d, bt_d, kw_d, bw_d, wc_d = [], [], [], [], [], [], []
        for s in range(NS):
            rs = slice(s * C, (s + 1) * C)
            ci_s = cum_i[rs]
            ctot = jnp.sum(logw[rs], axis=0, keepdims=True)
            e_ni = jnp.exp(-ci_s)
            e_ti = jnp.exp(ctot - ci_s)
            ab_d.append(kk[rs] * jnp.exp(cum_e[rs]))
            rb_d.append(r[rs] * jnp.exp(ci_s))
            kt_d.append(kd[rs] * e_ni)
            bt_d.append(b[rs] * e_ni)
            kw_d.append(kd[rs] * e_ti)
            bw_d.append(b[rs] * e_ti)
            wc_d.append(jnp.exp(ctot))
        abar.append(ab_d)
        rbar.append(rb_d)
        kt.append(kt_d)
        bt.append(bt_d)
        kw.append(kw_d)
        bw.append(bw_d)
        wc.append(wc_d)
        strict.append((prow > pcol) if d == 0 else (prow < pcol))
        incl.append((prow >= pcol) if d == 0 else (prow <= pcol))
    bonus_ref[...] = _mm_exact_rhs(r * kd_sum * rk_ref[...], pones, _NN, pieces=2) * v

    mm = functools.partial(_mm, passes=passes)
    chains = [(s, d, p) for s in range(NS) for d in range(N_DIR) for p in range(RWKV_PAIRS)]
    nch = range(len(chains))

    def sel(arr, i):
        s, d, p = chains[i]
        return arr[d][s][:, p * P:(p + 1) * P]

    cat0 = lambda a_, b_: jnp.concatenate([a_, b_], axis=0)
    cat1 = lambda a_, b_: jnp.concatenate([a_, b_], axis=1)
    vsl = [v[s * C:(s + 1) * C, p * P:(p + 1) * P] for s, _, p in chains]
    lhs = [cat0(sel(abar, i), sel(rbar, i)) for i in nch]
    by = [mm(lhs[i], cat0(_bd(sel(bt, i)), _bd(sel(kt, i))), _NT) for i in nch]
    a_kk = [jnp.where(strict[chains[i][1]], by[i][0:C, 0:P], 0.0) for i in nch]
    a_rb = [jnp.where(incl[chains[i][1]], by[i][C:2 * C, 0:P], 0.0) for i in nch]
    a_kv = [jnp.where(strict[chains[i][1]], by[i][0:C, P:2 * P], 0.0) for i in nch]
    a_rk = [jnp.where(incl[chains[i][1]], by[i][C:2 * C, P:2 * P], 0.0) for i in nch]
    on_v = [mm(cat0(a_kv[i], a_rk[i]), _bd(vsl[i]), _NN) for i in nch]

    x = [-m for m in a_kk]
    tinv = [eye_p + m for m in x]
    x = [mm(m, _bd(m), _NN) for m in x]
    for _ in range(4):
        both = [mm(cat0(tinv[i], x[i]), _bd(x[i]), _NN) for i in nch]
        tinv = [tinv[i] + both[i][0:C] for i in nch]
        x = [m[C:2 * C] for m in both]
    tinv = [tinv[i] + mm(tinv[i], _bd(x[i]), _NN) for i in nch]

    solved = [mm(tinv[i], cat1(_bd(sel(abar, i)), _bd(on_v[i][0:C])), _NN) for i in nch]
    ap = [m[:, 0:P] for m in solved]
    u0 = [m[:, P:2 * P] for m in solved]
    corr = [mm(a_rb[i], cat1(_bd(ap[i]), _bd(u0[i])), _NN) for i in nch]
    on_b = [mm(cat1(ap[i], u0[i]), sel(bw, i), _TN) for i in nch]
    vk = [mm(vsl[i], sel(kw, i), _TN) for i in nch]
    for i in nch:
        s, d, p = chains[i]
        rows = slice(s * C, (s + 1) * C)
        lanes = slice(p * P, (p + 1) * P)
        rp_ref[d, rows, lanes] = (sel(rbar, i) - corr[i][:, 0:P]).astype(BF16)
        y0_ref[d, rows, lanes] = on_v[i][C:2 * C] - corr[i][:, P:2 * P]
        gm_ref[d, s, p] = (eye_p * sel(wc, i) - diag_blocks(on_b[i][0:P])).astype(BF16)
        hm_ref[d, s, p] = diag_blocks(vk[i] - on_b[i][P:2 * P])


def _rwkv_local(z, seq_len, p, passes):
    n = z.shape[0]
    nchunk = n // RCHUNK
    cps = seq_len // RCHUNK
    assert cps % LOCAL_CHUNKS == 0
    W = RWKV_WIDTH
    rows = LOCAL_CHUNKS * RCHUNK
    hb = rows // 8
    last8 = n // 8 - 1
    mat = lambda dt: jax.ShapeDtypeStruct((N_DIR, nchunk, RWKV_PAIRS, RWKV_HEAD_DIM, PAIR_LANES), dt)
    mat_spec = pl.BlockSpec((N_DIR, LOCAL_CHUNKS, RWKV_PAIRS, RWKV_HEAD_DIM, PAIR_LANES),
                            lambda c: (0, c, 0, 0, 0))
    tok = lambda dt: jax.ShapeDtypeStruct((N_DIR, n, W), dt)
    tok_spec = pl.BlockSpec((N_DIR, rows, W), lambda c: (0, c, 0))
    row_spec = pl.BlockSpec((rows, W), lambda c: (c, 0))
    return pl.pallas_call(
        functools.partial(_rwkv_local_kernel, cps, passes),
        grid=(nchunk // LOCAL_CHUNKS,),
        in_specs=[pl.BlockSpec((rows, ZR_BLOCK), lambda c: (c, 0)),
                  pl.BlockSpec((8, ZR_BLOCK), lambda c: (jnp.maximum(c * hb - 1, 0), 0)),
                  pl.BlockSpec((8, ZR_BLOCK), lambda c: (jnp.minimum((c + 1) * hb, last8), 0)),
                  _resident((1, RWKV_COLS)),
                  _resident((N_DIR, 1, W)), _resident((N_DIR, DECAY_LORA, W)),
                  _resident((N_DIR, 1, W)), _resident((N_DIR, ICLR_LORA, W)),
                  _resident((GATE_LORA, W)),
                  _resident((1, W)), _resident((1, W)), _resident((1, W)),
                  _resident((W, W))],
        out_specs=[tok_spec, tok_spec, mat_spec, mat_spec, row_spec, row_spec],
        out_shape=[tok(BF16), tok(F32), mat(BF16), mat(F32),
                   jax.ShapeDtypeStruct((n, W), F32), jax.ShapeDtypeStruct((n, W), F32)],
        compiler_params=_params(("arbitrary",)),
        name="rwkv_local",
    )(z, z, z, p["mu"], p["w0"], p["w_up"], p["a0"], p["a_up"], p["g_up"],
      p["kk_scale"], p["k_a"], p["r_k"], p["pones"])


SCAN_CHUNKS = 4


def _rwkv_scan_kernel(has_init, *refs):
    s0_ref = refs[0] if has_init else None
    (rpf_ref, rpb_ref, y0f_ref, y0b_ref, gmf_ref, gmb_ref, hmf_ref, hmb_ref,
     ysf_ref, ysb_ref, sout_ref, s_scr) = refs[1:] if has_init else refs

    @pl.when(pl.program_id(1) == 0)
    def _():
        s_scr[...] = s0_ref[0] if has_init else jnp.zeros(s_scr.shape, F32)

    K = SCAN_CHUNKS
    C = RCHUNK
    rp_ref, y0_ref, gm_ref, hm_ref, ys_ref = ((rpf_ref, rpb_ref), (y0f_ref, y0b_ref), (gmf_ref, gmb_ref),
                                              (hmf_ref, hmb_ref), (ysf_ref, ysb_ref))
    chains = [(d, p) for d in range(N_DIR) for p in range(RWKV_PAIRS)]
    lanes = [slice(p * PAIR_LANES, (p + 1) * PAIR_LANES) for _, p in chains]
    nch = range(len(chains))
    s = [s_scr[d, p] for d, p in chains]
    for j in range(K):
        at = (j, K - 1 - j)
        rows = [slice(at[d] * C, (at[d] + 1) * C) for d, _ in chains]
        sb = [m.astype(BF16) for m in s]
        y = [_dg(rp_ref[chains[i][0]][0, rows[i], lanes[i]], sb[i], _NT) for i in nch]
        sg = [_dg(sb[i], _bd(gm_ref[chains[i][0]][0, at[chains[i][0]], chains[i][1]]), _NN) for i in nch]
        for i in nch:
            d, p = chains[i]
            ys_ref[d][rows[i], lanes[i]] = y[i] + y0_ref[d][0, rows[i], lanes[i]]
        s = [sg[i] + _bd(hm_ref[chains[i][0]][0, at[chains[i][0]], chains[i][1]]) for i in nch]
    for i in nch:
        d, p = chains[i]
        s_scr[d, p] = s[i]
        n = RWKV_HEAD_DIM
        sout_ref[0, d, 2 * p] = s[i][0:n, 0:n]
        sout_ref[0, d, 2 * p + 1] = s[i][n:2 * n, n:2 * n]


def _rwkv_scan(s0, rp, y0, gm, hm, batch, seq_len):
    K = SCAN_CHUNKS
    spb = seq_len // (RCHUNK * K)
    n = batch * seq_len

    def fwd(b, s):
        return b * spb + s

    def bwd(b, s):
        return b * spb + spb - 1 - s

    def mat_spec(d, at):
        return pl.BlockSpec((1, K, RWKV_PAIRS, RWKV_HEAD_DIM, PAIR_LANES), lambda b, s: (d, at(b, s), 0, 0, 0))

    def tok_spec(d, at):
        return pl.BlockSpec((1, K * RCHUNK, RWKV_WIDTH), lambda b, s: (d, at(b, s), 0))

    st_spec = pl.BlockSpec((1, N_DIR, RWKV_PAIRS, PAIR_LANES, PAIR_LANES), lambda b, s: (b, 0, 0, 0, 0))
    ys = jax.ShapeDtypeStruct((n, RWKV_WIDTH), F32)
    has_init = s0 is not None
    return pl.pallas_call(
        functools.partial(_rwkv_scan_kernel, has_init),
        grid=(batch, spb),
        in_specs=([st_spec] if has_init else [])
        + [tok_spec(0, fwd), tok_spec(1, bwd), tok_spec(0, fwd), tok_spec(1, bwd),
           mat_spec(0, fwd), mat_spec(1, bwd), mat_spec(0, fwd), mat_spec(1, bwd)],
        out_specs=[pl.BlockSpec((K * RCHUNK, RWKV_WIDTH), lambda b, s: (fwd(b, s), 0)),
                   pl.BlockSpec((K * RCHUNK, RWKV_WIDTH), lambda b, s: (bwd(b, s), 0)),
                   pl.BlockSpec((1, N_DIR, RWKV_HEADS, RWKV_HEAD_DIM, RWKV_HEAD_DIM),
                                lambda b, s: (b, 0, 0, 0, 0))],
        out_shape=[ys, ys,
                   jax.ShapeDtypeStruct((batch, N_DIR, RWKV_HEADS, RWKV_HEAD_DIM, RWKV_HEAD_DIM), F32)],
        scratch_shapes=[pltpu.VMEM((N_DIR, RWKV_PAIRS, PAIR_LANES, PAIR_LANES), F32)],
        compiler_params=_params(("arbitrary", "arbitrary")),
        name="rwkv_scan",
    )(*([s0] if has_init else []), rp, rp, y0, y0, gm, gm, hm, hm)


CONV_BLOCK_ROWS = 2048


def _dwconv(x, w_ref, width, vertical):
    T = x.shape[0]
    t = lax.broadcasted_iota(jnp.int32, (T, 1), 0)
    assert width & (width - 1) == 0
    colp = jnp.bitwise_and(t, width - 1)
    xl = jnp.where(colp == 0, 0.0, pltpu.roll(x, 1, 0))
    xr = jnp.where(colp == width - 1, 0.0, pltpu.roll(x, T - 1, 0))

    def tap_row(i):
        return w_ref[3 * i:3 * i + 1, :] * xl + w_ref[3 * i + 1:3 * i + 2, :] * x + w_ref[3 * i + 2:3 * i + 3, :] * xr

    out = tap_row(1)
    if vertical:
        out = out + jnp.where(t < width, 0.0, pltpu.roll(tap_row(0), width, 0))
        out = out + jnp.where(t >= T - width, 0.0, pltpu.roll(tap_row(2), T - width, 0))
    return out


def _conv_geometry(n, seq_len, rows):
    if rows > 1:
        return seq_len, seq_len // rows, True
    block = CONV_BLOCK_ROWS if (n % CONV_BLOCK_ROWS == 0 and CONV_BLOCK_ROWS % seq_len == 0) else seq_len
    return block, seq_len, False


def _store_time_on_lanes(x, out_ref, chunk):
    xt = x.T
    for c in range(x.shape[0] // chunk):
        out_ref[c] = xt[:, c * chunk:(c + 1) * chunk]


def _qk_conv_kernel(width, vertical, first_k_tile, x_ref, w_ref, o_ref, kt_ref):
    out = _silu(_dwconv(x_ref[...], w_ref, width, vertical))
    o_ref[...] = out

    @pl.when(pl.program_id(1) >= first_k_tile)
    def _():
        _store_time_on_lanes(out, kt_ref, MLSTM_CHUNK)


def _qk_conv(z, batch, seq_len, rows, conv_w9):
    n = batch * seq_len
    ch = 2 * MLSTM_WIDTH
    tc = CONV_CH_TILE
    off = ZM_OFF // tc
    first_k = MLSTM_WIDTH // tc
    block, width, vertical = _conv_geometry(n, seq_len, rows)
    cpb = block // MLSTM_CHUNK
    return pl.pallas_call(
        functools.partial(_qk_conv_kernel, width, vertical, first_k),
        grid=(n // block, ch // tc),
        in_specs=[pl.BlockSpec((block, tc), lambda b, j: (b, off + j)),
                  pl.BlockSpec((9, tc), lambda b, j: (0, j))],
        out_specs=[pl.BlockSpec((block, tc), lambda b, j: (b, j)),
                   pl.BlockSpec((cpb, tc, MLSTM_CHUNK), lambda b, j: (b, jnp.maximum(j - first_k, 0), 0))],
        out_shape=[jax.ShapeDtypeStruct((n, ch), F32),
                   jax.ShapeDtypeStruct((n // MLSTM_CHUNK, MLSTM_WIDTH, MLSTM_CHUNK), F32)],
        compiler_params=_params(("arbitrary", "arbitrary")),
        name="mlstm_qk_conv",
    )(z, conv_w9)


MLSTM_STEP_CHUNKS = 4

def _mlstm_scan_kernel(has_init, qkf_ref, qkb_ref, ktf_ref, ktb_ref, vf_ref, vb_ref, gcf_ref, gcb_ref,
                       grf_ref, grb_ref, gbc_ref, gbr_ref, *refs):
    init_refs = refs[0:3] if has_init else None
    hf_ref, hb_ref, cout_ref, nout_ref, mout_ref, c_scr, n_scr, m_scr = refs[3:] if has_init else refs
    step = pl.program_id(1)
    L = MLSTM_CHUNK
    dh = MLSTM_HEAD_DIM
    H = MLSTM_HEADS

    @pl.when(step == 0)
    def _():
        for scr, k in zip((c_scr, n_scr, m_scr), range(3)):
            scr[...] = init_refs[k][0] if has_init else jnp.zeros(scr.shape, F32)

    K = MLSTM_STEP_CHUNKS
    R = K * L
    row = lax.broadcasted_iota(jnp.int32, (L, L), 0)
    col = lax.broadcasted_iota(jnp.int32, (L, L), 1)
    lower = (row >= col)
    upper = (row <= col)
    lower_b = jnp.where(lower, 1.0, 0.0).astype(BF16)
    upper_b = jnp.where(upper, 1.0, 0.0).astype(BF16)
    rrow = lax.broadcasted_iota(jnp.int32, (R, R), 0)
    rcol = lax.broadcasted_iota(jnp.int32, (R, R), 1)
    same_chunk = jnp.bitwise_and(rrow, -L) == jnp.bitwise_and(rcol, -L)
    neg_inf = jnp.full((), -jnp.inf, F32)

    gcol, grow, bcol, brow, btot = [], [], [], [], []
    ones_b = jnp.ones((L, LANE), BF16)
    for d in range(N_DIR):
        gc_ref, gr_ref = (gcf_ref, grf_ref) if d == 0 else (gcb_ref, grb_ref)
        gcol.append(gc_ref[...] + gbc_ref[...])
        grow.append((gr_ref[...] + gbr_ref[...][None]).reshape(K * MLSTM_GATES, L))
        before = same_chunk & ((rrow >= rcol) if d == 0 else (rrow <= rcol))
        bcol.append(_mm_exact_lhs(jnp.where(before, 1.0, 0.0).astype(BF16), jax.nn.log_sigmoid(gcol[d]), _NN))
        frow = jax.nn.log_sigmoid(grow[d])
        brow.append(_mm_exact_rhs(frow, upper_b if d == 0 else lower_b, _NN))
        btot.append(_mm_exact_rhs(frow, ones_b, _NN))

    units = [(j, d, h) for j in range(K) for d in range(N_DIR) for h in range(H)]
    nun = range(len(units))
    q, k, kt, v, vb, qb = [], [], [], [], [], []
    c_row, b_col, b_last = [], [], []
    for j, d, h in units:
        at = j if d == 0 else K - 1 - j
        rows = slice(at * L, (at + 1) * L)
        st = d * H + h
        gi, gf = st, 2 * H + st
        qk_ref, kt_ref, v_ref = (qkf_ref, ktf_ref, vf_ref) if d == 0 else (qkb_ref, ktb_ref, vb_ref)
        q.append(qk_ref[rows, h * dh:(h + 1) * dh] * (dh ** -0.5))
        k.append(qk_ref[rows, MLSTM_WIDTH + h * dh:MLSTM_WIDTH + (h + 1) * dh])
        kt.append(kt_ref[at, h * dh:(h + 1) * dh, :])
        v.append(v_ref[rows, h * dh:(h + 1) * dh])
        qb.append(q[-1].astype(BF16))
        vb.append(v[-1].astype(BF16))
        b_col.append(jnp.broadcast_to(bcol[d][rows, gf:gf + 1], (L, LANE)))
        c_row.append(grow[d][at * MLSTM_GATES + gi:at * MLSTM_GATES + gi + 1, :]
                     - brow[d][at * MLSTM_GATES + gf:at * MLSTM_GATES + gf + 1, :])
        b_last.append(btot[d][at * MLSTM_GATES + gf:at * MLSTM_GATES + gf + 1, :])

    last = [L - 1 if d == 0 else 0 for _, d, _ in units]
    qk_t = [_dg(qb[i], k[i].astype(BF16), _NT) for i in nun]
    rel = [jnp.where(lower if units[i][1] == 0 else upper, c_row[i], neg_inf) for i in nun]
    mx = [jnp.broadcast_to(jnp.max(rel[i], axis=-1, keepdims=True), (L, LANE)) for i in nun]
    m_loc = [b_col[i] + mx[i] for i in nun]
    s_loc = [qk_t[i] * jnp.exp(rel[i] - mx[i][:, 0:L]) for i in nun]
    s_v = [_dg(s_loc[i].astype(BF16), vb[i], _NN) for i in nun]
    s_sum = [jnp.broadcast_to(jnp.sum(s_loc[i], axis=-1, keepdims=True), (L, LANE)) for i in nun]
    cmax = [mx[i][last[i]:last[i] + 1, :] for i in nun]
    m_w = [b_last[i] + cmax[i] for i in nun]
    wj = [jnp.exp(c_row[i] - cmax[i][:, 0:L]) for i in nun]
    kv = [_dg((kt[i] * wj[i]).astype(BF16), vb[i], _NN) for i in nun]
    w_k = [_mm(jnp.broadcast_to(wj[i], (8, L)), k[i], _NN, 3)[0:1] for i in nun]

    nst = N_DIR * H
    c_st = [c_scr[st] for st in range(nst)]
    n_st = [n_scr[st:st + 1, :] for st in range(nst)]
    m_st = [m_scr[st:st + 1, :] for st in range(nst)]
    for j in range(K):
        idx = [j * nst + st for st in range(nst)]
        q_c = [_dg(qb[i], c_st[st].astype(BF16), _NN) for st, i in enumerate(idx)]
        for st, i in enumerate(idx):
            _, d, h = units[i]
            at = j if d == 0 else K - 1 - j
            h_ref = hf_ref if d == 0 else hb_ref
            log_inter = b_col[i] + m_st[st]
            m_s = jnp.maximum(log_inter, m_loc[i])
            inter = jnp.exp(log_inter - m_s)
            local = jnp.exp(m_loc[i] - m_s)
            q_n = jnp.broadcast_to(jnp.sum(q[i] * n_st[st], axis=-1, keepdims=True), (L, LANE))
            den = inter * q_n + local * s_sum[i]
            scale = 1.0 / jnp.maximum(jnp.abs(den), jnp.exp(-m_s))
            h_ref[at * L:(at + 1) * L, h * dh:(h + 1) * dh] = (inter * scale) * q_c[st] + (local * scale) * s_v[i]
            m_new = jnp.maximum(b_last[i] + m_st[st], m_w[i])
            carry = jnp.exp(b_last[i] + m_st[st] - m_new)
            fresh = jnp.exp(m_w[i] - m_new)
            c_st[st] = carry * c_st[st] + fresh * kv[i]
            n_st[st] = carry * n_st[st] + fresh * w_k[i]
            m_st[st] = m_new


    for st in range(nst):
        c_scr[st] = c_st[st]
        n_scr[st:st + 1, :] = n_st[st]
        m_scr[st:st + 1, :] = m_st[st]
    nout_ref[0] = n_scr[...]
    mout_ref[0] = m_scr[...]

    @pl.when(step == pl.num_programs(1) - 1)
    def _():
        for st in range(nst):
            cout_ref[0, st] = c_st[st].T


def _mlstm_scan(z, qk, qk_blk, kt, gt, gate_bc, gate_br, c0, n0, m0, batch, seq_len):
    K = MLSTM_STEP_CHUNKS
    L = K * MLSTM_CHUNK
    assert seq_len % L == 0
    cps = seq_len // L
    n = batch * seq_len
    W = MLSTM_WIDTH
    nst = N_DIR * MLSTM_HEADS
    dh = MLSTM_HEAD_DIM

    def fw(b, c):
        return b * cps + c

    def bw(b, c):
        return b * cps + cps - 1 - c

    vblk = (ZM_OFF + 2 * W) // W
    gblk = ZG_OFF // LANE
    has_init = c0 is not None
    state_specs = [pl.BlockSpec((1, nst, dh, dh), lambda b, c: (b, 0, 0, 0)),
                   pl.BlockSpec((1, nst, dh), lambda b, c: (b, 0, 0)),
                   pl.BlockSpec((1, nst, LANE), lambda b, c: (b, 0, 0))]
    return pl.pallas_call(
        functools.partial(_mlstm_scan_kernel, has_init),
        grid=(batch, cps),
        in_specs=[pl.BlockSpec((L, 2 * W), lambda b, c: (fw(b, c), qk_blk)),
                  pl.BlockSpec((L, 2 * W), lambda b, c: (bw(b, c), qk_blk)),
                  pl.BlockSpec((K, W, MLSTM_CHUNK), lambda b, c: (fw(b, c), 0, 0)),
                  pl.BlockSpec((K, W, MLSTM_CHUNK), lambda b, c: (bw(b, c), 0, 0)),
                  pl.BlockSpec((L, W), lambda b, c: (fw(b, c), vblk)),
                  pl.BlockSpec((L, W), lambda b, c: (bw(b, c), vblk)),
                  pl.BlockSpec((L, LANE), lambda b, c: (fw(b, c), gblk)),
                  pl.BlockSpec((L, LANE), lambda b, c: (bw(b, c), gblk)),
                  pl.BlockSpec((K, MLSTM_GATES, MLSTM_CHUNK), lambda b, c: (fw(b, c), 0, 0)),
                  pl.BlockSpec((K, MLSTM_GATES, MLSTM_CHUNK), lambda b, c: (bw(b, c), 0, 0)),
                  _resident((1, LANE)),
                  _resident((MLSTM_GATES, 1))] + (state_specs if has_init else []),
        out_specs=[pl.BlockSpec((L, W), lambda b, c: (fw(b, c), 0)),
                   pl.BlockSpec((L, W), lambda b, c: (bw(b, c), 0))] + state_specs,
        out_shape=[jax.ShapeDtypeStruct((n, W), F32), jax.ShapeDtypeStruct((n, W), F32),
                   jax.ShapeDtypeStruct((batch, nst, dh, dh), F32),
                   jax.ShapeDtypeStruct((batch, nst, dh), F32),
                   jax.ShapeDtypeStruct((batch, nst, LANE), F32)],
        scratch_shapes=[pltpu.VMEM((nst, dh, dh), F32), pltpu.VMEM((nst, dh), F32),
                        pltpu.VMEM((nst, LANE), F32)],
        compiler_params=_params(("arbitrary", "arbitrary")),
        name="mlstm_scan",
    )(qk, qk, kt, kt, z, z, z, z, gt, gt, gate_bc, gate_br, *([c0, n0, m0] if has_init else []))


def _grid_conv(above, cur, below, cw_ref, width):
    T = cur.shape[0]
    E = T + 2 * width
    ext = jnp.concatenate([above, cur, below], axis=0)
    colp = jnp.bitwise_and(lax.broadcasted_iota(jnp.int32, (E, 1), 0), width - 1)
    left = jnp.where(colp == 0, 0.0, pltpu.roll(ext, 1, 0))
    right = jnp.where(colp == width - 1, 0.0, pltpu.roll(ext, E - 1, 0))

    def tap_row(i):
        rows = slice(i * width, i * width + T)
        return (cw_ref[3 * i:3 * i + 1, :] * left[rows] + cw_ref[3 * i + 1:3 * i + 2, :] * ext[rows]
                + cw_ref[3 * i + 2:3 * i + 3, :] * right[rows])

    return tap_row(0) + tap_row(1) + tap_row(2)


def _merge_kernel(conv, x_ref, mod_ref, ysf_ref, ysb_ref, bonus_ref, gate_ref, hf_ref, hb_ref, zo_ref,
                  zs_ref, lnxg_ref, lnxb_ref, gng_ref, pmean_ref, wbr_ref, wbm_ref, wout_ref, ng_ref, wup_ref,
                  cw_ref, cb_ref, x1_ref, act_ref, *scratch):
    if conv[0] == "grid":
        @pl.when(pl.program_id(0) == 0)
        def _():
            for ref in scratch:
                ref[...] = jnp.zeros(ref.shape, F32)

    mod = mod_ref[0]
    g1 = mod[:, 2 * D_MODEL:3 * D_MODEL]
    sh2 = mod[:, 3 * D_MODEL:4 * D_MODEL]
    sc2 = mod[:, 4 * D_MODEL:5 * D_MODEL]

    ys = ysf_ref[...] + ysb_ref[...]
    pmean = pmean_ref[...]
    mean = _mm_exact_rhs(ys, pmean, _NN)
    cen = ys - mean
    var = _mm_exact_rhs(cen * cen, pmean, _NN)
    y_r = (cen * lax.rsqrt(var + RWKV_GN_EPS) * lnxg_ref[...] + lnxb_ref[...] + bonus_ref[...]) * gate_ref[...]

    hs = hf_ref[...] + hb_ref[...]
    parts = []
    for h in range(MLSTM_HEADS):
        hh = hs[:, h * MLSTM_HEAD_DIM:(h + 1) * MLSTM_HEAD_DIM]
        mu = jnp.mean(hh, axis=-1, keepdims=True)
        ce = hh - mu
        va = jnp.mean(ce * ce, axis=-1, keepdims=True)
        parts.append(ce * lax.rsqrt(va + MLSTM_GN_EPS))
    y_m = jnp.concatenate(parts, axis=1) * gng_ref[...] * _sigmoid(zo_ref[...])

    gates = zs_ref[...]
    merged = (gates[:, 0:D_MODEL] * _dg(y_r.astype(BF16), wbr_ref[...], _NN)
              + gates[:, D_MODEL:2 * D_MODEL] * _dg(y_m.astype(BF16), wbm_ref[...], _NN))
    t = _dg(merged.astype(BF16), wout_ref[...], _NN)
    x1 = x_ref[...] + g1 * _rms(t, ng_ref[1:2, :])
    x1_ref[...] = x1
    h2 = _rms(x1, ng_ref[2:3, :]) * (1.0 + sc2) + sh2
    u = _dg(h2.astype(BF16), wup_ref[...], _NN)
    if conv[0] == "seq":
        pre = _dwconv(u[:, 0:D_FF], cw_ref, conv[1], False) + cb_ref[...]
        act_ref[...] = (_silu(pre) * u[:, D_FF:2 * D_FF]).astype(BF16)
    else:
        _, width, tiles_per_image = conv
        act_scr, val_scr, tail_scr = scratch
        T = act_scr.shape[0]
        step = pl.program_id(0)
        pos = (step + tiles_per_image - 1) % tiles_per_image
        for c0 in range(0, D_FF, CONV_CH_TILE):
            ch = slice(c0, c0 + CONV_CH_TILE)
            above = jnp.where(pos != 0, tail_scr[:, ch], 0.0)
            below = jnp.where(pos != tiles_per_image - 1, u[0:width, ch], 0.0)
            pre = _grid_conv(above, act_scr[:, ch], below, cw_ref.at[:, ch], width) + cb_ref[:, ch]
            act_ref[:, ch] = (_silu(pre) * val_scr[:, ch]).astype(BF16)
        tail_scr[...] = act_scr[T - width:T, :]
        act_scr[...] = u[:, 0:D_FF]
        val_scr[...] = u[:, D_FF:2 * D_FF]


def _merge(x2, mod, mod_row, z, ysf, ysb, bonus, gate, hf, hb, p, conv):
    n = x2.shape[0]
    W = RWKV_WIDTH
    rows = MERGE_TILE
    ntiles = n // rows
    delayed = conv[0] == "grid"
    cur = (lambda i: jnp.minimum(i, ntiles - 1)) if delayed else (lambda i: i)
    tile = lambda w: pl.BlockSpec((rows, w), lambda i: (cur(i), 0))
    act_spec = pl.BlockSpec((rows, D_FF), (lambda i: (jnp.maximum(i - 1, 0), 0)) if delayed else (lambda i: (i, 0)))
    scratch = ([pltpu.VMEM((rows, D_FF), F32), pltpu.VMEM((rows, D_FF), F32), pltpu.VMEM((conv[1], D_FF), F32)]
               if delayed else [])
    return pl.pallas_call(
        functools.partial(_merge_kernel, conv),
        grid=(ntiles + 1 if delayed else ntiles,),
        in_specs=[tile(D_MODEL),
                  pl.BlockSpec((1, 1, 6 * D_MODEL), lambda i: (mod_row(cur(i) * rows), 0, 0)),
                  tile(W), tile(W), tile(W), tile(W), tile(MLSTM_WIDTH), tile(MLSTM_WIDTH),
                  pl.BlockSpec((rows, MLSTM_WIDTH),
                               lambda i: (cur(i), (ZM_OFF + 3 * MLSTM_WIDTH) // MLSTM_WIDTH)),
                  pl.BlockSpec((rows, GATE_COLS), lambda i: (cur(i), ZS_OFF // GATE_COLS)),
                  _resident((1, W)), _resident((1, W)), _resident((1, MLSTM_WIDTH)),
                  _resident((W, W)),
                  _resident((W, D_MODEL)), _resident((MLSTM_WIDTH, D_MODEL)),
                  _resident((D_MODEL, D_MODEL)), _resident((4, D_MODEL)),
                  _resident((D_MODEL, 2 * D_FF)), _resident((9, D_FF)), _resident((1, D_FF))],
        out_specs=[tile(D_MODEL), act_spec],
        out_shape=[jax.ShapeDtypeStruct((n, D_MODEL), F32), jax.ShapeDtypeStruct((n, D_FF), BF16)],
        scratch_shapes=scratch,
        compiler_params=_params(("arbitrary",)),
        name="merge_ffn_up",
    )(x2, mod, ysf, ysb, bonus, gate, hf, hb, z, z, p["lnx_g"], p["lnx_b"], p["gn_g"], p["pmean"],
      p["w_br"], p["w_bm"], p["w_out"], p["norm_g"], p["ffn_up"], p["ffn_conv"], p["ffn_conv_b"])


def _down_kernel(x1_ref, mod_ref, a_ref, w_ref, ng_ref, o_ref):
    g2 = mod_ref[0][:, 5 * D_MODEL:6 * D_MODEL]
    f = _dg(a_ref[...], w_ref[...], _NN)
    o_ref[...] = x1_ref[...] + g2 * _rms(f, ng_ref[3:4, :])


def _down(x1, mod, mod_row, act, p):
    n = x1.shape[0]
    tile = DOWN_TILE
    return pl.pallas_call(
        _down_kernel,
        grid=(n // tile,),
        in_specs=[pl.BlockSpec((tile, D_MODEL), lambda i: (i, 0)),
                  pl.BlockSpec((1, 1, 6 * D_MODEL), lambda i: (mod_row(i * tile), 0, 0)),
                  pl.BlockSpec((tile, D_FF), lambda i: (i, 0)),
                  _resident((D_FF, D_MODEL)), _resident((4, D_MODEL))],
        out_specs=pl.BlockSpec((tile, D_MODEL), lambda i: (i, 0)),
        out_shape=jax.ShapeDtypeStruct((n, D_MODEL), F32),
        compiler_params=_params(("arbitrary",)),
        name="ffn_down",
    )(x1, mod, act, p["ffn_down"], p["norm_g"])


RWKV_LOCAL_PASSES = 1


def _state_to_pairs(s):
    b = s.shape[0]
    s = s.reshape(b, N_DIR, RWKV_PAIRS, 2, RWKV_HEAD_DIM, RWKV_HEAD_DIM)
    zero = jnp.zeros_like(s[:, :, :, 0])
    top = jnp.concatenate([s[:, :, :, 0], zero], axis=-1)
    bot = jnp.concatenate([zero, s[:, :, :, 1]], axis=-1)
    return jnp.concatenate([top, bot], axis=-2)


def _trunk(x, mod, mod_row, rows, states, p):
    batch, seq_len, _ = x.shape
    n = batch * seq_len
    x2 = x.reshape(n, D_MODEL)
    nst = N_DIR * MLSTM_HEADS
    if states is None:
        s0 = c0 = n0 = m0 = None
    else:
        s0, c0, n0, m0 = states
        s0 = _state_to_pairs(s0)
        c0 = jnp.swapaxes(c0, -1, -2).reshape(batch, nst, MLSTM_HEAD_DIM, MLSTM_HEAD_DIM)
        n0 = n0.reshape(batch, nst, MLSTM_HEAD_DIM)
        m0 = jnp.broadcast_to(m0.reshape(batch, nst, 1), (batch, nst, LANE))

    fuse_width = seq_len if (rows == 1 and IN_TILE % seq_len == 0 and MERGE_TILE % seq_len == 0) else None
    proj = _in_proj(x2, mod, mod_row, p["norm_g"][0:1], p["w_in"], p["mlstm_conv"], fuse_width)
    z = proj[0]

    rp, y0, gm, hm, gate, bonus = _rwkv_local(z, seq_len, p, RWKV_LOCAL_PASSES)
    ysf, ysb, s_fin = _rwkv_scan(s0, rp, y0, gm, hm, batch, seq_len)

    if fuse_width is None:
        z, gt = proj
        qk, kt = _qk_conv(z, batch, seq_len, rows, p["mlstm_conv"])
        qk_blk = 0
    else:
        z, gt, kt = proj
        qk, qk_blk = z, ZM_OFF // (2 * MLSTM_WIDTH)
    hf, hb, c_fin, n_fin, m_fin = _mlstm_scan(z, qk, qk_blk, kt, gt, p["gate_bc"], p["gate_br"],
                                              c0, n0, m0, batch, seq_len)

    if fuse_width is not None:
        ffn_conv = ("seq", fuse_width)
    else:
        width = seq_len // rows
        assert rows > 1 and width & (width - 1) == 0 and MERGE_TILE % width == 0 and seq_len % MERGE_TILE == 0
        ffn_conv = ("grid", width, seq_len // MERGE_TILE)
    x1, act = _merge(x2, mod, mod_row, z, ysf, ysb, bonus, gate, hf, hb, p, ffn_conv)
    out = _down(x1, mod, mod_row, act, p)

    new_states = (s_fin,
                  c_fin.reshape(batch, N_DIR, MLSTM_HEADS, MLSTM_HEAD_DIM, MLSTM_HEAD_DIM),
                  n_fin.reshape(batch, N_DIR, MLSTM_HEADS, MLSTM_HEAD_DIM),
                  m_fin[:, :, 0].reshape(batch, N_DIR, MLSTM_HEADS))
    return out.reshape(batch, seq_len, D_MODEL), new_states


def _pack_layer(l, ada_w, ada_b, norm_g, w_in, rwkv_mu, rwkv_w0, rwkv_w_up, rwkv_a0, rwkv_a_up,
                rwkv_g_up, rwkv_kk_scale, rwkv_k_a, rwkv_r_k, rwkv_lnx_g, rwkv_lnx_b, mlstm_conv,
                mlstm_gate_b, mlstm_gn_g, w_branch_rwkv, w_branch_mlstm, w_out, ffn_up, ffn_conv,
                ffn_conv_b, ffn_down):
    W = RWKV_WIDTH
    w_in_b = w_in[l].astype(BF16)

    head = jnp.arange(W, dtype=jnp.int32) // RWKV_HEAD_DIM
    same = (head[:, None] == head[None, :])
    gb = mlstm_gate_b[l].reshape(1, MLSTM_GATES)
    return dict(
        ada_w=ada_w[l], ada_b=ada_b[l], norm_g=norm_g[l], w_in=w_in_b,
        mu=rwkv_mu[l].reshape(1, RWKV_COLS),
        w0=rwkv_w0[l].reshape(N_DIR, 1, W), w_up=rwkv_w_up[l],
        a0=rwkv_a0[l].reshape(N_DIR, 1, W), a_up=rwkv_a_up[l], g_up=rwkv_g_up[l],
        kk_scale=rwkv_kk_scale[l].reshape(1, W), k_a=rwkv_k_a[l].reshape(1, W),
        r_k=rwkv_r_k[l].reshape(1, W),
        lnx_g=rwkv_lnx_g[l].reshape(1, W), lnx_b=rwkv_lnx_b[l].reshape(1, W),
        pones=same.astype(BF16), pmean=(same.astype(F32) / RWKV_HEAD_DIM).astype(BF16),
        mlstm_conv=mlstm_conv[l].reshape(9, 2 * MLSTM_WIDTH),
        gate_bc=jnp.pad(gb, ((0, 0), (0, LANE - MLSTM_GATES))), gate_br=gb.reshape(MLSTM_GATES, 1),
        gn_g=mlstm_gn_g[l].reshape(1, MLSTM_WIDTH),
        w_br=w_branch_rwkv[l].astype(BF16), w_bm=w_branch_mlstm[l].astype(BF16),
        w_out=w_out[l].astype(BF16), ffn_up=ffn_up[l].astype(BF16),
        ffn_conv=ffn_conv[l].reshape(9, D_FF), ffn_conv_b=ffn_conv_b[l].reshape(1, D_FF),
        ffn_down=ffn_down[l].astype(BF16),
    )


def kernel(x_prompt, x_sample, c, state_rwkv, state_mlstm_C, state_mlstm_n, state_mlstm_m, c_ctx,
           ada_w, ada_b, norm_g, w_in, rwkv_mu, rwkv_w0, rwkv_w_up, rwkv_a0, rwkv_a_up, rwkv_g_up,
           rwkv_kk_scale, rwkv_k_a, rwkv_r_k, rwkv_lnx_g, rwkv_lnx_b, mlstm_conv, mlstm_gate_b,
           mlstm_gn_g, w_branch_rwkv, w_branch_mlstm, w_out, ffn_up, ffn_conv, ffn_conv_b, ffn_down):
    depth = ada_w.shape[0]
    batch = x_prompt.shape[0]
    dec_batch, dec_seq, _ = x_sample.shape
    latent_rows = dec_seq // GRID_W
    cond = jnp.concatenate([c_ctx[None, :], c, jnp.zeros((8 - 1 - dec_batch, D_MODEL), F32)], axis=0)

    xp, xs = x_prompt, x_sample
    new_s, new_c, new_n, new_m = [], [], [], []
    for l in range(depth):
        p = _pack_layer(l, ada_w, ada_b, norm_g, w_in, rwkv_mu, rwkv_w0, rwkv_w_up, rwkv_a0, rwkv_a_up,
                        rwkv_g_up, rwkv_kk_scale, rwkv_k_a, rwkv_r_k, rwkv_lnx_g, rwkv_lnx_b, mlstm_conv,
                        mlstm_gate_b, mlstm_gn_g, w_branch_rwkv, w_branch_mlstm, w_out, ffn_up, ffn_conv,
                        ffn_conv_b, ffn_down)
        mod = _ada(cond, p["ada_w"], p["ada_b"]).reshape(8, 1, 6 * D_MODEL)
        xp, (s, cc, nn, mm) = _trunk(xp, mod, lambda r: 0, 1, None, p)
        new_s.append(s)
        new_c.append(cc)
        new_n.append(nn)
        new_m.append(mm)
        xs, _ = _trunk(xs, mod, lambda r: 1 + r // dec_seq, latent_rows,
                       (state_rwkv[:, l], state_mlstm_C[:, l], state_mlstm_n[:, l], state_mlstm_m[:, l]), p)
    return (xp, xs, jnp.stack(new_s, axis=1), jnp.stack(new_c, axis=1),
            jnp.stack(new_n, axis=1), jnp.stack(new_m, axis=1))
```

```python
import functools

import jax
import jax.numpy as jnp
from jax import lax
from jax.experimental import pallas as pl
from jax.experimental.pallas import tpu as pltpu

F32 = jnp.float32
BF16 = jnp.bfloat16

D_MODEL = 1024
N_DIR = 2
RWKV_HEADS = 8
RWKV_HEAD_DIM = 64
RWKV_WIDTH = RWKV_HEADS * RWKV_HEAD_DIM
DECAY_LORA = 64
ICLR_LORA = 64
GATE_LORA = 128
MLSTM_HEADS = 4
MLSTM_HEAD_DIM = 128
MLSTM_WIDTH = MLSTM_HEADS * MLSTM_HEAD_DIM
MLSTM_CHUNK = 64
D_FF = 2816
GRID_W = 64
RMS_EPS = 1e-6
RWKV_GN_EPS = 64e-5
MLSTM_GN_EPS = 1e-5
DECAY_SCALE = 0.606531

RWKV_COLS = 3 * RWKV_WIDTH + N_DIR * DECAY_LORA + N_DIR * ICLR_LORA + GATE_LORA
MLSTM_GATES = 2 * N_DIR * MLSTM_HEADS
MLSTM_COLS = 4 * MLSTM_WIDTH + MLSTM_GATES
GATE_COLS = 2 * D_MODEL

LANE = 128
ZR_BLOCK = 2048
ZG_OFF = RWKV_COLS
ZM_OFF = ZR_BLOCK
ZS_OFF = ZM_OFF + 4 * MLSTM_WIDTH
Z_COLS = ZS_OFF + GATE_COLS

IN_TILE = 512
MERGE_TILE = 256
DOWN_TILE = 512
RCHUNK = 64
CONV_CH_TILE = 256
VMEM_LIMIT = 56 * 1024 * 1024


def _params(sem):
    return pltpu.CompilerParams(dimension_semantics=sem, vmem_limit_bytes=VMEM_LIMIT)


def _resident(shape):
    nd = len(shape)
    return pl.BlockSpec(shape, lambda *_: (0,) * nd, pipeline_mode=pl.Buffered(1))


def _split2(a):
    hi = a.astype(BF16)
    lo = (a - hi.astype(F32)).astype(BF16)
    return hi, lo


def _split3(a):
    hi = a.astype(BF16)
    r1 = a - hi.astype(F32)
    mid = r1.astype(BF16)
    lo = (r1 - mid.astype(F32)).astype(BF16)
    return hi, mid, lo


def _dg(a, b, dims):
    return lax.dot_general(a, b, dims, preferred_element_type=F32)


def _mm(a, b, dims, passes):
    if passes == 1:
        return _dg(a.astype(BF16), b.astype(BF16), dims)
    ah, al = _split2(a)
    bh, bl = _split2(b)
    return _dg(ah, bh, dims) + (_dg(ah, bl, dims) + _dg(al, bh, dims))


def _mm_exact_lhs(a_bf16, b, dims):
    b1, b2, b3 = _split3(b)
    return _dg(a_bf16, b1, dims) + (_dg(a_bf16, b2, dims) + _dg(a_bf16, b3, dims))


def _mm_exact_rhs(a, b_bf16, dims, pieces=3):
    if pieces == 2:
        a1, a2 = _split2(a)
        return _dg(a1, b_bf16, dims) + _dg(a2, b_bf16, dims)
    a1, a2, a3 = _split3(a)
    return _dg(a1, b_bf16, dims) + (_dg(a2, b_bf16, dims) + _dg(a3, b_bf16, dims))


_NN = (((1,), (0,)), ((), ()))
_NT = (((1,), (1,)), ((), ()))
_TN = (((0,), (0,)), ((), ()))
_BNN = (((2,), (1,)), ((0,), (0,)))
_BNT = (((2,), (2,)), ((0,), (0,)))
_BTN = (((1,), (1,)), ((0,), (0,)))


def _sigmoid(x):
    return jax.nn.sigmoid(x)


def _silu(x):
    return x * jax.nn.sigmoid(x)


def _rms(x, g):
    return x * lax.rsqrt(jnp.mean(x * x, axis=-1, keepdims=True) + RMS_EPS) * g


def _ada_kernel(cond_ref, w_ref, b_ref, o_ref):
    s = _silu(cond_ref[...])
    o_ref[...] = _dg(s.astype(BF16), w_ref[...].astype(BF16), _NN) + b_ref[...]


def _ada(cond8, ada_w, ada_b):
    n = ada_w.shape[1]
    tn = 1536
    return pl.pallas_call(
        _ada_kernel,
        grid=(n // tn,),
        in_specs=[_resident((8, D_MODEL)),
                  pl.BlockSpec((D_MODEL, tn), lambda j: (0, j)),
                  pl.BlockSpec((1, tn), lambda j: (0, j))],
        out_specs=pl.BlockSpec((8, tn), lambda j: (0, j)),
        out_shape=jax.ShapeDtypeStruct((8, n), F32),
        compiler_params=_params(("arbitrary",)),
        name="ada_mod",
    )(cond8, ada_w, ada_b.reshape(1, n))


def _in_kernel(conv_width, x_ref, mod_ref, g_ref, w_ref, cw_ref, z_ref, gt_ref, *kt_ref):
    mod = mod_ref[0]
    sh = mod[:, 0:D_MODEL]
    sc = mod[:, D_MODEL:2 * D_MODEL]
    h = (_rms(x_ref[...], g_ref[...]) * (1.0 + sc) + sh).astype(BF16)
    m_off = RWKV_COLS
    t_off = RWKV_COLS + 4 * MLSTM_WIDTH
    z_ref[:, 0:ZG_OFF] = _dg(h, w_ref[:, 0:m_off], _NN)
    tail = _dg(h, w_ref[:, t_off:w_ref.shape[1]], _NN)
    zg = tail[:, 0:LANE]
    z_ref[:, ZG_OFF:ZM_OFF] = zg
    zgt = zg.T
    for c in range(zg.shape[0] // MLSTM_CHUNK):
        gt_ref[c] = zgt[0:MLSTM_GATES, c * MLSTM_CHUNK:(c + 1) * MLSTM_CHUNK]
    z_ref[:, ZS_OFF:Z_COLS] = _sigmoid(tail[:, MLSTM_GATES:MLSTM_GATES + GATE_COLS])
    zm = _dg(h, w_ref[:, m_off:t_off], _NN)
    if conv_width is None:
        z_ref[:, ZM_OFF:ZS_OFF] = zm
    else:
        qk_cols = 2 * MLSTM_WIDTH
        qk = _silu(_dwconv(zm[:, 0:qk_cols], cw_ref, conv_width, False))
        z_ref[:, ZM_OFF:ZM_OFF + qk_cols] = qk
        z_ref[:, ZM_OFF + qk_cols:ZS_OFF] = zm[:, qk_cols:]
        _store_time_on_lanes(qk[:, MLSTM_WIDTH:qk_cols], kt_ref[0], MLSTM_CHUNK)


def _in_proj(x2, mod, mod_row, norm_g0, w_in, conv_w9, conv_width):
    n = x2.shape[0]
    tile = IN_TILE
    cpt = tile // MLSTM_CHUNK
    nchunk = n // MLSTM_CHUNK
    out_specs = [pl.BlockSpec((tile, Z_COLS), lambda i: (i, 0)),
                 pl.BlockSpec((cpt, MLSTM_GATES, MLSTM_CHUNK), lambda i: (i, 0, 0))]
    out_shape = [jax.ShapeDtypeStruct((n, Z_COLS), F32),
                 jax.ShapeDtypeStruct((nchunk, MLSTM_GATES, MLSTM_CHUNK), F32)]
    if conv_width is not None:
        out_specs.append(pl.BlockSpec((cpt, MLSTM_WIDTH, MLSTM_CHUNK), lambda i: (i, 0, 0)))
        out_shape.append(jax.ShapeDtypeStruct((nchunk, MLSTM_WIDTH, MLSTM_CHUNK), F32))
    return pl.pallas_call(
        functools.partial(_in_kernel, conv_width),
        grid=(n // tile,),
        in_specs=[pl.BlockSpec((tile, D_MODEL), lambda i: (i, 0)),
                  pl.BlockSpec((1, 1, 6 * D_MODEL), lambda i: (mod_row(i * tile), 0, 0)),
                  _resident((1, D_MODEL)), _resident(w_in.shape), _resident(conv_w9.shape)],
        out_specs=out_specs,
        out_shape=out_shape,
        compiler_params=_params(("arbitrary",)),
        name="in_proj",
    )(x2, mod, norm_g0, w_in, conv_w9)


LOCAL_CHUNKS = 4
PAIR_LANES = 2 * RWKV_HEAD_DIM
RWKV_PAIRS = RWKV_HEADS // 2


def _bd(x):
    lane = lax.broadcasted_iota(jnp.int32, x.shape, 1)
    left = lane < RWKV_HEAD_DIM
    return jnp.concatenate([jnp.where(left, x, 0.0), jnp.where(left, 0.0, x)], axis=0)


def _rwkv_local_kernel(chunks_per_seq, passes, NS,
                       z_ref, zp_ref, zn_ref, mu_ref, w0_ref, wup_ref, a0_ref, aup_ref, gup_ref,
                       kks_ref, ka_ref, rk_ref, pones_ref,
                       rp_ref, y0_ref, gm_ref, hm_ref, gate_ref, bonus_ref):
    C = RCHUNK
    W = RWKV_WIDTH
    R = NS * C
    first = (pl.program_id(0) * NS) % chunks_per_seq
    has_prev = first != 0
    has_next = first + NS != chunks_per_seq

    z = z_ref[:, 0:RWKV_COLS]
    zp = jnp.where(has_prev, zp_ref[7:8, 0:RWKV_COLS], 0.0)
    zn = jnp.where(has_next, zn_ref[0:1, 0:RWKV_COLS], 0.0)
    trow = lax.broadcasted_iota(jnp.int32, (R, 1), 0)
    prev = jnp.where(trow == 0, zp, pltpu.roll(z, 1, 0))
    nxt = jnp.where(trow == R - 1, zn, pltpu.roll(z, R - 1, 0))
    zs = z + mu_ref[...] * (0.5 * (prev + nxt) - z)

    r = zs[:, 0:W]
    k = zs[:, W:2 * W]
    v = zs[:, 2 * W:3 * W]
    gd = zs[:, 3 * W + 2 * DECAY_LORA + 2 * ICLR_LORA:RWKV_COLS]
    gate_ref[...] = _dg(_sigmoid(gd).astype(BF16), gup_ref[...].astype(BF16), _NN)

    pones = pones_ref[...]
    kks = k * kks_ref[...]
    norm = jnp.sqrt(_mm_exact_rhs(kks * kks, pones, _NN, pieces=2))
    kk = kks / jnp.maximum(norm, 1e-12)

    P = PAIR_LANES
    row = lax.broadcasted_iota(jnp.int32, (R, R), 0)
    col = lax.broadcasted_iota(jnp.int32, (R, R), 1)
    same_chunk = jnp.bitwise_and(row, -C) == jnp.bitwise_and(col, -C)
    prow = lax.broadcasted_iota(jnp.int32, (C, P), 0)
    pcol = jnp.bitwise_and(lax.broadcasted_iota(jnp.int32, (C, P), 1), RWKV_HEAD_DIM - 1)
    eye_p = jnp.where(prow == pcol, 1.0, 0.0)
    left_head = lax.broadcasted_iota(jnp.int32, (C, P), 1) < RWKV_HEAD_DIM

    def diag_blocks(m):
        return jnp.where(left_head, m[0:RWKV_HEAD_DIM], m[RWKV_HEAD_DIM:P])

    abar, rbar, kt, bt, kw, bw, wc, strict, incl = [], [], [], [], [], [], [], [], []
    kd_sum = None
    for d in range(N_DIR):
        o = 3 * W + d * DECAY_LORA
        wd = zs[:, o:o + DECAY_LORA]
        o = 3 * W + 2 * DECAY_LORA + d * ICLR_LORA
        ad = zs[:, o:o + ICLR_LORA]
        logw = -DECAY_SCALE * _sigmoid(w0_ref[d] + _dg(jnp.tanh(wd).astype(BF16), wup_ref[d].astype(BF16), _NN))
        a = _sigmoid(a0_ref[d] + _dg(ad.astype(BF16), aup_ref[d].astype(BF16), _NN))
        kd = k * (1.0 + (a - 1.0) * ka_ref[...])
        b = kk * a
        kd_sum = kd if kd_sum is None else kd_sum + kd

        earlier_or_same = same_chunk & ((row >= col) if d == 0 else (row <= col))
        cum_i = _mm_exact_lhs(jnp.where(earlier_or_same, 1.0, 0.0).astype(BF16), logw, _NN)
        cum_e = cum_i - logw
        ab_d, rb_d, kt_d, bt_d, kw_d, bw_d, wc_d = [], [], [], [], [], [], []
        for s in range(NS):
            rs = slice(s * C, (s + 1) * C)
            ci_s = cum_i[rs]
            ctot = jnp.sum(logw[rs], axis=0, keepdims=True)
            e_ni = jnp.exp(-ci_s)
            e_ti = jnp.exp(ctot - ci_s)
            ab_d.append(kk[rs] * jnp.exp(cum_e[rs]))
            rb_d.append(r[rs] * jnp.exp(ci_s))
            kt_d.append(kd[rs] * e_ni)
            bt_d.append(b[rs] * e_ni)
            kw_d.append(kd[rs] * e_ti)
            bw_d.append(b[rs] * e_ti)
            wc_d.append(jnp.exp(ctot))
        abar.append(ab_d)
        rbar.append(rb_d)
        kt.append(kt_d)
        bt.append(bt_d)
        kw.append(kw_d)
        bw.append(bw_d)
        wc.append(wc_d)
        strict.append((prow > pcol) if d == 0 else (prow < pcol))
        incl.append((prow >= pcol) if d == 0 else (prow <= pcol))
    bonus_ref[...] = _mm_exact_rhs(r * kd_sum * rk_ref[...], pones, _NN, pieces=2) * v

    mm = functools.partial(_mm, passes=passes)
    chains = [(s, d, p) for s in range(NS) for d in range(N_DIR) for p in range(RWKV_PAIRS)]
    nch = range(len(chains))

    def sel(arr, i):
        s, d, p = chains[i]
        return arr[d][s][:, p * P:(p + 1) * P]

    cat0 = lambda a_, b_: jnp.concatenate([a_, b_], axis=0)
    cat1 = lambda a_, b_: jnp.concatenate([a_, b_], axis=1)
    vsl = [v[s * C:(s + 1) * C, p * P:(p + 1) * P] for s, _, p in chains]
    lhs = [cat0(sel(abar, i), sel(rbar, i)) for i in nch]
    by = [mm(lhs[i], cat0(_bd(sel(bt, i)), _bd(sel(kt, i))), _NT) for i in nch]
    a_kk = [jnp.where(strict[chains[i][1]], by[i][0:C, 0:P], 0.0) for i in nch]
    a_rb = [jnp.where(incl[chains[i][1]], by[i][C:2 * C, 0:P], 0.0) for i in nch]
    a_kv = [jnp.where(strict[chains[i][1]], by[i][0:C, P:2 * P], 0.0) for i in nch]
    a_rk = [jnp.where(incl[chains[i][1]], by[i][C:2 * C, P:2 * P], 0.0) for i in nch]
    on_v = [mm(cat0(a_kv[i], a_rk[i]), _bd(vsl[i]), _NN) for i in nch]

    x = [-m for m in a_kk]
    tinv = [eye_p + m for m in x]
    x = [mm(m, _bd(m), _NN) for m in x]
    for _ in range(4):
        both = [mm(cat0(tinv[i], x[i]), _bd(x[i]), _NN) for i in nch]
        tinv = [tinv[i] + both[i][0:C] for i in nch]
        x = [m[C:2 * C] for m in both]
    tinv = [tinv[i] + mm(tinv[i], _bd(x[i]), _NN) for i in nch]

    solved = [mm(tinv[i], cat1(_bd(sel(abar, i)), _bd(on_v[i][0:C])), _NN) for i in nch]
    ap = [m[:, 0:P] for m in solved]
    u0 = [m[:, P:2 * P] for m in solved]
    corr = [mm(a_rb[i], cat1(_bd(ap[i]), _bd(u0[i])), _NN) for i in nch]
    on_b = [mm(cat1(ap[i], u0[i]), sel(bw, i), _TN) for i in nch]
    vk = [mm(vsl[i], sel(kw, i), _TN) for i in nch]
    for i in nch:
        s, d, p = chains[i]
        rows = slice(s * C, (s + 1) * C)
        lanes = slice(p * P, (p + 1) * P)
        rp_ref[d, rows, lanes] = (sel(rbar, i) - corr[i][:, 0:P]).astype(BF16)
        gm_ref[d, s, p] = (eye_p * sel(wc, i) - diag_blocks(on_b[i][0:P])).astype(BF16)
        hm_ref[d, s, p] = diag_blocks(vk[i] - on_b[i][P:2 * P])
    for s in range(NS):
        for p in range(RWKV_PAIRS):
            f, b_ = chains.index((s, 0, p)), chains.index((s, 1, p))
            y0_ref[s * C:(s + 1) * C, p * P:(p + 1) * P] = ((on_v[f][C:2 * C] - corr[f][:, P:2 * P])
                                                            + (on_v[b_][C:2 * C] - corr[b_][:, P:2 * P]))


def _rwkv_local(z, seq_len, p, passes):
    n = z.shape[0]
    nchunk = n // RCHUNK
    cps = seq_len // RCHUNK
    ns = min(LOCAL_CHUNKS, cps)
    assert cps % ns == 0
    W = RWKV_WIDTH
    rows = ns * RCHUNK
    hb = rows // 8
    last8 = n // 8 - 1
    mat = lambda dt: jax.ShapeDtypeStruct((N_DIR, nchunk, RWKV_PAIRS, RWKV_HEAD_DIM, PAIR_LANES), dt)
    mat_spec = pl.BlockSpec((N_DIR, ns, RWKV_PAIRS, RWKV_HEAD_DIM, PAIR_LANES), lambda c: (0, c, 0, 0, 0))
    tok = lambda dt: jax.ShapeDtypeStruct((N_DIR, n, W), dt)
    tok_spec = pl.BlockSpec((N_DIR, rows, W), lambda c: (0, c, 0))
    row_spec = pl.BlockSpec((rows, W), lambda c: (c, 0))
    return pl.pallas_call(
        functools.partial(_rwkv_local_kernel, cps, passes, ns),
        grid=(nchunk // ns,),
        in_specs=[pl.BlockSpec((rows, ZR_BLOCK), lambda c: (c, 0)),
                  pl.BlockSpec((8, ZR_BLOCK), lambda c: (jnp.maximum(c * hb - 1, 0), 0)),
                  pl.BlockSpec((8, ZR_BLOCK), lambda c: (jnp.minimum((c + 1) * hb, last8), 0)),
                  _resident((1, RWKV_COLS)),
                  _resident((N_DIR, 1, W)), _resident((N_DIR, DECAY_LORA, W)),
                  _resident((N_DIR, 1, W)), _resident((N_DIR, ICLR_LORA, W)),
                  _resident((GATE_LORA, W)),
                  _resident((1, W)), _resident((1, W)), _resident((1, W)),
                  _resident((W, W))],
        out_specs=[tok_spec, row_spec, mat_spec, mat_spec, row_spec, row_spec],
        out_shape=[tok(BF16), jax.ShapeDtypeStruct((n, W), F32), mat(BF16), mat(F32),
                   jax.ShapeDtypeStruct((n, W), F32), jax.ShapeDtypeStruct((n, W), F32)],
        compiler_params=_params(("arbitrary",)),
        name="rwkv_local",
    )(z, z, z, p["mu"], p["w0"], p["w_up"], p["a0"], p["a_up"], p["g_up"],
      p["kk_scale"], p["k_a"], p["r_k"], p["pones"])


SCAN_CHUNKS = 4


def _rwkv_scan_kernel(has_init, *refs):
    s0_ref = refs[0] if has_init else None
    (rpf_ref, rpb_ref, gmf_ref, gmb_ref, hmf_ref, hmb_ref,
     ysf_ref, ysb_ref, sout_ref, s_scr) = refs[1:] if has_init else refs

    @pl.when(pl.program_id(1) == 0)
    def _():
        s_scr[...] = s0_ref[0] if has_init else jnp.zeros(s_scr.shape, F32)

    K = SCAN_CHUNKS
    C = RCHUNK
    rp_ref, gm_ref, hm_ref, ys_ref = (rpf_ref, rpb_ref), (gmf_ref, gmb_ref), (hmf_ref, hmb_ref), (ysf_ref, ysb_ref)
    chains = [(d, p) for d in range(N_DIR) for p in range(RWKV_PAIRS)]
    lanes = [slice(p * PAIR_LANES, (p + 1) * PAIR_LANES) for _, p in chains]
    nch = range(len(chains))
    s = [s_scr[d, p] for d, p in chains]
    for j in range(K):
        at = (j, K - 1 - j)
        rows = [slice(at[d] * C, (at[d] + 1) * C) for d, _ in chains]
        sb = [m.astype(BF16) for m in s]
        y = [_dg(rp_ref[chains[i][0]][0, rows[i], lanes[i]], sb[i], _NT) for i in nch]
        sg = [_dg(sb[i], _bd(gm_ref[chains[i][0]][0, at[chains[i][0]], chains[i][1]]), _NN) for i in nch]
        for i in nch:
            d, p = chains[i]
            ys_ref[d][rows[i], lanes[i]] = y[i]
        s = [sg[i] + _bd(hm_ref[chains[i][0]][0, at[chains[i][0]], chains[i][1]]) for i in nch]
    for i in nch:
        d, p = chains[i]
        s_scr[d, p] = s[i]
        n = RWKV_HEAD_DIM
        sout_ref[0, d, 2 * p] = s[i][0:n, 0:n]
        sout_ref[0, d, 2 * p + 1] = s[i][n:2 * n, n:2 * n]


def _rwkv_scan(s0, rp, gm, hm, batch, seq_len):
    K = SCAN_CHUNKS
    spb = seq_len // (RCHUNK * K)
    n = batch * seq_len

    def fwd(b, s):
        return b * spb + s

    def bwd(b, s):
        return b * spb + spb - 1 - s

    def mat_spec(d, at):
        return pl.BlockSpec((1, K, RWKV_PAIRS, RWKV_HEAD_DIM, PAIR_LANES), lambda b, s: (d, at(b, s), 0, 0, 0))

    def tok_spec(d, at):
        return pl.BlockSpec((1, K * RCHUNK, RWKV_WIDTH), lambda b, s: (d, at(b, s), 0))

    st_spec = pl.BlockSpec((1, N_DIR, RWKV_PAIRS, PAIR_LANES, PAIR_LANES), lambda b, s: (b, 0, 0, 0, 0))
    ys = jax.ShapeDtypeStruct((n, RWKV_WIDTH), F32)
    has_init = s0 is not None
    return pl.pallas_call(
        functools.partial(_rwkv_scan_kernel, has_init),
        grid=(batch, spb),
        in_specs=([st_spec] if has_init else [])
        + [tok_spec(0, fwd), tok_spec(1, bwd),
           mat_spec(0, fwd), mat_spec(1, bwd), mat_spec(0, fwd), mat_spec(1, bwd)],
        out_specs=[pl.BlockSpec((K * RCHUNK, RWKV_WIDTH), lambda b, s: (fwd(b, s), 0)),
                   pl.BlockSpec((K * RCHUNK, RWKV_WIDTH), lambda b, s: (bwd(b, s), 0)),
                   pl.BlockSpec((1, N_DIR, RWKV_HEADS, RWKV_HEAD_DIM, RWKV_HEAD_DIM),
                                lambda b, s: (b, 0, 0, 0, 0))],
        out_shape=[ys, ys,
                   jax.ShapeDtypeStruct((batch, N_DIR, RWKV_HEADS, RWKV_HEAD_DIM, RWKV_HEAD_DIM), F32)],
        scratch_shapes=[pltpu.VMEM((N_DIR, RWKV_PAIRS, PAIR_LANES, PAIR_LANES), F32)],
        compiler_params=_params(("arbitrary", "arbitrary")),
        name="rwkv_scan",
    )(*([s0] if has_init else []), rp, rp, gm, gm, hm, hm)


CONV_BLOCK_ROWS = 2048


def _dwconv(x, w_ref, width, vertical):
    T = x.shape[0]
    t = lax.broadcasted_iota(jnp.int32, (T, 1), 0)
    assert width & (width - 1) == 0
    colp = jnp.bitwise_and(t, width - 1)
    xl = jnp.where(colp == 0, 0.0, pltpu.roll(x, 1, 0))
    xr = jnp.where(colp == width - 1, 0.0, pltpu.roll(x, T - 1, 0))

    def tap_row(i):
        return w_ref[3 * i:3 * i + 1, :] * xl + w_ref[3 * i + 1:3 * i + 2, :] * x + w_ref[3 * i + 2:3 * i + 3, :] * xr

    out = tap_row(1)
    if vertical:
        out = out + jnp.where(t < width, 0.0, pltpu.roll(tap_row(0), width, 0))
        out = out + jnp.where(t >= T - width, 0.0, pltpu.roll(tap_row(2), T - width, 0))
    return out


def _conv_geometry(n, seq_len, rows):
    if rows > 1:
        return seq_len, seq_len // rows, True
    block = CONV_BLOCK_ROWS if (n % CONV_BLOCK_ROWS == 0 and CONV_BLOCK_ROWS % seq_len == 0) else seq_len
    return block, seq_len, False


def _store_time_on_lanes(x, out_ref, chunk):
    xt = x.T
    for c in range(x.shape[0] // chunk):
        out_ref[c] = xt[:, c * chunk:(c + 1) * chunk]


def _qk_conv_kernel(width, vertical, first_k_tile, x_ref, w_ref, o_ref, kt_ref):
    out = _silu(_dwconv(x_ref[...], w_ref, width, vertical))
    o_ref[...] = out

    @pl.when(pl.program_id(1) >= first_k_tile)
    def _():
        _store_time_on_lanes(out, kt_ref, MLSTM_CHUNK)


def _qk_conv(z, batch, seq_len, rows, conv_w9):
    n = batch * seq_len
    ch = 2 * MLSTM_WIDTH
    tc = CONV_CH_TILE
    off = ZM_OFF // tc
    first_k = MLSTM_WIDTH // tc
    block, width, vertical = _conv_geometry(n, seq_len, rows)
    cpb = block // MLSTM_CHUNK
    return pl.pallas_call(
        functools.partial(_qk_conv_kernel, width, vertical, first_k),
        grid=(n // block, ch // tc),
        in_specs=[pl.BlockSpec((block, tc), lambda b, j: (b, off + j)),
                  pl.BlockSpec((9, tc), lambda b, j: (0, j))],
        out_specs=[pl.BlockSpec((block, tc), lambda b, j: (b, j)),
                   pl.BlockSpec((cpb, tc, MLSTM_CHUNK), lambda b, j: (b, jnp.maximum(j - first_k, 0), 0))],
        out_shape=[jax.ShapeDtypeStruct((n, ch), F32),
                   jax.ShapeDtypeStruct((n // MLSTM_CHUNK, MLSTM_WIDTH, MLSTM_CHUNK), F32)],
        compiler_params=_params(("arbitrary", "arbitrary")),
        name="mlstm_qk_conv",
    )(z, conv_w9)


MLSTM_STEP_CHUNKS = 4

def _mlstm_scan_kernel(has_init, qkf_ref, qkb_ref, ktf_ref, ktb_ref, vf_ref, vb_ref, gcf_ref, gcb_ref,
                       grf_ref, grb_ref, gbc_ref, gbr_ref, *refs):
    init_refs = refs[0:3] if has_init else None
    hf_ref, hb_ref, cout_ref, nout_ref, mout_ref, c_scr, n_scr, m_scr = refs[3:] if has_init else refs
    step = pl.program_id(1)
    L = MLSTM_CHUNK
    dh = MLSTM_HEAD_DIM
    H = MLSTM_HEADS

    @pl.when(step == 0)
    def _():
        for scr, k in zip((c_scr, n_scr, m_scr), range(3)):
            scr[...] = init_refs[k][0] if has_init else jnp.zeros(scr.shape, F32)

    K = MLSTM_STEP_CHUNKS
    R = K * L
    row = lax.broadcasted_iota(jnp.int32, (L, L), 0)
    col = lax.broadcasted_iota(jnp.int32, (L, L), 1)
    lower = (row >= col)
    upper = (row <= col)
    lower_b = jnp.where(lower, 1.0, 0.0).astype(BF16)
    upper_b = jnp.where(upper, 1.0, 0.0).astype(BF16)
    rrow = lax.broadcasted_iota(jnp.int32, (R, R), 0)
    rcol = lax.broadcasted_iota(jnp.int32, (R, R), 1)
    same_chunk = jnp.bitwise_and(rrow, -L) == jnp.bitwise_and(rcol, -L)
    neg_inf = jnp.full((), -jnp.inf, F32)

    gcol, grow, bcol, brow, btot = [], [], [], [], []
    ones_b = jnp.ones((L, LANE), BF16)
    for d in range(N_DIR):
        gc_ref, gr_ref = (gcf_ref, grf_ref) if d == 0 else (gcb_ref, grb_ref)
        gcol.append(gc_ref[...] + gbc_ref[...])
        grow.append((gr_ref[...] + gbr_ref[...][None]).reshape(K * MLSTM_GATES, L))
        before = same_chunk & ((rrow >= rcol) if d == 0 else (rrow <= rcol))
        bcol.append(_mm_exact_lhs(jnp.where(before, 1.0, 0.0).astype(BF16), jax.nn.log_sigmoid(gcol[d]), _NN))
        frow = jax.nn.log_sigmoid(grow[d])
        brow.append(_mm_exact_rhs(frow, upper_b if d == 0 else lower_b, _NN))
        btot.append(_mm_exact_rhs(frow, ones_b, _NN))

    units = [(j, d, h) for j in range(K) for d in range(N_DIR) for h in range(H)]
    nun = range(len(units))
    q, k, kt, v, vb, qb = [], [], [], [], [], []
    c_row, b_col, b_last = [], [], []
    for j, d, h in units:
        at = j if d == 0 else K - 1 - j
        rows = slice(at * L, (at + 1) * L)
        st = d * H + h
        gi, gf = st, 2 * H + st
        qk_ref, kt_ref, v_ref = (qkf_ref, ktf_ref, vf_ref) if d == 0 else (qkb_ref, ktb_ref, vb_ref)
        q.append(qk_ref[rows, h * dh:(h + 1) * dh] * (dh ** -0.5))
        k.append(qk_ref[rows, MLSTM_WIDTH + h * dh:MLSTM_WIDTH + (h + 1) * dh])
        kt.append(kt_ref[at, h * dh:(h + 1) * dh, :])
        v.append(v_ref[rows, h * dh:(h + 1) * dh])
        qb.append(q[-1].astype(BF16))
        vb.append(v[-1].astype(BF16))
        b_col.append(jnp.broadcast_to(bcol[d][rows, gf:gf + 1], (L, LANE)))
        c_row.append(grow[d][at * MLSTM_GATES + gi:at * MLSTM_GATES + gi + 1, :]
                     - brow[d][at * MLSTM_GATES + gf:at * MLSTM_GATES + gf + 1, :])
        b_last.append(btot[d][at * MLSTM_GATES + gf:at * MLSTM_GATES + gf + 1, :])

    last = [L - 1 if d == 0 else 0 for _, d, _ in units]
    qk_t = [_dg(qb[i], k[i].astype(BF16), _NT) for i in nun]
    rel = [jnp.where(lower if units[i][1] == 0 else upper, c_row[i], neg_inf) for i in nun]
    mx = [jnp.broadcast_to(jnp.max(rel[i], axis=-1, keepdims=True), (L, LANE)) for i in nun]
    m_loc = [b_col[i] + mx[i] for i in nun]
    s_loc = [qk_t[i] * jnp.exp(rel[i] - mx[i][:, 0:L]) for i in nun]
    s_v = [_dg(s_loc[i].astype(BF16), vb[i], _NN) for i in nun]
    s_sum = [jnp.broadcast_to(jnp.sum(s_loc[i], axis=-1, keepdims=True), (L, LANE)) for i in nun]
    cmax = [mx[i][last[i]:last[i] + 1, :] for i in nun]
    m_w = [b_last[i] + cmax[i] for i in nun]
    wj = [jnp.exp(c_row[i] - cmax[i][:, 0:L]) for i in nun]
    kv = [_dg((kt[i] * wj[i]).astype(BF16), vb[i], _NN) for i in nun]
    w_k = [_mm(jnp.broadcast_to(wj[i], (8, L)), k[i], _NN, 3)[0:1] for i in nun]

    nst = N_DIR * H
    c_st = [c_scr[st] for st in range(nst)]
    n_st = [n_scr[st:st + 1, :] for st in range(nst)]
    m_st = [m_scr[st:st + 1, :] for st in range(nst)]
    for j in range(K):
        idx = [j * nst + st for st in range(nst)]
        q_c = [_dg(qb[i], c_st[st].astype(BF16), _NN) for st, i in enumerate(idx)]
        for st, i in enumerate(idx):
            _, d, h = units[i]
            at = j if d == 0 else K - 1 - j
            h_ref = hf_ref if d == 0 else hb_ref
            log_inter = b_col[i] + m_st[st]
            m_s = jnp.maximum(log_inter, m_loc[i])
            inter = jnp.exp(log_inter - m_s)
            local = jnp.exp(m_loc[i] - m_s)
            q_n = jnp.broadcast_to(jnp.sum(q[i] * n_st[st], axis=-1, keepdims=True), (L, LANE))
            den = inter * q_n + local * s_sum[i]
            scale = 1.0 / jnp.maximum(jnp.abs(den), jnp.exp(-m_s))
            h_ref[at * L:(at + 1) * L, h * dh:(h + 1) * dh] = (inter * scale) * q_c[st] + (local * scale) * s_v[i]
            m_new = jnp.maximum(b_last[i] + m_st[st], m_w[i])
            carry = jnp.exp(b_last[i] + m_st[st] - m_new)
            fresh = jnp.exp(m_w[i] - m_new)
            c_st[st] = carry * c_st[st] + fresh * kv[i]
            n_st[st] = carry * n_st[st] + fresh * w_k[i]
            m_st[st] = m_new


    for st in range(nst):
        c_scr[st] = c_st[st]
        n_scr[st:st + 1, :] = n_st[st]
        m_scr[st:st + 1, :] = m_st[st]
    nout_ref[0] = n_scr[...]
    mout_ref[0] = m_scr[...]

    @pl.when(step == pl.num_programs(1) - 1)
    def _():
        for st in range(nst):
            cout_ref[0, st] = c_st[st].T


def _mlstm_scan(z, qk, qk_blk, kt, gt, gate_bc, gate_br, c0, n0, m0, batch, seq_len):
    K = MLSTM_STEP_CHUNKS
    L = K * MLSTM_CHUNK
    assert seq_len % L == 0
    cps = seq_len // L
    n = batch * seq_len
    W = MLSTM_WIDTH
    nst = N_DIR * MLSTM_HEADS
    dh = MLSTM_HEAD_DIM

    def fw(b, c):
        return b * cps + c

    def bw(b, c):
        return b * cps + cps - 1 - c

    vblk = (ZM_OFF + 2 * W) // W
    gblk = ZG_OFF // LANE
    has_init = c0 is not None
    state_specs = [pl.BlockSpec((1, nst, dh, dh), lambda b, c: (b, 0, 0, 0)),
                   pl.BlockSpec((1, nst, dh), lambda b, c: (b, 0, 0)),
                   pl.BlockSpec((1, nst, LANE), lambda b, c: (b, 0, 0))]
    return pl.pallas_call(
        functools.partial(_mlstm_scan_kernel, has_init),
        grid=(batch, cps),
        in_specs=[pl.BlockSpec((L, 2 * W), lambda b, c: (fw(b, c), qk_blk)),
                  pl.BlockSpec((L, 2 * W), lambda b, c: (bw(b, c), qk_blk)),
                  pl.BlockSpec((K, W, MLSTM_CHUNK), lambda b, c: (fw(b, c), 0, 0)),
                  pl.BlockSpec((K, W, MLSTM_CHUNK), lambda b, c: (bw(b, c), 0, 0)),
                  pl.BlockSpec((L, W), lambda b, c: (fw(b, c), vblk)),
                  pl.BlockSpec((L, W), lambda b, c: (bw(b, c), vblk)),
                  pl.BlockSpec((L, LANE), lambda b, c: (fw(b, c), gblk)),
                  pl.BlockSpec((L, LANE), lambda b, c: (bw(b, c), gblk)),
                  pl.BlockSpec((K, MLSTM_GATES, MLSTM_CHUNK), lambda b, c: (fw(b, c), 0, 0)),
                  pl.BlockSpec((K, MLSTM_GATES, MLSTM_CHUNK), lambda b, c: (bw(b, c), 0, 0)),
                  _resident((1, LANE)),
                  _resident((MLSTM_GATES, 1))] + (state_specs if has_init else []),
        out_specs=[pl.BlockSpec((L, W), lambda b, c: (fw(b, c), 0)),
                   pl.BlockSpec((L, W), lambda b, c: (bw(b, c), 0))] + state_specs,
        out_shape=[jax.ShapeDtypeStruct((n, W), F32), jax.ShapeDtypeStruct((n, W), F32),
                   jax.ShapeDtypeStruct((batch, nst, dh, dh), F32),
                   jax.ShapeDtypeStruct((batch, nst, dh), F32),
                   jax.ShapeDtypeStruct((batch, nst, LANE), F32)],
        scratch_shapes=[pltpu.VMEM((nst, dh, dh), F32), pltpu.VMEM((nst, dh), F32),
                        pltpu.VMEM((nst, LANE), F32)],
        compiler_params=_params(("arbitrary", "arbitrary")),
        name="mlstm_scan",
    )(qk, qk, kt, kt, z, z, z, z, gt, gt, gate_bc, gate_br, *([c0, n0, m0] if has_init else []))


def _grid_conv(above, cur, below, cw_ref, width):
    T = cur.shape[0]
    E = T + 2 * width
    ext = jnp.concatenate([above, cur, below], axis=0)
    colp = jnp.bitwise_and(lax.broadcasted_iota(jnp.int32, (E, 1), 0), width - 1)
    left = jnp.where(colp == 0, 0.0, pltpu.roll(ext, 1, 0))
    right = jnp.where(colp == width - 1, 0.0, pltpu.roll(ext, E - 1, 0))

    def tap_row(i):
        rows = slice(i * width, i * width + T)
        return (cw_ref[3 * i:3 * i + 1, :] * left[rows] + cw_ref[3 * i + 1:3 * i + 2, :] * ext[rows]
                + cw_ref[3 * i + 2:3 * i + 3, :] * right[rows])

    return tap_row(0) + tap_row(1) + tap_row(2)


def _merge_kernel(conv, x_ref, mod_ref, ysf_ref, ysb_ref, y0_ref, bonus_ref, gate_ref, hf_ref, hb_ref, zo_ref,
                  zs_ref, lnxg_ref, lnxb_ref, gng_ref, pmean_ref, wbr_ref, wbm_ref, wout_ref, ng_ref, wup_ref,
                  cw_ref, cb_ref, x1_ref, act_ref, *scratch):
    if conv[0] == "grid":
        @pl.when(pl.program_id(0) == 0)
        def _():
            for ref in scratch:
                ref[...] = jnp.zeros(ref.shape, F32)

    mod = mod_ref[0]
    g1 = mod[:, 2 * D_MODEL:3 * D_MODEL]
    sh2 = mod[:, 3 * D_MODEL:4 * D_MODEL]
    sc2 = mod[:, 4 * D_MODEL:5 * D_MODEL]

    ys = (ysf_ref[...] + ysb_ref[...]) + y0_ref[...]
    pmean = pmean_ref[...]
    mean = _mm_exact_rhs(ys, pmean, _NN)
    cen = ys - mean
    var = _mm_exact_rhs(cen * cen, pmean, _NN)
    y_r = (cen * lax.rsqrt(var + RWKV_GN_EPS) * lnxg_ref[...] + lnxb_ref[...] + bonus_ref[...]) * gate_ref[...]

    hs = hf_ref[...] + hb_ref[...]
    parts = []
    for h in range(MLSTM_HEADS):
        hh = hs[:, h * MLSTM_HEAD_DIM:(h + 1) * MLSTM_HEAD_DIM]
        mu = jnp.mean(hh, axis=-1, keepdims=True)
        ce = hh - mu
        va = jnp.mean(ce * ce, axis=-1, keepdims=True)
        parts.append(ce * lax.rsqrt(va + MLSTM_GN_EPS))
    y_m = jnp.concatenate(parts, axis=1) * gng_ref[...] * _sigmoid(zo_ref[...])

    gates = zs_ref[...]
    merged = (gates[:, 0:D_MODEL] * _dg(y_r.astype(BF16), wbr_ref[...], _NN)
              + gates[:, D_MODEL:2 * D_MODEL] * _dg(y_m.astype(BF16), wbm_ref[...], _NN))
    t = _dg(merged.astype(BF16), wout_ref[...], _NN)
    x1 = x_ref[...] + g1 * _rms(t, ng_ref[1:2, :])
    x1_ref[...] = x1
    h2 = _rms(x1, ng_ref[2:3, :]) * (1.0 + sc2) + sh2
    u = _dg(h2.astype(BF16), wup_ref[...], _NN)
    if conv[0] == "seq":
        pre = _dwconv(u[:, 0:D_FF], cw_ref, conv[1], False) + cb_ref[...]
        act_ref[...] = (_silu(pre) * u[:, D_FF:2 * D_FF]).astype(BF16)
    else:
        _, width, tiles_per_image = conv
        act_scr, val_scr, tail_scr = scratch
        T = act_scr.shape[0]
        step = pl.program_id(0)
        pos = (step + tiles_per_image - 1) % tiles_per_image
        for c0 in range(0, D_FF, CONV_CH_TILE):
            ch = slice(c0, c0 + CONV_CH_TILE)
            above = jnp.where(pos != 0, tail_scr[:, ch], 0.0)
            below = jnp.where(pos != tiles_per_image - 1, u[0:width, ch], 0.0)
            pre = _grid_conv(above, act_scr[:, ch], below, cw_ref.at[:, ch], width) + cb_ref[:, ch]
            act_ref[:, ch] = (_silu(pre) * val_scr[:, ch]).astype(BF16)
        tail_scr[...] = act_scr[T - width:T, :]
        act_scr[...] = u[:, 0:D_FF]
        val_scr[...] = u[:, D_FF:2 * D_FF]


def _merge(x2, mod, mod_row, z, ysf, ysb, y0, bonus, gate, hf, hb, p, conv):
    n = x2.shape[0]
    W = RWKV_WIDTH
    rows = MERGE_TILE
    ntiles = n // rows
    delayed = conv[0] == "grid"
    cur = (lambda i: jnp.minimum(i, ntiles - 1)) if delayed else (lambda i: i)
    tile = lambda w: pl.BlockSpec((rows, w), lambda i: (cur(i), 0))
    act_spec = pl.BlockSpec((rows, D_FF), (lambda i: (jnp.maximum(i - 1, 0), 0)) if delayed else (lambda i: (i, 0)))
    scratch = ([pltpu.VMEM((rows, D_FF), F32), pltpu.VMEM((rows, D_FF), F32), pltpu.VMEM((conv[1], D_FF), F32)]
               if delayed else [])
    return pl.pallas_call(
        functools.partial(_merge_kernel, conv),
        grid=(ntiles + 1 if delayed else ntiles,),
        in_specs=[tile(D_MODEL),
                  pl.BlockSpec((1, 1, 6 * D_MODEL), lambda i: (mod_row(cur(i) * rows), 0, 0)),
                  tile(W), tile(W), tile(W), tile(W), tile(W), tile(MLSTM_WIDTH), tile(MLSTM_WIDTH),
                  pl.BlockSpec((rows, MLSTM_WIDTH),
                               lambda i: (cur(i), (ZM_OFF + 3 * MLSTM_WIDTH) // MLSTM_WIDTH)),
                  pl.BlockSpec((rows, GATE_COLS), lambda i: (cur(i), ZS_OFF // GATE_COLS)),
                  _resident((1, W)), _resident((1, W)), _resident((1, MLSTM_WIDTH)),
                  _resident((W, W)),
                  _resident((W, D_MODEL)), _resident((MLSTM_WIDTH, D_MODEL)),
                  _resident((D_MODEL, D_MODEL)), _resident((4, D_MODEL)),
                  _resident((D_MODEL, 2 * D_FF)), _resident((9, D_FF)), _resident((1, D_FF))],
        out_specs=[tile(D_MODEL), act_spec],
        out_shape=[jax.ShapeDtypeStruct((n, D_MODEL), F32), jax.ShapeDtypeStruct((n, D_FF), BF16)],
        scratch_shapes=scratch,
        compiler_params=_params(("arbitrary",)),
        name="merge_ffn_up",
    )(x2, mod, ysf, ysb, y0, bonus, gate, hf, hb, z, z, p["lnx_g"], p["lnx_b"], p["gn_g"], p["pmean"],
      p["w_br"], p["w_bm"], p["w_out"], p["norm_g"], p["ffn_up"], p["ffn_conv"], p["ffn_conv_b"])


def _down_kernel(x1_ref, mod_ref, a_ref, w_ref, ng_ref, o_ref):
    g2 = mod_ref[0][:, 5 * D_MODEL:6 * D_MODEL]
    f = _dg(a_ref[...], w_ref[...], _NN)
    o_ref[...] = x1_ref[...] + g2 * _rms(f, ng_ref[3:4, :])


def _down(x1, mod, mod_row, act, p):
    n = x1.shape[0]
    tile = DOWN_TILE
    return pl.pallas_call(
        _down_kernel,
        grid=(n // tile,),
        in_specs=[pl.BlockSpec((tile, D_MODEL), lambda i: (i, 0)),
                  pl.BlockSpec((1, 1, 6 * D_MODEL), lambda i: (mod_row(i * tile), 0, 0)),
                  pl.BlockSpec((tile, D_FF), lambda i: (i, 0)),
                  _resident((D_FF, D_MODEL)), _resident((4, D_MODEL))],
        out_specs=pl.BlockSpec((tile, D_MODEL), lambda i: (i, 0)),
        out_shape=jax.ShapeDtypeStruct((n, D_MODEL), F32),
        compiler_params=_params(("arbitrary",)),
        name="ffn_down",
    )(x1, mod, act, p["ffn_down"], p["norm_g"])


RWKV_LOCAL_PASSES = 1


def _state_to_pairs(s):
    b = s.shape[0]
    s = s.reshape(b, N_DIR, RWKV_PAIRS, 2, RWKV_HEAD_DIM, RWKV_HEAD_DIM)
    zero = jnp.zeros_like(s[:, :, :, 0])
    top = jnp.concatenate([s[:, :, :, 0], zero], axis=-1)
    bot = jnp.concatenate([zero, s[:, :, :, 1]], axis=-1)
    return jnp.concatenate([top, bot], axis=-2)


def _trunk(x, mod, mod_row, rows, states, p):
    batch, seq_len, _ = x.shape
    n = batch * seq_len
    x2 = x.reshape(n, D_MODEL)
    nst = N_DIR * MLSTM_HEADS
    if states is None:
        s0 = c0 = n0 = m0 = None
    else:
        s0, c0, n0, m0 = states
        s0 = _state_to_pairs(s0)
        c0 = jnp.swapaxes(c0, -1, -2).reshape(batch, nst, MLSTM_HEAD_DIM, MLSTM_HEAD_DIM)
        n0 = n0.reshape(batch, nst, MLSTM_HEAD_DIM)
        m0 = jnp.broadcast_to(m0.reshape(batch, nst, 1), (batch, nst, LANE))

    fuse_width = seq_len if (rows == 1 and IN_TILE % seq_len == 0 and MERGE_TILE % seq_len == 0) else None
    proj = _in_proj(x2, mod, mod_row, p["norm_g"][0:1], p["w_in"], p["mlstm_conv"], fuse_width)
    z = proj[0]

    rp, y0, gm, hm, gate, bonus = _rwkv_local(z, seq_len, p, RWKV_LOCAL_PASSES)
    ysf, ysb, s_fin = _rwkv_scan(s0, rp, gm, hm, batch, seq_len)

    if fuse_width is None:
        z, gt = proj
        qk, kt = _qk_conv(z, batch, seq_len, rows, p["mlstm_conv"])
        qk_blk = 0
    else:
        z, gt, kt = proj
        qk, qk_blk = z, ZM_OFF // (2 * MLSTM_WIDTH)
    hf, hb, c_fin, n_fin, m_fin = _mlstm_scan(z, qk, qk_blk, kt, gt, p["gate_bc"], p["gate_br"],
                                              c0, n0, m0, batch, seq_len)

    if fuse_width is not None:
        ffn_conv = ("seq", fuse_width)
    else:
        width = seq_len // rows
        assert rows > 1 and width & (width - 1) == 0 and MERGE_TILE % width == 0 and seq_len % MERGE_TILE == 0
        ffn_conv = ("grid", width, seq_len // MERGE_TILE)
    x1, act = _merge(x2, mod, mod_row, z, ysf, ysb, y0, bonus, gate, hf, hb, p, ffn_conv)
    out = _down(x1, mod, mod_row, act, p)

    new_states = (s_fin,
                  c_fin.reshape(batch, N_DIR, MLSTM_HEADS, MLSTM_HEAD_DIM, MLSTM_HEAD_DIM),
                  n_fin.reshape(batch, N_DIR, MLSTM_HEADS, MLSTM_HEAD_DIM),
                  m_fin[:, :, 0].reshape(batch, N_DIR, MLSTM_HEADS))
    return out.reshape(batch, seq_len, D_MODEL), new_states


def _pack_layer(l, ada_w, ada_b, norm_g, w_in, rwkv_mu, rwkv_w0, rwkv_w_up, rwkv_a0, rwkv_a_up,
                rwkv_g_up, rwkv_kk_scale, rwkv_k_a, rwkv_r_k, rwkv_lnx_g, rwkv_lnx_b, mlstm_conv,
                mlstm_gate_b, mlstm_gn_g, w_branch_rwkv, w_branch_mlstm, w_out, ffn_up, ffn_conv,
                ffn_conv_b, ffn_down):
    W = RWKV_WIDTH
    w_in_b = w_in[l].astype(BF16)

    head = jnp.arange(W, dtype=jnp.int32) // RWKV_HEAD_DIM
    same = (head[:, None] == head[None, :])
    gb = mlstm_gate_b[l].reshape(1, MLSTM_GATES)
    return dict(
        ada_w=ada_w[l], ada_b=ada_b[l], norm_g=norm_g[l], w_in=w_in_b,
        mu=rwkv_mu[l].reshape(1, RWKV_COLS),
        w0=rwkv_w0[l].reshape(N_DIR, 1, W), w_up=rwkv_w_up[l],
        a0=rwkv_a0[l].reshape(N_DIR, 1, W), a_up=rwkv_a_up[l], g_up=rwkv_g_up[l],
        kk_scale=rwkv_kk_scale[l].reshape(1, W), k_a=rwkv_k_a[l].reshape(1, W),
        r_k=rwkv_r_k[l].reshape(1, W),
        lnx_g=rwkv_lnx_g[l].reshape(1, W), lnx_b=rwkv_lnx_b[l].reshape(1, W),
        pones=same.astype(BF16), pmean=(same.astype(F32) / RWKV_HEAD_DIM).astype(BF16),
        mlstm_conv=mlstm_conv[l].reshape(9, 2 * MLSTM_WIDTH),
        gate_bc=jnp.pad(gb, ((0, 0), (0, LANE - MLSTM_GATES))), gate_br=gb.reshape(MLSTM_GATES, 1),
        gn_g=mlstm_gn_g[l].reshape(1, MLSTM_WIDTH),
        w_br=w_branch_rwkv[l].astype(BF16), w_bm=w_branch_mlstm[l].astype(BF16),
        w_out=w_out[l].astype(BF16), ffn_up=ffn_up[l].astype(BF16),
        ffn_conv=ffn_conv[l].reshape(9, D_FF), ffn_conv_b=ffn_conv_b[l].reshape(1, D_FF),
        ffn_down=ffn_down[l].astype(BF16),
    )


def kernel(x_prompt, x_sample, c, state_rwkv, state_mlstm_C, state_mlstm_n, state_mlstm_m, c_ctx,
           ada_w, ada_b, norm_g, w_in, rwkv_mu, rwkv_w0, rwkv_w_up, rwkv_a0, rwkv_a_up, rwkv_g_up,
           rwkv_kk_scale, rwkv_k_a, rwkv_r_k, rwkv_lnx_g, rwkv_lnx_b, mlstm_conv, mlstm_gate_b,
           mlstm_gn_g, w_branch_rwkv, w_branch_mlstm, w_out, ffn_up, ffn_conv, ffn_conv_b, ffn_down):
    depth = ada_w.shape[0]
    batch = x_prompt.shape[0]
    dec_batch, dec_seq, _ = x_sample.shape
    latent_rows = dec_seq // GRID_W
    cond = jnp.concatenate([c_ctx[None, :], c, jnp.zeros((8 - 1 - dec_batch, D_MODEL), F32)], axis=0)

    xp, xs = x_prompt, x_sample
    new_s, new_c, new_n, new_m = [], [], [], []
    for l in range(depth):
        p = _pack_layer(l, ada_w, ada_b, norm_g, w_in, rwkv_mu, rwkv_w0, rwkv_w_up, rwkv_a0, rwkv_a_up,
                        rwkv_g_up, rwkv_kk_scale, rwkv_k_a, rwkv_r_k, rwkv_lnx_g, rwkv_lnx_b, mlstm_conv,
                        mlstm_gate_b, mlstm_gn_g, w_branch_rwkv, w_branch_mlstm, w_out, ffn_up, ffn_conv,
                        ffn_conv_b, ffn_down)
        mod = _ada(cond, p["ada_w"], p["ada_b"]).reshape(8, 1, 6 * D_MODEL)
        xp, (s, cc, nn, mm) = _trunk(xp, mod, lambda r: 0, 1, None, p)
        new_s.append(s)
        new_c.append(cc)
        new_n.append(nn)
        new_m.append(mm)
        xs, _ = _trunk(xs, mod, lambda r: 1 + r // dec_seq, latent_rows,
                       (state_rwkv[:, l], state_mlstm_C[:, l], state_mlstm_n[:, l], state_mlstm_m[:, l]), p)
    return (xp, xs, jnp.stack(new_s, axis=1), jnp.stack(new_c, axis=1),
            jnp.stack(new_n, axis=1), jnp.stack(new_m, axis=1))
```

```python
import functools

import jax
import jax.numpy as jnp
from jax import lax
from jax.experimental import pallas as pl
from jax.experimental.pallas import tpu as pltpu

F32 = jnp.float32
BF16 = jnp.bfloat16

D_MODEL = 1024
N_DIR = 2
RWKV_HEADS = 8
RWKV_HEAD_DIM = 64
RWKV_WIDTH = RWKV_HEADS * RWKV_HEAD_DIM
DECAY_LORA = 64
ICLR_LORA = 64
GATE_LORA = 128
MLSTM_HEADS = 4
MLSTM_HEAD_DIM = 128
MLSTM_WIDTH = MLSTM_HEADS * MLSTM_HEAD_DIM
MLSTM_CHUNK = 64
D_FF = 2816
GRID_W = 64
RMS_EPS = 1e-6
RWKV_GN_EPS = 64e-5
MLSTM_GN_EPS = 1e-5
DECAY_SCALE = 0.606531

RWKV_COLS = 3 * RWKV_WIDTH + N_DIR * DECAY_LORA + N_DIR * ICLR_LORA + GATE_LORA
MLSTM_GATES = 2 * N_DIR * MLSTM_HEADS
MLSTM_COLS = 4 * MLSTM_WIDTH + MLSTM_GATES
GATE_COLS = 2 * D_MODEL

LANE = 128
ZR_BLOCK = 2048
ZG_OFF = RWKV_COLS
ZM_OFF = ZR_BLOCK
ZS_OFF = ZM_OFF + 4 * MLSTM_WIDTH
Z_COLS = ZS_OFF + GATE_COLS

IN_TILE = 512
MERGE_TILE = 256
DOWN_TILE = 512
RCHUNK = 64
CONV_CH_TILE = 256
VMEM_LIMIT = 56 * 1024 * 1024


def _params(sem):
    return pltpu.CompilerParams(dimension_semantics=sem, vmem_limit_bytes=VMEM_LIMIT)


def _resident(shape):
    nd = len(shape)
    return pl.BlockSpec(shape, lambda *_: (0,) * nd, pipeline_mode=pl.Buffered(1))


def _split2(a):
    hi = a.astype(BF16)
    lo = (a - hi.astype(F32)).astype(BF16)
    return hi, lo


def _split3(a):
    hi = a.astype(BF16)
    r1 = a - hi.astype(F32)
    mid = r1.astype(BF16)
    lo = (r1 - mid.astype(F32)).astype(BF16)
    return hi, mid, lo


def _dg(a, b, dims):
    return lax.dot_general(a, b, dims, preferred_element_type=F32)


def _mm(a, b, dims, passes):
    if passes == 1:
        return _dg(a.astype(BF16), b.astype(BF16), dims)
    ah, al = _split2(a)
    bh, bl = _split2(b)
    return _dg(ah, bh, dims) + (_dg(ah, bl, dims) + _dg(al, bh, dims))


def _mm_exact_lhs(a_bf16, b, dims):
    b1, b2, b3 = _split3(b)
    return _dg(a_bf16, b1, dims) + (_dg(a_bf16, b2, dims) + _dg(a_bf16, b3, dims))


def _mm_exact_rhs(a, b_bf16, dims, pieces=3):
    if pieces == 2:
        a1, a2 = _split2(a)
        return _dg(a1, b_bf16, dims) + _dg(a2, b_bf16, dims)
    a1, a2, a3 = _split3(a)
    return _dg(a1, b_bf16, dims) + (_dg(a2, b_bf16, dims) + _dg(a3, b_bf16, dims))


_NN = (((1,), (0,)), ((), ()))
_NT = (((1,), (1,)), ((), ()))
_TN = (((0,), (0,)), ((), ()))
_BNN = (((2,), (1,)), ((0,), (0,)))
_BNT = (((2,), (2,)), ((0,), (0,)))
_BTN = (((1,), (1,)), ((0,), (0,)))


def _sigmoid(x):
    return jax.nn.sigmoid(x)


def _silu(x):
    return x * jax.nn.sigmoid(x)


def _rms(x, g):
    return x * lax.rsqrt(jnp.mean(x * x, axis=-1, keepdims=True) + RMS_EPS) * g


def _ada_kernel(cond_ref, w_ref, b_ref, o_ref):
    s = _silu(cond_ref[...])
    o_ref[...] = _dg(s.astype(BF16), w_ref[...].astype(BF16), _NN) + b_ref[...]


def _ada(cond8, ada_w, ada_b):
    n = ada_w.shape[1]
    tn = 1536
    return pl.pallas_call(
        _ada_kernel,
        grid=(n // tn,),
        in_specs=[_resident((8, D_MODEL)),
                  pl.BlockSpec((D_MODEL, tn), lambda j: (0, j)),
                  pl.BlockSpec((1, tn), lambda j: (0, j))],
        out_specs=pl.BlockSpec((8, tn), lambda j: (0, j)),
        out_shape=jax.ShapeDtypeStruct((8, n), F32),
        compiler_params=_params(("arbitrary",)),
        name="ada_mod",
    )(cond8, ada_w, ada_b.reshape(1, n))


def _in_kernel(conv_width, x_ref, mod_ref, g_ref, w_ref, cw_ref, z_ref, gt_ref, *kt_ref):
    mod = mod_ref[0]
    sh = mod[:, 0:D_MODEL]
    sc = mod[:, D_MODEL:2 * D_MODEL]
    h = (_rms(x_ref[...], g_ref[...]) * (1.0 + sc) + sh).astype(BF16)
    m_off = RWKV_COLS
    t_off = RWKV_COLS + 4 * MLSTM_WIDTH
    z_ref[:, 0:ZG_OFF] = _dg(h, w_ref[:, 0:m_off], _NN)
    tail = _dg(h, w_ref[:, t_off:w_ref.shape[1]], _NN)
    zg = tail[:, 0:LANE]
    z_ref[:, ZG_OFF:ZM_OFF] = zg
    zgt = zg.T
    for c in range(zg.shape[0] // MLSTM_CHUNK):
        gt_ref[c] = zgt[0:MLSTM_GATES, c * MLSTM_CHUNK:(c + 1) * MLSTM_CHUNK]
    z_ref[:, ZS_OFF:Z_COLS] = _sigmoid(tail[:, MLSTM_GATES:MLSTM_GATES + GATE_COLS])
    zm = _dg(h, w_ref[:, m_off:t_off], _NN)
    if conv_width is None:
        z_ref[:, ZM_OFF:ZS_OFF] = zm
    else:
        qk_cols = 2 * MLSTM_WIDTH
        qk = _silu(_dwconv(zm[:, 0:qk_cols], cw_ref, conv_width, False))
        z_ref[:, ZM_OFF:ZM_OFF + qk_cols] = qk
        z_ref[:, ZM_OFF + qk_cols:ZS_OFF] = zm[:, qk_cols:]
        _store_time_on_lanes(qk[:, MLSTM_WIDTH:qk_cols], kt_ref[0], MLSTM_CHUNK)


def _in_proj(x2, mod, mod_row, norm_g0, w_in, conv_w9, conv_width):
    n = x2.shape[0]
    tile = IN_TILE
    cpt = tile // MLSTM_CHUNK
    nchunk = n // MLSTM_CHUNK
    out_specs = [pl.BlockSpec((tile, Z_COLS), lambda i: (i, 0)),
                 pl.BlockSpec((cpt, MLSTM_GATES, MLSTM_CHUNK), lambda i: (i, 0, 0))]
    out_shape = [jax.ShapeDtypeStruct((n, Z_COLS), F32),
                 jax.ShapeDtypeStruct((nchunk, MLSTM_GATES, MLSTM_CHUNK), F32)]
    if conv_width is not None:
        out_specs.append(pl.BlockSpec((cpt, MLSTM_WIDTH, MLSTM_CHUNK), lambda i: (i, 0, 0)))
        out_shape.append(jax.ShapeDtypeStruct((nchunk, MLSTM_WIDTH, MLSTM_CHUNK), F32))
    return pl.pallas_call(
        functools.partial(_in_kernel, conv_width),
        grid=(n // tile,),
        in_specs=[pl.BlockSpec((tile, D_MODEL), lambda i: (i, 0)),
                  pl.BlockSpec((1, 1, 6 * D_MODEL), lambda i: (mod_row(i * tile), 0, 0)),
                  _resident((1, D_MODEL)), _resident(w_in.shape), _resident(conv_w9.shape)],
        out_specs=out_specs,
        out_shape=out_shape,
        compiler_params=_params(("arbitrary",)),
        name="in_proj",
    )(x2, mod, norm_g0, w_in, conv_w9)


LOCAL_CHUNKS = 4
PAIR_LANES = 2 * RWKV_HEAD_DIM
RWKV_PAIRS = RWKV_HEADS // 2


def _bd(x):
    lane = lax.broadcasted_iota(jnp.int32, x.shape, 1)
    left = lane < RWKV_HEAD_DIM
    return jnp.concatenate([jnp.where(left, x, 0.0), jnp.where(left, 0.0, x)], axis=0)


def _rwkv_local_kernel(chunks_per_seq, passes, NS,
                       z_ref, zp_ref, zn_ref, mu_ref, w0_ref, wup_ref, a0_ref, aup_ref, gup_ref,
                       kks_ref, ka_ref, rk_ref, pones_ref,
                       rp_ref, y0_ref, gm_ref, hm_ref, gate_ref, bonus_ref):
    C = RCHUNK
    W = RWKV_WIDTH
    R = NS * C
    first = (pl.program_id(0) * NS) % chunks_per_seq
    has_prev = first != 0
    has_next = first + NS != chunks_per_seq

    z = z_ref[:, 0:RWKV_COLS]
    zp = jnp.where(has_prev, zp_ref[7:8, 0:RWKV_COLS], 0.0)
    zn = jnp.where(has_next, zn_ref[0:1, 0:RWKV_COLS], 0.0)
    trow = lax.broadcasted_iota(jnp.int32, (R, 1), 0)
    prev = jnp.where(trow == 0, zp, pltpu.roll(z, 1, 0))
    nxt = jnp.where(trow == R - 1, zn, pltpu.roll(z, R - 1, 0))
    zs = z + mu_ref[...] * (0.5 * (prev + nxt) - z)

    r = zs[:, 0:W]
    k = zs[:, W:2 * W]
    v = zs[:, 2 * W:3 * W]
    gd = zs[:, 3 * W + 2 * DECAY_LORA + 2 * ICLR_LORA:RWKV_COLS]
    gate_ref[...] = _dg(_sigmoid(gd).astype(BF16), gup_ref[...].astype(BF16), _NN)

    pones = pones_ref[...]
    kks = k * kks_ref[...]
    norm = jnp.sqrt(_mm_exact_rhs(kks * kks, pones, _NN, pieces=2))
    kk = kks / jnp.maximum(norm, 1e-12)

    P = PAIR_LANES
    row = lax.broadcasted_iota(jnp.int32, (R, R), 0)
    col = lax.broadcasted_iota(jnp.int32, (R, R), 1)
    same_chunk = jnp.bitwise_and(row, -C) == jnp.bitwise_and(col, -C)
    prow = lax.broadcasted_iota(jnp.int32, (C, P), 0)
    pcol = jnp.bitwise_and(lax.broadcasted_iota(jnp.int32, (C, P), 1), RWKV_HEAD_DIM - 1)
    eye_p = jnp.where(prow == pcol, 1.0, 0.0)
    left_head = lax.broadcasted_iota(jnp.int32, (C, P), 1) < RWKV_HEAD_DIM

    def diag_blocks(m):
        return jnp.where(left_head, m[0:RWKV_HEAD_DIM], m[RWKV_HEAD_DIM:P])

    abar, rbar, kt, bt, kw, bw, wc, strict, incl = [], [], [], [], [], [], [], [], []
    kd_sum = None
    for d in range(N_DIR):
        o = 3 * W + d * DECAY_LORA
        wd = zs[:, o:o + DECAY_LORA]
        o = 3 * W + 2 * DECAY_LORA + d * ICLR_LORA
        ad = zs[:, o:o + ICLR_LORA]
        logw = -DECAY_SCALE * _sigmoid(w0_ref[d] + _dg(jnp.tanh(wd).astype(BF16), wup_ref[d].astype(BF16), _NN))
        a = _sigmoid(a0_ref[d] + _dg(ad.astype(BF16), aup_ref[d].astype(BF16), _NN))
        kd = k * (1.0 + (a - 1.0) * ka_ref[...])
        b = kk * a
        kd_sum = kd if kd_sum is None else kd_sum + kd

        earlier_or_same = same_chunk & ((row >= col) if d == 0 else (row <= col))
        cum_i = _mm_exact_lhs(jnp.where(earlier_or_same, 1.0, 0.0).astype(BF16), logw, _NN)
        cum_e = cum_i - logw
        ab_d, rb_d, kt_d, bt_d, kw_d, bw_d, wc_d = [], [], [], [], [], [], []
        for s in range(NS):
            rs = slice(s * C, (s + 1) * C)
            ci_s = cum_i[rs]
            ctot = jnp.sum(logw[rs], axis=0, keepdims=True)
            e_ni = jnp.exp(-ci_s)
            e_ti = jnp.exp(ctot - ci_s)
            ab_d.append(kk[rs] * jnp.exp(cum_e[rs]))
            rb_d.append(r[rs] * jnp.exp(ci_s))
            kt_d.append(kd[rs] * e_ni)
            bt_d.append(b[rs] * e_ni)
            kw_d.append(kd[rs] * e_ti)
            bw_d.append(b[rs] * e_ti)
            wc_d.append(jnp.exp(ctot))
        abar.append(ab_d)
        rbar.append(rb_d)
        kt.append(kt_d)
        bt.append(bt_d)
        kw.append(kw_d)
        bw.append(bw_d)
        wc.append(wc_d)
        strict.append((prow > pcol) if d == 0 else (prow < pcol))
        incl.append((prow >= pcol) if d == 0 else (prow <= pcol))
    bonus_ref[...] = _mm_exact_rhs(r * kd_sum * rk_ref[...], pones, _NN, pieces=2) * v

    mm = functools.partial(_mm, passes=passes)
    chains = [(s, d, p) for s in range(NS) for d in range(N_DIR) for p in range(RWKV_PAIRS)]
    nch = range(len(chains))

    def sel(arr, i):
        s, d, p = chains[i]
        return arr[d][s][:, p * P:(p + 1) * P]

    cat0 = lambda a_, b_: jnp.concatenate([a_, b_], axis=0)
    cat1 = lambda a_, b_: jnp.concatenate([a_, b_], axis=1)
    vsl = [v[s * C:(s + 1) * C, p * P:(p + 1) * P] for s, _, p in chains]
    lhs = [cat0(sel(abar, i), sel(rbar, i)) for i in nch]
    by = [mm(lhs[i], cat0(_bd(sel(bt, i)), _bd(sel(kt, i))), _NT) for i in nch]
    a_kk = [jnp.where(strict[chains[i][1]], by[i][0:C, 0:P], 0.0) for i in nch]
    a_rb = [jnp.where(incl[chains[i][1]], by[i][C:2 * C, 0:P], 0.0) for i in nch]
    a_kv = [jnp.where(strict[chains[i][1]], by[i][0:C, P:2 * P], 0.0) for i in nch]
    a_rk = [jnp.where(incl[chains[i][1]], by[i][C:2 * C, P:2 * P], 0.0) for i in nch]
    on_v = [mm(cat0(a_kv[i], a_rk[i]), _bd(vsl[i]), _NN) for i in nch]

    x = [-m for m in a_kk]
    tinv = [eye_p + m for m in x]
    x = [mm(m, _bd(m), _NN) for m in x]
    for _ in range(4):
        both = [mm(cat0(tinv[i], x[i]), _bd(x[i]), _NN) for i in nch]
        tinv = [tinv[i] + both[i][0:C] for i in nch]
        x = [m[C:2 * C] for m in both]
    tinv = [tinv[i] + mm(tinv[i], _bd(x[i]), _NN) for i in nch]

    solved = [mm(tinv[i], cat1(_bd(sel(abar, i)), _bd(on_v[i][0:C])), _NN) for i in nch]
    ap = [m[:, 0:P] for m in solved]
    u0 = [m[:, P:2 * P] for m in solved]
    corr = [mm(a_rb[i], cat1(_bd(ap[i]), _bd(u0[i])), _NN) for i in nch]
    on_b = [mm(cat1(ap[i], u0[i]), sel(bw, i), _TN) for i in nch]
    vk = [mm(vsl[i], sel(kw, i), _TN) for i in nch]
    for i in nch:
        s, d, p = chains[i]
        rows = slice(s * C, (s + 1) * C)
        lanes = slice(p * P, (p + 1) * P)
        rp_ref[d, rows, lanes] = (sel(rbar, i) - corr[i][:, 0:P]).astype(BF16)
        gm_ref[d, s, p] = (eye_p * sel(wc, i) - diag_blocks(on_b[i][0:P])).astype(BF16)
        hm_ref[d, s, p] = diag_blocks(vk[i] - on_b[i][P:2 * P])
    for s in range(NS):
        for p in range(RWKV_PAIRS):
            f, b_ = chains.index((s, 0, p)), chains.index((s, 1, p))
            y0_ref[s * C:(s + 1) * C, p * P:(p + 1) * P] = ((on_v[f][C:2 * C] - corr[f][:, P:2 * P])
                                                            + (on_v[b_][C:2 * C] - corr[b_][:, P:2 * P]))


def _rwkv_local(z, seq_len, p, passes):
    n = z.shape[0]
    nchunk = n // RCHUNK
    cps = seq_len // RCHUNK
    ns = min(LOCAL_CHUNKS, cps)
    assert cps % ns == 0
    W = RWKV_WIDTH
    rows = ns * RCHUNK
    hb = rows // 8
    last8 = n // 8 - 1
    mat = lambda dt: jax.ShapeDtypeStruct((N_DIR, nchunk, RWKV_PAIRS, RWKV_HEAD_DIM, PAIR_LANES), dt)
    mat_spec = pl.BlockSpec((N_DIR, ns, RWKV_PAIRS, RWKV_HEAD_DIM, PAIR_LANES), lambda c: (0, c, 0, 0, 0))
    tok = lambda dt: jax.ShapeDtypeStruct((N_DIR, n, W), dt)
    tok_spec = pl.BlockSpec((N_DIR, rows, W), lambda c: (0, c, 0))
    row_spec = pl.BlockSpec((rows, W), lambda c: (c, 0))
    return pl.pallas_call(
        functools.partial(_rwkv_local_kernel, cps, passes, ns),
        grid=(nchunk // ns,),
        in_specs=[pl.BlockSpec((rows, ZR_BLOCK), lambda c: (c, 0)),
                  pl.BlockSpec((8, ZR_BLOCK), lambda c: (jnp.maximum(c * hb - 1, 0), 0)),
                  pl.BlockSpec((8, ZR_BLOCK), lambda c: (jnp.minimum((c + 1) * hb, last8), 0)),
                  _resident((1, RWKV_COLS)),
                  _resident((N_DIR, 1, W)), _resident((N_DIR, DECAY_LORA, W)),
                  _resident((N_DIR, 1, W)), _resident((N_DIR, ICLR_LORA, W)),
                  _resident((GATE_LORA, W)),
                  _resident((1, W)), _resident((1, W)), _resident((1, W)),
                  _resident((W, W))],
        out_specs=[tok_spec, row_spec, mat_spec, mat_spec, row_spec, row_spec],
        out_shape=[tok(BF16), jax.ShapeDtypeStruct((n, W), F32), mat(BF16), mat(F32),
                   jax.ShapeDtypeStruct((n, W), F32), jax.ShapeDtypeStruct((n, W), F32)],
        compiler_params=_params(("arbitrary",)),
        name="rwkv_local",
    )(z, z, z, p["mu"], p["w0"], p["w_up"], p["a0"], p["a_up"], p["g_up"],
      p["kk_scale"], p["k_a"], p["r_k"], p["pones"])


SCAN_CHUNKS = 4


def _rwkv_scan_kernel(has_init, *refs):
    s0_ref = refs[0] if has_init else None
    (rpf_ref, rpb_ref, gmf_ref, gmb_ref, hmf_ref, hmb_ref,
     ysf_ref, ysb_ref, sout_ref, s_scr) = refs[1:] if has_init else refs

    @pl.when(pl.program_id(1) == 0)
    def _():
        s_scr[...] = s0_ref[0] if has_init else jnp.zeros(s_scr.shape, F32)

    K = SCAN_CHUNKS
    C = RCHUNK
    rp_ref, gm_ref, hm_ref, ys_ref = (rpf_ref, rpb_ref), (gmf_ref, gmb_ref), (hmf_ref, hmb_ref), (ysf_ref, ysb_ref)
    chains = [(d, p) for d in range(N_DIR) for p in range(RWKV_PAIRS)]
    lanes = [slice(p * PAIR_LANES, (p + 1) * PAIR_LANES) for _, p in chains]
    nch = range(len(chains))
    s = [s_scr[d, p] for d, p in chains]
    for j in range(K):
        at = (j, K - 1 - j)
        rows = [slice(at[d] * C, (at[d] + 1) * C) for d, _ in chains]
        sb = [m.astype(BF16) for m in s]
        y = [_dg(rp_ref[chains[i][0]][0, rows[i], lanes[i]], sb[i], _NT) for i in nch]
        sg = [_dg(sb[i], _bd(gm_ref[chains[i][0]][0, at[chains[i][0]], chains[i][1]]), _NN) for i in nch]
        for i in nch:
            d, p = chains[i]
            ys_ref[d][rows[i], lanes[i]] = y[i]
        s = [sg[i] + _bd(hm_ref[chains[i][0]][0, at[chains[i][0]], chains[i][1]]) for i in nch]
    for i in nch:
        d, p = chains[i]
        s_scr[d, p] = s[i]
        n = RWKV_HEAD_DIM
        sout_ref[0, d, 2 * p] = s[i][0:n, 0:n]
        sout_ref[0, d, 2 * p + 1] = s[i][n:2 * n, n:2 * n]


def _rwkv_scan(s0, rp, gm, hm, batch, seq_len):
    K = SCAN_CHUNKS
    spb = seq_len // (RCHUNK * K)
    n = batch * seq_len

    def fwd(b, s):
        return b * spb + s

    def bwd(b, s):
        return b * spb + spb - 1 - s

    def mat_spec(d, at):
        return pl.BlockSpec((1, K, RWKV_PAIRS, RWKV_HEAD_DIM, PAIR_LANES), lambda b, s: (d, at(b, s), 0, 0, 0))

    def tok_spec(d, at):
        return pl.BlockSpec((1, K * RCHUNK, RWKV_WIDTH), lambda b, s: (d, at(b, s), 0))

    st_spec = pl.BlockSpec((1, N_DIR, RWKV_PAIRS, PAIR_LANES, PAIR_LANES), lambda b, s: (b, 0, 0, 0, 0))
    ys = jax.ShapeDtypeStruct((n, RWKV_WIDTH), F32)
    has_init = s0 is not None
    return pl.pallas_call(
        functools.partial(_rwkv_scan_kernel, has_init),
        grid=(batch, spb),
        in_specs=([st_spec] if has_init else [])
        + [tok_spec(0, fwd), tok_spec(1, bwd),
           mat_spec(0, fwd), mat_spec(1, bwd), mat_spec(0, fwd), mat_spec(1, bwd)],
        out_specs=[pl.BlockSpec((K * RCHUNK, RWKV_WIDTH), lambda b, s: (fwd(b, s), 0)),
                   pl.BlockSpec((K * RCHUNK, RWKV_WIDTH), lambda b, s: (bwd(b, s), 0)),
                   pl.BlockSpec((1, N_DIR, RWKV_HEADS, RWKV_HEAD_DIM, RWKV_HEAD_DIM),
                                lambda b, s: (b, 0, 0, 0, 0))],
        out_shape=[ys, ys,
                   jax.ShapeDtypeStruct((batch, N_DIR, RWKV_HEADS, RWKV_HEAD_DIM, RWKV_HEAD_DIM), F32)],
        scratch_shapes=[pltpu.VMEM((N_DIR, RWKV_PAIRS, PAIR_LANES, PAIR_LANES), F32)],
        compiler_params=_params(("arbitrary", "arbitrary")),
        name="rwkv_scan",
    )(*([s0] if has_init else []), rp, rp, gm, gm, hm, hm)


CONV_BLOCK_ROWS = 2048


def _dwconv(x, w_ref, width, vertical):
    T = x.shape[0]
    t = lax.broadcasted_iota(jnp.int32, (T, 1), 0)
    assert width & (width - 1) == 0
    colp = jnp.bitwise_and(t, width - 1)
    xl = jnp.where(colp == 0, 0.0, pltpu.roll(x, 1, 0))
    xr = jnp.where(colp == width - 1, 0.0, pltpu.roll(x, T - 1, 0))

    def tap_row(i):
        return w_ref[3 * i:3 * i + 1, :] * xl + w_ref[3 * i + 1:3 * i + 2, :] * x + w_ref[3 * i + 2:3 * i + 3, :] * xr

    out = tap_row(1)
    if vertical:
        out = out + jnp.where(t < width, 0.0, pltpu.roll(tap_row(0), width, 0))
        out = out + jnp.where(t >= T - width, 0.0, pltpu.roll(tap_row(2), T - width, 0))
    return out


def _conv_geometry(n, seq_len, rows):
    if rows > 1:
        return seq_len, seq_len // rows, True
    block = CONV_BLOCK_ROWS if (n % CONV_BLOCK_ROWS == 0 and CONV_BLOCK_ROWS % seq_len == 0) else seq_len
    return block, seq_len, False


def _store_time_on_lanes(x, out_ref, chunk):
    xt = x.T
    for c in range(x.shape[0] // chunk):
        out_ref[c] = xt[:, c * chunk:(c + 1) * chunk]


def _qk_conv_kernel(width, vertical, first_k_tile, x_ref, w_ref, o_ref, kt_ref):
    out = _silu(_dwconv(x_ref[...], w_ref, width, vertical))
    o_ref[...] = out

    @pl.when(pl.program_id(1) >= first_k_tile)
    def _():
        _store_time_on_lanes(out, kt_ref, MLSTM_CHUNK)


def _qk_conv(z, batch, seq_len, rows, conv_w9):
    n = batch * seq_len
    ch = 2 * MLSTM_WIDTH
    tc = CONV_CH_TILE
    off = ZM_OFF // tc
    first_k = MLSTM_WIDTH // tc
    block, width, vertical = _conv_geometry(n, seq_len, rows)
    cpb = block // MLSTM_CHUNK
    return pl.pallas_call(
        functools.partial(_qk_conv_kernel, width, vertical, first_k),
        grid=(n // block, ch // tc),
        in_specs=[pl.BlockSpec((block, tc), lambda b, j: (b, off + j)),
                  pl.BlockSpec((9, tc), lambda b, j: (0, j))],
        out_specs=[pl.BlockSpec((block, tc), lambda b, j: (b, j)),
                   pl.BlockSpec((cpb, tc, MLSTM_CHUNK), lambda b, j: (b, jnp.maximum(j - first_k, 0), 0))],
        out_shape=[jax.ShapeDtypeStruct((n, ch), F32),
                   jax.ShapeDtypeStruct((n // MLSTM_CHUNK, MLSTM_WIDTH, MLSTM_CHUNK), F32)],
        compiler_params=_params(("arbitrary", "arbitrary")),
        name="mlstm_qk_conv",
    )(z, conv_w9)


MLSTM_STEP_CHUNKS = 4

def _mlstm_scan_kernel(has_init, qkf_ref, qkb_ref, ktf_ref, ktb_ref, vf_ref, vb_ref, gcf_ref, gcb_ref,
                       grf_ref, grb_ref, gbc_ref, gbr_ref, *refs):
    init_refs = refs[0:3] if has_init else None
    hf_ref, hb_ref, cout_ref, nout_ref, mout_ref, c_scr, n_scr, m_scr = refs[3:] if has_init else refs
    step = pl.program_id(1)
    L = MLSTM_CHUNK
    dh = MLSTM_HEAD_DIM
    H = MLSTM_HEADS

    @pl.when(step == 0)
    def _():
        for scr, k in zip((c_scr, n_scr, m_scr), range(3)):
            scr[...] = init_refs[k][0] if has_init else jnp.zeros(scr.shape, F32)

    K = MLSTM_STEP_CHUNKS
    R = K * L
    row = lax.broadcasted_iota(jnp.int32, (L, L), 0)
    col = lax.broadcasted_iota(jnp.int32, (L, L), 1)
    lower = (row >= col)
    upper = (row <= col)
    lower_b = jnp.where(lower, 1.0, 0.0).astype(BF16)
    upper_b = jnp.where(upper, 1.0, 0.0).astype(BF16)
    rrow = lax.broadcasted_iota(jnp.int32, (R, R), 0)
    rcol = lax.broadcasted_iota(jnp.int32, (R, R), 1)
    same_chunk = jnp.bitwise_and(rrow, -L) == jnp.bitwise_and(rcol, -L)
    neg_inf = jnp.full((), -jnp.inf, F32)

    gcol, grow, bcol, brow, btot = [], [], [], [], []
    ones_b = jnp.ones((L, LANE), BF16)
    for d in range(N_DIR):
        gc_ref, gr_ref = (gcf_ref, grf_ref) if d == 0 else (gcb_ref, grb_ref)
        gcol.append(gc_ref[...] + gbc_ref[...])
        grow.append((gr_ref[...] + gbr_ref[...][None]).reshape(K * MLSTM_GATES, L))
        before = same_chunk & ((rrow >= rcol) if d == 0 else (rrow <= rcol))
        bcol.append(_mm_exact_lhs(jnp.where(before, 1.0, 0.0).astype(BF16), jax.nn.log_sigmoid(gcol[d]), _NN))
        frow = jax.nn.log_sigmoid(grow[d])
        brow.append(_mm_exact_rhs(frow, upper_b if d == 0 else lower_b, _NN))
        btot.append(_mm_exact_rhs(frow, ones_b, _NN))

    units = [(j, d, h) for j in range(K) for d in range(N_DIR) for h in range(H)]
    nun = range(len(units))
    q, k, kt, v, vb, qb = [], [], [], [], [], []
    c_row, b_col, b_last = [], [], []
    for j, d, h in units:
        at = j if d == 0 else K - 1 - j
        rows = slice(at * L, (at + 1) * L)
        st = d * H + h
        gi, gf = st, 2 * H + st
        qk_ref, kt_ref, v_ref = (qkf_ref, ktf_ref, vf_ref) if d == 0 else (qkb_ref, ktb_ref, vb_ref)
        q.append(qk_ref[rows, h * dh:(h + 1) * dh] * (dh ** -0.5))
        k.append(qk_ref[rows, MLSTM_WIDTH + h * dh:MLSTM_WIDTH + (h + 1) * dh])
        kt.append(kt_ref[at, h * dh:(h + 1) * dh, :])
        v.append(v_ref[rows, h * dh:(h + 1) * dh])
        qb.append(q[-1].astype(BF16))
        vb.append(v[-1].astype(BF16))
        b_col.append(jnp.broadcast_to(bcol[d][rows, gf:gf + 1], (L, LANE)))
        c_row.append(grow[d][at * MLSTM_GATES + gi:at * MLSTM_GATES + gi + 1, :]
                     - brow[d][at * MLSTM_GATES + gf:at * MLSTM_GATES + gf + 1, :])
        b_last.append(btot[d][at * MLSTM_GATES + gf:at * MLSTM_GATES + gf + 1, :])

    last = [L - 1 if d == 0 else 0 for _, d, _ in units]
    qk_t = [_dg(qb[i], k[i].astype(BF16), _NT) for i in nun]
    rel = [jnp.where(lower if units[i][1] == 0 else upper, c_row[i], neg_inf) for i in nun]
    mx = [jnp.broadcast_to(jnp.max(rel[i], axis=-1, keepdims=True), (L, LANE)) for i in nun]
    m_loc = [b_col[i] + mx[i] for i in nun]
    s_loc = [qk_t[i] * jnp.exp(rel[i] - mx[i][:, 0:L]) for i in nun]
    s_v = [_dg(s_loc[i].astype(BF16), vb[i], _NN) for i in nun]
    s_sum = [jnp.broadcast_to(jnp.sum(s_loc[i], axis=-1, keepdims=True), (L, LANE)) for i in nun]
    cmax = [mx[i][last[i]:last[i] + 1, :] for i in nun]
    m_w = [b_last[i] + cmax[i] for i in nun]
    wj = [jnp.exp(c_row[i] - cmax[i][:, 0:L]) for i in nun]
    kv = [_dg((kt[i] * wj[i]).astype(BF16), vb[i], _NN) for i in nun]
    w_k = [_mm(jnp.broadcast_to(wj[i], (8, L)), k[i], _NN, 3)[0:1] for i in nun]

    nst = N_DIR * H
    c_st = [c_scr[st] for st in range(nst)]
    n_st = [n_scr[st:st + 1, :] for st in range(nst)]
    m_st = [m_scr[st:st + 1, :] for st in range(nst)]
    for j in range(K):
        idx = [j * nst + st for st in range(nst)]
        q_c = [_dg(qb[i], c_st[st].astype(BF16), _NN) for st, i in enumerate(idx)]
        for st, i in enumerate(idx):
            _, d, h = units[i]
            at = j if d == 0 else K - 1 - j
            h_ref = hf_ref if d == 0 else hb_ref
            log_inter = b_col[i] + m_st[st]
            m_s = jnp.maximum(log_inter, m_loc[i])
            inter = jnp.exp(log_inter - m_s)
            local = jnp.exp(m_loc[i] - m_s)
            q_n = jnp.broadcast_to(jnp.sum(q[i] * n_st[st], axis=-1, keepdims=True), (L, LANE))
            den = inter * q_n + local * s_sum[i]
            scale = 1.0 / jnp.maximum(jnp.abs(den), jnp.exp(-m_s))
            h_ref[at * L:(at + 1) * L, h * dh:(h + 1) * dh] = (inter * scale) * q_c[st] + (local * scale) * s_v[i]
            m_new = jnp.maximum(b_last[i] + m_st[st], m_w[i])
            carry = jnp.exp(b_last[i] + m_st[st] - m_new)
            fresh = jnp.exp(m_w[i] - m_new)
            c_st[st] = carry * c_st[st] + fresh * kv[i]
            n_st[st] = carry * n_st[st] + fresh * w_k[i]
            m_st[st] = m_new


    for st in range(nst):
        c_scr[st] = c_st[st]
        n_scr[st:st + 1, :] = n_st[st]
        m_scr[st:st + 1, :] = m_st[st]
    nout_ref[0] = n_scr[...]
    mout_ref[0] = m_scr[...]

    @pl.when(step == pl.num_programs(1) - 1)
    def _():
        for st in range(nst):
            cout_ref[0, st] = c_st[st].T


def _mlstm_scan(z, qk, qk_blk, kt, gt, gate_bc, gate_br, c0, n0, m0, batch, seq_len):
    K = MLSTM_STEP_CHUNKS
    L = K * MLSTM_CHUNK
    assert seq_len % L == 0
    cps = seq_len // L
    n = batch * seq_len
    W = MLSTM_WIDTH
    nst = N_DIR * MLSTM_HEADS
    dh = MLSTM_HEAD_DIM

    def fw(b, c):
        return b * cps + c

    def bw(b, c):
        return b * cps + cps - 1 - c

    vblk = (ZM_OFF + 2 * W) // W
    gblk = ZG_OFF // LANE
    has_init = c0 is not None
    state_specs = [pl.BlockSpec((1, nst, dh, dh), lambda b, c: (b, 0, 0, 0)),
                   pl.BlockSpec((1, nst, dh), lambda b, c: (b, 0, 0)),
                   pl.BlockSpec((1, nst, LANE), lambda b, c: (b, 0, 0))]
    return pl.pallas_call(
        functools.partial(_mlstm_scan_kernel, has_init),
        grid=(batch, cps),
        in_specs=[pl.BlockSpec((L, 2 * W), lambda b, c: (fw(b, c), qk_blk)),
                  pl.BlockSpec((L, 2 * W), lambda b, c: (bw(b, c), qk_blk)),
                  pl.BlockSpec((K, W, MLSTM_CHUNK), lambda b, c: (fw(b, c), 0, 0)),
                  pl.BlockSpec((K, W, MLSTM_CHUNK), lambda b, c: (bw(b, c), 0, 0)),
                  pl.BlockSpec((L, W), lambda b, c: (fw(b, c), vblk)),
                  pl.BlockSpec((L, W), lambda b, c: (bw(b, c), vblk)),
                  pl.BlockSpec((L, LANE), lambda b, c: (fw(b, c), gblk)),
                  pl.BlockSpec((L, LANE), lambda b, c: (bw(b, c), gblk)),
                  pl.BlockSpec((K, MLSTM_GATES, MLSTM_CHUNK), lambda b, c: (fw(b, c), 0, 0)),
                  pl.BlockSpec((K, MLSTM_GATES, MLSTM_CHUNK), lambda b, c: (bw(b, c), 0, 0)),
                  _resident((1, LANE)),
                  _resident((MLSTM_GATES, 1))] + (state_specs if has_init else []),
        out_specs=[pl.BlockSpec((L, W), lambda b, c: (fw(b, c), 0)),
                   pl.BlockSpec((L, W), lambda b, c: (bw(b, c), 0))] + state_specs,
        out_shape=[jax.ShapeDtypeStruct((n, W), F32), jax.ShapeDtypeStruct((n, W), F32),
                   jax.ShapeDtypeStruct((batch, nst, dh, dh), F32),
                   jax.ShapeDtypeStruct((batch, nst, dh), F32),
                   jax.ShapeDtypeStruct((batch, nst, LANE), F32)],
        scratch_shapes=[pltpu.VMEM((nst, dh, dh), F32), pltpu.VMEM((nst, dh), F32),
                        pltpu.VMEM((nst, LANE), F32)],
        compiler_params=_params(("arbitrary", "arbitrary")),
        name="mlstm_scan",
    )(qk, qk, kt, kt, z, z, z, z, gt, gt, gate_bc, gate_br, *([c0, n0, m0] if has_init else []))


def _grid_conv(above, cur, below, cw_ref, width):
    T = cur.shape[0]
    E = T + 2 * width
    ext = jnp.concatenate([above, cur, below], axis=0)
    colp = jnp.bitwise_and(lax.broadcasted_iota(jnp.int32, (E, 1), 0), width - 1)
    left = jnp.where(colp == 0, 0.0, pltpu.roll(ext, 1, 0))
    right = jnp.where(colp == width - 1, 0.0, pltpu.roll(ext, E - 1, 0))

    def tap_row(i):
        rows = slice(i * width, i * width + T)
        return (cw_ref[3 * i:3 * i + 1, :] * left[rows] + cw_ref[3 * i + 1:3 * i + 2, :] * ext[rows]
                + cw_ref[3 * i + 2:3 * i + 3, :] * right[rows])

    return tap_row(0) + tap_row(1) + tap_row(2)


def _merge_kernel(conv, x_ref, mod_ref, ysf_ref, ysb_ref, y0_ref, bonus_ref, gate_ref, hf_ref, hb_ref, zo_ref,
                  zs_ref, lnxg_ref, lnxb_ref, gng_ref, pmean_ref, wbr_ref, wbm_ref, wout_ref, ng_ref, wup_ref,
                  cw_ref, cb_ref, *rest):
    if conv[0] == "seq":
        wdn_ref, out_ref = rest
    else:
        x1_ref, act_ref = rest[0:2]
        scratch = rest[2:]
    if conv[0] == "grid":
        @pl.when(pl.program_id(0) == 0)
        def _():
            for ref in scratch:
                ref[...] = jnp.zeros(ref.shape, F32)

    mod = mod_ref[0]
    g1 = mod[:, 2 * D_MODEL:3 * D_MODEL]
    sh2 = mod[:, 3 * D_MODEL:4 * D_MODEL]
    sc2 = mod[:, 4 * D_MODEL:5 * D_MODEL]

    ys = (ysf_ref[...] + ysb_ref[...]) + y0_ref[...]
    pmean = pmean_ref[...]
    mean = _mm_exact_rhs(ys, pmean, _NN)
    cen = ys - mean
    var = _mm_exact_rhs(cen * cen, pmean, _NN)
    y_r = (cen * lax.rsqrt(var + RWKV_GN_EPS) * lnxg_ref[...] + lnxb_ref[...] + bonus_ref[...]) * gate_ref[...]

    hs = hf_ref[...] + hb_ref[...]
    parts = []
    for h in range(MLSTM_HEADS):
        hh = hs[:, h * MLSTM_HEAD_DIM:(h + 1) * MLSTM_HEAD_DIM]
        mu = jnp.mean(hh, axis=-1, keepdims=True)
        ce = hh - mu
        va = jnp.mean(ce * ce, axis=-1, keepdims=True)
        parts.append(ce * lax.rsqrt(va + MLSTM_GN_EPS))
    y_m = jnp.concatenate(parts, axis=1) * gng_ref[...] * _sigmoid(zo_ref[...])

    gates = zs_ref[...]
    merged = (gates[:, 0:D_MODEL] * _dg(y_r.astype(BF16), wbr_ref[...], _NN)
              + gates[:, D_MODEL:2 * D_MODEL] * _dg(y_m.astype(BF16), wbm_ref[...], _NN))
    t = _dg(merged.astype(BF16), wout_ref[...], _NN)
    x1 = x_ref[...] + g1 * _rms(t, ng_ref[1:2, :])
    h2 = _rms(x1, ng_ref[2:3, :]) * (1.0 + sc2) + sh2
    u = _dg(h2.astype(BF16), wup_ref[...], _NN)
    if conv[0] == "seq":
        pre = _dwconv(u[:, 0:D_FF], cw_ref, conv[1], False) + cb_ref[...]
        act = (_silu(pre) * u[:, D_FF:2 * D_FF]).astype(BF16)
        g2 = mod[:, 5 * D_MODEL:6 * D_MODEL]
        out_ref[...] = x1 + g2 * _rms(_dg(act, wdn_ref[...], _NN), ng_ref[3:4, :])
    else:
        x1_ref[...] = x1
        _, width, tiles_per_image = conv
        act_scr, val_scr, tail_scr = scratch
        T = act_scr.shape[0]
        step = pl.program_id(0)
        pos = (step + tiles_per_image - 1) % tiles_per_image
        for c0 in range(0, D_FF, CONV_CH_TILE):
            ch = slice(c0, c0 + CONV_CH_TILE)
            above = jnp.where(pos != 0, tail_scr[:, ch], 0.0)
            below = jnp.where(pos != tiles_per_image - 1, u[0:width, ch], 0.0)
            pre = _grid_conv(above, act_scr[:, ch], below, cw_ref.at[:, ch], width) + cb_ref[:, ch]
            act_ref[:, ch] = (_silu(pre) * val_scr[:, ch]).astype(BF16)
        tail_scr[...] = act_scr[T - width:T, :]
        act_scr[...] = u[:, 0:D_FF]
        val_scr[...] = u[:, D_FF:2 * D_FF]


def _merge(x2, mod, mod_row, z, ysf, ysb, y0, bonus, gate, hf, hb, p, conv):
    n = x2.shape[0]
    W = RWKV_WIDTH
    rows = MERGE_TILE
    ntiles = n // rows
    delayed = conv[0] == "grid"
    cur = (lambda i: jnp.minimum(i, ntiles - 1)) if delayed else (lambda i: i)
    tile = lambda w: pl.BlockSpec((rows, w), lambda i: (cur(i), 0))
    act_spec = pl.BlockSpec((rows, D_FF), (lambda i: (jnp.maximum(i - 1, 0), 0)) if delayed else (lambda i: (i, 0)))
    scratch = ([pltpu.VMEM((rows, D_FF), F32), pltpu.VMEM((rows, D_FF), F32), pltpu.VMEM((conv[1], D_FF), F32)]
               if delayed else [])
    if delayed:
        extra_in, extra_args = [], []
        out_specs = [tile(D_MODEL), act_spec]
        out_shape = [jax.ShapeDtypeStruct((n, D_MODEL), F32), jax.ShapeDtypeStruct((n, D_FF), BF16)]
    else:
        extra_in, extra_args = [_resident((D_FF, D_MODEL))], [p["ffn_down"]]
        out_specs = [tile(D_MODEL)]
        out_shape = [jax.ShapeDtypeStruct((n, D_MODEL), F32)]
    return pl.pallas_call(
        functools.partial(_merge_kernel, conv),
        grid=(ntiles + 1 if delayed else ntiles,),
        in_specs=[tile(D_MODEL),
                  pl.BlockSpec((1, 1, 6 * D_MODEL), lambda i: (mod_row(cur(i) * rows), 0, 0)),
                  tile(W), tile(W), tile(W), tile(W), tile(W), tile(MLSTM_WIDTH), tile(MLSTM_WIDTH),
                  pl.BlockSpec((rows, MLSTM_WIDTH),
                               lambda i: (cur(i), (ZM_OFF + 3 * MLSTM_WIDTH) // MLSTM_WIDTH)),
                  pl.BlockSpec((rows, GATE_COLS), lambda i: (cur(i), ZS_OFF // GATE_COLS)),
                  _resident((1, W)), _resident((1, W)), _resident((1, MLSTM_WIDTH)),
                  _resident((W, W)),
                  _resident((W, D_MODEL)), _resident((MLSTM_WIDTH, D_MODEL)),
                  _resident((D_MODEL, D_MODEL)), _resident((4, D_MODEL)),
                  _resident((D_MODEL, 2 * D_FF)), _resident((9, D_FF)), _resident((1, D_FF))] + extra_in,
        out_specs=out_specs,
        out_shape=out_shape,
        scratch_shapes=scratch,
        compiler_params=_params(("arbitrary",)),
        name="merge_ffn_up",
    )(x2, mod, ysf, ysb, y0, bonus, gate, hf, hb, z, z, p["lnx_g"], p["lnx_b"], p["gn_g"], p["pmean"],
      p["w_br"], p["w_bm"], p["w_out"], p["norm_g"], p["ffn_up"], p["ffn_conv"], p["ffn_conv_b"], *extra_args)


def _down_kernel(x1_ref, mod_ref, a_ref, w_ref, ng_ref, o_ref):
    g2 = mod_ref[0][:, 5 * D_MODEL:6 * D_MODEL]
    f = _dg(a_ref[...], w_ref[...], _NN)
    o_ref[...] = x1_ref[...] + g2 * _rms(f, ng_ref[3:4, :])


def _down(x1, mod, mod_row, act, p):
    n = x1.shape[0]
    tile = DOWN_TILE
    return pl.pallas_call(
        _down_kernel,
        grid=(n // tile,),
        in_specs=[pl.BlockSpec((tile, D_MODEL), lambda i: (i, 0)),
                  pl.BlockSpec((1, 1, 6 * D_MODEL), lambda i: (mod_row(i * tile), 0, 0)),
                  pl.BlockSpec((tile, D_FF), lambda i: (i, 0)),
                  _resident((D_FF, D_MODEL)), _resident((4, D_MODEL))],
        out_specs=pl.BlockSpec((tile, D_MODEL), lambda i: (i, 0)),
        out_shape=jax.ShapeDtypeStruct((n, D_MODEL), F32),
        compiler_params=_params(("arbitrary",)),
        name="ffn_down",
    )(x1, mod, act, p["ffn_down"], p["norm_g"])


RWKV_LOCAL_PASSES = 1


def _state_to_pairs(s):
    b = s.shape[0]
    s = s.reshape(b, N_DIR, RWKV_PAIRS, 2, RWKV_HEAD_DIM, RWKV_HEAD_DIM)
    zero = jnp.zeros_like(s[:, :, :, 0])
    top = jnp.concatenate([s[:, :, :, 0], zero], axis=-1)
    bot = jnp.concatenate([zero, s[:, :, :, 1]], axis=-1)
    return jnp.concatenate([top, bot], axis=-2)


def _trunk(x, mod, mod_row, rows, states, p):
    batch, seq_len, _ = x.shape
    n = batch * seq_len
    x2 = x.reshape(n, D_MODEL)
    nst = N_DIR * MLSTM_HEADS
    if states is None:
        s0 = c0 = n0 = m0 = None
    else:
        s0, c0, n0, m0 = states
        s0 = _state_to_pairs(s0)
        c0 = jnp.swapaxes(c0, -1, -2).reshape(batch, nst, MLSTM_HEAD_DIM, MLSTM_HEAD_DIM)
        n0 = n0.reshape(batch, nst, MLSTM_HEAD_DIM)
        m0 = jnp.broadcast_to(m0.reshape(batch, nst, 1), (batch, nst, LANE))

    fuse_width = seq_len if (rows == 1 and IN_TILE % seq_len == 0 and MERGE_TILE % seq_len == 0) else None
    proj = _in_proj(x2, mod, mod_row, p["norm_g"][0:1], p["w_in"], p["mlstm_conv"], fuse_width)
    z = proj[0]

    rp, y0, gm, hm, gate, bonus = _rwkv_local(z, seq_len, p, RWKV_LOCAL_PASSES)
    ysf, ysb, s_fin = _rwkv_scan(s0, rp, gm, hm, batch, seq_len)

    if fuse_width is None:
        z, gt = proj
        qk, kt = _qk_conv(z, batch, seq_len, rows, p["mlstm_conv"])
        qk_blk = 0
    else:
        z, gt, kt = proj
        qk, qk_blk = z, ZM_OFF // (2 * MLSTM_WIDTH)
    hf, hb, c_fin, n_fin, m_fin = _mlstm_scan(z, qk, qk_blk, kt, gt, p["gate_bc"], p["gate_br"],
                                              c0, n0, m0, batch, seq_len)

    if fuse_width is not None:
        ffn_conv = ("seq", fuse_width)
    else:
        width = seq_len // rows
        assert rows > 1 and width & (width - 1) == 0 and MERGE_TILE % width == 0 and seq_len % MERGE_TILE == 0
        ffn_conv = ("grid", width, seq_len // MERGE_TILE)
    res = _merge(x2, mod, mod_row, z, ysf, ysb, y0, bonus, gate, hf, hb, p, ffn_conv)
    if ffn_conv[0] == "seq":
        out, = res
    else:
        x1, act = res
        out = _down(x1, mod, mod_row, act, p)

    new_states = (s_fin,
                  c_fin.reshape(batch, N_DIR, MLSTM_HEADS, MLSTM_HEAD_DIM, MLSTM_HEAD_DIM),
                  n_fin.reshape(batch, N_DIR, MLSTM_HEADS, MLSTM_HEAD_DIM),
                  m_fin[:, :, 0].reshape(batch, N_DIR, MLSTM_HEADS))
    return out.reshape(batch, seq_len, D_MODEL), new_states


def _pack_layer(l, ada_w, ada_b, norm_g, w_in, rwkv_mu, rwkv_w0, rwkv_w_up, rwkv_a0, rwkv_a_up,
                rwkv_g_up, rwkv_kk_scale, rwkv_k_a, rwkv_r_k, rwkv_lnx_g, rwkv_lnx_b, mlstm_conv,
                mlstm_gate_b, mlstm_gn_g, w_branch_rwkv, w_branch_mlstm, w_out, ffn_up, ffn_conv,
                ffn_conv_b, ffn_down):
    W = RWKV_WIDTH
    w_in_b = w_in[l].astype(BF16)

    head = jnp.arange(W, dtype=jnp.int32) // RWKV_HEAD_DIM
    same = (head[:, None] == head[None, :])
    gb = mlstm_gate_b[l].reshape(1, MLSTM_GATES)
    return dict(
        ada_w=ada_w[l], ada_b=ada_b[l], norm_g=norm_g[l], w_in=w_in_b,
        mu=rwkv_mu[l].reshape(1, RWKV_COLS),
        w0=rwkv_w0[l].reshape(N_DIR, 1, W), w_up=rwkv_w_up[l],
        a0=rwkv_a0[l].reshape(N_DIR, 1, W), a_up=rwkv_a_up[l], g_up=rwkv_g_up[l],
        kk_scale=rwkv_kk_scale[l].reshape(1, W), k_a=rwkv_k_a[l].reshape(1, W),
        r_k=rwkv_r_k[l].reshape(1, W),
        lnx_g=rwkv_lnx_g[l].reshape(1, W), lnx_b=rwkv_lnx_b[l].reshape(1, W),
        pones=same.astype(BF16), pmean=(same.astype(F32) / RWKV_HEAD_DIM).astype(BF16),
        mlstm_conv=mlstm_conv[l].reshape(9, 2 * MLSTM_WIDTH),
        gate_bc=jnp.pad(gb, ((0, 0), (0, LANE - MLSTM_GATES))), gate_br=gb.reshape(MLSTM_GATES, 1),
        gn_g=mlstm_gn_g[l].reshape(1, MLSTM_WIDTH),
        w_br=w_branch_rwkv[l].astype(BF16), w_bm=w_branch_mlstm[l].astype(BF16),
        w_out=w_out[l].astype(BF16), ffn_up=ffn_up[l].astype(BF16),
        ffn_conv=ffn_conv[l].reshape(9, D_FF), ffn_conv_b=ffn_conv_b[l].reshape(1, D_FF),
        ffn_down=ffn_down[l].astype(BF16),
    )


def kernel(x_prompt, x_sample, c, state_rwkv, state_mlstm_C, state_mlstm_n, state_mlstm_m, c_ctx,
           ada_w, ada_b, norm_g, w_in, rwkv_mu, rwkv_w0, rwkv_w_up, rwkv_a0, rwkv_a_up, rwkv_g_up,
           rwkv_kk_scale, rwkv_k_a, rwkv_r_k, rwkv_lnx_g, rwkv_lnx_b, mlstm_conv, mlstm_gate_b,
           mlstm_gn_g, w_branch_rwkv, w_branch_mlstm, w_out, ffn_up, ffn_conv, ffn_conv_b, ffn_down):
    depth = ada_w.shape[0]
    batch = x_prompt.shape[0]
    dec_batch, dec_seq, _ = x_sample.shape
    latent_rows = dec_seq // GRID_W
    cond = jnp.concatenate([c_ctx[None, :], c, jnp.zeros((8 - 1 - dec_batch, D_MODEL), F32)], axis=0)

    xp, xs = x_prompt, x_sample
    new_s, new_c, new_n, new_m = [], [], [], []
    for l in range(depth):
        p = _pack_layer(l, ada_w, ada_b, norm_g, w_in, rwkv_mu, rwkv_w0, rwkv_w_up, rwkv_a0, rwkv_a_up,
                        rwkv_g_up, rwkv_kk_scale, rwkv_k_a, rwkv_r_k, rwkv_lnx_g, rwkv_lnx_b, mlstm_conv,
                        mlstm_gate_b, mlstm_gn_g, w_branch_rwkv, w_branch_mlstm, w_out, ffn_up, ffn_conv,
                        ffn_conv_b, ffn_down)
        mod = _ada(cond, p["ada_w"], p["ada_b"]).reshape(8, 1, 6 * D_MODEL)
        xp, (s, cc, nn, mm) = _trunk(xp, mod, lambda r: 0, 1, None, p)
        new_s.append(s)
        new_c.append(cc)
        new_n.append(nn)
        new_m.append(mm)
        xs, _ = _trunk(xs, mod, lambda r: 1 + r // dec_seq, latent_rows,
                       (state_rwkv[:, l], state_mlstm_C[:, l], state_mlstm_n[:, l], state_mlstm_m[:, l]), p)
    return (xp, xs, jnp.stack(new_s, axis=1), jnp.stack(new_c, axis=1),
            jnp.stack(new_n, axis=1), jnp.stack(new_m, axis=1))
```

```python
import functools

import jax
import jax.numpy as jnp
from jax import lax
from jax.experimental import pallas as pl
from jax.experimental.pallas import tpu as pltpu

F32 = jnp.float32
BF16 = jnp.bfloat16

D_MODEL = 1024
N_DIR = 2
RWKV_HEADS = 8
RWKV_HEAD_DIM = 64
RWKV_WIDTH = RWKV_HEADS * RWKV_HEAD_DIM
DECAY_LORA = 64
ICLR_LORA = 64
GATE_LORA = 128
MLSTM_HEADS = 4
MLSTM_HEAD_DIM = 128
MLSTM_WIDTH = MLSTM_HEADS * MLSTM_HEAD_DIM
MLSTM_CHUNK = 64
D_FF = 2816
GRID_W = 64
RMS_EPS = 1e-6
RWKV_GN_EPS = 64e-5
MLSTM_GN_EPS = 1e-5
DECAY_SCALE = 0.606531

RWKV_COLS = 3 * RWKV_WIDTH + N_DIR * DECAY_LORA + N_DIR * ICLR_LORA + GATE_LORA
MLSTM_GATES = 2 * N_DIR * MLSTM_HEADS
MLSTM_COLS = 4 * MLSTM_WIDTH + MLSTM_GATES
GATE_COLS = 2 * D_MODEL

LANE = 128
ZR_BLOCK = 2048
ZG_OFF = RWKV_COLS
ZM_OFF = ZR_BLOCK
ZS_OFF = ZM_OFF + 4 * MLSTM_WIDTH
Z_COLS = ZS_OFF + GATE_COLS

IN_TILE = 512
MERGE_TILE = 256
DOWN_TILE = 512
RCHUNK = 64
CONV_CH_TILE = 256
VMEM_LIMIT = 56 * 1024 * 1024


def _params(sem):
    return pltpu.CompilerParams(dimension_semantics=sem, vmem_limit_bytes=VMEM_LIMIT)


def _resident(shape):
    nd = len(shape)
    return pl.BlockSpec(shape, lambda *_: (0,) * nd, pipeline_mode=pl.Buffered(1))


def _split2(a):
    hi = a.astype(BF16)
    lo = (a - hi.astype(F32)).astype(BF16)
    return hi, lo


def _split3(a):
    hi = a.astype(BF16)
    r1 = a - hi.astype(F32)
    mid = r1.astype(BF16)
    lo = (r1 - mid.astype(F32)).astype(BF16)
    return hi, mid, lo


def _dg(a, b, dims):
    return lax.dot_general(a, b, dims, preferred_element_type=F32)


def _mm(a, b, dims, passes):
    if passes == 1:
        return _dg(a.astype(BF16), b.astype(BF16), dims)
    ah, al = _split2(a)
    bh, bl = _split2(b)
    return _dg(ah, bh, dims) + (_dg(ah, bl, dims) + _dg(al, bh, dims))


def _mm_exact_lhs(a_bf16, b, dims):
    b1, b2, b3 = _split3(b)
    return _dg(a_bf16, b1, dims) + (_dg(a_bf16, b2, dims) + _dg(a_bf16, b3, dims))


def _mm_exact_rhs(a, b_bf16, dims, pieces=3):
    if pieces == 2:
        a1, a2 = _split2(a)
        return _dg(a1, b_bf16, dims) + _dg(a2, b_bf16, dims)
    a1, a2, a3 = _split3(a)
    return _dg(a1, b_bf16, dims) + (_dg(a2, b_bf16, dims) + _dg(a3, b_bf16, dims))


_NN = (((1,), (0,)), ((), ()))
_NT = (((1,), (1,)), ((), ()))
_TN = (((0,), (0,)), ((), ()))
_BNN = (((2,), (1,)), ((0,), (0,)))
_BNT = (((2,), (2,)), ((0,), (0,)))
_BTN = (((1,), (1,)), ((0,), (0,)))


def _sigmoid(x):
    return jax.nn.sigmoid(x)


def _silu(x):
    return x * jax.nn.sigmoid(x)


def _rms(x, g):
    return x * lax.rsqrt(jnp.mean(x * x, axis=-1, keepdims=True) + RMS_EPS) * g


def _ada_kernel(cond_ref, w_ref, b_ref, o_ref):
    s = _silu(cond_ref[...])
    o_ref[...] = _dg(s.astype(BF16), w_ref[...].astype(BF16), _NN) + b_ref[...]


def _ada(cond8, ada_w, ada_b):
    n = ada_w.shape[1]
    tn = 1536
    return pl.pallas_call(
        _ada_kernel,
        grid=(n // tn,),
        in_specs=[_resident((8, D_MODEL)),
                  pl.BlockSpec((D_MODEL, tn), lambda j: (0, j)),
                  pl.BlockSpec((1, tn), lambda j: (0, j))],
        out_specs=pl.BlockSpec((8, tn), lambda j: (0, j)),
        out_shape=jax.ShapeDtypeStruct((8, n), F32),
        compiler_params=_params(("arbitrary",)),
        name="ada_mod",
    )(cond8, ada_w, ada_b.reshape(1, n))


def _in_kernel(conv_width, x_ref, mod_ref, g_ref, w_ref, cw_ref, z_ref, gt_ref, *kt_ref):
    mod = mod_ref[0]
    sh = mod[:, 0:D_MODEL]
    sc = mod[:, D_MODEL:2 * D_MODEL]
    h = (_rms(x_ref[...], g_ref[...]) * (1.0 + sc) + sh).astype(BF16)
    m_off = RWKV_COLS
    t_off = RWKV_COLS + 4 * MLSTM_WIDTH
    z_ref[:, 0:ZG_OFF] = _dg(h, w_ref[:, 0:m_off], _NN)
    tail = _dg(h, w_ref[:, t_off:w_ref.shape[1]], _NN)
    zg = tail[:, 0:LANE]
    z_ref[:, ZG_OFF:ZM_OFF] = zg
    zgt = zg.T
    for c in range(zg.shape[0] // MLSTM_CHUNK):
        gt_ref[c] = zgt[0:MLSTM_GATES, c * MLSTM_CHUNK:(c + 1) * MLSTM_CHUNK]
    z_ref[:, ZS_OFF:Z_COLS] = _sigmoid(tail[:, MLSTM_GATES:MLSTM_GATES + GATE_COLS])
    zm = _dg(h, w_ref[:, m_off:t_off], _NN)
    if conv_width is None:
        z_ref[:, ZM_OFF:ZS_OFF] = zm
    else:
        qk_cols = 2 * MLSTM_WIDTH
        qk = _silu(_dwconv(zm[:, 0:qk_cols], cw_ref, conv_width, False))
        z_ref[:, ZM_OFF:ZM_OFF + qk_cols] = qk
        z_ref[:, ZM_OFF + qk_cols:ZS_OFF] = zm[:, qk_cols:]
        _store_time_on_lanes(qk[:, MLSTM_WIDTH:qk_cols], kt_ref[0], MLSTM_CHUNK)


def _in_proj(x2, mod, mod_row, norm_g0, w_in, conv_w9, conv_width):
    n = x2.shape[0]
    tile = IN_TILE
    cpt = tile // MLSTM_CHUNK
    nchunk = n // MLSTM_CHUNK
    out_specs = [pl.BlockSpec((tile, Z_COLS), lambda i: (i, 0)),
                 pl.BlockSpec((cpt, MLSTM_GATES, MLSTM_CHUNK), lambda i: (i, 0, 0))]
    out_shape = [jax.ShapeDtypeStruct((n, Z_COLS), F32),
                 jax.ShapeDtypeStruct((nchunk, MLSTM_GATES, MLSTM_CHUNK), F32)]
    if conv_width is not None:
        out_specs.append(pl.BlockSpec((cpt, MLSTM_WIDTH, MLSTM_CHUNK), lambda i: (i, 0, 0)))
        out_shape.append(jax.ShapeDtypeStruct((nchunk, MLSTM_WIDTH, MLSTM_CHUNK), F32))
    return pl.pallas_call(
        functools.partial(_in_kernel, conv_width),
        grid=(n // tile,),
        in_specs=[pl.BlockSpec((tile, D_MODEL), lambda i: (i, 0)),
                  pl.BlockSpec((1, 1, 6 * D_MODEL), lambda i: (mod_row(i * tile), 0, 0)),
                  _resident((1, D_MODEL)), _resident(w_in.shape), _resident(conv_w9.shape)],
        out_specs=out_specs,
        out_shape=out_shape,
        compiler_params=_params(("arbitrary",)),
        name="in_proj",
    )(x2, mod, norm_g0, w_in, conv_w9)


LOCAL_CHUNKS = 4
PAIR_LANES = 2 * RWKV_HEAD_DIM
RWKV_PAIRS = RWKV_HEADS // 2


def _bd(x):
    lane = lax.broadcasted_iota(jnp.int32, x.shape, 1)
    left = lane < RWKV_HEAD_DIM
    return jnp.concatenate([jnp.where(left, x, 0.0), jnp.where(left, 0.0, x)], axis=0)


def _rwkv_local_kernel(chunks_per_seq, passes, NS,
                       z_ref, zp_ref, zn_ref, mu_ref, w0_ref, wup_ref, a0_ref, aup_ref, gup_ref,
                       kks_ref, ka_ref, rk_ref, pones_ref,
                       rp_ref, y0_ref, gm_ref, hm_ref, gate_ref, bonus_ref):
    C = RCHUNK
    W = RWKV_WIDTH
    R = NS * C
    first = (pl.program_id(0) * NS) % chunks_per_seq
    has_prev = first != 0
    has_next = first + NS != chunks_per_seq

    z = z_ref[:, 0:RWKV_COLS]
    zp = jnp.where(has_prev, zp_ref[7:8, 0:RWKV_COLS], 0.0)
    zn = jnp.where(has_next, zn_ref[0:1, 0:RWKV_COLS], 0.0)
    trow = lax.broadcasted_iota(jnp.int32, (R, 1), 0)
    prev = jnp.where(trow == 0, zp, pltpu.roll(z, 1, 0))
    nxt = jnp.where(trow == R - 1, zn, pltpu.roll(z, R - 1, 0))
    zs = z + mu_ref[...] * (0.5 * (prev + nxt) - z)

    r = zs[:, 0:W]
    k = zs[:, W:2 * W]
    v = zs[:, 2 * W:3 * W]
    gd = zs[:, 3 * W + 2 * DECAY_LORA + 2 * ICLR_LORA:RWKV_COLS]
    gate_ref[...] = _dg(_sigmoid(gd).astype(BF16), gup_ref[...].astype(BF16), _NN)

    pones = pones_ref[...]
    kks = k * kks_ref[...]
    norm = jnp.sqrt(_mm_exact_rhs(kks * kks, pones, _NN, pieces=2))
    kk = kks / jnp.maximum(norm, 1e-12)

    P = PAIR_LANES
    row = lax.broadcasted_iota(jnp.int32, (R, R), 0)
    col = lax.broadcasted_iota(jnp.int32, (R, R), 1)
    same_chunk = jnp.bitwise_and(row, -C) == jnp.bitwise_and(col, -C)
    prow = lax.broadcasted_iota(jnp.int32, (C, P), 0)
    pcol = jnp.bitwise_and(lax.broadcasted_iota(jnp.int32, (C, P), 1), RWKV_HEAD_DIM - 1)
    eye_p = jnp.where(prow == pcol, 1.0, 0.0)
    left_head = lax.broadcasted_iota(jnp.int32, (C, P), 1) < RWKV_HEAD_DIM

    def diag_blocks(m):
        return jnp.where(left_head, m[0:RWKV_HEAD_DIM], m[RWKV_HEAD_DIM:P])

    abar, rbar, kt, bt, kw, bw, wc, strict, incl = [], [], [], [], [], [], [], [], []
    kd_sum = None
    for d in range(N_DIR):
        o = 3 * W + d * DECAY_LORA
        wd = zs[:, o:o + DECAY_LORA]
        o = 3 * W + 2 * DECAY_LORA + d * ICLR_LORA
        ad = zs[:, o:o + ICLR_LORA]
        logw = -DECAY_SCALE * _sigmoid(w0_ref[d] + _dg(jnp.tanh(wd).astype(BF16), wup_ref[d].astype(BF16), _NN))
        a = _sigmoid(a0_ref[d] + _dg(ad.astype(BF16), aup_ref[d].astype(BF16), _NN))
        kd = k * (1.0 + (a - 1.0) * ka_ref[...])
        b = kk * a
        kd_sum = kd if kd_sum is None else kd_sum + kd

        earlier_or_same = same_chunk & ((row >= col) if d == 0 else (row <= col))
        cum_i = _mm_exact_lhs(jnp.where(earlier_or_same, 1.0, 0.0).astype(BF16), logw, _NN)
        cum_e = cum_i - logw
        ab_d, rb_d, kt_d, bt_d, kw_d, bw_d, wc_d = [], [], [], [], [], [], []
        for s in range(NS):
            rs = slice(s * C, (s + 1) * C)
            ci_s = cum_i[rs]
            ctot = jnp.sum(logw[rs], axis=0, keepdims=True)
            e_ni = jnp.exp(-ci_s)
            e_ti = jnp.exp(ctot - ci_s)
            ab_d.append(kk[rs] * jnp.exp(cum_e[rs]))
            rb_d.append(r[rs] * jnp.exp(ci_s))
            kt_d.append(kd[rs] * e_ni)
            bt_d.append(b[rs] * e_ni)
            kw_d.append(kd[rs] * e_ti)
            bw_d.append(b[rs] * e_ti)
            wc_d.append(jnp.exp(ctot))
        abar.append(ab_d)
        rbar.append(rb_d)
        kt.append(kt_d)
        bt.append(bt_d)
        kw.append(kw_d)
        bw.append(bw_d)
        wc.append(wc_d)
        strict.append((prow > pcol) if d == 0 else (prow < pcol))
        incl.append((prow >= pcol) if d == 0 else (prow <= pcol))
    bonus_ref[...] = _mm_exact_rhs(r * kd_sum * rk_ref[...], pones, _NN, pieces=2) * v

    mm = functools.partial(_mm, passes=passes)
    chains = [(s, d, p) for s in range(NS) for d in range(N_DIR) for p in range(RWKV_PAIRS)]
    nch = range(len(chains))

    def sel(arr, i):
        s, d, p = chains[i]
        return arr[d][s][:, p * P:(p + 1) * P]

    cat0 = lambda a_, b_: jnp.concatenate([a_, b_], axis=0)
    cat1 = lambda a_, b_: jnp.concatenate([a_, b_], axis=1)
    vsl = [v[s * C:(s + 1) * C, p * P:(p + 1) * P] for s, _, p in chains]
    lhs = [cat0(sel(abar, i), sel(rbar, i)) for i in nch]
    by = [mm(lhs[i], cat0(_bd(sel(bt, i)), _bd(sel(kt, i))), _NT) for i in nch]
    a_kk = [jnp.where(strict[chains[i][1]], by[i][0:C, 0:P], 0.0) for i in nch]
    a_rb = [jnp.where(incl[chains[i][1]], by[i][C:2 * C, 0:P], 0.0) for i in nch]
    a_kv = [jnp.where(strict[chains[i][1]], by[i][0:C, P:2 * P], 0.0) for i in nch]
    a_rk = [jnp.where(incl[chains[i][1]], by[i][C:2 * C, P:2 * P], 0.0) for i in nch]
    on_v = [mm(cat0(a_kv[i], a_rk[i]), _bd(vsl[i]), _NN) for i in nch]

    x = [-m for m in a_kk]
    tinv = [eye_p + m for m in x]
    x = [mm(m, _bd(m), _NN) for m in x]
    for _ in range(4):
        both = [mm(cat0(tinv[i], x[i]), _bd(x[i]), _NN) for i in nch]
        tinv = [tinv[i] + both[i][0:C] for i in nch]
        x = [m[C:2 * C] for m in both]
    tinv = [tinv[i] + mm(tinv[i], _bd(x[i]), _NN) for i in nch]

    solved = [mm(tinv[i], cat1(_bd(sel(abar, i)), _bd(on_v[i][0:C])), _NN) for i in nch]
    ap = [m[:, 0:P] for m in solved]
    u0 = [m[:, P:2 * P] for m in solved]
    corr = [mm(a_rb[i], cat1(_bd(ap[i]), _bd(u0[i])), _NN) for i in nch]
    on_b = [mm(cat1(ap[i], u0[i]), sel(bw, i), _TN) for i in nch]
    vk = [mm(vsl[i], sel(kw, i), _TN) for i in nch]
    for i in nch:
        s, d, p = chains[i]
        rows = slice(s * C, (s + 1) * C)
        lanes = slice(p * P, (p + 1) * P)
        rp_ref[d, rows, lanes] = (sel(rbar, i) - corr[i][:, 0:P]).astype(BF16)
        gm_ref[d, s, p] = (eye_p * sel(wc, i) - diag_blocks(on_b[i][0:P])).astype(BF16)
        hm_ref[d, s, p] = diag_blocks(vk[i] - on_b[i][P:2 * P])
    for s in range(NS):
        for p in range(RWKV_PAIRS):
            f, b_ = chains.index((s, 0, p)), chains.index((s, 1, p))
            y0_ref[s * C:(s + 1) * C, p * P:(p + 1) * P] = ((on_v[f][C:2 * C] - corr[f][:, P:2 * P])
                                                            + (on_v[b_][C:2 * C] - corr[b_][:, P:2 * P]))


def _rwkv_local(z, seq_len, p, passes):
    n = z.shape[0]
    nchunk = n // RCHUNK
    cps = seq_len // RCHUNK
    ns = min(LOCAL_CHUNKS, cps)
    assert cps % ns == 0
    W = RWKV_WIDTH
    rows = ns * RCHUNK
    hb = rows // 8
    last8 = n // 8 - 1
    mat = lambda dt: jax.ShapeDtypeStruct((N_DIR, nchunk, RWKV_PAIRS, RWKV_HEAD_DIM, PAIR_LANES), dt)
    mat_spec = pl.BlockSpec((N_DIR, ns, RWKV_PAIRS, RWKV_HEAD_DIM, PAIR_LANES), lambda c: (0, c, 0, 0, 0))
    tok = lambda dt: jax.ShapeDtypeStruct((N_DIR, n, W), dt)
    tok_spec = pl.BlockSpec((N_DIR, rows, W), lambda c: (0, c, 0))
    row_spec = pl.BlockSpec((rows, W), lambda c: (c, 0))
    return pl.pallas_call(
        functools.partial(_rwkv_local_kernel, cps, passes, ns),
        grid=(nchunk // ns,),
        in_specs=[pl.BlockSpec((rows, ZR_BLOCK), lambda c: (c, 0)),
                  pl.BlockSpec((8, ZR_BLOCK), lambda c: (jnp.maximum(c * hb - 1, 0), 0)),
                  pl.BlockSpec((8, ZR_BLOCK), lambda c: (jnp.minimum((c + 1) * hb, last8), 0)),
                  _resident((1, RWKV_COLS)),
                  _resident((N_DIR, 1, W)), _resident((N_DIR, DECAY_LORA, W)),
                  _resident((N_DIR, 1, W)), _resident((N_DIR, ICLR_LORA, W)),
                  _resident((GATE_LORA, W)),
                  _resident((1, W)), _resident((1, W)), _resident((1, W)),
                  _resident((W, W))],
        out_specs=[tok_spec, row_spec, mat_spec, mat_spec, row_spec, row_spec],
        out_shape=[tok(BF16), jax.ShapeDtypeStruct((n, W), F32), mat(BF16), mat(F32),
                   jax.ShapeDtypeStruct((n, W), F32), jax.ShapeDtypeStruct((n, W), F32)],
        compiler_params=_params(("arbitrary",)),
        name="rwkv_local",
    )(z, z, z, p["mu"], p["w0"], p["w_up"], p["a0"], p["a_up"], p["g_up"],
      p["kk_scale"], p["k_a"], p["r_k"], p["pones"])


SCAN_CHUNKS = 4


def _rwkv_scan_kernel(has_init, *refs):
    s0_ref = refs[0] if has_init else None
    (rpf_ref, rpb_ref, gmf_ref, gmb_ref, hmf_ref, hmb_ref,
     ysf_ref, ysb_ref, sout_ref, s_scr) = refs[1:] if has_init else refs

    @pl.when(pl.program_id(1) == 0)
    def _():
        s_scr[...] = s0_ref[0] if has_init else jnp.zeros(s_scr.shape, F32)

    K = SCAN_CHUNKS
    C = RCHUNK
    rp_ref, gm_ref, hm_ref, ys_ref = (rpf_ref, rpb_ref), (gmf_ref, gmb_ref), (hmf_ref, hmb_ref), (ysf_ref, ysb_ref)
    chains = [(d, p) for d in range(N_DIR) for p in range(RWKV_PAIRS)]
    lanes = [slice(p * PAIR_LANES, (p + 1) * PAIR_LANES) for _, p in chains]
    nch = range(len(chains))
    s = [s_scr[d, p] for d, p in chains]
    for j in range(K):
        at = (j, K - 1 - j)
        rows = [slice(at[d] * C, (at[d] + 1) * C) for d, _ in chains]
        sb = [m.astype(BF16) for m in s]
        y = [_dg(rp_ref[chains[i][0]][0, rows[i], lanes[i]], sb[i], _NT) for i in nch]
        sg = [_dg(sb[i], _bd(gm_ref[chains[i][0]][0, at[chains[i][0]], chains[i][1]]), _NN) for i in nch]
        for i in nch:
            d, p = chains[i]
            ys_ref[d][rows[i], lanes[i]] = y[i]
        s = [sg[i] + _bd(hm_ref[chains[i][0]][0, at[chains[i][0]], chains[i][1]]) for i in nch]
    for i in nch:
        d, p = chains[i]
        s_scr[d, p] = s[i]
        n = RWKV_HEAD_DIM
        sout_ref[0, d, 2 * p] = s[i][0:n, 0:n]
        sout_ref[0, d, 2 * p + 1] = s[i][n:2 * n, n:2 * n]


def _rwkv_scan(s0, rp, gm, hm, batch, seq_len):
    K = SCAN_CHUNKS
    spb = seq_len // (RCHUNK * K)
    n = batch * seq_len

    def fwd(b, s):
        return b * spb + s

    def bwd(b, s):
        return b * spb + spb - 1 - s

    def mat_spec(d, at):
        return pl.BlockSpec((1, K, RWKV_PAIRS, RWKV_HEAD_DIM, PAIR_LANES), lambda b, s: (d, at(b, s), 0, 0, 0))

    def tok_spec(d, at):
        return pl.BlockSpec((1, K * RCHUNK, RWKV_WIDTH), lambda b, s: (d, at(b, s), 0))

    st_spec = pl.BlockSpec((1, N_DIR, RWKV_PAIRS, PAIR_LANES, PAIR_LANES), lambda b, s: (b, 0, 0, 0, 0))
    ys = jax.ShapeDtypeStruct((n, RWKV_WIDTH), F32)
    has_init = s0 is not None
    return pl.pallas_call(
        functools.partial(_rwkv_scan_kernel, has_init),
        grid=(batch, spb),
        in_specs=([st_spec] if has_init else [])
        + [tok_spec(0, fwd), tok_spec(1, bwd),
           mat_spec(0, fwd), mat_spec(1, bwd), mat_spec(0, fwd), mat_spec(1, bwd)],
        out_specs=[pl.BlockSpec((K * RCHUNK, RWKV_WIDTH), lambda b, s: (fwd(b, s), 0)),
                   pl.BlockSpec((K * RCHUNK, RWKV_WIDTH), lambda b, s: (bwd(b, s), 0)),
                   pl.BlockSpec((1, N_DIR, RWKV_HEADS, RWKV_HEAD_DIM, RWKV_HEAD_DIM),
                                lambda b, s: (b, 0, 0, 0, 0))],
        out_shape=[ys, ys,
                   jax.ShapeDtypeStruct((batch, N_DIR, RWKV_HEADS, RWKV_HEAD_DIM, RWKV_HEAD_DIM), F32)],
        scratch_shapes=[pltpu.VMEM((N_DIR, RWKV_PAIRS, PAIR_LANES, PAIR_LANES), F32)],
        compiler_params=_params(("arbitrary", "arbitrary")),
        name="rwkv_scan",
    )(*([s0] if has_init else []), rp, rp, gm, gm, hm, hm)


CONV_BLOCK_ROWS = 2048


def _dwconv(x, w_ref, width, vertical):
    T = x.shape[0]
    t = lax.broadcasted_iota(jnp.int32, (T, 1), 0)
    assert width & (width - 1) == 0
    colp = jnp.bitwise_and(t, width - 1)
    xl = jnp.where(colp == 0, 0.0, pltpu.roll(x, 1, 0))
    xr = jnp.where(colp == width - 1, 0.0, pltpu.roll(x, T - 1, 0))

    def tap_row(i):
        return w_ref[3 * i:3 * i + 1, :] * xl + w_ref[3 * i + 1:3 * i + 2, :] * x + w_ref[3 * i + 2:3 * i + 3, :] * xr

    out = tap_row(1)
    if vertical:
        out = out + jnp.where(t < width, 0.0, pltpu.roll(tap_row(0), width, 0))
        out = out + jnp.where(t >= T - width, 0.0, pltpu.roll(tap_row(2), T - width, 0))
    return out


def _conv_geometry(n, seq_len, rows):
    if rows > 1:
        return seq_len, seq_len // rows, True
    block = CONV_BLOCK_ROWS if (n % CONV_BLOCK_ROWS == 0 and CONV_BLOCK_ROWS % seq_len == 0) else seq_len
    return block, seq_len, False


def _store_time_on_lanes(x, out_ref, chunk):
    xt = x.T
    for c in range(x.shape[0] // chunk):
        out_ref[c] = xt[:, c * chunk:(c + 1) * chunk]


def _qk_conv_kernel(width, vertical, first_k_tile, x_ref, w_ref, o_ref, kt_ref):
    out = _silu(_dwconv(x_ref[...], w_ref, width, vertical))
    o_ref[...] = out

    @pl.when(pl.program_id(1) >= first_k_tile)
    def _():
        _store_time_on_lanes(out, kt_ref, MLSTM_CHUNK)


def _qk_conv(z, batch, seq_len, rows, conv_w9):
    n = batch * seq_len
    ch = 2 * MLSTM_WIDTH
    tc = CONV_CH_TILE
    off = ZM_OFF // tc
    first_k = MLSTM_WIDTH // tc
    block, width, vertical = _conv_geometry(n, seq_len, rows)
    cpb = block // MLSTM_CHUNK
    return pl.pallas_call(
        functools.partial(_qk_conv_kernel, width, vertical, first_k),
        grid=(n // block, ch // tc),
        in_specs=[pl.BlockSpec((block, tc), lambda b, j: (b, off + j)),
                  pl.BlockSpec((9, tc), lambda b, j: (0, j))],
        out_specs=[pl.BlockSpec((block, tc), lambda b, j: (b, j)),
                   pl.BlockSpec((cpb, tc, MLSTM_CHUNK), lambda b, j: (b, jnp.maximum(j - first_k, 0), 0))],
        out_shape=[jax.ShapeDtypeStruct((n, ch), F32),
                   jax.ShapeDtypeStruct((n // MLSTM_CHUNK, MLSTM_WIDTH, MLSTM_CHUNK), F32)],
        compiler_params=_params(("arbitrary", "arbitrary")),
        name="mlstm_qk_conv",
    )(z, conv_w9)


MLSTM_STEP_CHUNKS = 4

def _mlstm_scan_kernel(has_init, qkf_ref, qkb_ref, ktf_ref, ktb_ref, vf_ref, vb_ref, gcf_ref, gcb_ref,
                       grf_ref, grb_ref, gbc_ref, gbr_ref, *refs):
    init_refs = refs[0:3] if has_init else None
    hf_ref, hb_ref, cout_ref, nout_ref, mout_ref, c_scr, n_scr, m_scr = refs[3:] if has_init else refs
    step = pl.program_id(1)
    L = MLSTM_CHUNK
    dh = MLSTM_HEAD_DIM
    H = MLSTM_HEADS

    @pl.when(step == 0)
    def _():
        for scr, k in zip((c_scr, n_scr, m_scr), range(3)):
            scr[...] = init_refs[k][0] if has_init else jnp.zeros(scr.shape, F32)

    K = MLSTM_STEP_CHUNKS
    R = K * L
    row = lax.broadcasted_iota(jnp.int32, (L, L), 0)
    col = lax.broadcasted_iota(jnp.int32, (L, L), 1)
    lower = (row >= col)
    upper = (row <= col)
    lower_b = jnp.where(lower, 1.0, 0.0).astype(BF16)
    upper_b = jnp.where(upper, 1.0, 0.0).astype(BF16)
    rrow = lax.broadcasted_iota(jnp.int32, (R, R), 0)
    rcol = lax.broadcasted_iota(jnp.int32, (R, R), 1)
    same_chunk = jnp.bitwise_and(rrow, -L) == jnp.bitwise_and(rcol, -L)
    neg_inf = jnp.full((), -jnp.inf, F32)

    gcol, grow, bcol, brow, btot = [], [], [], [], []
    ones_b = jnp.ones((L, LANE), BF16)
    for d in range(N_DIR):
        gc_ref, gr_ref = (gcf_ref, grf_ref) if d == 0 else (gcb_ref, grb_ref)
        gcol.append(gc_ref[...] + gbc_ref[...])
        grow.append((gr_ref[...] + gbr_ref[...][None]).reshape(K * MLSTM_GATES, L))
        before = same_chunk & ((rrow >= rcol) if d == 0 else (rrow <= rcol))
        bcol.append(_mm_exact_lhs(jnp.where(before, 1.0, 0.0).astype(BF16), jax.nn.log_sigmoid(gcol[d]), _NN))
        frow = jax.nn.log_sigmoid(grow[d])
        brow.append(_mm_exact_rhs(frow, upper_b if d == 0 else lower_b, _NN))
        btot.append(_mm_exact_rhs(frow, ones_b, _NN))

    units = [(j, d, h) for j in range(K) for d in range(N_DIR) for h in range(H)]
    nun = range(len(units))
    q, k, kt, v, vb, qb = [], [], [], [], [], []
    c_row, b_col, b_last = [], [], []
    for j, d, h in units:
        at = j if d == 0 else K - 1 - j
        rows = slice(at * L, (at + 1) * L)
        st = d * H + h
        gi, gf = st, 2 * H + st
        qk_ref, kt_ref, v_ref = (qkf_ref, ktf_ref, vf_ref) if d == 0 else (qkb_ref, ktb_ref, vb_ref)
        q.append(qk_ref[rows, h * dh:(h + 1) * dh] * (dh ** -0.5))
        k.append(qk_ref[rows, MLSTM_WIDTH + h * dh:MLSTM_WIDTH + (h + 1) * dh])
        kt.append(kt_ref[at, h * dh:(h + 1) * dh, :])
        v.append(v_ref[rows, h * dh:(h + 1) * dh])
        qb.append(q[-1].astype(BF16))
        vb.append(v[-1].astype(BF16))
        b_col.append(jnp.broadcast_to(bcol[d][rows, gf:gf + 1], (L, LANE)))
        c_row.append(grow[d][at * MLSTM_GATES + gi:at * MLSTM_GATES + gi + 1, :]
                     - brow[d][at * MLSTM_GATES + gf:at * MLSTM_GATES + gf + 1, :])
        b_last.append(btot[d][at * MLSTM_GATES + gf:at * MLSTM_GATES + gf + 1, :])

    last = [L - 1 if d == 0 else 0 for _, d, _ in units]
    qk_t = [_dg(qb[i], k[i].astype(BF16), _NT) for i in nun]
    rel = [jnp.where(lower if units[i][1] == 0 else upper, c_row[i], neg_inf) for i in nun]
    mx = [jnp.broadcast_to(jnp.max(rel[i], axis=-1, keepdims=True), (L, LANE)) for i in nun]
    m_loc = [b_col[i] + mx[i] for i in nun]
    s_loc = [qk_t[i] * jnp.exp(rel[i] - mx[i][:, 0:L]) for i in nun]
    s_v = [_dg(s_loc[i].astype(BF16), vb[i], _NN) for i in nun]
    s_sum = [jnp.broadcast_to(jnp.sum(s_loc[i], axis=-1, keepdims=True), (L, LANE)) for i in nun]
    cmax = [mx[i][last[i]:last[i] + 1, :] for i in nun]
    m_w = [b_last[i] + cmax[i] for i in nun]
    wj = [jnp.exp(c_row[i] - cmax[i][:, 0:L]) for i in nun]
    kv = [_dg((kt[i] * wj[i]).astype(BF16), vb[i], _NN) for i in nun]
    w_k = [_mm(jnp.broadcast_to(wj[i], (8, L)), k[i], _NN, 3)[0:1] for i in nun]

    nst = N_DIR * H
    c_st = [c_scr[st] for st in range(nst)]
    n_st = [n_scr[st:st + 1, :] for st in range(nst)]
    m_st = [m_scr[st:st + 1, :] for st in range(nst)]
    for j in range(K):
        idx = [j * nst + st for st in range(nst)]
        q_c = [_dg(qb[i], c_st[st].astype(BF16), _NN) for st, i in enumerate(idx)]
        for st, i in enumerate(idx):
            _, d, h = units[i]
            at = j if d == 0 else K - 1 - j
            h_ref = hf_ref if d == 0 else hb_ref
            log_inter = b_col[i] + m_st[st]
            m_s = jnp.maximum(log_inter, m_loc[i])
            inter = jnp.exp(log_inter - m_s)
            local = jnp.exp(m_loc[i] - m_s)
            q_n = jnp.broadcast_to(jnp.sum(q[i] * n_st[st], axis=-1, keepdims=True), (L, LANE))
            den = inter * q_n + local * s_sum[i]
            scale = 1.0 / jnp.maximum(jnp.abs(den), jnp.exp(-m_s))
            h_ref[at * L:(at + 1) * L, h * dh:(h + 1) * dh] = (inter * scale) * q_c[st] + (local * scale) * s_v[i]
            m_new = jnp.maximum(b_last[i] + m_st[st], m_w[i])
            carry = jnp.exp(b_last[i] + m_st[st] - m_new)
            fresh = jnp.exp(m_w[i] - m_new)
            c_st[st] = carry * c_st[st] + fresh * kv[i]
            n_st[st] = carry * n_st[st] + fresh * w_k[i]
            m_st[st] = m_new


    for st in range(nst):
        c_scr[st] = c_st[st]
        n_scr[st:st + 1, :] = n_st[st]
        m_scr[st:st + 1, :] = m_st[st]
    nout_ref[0] = n_scr[...]
    mout_ref[0] = m_scr[...]

    @pl.when(step == pl.num_programs(1) - 1)
    def _():
        for st in range(nst):
            cout_ref[0, st] = c_st[st].T


def _mlstm_scan(z, qk, qk_blk, kt, gt, gate_bc, gate_br, c0, n0, m0, batch, seq_len):
    K = MLSTM_STEP_CHUNKS
    L = K * MLSTM_CHUNK
    assert seq_len % L == 0
    cps = seq_len // L
    n = batch * seq_len
    W = MLSTM_WIDTH
    nst = N_DIR * MLSTM_HEADS
    dh = MLSTM_HEAD_DIM

    def fw(b, c):
        return b * cps + c

    def bw(b, c):
        return b * cps + cps - 1 - c

    vblk = (ZM_OFF + 2 * W) // W
    gblk = ZG_OFF // LANE
    has_init = c0 is not None
    state_specs = [pl.BlockSpec((1, nst, dh, dh), lambda b, c: (b, 0, 0, 0)),
                   pl.BlockSpec((1, nst, dh), lambda b, c: (b, 0, 0)),
                   pl.BlockSpec((1, nst, LANE), lambda b, c: (b, 0, 0))]
    return pl.pallas_call(
        functools.partial(_mlstm_scan_kernel, has_init),
        grid=(batch, cps),
        in_specs=[pl.BlockSpec((L, 2 * W), lambda b, c: (fw(b, c), qk_blk)),
                  pl.BlockSpec((L, 2 * W), lambda b, c: (bw(b, c), qk_blk)),
                  pl.BlockSpec((K, W, MLSTM_CHUNK), lambda b, c: (fw(b, c), 0, 0)),
                  pl.BlockSpec((K, W, MLSTM_CHUNK), lambda b, c: (bw(b, c), 0, 0)),
                  pl.BlockSpec((L, W), lambda b, c: (fw(b, c), vblk)),
                  pl.BlockSpec((L, W), lambda b, c: (bw(b, c), vblk)),
                  pl.BlockSpec((L, LANE), lambda b, c: (fw(b, c), gblk)),
                  pl.BlockSpec((L, LANE), lambda b, c: (bw(b, c), gblk)),
                  pl.BlockSpec((K, MLSTM_GATES, MLSTM_CHUNK), lambda b, c: (fw(b, c), 0, 0)),
                  pl.BlockSpec((K, MLSTM_GATES, MLSTM_CHUNK), lambda b, c: (bw(b, c), 0, 0)),
                  _resident((1, LANE)),
                  _resident((MLSTM_GATES, 1))] + (state_specs if has_init else []),
        out_specs=[pl.BlockSpec((L, W), lambda b, c: (fw(b, c), 0)),
                   pl.BlockSpec((L, W), lambda b, c: (bw(b, c), 0))] + state_specs,
        out_shape=[jax.ShapeDtypeStruct((n, W), F32), jax.ShapeDtypeStruct((n, W), F32),
                   jax.ShapeDtypeStruct((batch, nst, dh, dh), F32),
                   jax.ShapeDtypeStruct((batch, nst, dh), F32),
                   jax.ShapeDtypeStruct((batch, nst, LANE), F32)],
        scratch_shapes=[pltpu.VMEM((nst, dh, dh), F32), pltpu.VMEM((nst, dh), F32),
                        pltpu.VMEM((nst, LANE), F32)],
        compiler_params=_params(("arbitrary", "arbitrary")),
        name="mlstm_scan",
    )(qk, qk, kt, kt, z, z, z, z, gt, gt, gate_bc, gate_br, *([c0, n0, m0] if has_init else []))


def _grid_conv(above, cur, below, cw_ref, width):
    T = cur.shape[0]
    E = T + 2 * width
    ext = jnp.concatenate([above, cur, below], axis=0)
    colp = jnp.bitwise_and(lax.broadcasted_iota(jnp.int32, (E, 1), 0), width - 1)
    left = jnp.where(colp == 0, 0.0, pltpu.roll(ext, 1, 0))
    right = jnp.where(colp == width - 1, 0.0, pltpu.roll(ext, E - 1, 0))

    def tap_row(i):
        rows = slice(i * width, i * width + T)
        return (cw_ref[3 * i:3 * i + 1, :] * left[rows] + cw_ref[3 * i + 1:3 * i + 2, :] * ext[rows]
                + cw_ref[3 * i + 2:3 * i + 3, :] * right[rows])

    return tap_row(0) + tap_row(1) + tap_row(2)


def _merge_kernel(conv, x_ref, mod_ref, ysf_ref, ysb_ref, y0_ref, bonus_ref, gate_ref, hf_ref, hb_ref, zo_ref,
                  zs_ref, lnxg_ref, lnxb_ref, gng_ref, pmean_ref, wbr_ref, wbm_ref, wout_ref, ng_ref, wup_ref,
                  cw_ref, cb_ref, *rest):
    if conv[0] == "seq":
        wdn_ref, out_ref = rest
    else:
        wdn_ref, modp_ref, out_ref = rest[0:3]
        scratch = rest[3:]

        @pl.when(pl.program_id(0) == 0)
        def _():
            for ref in scratch:
                ref[...] = jnp.zeros(ref.shape, ref.dtype)

    mod = mod_ref[0]
    g1 = mod[:, 2 * D_MODEL:3 * D_MODEL]
    sh2 = mod[:, 3 * D_MODEL:4 * D_MODEL]
    sc2 = mod[:, 4 * D_MODEL:5 * D_MODEL]

    ys = (ysf_ref[...] + ysb_ref[...]) + y0_ref[...]
    pmean = pmean_ref[...]
    mean = _mm_exact_rhs(ys, pmean, _NN)
    cen = ys - mean
    var = _mm_exact_rhs(cen * cen, pmean, _NN)
    y_r = (cen * lax.rsqrt(var + RWKV_GN_EPS) * lnxg_ref[...] + lnxb_ref[...] + bonus_ref[...]) * gate_ref[...]

    hs = hf_ref[...] + hb_ref[...]
    parts = []
    for h in range(MLSTM_HEADS):
        hh = hs[:, h * MLSTM_HEAD_DIM:(h + 1) * MLSTM_HEAD_DIM]
        mu = jnp.mean(hh, axis=-1, keepdims=True)
        ce = hh - mu
        va = jnp.mean(ce * ce, axis=-1, keepdims=True)
        parts.append(ce * lax.rsqrt(va + MLSTM_GN_EPS))
    y_m = jnp.concatenate(parts, axis=1) * gng_ref[...] * _sigmoid(zo_ref[...])

    gates = zs_ref[...]
    merged = (gates[:, 0:D_MODEL] * _dg(y_r.astype(BF16), wbr_ref[...], _NN)
              + gates[:, D_MODEL:2 * D_MODEL] * _dg(y_m.astype(BF16), wbm_ref[...], _NN))
    t = _dg(merged.astype(BF16), wout_ref[...], _NN)
    x1 = x_ref[...] + g1 * _rms(t, ng_ref[1:2, :])
    h2 = _rms(x1, ng_ref[2:3, :]) * (1.0 + sc2) + sh2
    u = _dg(h2.astype(BF16), wup_ref[...], _NN)
    if conv[0] == "seq":
        pre = _dwconv(u[:, 0:D_FF], cw_ref, conv[1], False) + cb_ref[...]
        act = (_silu(pre) * u[:, D_FF:2 * D_FF]).astype(BF16)
        g2 = mod[:, 5 * D_MODEL:6 * D_MODEL]
        out_ref[...] = x1 + g2 * _rms(_dg(act, wdn_ref[...], _NN), ng_ref[3:4, :])
    else:
        _, width, tiles_per_image = conv
        act_scr, val_scr, tail_scr, x1_scr, act_ref = scratch
        T = act_scr.shape[0]
        step = pl.program_id(0)
        pos = (step + tiles_per_image - 1) % tiles_per_image
        for c0 in range(0, D_FF, CONV_CH_TILE):
            ch = slice(c0, c0 + CONV_CH_TILE)
            above = jnp.where(pos != 0, tail_scr[:, ch], 0.0)
            below = jnp.where(pos != tiles_per_image - 1, u[0:width, ch], 0.0)
            pre = _grid_conv(above, act_scr[:, ch], below, cw_ref.at[:, ch], width) + cb_ref[:, ch]
            act_ref[:, ch] = (_silu(pre) * val_scr[:, ch]).astype(BF16)
        g2 = modp_ref[0][:, 5 * D_MODEL:6 * D_MODEL]
        out_ref[...] = x1_scr[...] + g2 * _rms(_dg(act_ref[...], wdn_ref[...], _NN), ng_ref[3:4, :])
        tail_scr[...] = act_scr[T - width:T, :]
        act_scr[...] = u[:, 0:D_FF]
        val_scr[...] = u[:, D_FF:2 * D_FF]
        x1_scr[...] = x1


def _merge(x2, mod, mod_row, z, ysf, ysb, y0, bonus, gate, hf, hb, p, conv):
    n = x2.shape[0]
    W = RWKV_WIDTH
    rows = MERGE_TILE
    ntiles = n // rows
    delayed = conv[0] == "grid"
    cur = (lambda i: jnp.minimum(i, ntiles - 1)) if delayed else (lambda i: i)
    tile = lambda w: pl.BlockSpec((rows, w), lambda i: (cur(i), 0))
    prev = lambda i: jnp.maximum(i - 1, 0)
    extra_in, extra_args = [_resident((D_FF, D_MODEL))], [p["ffn_down"]]
    if delayed:
        scratch = [pltpu.VMEM((rows, D_FF), F32), pltpu.VMEM((rows, D_FF), F32), pltpu.VMEM((conv[1], D_FF), F32),
                   pltpu.VMEM((rows, D_MODEL), F32), pltpu.VMEM((rows, D_FF), BF16)]
        extra_in.append(pl.BlockSpec((1, 1, 6 * D_MODEL), lambda i: (mod_row(prev(i) * rows), 0, 0)))
        extra_args.append(mod)
        out_spec = pl.BlockSpec((rows, D_MODEL), lambda i: (prev(i), 0))
    else:
        scratch = []
        out_spec = tile(D_MODEL)
    out_specs = [out_spec]
    out_shape = [jax.ShapeDtypeStruct((n, D_MODEL), F32)]
    return pl.pallas_call(
        functools.partial(_merge_kernel, conv),
        grid=(ntiles + 1 if delayed else ntiles,),
        in_specs=[tile(D_MODEL),
                  pl.BlockSpec((1, 1, 6 * D_MODEL), lambda i: (mod_row(cur(i) * rows), 0, 0)),
                  tile(W), tile(W), tile(W), tile(W), tile(W), tile(MLSTM_WIDTH), tile(MLSTM_WIDTH),
                  pl.BlockSpec((rows, MLSTM_WIDTH),
                               lambda i: (cur(i), (ZM_OFF + 3 * MLSTM_WIDTH) // MLSTM_WIDTH)),
                  pl.BlockSpec((rows, GATE_COLS), lambda i: (cur(i), ZS_OFF // GATE_COLS)),
                  _resident((1, W)), _resident((1, W)), _resident((1, MLSTM_WIDTH)),
                  _resident((W, W)),
                  _resident((W, D_MODEL)), _resident((MLSTM_WIDTH, D_MODEL)),
                  _resident((D_MODEL, D_MODEL)), _resident((4, D_MODEL)),
                  _resident((D_MODEL, 2 * D_FF)), _resident((9, D_FF)), _resident((1, D_FF))] + extra_in,
        out_specs=out_specs,
        out_shape=out_shape,
        scratch_shapes=scratch,
        compiler_params=_params(("arbitrary",)),
        name="merge_ffn_up",
    )(x2, mod, ysf, ysb, y0, bonus, gate, hf, hb, z, z, p["lnx_g"], p["lnx_b"], p["gn_g"], p["pmean"],
      p["w_br"], p["w_bm"], p["w_out"], p["norm_g"], p["ffn_up"], p["ffn_conv"], p["ffn_conv_b"], *extra_args)


def _down_kernel(x1_ref, mod_ref, a_ref, w_ref, ng_ref, o_ref):
    g2 = mod_ref[0][:, 5 * D_MODEL:6 * D_MODEL]
    f = _dg(a_ref[...], w_ref[...], _NN)
    o_ref[...] = x1_ref[...] + g2 * _rms(f, ng_ref[3:4, :])


def _down(x1, mod, mod_row, act, p):
    n = x1.shape[0]
    tile = DOWN_TILE
    return pl.pallas_call(
        _down_kernel,
        grid=(n // tile,),
        in_specs=[pl.BlockSpec((tile, D_MODEL), lambda i: (i, 0)),
                  pl.BlockSpec((1, 1, 6 * D_MODEL), lambda i: (mod_row(i * tile), 0, 0)),
                  pl.BlockSpec((tile, D_FF), lambda i: (i, 0)),
                  _resident((D_FF, D_MODEL)), _resident((4, D_MODEL))],
        out_specs=pl.BlockSpec((tile, D_MODEL), lambda i: (i, 0)),
        out_shape=jax.ShapeDtypeStruct((n, D_MODEL), F32),
        compiler_params=_params(("arbitrary",)),
        name="ffn_down",
    )(x1, mod, act, p["ffn_down"], p["norm_g"])


RWKV_LOCAL_PASSES = 1


def _state_to_pairs(s):
    b = s.shape[0]
    s = s.reshape(b, N_DIR, RWKV_PAIRS, 2, RWKV_HEAD_DIM, RWKV_HEAD_DIM)
    zero = jnp.zeros_like(s[:, :, :, 0])
    top = jnp.concatenate([s[:, :, :, 0], zero], axis=-1)
    bot = jnp.concatenate([zero, s[:, :, :, 1]], axis=-1)
    return jnp.concatenate([top, bot], axis=-2)


def _trunk(x, mod, mod_row, rows, states, p):
    batch, seq_len, _ = x.shape
    n = batch * seq_len
    x2 = x.reshape(n, D_MODEL)
    nst = N_DIR * MLSTM_HEADS
    if states is None:
        s0 = c0 = n0 = m0 = None
    else:
        s0, c0, n0, m0 = states
        s0 = _state_to_pairs(s0)
        c0 = jnp.swapaxes(c0, -1, -2).reshape(batch, nst, MLSTM_HEAD_DIM, MLSTM_HEAD_DIM)
        n0 = n0.reshape(batch, nst, MLSTM_HEAD_DIM)
        m0 = jnp.broadcast_to(m0.reshape(batch, nst, 1), (batch, nst, LANE))

    fuse_width = seq_len if (rows == 1 and IN_TILE % seq_len == 0 and MERGE_TILE % seq_len == 0) else None
    proj = _in_proj(x2, mod, mod_row, p["norm_g"][0:1], p["w_in"], p["mlstm_conv"], fuse_width)
    z = proj[0]

    rp, y0, gm, hm, gate, bonus = _rwkv_local(z, seq_len, p, RWKV_LOCAL_PASSES)
    ysf, ysb, s_fin = _rwkv_scan(s0, rp, gm, hm, batch, seq_len)

    if fuse_width is None:
        z, gt = proj
        qk, kt = _qk_conv(z, batch, seq_len, rows, p["mlstm_conv"])
        qk_blk = 0
    else:
        z, gt, kt = proj
        qk, qk_blk = z, ZM_OFF // (2 * MLSTM_WIDTH)
    hf, hb, c_fin, n_fin, m_fin = _mlstm_scan(z, qk, qk_blk, kt, gt, p["gate_bc"], p["gate_br"],
                                              c0, n0, m0, batch, seq_len)

    if fuse_width is not None:
        ffn_conv = ("seq", fuse_width)
    else:
        width = seq_len // rows
        assert rows > 1 and width & (width - 1) == 0 and MERGE_TILE % width == 0 and seq_len % MERGE_TILE == 0
        ffn_conv = ("grid", width, seq_len // MERGE_TILE)
    out, = _merge(x2, mod, mod_row, z, ysf, ysb, y0, bonus, gate, hf, hb, p, ffn_conv)

    new_states = (s_fin,
                  c_fin.reshape(batch, N_DIR, MLSTM_HEADS, MLSTM_HEAD_DIM, MLSTM_HEAD_DIM),
                  n_fin.reshape(batch, N_DIR, MLSTM_HEADS, MLSTM_HEAD_DIM),
                  m_fin[:, :, 0].reshape(batch, N_DIR, MLSTM_HEADS))
    return out.reshape(batch, seq_len, D_MODEL), new_states


def _pack_layer(l, ada_w, ada_b, norm_g, w_in, rwkv_mu, rwkv_w0, rwkv_w_up, rwkv_a0, rwkv_a_up,
                rwkv_g_up, rwkv_kk_scale, rwkv_k_a, rwkv_r_k, rwkv_lnx_g, rwkv_lnx_b, mlstm_conv,
                mlstm_gate_b, mlstm_gn_g, w_branch_rwkv, w_branch_mlstm, w_out, ffn_up, ffn_conv,
                ffn_conv_b, ffn_down):
    W = RWKV_WIDTH
    w_in_b = w_in[l].astype(BF16)

    head = jnp.arange(W, dtype=jnp.int32) // RWKV_HEAD_DIM
    same = (head[:, None] == head[None, :])
    gb = mlstm_gate_b[l].reshape(1, MLSTM_GATES)
    return dict(
        ada_w=ada_w[l], ada_b=ada_b[l], norm_g=norm_g[l], w_in=w_in_b,
        mu=rwkv_mu[l].reshape(1, RWKV_COLS),
        w0=rwkv_w0[l].reshape(N_DIR, 1, W), w_up=rwkv_w_up[l],
        a0=rwkv_a0[l].reshape(N_DIR, 1, W), a_up=rwkv_a_up[l], g_up=rwkv_g_up[l],
        kk_scale=rwkv_kk_scale[l].reshape(1, W), k_a=rwkv_k_a[l].reshape(1, W),
        r_k=rwkv_r_k[l].reshape(1, W),
        lnx_g=rwkv_lnx_g[l].reshape(1, W), lnx_b=rwkv_lnx_b[l].reshape(1, W),
        pones=same.astype(BF16), pmean=(same.astype(F32) / RWKV_HEAD_DIM).astype(BF16),
        mlstm_conv=mlstm_conv[l].reshape(9, 2 * MLSTM_WIDTH),
        gate_bc=jnp.pad(gb, ((0, 0), (0, LANE - MLSTM_GATES))), gate_br=gb.reshape(MLSTM_GATES, 1),
        gn_g=mlstm_gn_g[l].reshape(1, MLSTM_WIDTH),
        w_br=w_branch_rwkv[l].astype(BF16), w_bm=w_branch_mlstm[l].astype(BF16),
        w_out=w_out[l].astype(BF16), ffn_up=ffn_up[l].astype(BF16),
        ffn_conv=ffn_conv[l].reshape(9, D_FF), ffn_conv_b=ffn_conv_b[l].reshape(1, D_FF),
        ffn_down=ffn_down[l].astype(BF16),
    )


def kernel(x_prompt, x_sample, c, state_rwkv, state_mlstm_C, state_mlstm_n, state_mlstm_m, c_ctx,
           ada_w, ada_b, norm_g, w_in, rwkv_mu, rwkv_w0, rwkv_w_up, rwkv_a0, rwkv_a_up, rwkv_g_up,
           rwkv_kk_scale, rwkv_k_a, rwkv_r_k, rwkv_lnx_g, rwkv_lnx_b, mlstm_conv, mlstm_gate_b,
           mlstm_gn_g, w_branch_rwkv, w_branch_mlstm, w_out, ffn_up, ffn_conv, ffn_conv_b, ffn_down):
    depth = ada_w.shape[0]
    batch = x_prompt.shape[0]
    dec_batch, dec_seq, _ = x_sample.shape
    latent_rows = dec_seq // GRID_W
    cond = jnp.concatenate([c_ctx[None, :], c, jnp.zeros((8 - 1 - dec_batch, D_MODEL), F32)], axis=0)

    xp, xs = x_prompt, x_sample
    new_s, new_c, new_n, new_m = [], [], [], []
    for l in range(depth):
        p = _pack_layer(l, ada_w, ada_b, norm_g, w_in, rwkv_mu, rwkv_w0, rwkv_w_up, rwkv_a0, rwkv_a_up,
                        rwkv_g_up, rwkv_kk_scale, rwkv_k_a, rwkv_r_k, rwkv_lnx_g, rwkv_lnx_b, mlstm_conv,
                        mlstm_gate_b, mlstm_gn_g, w_branch_rwkv, w_branch_mlstm, w_out, ffn_up, ffn_conv,
                        ffn_conv_b, ffn_down)
        mod = _ada(cond, p["ada_w"], p["ada_b"]).reshape(8, 1, 6 * D_MODEL)
        xp, (s, cc, nn, mm) = _trunk(xp, mod, lambda r: 0, 1, None, p)
        new_s.append(s)
        new_c.append(cc)
        new_n.append(nn)
        new_m.append(mm)
        xs, _ = _trunk(xs, mod, lambda r: 1 + r // dec_seq, latent_rows,
                       (state_rwkv[:, l], state_mlstm_C[:, l], state_mlstm_n[:, l], state_mlstm_m[:, l]), p)
    return (xp, xs, jnp.stack(new_s, axis=1), jnp.stack(new_c, axis=1),
            jnp.stack(new_n, axis=1), jnp.stack(new_m, axis=1))
```

```python
import functools

import jax
import jax.numpy as jnp
from jax import lax
from jax.experimental import pallas as pl
from jax.experimental.pallas import tpu as pltpu

F32 = jnp.float32
BF16 = jnp.bfloat16

D_MODEL = 1024
N_DIR = 2
RWKV_HEADS = 8
RWKV_HEAD_DIM = 64
RWKV_WIDTH = RWKV_HEADS * RWKV_HEAD_DIM
DECAY_LORA = 64
ICLR_LORA = 64
GATE_LORA = 128
MLSTM_HEADS = 4
MLSTM_HEAD_DIM = 128
MLSTM_WIDTH = MLSTM_HEADS * MLSTM_HEAD_DIM
MLSTM_CHUNK = 64
D_FF = 2816
GRID_W = 64
RMS_EPS = 1e-6
RWKV_GN_EPS = 64e-5
MLSTM_GN_EPS = 1e-5
DECAY_SCALE = 0.606531

RWKV_COLS = 3 * RWKV_WIDTH + N_DIR * DECAY_LORA + N_DIR * ICLR_LORA + GATE_LORA
MLSTM_GATES = 2 * N_DIR * MLSTM_HEADS
GATE_COLS = 2 * D_MODEL

LANE = 128
ZR_BLOCK = 2048
ZG_OFF = RWKV_COLS
ZM_OFF = ZR_BLOCK
ZS_OFF = ZM_OFF + 4 * MLSTM_WIDTH
Z_COLS = ZS_OFF + GATE_COLS

IN_TILE = 512
MERGE_TILE = 256
RCHUNK = 64
CONV_CH_TILE = 256
VMEM_LIMIT = 56 * 1024 * 1024


def _params(sem):
    return pltpu.CompilerParams(dimension_semantics=sem, vmem_limit_bytes=VMEM_LIMIT)


def _resident(shape):
    nd = len(shape)
    return pl.BlockSpec(shape, lambda *_: (0,) * nd, pipeline_mode=pl.Buffered(1))


def _split2(a):
    hi = a.astype(BF16)
    lo = (a - hi.astype(F32)).astype(BF16)
    return hi, lo


def _split3(a):
    hi = a.astype(BF16)
    r1 = a - hi.astype(F32)
    mid = r1.astype(BF16)
    lo = (r1 - mid.astype(F32)).astype(BF16)
    return hi, mid, lo


def _dg(a, b, dims):
    return lax.dot_general(a, b, dims, preferred_element_type=F32)


def _mm(a, b, dims, passes):
    if passes == 1:
        return _dg(a.astype(BF16), b.astype(BF16), dims)
    ah, al = _split2(a)
    bh, bl = _split2(b)
    return _dg(ah, bh, dims) + (_dg(ah, bl, dims) + _dg(al, bh, dims))


def _mm_exact_lhs(a_bf16, b, dims):
    b1, b2, b3 = _split3(b)
    return _dg(a_bf16, b1, dims) + (_dg(a_bf16, b2, dims) + _dg(a_bf16, b3, dims))


def _mm_exact_rhs(a, b_bf16, dims, pieces=3):
    if pieces == 2:
        a1, a2 = _split2(a)
        return _dg(a1, b_bf16, dims) + _dg(a2, b_bf16, dims)
    a1, a2, a3 = _split3(a)
    return _dg(a1, b_bf16, dims) + (_dg(a2, b_bf16, dims) + _dg(a3, b_bf16, dims))


_NN = (((1,), (0,)), ((), ()))
_NT = (((1,), (1,)), ((), ()))
_TN = (((0,), (0,)), ((), ()))


def _sigmoid(x):
    return jax.nn.sigmoid(x)


def _silu(x):
    return x * jax.nn.sigmoid(x)


def _rms(x, g):
    return x * lax.rsqrt(jnp.mean(x * x, axis=-1, keepdims=True) + RMS_EPS) * g


def _ada_kernel(cond_ref, w_ref, b_ref, o_ref):
    s = _silu(cond_ref[...])
    o_ref[...] = _dg(s.astype(BF16), w_ref[...].astype(BF16), _NN) + b_ref[...]


def _ada(cond8, ada_w, ada_b):
    n = ada_w.shape[1]
    tn = 1536
    return pl.pallas_call(
        _ada_kernel,
        grid=(n // tn,),
        in_specs=[_resident((8, D_MODEL)),
                  pl.BlockSpec((D_MODEL, tn), lambda j: (0, j)),
                  pl.BlockSpec((1, tn), lambda j: (0, j))],
        out_specs=pl.BlockSpec((8, tn), lambda j: (0, j)),
        out_shape=jax.ShapeDtypeStruct((8, n), F32),
        compiler_params=_params(("arbitrary",)),
        name="ada_mod",
    )(cond8, ada_w, ada_b.reshape(1, n))


def _in_kernel(conv_width, x_ref, mod_ref, g_ref, w_ref, cw_ref, z_ref, gt_ref, *kt_ref):
    mod = mod_ref[0]
    sh = mod[:, 0:D_MODEL]
    sc = mod[:, D_MODEL:2 * D_MODEL]
    h = (_rms(x_ref[...], g_ref[...]) * (1.0 + sc) + sh).astype(BF16)
    m_off = RWKV_COLS
    t_off = RWKV_COLS + 4 * MLSTM_WIDTH
    z_ref[:, 0:ZG_OFF] = _dg(h, w_ref[:, 0:m_off], _NN)
    tail = _dg(h, w_ref[:, t_off:w_ref.shape[1]], _NN)
    zg = tail[:, 0:LANE]
    z_ref[:, ZG_OFF:ZM_OFF] = zg
    zgt = zg.T
    for c in range(zg.shape[0] // MLSTM_CHUNK):
        gt_ref[c] = zgt[0:MLSTM_GATES, c * MLSTM_CHUNK:(c + 1) * MLSTM_CHUNK]
    z_ref[:, ZS_OFF:Z_COLS] = _sigmoid(tail[:, MLSTM_GATES:MLSTM_GATES + GATE_COLS])
    zm = _dg(h, w_ref[:, m_off:t_off], _NN)
    if conv_width is None:
        z_ref[:, ZM_OFF:ZS_OFF] = zm
    else:
        qk_cols = 2 * MLSTM_WIDTH
        qk = _silu(_dwconv(zm[:, 0:qk_cols], cw_ref, conv_width, False))
        z_ref[:, ZM_OFF:ZM_OFF + qk_cols] = qk
        z_ref[:, ZM_OFF + qk_cols:ZS_OFF] = zm[:, qk_cols:]
        _store_time_on_lanes(qk[:, MLSTM_WIDTH:qk_cols], kt_ref[0], MLSTM_CHUNK)


def _in_proj(x2, mod, mod_row, norm_g0, w_in, conv_w9, conv_width):
    n = x2.shape[0]
    tile = IN_TILE
    cpt = tile // MLSTM_CHUNK
    nchunk = n // MLSTM_CHUNK
    out_specs = [pl.BlockSpec((tile, Z_COLS), lambda i: (i, 0)),
                 pl.BlockSpec((cpt, MLSTM_GATES, MLSTM_CHUNK), lambda i: (i, 0, 0))]
    out_shape = [jax.ShapeDtypeStruct((n, Z_COLS), F32),
                 jax.ShapeDtypeStruct((nchunk, MLSTM_GATES, MLSTM_CHUNK), F32)]
    if conv_width is not None:
        out_specs.append(pl.BlockSpec((cpt, MLSTM_WIDTH, MLSTM_CHUNK), lambda i: (i, 0, 0)))
        out_shape.append(jax.ShapeDtypeStruct((nchunk, MLSTM_WIDTH, MLSTM_CHUNK), F32))
    return pl.pallas_call(
        functools.partial(_in_kernel, conv_width),
        grid=(n // tile,),
        in_specs=[pl.BlockSpec((tile, D_MODEL), lambda i: (i, 0)),
                  pl.BlockSpec((1, 1, 6 * D_MODEL), lambda i: (mod_row(i * tile), 0, 0)),
                  _resident((1, D_MODEL)), _resident(w_in.shape), _resident(conv_w9.shape)],
        out_specs=out_specs,
        out_shape=out_shape,
        compiler_params=_params(("arbitrary",)),
        name="in_proj",
    )(x2, mod, norm_g0, w_in, conv_w9)


LOCAL_CHUNKS = 4
PAIR_LANES = 2 * RWKV_HEAD_DIM
RWKV_PAIRS = RWKV_HEADS // 2


def _bd(x):
    lane = lax.broadcasted_iota(jnp.int32, x.shape, 1)
    left = lane < RWKV_HEAD_DIM
    return jnp.concatenate([jnp.where(left, x, 0.0), jnp.where(left, 0.0, x)], axis=0)


def _rwkv_local_kernel(chunks_per_seq, passes, NS,
                       z_ref, zp_ref, zn_ref, mu_ref, w0_ref, wup_ref, a0_ref, aup_ref, gup_ref,
                       kks_ref, ka_ref, rk_ref, pones_ref,
                       rp_ref, y0_ref, gm_ref, hm_ref, gate_ref, bonus_ref):
    C = RCHUNK
    W = RWKV_WIDTH
    R = NS * C
    first = (pl.program_id(0) * NS) % chunks_per_seq
    has_prev = first != 0
    has_next = first + NS != chunks_per_seq

    z = z_ref[:, 0:RWKV_COLS]
    zp = jnp.where(has_prev, zp_ref[7:8, 0:RWKV_COLS], 0.0)
    zn = jnp.where(has_next, zn_ref[0:1, 0:RWKV_COLS], 0.0)
    trow = lax.broadcasted_iota(jnp.int32, (R, 1), 0)
    prev = jnp.where(trow == 0, zp, pltpu.roll(z, 1, 0))
    nxt = jnp.where(trow == R - 1, zn, pltpu.roll(z, R - 1, 0))
    zs = z + mu_ref[...] * (0.5 * (prev + nxt) - z)

    r = zs[:, 0:W]
    k = zs[:, W:2 * W]
    v = zs[:, 2 * W:3 * W]
    gd = zs[:, 3 * W + 2 * DECAY_LORA + 2 * ICLR_LORA:RWKV_COLS]
    gate_ref[...] = _dg(_sigmoid(gd).astype(BF16), gup_ref[...].astype(BF16), _NN)

    pones = pones_ref[...]
    kks = k * kks_ref[...]
    norm = jnp.sqrt(_mm_exact_rhs(kks * kks, pones, _NN, pieces=2))
    kk = kks / jnp.maximum(norm, 1e-12)

    P = PAIR_LANES
    row = lax.broadcasted_iota(jnp.int32, (R, R), 0)
    col = lax.broadcasted_iota(jnp.int32, (R, R), 1)
    same_chunk = jnp.bitwise_and(row, -C) == jnp.bitwise_and(col, -C)
    prow = lax.broadcasted_iota(jnp.int32, (C, P), 0)
    pcol = jnp.bitwise_and(lax.broadcasted_iota(jnp.int32, (C, P), 1), RWKV_HEAD_DIM - 1)
    eye_p = jnp.where(prow == pcol, 1.0, 0.0)
    left_head = lax.broadcasted_iota(jnp.int32, (C, P), 1) < RWKV_HEAD_DIM

    def diag_blocks(m):
        return jnp.where(left_head, m[0:RWKV_HEAD_DIM], m[RWKV_HEAD_DIM:P])

    abar, rbar, kt, bt, kw, bw, wc, strict, incl = [], [], [], [], [], [], [], [], []
    kd_sum = None
    for d in range(N_DIR):
        o = 3 * W + d * DECAY_LORA
        wd = zs[:, o:o + DECAY_LORA]
        o = 3 * W + 2 * DECAY_LORA + d * ICLR_LORA
        ad = zs[:, o:o + ICLR_LORA]
        logw = -DECAY_SCALE * _sigmoid(w0_ref[d] + _dg(jnp.tanh(wd).astype(BF16), wup_ref[d].astype(BF16), _NN))
        a = _sigmoid(a0_ref[d] + _dg(ad.astype(BF16), aup_ref[d].astype(BF16), _NN))
        kd = k * (1.0 + (a - 1.0) * ka_ref[...])
        b = kk * a
        kd_sum = kd if kd_sum is None else kd_sum + kd

        earlier_or_same = same_chunk & ((row >= col) if d == 0 else (row <= col))
        cum_i = _mm_exact_lhs(jnp.where(earlier_or_same, 1.0, 0.0).astype(BF16), logw, _NN)
        cum_e = cum_i - logw
        ab_d, rb_d, kt_d, bt_d, kw_d, bw_d, wc_d = [], [], [], [], [], [], []
        for s in range(NS):
            rs = slice(s * C, (s + 1) * C)
            ci_s = cum_i[rs]
            ctot = jnp.sum(logw[rs], axis=0, keepdims=True)
            e_ni = jnp.exp(-ci_s)
            e_ti = jnp.exp(ctot - ci_s)
            ab_d.append(kk[rs] * jnp.exp(cum_e[rs]))
            rb_d.append(r[rs] * jnp.exp(ci_s))
            kt_d.append(kd[rs] * e_ni)
            bt_d.append(b[rs] * e_ni)
            kw_d.append(kd[rs] * e_ti)
            bw_d.append(b[rs] * e_ti)
            wc_d.append(jnp.exp(ctot))
        abar.append(ab_d)
        rbar.append(rb_d)
        kt.append(kt_d)
        bt.append(bt_d)
        kw.append(kw_d)
        bw.append(bw_d)
        wc.append(wc_d)
        strict.append((prow > pcol) if d == 0 else (prow < pcol))
        incl.append((prow >= pcol) if d == 0 else (prow <= pcol))
    bonus_ref[...] = _mm_exact_rhs(r * kd_sum * rk_ref[...], pones, _NN, pieces=2) * v

    mm = functools.partial(_mm, passes=passes)
    chains = [(s, d, p) for s in range(NS) for d in range(N_DIR) for p in range(RWKV_PAIRS)]
    nch = range(len(chains))

    def sel(arr, i):
        s, d, p = chains[i]
        return arr[d][s][:, p * P:(p + 1) * P]

    cat0 = lambda a_, b_: jnp.concatenate([a_, b_], axis=0)
    cat1 = lambda a_, b_: jnp.concatenate([a_, b_], axis=1)
    vsl = [v[s * C:(s + 1) * C, p * P:(p + 1) * P] for s, _, p in chains]
    lhs = [cat0(sel(abar, i), sel(rbar, i)) for i in nch]
    by = [mm(lhs[i], cat0(_bd(sel(bt, i)), _bd(sel(kt, i))), _NT) for i in nch]
    a_kk = [jnp.where(strict[chains[i][1]], by[i][0:C, 0:P], 0.0) for i in nch]
    a_rb = [jnp.where(incl[chains[i][1]], by[i][C:2 * C, 0:P], 0.0) for i in nch]
    a_kv = [jnp.where(strict[chains[i][1]], by[i][0:C, P:2 * P], 0.0) for i in nch]
    a_rk = [jnp.where(incl[chains[i][1]], by[i][C:2 * C, P:2 * P], 0.0) for i in nch]
    on_v = [mm(cat0(a_kv[i], a_rk[i]), _bd(vsl[i]), _NN) for i in nch]

    x = [-m for m in a_kk]
    tinv = [eye_p + m for m in x]
    x = [mm(m, _bd(m), _NN) for m in x]
    for _ in range(4):
        both = [mm(cat0(tinv[i], x[i]), _bd(x[i]), _NN) for i in nch]
        tinv = [tinv[i] + both[i][0:C] for i in nch]
        x = [m[C:2 * C] for m in both]
    tinv = [tinv[i] + mm(tinv[i], _bd(x[i]), _NN) for i in nch]

    solved = [mm(tinv[i], cat1(_bd(sel(abar, i)), _bd(on_v[i][0:C])), _NN) for i in nch]
    ap = [m[:, 0:P] for m in solved]
    u0 = [m[:, P:2 * P] for m in solved]
    corr = [mm(a_rb[i], cat1(_bd(ap[i]), _bd(u0[i])), _NN) for i in nch]
    on_b = [mm(cat1(ap[i], u0[i]), sel(bw, i), _TN) for i in nch]
    vk = [mm(vsl[i], sel(kw, i), _TN) for i in nch]
    for i in nch:
        s, d, p = chains[i]
        rows = slice(s * C, (s + 1) * C)
        lanes = slice(p * P, (p + 1) * P)
        rp_ref[d, rows, lanes] = (sel(rbar, i) - corr[i][:, 0:P]).astype(BF16)
        gm_ref[d, s, p] = (eye_p * sel(wc, i) - diag_blocks(on_b[i][0:P])).astype(BF16)
        hm_ref[d, s, p] = diag_blocks(vk[i] - on_b[i][P:2 * P])
    for s in range(NS):
        for p in range(RWKV_PAIRS):
            f, b_ = chains.index((s, 0, p)), chains.index((s, 1, p))
            y0_ref[s * C:(s + 1) * C, p * P:(p + 1) * P] = ((on_v[f][C:2 * C] - corr[f][:, P:2 * P])
                                                            + (on_v[b_][C:2 * C] - corr[b_][:, P:2 * P]))


def _rwkv_local(z, seq_len, p, passes):
    n = z.shape[0]
    nchunk = n // RCHUNK
    cps = seq_len // RCHUNK
    ns = min(LOCAL_CHUNKS, cps)
    assert cps % ns == 0
    W = RWKV_WIDTH
    rows = ns * RCHUNK
    hb = rows // 8
    last8 = n // 8 - 1
    mat = lambda dt: jax.ShapeDtypeStruct((N_DIR, nchunk, RWKV_PAIRS, RWKV_HEAD_DIM, PAIR_LANES), dt)
    mat_spec = pl.BlockSpec((N_DIR, ns, RWKV_PAIRS, RWKV_HEAD_DIM, PAIR_LANES), lambda c: (0, c, 0, 0, 0))
    tok = lambda dt: jax.ShapeDtypeStruct((N_DIR, n, W), dt)
    tok_spec = pl.BlockSpec((N_DIR, rows, W), lambda c: (0, c, 0))
    row_spec = pl.BlockSpec((rows, W), lambda c: (c, 0))
    return pl.pallas_call(
        functools.partial(_rwkv_local_kernel, cps, passes, ns),
        grid=(nchunk // ns,),
        in_specs=[pl.BlockSpec((rows, ZR_BLOCK), lambda c: (c, 0)),
                  pl.BlockSpec((8, ZR_BLOCK), lambda c: (jnp.maximum(c * hb - 1, 0), 0)),
                  pl.BlockSpec((8, ZR_BLOCK), lambda c: (jnp.minimum((c + 1) * hb, last8), 0)),
                  _resident((1, RWKV_COLS)),
                  _resident((N_DIR, 1, W)), _resident((N_DIR, DECAY_LORA, W)),
                  _resident((N_DIR, 1, W)), _resident((N_DIR, ICLR_LORA, W)),
                  _resident((GATE_LORA, W)),
                  _resident((1, W)), _resident((1, W)), _resident((1, W)),
                  _resident((W, W))],
        out_specs=[tok_spec, row_spec, mat_spec, mat_spec, row_spec, row_spec],
        out_shape=[tok(BF16), jax.ShapeDtypeStruct((n, W), F32), mat(BF16), mat(F32),
                   jax.ShapeDtypeStruct((n, W), F32), jax.ShapeDtypeStruct((n, W), F32)],
        compiler_params=_params(("arbitrary",)),
        name="rwkv_local",
    )(z, z, z, p["mu"], p["w0"], p["w_up"], p["a0"], p["a_up"], p["g_up"],
      p["kk_scale"], p["k_a"], p["r_k"], p["pones"])


SCAN_CHUNKS = 8


def _rwkv_scan_kernel(has_init, K, *refs):
    s0_ref = refs[0] if has_init else None
    (rpf_ref, rpb_ref, gmf_ref, gmb_ref, hmf_ref, hmb_ref,
     ysf_ref, ysb_ref, sout_ref, s_scr) = refs[1:] if has_init else refs

    @pl.when(pl.program_id(1) == 0)
    def _():
        s_scr[...] = s0_ref[0] if has_init else jnp.zeros(s_scr.shape, F32)

    C = RCHUNK
    rp_ref, gm_ref, hm_ref, ys_ref = (rpf_ref, rpb_ref), (gmf_ref, gmb_ref), (hmf_ref, hmb_ref), (ysf_ref, ysb_ref)
    chains = [(d, p) for d in range(N_DIR) for p in range(RWKV_PAIRS)]
    lanes = [slice(p * PAIR_LANES, (p + 1) * PAIR_LANES) for _, p in chains]
    nch = range(len(chains))
    s = [s_scr[d, p] for d, p in chains]
    for j in range(K):
        at = (j, K - 1 - j)
        rows = [slice(at[d] * C, (at[d] + 1) * C) for d, _ in chains]
        sb = [m.astype(BF16) for m in s]
        y = [_dg(rp_ref[chains[i][0]][0, rows[i], lanes[i]], sb[i], _NT) for i in nch]
        sg = [_dg(sb[i], _bd(gm_ref[chains[i][0]][0, at[chains[i][0]], chains[i][1]]), _NN) for i in nch]
        for i in nch:
            d, p = chains[i]
            ys_ref[d][rows[i], lanes[i]] = y[i]
        s = [sg[i] + _bd(hm_ref[chains[i][0]][0, at[chains[i][0]], chains[i][1]]) for i in nch]
    for i in nch:
        d, p = chains[i]
        s_scr[d, p] = s[i]
        n = RWKV_HEAD_DIM
        sout_ref[0, d, 2 * p] = s[i][0:n, 0:n]
        sout_ref[0, d, 2 * p + 1] = s[i][n:2 * n, n:2 * n]


def _rwkv_scan(s0, rp, gm, hm, batch, seq_len):
    K = min(SCAN_CHUNKS, seq_len // RCHUNK)
    assert seq_len % (RCHUNK * K) == 0
    spb = seq_len // (RCHUNK * K)
    n = batch * seq_len

    def fwd(b, s):
        return b * spb + s

    def bwd(b, s):
        return b * spb + spb - 1 - s

    def mat_spec(d, at):
        return pl.BlockSpec((1, K, RWKV_PAIRS, RWKV_HEAD_DIM, PAIR_LANES), lambda b, s: (d, at(b, s), 0, 0, 0))

    def tok_spec(d, at):
        return pl.BlockSpec((1, K * RCHUNK, RWKV_WIDTH), lambda b, s: (d, at(b, s), 0))

    st_spec = pl.BlockSpec((1, N_DIR, RWKV_PAIRS, PAIR_LANES, PAIR_LANES), lambda b, s: (b, 0, 0, 0, 0))
    ys = jax.ShapeDtypeStruct((n, RWKV_WIDTH), F32)
    has_init = s0 is not None
    return pl.pallas_call(
        functools.partial(_rwkv_scan_kernel, has_init, K),
        grid=(batch, spb),
        in_specs=([st_spec] if has_init else [])
        + [tok_spec(0, fwd), tok_spec(1, bwd),
           mat_spec(0, fwd), mat_spec(1, bwd), mat_spec(0, fwd), mat_spec(1, bwd)],
        out_specs=[pl.BlockSpec((K * RCHUNK, RWKV_WIDTH), lambda b, s: (fwd(b, s), 0)),
                   pl.BlockSpec((K * RCHUNK, RWKV_WIDTH), lambda b, s: (bwd(b, s), 0)),
                   pl.BlockSpec((1, N_DIR, RWKV_HEADS, RWKV_HEAD_DIM, RWKV_HEAD_DIM),
                                lambda b, s: (b, 0, 0, 0, 0))],
        out_shape=[ys, ys,
                   jax.ShapeDtypeStruct((batch, N_DIR, RWKV_HEADS, RWKV_HEAD_DIM, RWKV_HEAD_DIM), F32)],
        scratch_shapes=[pltpu.VMEM((N_DIR, RWKV_PAIRS, PAIR_LANES, PAIR_LANES), F32)],
        compiler_params=_params(("arbitrary", "arbitrary")),
        name="rwkv_scan",
    )(*([s0] if has_init else []), rp, rp, gm, gm, hm, hm)


def _dwconv(x, w_ref, width, vertical):
    T = x.shape[0]
    t = lax.broadcasted_iota(jnp.int32, (T, 1), 0)
    assert width & (width - 1) == 0
    colp = jnp.bitwise_and(t, width - 1)
    xl = jnp.where(colp == 0, 0.0, pltpu.roll(x, 1, 0))
    xr = jnp.where(colp == width - 1, 0.0, pltpu.roll(x, T - 1, 0))

    def tap_row(i):
        return w_ref[3 * i:3 * i + 1, :] * xl + w_ref[3 * i + 1:3 * i + 2, :] * x + w_ref[3 * i + 2:3 * i + 3, :] * xr

    out = tap_row(1)
    if vertical:
        out = out + jnp.where(t < width, 0.0, pltpu.roll(tap_row(0), width, 0))
        out = out + jnp.where(t >= T - width, 0.0, pltpu.roll(tap_row(2), T - width, 0))
    return out


def _store_time_on_lanes(x, out_ref, chunk):
    xt = x.T
    for c in range(x.shape[0] // chunk):
        out_ref[c] = xt[:, c * chunk:(c + 1) * chunk]


def _qk_conv_kernel(width, first_k_tile, x_ref, w_ref, o_ref, kt_ref):
    out = _silu(_dwconv(x_ref[...], w_ref, width, True))
    o_ref[...] = out

    @pl.when(pl.program_id(1) >= first_k_tile)
    def _():
        _store_time_on_lanes(out, kt_ref, MLSTM_CHUNK)


def _qk_conv(z, batch, seq_len, rows, conv_w9):
    n = batch * seq_len
    ch = 2 * MLSTM_WIDTH
    tc = CONV_CH_TILE
    off = ZM_OFF // tc
    first_k = MLSTM_WIDTH // tc
    block, width = seq_len, seq_len // rows
    cpb = block // MLSTM_CHUNK
    return pl.pallas_call(
        functools.partial(_qk_conv_kernel, width, first_k),
        grid=(n // block, ch // tc),
        in_specs=[pl.BlockSpec((block, tc), lambda b, j: (b, off + j)),
                  pl.BlockSpec((9, tc), lambda b, j: (0, j))],
        out_specs=[pl.BlockSpec((block, tc), lambda b, j: (b, j)),
                   pl.BlockSpec((cpb, tc, MLSTM_CHUNK), lambda b, j: (b, jnp.maximum(j - first_k, 0), 0))],
        out_shape=[jax.ShapeDtypeStruct((n, ch), F32),
                   jax.ShapeDtypeStruct((n // MLSTM_CHUNK, MLSTM_WIDTH, MLSTM_CHUNK), F32)],
        compiler_params=_params(("arbitrary", "arbitrary")),
        name="mlstm_qk_conv",
    )(z, conv_w9)


MLSTM_STEP_CHUNKS = 4

def _mlstm_scan_kernel(has_init, qkf_ref, qkb_ref, ktf_ref, ktb_ref, vf_ref, vb_ref, gcf_ref, gcb_ref,
                       grf_ref, grb_ref, gbc_ref, gbr_ref, *refs):
    init_refs = refs[0:3] if has_init else None
    hf_ref, hb_ref, cout_ref, nout_ref, mout_ref, c_scr, n_scr, m_scr = refs[3:] if has_init else refs
    step = pl.program_id(1)
    L = MLSTM_CHUNK
    dh = MLSTM_HEAD_DIM
    H = MLSTM_HEADS

    @pl.when(step == 0)
    def _():
        for scr, k in zip((c_scr, n_scr, m_scr), range(3)):
            scr[...] = init_refs[k][0] if has_init else jnp.zeros(scr.shape, F32)

    K = MLSTM_STEP_CHUNKS
    R = K * L
    row = lax.broadcasted_iota(jnp.int32, (L, L), 0)
    col = lax.broadcasted_iota(jnp.int32, (L, L), 1)
    lower = (row >= col)
    upper = (row <= col)
    lower_b = jnp.where(lower, 1.0, 0.0).astype(BF16)
    upper_b = jnp.where(upper, 1.0, 0.0).astype(BF16)
    rrow = lax.broadcasted_iota(jnp.int32, (R, R), 0)
    rcol = lax.broadcasted_iota(jnp.int32, (R, R), 1)
    same_chunk = jnp.bitwise_and(rrow, -L) == jnp.bitwise_and(rcol, -L)
    neg_inf = jnp.full((), -jnp.inf, F32)

    gcol, grow, bcol, brow, btot = [], [], [], [], []
    ones_b = jnp.ones((L, LANE), BF16)
    for d in range(N_DIR):
        gc_ref, gr_ref = (gcf_ref, grf_ref) if d == 0 else (gcb_ref, grb_ref)
        gcol.append(gc_ref[...] + gbc_ref[...])
        grow.append((gr_ref[...] + gbr_ref[...][None]).reshape(K * MLSTM_GATES, L))
        before = same_chunk & ((rrow >= rcol) if d == 0 else (rrow <= rcol))
        bcol.append(_mm_exact_lhs(jnp.where(before, 1.0, 0.0).astype(BF16), jax.nn.log_sigmoid(gcol[d]), _NN))
        frow = jax.nn.log_sigmoid(grow[d])
        brow.append(_mm_exact_rhs(frow, upper_b if d == 0 else lower_b, _NN))
        btot.append(_mm_exact_rhs(frow, ones_b, _NN))

    units = [(j, d, h) for j in range(K) for d in range(N_DIR) for h in range(H)]
    nun = range(len(units))
    q, k, kt, v, vb, qb = [], [], [], [], [], []
    c_row, b_col, b_last = [], [], []
    for j, d, h in units:
        at = j if d == 0 else K - 1 - j
        rows = slice(at * L, (at + 1) * L)
        st = d * H + h
        gi, gf = st, 2 * H + st
        qk_ref, kt_ref, v_ref = (qkf_ref, ktf_ref, vf_ref) if d == 0 else (qkb_ref, ktb_ref, vb_ref)
        q.append(qk_ref[rows, h * dh:(h + 1) * dh] * (dh ** -0.5))
        k.append(qk_ref[rows, MLSTM_WIDTH + h * dh:MLSTM_WIDTH + (h + 1) * dh])
        kt.append(kt_ref[at, h * dh:(h + 1) * dh, :])
        v.append(v_ref[rows, h * dh:(h + 1) * dh])
        qb.append(q[-1].astype(BF16))
        vb.append(v[-1].astype(BF16))
        b_col.append(jnp.broadcast_to(bcol[d][rows, gf:gf + 1], (L, LANE)))
        c_row.append(grow[d][at * MLSTM_GATES + gi:at * MLSTM_GATES + gi + 1, :]
                     - brow[d][at * MLSTM_GATES + gf:at * MLSTM_GATES + gf + 1, :])
        b_last.append(btot[d][at * MLSTM_GATES + gf:at * MLSTM_GATES + gf + 1, :])

    last = [L - 1 if d == 0 else 0 for _, d, _ in units]
    qk_t = [_dg(qb[i], k[i].astype(BF16), _NT) for i in nun]
    rel = [jnp.where(lower if units[i][1] == 0 else upper, c_row[i], neg_inf) for i in nun]
    mx = [jnp.broadcast_to(jnp.max(rel[i], axis=-1, keepdims=True), (L, LANE)) for i in nun]
    m_loc = [b_col[i] + mx[i] for i in nun]
    s_loc = [qk_t[i] * jnp.exp(rel[i] - mx[i][:, 0:L]) for i in nun]
    s_v = [_dg(s_loc[i].astype(BF16), vb[i], _NN) for i in nun]
    s_sum = [jnp.broadcast_to(jnp.sum(s_loc[i], axis=-1, keepdims=True), (L, LANE)) for i in nun]
    cmax = [mx[i][last[i]:last[i] + 1, :] for i in nun]
    m_w = [b_last[i] + cmax[i] for i in nun]
    wj = [jnp.exp(c_row[i] - cmax[i][:, 0:L]) for i in nun]
    kv = [_dg((kt[i] * wj[i]).astype(BF16), vb[i], _NN) for i in nun]
    w_k = [_mm(jnp.broadcast_to(wj[i], (8, L)), k[i], _NN, 3)[0:1] for i in nun]

    nst = N_DIR * H
    c_st = [c_scr[st] for st in range(nst)]
    n_st = [n_scr[st:st + 1, :] for st in range(nst)]
    m_st = [m_scr[st:st + 1, :] for st in range(nst)]
    for j in range(K):
        idx = [j * nst + st for st in range(nst)]
        q_c = [_dg(qb[i], c_st[st].astype(BF16), _NN) for st, i in enumerate(idx)]
        for st, i in enumerate(idx):
            _, d, h = units[i]
            at = j if d == 0 else K - 1 - j
            h_ref = hf_ref if d == 0 else hb_ref
            log_inter = b_col[i] + m_st[st]
            m_s = jnp.maximum(log_inter, m_loc[i])
            inter = jnp.exp(log_inter - m_s)
            local = jnp.exp(m_loc[i] - m_s)
            q_n = jnp.broadcast_to(jnp.sum(q[i] * n_st[st], axis=-1, keepdims=True), (L, LANE))
            den = inter * q_n + local * s_sum[i]
            scale = 1.0 / jnp.maximum(jnp.abs(den), jnp.exp(-m_s))
            h_ref[at * L:(at + 1) * L, h * dh:(h + 1) * dh] = (inter * scale) * q_c[st] + (local * scale) * s_v[i]
            m_new = jnp.maximum(b_last[i] + m_st[st], m_w[i])
            carry = jnp.exp(b_last[i] + m_st[st] - m_new)
            fresh = jnp.exp(m_w[i] - m_new)
            c_st[st] = carry * c_st[st] + fresh * kv[i]
            n_st[st] = carry * n_st[st] + fresh * w_k[i]
            m_st[st] = m_new


    for st in range(nst):
        c_scr[st] = c_st[st]
        n_scr[st:st + 1, :] = n_st[st]
        m_scr[st:st + 1, :] = m_st[st]
    nout_ref[0] = n_scr[...]
    mout_ref[0] = m_scr[...]

    @pl.when(step == pl.num_programs(1) - 1)
    def _():
        for st in range(nst):
            cout_ref[0, st] = c_st[st].T


def _mlstm_scan(z, qk, qk_blk, kt, gt, gate_bc, gate_br, c0, n0, m0, batch, seq_len):
    K = MLSTM_STEP_CHUNKS
    L = K * MLSTM_CHUNK
    assert seq_len % L == 0
    cps = seq_len // L
    n = batch * seq_len
    W = MLSTM_WIDTH
    nst = N_DIR * MLSTM_HEADS
    dh = MLSTM_HEAD_DIM

    def fw(b, c):
        return b * cps + c

    def bw(b, c):
        return b * cps + cps - 1 - c

    vblk = (ZM_OFF + 2 * W) // W
    gblk = ZG_OFF // LANE
    has_init = c0 is not None
    state_specs = [pl.BlockSpec((1, nst, dh, dh), lambda b, c: (b, 0, 0, 0)),
                   pl.BlockSpec((1, nst, dh), lambda b, c: (b, 0, 0)),
                   pl.BlockSpec((1, nst, LANE), lambda b, c: (b, 0, 0))]
    return pl.pallas_call(
        functools.partial(_mlstm_scan_kernel, has_init),
        grid=(batch, cps),
        in_specs=[pl.BlockSpec((L, 2 * W), lambda b, c: (fw(b, c), qk_blk)),
                  pl.BlockSpec((L, 2 * W), lambda b, c: (bw(b, c), qk_blk)),
                  pl.BlockSpec((K, W, MLSTM_CHUNK), lambda b, c: (fw(b, c), 0, 0)),
                  pl.BlockSpec((K, W, MLSTM_CHUNK), lambda b, c: (bw(b, c), 0, 0)),
                  pl.BlockSpec((L, W), lambda b, c: (fw(b, c), vblk)),
                  pl.BlockSpec((L, W), lambda b, c: (bw(b, c), vblk)),
                  pl.BlockSpec((L, LANE), lambda b, c: (fw(b, c), gblk)),
                  pl.BlockSpec((L, LANE), lambda b, c: (bw(b, c), gblk)),
                  pl.BlockSpec((K, MLSTM_GATES, MLSTM_CHUNK), lambda b, c: (fw(b, c), 0, 0)),
                  pl.BlockSpec((K, MLSTM_GATES, MLSTM_CHUNK), lambda b, c: (bw(b, c), 0, 0)),
                  _resident((1, LANE)),
                  _resident((MLSTM_GATES, 1))] + (state_specs if has_init else []),
        out_specs=[pl.BlockSpec((L, W), lambda b, c: (fw(b, c), 0)),
                   pl.BlockSpec((L, W), lambda b, c: (bw(b, c), 0))] + state_specs,
        out_shape=[jax.ShapeDtypeStruct((n, W), F32), jax.ShapeDtypeStruct((n, W), F32),
                   jax.ShapeDtypeStruct((batch, nst, dh, dh), F32),
                   jax.ShapeDtypeStruct((batch, nst, dh), F32),
                   jax.ShapeDtypeStruct((batch, nst, LANE), F32)],
        scratch_shapes=[pltpu.VMEM((nst, dh, dh), F32), pltpu.VMEM((nst, dh), F32),
                        pltpu.VMEM((nst, LANE), F32)],
        compiler_params=_params(("arbitrary", "arbitrary")),
        name="mlstm_scan",
    )(qk, qk, kt, kt, z, z, z, z, gt, gt, gate_bc, gate_br, *([c0, n0, m0] if has_init else []))


def _grid_conv(above, cur, below, cw_ref, width):
    T = cur.shape[0]
    E = T + 2 * width
    ext = jnp.concatenate([above, cur, below], axis=0)
    colp = jnp.bitwise_and(lax.broadcasted_iota(jnp.int32, (E, 1), 0), width - 1)
    left = jnp.where(colp == 0, 0.0, pltpu.roll(ext, 1, 0))
    right = jnp.where(colp == width - 1, 0.0, pltpu.roll(ext, E - 1, 0))

    def tap_row(i):
        rows = slice(i * width, i * width + T)
        return (cw_ref[3 * i:3 * i + 1, :] * left[rows] + cw_ref[3 * i + 1:3 * i + 2, :] * ext[rows]
                + cw_ref[3 * i + 2:3 * i + 3, :] * right[rows])

    return tap_row(0) + tap_row(1) + tap_row(2)


def _merge_kernel(conv, x_ref, mod_ref, ysf_ref, ysb_ref, y0_ref, bonus_ref, gate_ref, hf_ref, hb_ref, zo_ref,
                  zs_ref, lnxg_ref, lnxb_ref, gng_ref, pmean_ref, wbr_ref, wbm_ref, wout_ref, ng_ref, wup_ref,
                  cw_ref, cb_ref, *rest):
    if conv[0] == "seq":
        wdn_ref, out_ref = rest
    else:
        wdn_ref, modp_ref, out_ref = rest[0:3]
        scratch = rest[3:]

        @pl.when(pl.program_id(0) == 0)
        def _():
            for ref in scratch:
                ref[...] = jnp.zeros(ref.shape, ref.dtype)

    mod = mod_ref[0]
    g1 = mod[:, 2 * D_MODEL:3 * D_MODEL]
    sh2 = mod[:, 3 * D_MODEL:4 * D_MODEL]
    sc2 = mod[:, 4 * D_MODEL:5 * D_MODEL]

    ys = (ysf_ref[...] + ysb_ref[...]) + y0_ref[...]
    pmean = pmean_ref[...]
    mean = _mm_exact_rhs(ys, pmean, _NN)
    cen = ys - mean
    var = _mm_exact_rhs(cen * cen, pmean, _NN)
    y_r = (cen * lax.rsqrt(var + RWKV_GN_EPS) * lnxg_ref[...] + lnxb_ref[...] + bonus_ref[...]) * gate_ref[...]

    hs = hf_ref[...] + hb_ref[...]
    parts = []
    for h in range(MLSTM_HEADS):
        hh = hs[:, h * MLSTM_HEAD_DIM:(h + 1) * MLSTM_HEAD_DIM]
        mu = jnp.mean(hh, axis=-1, keepdims=True)
        ce = hh - mu
        va = jnp.mean(ce * ce, axis=-1, keepdims=True)
        parts.append(ce * lax.rsqrt(va + MLSTM_GN_EPS))
    y_m = jnp.concatenate(parts, axis=1) * gng_ref[...] * _sigmoid(zo_ref[...])

    gates = zs_ref[...]
    merged = (gates[:, 0:D_MODEL] * _dg(y_r.astype(BF16), wbr_ref[...], _NN)
              + gates[:, D_MODEL:2 * D_MODEL] * _dg(y_m.astype(BF16), wbm_ref[...], _NN))
    t = _dg(merged.astype(BF16), wout_ref[...], _NN)
    x1 = x_ref[...] + g1 * _rms(t, ng_ref[1:2, :])
    h2 = _rms(x1, ng_ref[2:3, :]) * (1.0 + sc2) + sh2
    u = _dg(h2.astype(BF16), wup_ref[...], _NN)
    if conv[0] == "seq":
        pre = _dwconv(u[:, 0:D_FF], cw_ref, conv[1], False) + cb_ref[...]
        act = (_silu(pre) * u[:, D_FF:2 * D_FF]).astype(BF16)
        g2 = mod[:, 5 * D_MODEL:6 * D_MODEL]
        out_ref[...] = x1 + g2 * _rms(_dg(act, wdn_ref[...], _NN), ng_ref[3:4, :])
    else:
        _, width, tiles_per_image = conv
        act_scr, val_scr, tail_scr, x1_scr, act_ref = scratch
        T = act_scr.shape[0]
        step = pl.program_id(0)
        pos = (step + tiles_per_image - 1) % tiles_per_image
        for c0 in range(0, D_FF, CONV_CH_TILE):
            ch = slice(c0, c0 + CONV_CH_TILE)
            above = jnp.where(pos != 0, tail_scr[:, ch], 0.0)
            below = jnp.where(pos != tiles_per_image - 1, u[0:width, ch], 0.0)
            pre = _grid_conv(above, act_scr[:, ch], below, cw_ref.at[:, ch], width) + cb_ref[:, ch]
            act_ref[:, ch] = (_silu(pre) * val_scr[:, ch]).astype(BF16)
        g2 = modp_ref[0][:, 5 * D_MODEL:6 * D_MODEL]
        out_ref[...] = x1_scr[...] + g2 * _rms(_dg(act_ref[...], wdn_ref[...], _NN), ng_ref[3:4, :])
        tail_scr[...] = act_scr[T - width:T, :]
        act_scr[...] = u[:, 0:D_FF]
        val_scr[...] = u[:, D_FF:2 * D_FF]
        x1_scr[...] = x1


def _merge(x2, mod, mod_row, z, ysf, ysb, y0, bonus, gate, hf, hb, p, conv):
    n = x2.shape[0]
    W = RWKV_WIDTH
    rows = MERGE_TILE
    ntiles = n // rows
    delayed = conv[0] == "grid"
    cur = (lambda i: jnp.minimum(i, ntiles - 1)) if delayed else (lambda i: i)
    tile = lambda w: pl.BlockSpec((rows, w), lambda i: (cur(i), 0))
    prev = lambda i: jnp.maximum(i - 1, 0)
    extra_in, extra_args = [_resident((D_FF, D_MODEL))], [p["ffn_down"]]
    if delayed:
        scratch = [pltpu.VMEM((rows, D_FF), F32), pltpu.VMEM((rows, D_FF), F32), pltpu.VMEM((conv[1], D_FF), F32),
                   pltpu.VMEM((rows, D_MODEL), F32), pltpu.VMEM((rows, D_FF), BF16)]
        extra_in.append(pl.BlockSpec((1, 1, 6 * D_MODEL), lambda i: (mod_row(prev(i) * rows), 0, 0)))
        extra_args.append(mod)
        out_spec = pl.BlockSpec((rows, D_MODEL), lambda i: (prev(i), 0))
    else:
        scratch = []
        out_spec = tile(D_MODEL)
    out_specs = [out_spec]
    out_shape = [jax.ShapeDtypeStruct((n, D_MODEL), F32)]
    return pl.pallas_call(
        functools.partial(_merge_kernel, conv),
        grid=(ntiles + 1 if delayed else ntiles,),
        in_specs=[tile(D_MODEL),
                  pl.BlockSpec((1, 1, 6 * D_MODEL), lambda i: (mod_row(cur(i) * rows), 0, 0)),
                  tile(W), tile(W), tile(W), tile(W), tile(W), tile(MLSTM_WIDTH), tile(MLSTM_WIDTH),
                  pl.BlockSpec((rows, MLSTM_WIDTH),
                               lambda i: (cur(i), (ZM_OFF + 3 * MLSTM_WIDTH) // MLSTM_WIDTH)),
                  pl.BlockSpec((rows, GATE_COLS), lambda i: (cur(i), ZS_OFF // GATE_COLS)),
                  _resident((1, W)), _resident((1, W)), _resident((1, MLSTM_WIDTH)),
                  _resident((W, W)),
                  _resident((W, D_MODEL)), _resident((MLSTM_WIDTH, D_MODEL)),
                  _resident((D_MODEL, D_MODEL)), _resident((4, D_MODEL)),
                  _resident((D_MODEL, 2 * D_FF)), _resident((9, D_FF)), _resident((1, D_FF))] + extra_in,
        out_specs=out_specs,
        out_shape=out_shape,
        scratch_shapes=scratch,
        compiler_params=_params(("arbitrary",)),
        name="merge_ffn_up",
    )(x2, mod, ysf, ysb, y0, bonus, gate, hf, hb, z, z, p["lnx_g"], p["lnx_b"], p["gn_g"], p["pmean"],
      p["w_br"], p["w_bm"], p["w_out"], p["norm_g"], p["ffn_up"], p["ffn_conv"], p["ffn_conv_b"], *extra_args)


RWKV_LOCAL_PASSES = 1


def _state_to_pairs(s):
    b = s.shape[0]
    s = s.reshape(b, N_DIR, RWKV_PAIRS, 2, RWKV_HEAD_DIM, RWKV_HEAD_DIM)
    zero = jnp.zeros_like(s[:, :, :, 0])
    top = jnp.concatenate([s[:, :, :, 0], zero], axis=-1)
    bot = jnp.concatenate([zero, s[:, :, :, 1]], axis=-1)
    return jnp.concatenate([top, bot], axis=-2)


def _trunk(x, mod, mod_row, rows, states, p):
    batch, seq_len, _ = x.shape
    n = batch * seq_len
    x2 = x.reshape(n, D_MODEL)
    nst = N_DIR * MLSTM_HEADS
    if states is None:
        s0 = c0 = n0 = m0 = None
    else:
        s0, c0, n0, m0 = states
        s0 = _state_to_pairs(s0)
        c0 = jnp.swapaxes(c0, -1, -2).reshape(batch, nst, MLSTM_HEAD_DIM, MLSTM_HEAD_DIM)
        n0 = n0.reshape(batch, nst, MLSTM_HEAD_DIM)
        m0 = jnp.broadcast_to(m0.reshape(batch, nst, 1), (batch, nst, LANE))

    fuse_width = seq_len if (rows == 1 and IN_TILE % seq_len == 0 and MERGE_TILE % seq_len == 0) else None
    proj = _in_proj(x2, mod, mod_row, p["norm_g"][0:1], p["w_in"], p["mlstm_conv"], fuse_width)
    z = proj[0]

    rp, y0, gm, hm, gate, bonus = _rwkv_local(z, seq_len, p, RWKV_LOCAL_PASSES)
    ysf, ysb, s_fin = _rwkv_scan(s0, rp, gm, hm, batch, seq_len)

    if fuse_width is None:
        z, gt = proj
        qk, kt = _qk_conv(z, batch, seq_len, rows, p["mlstm_conv"])
        qk_blk = 0
    else:
        z, gt, kt = proj
        qk, qk_blk = z, ZM_OFF // (2 * MLSTM_WIDTH)
    hf, hb, c_fin, n_fin, m_fin = _mlstm_scan(z, qk, qk_blk, kt, gt, p["gate_bc"], p["gate_br"],
                                              c0, n0, m0, batch, seq_len)

    if fuse_width is not None:
        ffn_conv = ("seq", fuse_width)
    else:
        width = seq_len // rows
        assert rows > 1 and width & (width - 1) == 0 and MERGE_TILE % width == 0 and seq_len % MERGE_TILE == 0
        ffn_conv = ("grid", width, seq_len // MERGE_TILE)
    out, = _merge(x2, mod, mod_row, z, ysf, ysb, y0, bonus, gate, hf, hb, p, ffn_conv)

    new_states = (s_fin,
                  c_fin.reshape(batch, N_DIR, MLSTM_HEADS, MLSTM_HEAD_DIM, MLSTM_HEAD_DIM),
                  n_fin.reshape(batch, N_DIR, MLSTM_HEADS, MLSTM_HEAD_DIM),
                  m_fin[:, :, 0].reshape(batch, N_DIR, MLSTM_HEADS))
    return out.reshape(batch, seq_len, D_MODEL), new_states


def _pack_layer(l, ada_w, ada_b, norm_g, w_in, rwkv_mu, rwkv_w0, rwkv_w_up, rwkv_a0, rwkv_a_up,
                rwkv_g_up, rwkv_kk_scale, rwkv_k_a, rwkv_r_k, rwkv_lnx_g, rwkv_lnx_b, mlstm_conv,
                mlstm_gate_b, mlstm_gn_g, w_branch_rwkv, w_branch_mlstm, w_out, ffn_up, ffn_conv,
                ffn_conv_b, ffn_down):
    W = RWKV_WIDTH
    w_in_b = w_in[l].astype(BF16)

    head = jnp.arange(W, dtype=jnp.int32) // RWKV_HEAD_DIM
    same = (head[:, None] == head[None, :])
    gb = mlstm_gate_b[l].reshape(1, MLSTM_GATES)
    return dict(
        ada_w=ada_w[l], ada_b=ada_b[l], norm_g=norm_g[l], w_in=w_in_b,
        mu=rwkv_mu[l].reshape(1, RWKV_COLS),
        w0=rwkv_w0[l].reshape(N_DIR, 1, W), w_up=rwkv_w_up[l],
        a0=rwkv_a0[l].reshape(N_DIR, 1, W), a_up=rwkv_a_up[l], g_up=rwkv_g_up[l],
        kk_scale=rwkv_kk_scale[l].reshape(1, W), k_a=rwkv_k_a[l].reshape(1, W),
        r_k=rwkv_r_k[l].reshape(1, W),
        lnx_g=rwkv_lnx_g[l].reshape(1, W), lnx_b=rwkv_lnx_b[l].reshape(1, W),
        pones=same.astype(BF16), pmean=(same.astype(F32) / RWKV_HEAD_DIM).astype(BF16),
        mlstm_conv=mlstm_conv[l].reshape(9, 2 * MLSTM_WIDTH),
        gate_bc=jnp.pad(gb, ((0, 0), (0, LANE - MLSTM_GATES))), gate_br=gb.reshape(MLSTM_GATES, 1),
        gn_g=mlstm_gn_g[l].reshape(1, MLSTM_WIDTH),
        w_br=w_branch_rwkv[l].astype(BF16), w_bm=w_branch_mlstm[l].astype(BF16),
        w_out=w_out[l].astype(BF16), ffn_up=ffn_up[l].astype(BF16),
        ffn_conv=ffn_conv[l].reshape(9, D_FF), ffn_conv_b=ffn_conv_b[l].reshape(1, D_FF),
        ffn_down=ffn_down[l].astype(BF16),
    )


def kernel(x_prompt, x_sample, c, state_rwkv, state_mlstm_C, state_mlstm_n, state_mlstm_m, c_ctx,
           ada_w, ada_b, norm_g, w_in, rwkv_mu, rwkv_w0, rwkv_w_up, rwkv_a0, rwkv_a_up, rwkv_g_up,
           rwkv_kk_scale, rwkv_k_a, rwkv_r_k, rwkv_lnx_g, rwkv_lnx_b, mlstm_conv, mlstm_gate_b,
           mlstm_gn_g, w_branch_rwkv, w_branch_mlstm, w_out, ffn_up, ffn_conv, ffn_conv_b, ffn_down):
    depth = ada_w.shape[0]
    batch = x_prompt.shape[0]
    dec_batch, dec_seq, _ = x_sample.shape
    latent_rows = dec_seq // GRID_W
    cond = jnp.concatenate([c_ctx[None, :], c, jnp.zeros((8 - 1 - dec_batch, D_MODEL), F32)], axis=0)

    xp, xs = x_prompt, x_sample
    new_s, new_c, new_n, new_m = [], [], [], []
    for l in range(depth):
        p = _pack_layer(l, ada_w, ada_b, norm_g, w_in, rwkv_mu, rwkv_w0, rwkv_w_up, rwkv_a0, rwkv_a_up,
                        rwkv_g_up, rwkv_kk_scale, rwkv_k_a, rwkv_r_k, rwkv_lnx_g, rwkv_lnx_b, mlstm_conv,
                        mlstm_gate_b, mlstm_gn_g, w_branch_rwkv, w_branch_mlstm, w_out, ffn_up, ffn_conv,
                        ffn_conv_b, ffn_down)
        mod = _ada(cond, p["ada_w"], p["ada_b"]).reshape(8, 1, 6 * D_MODEL)
        xp, (s, cc, nn, mm) = _trunk(xp, mod, lambda r: 0, 1, None, p)
        new_s.append(s)
        new_c.append(cc)
        new_n.append(nn)
        new_m.append(mm)
        xs, _ = _trunk(xs, mod, lambda r: 1 + r // dec_seq, latent_rows,
                       (state_rwkv[:, l], state_mlstm_C[:, l], state_mlstm_n[:, l], state_mlstm_m[:, l]), p)
    return (xp, xs, jnp.stack(new_s, axis=1), jnp.stack(new_c, axis=1),
            jnp.stack(new_n, axis=1), jnp.stack(new_m, axis=1))
```

```python
import functools

import jax
import jax.numpy as jnp
from jax import lax
from jax.experimental import pallas as pl
from jax.experimental.pallas import tpu as pltpu

F32 = jnp.float32
BF16 = jnp.bfloat16

D_MODEL = 1024
N_DIR = 2
RWKV_HEADS = 8
RWKV_HEAD_DIM = 64
RWKV_WIDTH = RWKV_HEADS * RWKV_HEAD_DIM
DECAY_LORA = 64
ICLR_LORA = 64
GATE_LORA = 128
MLSTM_HEADS = 4
MLSTM_HEAD_DIM = 128
MLSTM_WIDTH = MLSTM_HEADS * MLSTM_HEAD_DIM
MLSTM_CHUNK = 64
D_FF = 2816
GRID_W = 64
RMS_EPS = 1e-6
RWKV_GN_EPS = 64e-5
MLSTM_GN_EPS = 1e-5
DECAY_SCALE = 0.606531

RWKV_COLS = 3 * RWKV_WIDTH + N_DIR * DECAY_LORA + N_DIR * ICLR_LORA + GATE_LORA
MLSTM_GATES = 2 * N_DIR * MLSTM_HEADS
GATE_COLS = 2 * D_MODEL

LANE = 128
ZR_BLOCK = 2048
ZG_OFF = RWKV_COLS
ZM_OFF = ZR_BLOCK
ZS_OFF = ZM_OFF + 4 * MLSTM_WIDTH
Z_COLS = ZS_OFF + GATE_COLS

IN_TILE = 512
MERGE_TILE = 256
RCHUNK = 64
CONV_CH_TILE = 256
VMEM_LIMIT = 56 * 1024 * 1024


def _params(sem):
    return pltpu.CompilerParams(dimension_semantics=sem, vmem_limit_bytes=VMEM_LIMIT)


def _resident(shape):
    nd = len(shape)
    return pl.BlockSpec(shape, lambda *_: (0,) * nd, pipeline_mode=pl.Buffered(1))


def _split2(a):
    hi = a.astype(BF16)
    lo = (a - hi.astype(F32)).astype(BF16)
    return hi, lo


def _split3(a):
    hi = a.astype(BF16)
    r1 = a - hi.astype(F32)
    mid = r1.astype(BF16)
    lo = (r1 - mid.astype(F32)).astype(BF16)
    return hi, mid, lo


def _dg(a, b, dims):
    return lax.dot_general(a, b, dims, preferred_element_type=F32)


def _mm(a, b, dims, passes):
    if passes == 1:
        return _dg(a.astype(BF16), b.astype(BF16), dims)
    ah, al = _split2(a)
    bh, bl = _split2(b)
    return _dg(ah, bh, dims) + (_dg(ah, bl, dims) + _dg(al, bh, dims))


def _mm_exact_lhs(a_bf16, b, dims, pieces=3):
    if pieces == 2:
        b1, b2 = _split2(b)
        return _dg(a_bf16, b1, dims) + _dg(a_bf16, b2, dims)
    b1, b2, b3 = _split3(b)
    return _dg(a_bf16, b1, dims) + (_dg(a_bf16, b2, dims) + _dg(a_bf16, b3, dims))


def _mm_exact_rhs(a, b_bf16, dims, pieces=3):
    if pieces == 2:
        a1, a2 = _split2(a)
        return _dg(a1, b_bf16, dims) + _dg(a2, b_bf16, dims)
    a1, a2, a3 = _split3(a)
    return _dg(a1, b_bf16, dims) + (_dg(a2, b_bf16, dims) + _dg(a3, b_bf16, dims))


_NN = (((1,), (0,)), ((), ()))
_NT = (((1,), (1,)), ((), ()))
_TN = (((0,), (0,)), ((), ()))


def _sigmoid(x):
    return jax.nn.sigmoid(x)


def _silu(x):
    return x * jax.nn.sigmoid(x)


def _rms(x, g):
    return x * lax.rsqrt(jnp.mean(x * x, axis=-1, keepdims=True) + RMS_EPS) * g


def _ada_kernel(cond_ref, w_ref, b_ref, o_ref):
    s = _silu(cond_ref[...])
    o_ref[...] = _dg(s.astype(BF16), w_ref[...].astype(BF16), _NN) + b_ref[...]


def _ada(cond8, ada_w, ada_b):
    n = ada_w.shape[1]
    tn = 1536
    return pl.pallas_call(
        _ada_kernel,
        grid=(n // tn,),
        in_specs=[_resident((8, D_MODEL)),
                  pl.BlockSpec((D_MODEL, tn), lambda j: (0, j)),
                  pl.BlockSpec((1, tn), lambda j: (0, j))],
        out_specs=pl.BlockSpec((8, tn), lambda j: (0, j)),
        out_shape=jax.ShapeDtypeStruct((8, n), F32),
        compiler_params=_params(("arbitrary",)),
        name="ada_mod",
    )(cond8, ada_w, ada_b.reshape(1, n))


def _in_kernel(conv_width, x_ref, mod_ref, g_ref, w_ref, cw_ref, z_ref, gt_ref, *kt_ref):
    mod = mod_ref[0]
    sh = mod[:, 0:D_MODEL]
    sc = mod[:, D_MODEL:2 * D_MODEL]
    h = (_rms(x_ref[...], g_ref[...]) * (1.0 + sc) + sh).astype(BF16)
    m_off = RWKV_COLS
    t_off = RWKV_COLS + 4 * MLSTM_WIDTH
    z_ref[:, 0:ZG_OFF] = _dg(h, w_ref[:, 0:m_off], _NN)
    tail = _dg(h, w_ref[:, t_off:w_ref.shape[1]], _NN)
    zg = tail[:, 0:LANE]
    z_ref[:, ZG_OFF:ZM_OFF] = zg
    zgt = zg.T
    for c in range(zg.shape[0] // MLSTM_CHUNK):
        gt_ref[c] = zgt[0:MLSTM_GATES, c * MLSTM_CHUNK:(c + 1) * MLSTM_CHUNK]
    z_ref[:, ZS_OFF:Z_COLS] = _sigmoid(tail[:, MLSTM_GATES:MLSTM_GATES + GATE_COLS])
    zm = _dg(h, w_ref[:, m_off:t_off], _NN)
    if conv_width is None:
        z_ref[:, ZM_OFF:ZS_OFF] = zm
    else:
        qk_cols = 2 * MLSTM_WIDTH
        qk = _silu(_dwconv(zm[:, 0:qk_cols], cw_ref, conv_width, False))
        z_ref[:, ZM_OFF:ZM_OFF + qk_cols] = qk
        z_ref[:, ZM_OFF + qk_cols:ZS_OFF] = zm[:, qk_cols:]
        _store_time_on_lanes(qk[:, MLSTM_WIDTH:qk_cols], kt_ref[0], MLSTM_CHUNK)


def _in_proj(x2, mod, mod_row, norm_g0, w_in, conv_w9, conv_width):
    n = x2.shape[0]
    tile = IN_TILE
    cpt = tile // MLSTM_CHUNK
    nchunk = n // MLSTM_CHUNK
    out_specs = [pl.BlockSpec((tile, Z_COLS), lambda i: (i, 0)),
                 pl.BlockSpec((cpt, MLSTM_GATES, MLSTM_CHUNK), lambda i: (i, 0, 0))]
    out_shape = [jax.ShapeDtypeStruct((n, Z_COLS), F32),
                 jax.ShapeDtypeStruct((nchunk, MLSTM_GATES, MLSTM_CHUNK), F32)]
    if conv_width is not None:
        out_specs.append(pl.BlockSpec((cpt, MLSTM_WIDTH, MLSTM_CHUNK), lambda i: (i, 0, 0)))
        out_shape.append(jax.ShapeDtypeStruct((nchunk, MLSTM_WIDTH, MLSTM_CHUNK), F32))
    return pl.pallas_call(
        functools.partial(_in_kernel, conv_width),
        grid=(n // tile,),
        in_specs=[pl.BlockSpec((tile, D_MODEL), lambda i: (i, 0)),
                  pl.BlockSpec((1, 1, 6 * D_MODEL), lambda i: (mod_row(i * tile), 0, 0)),
                  _resident((1, D_MODEL)), _resident(w_in.shape), _resident(conv_w9.shape)],
        out_specs=out_specs,
        out_shape=out_shape,
        compiler_params=_params(("arbitrary",)),
        name="in_proj",
    )(x2, mod, norm_g0, w_in, conv_w9)


LOCAL_CHUNKS = 4
PAIR_LANES = 2 * RWKV_HEAD_DIM
RWKV_PAIRS = RWKV_HEADS // 2


def _bd(x):
    lane = lax.broadcasted_iota(jnp.int32, x.shape, 1)
    left = lane < RWKV_HEAD_DIM
    return jnp.concatenate([jnp.where(left, x, 0.0), jnp.where(left, 0.0, x)], axis=0)


def _rwkv_local_kernel(chunks_per_seq, passes, NS,
                       z_ref, zp_ref, zn_ref, mu_ref, w0_ref, wup_ref, a0_ref, aup_ref, gup_ref,
                       kks_ref, ka_ref, rk_ref, pones_ref,
                       rp_ref, y0_ref, gm_ref, hm_ref, gate_ref, bonus_ref):
    C = RCHUNK
    W = RWKV_WIDTH
    R = NS * C
    first = (pl.program_id(0) * NS) % chunks_per_seq
    has_prev = first != 0
    has_next = first + NS != chunks_per_seq

    z = z_ref[:, 0:RWKV_COLS]
    zp = jnp.where(has_prev, zp_ref[7:8, 0:RWKV_COLS], 0.0)
    zn = jnp.where(has_next, zn_ref[0:1, 0:RWKV_COLS], 0.0)
    trow = lax.broadcasted_iota(jnp.int32, (R, 1), 0)
    prev = jnp.where(trow == 0, zp, pltpu.roll(z, 1, 0))
    nxt = jnp.where(trow == R - 1, zn, pltpu.roll(z, R - 1, 0))
    zs = z + mu_ref[...] * (0.5 * (prev + nxt) - z)

    r = zs[:, 0:W]
    k = zs[:, W:2 * W]
    v = zs[:, 2 * W:3 * W]
    gd = zs[:, 3 * W + 2 * DECAY_LORA + 2 * ICLR_LORA:RWKV_COLS]
    gate_ref[...] = _dg(_sigmoid(gd).astype(BF16), gup_ref[...].astype(BF16), _NN)

    pones = pones_ref[...]
    kks = k * kks_ref[...]
    norm = jnp.sqrt(_mm_exact_rhs(kks * kks, pones, _NN, pieces=2))
    kk = kks / jnp.maximum(norm, 1e-12)

    P = PAIR_LANES
    row = lax.broadcasted_iota(jnp.int32, (R, R), 0)
    col = lax.broadcasted_iota(jnp.int32, (R, R), 1)
    same_chunk = jnp.bitwise_and(row, -C) == jnp.bitwise_and(col, -C)
    prow = lax.broadcasted_iota(jnp.int32, (C, P), 0)
    pcol = jnp.bitwise_and(lax.broadcasted_iota(jnp.int32, (C, P), 1), RWKV_HEAD_DIM - 1)
    eye_p = jnp.where(prow == pcol, 1.0, 0.0)
    left_head = lax.broadcasted_iota(jnp.int32, (C, P), 1) < RWKV_HEAD_DIM

    def diag_blocks(m):
        return jnp.where(left_head, m[0:RWKV_HEAD_DIM], m[RWKV_HEAD_DIM:P])

    abar, rbar, kt, bt, kw, bw, wc, strict, incl = [], [], [], [], [], [], [], [], []
    kd_sum = None
    for d in range(N_DIR):
        o = 3 * W + d * DECAY_LORA
        wd = zs[:, o:o + DECAY_LORA]
        o = 3 * W + 2 * DECAY_LORA + d * ICLR_LORA
        ad = zs[:, o:o + ICLR_LORA]
        logw = -DECAY_SCALE * _sigmoid(w0_ref[d] + _dg(jnp.tanh(wd).astype(BF16), wup_ref[d].astype(BF16), _NN))
        a = _sigmoid(a0_ref[d] + _dg(ad.astype(BF16), aup_ref[d].astype(BF16), _NN))
        kd = k * (1.0 + (a - 1.0) * ka_ref[...])
        b = kk * a
        kd_sum = kd if kd_sum is None else kd_sum + kd

        earlier_or_same = same_chunk & ((row >= col) if d == 0 else (row <= col))
        cum_i = _mm_exact_lhs(jnp.where(earlier_or_same, 1.0, 0.0).astype(BF16), logw, _NN, pieces=2)
        cum_e = cum_i - logw
        ab_d, rb_d, kt_d, bt_d, kw_d, bw_d, wc_d = [], [], [], [], [], [], []
        for s in range(NS):
            rs = slice(s * C, (s + 1) * C)
            ci_s = cum_i[rs]
            ctot = jnp.sum(logw[rs], axis=0, keepdims=True)
            e_ni = jnp.exp(-ci_s)
            e_ti = jnp.exp(ctot - ci_s)
            ab_d.append(kk[rs] * jnp.exp(cum_e[rs]))
            rb_d.append(r[rs] * jnp.exp(ci_s))
            kt_d.append(kd[rs] * e_ni)
            bt_d.append(b[rs] * e_ni)
            kw_d.append(kd[rs] * e_ti)
            bw_d.append(b[rs] * e_ti)
            wc_d.append(jnp.exp(ctot))
        abar.append(ab_d)
        rbar.append(rb_d)
        kt.append(kt_d)
        bt.append(bt_d)
        kw.append(kw_d)
        bw.append(bw_d)
        wc.append(wc_d)
        strict.append((prow > pcol) if d == 0 else (prow < pcol))
        incl.append((prow >= pcol) if d == 0 else (prow <= pcol))
    bonus_ref[...] = _mm_exact_rhs(r * kd_sum * rk_ref[...], pones, _NN, pieces=2) * v

    mm = functools.partial(_mm, passes=passes)
    chains = [(s, d, p) for s in range(NS) for d in range(N_DIR) for p in range(RWKV_PAIRS)]
    nch = range(len(chains))

    def sel(arr, i):
        s, d, p = chains[i]
        return arr[d][s][:, p * P:(p + 1) * P]

    cat0 = lambda a_, b_: jnp.concatenate([a_, b_], axis=0)
    cat1 = lambda a_, b_: jnp.concatenate([a_, b_], axis=1)
    vsl = [v[s * C:(s + 1) * C, p * P:(p + 1) * P] for s, _, p in chains]
    lhs = [cat0(sel(abar, i), sel(rbar, i)) for i in nch]
    by = [mm(lhs[i], cat0(_bd(sel(bt, i)), _bd(sel(kt, i))), _NT) for i in nch]
    a_kk = [jnp.where(strict[chains[i][1]], by[i][0:C, 0:P], 0.0) for i in nch]
    a_rb = [jnp.where(incl[chains[i][1]], by[i][C:2 * C, 0:P], 0.0) for i in nch]
    a_kv = [jnp.where(strict[chains[i][1]], by[i][0:C, P:2 * P], 0.0) for i in nch]
    a_rk = [jnp.where(incl[chains[i][1]], by[i][C:2 * C, P:2 * P], 0.0) for i in nch]
    on_v = [mm(cat0(a_kv[i], a_rk[i]), _bd(vsl[i]), _NN) for i in nch]

    x = [-m for m in a_kk]
    tinv = [eye_p + m for m in x]
    x = [mm(m, _bd(m), _NN) for m in x]
    for _ in range(4):
        both = [mm(cat0(tinv[i], x[i]), _bd(x[i]), _NN) for i in nch]
        tinv = [tinv[i] + both[i][0:C] for i in nch]
        x = [m[C:2 * C] for m in both]
    tinv = [tinv[i] + mm(tinv[i], _bd(x[i]), _NN) for i in nch]

    solved = [mm(tinv[i], cat1(_bd(sel(abar, i)), _bd(on_v[i][0:C])), _NN) for i in nch]
    ap = [m[:, 0:P] for m in solved]
    u0 = [m[:, P:2 * P] for m in solved]
    corr = [mm(a_rb[i], cat1(_bd(ap[i]), _bd(u0[i])), _NN) for i in nch]
    on_b = [mm(cat1(ap[i], u0[i]), sel(bw, i), _TN) for i in nch]
    vk = [mm(vsl[i], sel(kw, i), _TN) for i in nch]
    for i in nch:
        s, d, p = chains[i]
        rows = slice(s * C, (s + 1) * C)
        lanes = slice(p * P, (p + 1) * P)
        rp_ref[d, rows, lanes] = (sel(rbar, i) - corr[i][:, 0:P]).astype(BF16)
        gm_ref[d, s, p] = (eye_p * sel(wc, i) - diag_blocks(on_b[i][0:P])).astype(BF16)
        hm_ref[d, s, p] = diag_blocks(vk[i] - on_b[i][P:2 * P])
    for s in range(NS):
        for p in range(RWKV_PAIRS):
            f, b_ = chains.index((s, 0, p)), chains.index((s, 1, p))
            y0_ref[s * C:(s + 1) * C, p * P:(p + 1) * P] = ((on_v[f][C:2 * C] - corr[f][:, P:2 * P])
                                                            + (on_v[b_][C:2 * C] - corr[b_][:, P:2 * P]))


def _rwkv_local(z, seq_len, p, passes):
    n = z.shape[0]
    nchunk = n // RCHUNK
    cps = seq_len // RCHUNK
    ns = min(LOCAL_CHUNKS, cps)
    assert cps % ns == 0
    W = RWKV_WIDTH
    rows = ns * RCHUNK
    hb = rows // 8
    last8 = n // 8 - 1
    mat = lambda dt: jax.ShapeDtypeStruct((N_DIR, nchunk, RWKV_PAIRS, RWKV_HEAD_DIM, PAIR_LANES), dt)
    mat_spec = pl.BlockSpec((N_DIR, ns, RWKV_PAIRS, RWKV_HEAD_DIM, PAIR_LANES), lambda c: (0, c, 0, 0, 0))
    tok = lambda dt: jax.ShapeDtypeStruct((N_DIR, n, W), dt)
    tok_spec = pl.BlockSpec((N_DIR, rows, W), lambda c: (0, c, 0))
    row_spec = pl.BlockSpec((rows, W), lambda c: (c, 0))
    return pl.pallas_call(
        functools.partial(_rwkv_local_kernel, cps, passes, ns),
        grid=(nchunk // ns,),
        in_specs=[pl.BlockSpec((rows, ZR_BLOCK), lambda c: (c, 0)),
                  pl.BlockSpec((8, ZR_BLOCK), lambda c: (jnp.maximum(c * hb - 1, 0), 0)),
                  pl.BlockSpec((8, ZR_BLOCK), lambda c: (jnp.minimum((c + 1) * hb, last8), 0)),
                  _resident((1, RWKV_COLS)),
                  _resident((N_DIR, 1, W)), _resident((N_DIR, DECAY_LORA, W)),
                  _resident((N_DIR, 1, W)), _resident((N_DIR, ICLR_LORA, W)),
                  _resident((GATE_LORA, W)),
                  _resident((1, W)), _resident((1, W)), _resident((1, W)),
                  _resident((W, W))],
        out_specs=[tok_spec, row_spec, mat_spec, mat_spec, row_spec, row_spec],
        out_shape=[tok(BF16), jax.ShapeDtypeStruct((n, W), F32), mat(BF16), mat(F32),
                   jax.ShapeDtypeStruct((n, W), F32), jax.ShapeDtypeStruct((n, W), F32)],
        compiler_params=_params(("arbitrary",)),
        name="rwkv_local",
    )(z, z, z, p["mu"], p["w0"], p["w_up"], p["a0"], p["a_up"], p["g_up"],
      p["kk_scale"], p["k_a"], p["r_k"], p["pones"])


SCAN_CHUNKS = 8


def _rwkv_scan_kernel(has_init, K, *refs):
    s0_ref = refs[0] if has_init else None
    (rpf_ref, rpb_ref, gmf_ref, gmb_ref, hmf_ref, hmb_ref,
     ysf_ref, ysb_ref, sout_ref, s_scr) = refs[1:] if has_init else refs

    @pl.when(pl.program_id(1) == 0)
    def _():
        s_scr[...] = s0_ref[0] if has_init else jnp.zeros(s_scr.shape, F32)

    C = RCHUNK
    rp_ref, gm_ref, hm_ref, ys_ref = (rpf_ref, rpb_ref), (gmf_ref, gmb_ref), (hmf_ref, hmb_ref), (ysf_ref, ysb_ref)
    chains = [(d, p) for d in range(N_DIR) for p in range(RWKV_PAIRS)]
    lanes = [slice(p * PAIR_LANES, (p + 1) * PAIR_LANES) for _, p in chains]
    nch = range(len(chains))
    s = [s_scr[d, p] for d, p in chains]
    for j in range(K):
        at = (j, K - 1 - j)
        rows = [slice(at[d] * C, (at[d] + 1) * C) for d, _ in chains]
        sb = [m.astype(BF16) for m in s]
        y = [_dg(rp_ref[chains[i][0]][0, rows[i], lanes[i]], sb[i], _NT) for i in nch]
        sg = [_dg(sb[i], _bd(gm_ref[chains[i][0]][0, at[chains[i][0]], chains[i][1]]), _NN) for i in nch]
        for i in nch:
            d, p = chains[i]
            ys_ref[d][rows[i], lanes[i]] = y[i]
        s = [sg[i] + _bd(hm_ref[chains[i][0]][0, at[chains[i][0]], chains[i][1]]) for i in nch]
    for i in nch:
        d, p = chains[i]
        s_scr[d, p] = s[i]
        n = RWKV_HEAD_DIM
        sout_ref[0, d, 2 * p] = s[i][0:n, 0:n]
        sout_ref[0, d, 2 * p + 1] = s[i][n:2 * n, n:2 * n]


def _rwkv_scan(s0, rp, gm, hm, batch, seq_len):
    K = min(SCAN_CHUNKS, seq_len // RCHUNK)
    assert seq_len % (RCHUNK * K) == 0
    spb = seq_len // (RCHUNK * K)
    n = batch * seq_len

    def fwd(b, s):
        return b * spb + s

    def bwd(b, s):
        return b * spb + spb - 1 - s

    def mat_spec(d, at):
        return pl.BlockSpec((1, K, RWKV_PAIRS, RWKV_HEAD_DIM, PAIR_LANES), lambda b, s: (d, at(b, s), 0, 0, 0))

    def tok_spec(d, at):
        return pl.BlockSpec((1, K * RCHUNK, RWKV_WIDTH), lambda b, s: (d, at(b, s), 0))

    st_spec = pl.BlockSpec((1, N_DIR, RWKV_PAIRS, PAIR_LANES, PAIR_LANES), lambda b, s: (b, 0, 0, 0, 0))
    ys = jax.ShapeDtypeStruct((n, RWKV_WIDTH), F32)
    has_init = s0 is not None
    return pl.pallas_call(
        functools.partial(_rwkv_scan_kernel, has_init, K),
        grid=(batch, spb),
        in_specs=([st_spec] if has_init else [])
        + [tok_spec(0, fwd), tok_spec(1, bwd),
           mat_spec(0, fwd), mat_spec(1, bwd), mat_spec(0, fwd), mat_spec(1, bwd)],
        out_specs=[pl.BlockSpec((K * RCHUNK, RWKV_WIDTH), lambda b, s: (fwd(b, s), 0)),
                   pl.BlockSpec((K * RCHUNK, RWKV_WIDTH), lambda b, s: (bwd(b, s), 0)),
                   pl.BlockSpec((1, N_DIR, RWKV_HEADS, RWKV_HEAD_DIM, RWKV_HEAD_DIM),
                                lambda b, s: (b, 0, 0, 0, 0))],
        out_shape=[ys, ys,
                   jax.ShapeDtypeStruct((batch, N_DIR, RWKV_HEADS, RWKV_HEAD_DIM, RWKV_HEAD_DIM), F32)],
        scratch_shapes=[pltpu.VMEM((N_DIR, RWKV_PAIRS, PAIR_LANES, PAIR_LANES), F32)],
        compiler_params=_params(("arbitrary", "arbitrary")),
        name="rwkv_scan",
    )(*([s0] if has_init else []), rp, rp, gm, gm, hm, hm)


def _dwconv(x, w_ref, width, vertical):
    T = x.shape[0]
    t = lax.broadcasted_iota(jnp.int32, (T, 1), 0)
    assert width & (width - 1) == 0
    colp = jnp.bitwise_and(t, width - 1)
    xl = jnp.where(colp == 0, 0.0, pltpu.roll(x, 1, 0))
    xr = jnp.where(colp == width - 1, 0.0, pltpu.roll(x, T - 1, 0))

    def tap_row(i):
        return w_ref[3 * i:3 * i + 1, :] * xl + w_ref[3 * i + 1:3 * i + 2, :] * x + w_ref[3 * i + 2:3 * i + 3, :] * xr

    out = tap_row(1)
    if vertical:
        out = out + jnp.where(t < width, 0.0, pltpu.roll(tap_row(0), width, 0))
        out = out + jnp.where(t >= T - width, 0.0, pltpu.roll(tap_row(2), T - width, 0))
    return out


def _store_time_on_lanes(x, out_ref, chunk):
    xt = x.T
    for c in range(x.shape[0] // chunk):
        out_ref[c] = xt[:, c * chunk:(c + 1) * chunk]


def _qk_conv_kernel(width, first_k_tile, x_ref, w_ref, o_ref, kt_ref):
    out = _silu(_dwconv(x_ref[...], w_ref, width, True))
    o_ref[...] = out

    @pl.when(pl.program_id(1) >= first_k_tile)
    def _():
        _store_time_on_lanes(out, kt_ref, MLSTM_CHUNK)


def _qk_conv(z, batch, seq_len, rows, conv_w9):
    n = batch * seq_len
    ch = 2 * MLSTM_WIDTH
    tc = CONV_CH_TILE
    off = ZM_OFF // tc
    first_k = MLSTM_WIDTH // tc
    block, width = seq_len, seq_len // rows
    cpb = block // MLSTM_CHUNK
    return pl.pallas_call(
        functools.partial(_qk_conv_kernel, width, first_k),
        grid=(n // block, ch // tc),
        in_specs=[pl.BlockSpec((block, tc), lambda b, j: (b, off + j)),
                  pl.BlockSpec((9, tc), lambda b, j: (0, j))],
        out_specs=[pl.BlockSpec((block, tc), lambda b, j: (b, j)),
                   pl.BlockSpec((cpb, tc, MLSTM_CHUNK), lambda b, j: (b, jnp.maximum(j - first_k, 0), 0))],
        out_shape=[jax.ShapeDtypeStruct((n, ch), F32),
                   jax.ShapeDtypeStruct((n // MLSTM_CHUNK, MLSTM_WIDTH, MLSTM_CHUNK), F32)],
        compiler_params=_params(("arbitrary", "arbitrary")),
        name="mlstm_qk_conv",
    )(z, conv_w9)


MLSTM_STEP_CHUNKS = 4

def _mlstm_scan_kernel(has_init, qkf_ref, qkb_ref, ktf_ref, ktb_ref, vf_ref, vb_ref, gcf_ref, gcb_ref,
                       grf_ref, grb_ref, gbc_ref, gbr_ref, *refs):
    init_refs = refs[0:3] if has_init else None
    hf_ref, hb_ref, cout_ref, nout_ref, mout_ref, c_scr, n_scr, m_scr = refs[3:] if has_init else refs
    step = pl.program_id(1)
    L = MLSTM_CHUNK
    dh = MLSTM_HEAD_DIM
    H = MLSTM_HEADS

    @pl.when(step == 0)
    def _():
        for scr, k in zip((c_scr, n_scr, m_scr), range(3)):
            scr[...] = init_refs[k][0] if has_init else jnp.zeros(scr.shape, F32)

    K = MLSTM_STEP_CHUNKS
    R = K * L
    row = lax.broadcasted_iota(jnp.int32, (L, L), 0)
    col = lax.broadcasted_iota(jnp.int32, (L, L), 1)
    lower = (row >= col)
    upper = (row <= col)
    lower_b = jnp.where(lower, 1.0, 0.0).astype(BF16)
    upper_b = jnp.where(upper, 1.0, 0.0).astype(BF16)
    rrow = lax.broadcasted_iota(jnp.int32, (R, R), 0)
    rcol = lax.broadcasted_iota(jnp.int32, (R, R), 1)
    same_chunk = jnp.bitwise_and(rrow, -L) == jnp.bitwise_and(rcol, -L)
    neg_inf = jnp.full((), -jnp.inf, F32)

    gcol, grow, bcol, brow, btot = [], [], [], [], []
    ones_b = jnp.ones((L, LANE), BF16)
    for d in range(N_DIR):
        gc_ref, gr_ref = (gcf_ref, grf_ref) if d == 0 else (gcb_ref, grb_ref)
        gcol.append(gc_ref[...] + gbc_ref[...])
        grow.append((gr_ref[...] + gbr_ref[...][None]).reshape(K * MLSTM_GATES, L))
        before = same_chunk & ((rrow >= rcol) if d == 0 else (rrow <= rcol))
        bcol.append(_mm_exact_lhs(jnp.where(before, 1.0, 0.0).astype(BF16), jax.nn.log_sigmoid(gcol[d]), _NN))
        frow = jax.nn.log_sigmoid(grow[d])
        brow.append(_mm_exact_rhs(frow, upper_b if d == 0 else lower_b, _NN))
        btot.append(_mm_exact_rhs(frow, ones_b, _NN))

    units = [(j, d, h) for j in range(K) for d in range(N_DIR) for h in range(H)]
    nun = range(len(units))
    q, k, kt, v, vb, qb = [], [], [], [], [], []
    c_row, b_col, b_last = [], [], []
    for j, d, h in units:
        at = j if d == 0 else K - 1 - j
        rows = slice(at * L, (at + 1) * L)
        st = d * H + h
        gi, gf = st, 2 * H + st
        qk_ref, kt_ref, v_ref = (qkf_ref, ktf_ref, vf_ref) if d == 0 else (qkb_ref, ktb_ref, vb_ref)
        q.append(qk_ref[rows, h * dh:(h + 1) * dh] * (dh ** -0.5))
        k.append(qk_ref[rows, MLSTM_WIDTH + h * dh:MLSTM_WIDTH + (h + 1) * dh])
        kt.append(kt_ref[at, h * dh:(h + 1) * dh, :])
        v.append(v_ref[rows, h * dh:(h + 1) * dh])
        qb.append(q[-1].astype(BF16))
        vb.append(v[-1].astype(BF16))
        b_col.append(jnp.broadcast_to(bcol[d][rows, gf:gf + 1], (L, LANE)))
        c_row.append(grow[d][at * MLSTM_GATES + gi:at * MLSTM_GATES + gi + 1, :]
                     - brow[d][at * MLSTM_GATES + gf:at * MLSTM_GATES + gf + 1, :])
        b_last.append(btot[d][at * MLSTM_GATES + gf:at * MLSTM_GATES + gf + 1, :])

    last = [L - 1 if d == 0 else 0 for _, d, _ in units]
    qk_t = [_dg(qb[i], k[i].astype(BF16), _NT) for i in nun]
    rel = [jnp.where(lower if units[i][1] == 0 else upper, c_row[i], neg_inf) for i in nun]
    mx = [jnp.broadcast_to(jnp.max(rel[i], axis=-1, keepdims=True), (L, LANE)) for i in nun]
    m_loc = [b_col[i] + mx[i] for i in nun]
    s_loc = [qk_t[i] * jnp.exp(rel[i] - mx[i][:, 0:L]) for i in nun]
    s_v = [_dg(s_loc[i].astype(BF16), vb[i], _NN) for i in nun]
    s_sum = [jnp.broadcast_to(jnp.sum(s_loc[i], axis=-1, keepdims=True), (L, LANE)) for i in nun]
    cmax = [mx[i][last[i]:last[i] + 1, :] for i in nun]
    m_w = [b_last[i] + cmax[i] for i in nun]
    wj = [jnp.exp(c_row[i] - cmax[i][:, 0:L]) for i in nun]
    kv = [_dg((kt[i] * wj[i]).astype(BF16), vb[i], _NN) for i in nun]
    w_k = [_mm(jnp.broadcast_to(wj[i], (8, L)), k[i], _NN, 3)[0:1] for i in nun]

    nst = N_DIR * H
    c_st = [c_scr[st] for st in range(nst)]
    n_st = [n_scr[st:st + 1, :] for st in range(nst)]
    m_st = [m_scr[st:st + 1, :] for st in range(nst)]
    for j in range(K):
        idx = [j * nst + st for st in range(nst)]
        q_c = [_dg(qb[i], c_st[st].astype(BF16), _NN) for st, i in enumerate(idx)]
        for st, i in enumerate(idx):
            _, d, h = units[i]
            at = j if d == 0 else K - 1 - j
            h_ref = hf_ref if d == 0 else hb_ref
            log_inter = b_col[i] + m_st[st]
            m_s = jnp.maximum(log_inter, m_loc[i])
            inter = jnp.exp(log_inter - m_s)
            local = jnp.exp(m_loc[i] - m_s)
            q_n = jnp.broadcast_to(jnp.sum(q[i] * n_st[st], axis=-1, keepdims=True), (L, LANE))
            den = inter * q_n + local * s_sum[i]
            scale = 1.0 / jnp.maximum(jnp.abs(den), jnp.exp(-m_s))
            h_ref[at * L:(at + 1) * L, h * dh:(h + 1) * dh] = (inter * scale) * q_c[st] + (local * scale) * s_v[i]
            m_new = jnp.maximum(b_last[i] + m_st[st], m_w[i])
            carry = jnp.exp(b_last[i] + m_st[st] - m_new)
            fresh = jnp.exp(m_w[i] - m_new)
            c_st[st] = carry * c_st[st] + fresh * kv[i]
            n_st[st] = carry * n_st[st] + fresh * w_k[i]
            m_st[st] = m_new


    for st in range(nst):
        c_scr[st] = c_st[st]
        n_scr[st:st + 1, :] = n_st[st]
        m_scr[st:st + 1, :] = m_st[st]
    nout_ref[0] = n_scr[...]
    mout_ref[0] = m_scr[...]

    @pl.when(step == pl.num_programs(1) - 1)
    def _():
        for st in range(nst):
            cout_ref[0, st] = c_st[st].T


def _mlstm_scan(z, qk, qk_blk, kt, gt, gate_bc, gate_br, c0, n0, m0, batch, seq_len):
    K = MLSTM_STEP_CHUNKS
    L = K * MLSTM_CHUNK
    assert seq_len % L == 0
    cps = seq_len // L
    n = batch * seq_len
    W = MLSTM_WIDTH
    nst = N_DIR * MLSTM_HEADS
    dh = MLSTM_HEAD_DIM

    def fw(b, c):
        return b * cps + c

    def bw(b, c):
        return b * cps + cps - 1 - c

    vblk = (ZM_OFF + 2 * W) // W
    gblk = ZG_OFF // LANE
    has_init = c0 is not None
    state_specs = [pl.BlockSpec((1, nst, dh, dh), lambda b, c: (b, 0, 0, 0)),
                   pl.BlockSpec((1, nst, dh), lambda b, c: (b, 0, 0)),
                   pl.BlockSpec((1, nst, LANE), lambda b, c: (b, 0, 0))]
    return pl.pallas_call(
        functools.partial(_mlstm_scan_kernel, has_init),
        grid=(batch, cps),
        in_specs=[pl.BlockSpec((L, 2 * W), lambda b, c: (fw(b, c), qk_blk)),
                  pl.BlockSpec((L, 2 * W), lambda b, c: (bw(b, c), qk_blk)),
                  pl.BlockSpec((K, W, MLSTM_CHUNK), lambda b, c: (fw(b, c), 0, 0)),
                  pl.BlockSpec((K, W, MLSTM_CHUNK), lambda b, c: (bw(b, c), 0, 0)),
                  pl.BlockSpec((L, W), lambda b, c: (fw(b, c), vblk)),
                  pl.BlockSpec((L, W), lambda b, c: (bw(b, c), vblk)),
                  pl.BlockSpec((L, LANE), lambda b, c: (fw(b, c), gblk)),
                  pl.BlockSpec((L, LANE), lambda b, c: (bw(b, c), gblk)),
                  pl.BlockSpec((K, MLSTM_GATES, MLSTM_CHUNK), lambda b, c: (fw(b, c), 0, 0)),
                  pl.BlockSpec((K, MLSTM_GATES, MLSTM_CHUNK), lambda b, c: (bw(b, c), 0, 0)),
                  _resident((1, LANE)),
                  _resident((MLSTM_GATES, 1))] + (state_specs if has_init else []),
        out_specs=[pl.BlockSpec((L, W), lambda b, c: (fw(b, c), 0)),
                   pl.BlockSpec((L, W), lambda b, c: (bw(b, c), 0))] + state_specs,
        out_shape=[jax.ShapeDtypeStruct((n, W), F32), jax.ShapeDtypeStruct((n, W), F32),
                   jax.ShapeDtypeStruct((batch, nst, dh, dh), F32),
                   jax.ShapeDtypeStruct((batch, nst, dh), F32),
                   jax.ShapeDtypeStruct((batch, nst, LANE), F32)],
        scratch_shapes=[pltpu.VMEM((nst, dh, dh), F32), pltpu.VMEM((nst, dh), F32),
                        pltpu.VMEM((nst, LANE), F32)],
        compiler_params=_params(("arbitrary", "arbitrary")),
        name="mlstm_scan",
    )(qk, qk, kt, kt, z, z, z, z, gt, gt, gate_bc, gate_br, *([c0, n0, m0] if has_init else []))


def _grid_conv(above, cur, below, cw_ref, width):
    T = cur.shape[0]
    E = T + 2 * width
    ext = jnp.concatenate([above, cur, below], axis=0)
    colp = jnp.bitwise_and(lax.broadcasted_iota(jnp.int32, (E, 1), 0), width - 1)
    left = jnp.where(colp == 0, 0.0, pltpu.roll(ext, 1, 0))
    right = jnp.where(colp == width - 1, 0.0, pltpu.roll(ext, E - 1, 0))

    def tap_row(i):
        rows = slice(i * width, i * width + T)
        return (cw_ref[3 * i:3 * i + 1, :] * left[rows] + cw_ref[3 * i + 1:3 * i + 2, :] * ext[rows]
                + cw_ref[3 * i + 2:3 * i + 3, :] * right[rows])

    return tap_row(0) + tap_row(1) + tap_row(2)


def _merge_kernel(conv, x_ref, mod_ref, ysf_ref, ysb_ref, y0_ref, bonus_ref, gate_ref, hf_ref, hb_ref, zo_ref,
                  zs_ref, lnxg_ref, lnxb_ref, gng_ref, pmean_ref, wbr_ref, wbm_ref, wout_ref, ng_ref, wup_ref,
                  cw_ref, cb_ref, *rest):
    if conv[0] == "seq":
        wdn_ref, out_ref = rest
    else:
        wdn_ref, modp_ref, out_ref = rest[0:3]
        scratch = rest[3:]

        @pl.when(pl.program_id(0) == 0)
        def _():
            for ref in scratch:
                ref[...] = jnp.zeros(ref.shape, ref.dtype)

    mod = mod_ref[0]
    g1 = mod[:, 2 * D_MODEL:3 * D_MODEL]
    sh2 = mod[:, 3 * D_MODEL:4 * D_MODEL]
    sc2 = mod[:, 4 * D_MODEL:5 * D_MODEL]

    ys = (ysf_ref[...] + ysb_ref[...]) + y0_ref[...]
    pmean = pmean_ref[...]
    mean = _mm_exact_rhs(ys, pmean, _NN)
    cen = ys - mean
    var = _mm_exact_rhs(cen * cen, pmean, _NN)
    y_r = (cen * lax.rsqrt(var + RWKV_GN_EPS) * lnxg_ref[...] + lnxb_ref[...] + bonus_ref[...]) * gate_ref[...]

    hs = hf_ref[...] + hb_ref[...]
    parts = []
    for h in range(MLSTM_HEADS):
        hh = hs[:, h * MLSTM_HEAD_DIM:(h + 1) * MLSTM_HEAD_DIM]
        mu = jnp.mean(hh, axis=-1, keepdims=True)
        ce = hh - mu
        va = jnp.mean(ce * ce, axis=-1, keepdims=True)
        parts.append(ce * lax.rsqrt(va + MLSTM_GN_EPS))
    y_m = jnp.concatenate(parts, axis=1) * gng_ref[...] * _sigmoid(zo_ref[...])

    gates = zs_ref[...]
    merged = (gates[:, 0:D_MODEL] * _dg(y_r.astype(BF16), wbr_ref[...], _NN)
              + gates[:, D_MODEL:2 * D_MODEL] * _dg(y_m.astype(BF16), wbm_ref[...], _NN))
    t = _dg(merged.astype(BF16), wout_ref[...], _NN)
    x1 = x_ref[...] + g1 * _rms(t, ng_ref[1:2, :])
    h2 = _rms(x1, ng_ref[2:3, :]) * (1.0 + sc2) + sh2
    u = _dg(h2.astype(BF16), wup_ref[...], _NN)
    if conv[0] == "seq":
        pre = _dwconv(u[:, 0:D_FF], cw_ref, conv[1], False) + cb_ref[...]
        act = (_silu(pre) * u[:, D_FF:2 * D_FF]).astype(BF16)
        g2 = mod[:, 5 * D_MODEL:6 * D_MODEL]
        out_ref[...] = x1 + g2 * _rms(_dg(act, wdn_ref[...], _NN), ng_ref[3:4, :])
    else:
        _, width, tiles_per_image = conv
        act_scr, val_scr, tail_scr, x1_scr, act_ref = scratch
        T = act_scr.shape[0]
        step = pl.program_id(0)
        pos = (step + tiles_per_image - 1) % tiles_per_image
        for c0 in range(0, D_FF, CONV_CH_TILE):
            ch = slice(c0, c0 + CONV_CH_TILE)
            above = jnp.where(pos != 0, tail_scr[:, ch], 0.0)
            below = jnp.where(pos != tiles_per_image - 1, u[0:width, ch], 0.0)
            pre = _grid_conv(above, act_scr[:, ch], below, cw_ref.at[:, ch], width) + cb_ref[:, ch]
            act_ref[:, ch] = (_silu(pre) * val_scr[:, ch]).astype(BF16)
        g2 = modp_ref[0][:, 5 * D_MODEL:6 * D_MODEL]
        out_ref[...] = x1_scr[...] + g2 * _rms(_dg(act_ref[...], wdn_ref[...], _NN), ng_ref[3:4, :])
        tail_scr[...] = act_scr[T - width:T, :]
        act_scr[...] = u[:, 0:D_FF]
        val_scr[...] = u[:, D_FF:2 * D_FF]
        x1_scr[...] = x1


def _merge(x2, mod, mod_row, z, ysf, ysb, y0, bonus, gate, hf, hb, p, conv):
    n = x2.shape[0]
    W = RWKV_WIDTH
    rows = MERGE_TILE
    ntiles = n // rows
    delayed = conv[0] == "grid"
    cur = (lambda i: jnp.minimum(i, ntiles - 1)) if delayed else (lambda i: i)
    tile = lambda w: pl.BlockSpec((rows, w), lambda i: (cur(i), 0))
    prev = lambda i: jnp.maximum(i - 1, 0)
    extra_in, extra_args = [_resident((D_FF, D_MODEL))], [p["ffn_down"]]
    if delayed:
        scratch = [pltpu.VMEM((rows, D_FF), F32), pltpu.VMEM((rows, D_FF), F32), pltpu.VMEM((conv[1], D_FF), F32),
                   pltpu.VMEM((rows, D_MODEL), F32), pltpu.VMEM((rows, D_FF), BF16)]
        extra_in.append(pl.BlockSpec((1, 1, 6 * D_MODEL), lambda i: (mod_row(prev(i) * rows), 0, 0)))
        extra_args.append(mod)
        out_spec = pl.BlockSpec((rows, D_MODEL), lambda i: (prev(i), 0))
    else:
        scratch = []
        out_spec = tile(D_MODEL)
    out_specs = [out_spec]
    out_shape = [jax.ShapeDtypeStruct((n, D_MODEL), F32)]
    return pl.pallas_call(
        functools.partial(_merge_kernel, conv),
        grid=(ntiles + 1 if delayed else ntiles,),
        in_specs=[tile(D_MODEL),
                  pl.BlockSpec((1, 1, 6 * D_MODEL), lambda i: (mod_row(cur(i) * rows), 0, 0)),
                  tile(W), tile(W), tile(W), tile(W), tile(W), tile(MLSTM_WIDTH), tile(MLSTM_WIDTH),
                  pl.BlockSpec((rows, MLSTM_WIDTH),
                               lambda i: (cur(i), (ZM_OFF + 3 * MLSTM_WIDTH) // MLSTM_WIDTH)),
                  pl.BlockSpec((rows, GATE_COLS), lambda i: (cur(i), ZS_OFF // GATE_COLS)),
                  _resident((1, W)), _resident((1, W)), _resident((1, MLSTM_WIDTH)),
                  _resident((W, W)),
                  _resident((W, D_MODEL)), _resident((MLSTM_WIDTH, D_MODEL)),
                  _resident((D_MODEL, D_MODEL)), _resident((4, D_MODEL)),
                  _resident((D_MODEL, 2 * D_FF)), _resident((9, D_FF)), _resident((1, D_FF))] + extra_in,
        out_specs=out_specs,
        out_shape=out_shape,
        scratch_shapes=scratch,
        compiler_params=_params(("arbitrary",)),
        name="merge_ffn_up",
    )(x2, mod, ysf, ysb, y0, bonus, gate, hf, hb, z, z, p["lnx_g"], p["lnx_b"], p["gn_g"], p["pmean"],
      p["w_br"], p["w_bm"], p["w_out"], p["norm_g"], p["ffn_up"], p["ffn_conv"], p["ffn_conv_b"], *extra_args)


RWKV_LOCAL_PASSES = 1


def _state_to_pairs(s):
    b = s.shape[0]
    s = s.reshape(b, N_DIR, RWKV_PAIRS, 2, RWKV_HEAD_DIM, RWKV_HEAD_DIM)
    zero = jnp.zeros_like(s[:, :, :, 0])
    top = jnp.concatenate([s[:, :, :, 0], zero], axis=-1)
    bot = jnp.concatenate([zero, s[:, :, :, 1]], axis=-1)
    return jnp.concatenate([top, bot], axis=-2)


def _trunk(x, mod, mod_row, rows, states, p):
    batch, seq_len, _ = x.shape
    n = batch * seq_len
    x2 = x.reshape(n, D_MODEL)
    nst = N_DIR * MLSTM_HEADS
    if states is None:
        s0 = c0 = n0 = m0 = None
    else:
        s0, c0, n0, m0 = states
        s0 = _state_to_pairs(s0)
        c0 = jnp.swapaxes(c0, -1, -2).reshape(batch, nst, MLSTM_HEAD_DIM, MLSTM_HEAD_DIM)
        n0 = n0.reshape(batch, nst, MLSTM_HEAD_DIM)
        m0 = jnp.broadcast_to(m0.reshape(batch, nst, 1), (batch, nst, LANE))

    fuse_width = seq_len if (rows == 1 and IN_TILE % seq_len == 0 and MERGE_TILE % seq_len == 0) else None
    proj = _in_proj(x2, mod, mod_row, p["norm_g"][0:1], p["w_in"], p["mlstm_conv"], fuse_width)
    z = proj[0]

    rp, y0, gm, hm, gate, bonus = _rwkv_local(z, seq_len, p, RWKV_LOCAL_PASSES)
    ysf, ysb, s_fin = _rwkv_scan(s0, rp, gm, hm, batch, seq_len)

    if fuse_width is None:
        z, gt = proj
        qk, kt = _qk_conv(z, batch, seq_len, rows, p["mlstm_conv"])
        qk_blk = 0
    else:
        z, gt, kt = proj
        qk, qk_blk = z, ZM_OFF // (2 * MLSTM_WIDTH)
    hf, hb, c_fin, n_fin, m_fin = _mlstm_scan(z, qk, qk_blk, kt, gt, p["gate_bc"], p["gate_br"],
                                              c0, n0, m0, batch, seq_len)

    if fuse_width is not None:
        ffn_conv = ("seq", fuse_width)
    else:
        width = seq_len // rows
        assert rows > 1 and width & (width - 1) == 0 and MERGE_TILE % width == 0 and seq_len % MERGE_TILE == 0
        ffn_conv = ("grid", width, seq_len // MERGE_TILE)
    out, = _merge(x2, mod, mod_row, z, ysf, ysb, y0, bonus, gate, hf, hb, p, ffn_conv)

    new_states = (s_fin,
                  c_fin.reshape(batch, N_DIR, MLSTM_HEADS, MLSTM_HEAD_DIM, MLSTM_HEAD_DIM),
                  n_fin.reshape(batch, N_DIR, MLSTM_HEADS, MLSTM_HEAD_DIM),
                  m_fin[:, :, 0].reshape(batch, N_DIR, MLSTM_HEADS))
    return out.reshape(batch, seq_len, D_MODEL), new_states


def _pack_layer(l, ada_w, ada_b, norm_g, w_in, rwkv_mu, rwkv_w0, rwkv_w_up, rwkv_a0, rwkv_a_up,
                rwkv_g_up, rwkv_kk_scale, rwkv_k_a, rwkv_r_k, rwkv_lnx_g, rwkv_lnx_b, mlstm_conv,
                mlstm_gate_b, mlstm_gn_g, w_branch_rwkv, w_branch_mlstm, w_out, ffn_up, ffn_conv,
                ffn_conv_b, ffn_down):
    W = RWKV_WIDTH
    w_in_b = w_in[l].astype(BF16)

    head = jnp.arange(W, dtype=jnp.int32) // RWKV_HEAD_DIM
    same = (head[:, None] == head[None, :])
    gb = mlstm_gate_b[l].reshape(1, MLSTM_GATES)
    return dict(
        ada_w=ada_w[l], ada_b=ada_b[l], norm_g=norm_g[l], w_in=w_in_b,
        mu=rwkv_mu[l].reshape(1, RWKV_COLS),
        w0=rwkv_w0[l].reshape(N_DIR, 1, W), w_up=rwkv_w_up[l],
        a0=rwkv_a0[l].reshape(N_DIR, 1, W), a_up=rwkv_a_up[l], g_up=rwkv_g_up[l],
        kk_scale=rwkv_kk_scale[l].reshape(1, W), k_a=rwkv_k_a[l].reshape(1, W),
        r_k=rwkv_r_k[l].reshape(1, W),
        lnx_g=rwkv_lnx_g[l].reshape(1, W), lnx_b=rwkv_lnx_b[l].reshape(1, W),
        pones=same.astype(BF16), pmean=(same.astype(F32) / RWKV_HEAD_DIM).astype(BF16),
        mlstm_conv=mlstm_conv[l].reshape(9, 2 * MLSTM_WIDTH),
        gate_bc=jnp.pad(gb, ((0, 0), (0, LANE - MLSTM_GATES))), gate_br=gb.reshape(MLSTM_GATES, 1),
        gn_g=mlstm_gn_g[l].reshape(1, MLSTM_WIDTH),
        w_br=w_branch_rwkv[l].astype(BF16), w_bm=w_branch_mlstm[l].astype(BF16),
        w_out=w_out[l].astype(BF16), ffn_up=ffn_up[l].astype(BF16),
        ffn_conv=ffn_conv[l].reshape(9, D_FF), ffn_conv_b=ffn_conv_b[l].reshape(1, D_FF),
        ffn_down=ffn_down[l].astype(BF16),
    )


def kernel(x_prompt, x_sample, c, state_rwkv, state_mlstm_C, state_mlstm_n, state_mlstm_m, c_ctx,
           ada_w, ada_b, norm_g, w_in, rwkv_mu, rwkv_w0, rwkv_w_up, rwkv_a0, rwkv_a_up, rwkv_g_up,
           rwkv_kk_scale, rwkv_k_a, rwkv_r_k, rwkv_lnx_g, rwkv_lnx_b, mlstm_conv, mlstm_gate_b,
           mlstm_gn_g, w_branch_rwkv, w_branch_mlstm, w_out, ffn_up, ffn_conv, ffn_conv_b, ffn_down):
    depth = ada_w.shape[0]
    batch = x_prompt.shape[0]
    dec_batch, dec_seq, _ = x_sample.shape
    latent_rows = dec_seq // GRID_W
    cond = jnp.concatenate([c_ctx[None, :], c, jnp.zeros((8 - 1 - dec_batch, D_MODEL), F32)], axis=0)

    xp, xs = x_prompt, x_sample
    new_s, new_c, new_n, new_m = [], [], [], []
    for l in range(depth):
        p = _pack_layer(l, ada_w, ada_b, norm_g, w_in, rwkv_mu, rwkv_w0, rwkv_w_up, rwkv_a0, rwkv_a_up,
                        rwkv_g_up, rwkv_kk_scale, rwkv_k_a, rwkv_r_k, rwkv_lnx_g, rwkv_lnx_b, mlstm_conv,
                        mlstm_gate_b, mlstm_gn_g, w_branch_rwkv, w_branch_mlstm, w_out, ffn_up, ffn_conv,
                        ffn_conv_b, ffn_down)
        mod = _ada(cond, p["ada_w"], p["ada_b"]).reshape(8, 1, 6 * D_MODEL)
        xp, (s, cc, nn, mm) = _trunk(xp, mod, lambda r: 0, 1, None, p)
        new_s.append(s)
        new_c.append(cc)
        new_n.append(nn)
        new_m.append(mm)
        xs, _ = _trunk(xs, mod, lambda r: 1 + r // dec_seq, latent_rows,
                       (state_rwkv[:, l], state_mlstm_C[:, l], state_mlstm_n[:, l], state_mlstm_m[:, l]), p)
    return (xp, xs, jnp.stack(new_s, axis=1), jnp.stack(new_c, axis=1),
            jnp.stack(new_n, axis=1), jnp.stack(new_m, axis=1))
```

```python
import functools

import jax
import jax.numpy as jnp
from jax import lax
from jax.experimental import pallas as pl
from jax.experimental.pallas import tpu as pltpu

F32 = jnp.float32
BF16 = jnp.bfloat16

D_MODEL = 1024
N_DIR = 2
RWKV_HEADS = 8
RWKV_HEAD_DIM = 64
RWKV_WIDTH = RWKV_HEADS * RWKV_HEAD_DIM
DECAY_LORA = 64
ICLR_LORA = 64
GATE_LORA = 128
MLSTM_HEADS = 4
MLSTM_HEAD_DIM = 128
MLSTM_WIDTH = MLSTM_HEADS * MLSTM_HEAD_DIM
MLSTM_CHUNK = 64
D_FF = 2816
GRID_W = 64
RMS_EPS = 1e-6
RWKV_GN_EPS = 64e-5
MLSTM_GN_EPS = 1e-5
DECAY_SCALE = 0.606531

RWKV_COLS = 3 * RWKV_WIDTH + N_DIR * DECAY_LORA + N_DIR * ICLR_LORA + GATE_LORA
MLSTM_GATES = 2 * N_DIR * MLSTM_HEADS
GATE_COLS = 2 * D_MODEL

LANE = 128
ZR_BLOCK = 2048
ZG_OFF = RWKV_COLS
ZM_OFF = ZR_BLOCK
ZS_OFF = ZM_OFF + 4 * MLSTM_WIDTH
Z_COLS = ZS_OFF + GATE_COLS

IN_TILE = 512
MERGE_TILE = 256
RCHUNK = 64
CONV_CH_TILE = 256
VMEM_LIMIT = 56 * 1024 * 1024


def _params(sem):
    return pltpu.CompilerParams(dimension_semantics=sem, vmem_limit_bytes=VMEM_LIMIT)


def _resident(shape):
    nd = len(shape)
    return pl.BlockSpec(shape, lambda *_: (0,) * nd, pipeline_mode=pl.Buffered(1))


def _split2(a):
    hi = a.astype(BF16)
    lo = (a - hi.astype(F32)).astype(BF16)
    return hi, lo


def _split3(a):
    hi = a.astype(BF16)
    r1 = a - hi.astype(F32)
    mid = r1.astype(BF16)
    lo = (r1 - mid.astype(F32)).astype(BF16)
    return hi, mid, lo


def _dg(a, b, dims):
    return lax.dot_general(a, b, dims, preferred_element_type=F32)


def _mm(a, b, dims, passes):
    if passes == 1:
        return _dg(a.astype(BF16), b.astype(BF16), dims)
    ah, al = _split2(a)
    bh, bl = _split2(b)
    return _dg(ah, bh, dims) + (_dg(ah, bl, dims) + _dg(al, bh, dims))


def _mm_exact_lhs(a_bf16, b, dims, pieces=3):
    if pieces == 2:
        b1, b2 = _split2(b)
        return _dg(a_bf16, b1, dims) + _dg(a_bf16, b2, dims)
    b1, b2, b3 = _split3(b)
    return _dg(a_bf16, b1, dims) + (_dg(a_bf16, b2, dims) + _dg(a_bf16, b3, dims))


def _mm_exact_rhs(a, b_bf16, dims, pieces=3):
    if pieces == 2:
        a1, a2 = _split2(a)
        return _dg(a1, b_bf16, dims) + _dg(a2, b_bf16, dims)
    a1, a2, a3 = _split3(a)
    return _dg(a1, b_bf16, dims) + (_dg(a2, b_bf16, dims) + _dg(a3, b_bf16, dims))


_NN = (((1,), (0,)), ((), ()))
_NT = (((1,), (1,)), ((), ()))
_TN = (((0,), (0,)), ((), ()))


def _sigmoid(x):
    return jax.nn.sigmoid(x)


def _silu(x):
    return x * jax.nn.sigmoid(x)


def _rms(x, g):
    return x * lax.rsqrt(jnp.mean(x * x, axis=-1, keepdims=True) + RMS_EPS) * g


def _ada_kernel(cond_ref, w_ref, b_ref, o_ref):
    s = _silu(cond_ref[...])
    o_ref[...] = _dg(s.astype(BF16), w_ref[...].astype(BF16), _NN) + b_ref[...]


def _ada(cond8, ada_w, ada_b):
    n = ada_w.shape[1]
    tn = 1536
    return pl.pallas_call(
        _ada_kernel,
        grid=(n // tn,),
        in_specs=[_resident((8, D_MODEL)),
                  pl.BlockSpec((D_MODEL, tn), lambda j: (0, j)),
                  pl.BlockSpec((1, tn), lambda j: (0, j))],
        out_specs=pl.BlockSpec((8, tn), lambda j: (0, j)),
        out_shape=jax.ShapeDtypeStruct((8, n), F32),
        compiler_params=_params(("arbitrary",)),
        name="ada_mod",
    )(cond8, ada_w, ada_b.reshape(1, n))


def _in_kernel(conv_width, x_ref, mod_ref, g_ref, w_ref, cw_ref, z_ref, gt_ref, *kt_ref):
    mod = mod_ref[0]
    sh = mod[:, 0:D_MODEL]
    sc = mod[:, D_MODEL:2 * D_MODEL]
    h = (_rms(x_ref[...], g_ref[...]) * (1.0 + sc) + sh).astype(BF16)
    m_off = RWKV_COLS
    t_off = RWKV_COLS + 4 * MLSTM_WIDTH
    z_ref[:, 0:ZG_OFF] = _dg(h, w_ref[:, 0:m_off], _NN)
    tail = _dg(h, w_ref[:, t_off:w_ref.shape[1]], _NN)
    zg = tail[:, 0:LANE]
    z_ref[:, ZG_OFF:ZM_OFF] = zg
    zgt = zg.T
    for c in range(zg.shape[0] // MLSTM_CHUNK):
        gt_ref[c] = zgt[0:MLSTM_GATES, c * MLSTM_CHUNK:(c + 1) * MLSTM_CHUNK]
    z_ref[:, ZS_OFF:Z_COLS] = _sigmoid(tail[:, MLSTM_GATES:MLSTM_GATES + GATE_COLS])
    zm = _dg(h, w_ref[:, m_off:t_off], _NN)
    if conv_width is None:
        z_ref[:, ZM_OFF:ZS_OFF] = zm
    else:
        qk_cols = 2 * MLSTM_WIDTH
        qk = _silu(_dwconv(zm[:, 0:qk_cols], cw_ref, conv_width, False))
        z_ref[:, ZM_OFF:ZM_OFF + qk_cols] = qk
        z_ref[:, ZM_OFF + qk_cols:ZS_OFF] = zm[:, qk_cols:]
        _store_time_on_lanes(qk[:, MLSTM_WIDTH:qk_cols], kt_ref[0], MLSTM_CHUNK)


def _in_proj(x2, mod, mod_row, norm_g0, w_in, conv_w9, conv_width):
    n = x2.shape[0]
    tile = IN_TILE
    cpt = tile // MLSTM_CHUNK
    nchunk = n // MLSTM_CHUNK
    out_specs = [pl.BlockSpec((tile, Z_COLS), lambda i: (i, 0)),
                 pl.BlockSpec((cpt, MLSTM_GATES, MLSTM_CHUNK), lambda i: (i, 0, 0))]
    out_shape = [jax.ShapeDtypeStruct((n, Z_COLS), F32),
                 jax.ShapeDtypeStruct((nchunk, MLSTM_GATES, MLSTM_CHUNK), F32)]
    if conv_width is not None:
        out_specs.append(pl.BlockSpec((cpt, MLSTM_WIDTH, MLSTM_CHUNK), lambda i: (i, 0, 0)))
        out_shape.append(jax.ShapeDtypeStruct((nchunk, MLSTM_WIDTH, MLSTM_CHUNK), F32))
    return pl.pallas_call(
        functools.partial(_in_kernel, conv_width),
        grid=(n // tile,),
        in_specs=[pl.BlockSpec((tile, D_MODEL), lambda i: (i, 0)),
                  pl.BlockSpec((1, 1, 6 * D_MODEL), lambda i: (mod_row(i * tile), 0, 0)),
                  _resident((1, D_MODEL)), _resident(w_in.shape), _resident(conv_w9.shape)],
        out_specs=out_specs,
        out_shape=out_shape,
        compiler_params=_params(("arbitrary",)),
        name="in_proj",
    )(x2, mod, norm_g0, w_in, conv_w9)


LOCAL_CHUNKS = 4
PAIR_LANES = 2 * RWKV_HEAD_DIM
RWKV_PAIRS = RWKV_HEADS // 2


def _bd(x):
    lane = lax.broadcasted_iota(jnp.int32, x.shape, 1)
    left = lane < RWKV_HEAD_DIM
    return jnp.concatenate([jnp.where(left, x, 0.0), jnp.where(left, 0.0, x)], axis=0)


def _rwkv_local_kernel(chunks_per_seq, passes, NS,
                       z_ref, zp_ref, zn_ref, mu_ref, w0_ref, wup_ref, a0_ref, aup_ref, gup_ref,
                       kks_ref, ka_ref, rk_ref, pones_ref,
                       rp_ref, y0_ref, gm_ref, hm_ref, gate_ref, bonus_ref):
    C = RCHUNK
    W = RWKV_WIDTH
    R = NS * C
    first = (pl.program_id(0) * NS) % chunks_per_seq
    has_prev = first != 0
    has_next = first + NS != chunks_per_seq

    z = z_ref[:, 0:RWKV_COLS]
    zp = jnp.where(has_prev, zp_ref[7:8, 0:RWKV_COLS], 0.0)
    zn = jnp.where(has_next, zn_ref[0:1, 0:RWKV_COLS], 0.0)
    trow = lax.broadcasted_iota(jnp.int32, (R, 1), 0)
    prev = jnp.where(trow == 0, zp, pltpu.roll(z, 1, 0))
    nxt = jnp.where(trow == R - 1, zn, pltpu.roll(z, R - 1, 0))
    zs = z + mu_ref[...] * (0.5 * (prev + nxt) - z)

    r = zs[:, 0:W]
    k = zs[:, W:2 * W]
    v = zs[:, 2 * W:3 * W]
    gd = zs[:, 3 * W + 2 * DECAY_LORA + 2 * ICLR_LORA:RWKV_COLS]
    gate_ref[...] = _dg(_sigmoid(gd).astype(BF16), gup_ref[...].astype(BF16), _NN)

    pones = pones_ref[...]
    kks = k * kks_ref[...]
    norm = jnp.sqrt(_mm_exact_rhs(kks * kks, pones, _NN, pieces=2))
    kk = kks / jnp.maximum(norm, 1e-12)

    P = PAIR_LANES
    row = lax.broadcasted_iota(jnp.int32, (R, R), 0)
    col = lax.broadcasted_iota(jnp.int32, (R, R), 1)
    same_chunk = jnp.bitwise_and(row, -C) == jnp.bitwise_and(col, -C)
    prow = lax.broadcasted_iota(jnp.int32, (C, P), 0)
    pcol = jnp.bitwise_and(lax.broadcasted_iota(jnp.int32, (C, P), 1), RWKV_HEAD_DIM - 1)
    eye_p = jnp.where(prow == pcol, 1.0, 0.0)
    left_head = lax.broadcasted_iota(jnp.int32, (C, P), 1) < RWKV_HEAD_DIM

    def diag_blocks(m):
        return jnp.where(left_head, m[0:RWKV_HEAD_DIM], m[RWKV_HEAD_DIM:P])

    abar, rbar, kt, bt, kw, bw, wc, strict, incl = [], [], [], [], [], [], [], [], []
    kd_sum = None
    for d in range(N_DIR):
        o = 3 * W + d * DECAY_LORA
        wd = zs[:, o:o + DECAY_LORA]
        o = 3 * W + 2 * DECAY_LORA + d * ICLR_LORA
        ad = zs[:, o:o + ICLR_LORA]
        logw = -DECAY_SCALE * _sigmoid(w0_ref[d] + _dg(jnp.tanh(wd).astype(BF16), wup_ref[d].astype(BF16), _NN))
        a = _sigmoid(a0_ref[d] + _dg(ad.astype(BF16), aup_ref[d].astype(BF16), _NN))
        kd = k * (1.0 + (a - 1.0) * ka_ref[...])
        b = kk * a
        kd_sum = kd if kd_sum is None else kd_sum + kd

        earlier_or_same = same_chunk & ((row >= col) if d == 0 else (row <= col))
        cum_i = _mm_exact_lhs(jnp.where(earlier_or_same, 1.0, 0.0).astype(BF16), logw, _NN, pieces=2)
        cum_e = cum_i - logw
        ab_d, rb_d, kt_d, bt_d, kw_d, bw_d, wc_d = [], [], [], [], [], [], []
        for s in range(NS):
            rs = slice(s * C, (s + 1) * C)
            ci_s = cum_i[rs]
            ctot = jnp.sum(logw[rs], axis=0, keepdims=True)
            e_ni = jnp.exp(-ci_s)
            e_ti = jnp.exp(ctot - ci_s)
            ab_d.append(kk[rs] * jnp.exp(cum_e[rs]))
            rb_d.append(r[rs] * jnp.exp(ci_s))
            kt_d.append(kd[rs] * e_ni)
            bt_d.append(b[rs] * e_ni)
            kw_d.append(kd[rs] * e_ti)
            bw_d.append(b[rs] * e_ti)
            wc_d.append(jnp.exp(ctot))
        abar.append(ab_d)
        rbar.append(rb_d)
        kt.append(kt_d)
        bt.append(bt_d)
        kw.append(kw_d)
        bw.append(bw_d)
        wc.append(wc_d)
        strict.append((prow > pcol) if d == 0 else (prow < pcol))
        incl.append((prow >= pcol) if d == 0 else (prow <= pcol))
    bonus_ref[...] = _mm_exact_rhs(r * kd_sum * rk_ref[...], pones, _NN, pieces=2) * v

    mm = functools.partial(_mm, passes=passes)
    chains = [(s, d, p) for s in range(NS) for d in range(N_DIR) for p in range(RWKV_PAIRS)]
    nch = range(len(chains))

    def sel(arr, i):
        s, d, p = chains[i]
        return arr[d][s][:, p * P:(p + 1) * P]

    cat0 = lambda a_, b_: jnp.concatenate([a_, b_], axis=0)
    cat1 = lambda a_, b_: jnp.concatenate([a_, b_], axis=1)
    vsl = [v[s * C:(s + 1) * C, p * P:(p + 1) * P] for s, _, p in chains]
    lhs = [cat0(sel(abar, i), sel(rbar, i)) for i in nch]
    by = [mm(lhs[i], cat0(_bd(sel(bt, i)), _bd(sel(kt, i))), _NT) for i in nch]
    a_kk = [jnp.where(strict[chains[i][1]], by[i][0:C, 0:P], 0.0) for i in nch]
    a_rb = [jnp.where(incl[chains[i][1]], by[i][C:2 * C, 0:P], 0.0) for i in nch]
    a_kv = [jnp.where(strict[chains[i][1]], by[i][0:C, P:2 * P], 0.0) for i in nch]
    a_rk = [jnp.where(incl[chains[i][1]], by[i][C:2 * C, P:2 * P], 0.0) for i in nch]
    on_v = [mm(cat0(a_kv[i], a_rk[i]), _bd(vsl[i]), _NN) for i in nch]

    x = [-m for m in a_kk]
    tinv = [eye_p + m for m in x]
    x = [mm(m, _bd(m), _NN) for m in x]
    for _ in range(4):
        both = [mm(cat0(tinv[i], x[i]), _bd(x[i]), _NN) for i in nch]
        tinv = [tinv[i] + both[i][0:C] for i in nch]
        x = [m[C:2 * C] for m in both]
    tinv = [tinv[i] + mm(tinv[i], _bd(x[i]), _NN) for i in nch]

    solved = [mm(tinv[i], cat1(_bd(sel(abar, i)), _bd(on_v[i][0:C])), _NN) for i in nch]
    ap = [m[:, 0:P] for m in solved]
    u0 = [m[:, P:2 * P] for m in solved]
    corr = [mm(a_rb[i], cat1(_bd(ap[i]), _bd(u0[i])), _NN) for i in nch]
    on_b = [mm(cat1(ap[i], u0[i]), sel(bw, i), _TN) for i in nch]
    vk = [mm(vsl[i], sel(kw, i), _TN) for i in nch]
    for i in nch:
        s, d, p = chains[i]
        rows = slice(s * C, (s + 1) * C)
        lanes = slice(p * P, (p + 1) * P)
        rp_ref[d, rows, lanes] = (sel(rbar, i) - corr[i][:, 0:P]).astype(BF16)
        gm_ref[d, s, p] = (eye_p * sel(wc, i) - diag_blocks(on_b[i][0:P])).astype(BF16)
        hm_ref[d, s, p] = diag_blocks(vk[i] - on_b[i][P:2 * P])
    for s in range(NS):
        for p in range(RWKV_PAIRS):
            f, b_ = chains.index((s, 0, p)), chains.index((s, 1, p))
            y0_ref[s * C:(s + 1) * C, p * P:(p + 1) * P] = ((on_v[f][C:2 * C] - corr[f][:, P:2 * P])
                                                            + (on_v[b_][C:2 * C] - corr[b_][:, P:2 * P]))


def _rwkv_local(z, seq_len, p, passes):
    n = z.shape[0]
    nchunk = n // RCHUNK
    cps = seq_len // RCHUNK
    ns = min(LOCAL_CHUNKS, cps)
    assert cps % ns == 0
    W = RWKV_WIDTH
    rows = ns * RCHUNK
    hb = rows // 8
    last8 = n // 8 - 1
    mat = lambda dt: jax.ShapeDtypeStruct((N_DIR, nchunk, RWKV_PAIRS, RWKV_HEAD_DIM, PAIR_LANES), dt)
    mat_spec = pl.BlockSpec((N_DIR, ns, RWKV_PAIRS, RWKV_HEAD_DIM, PAIR_LANES), lambda c: (0, c, 0, 0, 0))
    tok = lambda dt: jax.ShapeDtypeStruct((N_DIR, n, W), dt)
    tok_spec = pl.BlockSpec((N_DIR, rows, W), lambda c: (0, c, 0))
    row_spec = pl.BlockSpec((rows, W), lambda c: (c, 0))
    return pl.pallas_call(
        functools.partial(_rwkv_local_kernel, cps, passes, ns),
        grid=(nchunk // ns,),
        in_specs=[pl.BlockSpec((rows, ZR_BLOCK), lambda c: (c, 0)),
                  pl.BlockSpec((8, ZR_BLOCK), lambda c: (jnp.maximum(c * hb - 1, 0), 0)),
                  pl.BlockSpec((8, ZR_BLOCK), lambda c: (jnp.minimum((c + 1) * hb, last8), 0)),
                  _resident((1, RWKV_COLS)),
                  _resident((N_DIR, 1, W)), _resident((N_DIR, DECAY_LORA, W)),
                  _resident((N_DIR, 1, W)), _resident((N_DIR, ICLR_LORA, W)),
                  _resident((GATE_LORA, W)),
                  _resident((1, W)), _resident((1, W)), _resident((1, W)),
                  _resident((W, W))],
        out_specs=[tok_spec, row_spec, mat_spec, mat_spec, row_spec, row_spec],
        out_shape=[tok(BF16), jax.ShapeDtypeStruct((n, W), F32), mat(BF16), mat(F32),
                   jax.ShapeDtypeStruct((n, W), F32), jax.ShapeDtypeStruct((n, W), F32)],
        compiler_params=_params(("arbitrary",)),
        name="rwkv_local",
    )(z, z, z, p["mu"], p["w0"], p["w_up"], p["a0"], p["a_up"], p["g_up"],
      p["kk_scale"], p["k_a"], p["r_k"], p["pones"])


SCAN_CHUNKS = 8


def _rwkv_scan_kernel(has_init, K, *refs):
    s0_ref = refs[0] if has_init else None
    (rpf_ref, rpb_ref, gmf_ref, gmb_ref, hmf_ref, hmb_ref,
     ysf_ref, ysb_ref, sout_ref, s_scr) = refs[1:] if has_init else refs

    @pl.when(pl.program_id(1) == 0)
    def _():
        s_scr[...] = s0_ref[0] if has_init else jnp.zeros(s_scr.shape, F32)

    C = RCHUNK
    rp_ref, gm_ref, hm_ref, ys_ref = (rpf_ref, rpb_ref), (gmf_ref, gmb_ref), (hmf_ref, hmb_ref), (ysf_ref, ysb_ref)
    chains = [(d, p) for d in range(N_DIR) for p in range(RWKV_PAIRS)]
    lanes = [slice(p * PAIR_LANES, (p + 1) * PAIR_LANES) for _, p in chains]
    nch = range(len(chains))
    s = [s_scr[d, p] for d, p in chains]
    for j in range(K):
        at = (j, K - 1 - j)
        rows = [slice(at[d] * C, (at[d] + 1) * C) for d, _ in chains]
        sb = [m.astype(BF16) for m in s]
        y = [_dg(rp_ref[chains[i][0]][0, rows[i], lanes[i]], sb[i], _NT) for i in nch]
        sg = [_dg(sb[i], _bd(gm_ref[chains[i][0]][0, at[chains[i][0]], chains[i][1]]), _NN) for i in nch]
        for i in nch:
            d, p = chains[i]
            ys_ref[d][rows[i], lanes[i]] = y[i]
        s = [sg[i] + _bd(hm_ref[chains[i][0]][0, at[chains[i][0]], chains[i][1]]) for i in nch]
    for i in nch:
        d, p = chains[i]
        s_scr[d, p] = s[i]
        n = RWKV_HEAD_DIM
        sout_ref[0, d, 2 * p] = s[i][0:n, 0:n]
        sout_ref[0, d, 2 * p + 1] = s[i][n:2 * n, n:2 * n]


def _rwkv_scan(s0, rp, gm, hm, batch, seq_len):
    K = min(SCAN_CHUNKS, seq_len // RCHUNK)
    assert seq_len % (RCHUNK * K) == 0
    spb = seq_len // (RCHUNK * K)
    n = batch * seq_len

    def fwd(b, s):
        return b * spb + s

    def bwd(b, s):
        return b * spb + spb - 1 - s

    def mat_spec(d, at):
        return pl.BlockSpec((1, K, RWKV_PAIRS, RWKV_HEAD_DIM, PAIR_LANES), lambda b, s: (d, at(b, s), 0, 0, 0))

    def tok_spec(d, at):
        return pl.BlockSpec((1, K * RCHUNK, RWKV_WIDTH), lambda b, s: (d, at(b, s), 0))

    st_spec = pl.BlockSpec((1, N_DIR, RWKV_PAIRS, PAIR_LANES, PAIR_LANES), lambda b, s: (b, 0, 0, 0, 0))
    ys = jax.ShapeDtypeStruct((n, RWKV_WIDTH), F32)
    has_init = s0 is not None
    return pl.pallas_call(
        functools.partial(_rwkv_scan_kernel, has_init, K),
        grid=(batch, spb),
        in_specs=([st_spec] if has_init else [])
        + [tok_spec(0, fwd), tok_spec(1, bwd),
           mat_spec(0, fwd), mat_spec(1, bwd), mat_spec(0, fwd), mat_spec(1, bwd)],
        out_specs=[pl.BlockSpec((K * RCHUNK, RWKV_WIDTH), lambda b, s: (fwd(b, s), 0)),
                   pl.BlockSpec((K * RCHUNK, RWKV_WIDTH), lambda b, s: (bwd(b, s), 0)),
                   pl.BlockSpec((1, N_DIR, RWKV_HEADS, RWKV_HEAD_DIM, RWKV_HEAD_DIM),
                                lambda b, s: (b, 0, 0, 0, 0))],
        out_shape=[ys, ys,
                   jax.ShapeDtypeStruct((batch, N_DIR, RWKV_HEADS, RWKV_HEAD_DIM, RWKV_HEAD_DIM), F32)],
        scratch_shapes=[pltpu.VMEM((N_DIR, RWKV_PAIRS, PAIR_LANES, PAIR_LANES), F32)],
        compiler_params=_params(("arbitrary", "arbitrary")),
        name="rwkv_scan",
    )(*([s0] if has_init else []), rp, rp, gm, gm, hm, hm)


def _dwconv(x, w_ref, width, vertical):
    T = x.shape[0]
    t = lax.broadcasted_iota(jnp.int32, (T, 1), 0)
    assert width & (width - 1) == 0
    colp = jnp.bitwise_and(t, width - 1)
    xl = jnp.where(colp == 0, 0.0, pltpu.roll(x, 1, 0))
    xr = jnp.where(colp == width - 1, 0.0, pltpu.roll(x, T - 1, 0))

    def tap_row(i):
        return w_ref[3 * i:3 * i + 1, :] * xl + w_ref[3 * i + 1:3 * i + 2, :] * x + w_ref[3 * i + 2:3 * i + 3, :] * xr

    out = tap_row(1)
    if vertical:
        out = out + jnp.where(t < width, 0.0, pltpu.roll(tap_row(0), width, 0))
        out = out + jnp.where(t >= T - width, 0.0, pltpu.roll(tap_row(2), T - width, 0))
    return out


def _store_time_on_lanes(x, out_ref, chunk):
    xt = x.T
    for c in range(x.shape[0] // chunk):
        out_ref[c] = xt[:, c * chunk:(c + 1) * chunk]


def _qk_conv_kernel(width, first_k_tile, x_ref, w_ref, o_ref, kt_ref):
    out = _silu(_dwconv(x_ref[...], w_ref, width, True))
    o_ref[...] = out

    @pl.when(pl.program_id(1) >= first_k_tile)
    def _():
        _store_time_on_lanes(out, kt_ref, MLSTM_CHUNK)


def _qk_conv(z, batch, seq_len, rows, conv_w9):
    n = batch * seq_len
    ch = 2 * MLSTM_WIDTH
    tc = CONV_CH_TILE
    off = ZM_OFF // tc
    first_k = MLSTM_WIDTH // tc
    block, width = seq_len, seq_len // rows
    cpb = block // MLSTM_CHUNK
    return pl.pallas_call(
        functools.partial(_qk_conv_kernel, width, first_k),
        grid=(n // block, ch // tc),
        in_specs=[pl.BlockSpec((block, tc), lambda b, j: (b, off + j)),
                  pl.BlockSpec((9, tc), lambda b, j: (0, j))],
        out_specs=[pl.BlockSpec((block, tc), lambda b, j: (b, j)),
                   pl.BlockSpec((cpb, tc, MLSTM_CHUNK), lambda b, j: (b, jnp.maximum(j - first_k, 0), 0))],
        out_shape=[jax.ShapeDtypeStruct((n, ch), F32),
                   jax.ShapeDtypeStruct((n // MLSTM_CHUNK, MLSTM_WIDTH, MLSTM_CHUNK), F32)],
        compiler_params=_params(("arbitrary", "arbitrary")),
        name="mlstm_qk_conv",
    )(z, conv_w9)


MLSTM_STEP_CHUNKS = 4

def _mlstm_scan_kernel(has_init, qkf_ref, qkb_ref, ktf_ref, ktb_ref, vf_ref, vb_ref, gcf_ref, gcb_ref,
                       grf_ref, grb_ref, gbc_ref, gbr_ref, *refs):
    init_refs = refs[0:3] if has_init else None
    hf_ref, hb_ref, cout_ref, nout_ref, mout_ref, c_scr, n_scr, m_scr = refs[3:] if has_init else refs
    step = pl.program_id(1)
    L = MLSTM_CHUNK
    dh = MLSTM_HEAD_DIM
    H = MLSTM_HEADS

    @pl.when(step == 0)
    def _():
        for scr, k in zip((c_scr, n_scr, m_scr), range(3)):
            scr[...] = init_refs[k][0] if has_init else jnp.zeros(scr.shape, F32)

    K = MLSTM_STEP_CHUNKS
    R = K * L
    row = lax.broadcasted_iota(jnp.int32, (L, L), 0)
    col = lax.broadcasted_iota(jnp.int32, (L, L), 1)
    lower = (row >= col)
    upper = (row <= col)
    lower_b = jnp.where(lower, 1.0, 0.0).astype(BF16)
    upper_b = jnp.where(upper, 1.0, 0.0).astype(BF16)
    rrow = lax.broadcasted_iota(jnp.int32, (R, R), 0)
    rcol = lax.broadcasted_iota(jnp.int32, (R, R), 1)
    same_chunk = jnp.bitwise_and(rrow, -L) == jnp.bitwise_and(rcol, -L)
    neg_inf = jnp.full((), -jnp.inf, F32)

    gcol, grow, bcol, brow, btot = [], [], [], [], []
    ones_b = jnp.ones((L, LANE), BF16)
    for d in range(N_DIR):
        gc_ref, gr_ref = (gcf_ref, grf_ref) if d == 0 else (gcb_ref, grb_ref)
        gcol.append(gc_ref[...] + gbc_ref[...])
        grow.append((gr_ref[...] + gbr_ref[...][None]).reshape(K * MLSTM_GATES, L))
        before = same_chunk & ((rrow >= rcol) if d == 0 else (rrow <= rcol))
        bcol.append(_mm_exact_lhs(jnp.where(before, 1.0, 0.0).astype(BF16), jax.nn.log_sigmoid(gcol[d]), _NN))
        frow = jax.nn.log_sigmoid(grow[d])
        brow.append(_mm_exact_rhs(frow, upper_b if d == 0 else lower_b, _NN))
        btot.append(_mm_exact_rhs(frow, ones_b, _NN))

    units = [(j, d, h) for j in range(K) for d in range(N_DIR) for h in range(H)]
    nun = range(len(units))
    q, k, kt, v, vb, qb = [], [], [], [], [], []
    c_row, b_col, b_last = [], [], []
    for j, d, h in units:
        at = j if d == 0 else K - 1 - j
        rows = slice(at * L, (at + 1) * L)
        st = d * H + h
        gi, gf = st, 2 * H + st
        qk_ref, kt_ref, v_ref = (qkf_ref, ktf_ref, vf_ref) if d == 0 else (qkb_ref, ktb_ref, vb_ref)
        q.append(qk_ref[rows, h * dh:(h + 1) * dh] * (dh ** -0.5))
        k.append(qk_ref[rows, MLSTM_WIDTH + h * dh:MLSTM_WIDTH + (h + 1) * dh])
        kt.append(kt_ref[at, h * dh:(h + 1) * dh, :])
        v.append(v_ref[rows, h * dh:(h + 1) * dh])
        qb.append(q[-1].astype(BF16))
        vb.append(v[-1].astype(BF16))
        b_col.append(jnp.broadcast_to(bcol[d][rows, gf:gf + 1], (L, LANE)))
        c_row.append(grow[d][at * MLSTM_GATES + gi:at * MLSTM_GATES + gi + 1, :]
                     - brow[d][at * MLSTM_GATES + gf:at * MLSTM_GATES + gf + 1, :])
        b_last.append(btot[d][at * MLSTM_GATES + gf:at * MLSTM_GATES + gf + 1, :])

    last = [L - 1 if d == 0 else 0 for _, d, _ in units]
    qk_t = [_dg(qb[i], k[i].astype(BF16), _NT) for i in nun]
    rel = [jnp.where(lower if units[i][1] == 0 else upper, c_row[i], neg_inf) for i in nun]
    mx = [jnp.broadcast_to(jnp.max(rel[i], axis=-1, keepdims=True), (L, LANE)) for i in nun]
    m_loc = [b_col[i] + mx[i] for i in nun]
    s_loc = [qk_t[i] * jnp.exp(rel[i] - mx[i][:, 0:L]) for i in nun]
    s_v = [_dg(s_loc[i].astype(BF16), vb[i], _NN) for i in nun]
    s_sum = [jnp.broadcast_to(jnp.sum(s_loc[i], axis=-1, keepdims=True), (L, LANE)) for i in nun]
    cmax = [mx[i][last[i]:last[i] + 1, :] for i in nun]
    m_w = [b_last[i] + cmax[i] for i in nun]
    wj = [jnp.exp(c_row[i] - cmax[i][:, 0:L]) for i in nun]
    kv = [_dg((kt[i] * wj[i]).astype(BF16), vb[i], _NN) for i in nun]
    w_k = [_mm(jnp.broadcast_to(wj[i], (8, L)), k[i], _NN, 3)[0:1] for i in nun]

    nst = N_DIR * H
    c_st = [c_scr[st] for st in range(nst)]
    n_st = [n_scr[st:st + 1, :] for st in range(nst)]
    m_st = [m_scr[st:st + 1, :] for st in range(nst)]
    for j in range(K):
        idx = [j * nst + st for st in range(nst)]
        q_c = [_dg(qb[i], c_st[st].astype(BF16), _NN) for st, i in enumerate(idx)]
        for st, i in enumerate(idx):
            _, d, h = units[i]
            at = j if d == 0 else K - 1 - j
            h_ref = hf_ref if d == 0 else hb_ref
            log_inter = b_col[i] + m_st[st]
            m_s = jnp.maximum(log_inter, m_loc[i])
            inter = jnp.exp(log_inter - m_s)
            local = jnp.exp(m_loc[i] - m_s)
            q_n = jnp.broadcast_to(jnp.sum(q[i] * n_st[st], axis=-1, keepdims=True), (L, LANE))
            den = inter * q_n + local * s_sum[i]
            scale = 1.0 / jnp.maximum(jnp.abs(den), jnp.exp(-m_s))
            h_ref[at * L:(at + 1) * L, h * dh:(h + 1) * dh] = (inter * scale) * q_c[st] + (local * scale) * s_v[i]
            m_new = jnp.maximum(b_last[i] + m_st[st], m_w[i])
            carry = jnp.exp(b_last[i] + m_st[st] - m_new)
            fresh = jnp.exp(m_w[i] - m_new)
            c_st[st] = carry * c_st[st] + fresh * kv[i]
            n_st[st] = carry * n_st[st] + fresh * w_k[i]
            m_st[st] = m_new


    for st in range(nst):
        c_scr[st] = c_st[st]
        n_scr[st:st + 1, :] = n_st[st]
        m_scr[st:st + 1, :] = m_st[st]
    nout_ref[0] = n_scr[...]
    mout_ref[0] = m_scr[...]

    @pl.when(step == pl.num_programs(1) - 1)
    def _():
        for st in range(nst):
            cout_ref[0, st] = c_st[st].T


def _mlstm_scan(z, qk, qk_blk, kt, gt, gate_bc, gate_br, c0, n0, m0, batch, seq_len):
    K = MLSTM_STEP_CHUNKS
    L = K * MLSTM_CHUNK
    assert seq_len % L == 0
    cps = seq_len // L
    n = batch * seq_len
    W = MLSTM_WIDTH
    nst = N_DIR * MLSTM_HEADS
    dh = MLSTM_HEAD_DIM

    def fw(b, c):
        return b * cps + c

    def bw(b, c):
        return b * cps + cps - 1 - c

    vblk = (ZM_OFF + 2 * W) // W
    gblk = ZG_OFF // LANE
    has_init = c0 is not None
    state_specs = [pl.BlockSpec((1, nst, dh, dh), lambda b, c: (b, 0, 0, 0)),
                   pl.BlockSpec((1, nst, dh), lambda b, c: (b, 0, 0)),
                   pl.BlockSpec((1, nst, LANE), lambda b, c: (b, 0, 0))]
    return pl.pallas_call(
        functools.partial(_mlstm_scan_kernel, has_init),
        grid=(batch, cps),
        in_specs=[pl.BlockSpec((L, 2 * W), lambda b, c: (fw(b, c), qk_blk)),
                  pl.BlockSpec((L, 2 * W), lambda b, c: (bw(b, c), qk_blk)),
                  pl.BlockSpec((K, W, MLSTM_CHUNK), lambda b, c: (fw(b, c), 0, 0)),
                  pl.BlockSpec((K, W, MLSTM_CHUNK), lambda b, c: (bw(b, c), 0, 0)),
                  pl.BlockSpec((L, W), lambda b, c: (fw(b, c), vblk)),
                  pl.BlockSpec((L, W), lambda b, c: (bw(b, c), vblk)),
                  pl.BlockSpec((L, LANE), lambda b, c: (fw(b, c), gblk)),
                  pl.BlockSpec((L, LANE), lambda b, c: (bw(b, c), gblk)),
                  pl.BlockSpec((K, MLSTM_GATES, MLSTM_CHUNK), lambda b, c: (fw(b, c), 0, 0)),
                  pl.BlockSpec((K, MLSTM_GATES, MLSTM_CHUNK), lambda b, c: (bw(b, c), 0, 0)),
                  _resident((1, LANE)),
                  _resident((MLSTM_GATES, 1))] + (state_specs if has_init else []),
        out_specs=[pl.BlockSpec((L, W), lambda b, c: (fw(b, c), 0)),
                   pl.BlockSpec((L, W), lambda b, c: (bw(b, c), 0))] + state_specs,
        out_shape=[jax.ShapeDtypeStruct((n, W), F32), jax.ShapeDtypeStruct((n, W), F32),
                   jax.ShapeDtypeStruct((batch, nst, dh, dh), F32),
                   jax.ShapeDtypeStruct((batch, nst, dh), F32),
                   jax.ShapeDtypeStruct((batch, nst, LANE), F32)],
        scratch_shapes=[pltpu.VMEM((nst, dh, dh), F32), pltpu.VMEM((nst, dh), F32),
                        pltpu.VMEM((nst, LANE), F32)],
        compiler_params=_params(("arbitrary", "arbitrary")),
        name="mlstm_scan",
    )(qk, qk, kt, kt, z, z, z, z, gt, gt, gate_bc, gate_br, *([c0, n0, m0] if has_init else []))


def _grid_conv(above, cur, below, cw_ref, width):
    T = cur.shape[0]
    E = T + 2 * width
    ext = jnp.concatenate([above, cur, below], axis=0)
    colp = jnp.bitwise_and(lax.broadcasted_iota(jnp.int32, (E, 1), 0), width - 1)
    left = jnp.where(colp == 0, 0.0, pltpu.roll(ext, 1, 0))
    right = jnp.where(colp == width - 1, 0.0, pltpu.roll(ext, E - 1, 0))

    def tap_row(i):
        rows = slice(i * width, i * width + T)
        return (cw_ref[3 * i:3 * i + 1, :] * left[rows] + cw_ref[3 * i + 1:3 * i + 2, :] * ext[rows]
                + cw_ref[3 * i + 2:3 * i + 3, :] * right[rows])

    return tap_row(0) + tap_row(1) + tap_row(2)


def _merge_kernel(conv, x_ref, mod_ref, ysf_ref, ysb_ref, y0_ref, bonus_ref, gate_ref, hf_ref, hb_ref, zo_ref,
                  zs_ref, lnxg_ref, lnxb_ref, gng_ref, pmean_ref, wbr_ref, wbm_ref, wout_ref, ng_ref, wup_ref,
                  cw_ref, cb_ref, *rest):
    if conv[0] == "seq":
        wdn_ref, out_ref = rest
    else:
        wdn_ref, modp_ref, out_ref = rest[0:3]
        scratch = rest[3:]

        @pl.when(pl.program_id(0) == 0)
        def _():
            for ref in scratch:
                ref[...] = jnp.zeros(ref.shape, ref.dtype)

    mod = mod_ref[0]
    g1 = mod[:, 2 * D_MODEL:3 * D_MODEL]
    sh2 = mod[:, 3 * D_MODEL:4 * D_MODEL]
    sc2 = mod[:, 4 * D_MODEL:5 * D_MODEL]

    ys = (ysf_ref[...] + ysb_ref[...]) + y0_ref[...]
    pmean = pmean_ref[...]
    mean = _mm_exact_rhs(ys, pmean, _NN, pieces=2)
    cen = ys - mean
    var = _mm_exact_rhs(cen * cen, pmean, _NN, pieces=2)
    y_r = (cen * lax.rsqrt(var + RWKV_GN_EPS) * lnxg_ref[...] + lnxb_ref[...] + bonus_ref[...]) * gate_ref[...]

    hs = hf_ref[...] + hb_ref[...]
    parts = []
    for h in range(MLSTM_HEADS):
        hh = hs[:, h * MLSTM_HEAD_DIM:(h + 1) * MLSTM_HEAD_DIM]
        mu = jnp.mean(hh, axis=-1, keepdims=True)
        ce = hh - mu
        va = jnp.mean(ce * ce, axis=-1, keepdims=True)
        parts.append(ce * lax.rsqrt(va + MLSTM_GN_EPS))
    y_m = jnp.concatenate(parts, axis=1) * gng_ref[...] * _sigmoid(zo_ref[...])

    gates = zs_ref[...]
    merged = (gates[:, 0:D_MODEL] * _dg(y_r.astype(BF16), wbr_ref[...], _NN)
              + gates[:, D_MODEL:2 * D_MODEL] * _dg(y_m.astype(BF16), wbm_ref[...], _NN))
    t = _dg(merged.astype(BF16), wout_ref[...], _NN)
    x1 = x_ref[...] + g1 * _rms(t, ng_ref[1:2, :])
    h2 = _rms(x1, ng_ref[2:3, :]) * (1.0 + sc2) + sh2
    u = _dg(h2.astype(BF16), wup_ref[...], _NN)
    if conv[0] == "seq":
        pre = _dwconv(u[:, 0:D_FF], cw_ref, conv[1], False) + cb_ref[...]
        act = (_silu(pre) * u[:, D_FF:2 * D_FF]).astype(BF16)
        g2 = mod[:, 5 * D_MODEL:6 * D_MODEL]
        out_ref[...] = x1 + g2 * _rms(_dg(act, wdn_ref[...], _NN), ng_ref[3:4, :])
    else:
        _, width, tiles_per_image = conv
        act_scr, val_scr, tail_scr, x1_scr, act_ref = scratch
        T = act_scr.shape[0]
        step = pl.program_id(0)
        pos = (step + tiles_per_image - 1) % tiles_per_image
        for c0 in range(0, D_FF, CONV_CH_TILE):
            ch = slice(c0, c0 + CONV_CH_TILE)
            above = jnp.where(pos != 0, tail_scr[:, ch], 0.0)
            below = jnp.where(pos != tiles_per_image - 1, u[0:width, ch], 0.0)
            pre = _grid_conv(above, act_scr[:, ch], below, cw_ref.at[:, ch], width) + cb_ref[:, ch]
            act_ref[:, ch] = (_silu(pre) * val_scr[:, ch]).astype(BF16)
        g2 = modp_ref[0][:, 5 * D_MODEL:6 * D_MODEL]
        out_ref[...] = x1_scr[...] + g2 * _rms(_dg(act_ref[...], wdn_ref[...], _NN), ng_ref[3:4, :])
        tail_scr[...] = act_scr[T - width:T, :]
        act_scr[...] = u[:, 0:D_FF]
        val_scr[...] = u[:, D_FF:2 * D_FF]
        x1_scr[...] = x1


def _merge(x2, mod, mod_row, z, ysf, ysb, y0, bonus, gate, hf, hb, p, conv):
    n = x2.shape[0]
    W = RWKV_WIDTH
    rows = MERGE_TILE
    ntiles = n // rows
    delayed = conv[0] == "grid"
    cur = (lambda i: jnp.minimum(i, ntiles - 1)) if delayed else (lambda i: i)
    tile = lambda w: pl.BlockSpec((rows, w), lambda i: (cur(i), 0))
    prev = lambda i: jnp.maximum(i - 1, 0)
    extra_in, extra_args = [_resident((D_FF, D_MODEL))], [p["ffn_down"]]
    if delayed:
        scratch = [pltpu.VMEM((rows, D_FF), F32), pltpu.VMEM((rows, D_FF), F32), pltpu.VMEM((conv[1], D_FF), F32),
                   pltpu.VMEM((rows, D_MODEL), F32), pltpu.VMEM((rows, D_FF), BF16)]
        extra_in.append(pl.BlockSpec((1, 1, 6 * D_MODEL), lambda i: (mod_row(prev(i) * rows), 0, 0)))
        extra_args.append(mod)
        out_spec = pl.BlockSpec((rows, D_MODEL), lambda i: (prev(i), 0))
    else:
        scratch = []
        out_spec = tile(D_MODEL)
    out_specs = [out_spec]
    out_shape = [jax.ShapeDtypeStruct((n, D_MODEL), F32)]
    return pl.pallas_call(
        functools.partial(_merge_kernel, conv),
        grid=(ntiles + 1 if delayed else ntiles,),
        in_specs=[tile(D_MODEL),
                  pl.BlockSpec((1, 1, 6 * D_MODEL), lambda i: (mod_row(cur(i) * rows), 0, 0)),
                  tile(W), tile(W), tile(W), tile(W), tile(W), tile(MLSTM_WIDTH), tile(MLSTM_WIDTH),
                  pl.BlockSpec((rows, MLSTM_WIDTH),
                               lambda i: (cur(i), (ZM_OFF + 3 * MLSTM_WIDTH) // MLSTM_WIDTH)),
                  pl.BlockSpec((rows, GATE_COLS), lambda i: (cur(i), ZS_OFF // GATE_COLS)),
                  _resident((1, W)), _resident((1, W)), _resident((1, MLSTM_WIDTH)),
                  _resident((W, W)),
                  _resident((W, D_MODEL)), _resident((MLSTM_WIDTH, D_MODEL)),
                  _resident((D_MODEL, D_MODEL)), _resident((4, D_MODEL)),
                  _resident((D_MODEL, 2 * D_FF)), _resident((9, D_FF)), _resident((1, D_FF))] + extra_in,
        out_specs=out_specs,
        out_shape=out_shape,
        scratch_shapes=scratch,
        compiler_params=_params(("arbitrary",)),
        name="merge_ffn_up",
    )(x2, mod, ysf, ysb, y0, bonus, gate, hf, hb, z, z, p["lnx_g"], p["lnx_b"], p["gn_g"], p["pmean"],
      p["w_br"], p["w_bm"], p["w_out"], p["norm_g"], p["ffn_up"], p["ffn_conv"], p["ffn_conv_b"], *extra_args)


RWKV_LOCAL_PASSES = 1


def _state_to_pairs(s):
    b = s.shape[0]
    s = s.reshape(b, N_DIR, RWKV_PAIRS, 2, RWKV_HEAD_DIM, RWKV_HEAD_DIM)
    zero = jnp.zeros_like(s[:, :, :, 0])
    top = jnp.concatenate([s[:, :, :, 0], zero], axis=-1)
    bot = jnp.concatenate([zero, s[:, :, :, 1]], axis=-1)
    return jnp.concatenate([top, bot], axis=-2)


def _trunk(x, mod, mod_row, rows, states, p):
    batch, seq_len, _ = x.shape
    n = batch * seq_len
    x2 = x.reshape(n, D_MODEL)
    nst = N_DIR * MLSTM_HEADS
    if states is None:
        s0 = c0 = n0 = m0 = None
    else:
        s0, c0, n0, m0 = states
        s0 = _state_to_pairs(s0)
        c0 = jnp.swapaxes(c0, -1, -2).reshape(batch, nst, MLSTM_HEAD_DIM, MLSTM_HEAD_DIM)
        n0 = n0.reshape(batch, nst, MLSTM_HEAD_DIM)
        m0 = jnp.broadcast_to(m0.reshape(batch, nst, 1), (batch, nst, LANE))

    fuse_width = seq_len if (rows == 1 and IN_TILE % seq_len == 0 and MERGE_TILE % seq_len == 0) else None
    proj = _in_proj(x2, mod, mod_row, p["norm_g"][0:1], p["w_in"], p["mlstm_conv"], fuse_width)
    z = proj[0]

    rp, y0, gm, hm, gate, bonus = _rwkv_local(z, seq_len, p, RWKV_LOCAL_PASSES)
    ysf, ysb, s_fin = _rwkv_scan(s0, rp, gm, hm, batch, seq_len)

    if fuse_width is None:
        z, gt = proj
        qk, kt = _qk_conv(z, batch, seq_len, rows, p["mlstm_conv"])
        qk_blk = 0
    else:
        z, gt, kt = proj
        qk, qk_blk = z, ZM_OFF // (2 * MLSTM_WIDTH)
    hf, hb, c_fin, n_fin, m_fin = _mlstm_scan(z, qk, qk_blk, kt, gt, p["gate_bc"], p["gate_br"],
                                              c0, n0, m0, batch, seq_len)

    if fuse_width is not None:
        ffn_conv = ("seq", fuse_width)
    else:
        width = seq_len // rows
        assert rows > 1 and width & (width - 1) == 0 and MERGE_TILE % width == 0 and seq_len % MERGE_TILE == 0
        ffn_conv = ("grid", width, seq_len // MERGE_TILE)
    out, = _merge(x2, mod, mod_row, z, ysf, ysb, y0, bonus, gate, hf, hb, p, ffn_conv)

    new_states = (s_fin,
                  c_fin.reshape(batch, N_DIR, MLSTM_HEADS, MLSTM_HEAD_DIM, MLSTM_HEAD_DIM),
                  n_fin.reshape(batch, N_DIR, MLSTM_HEADS, MLSTM_HEAD_DIM),
                  m_fin[:, :, 0].reshape(batch, N_DIR, MLSTM_HEADS))
    return out.reshape(batch, seq_len, D_MODEL), new_states


def _pack_layer(l, ada_w, ada_b, norm_g, w_in, rwkv_mu, rwkv_w0, rwkv_w_up, rwkv_a0, rwkv_a_up,
                rwkv_g_up, rwkv_kk_scale, rwkv_k_a, rwkv_r_k, rwkv_lnx_g, rwkv_lnx_b, mlstm_conv,
                mlstm_gate_b, mlstm_gn_g, w_branch_rwkv, w_branch_mlstm, w_out, ffn_up, ffn_conv,
                ffn_conv_b, ffn_down):
    W = RWKV_WIDTH
    w_in_b = w_in[l].astype(BF16)

    head = jnp.arange(W, dtype=jnp.int32) // RWKV_HEAD_DIM
    same = (head[:, None] == head[None, :])
    gb = mlstm_gate_b[l].reshape(1, MLSTM_GATES)
    return dict(
        ada_w=ada_w[l], ada_b=ada_b[l], norm_g=norm_g[l], w_in=w_in_b,
        mu=rwkv_mu[l].reshape(1, RWKV_COLS),
        w0=rwkv_w0[l].reshape(N_DIR, 1, W), w_up=rwkv_w_up[l],
        a0=rwkv_a0[l].reshape(N_DIR, 1, W), a_up=rwkv_a_up[l], g_up=rwkv_g_up[l],
        kk_scale=rwkv_kk_scale[l].reshape(1, W), k_a=rwkv_k_a[l].reshape(1, W),
        r_k=rwkv_r_k[l].reshape(1, W),
        lnx_g=rwkv_lnx_g[l].reshape(1, W), lnx_b=rwkv_lnx_b[l].reshape(1, W),
        pones=same.astype(BF16), pmean=(same.astype(F32) / RWKV_HEAD_DIM).astype(BF16),
        mlstm_conv=mlstm_conv[l].reshape(9, 2 * MLSTM_WIDTH),
        gate_bc=jnp.pad(gb, ((0, 0), (0, LANE - MLSTM_GATES))), gate_br=gb.reshape(MLSTM_GATES, 1),
        gn_g=mlstm_gn_g[l].reshape(1, MLSTM_WIDTH),
        w_br=w_branch_rwkv[l].astype(BF16), w_bm=w_branch_mlstm[l].astype(BF16),
        w_out=w_out[l].astype(BF16), ffn_up=ffn_up[l].astype(BF16),
        ffn_conv=ffn_conv[l].reshape(9, D_FF), ffn_conv_b=ffn_conv_b[l].reshape(1, D_FF),
        ffn_down=ffn_down[l].astype(BF16),
    )


def kernel(x_prompt, x_sample, c, state_rwkv, state_mlstm_C, state_mlstm_n, state_mlstm_m, c_ctx,
           ada_w, ada_b, norm_g, w_in, rwkv_mu, rwkv_w0, rwkv_w_up, rwkv_a0, rwkv_a_up, rwkv_g_up,
           rwkv_kk_scale, rwkv_k_a, rwkv_r_k, rwkv_lnx_g, rwkv_lnx_b, mlstm_conv, mlstm_gate_b,
           mlstm_gn_g, w_branch_rwkv, w_branch_mlstm, w_out, ffn_up, ffn_conv, ffn_conv_b, ffn_down):
    depth = ada_w.shape[0]
    batch = x_prompt.shape[0]
    dec_batch, dec_seq, _ = x_sample.shape
    latent_rows = dec_seq // GRID_W
    cond = jnp.concatenate([c_ctx[None, :], c, jnp.zeros((8 - 1 - dec_batch, D_MODEL), F32)], axis=0)

    xp, xs = x_prompt, x_sample
    new_s, new_c, new_n, new_m = [], [], [], []
    for l in range(depth):
        p = _pack_layer(l, ada_w, ada_b, norm_g, w_in, rwkv_mu, rwkv_w0, rwkv_w_up, rwkv_a0, rwkv_a_up,
                        rwkv_g_up, rwkv_kk_scale, rwkv_k_a, rwkv_r_k, rwkv_lnx_g, rwkv_lnx_b, mlstm_conv,
                        mlstm_gate_b, mlstm_gn_g, w_branch_rwkv, w_branch_mlstm, w_out, ffn_up, ffn_conv,
                        ffn_conv_b, ffn_down)
        mod = _ada(cond, p["ada_w"], p["ada_b"]).reshape(8, 1, 6 * D_MODEL)
        xp, (s, cc, nn, mm) = _trunk(xp, mod, lambda r: 0, 1, None, p)
        new_s.append(s)
        new_c.append(cc)
        new_n.append(nn)
        new_m.append(mm)
        xs, _ = _trunk(xs, mod, lambda r: 1 + r // dec_seq, latent_rows,
                       (state_rwkv[:, l], state_mlstm_C[:, l], state_mlstm_n[:, l], state_mlstm_m[:, l]), p)
    return (xp, xs, jnp.stack(new_s, axis=1), jnp.stack(new_c, axis=1),
            jnp.stack(new_n, axis=1), jnp.stack(new_m, axis=1))
```
